```python
import jax, jax.numpy as jnp
from jax import lax
import numpy as np

D_MODEL = 1024
BATCH = 16
SEQ = 4096
DEPTH = 4

HEAD_DIM = 64
N_Q_A = 16
N_KV_A = 2
GROUP_A = N_Q_A // N_KV_A
WINDOW = 128
BLOCK = 128
N_H_B = 16
ROT_DIM = HEAD_DIM // 4
ROPE_THETA = 500000.0
D_FF = -(-8 * D_MODEL // (3 * 256)) * 256
N_MIXERS = 2
N_A = (DEPTH + 1) // 2
N_B = DEPTH // 2
QKV_A = (N_Q_A + 2 * N_KV_A) * HEAD_DIM
QKV_B = 3 * N_H_B * HEAD_DIM
EPS = 1e-6

kernel_name = 'hybrid_swa_sink_stickbreak_block'


def rmsnorm(x, gain):
    xf = x.astype(jnp.float32)
    y = xf * lax.rsqrt(jnp.mean(xf * xf, axis=-1, keepdims=True) + EPS)
    return (y * gain.astype(jnp.float32)).astype(x.dtype)


def partial_rope(x, positions):
    half = ROT_DIM // 2
    inv_freq = jnp.power(jnp.float32(ROPE_THETA), -jnp.arange(half, dtype=jnp.float32) * 2.0 / ROT_DIM)
    ang = positions.astype(jnp.float32)[:, :, None, None] * inv_freq
    cos, sin = jnp.cos(ang), jnp.sin(ang)
    xr = x[..., :ROT_DIM].astype(jnp.float32)
    x1, x2 = xr[..., :half], xr[..., half:]
    rot = jnp.concatenate([x1 * cos - x2 * sin, x2 * cos + x1 * sin], axis=-1).astype(x.dtype)
    return jnp.concatenate([rot, x[..., ROT_DIM:]], axis=-1)


def sliding_window_sink_attention(h, positions, w_qkv, q_gain, k_gain, sinks, w_o):
    B, S, _ = h.shape
    qkv = h @ w_qkv
    q, k, v = jnp.split(qkv, [N_Q_A * HEAD_DIM, (N_Q_A + N_KV_A) * HEAD_DIM], axis=-1)
    q = q.reshape(B, S, N_Q_A, HEAD_DIM)
    k = k.reshape(B, S, N_KV_A, HEAD_DIM)
    v = v.reshape(B, S, N_KV_A, HEAD_DIM)
    q = partial_rope(rmsnorm(q, q_gain), positions)
    k = partial_rope(rmsnorm(k, k_gain), positions)
    q = q.reshape(B, S, N_KV_A, GROUP_A, HEAD_DIM)
    pad = jnp.zeros((B, BLOCK, N_KV_A, HEAD_DIM), k.dtype)
    kp = jnp.concatenate([pad, k], axis=1)
    vp = jnp.concatenate([pad, v], axis=1)
    scale = HEAD_DIM ** -0.5
    q_idx = jnp.arange(BLOCK)[:, None] + BLOCK
    k_idx = jnp.arange(2 * BLOCK)[None, :]
    rel = q_idx - k_idx
    band = (rel >= 0) & (rel < WINDOW)
    sink_logit = sinks.astype(jnp.float32).reshape(1, N_KV_A, GROUP_A, 1, 1)

    def block_fn(i):
        start = i * BLOCK
        qb = lax.dynamic_slice_in_dim(q, start, BLOCK, axis=1)
        kb = lax.dynamic_slice_in_dim(kp, start, 2 * BLOCK, axis=1)
        vb = lax.dynamic_slice_in_dim(vp, start, 2 * BLOCK, axis=1)
        s = jnp.einsum('bqkgd,bskd->bkgqs', qb, kb).astype(jnp.float32) * scale
        valid = band & (start - BLOCK + k_idx >= 0)
        s = jnp.where(valid, s, -jnp.inf)
        sink_col = jnp.broadcast_to(sink_logit, s.shape[:-1] + (1,))
        p = jax.nn.softmax(jnp.concatenate([s, sink_col], axis=-1), axis=-1)[..., :-1]
        o = jnp.einsum('bkgqs,bskd->bqkgd', p.astype(vb.dtype), vb)
        return o.reshape(B, BLOCK, N_Q_A * HEAD_DIM)

    out = lax.map(block_fn, jnp.arange(S // BLOCK))
    out = jnp.moveaxis(out, 0, 1).reshape(B, S, N_Q_A * HEAD_DIM)
    return out @ w_o


def stick_breaking_attention(h, w_qkv, w_o):
    B, S, _ = h.shape
    qkv = h @ w_qkv
    q, k, v = jnp.split(qkv, 3, axis=-1)
    q = q.reshape(B, S, N_H_B, HEAD_DIM)
    k = k.reshape(B, S, N_H_B, HEAD_DIM)
    v = v.reshape(B, S, N_H_B, HEAD_DIM)
    scale = HEAD_DIM ** -0.5
    k_idx = jnp.arange(S)[None, :]

    def block_fn(i):
        start = i * BLOCK
        qb = lax.dynamic_slice_in_dim(q, start, BLOCK, axis=1)
        z = jnp.einsum('bqhd,bshd->bhqs', qb, k).astype(jnp.float32) * scale
        t_idx = start + jnp.arange(BLOCK)[:, None]
        strict = k_idx < t_idx
        log_beta = jax.nn.log_sigmoid(z)
        log_one_minus = jnp.where(strict, jax.nn.log_sigmoid(-z), 0.0)
        rc = lax.cumsum(log_one_minus, axis=3, reverse=True)
        after = jnp.pad(rc[..., 1:], ((0, 0), (0, 0), (0, 0), (0, 1)))
        a = jnp.where(strict, jnp.exp(log_beta + after), 0.0)
        o = jnp.einsum('bhqs,bshd->bqhd', a.astype(v.dtype), v)
        return o.reshape(B, BLOCK, N_H_B * HEAD_DIM)

    out = lax.map(block_fn, jnp.arange(S // BLOCK))
    out = jnp.moveaxis(out, 0, 1).reshape(B, S, N_H_B * HEAD_DIM)
    return out @ w_o


def swiglu(h, w_gate, w_up, w_down):
    return (jax.nn.silu(h @ w_gate) * (h @ w_up)) @ w_down


def _fwd_setup_inputs(seed: int = 0) -> dict:
    key = jax.random.key(seed)
    ks = jax.random.split(key, 20)
    f32 = jnp.float32
    nrm = lambda k, shape, s: jax.random.normal(k, shape, f32) * s
    x = jax.random.normal(ks[0], (BATCH, SEQ, D_MODEL), f32)
    c = jax.random.normal(ks[1], (BATCH, D_MODEL), f32)
    offset = jax.random.randint(ks[2], (BATCH, 1), 0, 4096, dtype=jnp.int32)
    positions = offset + jnp.arange(SEQ, dtype=jnp.int32)[None, :]
    return {
        'x': x,
        'c': c,
        'positions': positions,
        'ada_w': nrm(ks[3], (DEPTH, D_MODEL, 6 * D_MODEL), 0.5 * D_MODEL ** -0.5),
        'ada_b': nrm(ks[4], (DEPTH, 6 * D_MODEL), 0.01),
        'norm1_g': 1.0 + nrm(ks[5], (DEPTH, D_MODEL), 0.05),
        'norm2_g': 1.0 + nrm(ks[6], (DEPTH, D_MODEL), 0.05),
        'wqkv_a': nrm(ks[7], (N_A, D_MODEL, QKV_A), D_MODEL ** -0.5),
        'q_norm_a': 1.0 + nrm(ks[8], (N_A, HEAD_DIM), 0.05),
        'k_norm_a': 1.0 + nrm(ks[9], (N_A, HEAD_DIM), 0.05),
        'sinks_a': nrm(ks[10], (N_A, N_Q_A), 1.0),
        'wo_a': nrm(ks[11], (N_A, N_Q_A * HEAD_DIM, D_MODEL), (N_Q_A * HEAD_DIM) ** -0.5),
        'wqkv_b': nrm(ks[12], (N_B, D_MODEL, QKV_B), D_MODEL ** -0.5),
        'wo_b': nrm(ks[13], (N_B, N_H_B * HEAD_DIM, D_MODEL), (N_H_B * HEAD_DIM) ** -0.5),
        'w_gate': nrm(ks[14], (DEPTH, D_MODEL, D_FF), D_MODEL ** -0.5),
        'w_up': nrm(ks[15], (DEPTH, D_MODEL, D_FF), D_MODEL ** -0.5),
        'w_down': nrm(ks[16], (DEPTH, D_FF, D_MODEL), D_FF ** -0.5),
    }


def _fwd_reference(x, c, positions, ada_w, ada_b, norm1_g, norm2_g, wqkv_a, q_norm_a, k_norm_a,
              sinks_a, wo_a, wqkv_b, wo_b, w_gate, w_up, w_down):
    cond = jax.nn.silu(c)
    for i in range(DEPTH):
        mod = (cond @ ada_w[i] + ada_b[i])[:, None, :]
        sh1, sc1, g1, sh2, sc2, g2 = jnp.split(mod, 6, axis=-1)
        h = rmsnorm(x, norm1_g[i]) * (1.0 + sc1) + sh1
        j = i // N_MIXERS
        if i % N_MIXERS == 0:
            y = sliding_window_sink_attention(h, positions, wqkv_a[j], q_norm_a[j], k_norm_a[j],
                                              sinks_a[j], wo_a[j])
        else:
            y = stick_breaking_attention(h, wqkv_b[j], wo_b[j])
        x = x + g1 * y
        h = rmsnorm(x, norm2_g[i]) * (1.0 + sc2) + sh2
        x = x + g2 * swiglu(h, w_gate[i], w_up[i], w_down[i])
    return x


import jax as _jax
import jax.numpy as _jnp

TWIN_FORMAT = 'train_step'
FWD_PARAMS = ['x', 'c', 'positions', 'ada_w', 'ada_b', 'norm1_g', 'norm2_g', 'wqkv_a', 'q_norm_a', 'k_norm_a', 'sinks_a', 'wo_a', 'wqkv_b', 'wo_b', 'w_gate', 'w_up', 'w_down']
TWIN_WEIGHTS = ['ada_w', 'ada_b', 'norm1_g', 'norm2_g', 'wqkv_a', 'q_norm_a', 'k_norm_a', 'sinks_a', 'wo_a', 'wqkv_b', 'wo_b', 'w_gate', 'w_up', 'w_down']
TWIN_DIFF_INPUT = 'x'
TWIN_INPUTS = ['x', 'c', 'positions', 'ada_w', 'ada_b', 'norm1_g', 'norm2_g', 'wqkv_a', 'q_norm_a', 'k_norm_a', 'sinks_a', 'wo_a', 'wqkv_b', 'wo_b', 'w_gate', 'w_up', 'w_down', 'loss_target', 'm_ada_w', 'm_ada_b', 'm_norm1_g', 'm_norm2_g', 'm_wqkv_a', 'm_q_norm_a', 'm_k_norm_a', 'm_sinks_a', 'm_wo_a', 'm_wqkv_b', 'm_wo_b', 'm_w_gate', 'm_w_up', 'm_w_down', 'v_ada_w', 'v_ada_b', 'v_norm1_g', 'v_norm2_g', 'v_wqkv_a', 'v_q_norm_a', 'v_k_norm_a', 'v_sinks_a', 'v_wo_a', 'v_wqkv_b', 'v_wo_b', 'v_w_gate', 'v_w_up', 'v_w_down']
TWIN_OUTPUTS = ['loss', 'grad_x', 'grad_ada_w', 'grad_ada_b', 'grad_norm1_g', 'grad_norm2_g', 'grad_wqkv_a', 'grad_q_norm_a', 'grad_k_norm_a', 'grad_sinks_a', 'grad_wo_a', 'grad_wqkv_b', 'grad_wo_b', 'grad_w_gate', 'grad_w_up', 'grad_w_down', 'delta_ada_w', 'delta_ada_b', 'delta_norm1_g', 'delta_norm2_g', 'delta_wqkv_a', 'delta_q_norm_a', 'delta_k_norm_a', 'delta_sinks_a', 'delta_wo_a', 'delta_wqkv_b', 'delta_wo_b', 'delta_w_gate', 'delta_w_up', 'delta_w_down', 'new_m_ada_w', 'new_m_ada_b', 'new_m_norm1_g', 'new_m_norm2_g', 'new_m_wqkv_a', 'new_m_q_norm_a', 'new_m_k_norm_a', 'new_m_sinks_a', 'new_m_wo_a', 'new_m_wqkv_b', 'new_m_wo_b', 'new_m_w_gate', 'new_m_w_up', 'new_m_w_down', 'new_v_ada_w', 'new_v_ada_b', 'new_v_norm1_g', 'new_v_norm2_g', 'new_v_wqkv_a', 'new_v_q_norm_a', 'new_v_k_norm_a', 'new_v_sinks_a', 'new_v_wo_a', 'new_v_wqkv_b', 'new_v_wo_b', 'new_v_w_gate', 'new_v_w_up', 'new_v_w_down']
TWIN_LEAF_KINDS = {'loss': 'loss', 'grad_x': 'grad_x', 'grad_ada_w': 'grad_w', 'grad_ada_b': 'grad_w', 'grad_norm1_g': 'grad_w', 'grad_norm2_g': 'grad_w', 'grad_wqkv_a': 'grad_w', 'grad_q_norm_a': 'grad_w', 'grad_k_norm_a': 'grad_w', 'grad_sinks_a': 'grad_w', 'grad_wo_a': 'grad_w', 'grad_wqkv_b': 'grad_w', 'grad_wo_b': 'grad_w', 'grad_w_gate': 'grad_w', 'grad_w_up': 'grad_w', 'grad_w_down': 'grad_w', 'delta_ada_w': 'delta_w', 'delta_ada_b': 'delta_w', 'delta_norm1_g': 'delta_w', 'delta_norm2_g': 'delta_w', 'delta_wqkv_a': 'delta_w', 'delta_q_norm_a': 'delta_w', 'delta_k_norm_a': 'delta_w', 'delta_sinks_a': 'delta_w', 'delta_wo_a': 'delta_w', 'delta_wqkv_b': 'delta_w', 'delta_wo_b': 'delta_w', 'delta_w_gate': 'delta_w', 'delta_w_up': 'delta_w', 'delta_w_down': 'delta_w', 'new_m_ada_w': 'new_m', 'new_m_ada_b': 'new_m', 'new_m_norm1_g': 'new_m', 'new_m_norm2_g': 'new_m', 'new_m_wqkv_a': 'new_m', 'new_m_q_norm_a': 'new_m', 'new_m_k_norm_a': 'new_m', 'new_m_sinks_a': 'new_m', 'new_m_wo_a': 'new_m', 'new_m_wqkv_b': 'new_m', 'new_m_wo_b': 'new_m', 'new_m_w_gate': 'new_m', 'new_m_w_up': 'new_m', 'new_m_w_down': 'new_m', 'new_v_ada_w': 'new_v', 'new_v_ada_b': 'new_v', 'new_v_norm1_g': 'new_v', 'new_v_norm2_g': 'new_v', 'new_v_wqkv_a': 'new_v', 'new_v_q_norm_a': 'new_v', 'new_v_k_norm_a': 'new_v', 'new_v_sinks_a': 'new_v', 'new_v_wo_a': 'new_v', 'new_v_wqkv_b': 'new_v', 'new_v_wo_b': 'new_v', 'new_v_w_gate': 'new_v', 'new_v_w_up': 'new_v', 'new_v_w_down': 'new_v'}


def _forward(args):
    return _fwd_reference(*[args[k] for k in FWD_PARAMS])


def _output_shape():
    out = _jax.eval_shape(lambda: _forward(_fwd_setup_inputs(0)))
    return out.shape, out.dtype

N_MICROBATCH = 1
ADAM_LR = 0.001
ADAM_B1 = 0.9
ADAM_B2 = 0.999
ADAM_EPS = 1e-08
ADAM_WD = 0.01
ADAM_STEP = 10
PER_EXAMPLE_BATCH_AXIS = {'x': 0, 'c': 0, 'positions': 0, 'loss_target': 0}
SHARED_INPUTS = []
_WEIGHT_DTYPES = {'ada_w': _jnp.float32, 'ada_b': _jnp.float32, 'norm1_g': _jnp.float32, 'norm2_g': _jnp.float32, 'wqkv_a': _jnp.float32, 'q_norm_a': _jnp.float32, 'k_norm_a': _jnp.float32, 'sinks_a': _jnp.float32, 'wo_a': _jnp.float32, 'wqkv_b': _jnp.float32, 'wo_b': _jnp.float32, 'w_gate': _jnp.float32, 'w_up': _jnp.float32, 'w_down': _jnp.float32}
MOMENT_SCALE = {'ada_w': 2.220434e+00, 'ada_b': 4.488866e+00, 'norm1_g': 2.783652e+00, 'norm2_g': 6.825950e+00, 'wqkv_a': 1.098551e+00, 'q_norm_a': 1.020198e+00, 'k_norm_a': 1.021547e+00, 'sinks_a': 2.313945e-01, 'wo_a': 8.946244e-01, 'wqkv_b': 4.508990e-01, 'wo_b': 7.254651e-01, 'w_gate': 1.281927e-01, 'w_up': 1.175110e-01, 'w_down': 1.833380e-01}


def _to_microbatches(a, axis):
    t = _jnp.moveaxis(a, axis, 0)
    t = t.reshape((N_MICROBATCH, t.shape[0] // N_MICROBATCH) + t.shape[1:])
    return _jnp.moveaxis(t, 1, axis + 1)


def setup_inputs(seed: int = 0) -> dict:
    inp = _fwd_setup_inputs(seed)
    key = _jax.random.fold_in(_jax.random.key(seed), 7919)
    shape, _ = _output_shape()
    out = dict(inp)
    out["loss_target"] = _jax.random.normal(_jax.random.fold_in(key, 0), shape, _jnp.float32)
    for i, name in enumerate(TWIN_WEIGHTS):
        w = inp[name].astype(_jnp.float32)
        if MOMENT_SCALE is None:
            s = _jnp.sqrt(_jnp.mean(_jnp.square(w)) + 1e-30)
        else:
            s = MOMENT_SCALE[name]
        km, kv = _jax.random.split(_jax.random.fold_in(key, i + 1))
        out[name] = w
        out["m_" + name] = s * _jax.random.normal(km, w.shape, _jnp.float32)
        out["v_" + name] = (s * s) * _jax.random.uniform(kv, w.shape, _jnp.float32, 0.5, 1.5)
    if N_MICROBATCH > 1:
        for name, axis in PER_EXAMPLE_BATCH_AXIS.items():
            out[name] = _to_microbatches(out[name], axis)
    return {'x': out['x'], 'c': out['c'], 'positions': out['positions'], 'ada_w': out['ada_w'], 'ada_b': out['ada_b'], 'norm1_g': out['norm1_g'], 'norm2_g': out['norm2_g'], 'wqkv_a': out['wqkv_a'], 'q_norm_a': out['q_norm_a'], 'k_norm_a': out['k_norm_a'], 'sinks_a': out['sinks_a'], 'wo_a': out['wo_a'], 'wqkv_b': out['wqkv_b'], 'wo_b': out['wo_b'], 'w_gate': out['w_gate'], 'w_up': out['w_up'], 'w_down': out['w_down'], 'loss_target': out['loss_target'], 'm_ada_w': out['m_ada_w'], 'm_ada_b': out['m_ada_b'], 'm_norm1_g': out['m_norm1_g'], 'm_norm2_g': out['m_norm2_g'], 'm_wqkv_a': out['m_wqkv_a'], 'm_q_norm_a': out['m_q_norm_a'], 'm_k_norm_a': out['m_k_norm_a'], 'm_sinks_a': out['m_sinks_a'], 'm_wo_a': out['m_wo_a'], 'm_wqkv_b': out['m_wqkv_b'], 'm_wo_b': out['m_wo_b'], 'm_w_gate': out['m_w_gate'], 'm_w_up': out['m_w_up'], 'm_w_down': out['m_w_down'], 'v_ada_w': out['v_ada_w'], 'v_ada_b': out['v_ada_b'], 'v_norm1_g': out['v_norm1_g'], 'v_norm2_g': out['v_norm2_g'], 'v_wqkv_a': out['v_wqkv_a'], 'v_q_norm_a': out['v_q_norm_a'], 'v_k_norm_a': out['v_k_norm_a'], 'v_sinks_a': out['v_sinks_a'], 'v_wo_a': out['v_wo_a'], 'v_wqkv_b': out['v_wqkv_b'], 'v_wo_b': out['v_wo_b'], 'v_w_gate': out['v_w_gate'], 'v_w_up': out['v_w_up'], 'v_w_down': out['v_w_down']}


def _loss(weights, diff, rest, loss_target):
    with _jax.named_scope("forward"):
        args = {**rest, TWIN_DIFF_INPUT: diff, **{k: w.astype(_WEIGHT_DTYPES[k]) for k, w in weights.items()}}
        y = _forward(args)
    with _jax.named_scope("loss_head"):
        err = _jnp.square(y.astype(_jnp.float32) - loss_target)
        return 0.5 * _jnp.sum(_jnp.mean(err, axis=-1)) if err.ndim else 0.5 * err


def _adamw(w, g, m, v):
    m = ADAM_B1 * m + (1.0 - ADAM_B1) * g
    v = ADAM_B2 * v + (1.0 - ADAM_B2) * _jnp.square(g)
    m_hat = m / (1.0 - ADAM_B1 ** ADAM_STEP)
    v_hat = v / (1.0 - ADAM_B2 ** ADAM_STEP)
    delta = -ADAM_LR * (m_hat / (_jnp.sqrt(v_hat) + ADAM_EPS) + ADAM_WD * w)
    return delta, m, v


def reference(x, c, positions, ada_w, ada_b, norm1_g, norm2_g, wqkv_a, q_norm_a, k_norm_a, sinks_a, wo_a, wqkv_b, wo_b, w_gate, w_up, w_down, loss_target, m_ada_w, m_ada_b, m_norm1_g, m_norm2_g, m_wqkv_a, m_q_norm_a, m_k_norm_a, m_sinks_a, m_wo_a, m_wqkv_b, m_wo_b, m_w_gate, m_w_up, m_w_down, v_ada_w, v_ada_b, v_norm1_g, v_norm2_g, v_wqkv_a, v_q_norm_a, v_k_norm_a, v_sinks_a, v_wo_a, v_wqkv_b, v_wo_b, v_w_gate, v_w_up, v_w_down):
    given = dict(x=x, c=c, positions=positions, ada_w=ada_w, ada_b=ada_b, norm1_g=norm1_g, norm2_g=norm2_g, wqkv_a=wqkv_a, q_norm_a=q_norm_a, k_norm_a=k_norm_a, sinks_a=sinks_a, wo_a=wo_a, wqkv_b=wqkv_b, wo_b=wo_b, w_gate=w_gate, w_up=w_up, w_down=w_down, loss_target=loss_target, m_ada_w=m_ada_w, m_ada_b=m_ada_b, m_norm1_g=m_norm1_g, m_norm2_g=m_norm2_g, m_wqkv_a=m_wqkv_a, m_q_norm_a=m_q_norm_a, m_k_norm_a=m_k_norm_a, m_sinks_a=m_sinks_a, m_wo_a=m_wo_a, m_wqkv_b=m_wqkv_b, m_wo_b=m_wo_b, m_w_gate=m_w_gate, m_w_up=m_w_up, m_w_down=m_w_down, v_ada_w=v_ada_w, v_ada_b=v_ada_b, v_norm1_g=v_norm1_g, v_norm2_g=v_norm2_g, v_wqkv_a=v_wqkv_a, v_q_norm_a=v_q_norm_a, v_k_norm_a=v_k_norm_a, v_sinks_a=v_sinks_a, v_wo_a=v_wo_a, v_wqkv_b=v_wqkv_b, v_wo_b=v_wo_b, v_w_gate=v_w_gate, v_w_up=v_w_up, v_w_down=v_w_down)
    weights = {n: given[n] for n in TWIN_WEIGHTS}
    shared = {n: given[n] for n in SHARED_INPUTS}
    per_example = {n: given[n] for n in ['x', 'c', 'positions']}
    grad_fn = _jax.value_and_grad(_loss, argnums=(0, 1))

    def one_microbatch(ex, loss_target):
        ex = dict(ex)
        diff = ex.pop(TWIN_DIFF_INPUT)
        return grad_fn(weights, diff, {**shared, **ex}, loss_target)

    if N_MICROBATCH == 1:
        loss, (grad_w, grad_x) = one_microbatch(per_example, given["loss_target"])
    else:
        def body(carry, xs):
            loss_sum, grad_sum = carry
            l_k, (gw_k, gx_k) = one_microbatch(xs[0], xs[1])
            with _jax.named_scope("update"):
                return (loss_sum + l_k, _jax.tree.map(_jnp.add, grad_sum, gw_k)), gx_k

        init = (_jnp.zeros((), _jnp.float32), _jax.tree.map(_jnp.zeros_like, weights))
        (loss, grad_w), grad_x = _jax.lax.scan(body, init, (per_example, given["loss_target"]))
    with _jax.named_scope("update"):
        delta_w, new_m, new_v = {}, {}, {}
        for n in TWIN_WEIGHTS:
            delta_w[n], new_m[n], new_v[n] = _adamw(weights[n], grad_w[n], given["m_" + n], given["v_" + n])
    return (loss, grad_x, *[grad_w[n] for n in TWIN_WEIGHTS], *[delta_w[n] for n in TWIN_WEIGHTS],
            *[new_m[n] for n in TWIN_WEIGHTS], *[new_v[n] for n in TWIN_WEIGHTS])
```

```python
import functools
import math

import jax
import jax.numpy as jnp
from jax import lax
from jax.experimental import pallas as pl
from jax.experimental.pallas import tpu as pltpu

F32 = jnp.float32
BF16 = jnp.bfloat16
NDEV = 8
HEAD = 64
BLK = 128
LANES = 128
EPS = 1e-6
ROT = HEAD // 4
ROPE_THETA = 500000.0
SCALE = HEAD ** -0.5
NEG = -1e30
VMEM_LIMIT = 56 * 1024 * 1024
MESH = pl.DeviceIdType.MESH
HIGH = lax.Precision.HIGHEST

ADAM_LR = 0.001
ADAM_B1 = 0.9
ADAM_B2 = 0.999
ADAM_EPS = 1e-08
ADAM_WD = 0.01
ADAM_STEP = 10


def _params(*sem):
    return pltpu.CompilerParams(dimension_semantics=sem, vmem_limit_bytes=VMEM_LIMIT)


def _pick(n, cap, mult):
    if n <= cap:
        return n
    best = None
    for t in range(mult, cap + 1, mult):
        if n % t == 0:
            best = t
    assert best is not None, (n, cap, mult)
    return best


def _dot(a, b, dims, precision=None):
    return lax.dot_general(a, b, (dims, ((), ())), preferred_element_type=F32, precision=precision)


NN = ((1,), (0,))
NT = ((1,), (1,))
TN = ((0,), (0,))


def _all_gather(x, name):
    m, n = x.shape

    def body(x_ref, out_ref, send_sems, recv_sems, local_sem):
        ix, iy, ic = lax.axis_index("x"), lax.axis_index("y"), lax.axis_index("c")
        me, sibling = (ix, iy, ic), (ix, iy, 1 - ic)
        chips = [(1 - ix, iy), (ix, 1 - iy), (1 - ix, 1 - iy)]

        def slab(px, py, pc):
            return out_ref.at[4 * px + 2 * py + pc]

        def copy(k, block, to, src=None):
            return pltpu.make_async_remote_copy(
                src_ref=slab(*block) if src is None else src, dst_ref=slab(*block),
                send_sem=send_sems.at[k], recv_sem=recv_sems.at[k], device_id=to, device_id_type=MESH)

        mine = pltpu.make_async_copy(x_ref, slab(*me), local_sem)
        mine.start()
        first = [copy(0, me, sibling, src=x_ref)]
        first += [copy(1 + j, me, (*chip, ic), src=x_ref) for j, chip in enumerate(chips)]
        for cp in first:
            cp.start()
        passed = [copy(4 + j, (*chip, ic), sibling) for j, chip in enumerate(chips)]
        for j, chip in enumerate(chips):
            copy(1 + j, (*chip, ic), me).wait_recv()
            passed[j].start()
        copy(0, sibling, me).wait_recv()
        for j, chip in enumerate(chips):
            copy(4 + j, (*chip, 1 - ic), me).wait_recv()
        for cp in first + passed:
            cp.wait_send()
        mine.wait()

    return pl.pallas_call(
        body, name=name,
        out_shape=jax.ShapeDtypeStruct((NDEV, m, n), x.dtype),
        in_specs=[pl.BlockSpec(memory_space=pl.ANY)],
        out_specs=pl.BlockSpec(memory_space=pl.ANY),
        scratch_shapes=[pltpu.SemaphoreType.DMA((7,)), pltpu.SemaphoreType.DMA((7,)), pltpu.SemaphoreType.DMA(())],
    )(x)


def _exchange(p, name):
    _, m, n = p.shape

    def body(p_ref, r_ref, send_sems, recv_sems, local_sem):
        ix, iy, ic = lax.axis_index("x"), lax.axis_index("y"), lax.axis_index("c")
        me = 4 * ix + 2 * iy + ic
        own = pltpu.make_async_copy(p_ref.at[me], r_ref.at[me], local_sem)
        own.start()
        copies = []
        for k in range(1, NDEV):
            px = 1 - ix if k & 4 else ix
            py = 1 - iy if k & 2 else iy
            pc = 1 - ic if k & 1 else ic
            cp = pltpu.make_async_remote_copy(
                src_ref=p_ref.at[4 * px + 2 * py + pc], dst_ref=r_ref.at[me],
                send_sem=send_sems.at[k - 1], recv_sem=recv_sems.at[k - 1],
                device_id=(px, py, pc), device_id_type=MESH)
            cp.start()
            copies.append(cp)
        for cp in copies:
            cp.wait()
        own.wait()

    return pl.pallas_call(
        body, name=name,
        out_shape=jax.ShapeDtypeStruct(p.shape, p.dtype),
        in_specs=[pl.BlockSpec(memory_space=pl.ANY)],
        out_specs=pl.BlockSpec(memory_space=pl.ANY),
        scratch_shapes=[pltpu.SemaphoreType.DMA((7,)), pltpu.SemaphoreType.DMA((7,)), pltpu.SemaphoreType.DMA(())],
    )(p)


def _sum_leading(r, name):
    k, m, n = r.shape
    tm = _pick(m, max(8, (4 * 1024 * 1024) // (k * n * 4) // 8 * 8), 8)

    def body(r_ref, o_ref):
        acc = r_ref[0]
        for s in range(1, k):
            acc = acc + r_ref[s]
        o_ref[...] = acc

    return pl.pallas_call(
        body, name=name, grid=(m // tm,),
        in_specs=[pl.BlockSpec((k, tm, n), lambda i: (0, i, 0))],
        out_specs=pl.BlockSpec((tm, n), lambda i: (i, 0)),
        out_shape=jax.ShapeDtypeStruct((m, n), F32),
        compiler_params=_params("parallel"),
    )(r)


def _mm_nt(a, bt, out_dtype, name):
    M, K = a.shape
    N = bt.shape[0]
    tm, tn = _pick(M, 512, 8), _pick(N, 1536, LANES)

    def body(a_ref, b_ref, o_ref):
        o_ref[...] = _dot(a_ref[...], b_ref[...], NT).astype(out_dtype)

    return pl.pallas_call(
        body, name=name, grid=(M // tm, N // tn),
        in_specs=[pl.BlockSpec((tm, K), lambda i, j: (i, 0)), pl.BlockSpec((tn, K), lambda i, j: (j, 0))],
        out_specs=pl.BlockSpec((tm, tn), lambda i, j: (i, j)),
        out_shape=jax.ShapeDtypeStruct((M, N), out_dtype),
        compiler_params=_params("parallel", "parallel"),
    )(a, bt)


def _mm_nn(pairs, name):
    M = pairs[0][0].shape[0]
    N = pairs[0][1].shape[1]
    tm, tn = _pick(M, 512, 8), _pick(N, 1024, LANES)
    np_ = len(pairs)

    def body(*refs):
        o_ref = refs[-1]
        acc = _dot(refs[0][...], refs[1][...], NN)
        for p in range(1, np_):
            acc = acc + _dot(refs[2 * p][...], refs[2 * p + 1][...], NN)
        o_ref[...] = acc

    in_specs, args = [], []
    for a, b in pairs:
        K = a.shape[1]
        in_specs += [pl.BlockSpec((tm, K), lambda i, j: (i, 0)), pl.BlockSpec((K, tn), lambda i, j: (0, j))]
        args += [a, b]
    return pl.pallas_call(
        body, name=name, grid=(M // tm, N // tn),
        in_specs=in_specs,
        out_specs=pl.BlockSpec((tm, tn), lambda i, j: (i, j)),
        out_shape=jax.ShapeDtypeStruct((M, N), F32),
        compiler_params=_params("parallel", "parallel"),
    )(*args)


def _mm_tn(a, b, name):
    M, N1 = a.shape
    N2 = b.shape[1]
    t1, tk = _pick(N1, 1536, LANES), _pick(M, 512, 8)

    def body(a_ref, b_ref, o_ref):
        @pl.when(pl.program_id(1) == 0)
        def _():
            o_ref[...] = jnp.zeros_like(o_ref)
        o_ref[...] += _dot(a_ref[...], b_ref[...], TN)

    return pl.pallas_call(
        body, name=name, grid=(N1 // t1, M // tk),
        in_specs=[pl.BlockSpec((tk, t1), lambda i, k: (k, i)), pl.BlockSpec((tk, N2), lambda i, k: (k, 0))],
        out_specs=pl.BlockSpec((t1, N2), lambda i, k: (i, 0)),
        out_shape=jax.ShapeDtypeStruct((N1, N2), F32),
        compiler_params=_params("parallel", "arbitrary"),
    )(a, b)


def _mm_res(a, w, x, gate, S, name):
    T, K = a.shape
    D = w.shape[1]
    tm, tn = _pick(S, 512, 8), _pick(D, 512, LANES)
    nb = S // tm

    def body(a_ref, w_ref, x_ref, g_ref, y_ref, o_ref):
        y = _dot(a_ref[...], w_ref[...], NN)
        y_ref[...] = y
        o_ref[...] = x_ref[...] + g_ref[0] * y

    return pl.pallas_call(
        body, name=name, grid=(T // tm, D // tn),
        in_specs=[pl.BlockSpec((tm, K), lambda i, j: (i, 0)), pl.BlockSpec((K, tn), lambda i, j: (0, j)),
                  pl.BlockSpec((tm, tn), lambda i, j: (i, j)), pl.BlockSpec((1, 1, tn), lambda i, j: (i // nb, 0, j))],
        out_specs=[pl.BlockSpec((tm, tn), lambda i, j: (i, j)), pl.BlockSpec((tm, tn), lambda i, j: (i, j))],
        out_shape=[jax.ShapeDtypeStruct((T, D), F32), jax.ShapeDtypeStruct((T, D), F32)],
        compiler_params=_params("parallel", "parallel"),
    )(a, w, x, gate)


def _swiglu_fwd(h, wgt, wut, name):
    T, D = h.shape
    F = wgt.shape[0]
    tm, tn = _pick(T, 512, 8), _pick(F, 1536, LANES)

    def body(h_ref, g_ref, u_ref, go_ref, uo_ref, a_ref):
        hh = h_ref[...]
        g = _dot(hh, g_ref[...], NT)
        u = _dot(hh, u_ref[...], NT)
        go_ref[...] = g
        uo_ref[...] = u
        a_ref[...] = (g * jax.nn.sigmoid(g) * u).astype(BF16)

    spec_w = pl.BlockSpec((tn, D), lambda i, j: (j, 0))
    spec_o = pl.BlockSpec((tm, tn), lambda i, j: (i, j))
    return pl.pallas_call(
        body, name=name, grid=(T // tm, F // tn),
        in_specs=[pl.BlockSpec((tm, D), lambda i, j: (i, 0)), spec_w, spec_w],
        out_specs=[spec_o, spec_o, spec_o],
        out_shape=[jax.ShapeDtypeStruct((T, F), F32), jax.ShapeDtypeStruct((T, F), F32),
                   jax.ShapeDtypeStruct((T, F), BF16)],
        compiler_params=_params("parallel", "parallel"),
    )(h, wgt, wut)


def _swiglu_bwd(dy, wd, gate, up, name):
    T, D = dy.shape
    F = wd.shape[0]
    tm, tn = _pick(T, 512, 8), _pick(F, 1536, LANES)

    def body(dy_ref, w_ref, g_ref, u_ref, dg_ref, du_ref):
        da = _dot(dy_ref[...], w_ref[...], NT)
        g = g_ref[...]
        sg = jax.nn.sigmoid(g)
        silu = g * sg
        du_ref[...] = (da * silu).astype(BF16)
        dg_ref[...] = (da * u_ref[...] * (sg + silu * (1.0 - sg))).astype(BF16)

    spec_o = pl.BlockSpec((tm, tn), lambda i, j: (i, j))
    return pl.pallas_call(
        body, name=name, grid=(T // tm, F // tn),
        in_specs=[pl.BlockSpec((tm, D), lambda i, j: (i, 0)), pl.BlockSpec((tn, D), lambda i, j: (j, 0)), spec_o, spec_o],
        out_specs=[spec_o, spec_o],
        out_shape=[jax.ShapeDtypeStruct((T, F), BF16), jax.ShapeDtypeStruct((T, F), BF16)],
        compiler_params=_params("parallel", "parallel"),
    )(dy, wd, gate, up)


def _norm_mod(x, gain, sc, sh, S, name):
    T, D = x.shape
    tm = _pick(S, 512, 8)
    nb = S // tm

    def body(x_ref, g_ref, sc_ref, sh_ref, o_ref):
        xv = x_ref[...]
        r = lax.rsqrt(jnp.mean(xv * xv, axis=-1, keepdims=True) + EPS)
        o_ref[...] = ((xv * r) * g_ref[...] * (1.0 + sc_ref[0]) + sh_ref[0]).astype(BF16)

    spec_b = pl.BlockSpec((1, 1, D), lambda i: (i // nb, 0, 0))
    return pl.pallas_call(
        body, name=name, grid=(T // tm,),
        in_specs=[pl.BlockSpec((tm, D), lambda i: (i, 0)), pl.BlockSpec((1, D), lambda i: (0, 0)), spec_b, spec_b],
        out_specs=pl.BlockSpec((tm, D), lambda i: (i, 0)),
        out_shape=jax.ShapeDtypeStruct((T, D), BF16),
        compiler_params=_params("parallel"),
    )(x, gain, sc, sh)


def _norm_mod_bwd(x, dh, dres, gain, sc, S, name):
    T, D = x.shape
    B = T // S
    tm = _pick(S, 256, 8)
    nb = S // tm

    def body(x_ref, dh_ref, dr_ref, g_ref, sc_ref, o_ref, dsh_ref, dsc_ref, dg_ref):
        @pl.when(pl.program_id(1) == 0)
        def _():
            dsh_ref[...] = jnp.zeros_like(dsh_ref)
            dsc_ref[...] = jnp.zeros_like(dsc_ref)
            dg_ref[...] = jnp.zeros_like(dg_ref)
        xv, dhv, g = x_ref[...], dh_ref[...], g_ref[...]
        r = lax.rsqrt(jnp.mean(xv * xv, axis=-1, keepdims=True) + EPS)
        xhat = xv * r
        dsh_ref[0] += jnp.sum(dhv, axis=0, keepdims=True)
        dsc_ref[0] += jnp.sum(dhv * (xhat * g), axis=0, keepdims=True)
        dn = dhv * (1.0 + sc_ref[0])
        dg_ref[0] += jnp.sum(dn * xhat, axis=0, keepdims=True)
        dxh = dn * g
        o_ref[...] = dr_ref[...] + r * (dxh - xhat * jnp.mean(dxh * xhat, axis=-1, keepdims=True))

    spec_t = pl.BlockSpec((tm, D), lambda b, i: (b * nb + i, 0))
    spec_b = pl.BlockSpec((1, 1, D), lambda b, i: (b, 0, 0))
    red = jax.ShapeDtypeStruct((B, 1, D), F32)
    return pl.pallas_call(
        body, name=name, grid=(B, nb),
        in_specs=[spec_t, spec_t, spec_t, pl.BlockSpec((1, D), lambda b, i: (0, 0)), spec_b],
        out_specs=[spec_t, spec_b, spec_b, spec_b],
        out_shape=[jax.ShapeDtypeStruct((T, D), F32), red, red, red],
        compiler_params=_params("parallel", "arbitrary"),
    )(x, dh, dres, gain, sc)


def _gate_bwd(dx, y, gate, S, name):
    T, D = dx.shape
    B = T // S
    tm = _pick(S, 512, 8)
    nb = S // tm

    def body(dx_ref, y_ref, g_ref, dy_ref, dg_ref):
        @pl.when(pl.program_id(1) == 0)
        def _():
            dg_ref[...] = jnp.zeros_like(dg_ref)
        d = dx_ref[...]
        dy_ref[...] = (d * g_ref[0]).astype(BF16)
        dg_ref[0] += jnp.sum(d * y_ref[...], axis=0, keepdims=True)

    spec_t = pl.BlockSpec((tm, D), lambda b, i: (b * nb + i, 0))
    spec_b = pl.BlockSpec((1, 1, D), lambda b, i: (b, 0, 0))
    return pl.pallas_call(
        body, name=name, grid=(B, nb),
        in_specs=[spec_t, spec_t, spec_b],
        out_specs=[spec_t, spec_b],
        out_shape=[jax.ShapeDtypeStruct((T, D), BF16), jax.ShapeDtypeStruct((B, 1, D), F32)],
        compiler_params=_params("parallel", "arbitrary"),
    )(dx, y, gate)


def _loss_head(y, target, name):
    T, D = y.shape
    tm = _pick(T, 512, 8)

    def body(y_ref, t_ref, dy_ref, l_ref):
        @pl.when(pl.program_id(0) == 0)
        def _():
            l_ref[...] = jnp.zeros_like(l_ref)
        e = y_ref[...] - t_ref[...]
        dy_ref[...] = e * (1.0 / D)
        l_ref[...] += 0.5 * jnp.sum(jnp.mean(e * e, axis=-1, keepdims=True), axis=0, keepdims=True)

    spec = pl.BlockSpec((tm, D), lambda i: (i, 0))
    return pl.pallas_call(
        body, name=name, grid=(T // tm,),
        in_specs=[spec, spec],
        out_specs=[spec, pl.BlockSpec((8, LANES), lambda i: (0, 0))],
        out_shape=[jax.ShapeDtypeStruct((T, D), F32), jax.ShapeDtypeStruct((8, LANES), F32)],
        compiler_params=_params("arbitrary"),
    )(y, target)


def _ada_fwd(c_all, ada_w, bias, name):
    NB, D = c_all.shape
    L, _, W = ada_w.shape

    def body(c_ref, w_ref, b_ref, o_ref):
        cv = c_ref[...]
        cond = cv * jax.nn.sigmoid(cv)
        o_ref[0] = _dot(cond, w_ref[0], NN, HIGH) + b_ref[0]

    return pl.pallas_call(
        body, name=name, grid=(L,),
        in_specs=[pl.BlockSpec((NB, D), lambda l: (0, 0)), pl.BlockSpec((1, D, W), lambda l: (l, 0, 0)),
                  pl.BlockSpec((1, 1, W), lambda l: (l, 0, 0))],
        out_specs=pl.BlockSpec((1, NB, W), lambda l: (l, 0, 0)),
        out_shape=jax.ShapeDtypeStruct((L, NB, W), F32),
        compiler_params=_params("parallel"),
    )(c_all, ada_w, bias)


def _ada_bwd(c_all, dmod, name):
    NB, D = c_all.shape
    L, _, W = dmod.shape

    def body(c_ref, d_ref, o_ref):
        cv = c_ref[...]
        cond = cv * jax.nn.sigmoid(cv)
        o_ref[0] = _dot(cond, d_ref[0], TN, HIGH)

    return pl.pallas_call(
        body, name=name, grid=(L,),
        in_specs=[pl.BlockSpec((NB, D), lambda l: (0, 0)), pl.BlockSpec((1, NB, W), lambda l: (l, 0, 0))],
        out_specs=pl.BlockSpec((1, D, W), lambda l: (l, 0, 0)),
        out_shape=jax.ShapeDtypeStruct((L, D, W), F32),
        compiler_params=_params("parallel"),
    )(c_all, dmod)


def _lo_mask(shape):
    return lax.broadcasted_iota(jnp.int32, shape, len(shape) - 1) < HEAD


def _head_sum_matrix():
    r = lax.broadcasted_iota(jnp.int32, (LANES, LANES), 0) // HEAD
    c = lax.broadcasted_iota(jnp.int32, (LANES, LANES), 1) // HEAD
    return (r == c).astype(F32)


def _rope(y, cs, s1, s2):
    return y * cs + pltpu.roll(y, LANES - ROT // 2, 1) * s1 + pltpu.roll(y, ROT // 2, 1) * s2


def _rope_bwd(d, cs, s1, s2):
    return d * cs + pltpu.roll(d * s1, ROT // 2, 1) + pltpu.roll(d * s2, LANES - ROT // 2, 1)


def _qk_prep(qkv, cs, s1, s2, qg, kg, name):
    T, W = qkv.shape
    NQ = W - 2 * LANES
    tm = _pick(T, 512, 8)

    def body(x_ref, cs_ref, s1_ref, s2_ref, qg_ref, kg_ref, q_ref, k_ref, v_ref):
        P = _head_sum_matrix()
        cs_, s1_, s2_ = cs_ref[...], s1_ref[...], s2_ref[...]
        lo = _lo_mask((tm, LANES))

        def norm_rope(xv, g):
            ms = _dot(xv * xv, P, NN, HIGH) * (1.0 / HEAD)
            return _rope(xv * lax.rsqrt(ms + EPS) * g, cs_, s1_, s2_)

        for j in range(NQ // LANES):
            q_ref[:, j * LANES:(j + 1) * LANES] = norm_rope(x_ref[:, j * LANES:(j + 1) * LANES], qg_ref[...]).astype(BF16)
        kr = norm_rope(x_ref[:, NQ:NQ + LANES], kg_ref[...])
        ks = pltpu.roll(kr, HEAD, 1)
        k_ref[:, :LANES] = jnp.where(lo, kr, ks).astype(BF16)
        k_ref[:, LANES:] = jnp.where(lo, ks, kr).astype(BF16)
        vr = x_ref[:, NQ + LANES:]
        vs = pltpu.roll(vr, HEAD, 1)
        v_ref[:, :LANES] = jnp.where(lo, vr, vs).astype(BF16)
        v_ref[:, LANES:] = jnp.where(lo, vs, vr).astype(BF16)

    spec_t = pl.BlockSpec((tm, LANES), lambda i: (i, 0))
    spec_g = pl.BlockSpec((1, LANES), lambda i: (0, 0))
    return pl.pallas_call(
        body, name=name, grid=(T // tm,),
        in_specs=[pl.BlockSpec((tm, W), lambda i: (i, 0)), spec_t, spec_t, spec_t, spec_g, spec_g],
        out_specs=[pl.BlockSpec((tm, NQ), lambda i: (i, 0)), pl.BlockSpec((tm, 2 * LANES), lambda i: (i, 0)),
                   pl.BlockSpec((tm, 2 * LANES), lambda i: (i, 0))],
        out_shape=[jax.ShapeDtypeStruct((T, NQ), BF16), jax.ShapeDtypeStruct((T, 2 * LANES), BF16),
                   jax.ShapeDtypeStruct((T, 2 * LANES), BF16)],
        compiler_params=_params("parallel"),
    )(qkv, cs, s1, s2, qg, kg)


def _stack_heads(x2):
    lo = _lo_mask(x2.shape)
    z = jnp.zeros_like(x2)
    return jnp.concatenate([jnp.where(lo, x2, z), jnp.where(lo, z, x2)], axis=0)


def _unstack_heads(xs):
    r = xs.shape[0] // 2
    return jnp.where(_lo_mask((r, LANES)), xs[:r], xs[r:])


def _swa_scores(q2, kp, kc, sink2, i):
    qs = _stack_heads(q2) * SCALE
    kk = jnp.concatenate([kp, kc], axis=0)
    s = _dot(qs, kk, NT)
    qo = lax.broadcasted_iota(jnp.int32, s.shape, 0) % BLK
    kc_ = lax.broadcasted_iota(jnp.int32, s.shape, 1)
    rel = qo + BLK - kc_
    valid = (rel >= 0) & (rel < BLK) & ((kc_ >= BLK) | (i > 0))
    sk = jnp.concatenate([jnp.broadcast_to(sink2[:, 0:1], (BLK, 1)), jnp.broadcast_to(sink2[:, HEAD:HEAD + 1], (BLK, 1))], axis=0)
    return qs, kk, jnp.where(valid, s, NEG), valid, sk


def _swa_fwd(q, kd, vd, sink2, B, name):
    T, NQ = q.shape
    NP = NQ // LANES
    nq = T // B // BLK
    grp = NP // (kd.shape[1] // LANES)

    def body(q_ref, kp_ref, kc_ref, vp_ref, vc_ref, s_ref, o_ref, l_ref):
        i = pl.program_id(2)
        _, _, s, valid, sk = _swa_scores(q_ref[...], kp_ref[...], kc_ref[...], s_ref[0], i)
        m = jnp.maximum(jnp.max(s, axis=1, keepdims=True), sk)
        p = jnp.where(valid, jnp.exp(s - m), 0.0)
        l = jnp.sum(p, axis=1, keepdims=True) + jnp.exp(sk - m)
        p = (p / l).astype(BF16)
        p2 = jnp.concatenate([p[:BLK], p[BLK:]], axis=1)
        vv = jnp.concatenate([vp_ref[...], vc_ref[...]], axis=0)
        o_ref[...] = _dot(p2, _stack_heads(vv), NN).astype(BF16)
        l_ref[...] = _unstack_heads(jnp.broadcast_to(m + jnp.log(l), (2 * BLK, LANES)))

    spec_q = pl.BlockSpec((BLK, LANES), lambda b, j, i: (b * nq + i, j))
    spec_p = pl.BlockSpec((BLK, LANES), lambda b, j, i: (b * nq + jnp.maximum(i - 1, 0), j // grp))
    spec_c = pl.BlockSpec((BLK, LANES), lambda b, j, i: (b * nq + i, j // grp))
    return pl.pallas_call(
        body, name=name, grid=(B, NP, nq),
        in_specs=[spec_q, spec_p, spec_c, spec_p, spec_c, pl.BlockSpec((1, 1, LANES), lambda b, j, i: (j, 0, 0))],
        out_specs=[spec_q, spec_q],
        out_shape=[jax.ShapeDtypeStruct((T, NQ), BF16), jax.ShapeDtypeStruct((T, NQ), F32)],
        compiler_params=_params("parallel", "parallel", "parallel"),
    )(q, kd, kd, vd, vd, sink2)


def _swa_bwd(q, kd, vd, sink2, do, lse, B, name):
    T, NQ = q.shape
    NP = NQ // LANES
    nq = T // B // BLK
    grp = NP // (kd.shape[1] // LANES)

    def body(q_ref, kp_ref, kc_ref, vp_ref, vc_ref, s_ref, do_ref, l_ref,
             dq_ref, dkc_ref, dkp_ref, dvc_ref, dvp_ref, ds_ref):
        b, i = pl.program_id(1), pl.program_id(2)

        @pl.when((b == 0) & (i == 0))
        def _():
            ds_ref[...] = jnp.zeros_like(ds_ref)
        qs, kk, s, valid, sk = _swa_scores(q_ref[...], kp_ref[...], kc_ref[...], s_ref[0], i)
        lse_ = l_ref[...]
        lse_s = jnp.concatenate([lse_[:, 0:1], lse_[:, HEAD:HEAD + 1]], axis=0)
        p = jnp.where(valid, jnp.exp(s - lse_s), 0.0)
        dos = _stack_heads(do_ref[...])
        vv = jnp.concatenate([vp_ref[...], vc_ref[...]], axis=0)
        dp = _dot(dos, vv, NT)
        delta = jnp.sum(p * dp, axis=1, keepdims=True)
        dsc = (p * (dp - delta)).astype(BF16)
        dsk = -jnp.exp(sk - lse_s) * delta
        dsk_lo = jnp.sum(dsk[:BLK], axis=0, keepdims=True)
        dsk_hi = jnp.sum(dsk[BLK:], axis=0, keepdims=True)
        ds_ref[0] += jnp.where(_lo_mask((1, LANES)), dsk_lo, dsk_hi)
        dq_ref[...] = _unstack_heads(_dot(dsc, kk, NN)) * SCALE
        dk = _dot(dsc, qs, TN)
        dv = _dot(p.astype(BF16), dos, TN)
        dkp_ref[...] = dk[:BLK]
        dkc_ref[...] = dk[BLK:]
        dvp_ref[...] = dv[:BLK]
        dvc_ref[...] = dv[BLK:]

    spec_q = pl.BlockSpec((BLK, LANES), lambda j, b, i: (b * nq + i, j))
    spec_p = pl.BlockSpec((BLK, LANES), lambda j, b, i: (b * nq + jnp.maximum(i - 1, 0), j // grp))
    spec_c = pl.BlockSpec((BLK, LANES), lambda j, b, i: (b * nq + i, j // grp))
    spec_s = pl.BlockSpec((1, 1, LANES), lambda j, b, i: (j, 0, 0))
    big = jax.ShapeDtypeStruct((T, NQ), F32)
    return pl.pallas_call(
        body, name=name, grid=(NP, B, nq),
        in_specs=[spec_q, spec_p, spec_c, spec_p, spec_c, spec_s, spec_q, spec_q],
        out_specs=[spec_q, spec_q, spec_q, spec_q, spec_q, spec_s],
        out_shape=[big, big, big, big, big, jax.ShapeDtypeStruct((NP, 1, LANES), F32)],
        compiler_params=_params("arbitrary", "arbitrary", "arbitrary"),
    )(q, kd, kd, vd, vd, sink2, do, lse)


def _qk_prep_bwd(qkv, cs, s1, s2, qg, kg, dq, dkc, dkp, dvc, dvp, B, name):
    T, W = qkv.shape
    NQ = W - 2 * LANES
    NP = NQ // LANES
    nq = T // B // BLK
    grp = NP // 2

    def body(x_ref, cs_ref, s1_ref, s2_ref, qg_ref, kg_ref, dq_ref, dkc_ref, dkp_ref, dvc_ref, dvp_ref,
             o_ref, dqg_ref, dkg_ref):
        b, i = pl.program_id(0), pl.program_id(1)

        @pl.when((b == 0) & (i == 0))
        def _():
            dqg_ref[...] = jnp.zeros_like(dqg_ref)
            dkg_ref[...] = jnp.zeros_like(dkg_ref)
        P = _head_sum_matrix()
        cs_, s1_, s2_ = cs_ref[...], s1_ref[...], s2_ref[...]
        lo = _lo_mask((BLK, LANES))
        has_next = (i + 1 < nq).astype(F32)

        def norm_rope_bwd(xv, g, d):
            du = _rope_bwd(d, cs_, s1_, s2_)
            r = lax.rsqrt(_dot(xv * xv, P, NN, HIGH) * (1.0 / HEAD) + EPS)
            xhat = xv * r
            dgain = jnp.sum(du * xhat, axis=0, keepdims=True)
            uu = du * g
            dx = r * (uu - xhat * (_dot(uu * xhat, P, NN, HIGH) * (1.0 / HEAD)))
            return dx, dgain + pltpu.roll(dgain, HEAD, 1)

        dqg = jnp.zeros((1, LANES), F32)
        for j in range(NP):
            sl = slice(j * LANES, (j + 1) * LANES)
            dx, dg = norm_rope_bwd(x_ref[:, sl], qg_ref[...], dq_ref[:, sl])
            o_ref[:, sl] = dx.astype(BF16)
            dqg = dqg + dg
        dqg_ref[...] += dqg

        def fold(c_ref, p_ref, g):
            t = jnp.zeros((BLK, LANES), F32)
            for j in range(g * grp, (g + 1) * grp):
                sl = slice(j * LANES, (j + 1) * LANES)
                t = t + c_ref[:, sl] + has_next * p_ref[:, sl]
            return t + pltpu.roll(t, HEAD, 1)

        dk = jnp.where(lo, fold(dkc_ref, dkp_ref, 0), fold(dkc_ref, dkp_ref, 1))
        dx, dg = norm_rope_bwd(x_ref[:, NQ:NQ + LANES], kg_ref[...], dk)
        o_ref[:, NQ:NQ + LANES] = dx.astype(BF16)
        dkg_ref[...] += dg
        dv = jnp.where(lo, fold(dvc_ref, dvp_ref, 0), fold(dvc_ref, dvp_ref, 1))
        o_ref[:, NQ + LANES:] = dv.astype(BF16)

    spec_t = pl.BlockSpec((BLK, LANES), lambda b, i: (b * nq + i, 0))
    spec_g = pl.BlockSpec((1, LANES), lambda b, i: (0, 0))
    spec_c = pl.BlockSpec((BLK, NQ), lambda b, i: (b * nq + i, 0))
    spec_n = pl.BlockSpec((BLK, NQ), lambda b, i: (b * nq + jnp.minimum(i + 1, nq - 1), 0))
    row = jax.ShapeDtypeStruct((1, LANES), F32)
    return pl.pallas_call(
        body, name=name, grid=(B, nq),
        in_specs=[pl.BlockSpec((BLK, W), lambda b, i: (b * nq + i, 0)), spec_t, spec_t, spec_t, spec_g, spec_g,
                  spec_c, spec_c, spec_n, spec_c, spec_n],
        out_specs=[pl.BlockSpec((BLK, W), lambda b, i: (b * nq + i, 0)), spec_g, spec_g],
        out_shape=[jax.ShapeDtypeStruct((T, W), BF16), row, row],
        compiler_params=_params("arbitrary", "arbitrary"),
    )(qkv, cs, s1, s2, qg, kg, dq, dkc, dkp, dvc, dvp)


def _split_bf16(x):
    hi = x.astype(BF16)
    return hi, (x - hi.astype(F32)).astype(BF16)


def _sb_terms(qs, kj, i, j):
    z = _dot(qs, kj, NT)
    e = jnp.exp(-jnp.abs(z))
    lb = jnp.minimum(z, 0.0) - jnp.log(1.0 + e)
    row = lax.broadcasted_iota(jnp.int32, z.shape, 0) % BLK
    col = lax.broadcasted_iota(jnp.int32, z.shape, 1)
    strict = (j * BLK + col) < (i * BLK + row)
    return lb, jnp.where(strict, lb - z, 0.0), strict, z, e


def _sb_fwd(qkv, B, name):
    T, W = qkv.shape
    NQ = W // 3
    NP = NQ // LANES
    S = T // B
    nq = S // BLK

    def body(q_ref, k_ref, v_ref, o_ref, t_ref):
        i = pl.program_id(2)
        qs = _stack_heads(q_ref[...]) * SCALE
        r_ = lax.broadcasted_iota(jnp.int32, (BLK, BLK), 0)
        c_ = lax.broadcasted_iota(jnp.int32, (BLK, BLK), 1)
        U = (r_ > c_).astype(BF16)

        def step(n, carry):
            c, acc = carry
            j = i - n
            rows = pl.ds(pl.multiple_of(j * BLK, BLK), BLK)
            lb, L, strict, _, _ = _sb_terms(qs, k_ref[rows, :], i, j)
            Lh, Ll = _split_bf16(L)
            after = _dot(Lh, U, NN) + _dot(Ll, U, NN) + c
            a = jnp.where(strict, jnp.exp(lb + after), 0.0).astype(BF16)
            a2 = jnp.concatenate([a[:BLK], a[BLK:]], axis=1)
            acc = acc + _dot(a2, _stack_heads(v_ref[rows, :]), NN)
            return c + jnp.sum(L, axis=1, keepdims=True), acc

        c, acc = lax.fori_loop(0, i + 1, step, (jnp.zeros((2 * BLK, 1), F32), jnp.zeros((BLK, LANES), F32)))
        o_ref[...] = acc.astype(BF16)
        t_ref[...] = _unstack_heads(jnp.broadcast_to(c, (2 * BLK, LANES)))

    spec_q = pl.BlockSpec((BLK, LANES), lambda b, p, i: (b * nq + i, p))
    return pl.pallas_call(
        body, name=name, grid=(B, NP, nq),
        in_specs=[spec_q, pl.BlockSpec((S, LANES), lambda b, p, i: (b, NP + p)),
                  pl.BlockSpec((S, LANES), lambda b, p, i: (b, 2 * NP + p))],
        out_specs=[spec_q, spec_q],
        out_shape=[jax.ShapeDtypeStruct((T, NQ), BF16), jax.ShapeDtypeStruct((T, NQ), F32)],
        compiler_params=_params("parallel", "parallel", "arbitrary"),
    )(qkv, qkv, qkv)


def _sb_bwd(qkv, do, tot, B, name):
    T, W = qkv.shape
    NQ = W // 3
    NP = NQ // LANES
    S = T // B
    nq = S // BLK

    def body(q_ref, k_ref, v_ref, do_ref, t_ref, dq_ref, dk_ref, dv_ref):
        i = pl.program_id(2)

        @pl.when(i == 0)
        def _():
            dk_ref[...] = jnp.zeros_like(dk_ref)
            dv_ref[...] = jnp.zeros_like(dv_ref)
        qs = _stack_heads(q_ref[...]) * SCALE
        dos = _stack_heads(do_ref[...])
        tt = t_ref[...]
        tot_s = jnp.concatenate([tt[:, 0:1], tt[:, HEAD:HEAD + 1]], axis=0)
        r_ = lax.broadcasted_iota(jnp.int32, (BLK, BLK), 0)
        c_ = lax.broadcasted_iota(jnp.int32, (BLK, BLK), 1)
        Uinc = (r_ <= c_).astype(BF16)
        Uexc = (r_ < c_).astype(BF16)

        def step(j, carry):
            cc, cg, dq = carry
            rows = pl.ds(pl.multiple_of(j * BLK, BLK), BLK)
            kj, vj = k_ref[rows, :], v_ref[rows, :]
            lb, L, strict, z, e = _sb_terms(qs, kj, i, j)
            Lh, Ll = _split_bf16(L)
            csum = _dot(Lh, Uinc, NN) + _dot(Ll, Uinc, NN) + cc
            a = jnp.where(strict, jnp.exp(lb + (tot_s - csum)), 0.0)
            g = a * _dot(dos, vj, NT)
            gh, gl = _split_bf16(g)
            G = _dot(gh, Uexc, NN) + _dot(gl, Uexc, NN) + cg
            rcp = 1.0 / (1.0 + e)
            pos = z >= 0.0
            beta = jnp.where(pos, rcp, e * rcp)
            omb = jnp.where(pos, e * rcp, rcp)
            dz = jnp.where(strict, g * omb - G * beta, 0.0).astype(BF16)
            dk_ref[rows, :] += _dot(dz, qs, TN)
            dv_ref[rows, :] += _dot(a.astype(BF16), dos, TN)
            return (cc + jnp.sum(L, axis=1, keepdims=True), cg + jnp.sum(g, axis=1, keepdims=True),
                    dq + _dot(dz, kj, NN))

        zero = jnp.zeros((2 * BLK, 1), F32)
        _, _, dq = lax.fori_loop(0, i + 1, step, (zero, zero, jnp.zeros((2 * BLK, LANES), F32)))
        dq_ref[...] = _unstack_heads(dq) * SCALE

    spec_q = pl.BlockSpec((BLK, LANES), lambda b, p, i: (b * nq + i, p))
    spec_s = pl.BlockSpec((S, LANES), lambda b, p, i: (b, p))
    big = jax.ShapeDtypeStruct((T, NQ), F32)
    return pl.pallas_call(
        body, name=name, grid=(B, NP, nq),
        in_specs=[spec_q, pl.BlockSpec((S, LANES), lambda b, p, i: (b, NP + p)),
                  pl.BlockSpec((S, LANES), lambda b, p, i: (b, 2 * NP + p)), spec_q, spec_q],
        out_specs=[spec_q, spec_s, spec_s],
        out_shape=[big, big, big],
        compiler_params=_params("parallel", "parallel", "arbitrary"),
    )(qkv, qkv, qkv, do, tot)


def _adamw(w, g, m, v, name):
    shape = w.shape
    cols = shape[-1]
    rows = math.prod(shape[:-1])
    tr = _pick(rows, max(8, (1 << 19) // max(cols, LANES) // 8 * 8), 8)

    def body(w_ref, g_ref, m_ref, v_ref, d_ref, mo_ref, vo_ref):
        gv = g_ref[...]
        mn = ADAM_B1 * m_ref[...] + (1.0 - ADAM_B1) * gv
        vn = ADAM_B2 * v_ref[...] + (1.0 - ADAM_B2) * (gv * gv)
        m_hat = mn / (1.0 - ADAM_B1 ** ADAM_STEP)
        v_hat = vn / (1.0 - ADAM_B2 ** ADAM_STEP)
        d_ref[...] = -ADAM_LR * (m_hat / (jnp.sqrt(v_hat) + ADAM_EPS) + ADAM_WD * w_ref[...])
        mo_ref[...] = mn
        vo_ref[...] = vn

    spec = pl.BlockSpec((tr, cols), lambda i: (i, 0))
    out = jax.ShapeDtypeStruct((rows, cols), F32)
    d, mn, vn = pl.pallas_call(
        body, name=name, grid=(rows // tr,),
        in_specs=[spec] * 4, out_specs=[spec] * 3, out_shape=[out] * 3,
        compiler_params=_params("parallel"),
    )(w.reshape(rows, cols), g.reshape(rows, cols), m.reshape(rows, cols), v.reshape(rows, cols))
    return d.reshape(shape), mn.reshape(shape), vn.reshape(shape)


def _pad_rows(a, rows):
    return jnp.pad(a, ((0, rows - a.shape[0]), (0, 0)))


def kernel(x, c, positions, ada_w, ada_b, norm1_g, norm2_g, wqkv_a, q_norm_a, k_norm_a, sinks_a, wo_a, wqkv_b, wo_b, w_gate, w_up, w_down, loss_target, m_ada_w, m_ada_b, m_norm1_g, m_norm2_g, m_wqkv_a, m_q_norm_a, m_k_norm_a, m_sinks_a, m_wo_a, m_wqkv_b, m_wo_b, m_w_gate, m_w_up, m_w_down, v_ada_w, v_ada_b, v_norm1_g, v_norm2_g, v_wqkv_a, v_q_norm_a, v_k_norm_a, v_sinks_a, v_wo_a, v_wqkv_b, v_wo_b, v_w_gate, v_w_up, v_w_down):
    B, S, D = x.shape
    T = B * S
    L = ada_w.shape[0]
    NA, NB_ = wqkv_a.shape[0], wqkv_b.shape[0]
    me = 4 * lax.axis_index("x") + 2 * lax.axis_index("y") + lax.axis_index("c")
    xt = x.reshape(T, D)

    col_sharded = [(wqkv_a, NA), (wqkv_b, NB_), (w_gate, L), (w_up, L)]
    row_sharded = [(wo_a, NA), (wo_b, NB_), (w_down, L)]
    pieces, layout = [], []
    for w, n in col_sharded:
        for l in range(n):
            pieces.append(w[l].T.astype(BF16))
            layout.append(w.shape[2])
    for w, n in row_sharded:
        for l in range(n):
            pieces.append(w[l].astype(BF16))
            layout.append(w.shape[1])
    packed = jnp.concatenate(pieces, axis=0)
    R = packed.shape[0]
    gathered = _all_gather(packed, "ag_weights")
    full, off = [], 0
    for rows in layout:
        full.append(gathered[:, off:off + rows, :].reshape(NDEV * rows, D))
        off += rows
    it = iter(full)
    wqkv_a_t = [next(it) for _ in range(NA)]
    wqkv_b_t = [next(it) for _ in range(NB_)]
    wg_t = [next(it) for _ in range(L)]
    wu_t = [next(it) for _ in range(L)]
    wo_a_f = [next(it) for _ in range(NA)]
    wo_b_f = [next(it) for _ in range(NB_)]
    wd_f = [next(it) for _ in range(L)]

    WA = ada_w.shape[2]
    c_all = _all_gather(c, "ag_c").reshape(NDEV * B, D)
    bias = lax.dynamic_slice_in_dim(ada_b, me * WA, WA, axis=1).reshape(L, 1, WA)
    mod_part = _ada_fwd(c_all, ada_w, bias, "ada_fwd")
    mod_all = _all_gather(mod_part.reshape(L * NDEV * B, WA), "ag_mod")
    mod_all = mod_all.reshape(NDEV, L, NDEV * B, WA).transpose(1, 2, 0, 3).reshape(L, NDEV * B, NDEV * WA)
    mod = lax.dynamic_slice_in_dim(mod_all, me * B, B, axis=1)
    mod = mod.reshape(L, B, 6, 1, D)
    sh1, sc1, g1, sh2, sc2, g2 = [mod[:, :, k] for k in range(6)]

    half = ROT // 2
    inv_freq = jnp.power(jnp.float32(ROPE_THETA), -jnp.arange(half, dtype=F32) * 2.0 / ROT)
    ang = positions.reshape(T, 1).astype(F32) * inv_freq[None, :]
    cos, sin = jnp.cos(ang), jnp.sin(ang)
    ones = jnp.ones((T, HEAD - ROT), F32)
    zeros = jnp.zeros((T, HEAD - ROT), F32)
    z8 = jnp.zeros((T, half), F32)
    cs = jnp.tile(jnp.concatenate([cos, cos, ones], axis=1), (1, 2))
    s1 = jnp.tile(jnp.concatenate([-sin, z8, zeros], axis=1), (1, 2))
    s2 = jnp.tile(jnp.concatenate([z8, sin, zeros], axis=1), (1, 2))

    saved = []
    xc = xt
    for l in range(L):
        j = l // 2
        h1 = _norm_mod(xc, norm1_g[l:l + 1], sc1[l], sh1[l], S, f"norm1_{l}")
        sv = dict(x_in=xc, h1=h1)
        if l % 2 == 0:
            qkv = _mm_nt(h1, wqkv_a_t[j], F32, f"qkv_a_{l}")
            qg = jnp.tile(q_norm_a[j:j + 1], (1, 2))
            kg = jnp.tile(k_norm_a[j:j + 1], (1, 2))
            qn, kd, vd = _qk_prep(qkv, cs, s1, s2, qg, kg, f"qk_prep_{l}")
            sink2 = jnp.repeat(sinks_a[j].reshape(-1, 2), HEAD, axis=1).reshape(-1, 1, LANES)
            attn, lse = _swa_fwd(qn, kd, vd, sink2, B, f"swa_fwd_{l}")
            sv.update(qkv=qkv, qg=qg, kg=kg, qn=qn, kd=kd, vd=vd, sink2=sink2, lse=lse)
            wo = wo_a_f[j]
        else:
            qkv = _mm_nt(h1, wqkv_b_t[j], BF16, f"qkv_b_{l}")
            attn, tot = _sb_fwd(qkv, B, f"sb_fwd_{l}")
            sv.update(qkv=qkv, tot=tot)
            wo = wo_b_f[j]
        y1, xm = _mm_res(attn, wo, xc, g1[l], S, f"attn_out_{l}")
        h2 = _norm_mod(xm, norm2_g[l:l + 1], sc2[l], sh2[l], S, f"norm2_{l}")
        gate, up, act = _swiglu_fwd(h2, wg_t[l], wu_t[l], f"swiglu_fwd_{l}")
        y2, xc = _mm_res(act, wd_f[l], xm, g2[l], S, f"mlp_out_{l}")
        sv.update(attn=attn, y1=y1, x_mid=xm, h2=h2, gate=gate, up=up, act=act, y2=y2)
        saved.append(sv)

    dx, loss_tile = _loss_head(xc, loss_target.reshape(T, D), "loss_head")

    g_qkv_a, g_qkv_b, g_gate, g_up, g_wo_a, g_wo_b, g_down = ([None] * NA, [None] * NB_, [None] * L, [None] * L,
                                                             [None] * NA, [None] * NB_, [None] * L)
    dmod = [None] * L
    dn1, dn2 = [None] * L, [None] * L
    dqg, dkg, dsink = [None] * NA, [None] * NA, [None] * NA
    for l in reversed(range(L)):
        j = l // 2
        sv = saved[l]
        dy2, dg2 = _gate_bwd(dx, sv["y2"], g2[l], S, f"gate2_bwd_{l}")
        dgate, dup = _swiglu_bwd(dy2, wd_f[l], sv["gate"], sv["up"], f"swiglu_bwd_{l}")
        g_down[l] = _mm_tn(sv["act"], dy2, f"dw_down_{l}")
        dh2 = _mm_nn([(dgate, wg_t[l]), (dup, wu_t[l])], f"dh2_{l}")
        g_gate[l] = _mm_tn(dgate, sv["h2"], f"dw_gate_{l}")
        g_up[l] = _mm_tn(dup, sv["h2"], f"dw_up_{l}")
        dxm, dsh2, dsc2, dn2[l] = _norm_mod_bwd(sv["x_mid"], dh2, dx, norm2_g[l:l + 1], sc2[l], S, f"norm2_bwd_{l}")
        dy1, dg1 = _gate_bwd(dxm, sv["y1"], g1[l], S, f"gate1_bwd_{l}")
        wo = wo_a_f[j] if l % 2 == 0 else wo_b_f[j]
        dattn = _mm_nt(dy1, wo, BF16, f"dattn_{l}")
        gwo = _mm_tn(sv["attn"], dy1, f"dw_o_{l}")
        if l % 2 == 0:
            g_wo_a[j] = gwo
            dq, dkc, dkp, dvc, dvp, dsink[j] = _swa_bwd(sv["qn"], sv["kd"], sv["vd"], sv["sink2"], dattn, sv["lse"], B,
                                                        f"swa_bwd_{l}")
            dqkv, dqg[j], dkg[j] = _qk_prep_bwd(sv["qkv"], cs, s1, s2, sv["qg"], sv["kg"], dq, dkc, dkp, dvc, dvp, B,
                                                f"qk_prep_bwd_{l}")
            wt = wqkv_a_t[j]
        else:
            g_wo_b[j] = gwo
            dq, dk, dv = _sb_bwd(sv["qkv"], dattn, sv["tot"], B, f"sb_bwd_{l}")
            dqkv = jnp.concatenate([dq, dk, dv], axis=1).astype(BF16)
            wt = wqkv_b_t[j]
        dh1 = _mm_nn([(dqkv, wt)], f"dh1_{l}")
        gq = _mm_tn(dqkv, sv["h1"], f"dw_qkv_{l}")
        if l % 2 == 0:
            g_qkv_a[j] = gq
        else:
            g_qkv_b[j] = gq
        dx, dsh1, dsc1, dn1[l] = _norm_mod_bwd(sv["x_in"], dh1, dxm, norm1_g[l:l + 1], sc1[l], S, f"norm1_bwd_{l}")
        dmod[l] = jnp.concatenate([dsh1, dsc1, dg1, dsh2, dsc2, dg2], axis=1)
    grad_x = dx.reshape(B, S, D)

    ndm = L * 6
    dmod_rows = jnp.stack(dmod, axis=1).reshape(B * ndm, D)
    misc = jnp.concatenate(
        [jnp.concatenate(dn1, axis=0).reshape(B * L, D), jnp.concatenate(dn2, axis=0).reshape(B * L, D),
         _pad_rows(jnp.concatenate([jnp.pad(r, ((0, 0), (0, D - LANES))) for r in dqg + dkg]
                                   + [jnp.pad(r[:, 0, ::HEAD].reshape(1, -1), ((0, 0), (0, D - 2 * r.shape[0]))) for r in dsink]
                                   + [jnp.pad(loss_tile[0:1, 0:1], ((0, 0), (0, D - 1)))], axis=0), 8)], axis=0)
    nmisc = misc.shape[0]
    small = _all_gather(jnp.concatenate([dmod_rows, _pad_rows(misc, -(-nmisc // 8) * 8)], axis=0), "ag_small")
    dmod_all = small[:, :B * ndm].reshape(NDEV * B, ndm, D)
    g_ada_b = _sum_leading(dmod_all, "sum_dmod").reshape(L, 6 * D)
    misc_sum = _sum_leading(small[:, B * ndm:], "sum_misc")
    g_n1 = misc_sum[0:B * L].reshape(L, B, D)
    g_n2 = misc_sum[B * L:2 * B * L].reshape(L, B, D)
    g_norm1 = _sum_leading(g_n1.transpose(1, 0, 2), "sum_n1")
    g_norm2 = _sum_leading(g_n2.transpose(1, 0, 2), "sum_n2")
    o = 2 * B * L
    g_qn = misc_sum[o:o + NA, :HEAD]
    g_kn = misc_sum[o + NA:o + 2 * NA, :HEAD]
    nsink = sinks_a.shape[1]
    g_sink = misc_sum[o + 2 * NA:o + 3 * NA, :nsink]
    loss = misc_sum[o + 3 * NA, 0]

    dmod_loc = lax.dynamic_slice_in_dim(dmod_all.reshape(NDEV * B, L, 6 * D), me * WA, WA, axis=2)
    g_ada_w = _ada_bwd(c_all, dmod_loc.transpose(1, 0, 2), "ada_bwd")

    parts = []
    for g in g_qkv_a + g_qkv_b + g_gate + g_up + g_wo_a + g_wo_b + g_down:
        parts.append(g.reshape(NDEV, g.shape[0] // NDEV, D))
    partial = jnp.concatenate(parts, axis=1)
    received = _exchange(partial, "grad_exchange")
    gsum = _sum_leading(received, "grad_sum")
    shards, off = [], 0
    for rows in layout:
        shards.append(gsum[off:off + rows])
        off += rows
    it = iter(shards)
    gw_qkv_a = jnp.stack([next(it).T for _ in range(NA)])
    gw_qkv_b = jnp.stack([next(it).T for _ in range(NB_)])
    gw_gate = jnp.stack([next(it).T for _ in range(L)])
    gw_up = jnp.stack([next(it).T for _ in range(L)])
    gw_wo_a = jnp.stack([next(it) for _ in range(NA)])
    gw_wo_b = jnp.stack([next(it) for _ in range(NB_)])
    gw_down = jnp.stack([next(it) for _ in range(L)])

    grads = [g_ada_w, g_ada_b, g_norm1, g_norm2, gw_qkv_a, g_qn, g_kn, g_sink, gw_wo_a, gw_qkv_b, gw_wo_b,
             gw_gate, gw_up, gw_down]
    ws = [ada_w, ada_b, norm1_g, norm2_g, wqkv_a, q_norm_a, k_norm_a, sinks_a, wo_a, wqkv_b, wo_b, w_gate, w_up, w_down]
    ms = [m_ada_w, m_ada_b, m_norm1_g, m_norm2_g, m_wqkv_a, m_q_norm_a, m_k_norm_a, m_sinks_a, m_wo_a, m_wqkv_b,
          m_wo_b, m_w_gate, m_w_up, m_w_down]
    vs = [v_ada_w, v_ada_b, v_norm1_g, v_norm2_g, v_wqkv_a, v_q_norm_a, v_k_norm_a, v_sinks_a, v_wo_a, v_wqkv_b,
          v_wo_b, v_w_gate, v_w_up, v_w_down]
    deltas, new_m, new_v = [], [], []
    for k, (w, g, m, v) in enumerate(zip(ws, grads, ms, vs)):
        g = g.reshape(w.shape)
        d, mn, vn = _adamw(w, g, m, v, f"adamw_{k}")
        grads[k] = g
        deltas.append(d)
        new_m.append(mn)
        new_v.append(vn)
    return (loss, grad_x, *grads, *deltas, *new_m, *new_v)
```

```python
import functools
import math

import jax
import jax.numpy as jnp
from jax import lax
from jax.experimental import pallas as pl
from jax.experimental.pallas import tpu as pltpu

F32 = jnp.float32
BF16 = jnp.bfloat16
NDEV = 8
HEAD = 64
BLK = 128
LANES = 128
EPS = 1e-6
ROT = HEAD // 4
ROPE_THETA = 500000.0
SCALE = HEAD ** -0.5
NEG = -1e30
VMEM_LIMIT = 56 * 1024 * 1024
MESH = pl.DeviceIdType.MESH
HIGH = lax.Precision.HIGHEST

ADAM_LR = 0.001
ADAM_B1 = 0.9
ADAM_B2 = 0.999
ADAM_EPS = 1e-08
ADAM_WD = 0.01
ADAM_STEP = 10


def _params(*sem):
    return pltpu.CompilerParams(dimension_semantics=sem, vmem_limit_bytes=VMEM_LIMIT)


def _pick(n, cap, mult):
    if n <= cap:
        return n
    best = None
    for t in range(mult, cap + 1, mult):
        if n % t == 0:
            best = t
    assert best is not None, (n, cap, mult)
    return best


def _dot(a, b, dims, precision=None):
    return lax.dot_general(a, b, (dims, ((), ())), preferred_element_type=F32, precision=precision)


NN = ((1,), (0,))
NT = ((1,), (1,))
TN = ((0,), (0,))


def _all_gather(x, name):
    m, n = x.shape

    def body(x_ref, out_ref, send_sems, recv_sems, local_sem):
        ix, iy, ic = lax.axis_index("x"), lax.axis_index("y"), lax.axis_index("c")
        me, sibling = (ix, iy, ic), (ix, iy, 1 - ic)
        chips = [(1 - ix, iy), (ix, 1 - iy), (1 - ix, 1 - iy)]

        def slab(px, py, pc):
            return out_ref.at[4 * px + 2 * py + pc]

        def copy(k, block, to, src=None):
            return pltpu.make_async_remote_copy(
                src_ref=slab(*block) if src is None else src, dst_ref=slab(*block),
                send_sem=send_sems.at[k], recv_sem=recv_sems.at[k], device_id=to, device_id_type=MESH)

        mine = pltpu.make_async_copy(x_ref, slab(*me), local_sem)
        mine.start()
        first = [copy(0, me, sibling, src=x_ref)]
        first += [copy(1 + j, me, (*chip, ic), src=x_ref) for j, chip in enumerate(chips)]
        for cp in first:
            cp.start()
        passed = [copy(4 + j, (*chip, ic), sibling) for j, chip in enumerate(chips)]
        for j, chip in enumerate(chips):
            copy(1 + j, (*chip, ic), me).wait_recv()
            passed[j].start()
        copy(0, sibling, me).wait_recv()
        for j, chip in enumerate(chips):
            copy(4 + j, (*chip, 1 - ic), me).wait_recv()
        for cp in first + passed:
            cp.wait_send()
        mine.wait()

    return pl.pallas_call(
        body, name=name,
        out_shape=jax.ShapeDtypeStruct((NDEV, m, n), x.dtype),
        in_specs=[pl.BlockSpec(memory_space=pl.ANY)],
        out_specs=pl.BlockSpec(memory_space=pl.ANY),
        scratch_shapes=[pltpu.SemaphoreType.DMA((7,)), pltpu.SemaphoreType.DMA((7,)), pltpu.SemaphoreType.DMA(())],
    )(x)


def _exchange(p, name):
    _, m, n = p.shape

    def body(p_ref, r_ref, send_sems, recv_sems, local_sem):
        ix, iy, ic = lax.axis_index("x"), lax.axis_index("y"), lax.axis_index("c")
        me = 4 * ix + 2 * iy + ic
        own = pltpu.make_async_copy(p_ref.at[me], r_ref.at[me], local_sem)
        own.start()
        copies = []
        for k in range(1, NDEV):
            px = 1 - ix if k & 4 else ix
            py = 1 - iy if k & 2 else iy
            pc = 1 - ic if k & 1 else ic
            cp = pltpu.make_async_remote_copy(
                src_ref=p_ref.at[4 * px + 2 * py + pc], dst_ref=r_ref.at[me],
                send_sem=send_sems.at[k - 1], recv_sem=recv_sems.at[k - 1],
                device_id=(px, py, pc), device_id_type=MESH)
            cp.start()
            copies.append(cp)
        for cp in copies:
            cp.wait()
        own.wait()

    return pl.pallas_call(
        body, name=name,
        out_shape=jax.ShapeDtypeStruct(p.shape, p.dtype),
        in_specs=[pl.BlockSpec(memory_space=pl.ANY)],
        out_specs=pl.BlockSpec(memory_space=pl.ANY),
        scratch_shapes=[pltpu.SemaphoreType.DMA((7,)), pltpu.SemaphoreType.DMA((7,)), pltpu.SemaphoreType.DMA(())],
    )(p)


def _sum_leading(r, name):
    k, m, n = r.shape
    tm = _pick(m, max(8, (4 * 1024 * 1024) // (k * n * 4) // 8 * 8), 8)

    def body(r_ref, o_ref):
        acc = r_ref[0]
        for s in range(1, k):
            acc = acc + r_ref[s]
        o_ref[...] = acc

    return pl.pallas_call(
        body, name=name, grid=(m // tm,),
        in_specs=[pl.BlockSpec((k, tm, n), lambda i: (0, i, 0))],
        out_specs=pl.BlockSpec((tm, n), lambda i: (i, 0)),
        out_shape=jax.ShapeDtypeStruct((m, n), F32),
        compiler_params=_params("parallel"),
    )(r)


def _mm_nt(a, bt, out_dtype, name):
    M, K = a.shape
    N = bt.shape[0]
    tm, tn = _pick(M, 512, 8), _pick(N, 1536, LANES)

    def body(a_ref, b_ref, o_ref):
        o_ref[...] = _dot(a_ref[...], b_ref[...], NT).astype(out_dtype)

    return pl.pallas_call(
        body, name=name, grid=(M // tm, N // tn),
        in_specs=[pl.BlockSpec((tm, K), lambda i, j: (i, 0)), pl.BlockSpec((tn, K), lambda i, j: (j, 0))],
        out_specs=pl.BlockSpec((tm, tn), lambda i, j: (i, j)),
        out_shape=jax.ShapeDtypeStruct((M, N), out_dtype),
        compiler_params=_params("parallel", "parallel"),
    )(a, bt)


def _mm_nn(pairs, name):
    M = pairs[0][0].shape[0]
    N = pairs[0][1].shape[1]
    tm, tn = _pick(M, 512, 8), _pick(N, 1024, LANES)
    np_ = len(pairs)

    def body(*refs):
        o_ref = refs[-1]
        acc = _dot(refs[0][...], refs[1][...], NN)
        for p in range(1, np_):
            acc = acc + _dot(refs[2 * p][...], refs[2 * p + 1][...], NN)
        o_ref[...] = acc

    in_specs, args = [], []
    for a, b in pairs:
        K = a.shape[1]
        in_specs += [pl.BlockSpec((tm, K), lambda i, j: (i, 0)), pl.BlockSpec((K, tn), lambda i, j: (0, j))]
        args += [a, b]
    return pl.pallas_call(
        body, name=name, grid=(M // tm, N // tn),
        in_specs=in_specs,
        out_specs=pl.BlockSpec((tm, tn), lambda i, j: (i, j)),
        out_shape=jax.ShapeDtypeStruct((M, N), F32),
        compiler_params=_params("parallel", "parallel"),
    )(*args)


def _mm_tn(a, b, name):
    M, N1 = a.shape
    N2 = b.shape[1]
    t1, tk = _pick(N1, 1536, LANES), _pick(M, 512, 8)

    def body(a_ref, b_ref, o_ref):
        @pl.when(pl.program_id(1) == 0)
        def _():
            o_ref[...] = jnp.zeros_like(o_ref)
        o_ref[...] += _dot(a_ref[...], b_ref[...], TN)

    return pl.pallas_call(
        body, name=name, grid=(N1 // t1, M // tk),
        in_specs=[pl.BlockSpec((tk, t1), lambda i, k: (k, i)), pl.BlockSpec((tk, N2), lambda i, k: (k, 0))],
        out_specs=pl.BlockSpec((t1, N2), lambda i, k: (i, 0)),
        out_shape=jax.ShapeDtypeStruct((N1, N2), F32),
        compiler_params=_params("parallel", "arbitrary"),
    )(a, b)


def _mm_res(a, w, x, gate, S, name):
    T, K = a.shape
    D = w.shape[1]
    tm, tn = _pick(S, 512, 8), _pick(D, 512, LANES)
    nb = S // tm

    def body(a_ref, w_ref, x_ref, g_ref, y_ref, o_ref):
        y = _dot(a_ref[...], w_ref[...], NN)
        y_ref[...] = y
        o_ref[...] = x_ref[...] + g_ref[0] * y

    return pl.pallas_call(
        body, name=name, grid=(T // tm, D // tn),
        in_specs=[pl.BlockSpec((tm, K), lambda i, j: (i, 0)), pl.BlockSpec((K, tn), lambda i, j: (0, j)),
                  pl.BlockSpec((tm, tn), lambda i, j: (i, j)), pl.BlockSpec((1, 1, tn), lambda i, j: (i // nb, 0, j))],
        out_specs=[pl.BlockSpec((tm, tn), lambda i, j: (i, j)), pl.BlockSpec((tm, tn), lambda i, j: (i, j))],
        out_shape=[jax.ShapeDtypeStruct((T, D), F32), jax.ShapeDtypeStruct((T, D), F32)],
        compiler_params=_params("parallel", "parallel"),
    )(a, w, x, gate)


def _swiglu_fwd(h, wgt, wut, name):
    T, D = h.shape
    F = wgt.shape[0]
    tm, tn = _pick(T, 512, 8), _pick(F, 1536, LANES)

    def body(h_ref, g_ref, u_ref, go_ref, uo_ref, a_ref):
        hh = h_ref[...]
        g = _dot(hh, g_ref[...], NT)
        u = _dot(hh, u_ref[...], NT)
        go_ref[...] = g
        uo_ref[...] = u
        a_ref[...] = (g * jax.nn.sigmoid(g) * u).astype(BF16)

    spec_w = pl.BlockSpec((tn, D), lambda i, j: (j, 0))
    spec_o = pl.BlockSpec((tm, tn), lambda i, j: (i, j))
    return pl.pallas_call(
        body, name=name, grid=(T // tm, F // tn),
        in_specs=[pl.BlockSpec((tm, D), lambda i, j: (i, 0)), spec_w, spec_w],
        out_specs=[spec_o, spec_o, spec_o],
        out_shape=[jax.ShapeDtypeStruct((T, F), F32), jax.ShapeDtypeStruct((T, F), F32),
                   jax.ShapeDtypeStruct((T, F), BF16)],
        compiler_params=_params("parallel", "parallel"),
    )(h, wgt, wut)


def _swiglu_bwd(dy, wd, gate, up, name):
    T, D = dy.shape
    F = wd.shape[0]
    tm, tn = _pick(T, 512, 8), _pick(F, 1536, LANES)

    def body(dy_ref, w_ref, g_ref, u_ref, dg_ref, du_ref):
        da = _dot(dy_ref[...], w_ref[...], NT)
        g = g_ref[...]
        sg = jax.nn.sigmoid(g)
        silu = g * sg
        du_ref[...] = (da * silu).astype(BF16)
        dg_ref[...] = (da * u_ref[...] * (sg + silu * (1.0 - sg))).astype(BF16)

    spec_o = pl.BlockSpec((tm, tn), lambda i, j: (i, j))
    return pl.pallas_call(
        body, name=name, grid=(T // tm, F // tn),
        in_specs=[pl.BlockSpec((tm, D), lambda i, j: (i, 0)), pl.BlockSpec((tn, D), lambda i, j: (j, 0)), spec_o, spec_o],
        out_specs=[spec_o, spec_o],
        out_shape=[jax.ShapeDtypeStruct((T, F), BF16), jax.ShapeDtypeStruct((T, F), BF16)],
        compiler_params=_params("parallel", "parallel"),
    )(dy, wd, gate, up)


def _norm_mod(x, gain, sc, sh, S, name):
    T, D = x.shape
    tm = _pick(S, 512, 8)
    nb = S // tm

    def body(x_ref, g_ref, sc_ref, sh_ref, o_ref):
        xv = x_ref[...]
        r = lax.rsqrt(jnp.mean(xv * xv, axis=-1, keepdims=True) + EPS)
        o_ref[...] = ((xv * r) * g_ref[...] * (1.0 + sc_ref[0]) + sh_ref[0]).astype(BF16)

    spec_b = pl.BlockSpec((1, 1, D), lambda i: (i // nb, 0, 0))
    return pl.pallas_call(
        body, name=name, grid=(T // tm,),
        in_specs=[pl.BlockSpec((tm, D), lambda i: (i, 0)), pl.BlockSpec((1, D), lambda i: (0, 0)), spec_b, spec_b],
        out_specs=pl.BlockSpec((tm, D), lambda i: (i, 0)),
        out_shape=jax.ShapeDtypeStruct((T, D), BF16),
        compiler_params=_params("parallel"),
    )(x, gain, sc, sh)


def _norm_mod_bwd(x, dh, dres, gain, sc, S, name):
    T, D = x.shape
    B = T // S
    tm = _pick(S, 256, 8)
    nb = S // tm

    def body(x_ref, dh_ref, dr_ref, g_ref, sc_ref, o_ref, dsh_ref, dsc_ref, dg_ref):
        @pl.when(pl.program_id(1) == 0)
        def _():
            dsh_ref[...] = jnp.zeros_like(dsh_ref)
            dsc_ref[...] = jnp.zeros_like(dsc_ref)
            dg_ref[...] = jnp.zeros_like(dg_ref)
        xv, dhv, g = x_ref[...], dh_ref[...], g_ref[...]
        r = lax.rsqrt(jnp.mean(xv * xv, axis=-1, keepdims=True) + EPS)
        xhat = xv * r
        dsh_ref[0] += jnp.sum(dhv, axis=0, keepdims=True)
        dsc_ref[0] += jnp.sum(dhv * (xhat * g), axis=0, keepdims=True)
        dn = dhv * (1.0 + sc_ref[0])
        dg_ref[0] += jnp.sum(dn * xhat, axis=0, keepdims=True)
        dxh = dn * g
        o_ref[...] = dr_ref[...] + r * (dxh - xhat * jnp.mean(dxh * xhat, axis=-1, keepdims=True))

    spec_t = pl.BlockSpec((tm, D), lambda b, i: (b * nb + i, 0))
    spec_b = pl.BlockSpec((1, 1, D), lambda b, i: (b, 0, 0))
    red = jax.ShapeDtypeStruct((B, 1, D), F32)
    return pl.pallas_call(
        body, name=name, grid=(B, nb),
        in_specs=[spec_t, spec_t, spec_t, pl.BlockSpec((1, D), lambda b, i: (0, 0)), spec_b],
        out_specs=[spec_t, spec_b, spec_b, spec_b],
        out_shape=[jax.ShapeDtypeStruct((T, D), F32), red, red, red],
        compiler_params=_params("parallel", "arbitrary"),
    )(x, dh, dres, gain, sc)


def _gate_bwd(dx, y, gate, S, name):
    T, D = dx.shape
    B = T // S
    tm = _pick(S, 512, 8)
    nb = S // tm

    def body(dx_ref, y_ref, g_ref, dy_ref, dg_ref):
        @pl.when(pl.program_id(1) == 0)
        def _():
            dg_ref[...] = jnp.zeros_like(dg_ref)
        d = dx_ref[...]
        dy_ref[...] = (d * g_ref[0]).astype(BF16)
        dg_ref[0] += jnp.sum(d * y_ref[...], axis=0, keepdims=True)

    spec_t = pl.BlockSpec((tm, D), lambda b, i: (b * nb + i, 0))
    spec_b = pl.BlockSpec((1, 1, D), lambda b, i: (b, 0, 0))
    return pl.pallas_call(
        body, name=name, grid=(B, nb),
        in_specs=[spec_t, spec_t, spec_b],
        out_specs=[spec_t, spec_b],
        out_shape=[jax.ShapeDtypeStruct((T, D), BF16), jax.ShapeDtypeStruct((B, 1, D), F32)],
        compiler_params=_params("parallel", "arbitrary"),
    )(dx, y, gate)


def _loss_head(y, target, name):
    T, D = y.shape
    tm = _pick(T, 512, 8)

    def body(y_ref, t_ref, dy_ref, l_ref):
        @pl.when(pl.program_id(0) == 0)
        def _():
            l_ref[...] = jnp.zeros_like(l_ref)
        e = y_ref[...] - t_ref[...]
        dy_ref[...] = e * (1.0 / D)
        l_ref[...] += 0.5 * jnp.sum(jnp.mean(e * e, axis=-1, keepdims=True), axis=0, keepdims=True)

    spec = pl.BlockSpec((tm, D), lambda i: (i, 0))
    return pl.pallas_call(
        body, name=name, grid=(T // tm,),
        in_specs=[spec, spec],
        out_specs=[spec, pl.BlockSpec((8, LANES), lambda i: (0, 0))],
        out_shape=[jax.ShapeDtypeStruct((T, D), F32), jax.ShapeDtypeStruct((8, LANES), F32)],
        compiler_params=_params("arbitrary"),
    )(y, target)


def _ada_fwd(c_all, ada_w, bias, name):
    NB, D = c_all.shape
    L, _, W = ada_w.shape

    def body(c_ref, w_ref, b_ref, o_ref):
        cv = c_ref[...]
        cond = cv * jax.nn.sigmoid(cv)
        o_ref[0] = _dot(cond, w_ref[0], NN, HIGH) + b_ref[0]

    return pl.pallas_call(
        body, name=name, grid=(L,),
        in_specs=[pl.BlockSpec((NB, D), lambda l: (0, 0)), pl.BlockSpec((1, D, W), lambda l: (l, 0, 0)),
                  pl.BlockSpec((1, 1, W), lambda l: (l, 0, 0))],
        out_specs=pl.BlockSpec((1, NB, W), lambda l: (l, 0, 0)),
        out_shape=jax.ShapeDtypeStruct((L, NB, W), F32),
        compiler_params=_params("parallel"),
    )(c_all, ada_w, bias)


def _ada_bwd(c_all, dmod, name):
    NB, D = c_all.shape
    L, _, W = dmod.shape

    def body(c_ref, d_ref, o_ref):
        cv = c_ref[...]
        cond = cv * jax.nn.sigmoid(cv)
        o_ref[0] = _dot(cond, d_ref[0], TN, HIGH)

    return pl.pallas_call(
        body, name=name, grid=(L,),
        in_specs=[pl.BlockSpec((NB, D), lambda l: (0, 0)), pl.BlockSpec((1, NB, W), lambda l: (l, 0, 0))],
        out_specs=pl.BlockSpec((1, D, W), lambda l: (l, 0, 0)),
        out_shape=jax.ShapeDtypeStruct((L, D, W), F32),
        compiler_params=_params("parallel"),
    )(c_all, dmod)


def _lo_mask(shape):
    return lax.broadcasted_iota(jnp.int32, shape, len(shape) - 1) < HEAD


def _head_sum_matrix():
    r = lax.broadcasted_iota(jnp.int32, (LANES, LANES), 0) // HEAD
    c = lax.broadcasted_iota(jnp.int32, (LANES, LANES), 1) // HEAD
    return (r == c).astype(F32)


def _rope(y, cs, s1, s2):
    return y * cs + pltpu.roll(y, LANES - ROT // 2, 1) * s1 + pltpu.roll(y, ROT // 2, 1) * s2


def _rope_bwd(d, cs, s1, s2):
    return d * cs + pltpu.roll(d * s1, ROT // 2, 1) + pltpu.roll(d * s2, LANES - ROT // 2, 1)


def _qk_prep(qkv, cs, s1, s2, qg, kg, name):
    T, W = qkv.shape
    NQ = W - 2 * LANES
    tm = _pick(T, 512, 8)

    def body(x_ref, cs_ref, s1_ref, s2_ref, qg_ref, kg_ref, q_ref, k_ref, v_ref):
        P = _head_sum_matrix()
        cs_, s1_, s2_ = cs_ref[...], s1_ref[...], s2_ref[...]
        lo = _lo_mask((tm, LANES))

        def norm_rope(xv, g):
            ms = _dot(xv * xv, P, NN, HIGH) * (1.0 / HEAD)
            return _rope(xv * lax.rsqrt(ms + EPS) * g, cs_, s1_, s2_)

        for j in range(NQ // LANES):
            q_ref[:, j * LANES:(j + 1) * LANES] = norm_rope(x_ref[:, j * LANES:(j + 1) * LANES], qg_ref[...]).astype(BF16)
        kr = norm_rope(x_ref[:, NQ:NQ + LANES], kg_ref[...])
        ks = pltpu.roll(kr, HEAD, 1)
        k_ref[:, :LANES] = jnp.where(lo, kr, ks).astype(BF16)
        k_ref[:, LANES:] = jnp.where(lo, ks, kr).astype(BF16)
        vr = x_ref[:, NQ + LANES:]
        vs = pltpu.roll(vr, HEAD, 1)
        v_ref[:, :LANES] = jnp.where(lo, vr, vs).astype(BF16)
        v_ref[:, LANES:] = jnp.where(lo, vs, vr).astype(BF16)

    spec_t = pl.BlockSpec((tm, LANES), lambda i: (i, 0))
    spec_g = pl.BlockSpec((1, LANES), lambda i: (0, 0))
    return pl.pallas_call(
        body, name=name, grid=(T // tm,),
        in_specs=[pl.BlockSpec((tm, W), lambda i: (i, 0)), spec_t, spec_t, spec_t, spec_g, spec_g],
        out_specs=[pl.BlockSpec((tm, NQ), lambda i: (i, 0)), pl.BlockSpec((tm, 2 * LANES), lambda i: (i, 0)),
                   pl.BlockSpec((tm, 2 * LANES), lambda i: (i, 0))],
        out_shape=[jax.ShapeDtypeStruct((T, NQ), BF16), jax.ShapeDtypeStruct((T, 2 * LANES), BF16),
                   jax.ShapeDtypeStruct((T, 2 * LANES), BF16)],
        compiler_params=_params("parallel"),
    )(qkv, cs, s1, s2, qg, kg)


def _stack_heads(x2):
    lo = _lo_mask(x2.shape)
    z = jnp.zeros_like(x2)
    return jnp.concatenate([jnp.where(lo, x2, z), jnp.where(lo, z, x2)], axis=0)


def _unstack_heads(xs):
    r = xs.shape[0] // 2
    return jnp.where(_lo_mask((r, LANES)), xs[:r], xs[r:])


def _swa_scores(q2, kp, kc, sink2, i):
    qs = _stack_heads(q2) * SCALE
    kk = jnp.concatenate([kp, kc], axis=0)
    s = _dot(qs, kk, NT)
    qo = lax.broadcasted_iota(jnp.int32, s.shape, 0) % BLK
    kc_ = lax.broadcasted_iota(jnp.int32, s.shape, 1)
    rel = qo + BLK - kc_
    valid = (rel >= 0) & (rel < BLK) & ((kc_ >= BLK) | (i > 0))
    sk = jnp.concatenate([jnp.broadcast_to(sink2[:, 0:1], (BLK, 1)), jnp.broadcast_to(sink2[:, HEAD:HEAD + 1], (BLK, 1))], axis=0)
    return qs, kk, jnp.where(valid, s, NEG), valid, sk


def _swa_fwd(q, kd, vd, sink2, B, name):
    T, NQ = q.shape
    NP = NQ // LANES
    nq = T // B // BLK
    grp = NP // (kd.shape[1] // LANES)

    def body(q_ref, kp_ref, kc_ref, vp_ref, vc_ref, s_ref, o_ref, l_ref):
        i = pl.program_id(2)
        _, _, s, valid, sk = _swa_scores(q_ref[...], kp_ref[...], kc_ref[...], s_ref[0], i)
        m = jnp.maximum(jnp.max(s, axis=1, keepdims=True), sk)
        p = jnp.where(valid, jnp.exp(s - m), 0.0)
        l = jnp.sum(p, axis=1, keepdims=True) + jnp.exp(sk - m)
        p = (p / l).astype(BF16)
        p2 = jnp.concatenate([p[:BLK], p[BLK:]], axis=1)
        vv = jnp.concatenate([vp_ref[...], vc_ref[...]], axis=0)
        o_ref[...] = _dot(p2, _stack_heads(vv), NN).astype(BF16)
        l_ref[...] = _unstack_heads(jnp.broadcast_to(m + jnp.log(l), (2 * BLK, LANES)))

    spec_q = pl.BlockSpec((BLK, LANES), lambda b, j, i: (b * nq + i, j))
    spec_p = pl.BlockSpec((BLK, LANES), lambda b, j, i: (b * nq + jnp.maximum(i - 1, 0), j // grp))
    spec_c = pl.BlockSpec((BLK, LANES), lambda b, j, i: (b * nq + i, j // grp))
    return pl.pallas_call(
        body, name=name, grid=(B, NP, nq),
        in_specs=[spec_q, spec_p, spec_c, spec_p, spec_c, pl.BlockSpec((1, 1, LANES), lambda b, j, i: (j, 0, 0))],
        out_specs=[spec_q, spec_q],
        out_shape=[jax.ShapeDtypeStruct((T, NQ), BF16), jax.ShapeDtypeStruct((T, NQ), F32)],
        compiler_params=_params("parallel", "parallel", "parallel"),
    )(q, kd, kd, vd, vd, sink2)


def _swa_bwd(q, kd, vd, sink2, do, lse, B, name):
    T, NQ = q.shape
    NP = NQ // LANES
    nq = T // B // BLK
    grp = NP // (kd.shape[1] // LANES)

    def body(q_ref, kp_ref, kc_ref, vp_ref, vc_ref, s_ref, do_ref, l_ref,
             dq_ref, dkc_ref, dkp_ref, dvc_ref, dvp_ref, ds_ref):
        b, i = pl.program_id(1), pl.program_id(2)

        @pl.when((b == 0) & (i == 0))
        def _():
            ds_ref[...] = jnp.zeros_like(ds_ref)
        qs, kk, s, valid, sk = _swa_scores(q_ref[...], kp_ref[...], kc_ref[...], s_ref[0], i)
        lse_ = l_ref[...]
        lse_s = jnp.concatenate([lse_[:, 0:1], lse_[:, HEAD:HEAD + 1]], axis=0)
        p = jnp.where(valid, jnp.exp(s - lse_s), 0.0)
        dos = _stack_heads(do_ref[...])
        vv = jnp.concatenate([vp_ref[...], vc_ref[...]], axis=0)
        dp = _dot(dos, vv, NT)
        delta = jnp.sum(p * dp, axis=1, keepdims=True)
        dsc = (p * (dp - delta)).astype(BF16)
        dsk = -jnp.exp(sk - lse_s) * delta
        dsk_lo = jnp.sum(dsk[:BLK], axis=0, keepdims=True)
        dsk_hi = jnp.sum(dsk[BLK:], axis=0, keepdims=True)
        ds_ref[0] += jnp.where(_lo_mask((1, LANES)), dsk_lo, dsk_hi)
        dq_ref[...] = _unstack_heads(_dot(dsc, kk, NN)) * SCALE
        dk = _dot(dsc, qs, TN)
        dv = _dot(p.astype(BF16), dos, TN)
        dkp_ref[...] = dk[:BLK]
        dkc_ref[...] = dk[BLK:]
        dvp_ref[...] = dv[:BLK]
        dvc_ref[...] = dv[BLK:]

    spec_q = pl.BlockSpec((BLK, LANES), lambda j, b, i: (b * nq + i, j))
    spec_p = pl.BlockSpec((BLK, LANES), lambda j, b, i: (b * nq + jnp.maximum(i - 1, 0), j // grp))
    spec_c = pl.BlockSpec((BLK, LANES), lambda j, b, i: (b * nq + i, j // grp))
    spec_s = pl.BlockSpec((1, 1, LANES), lambda j, b, i: (j, 0, 0))
    big = jax.ShapeDtypeStruct((T, NQ), F32)
    return pl.pallas_call(
        body, name=name, grid=(NP, B, nq),
        in_specs=[spec_q, spec_p, spec_c, spec_p, spec_c, spec_s, spec_q, spec_q],
        out_specs=[spec_q, spec_q, spec_q, spec_q, spec_q, spec_s],
        out_shape=[big, big, big, big, big, jax.ShapeDtypeStruct((NP, 1, LANES), F32)],
        compiler_params=_params("arbitrary", "arbitrary", "arbitrary"),
    )(q, kd, kd, vd, vd, sink2, do, lse)


def _qk_prep_bwd(qkv, cs, s1, s2, qg, kg, dq, dkc, dkp, dvc, dvp, B, name):
    T, W = qkv.shape
    NQ = W - 2 * LANES
    NP = NQ // LANES
    nq = T // B // BLK
    grp = NP // 2

    def body(x_ref, cs_ref, s1_ref, s2_ref, qg_ref, kg_ref, dq_ref, dkc_ref, dkp_ref, dvc_ref, dvp_ref,
             o_ref, dqg_ref, dkg_ref):
        b, i = pl.program_id(0), pl.program_id(1)

        @pl.when((b == 0) & (i == 0))
        def _():
            dqg_ref[...] = jnp.zeros_like(dqg_ref)
            dkg_ref[...] = jnp.zeros_like(dkg_ref)
        P = _head_sum_matrix()
        cs_, s1_, s2_ = cs_ref[...], s1_ref[...], s2_ref[...]
        lo = _lo_mask((BLK, LANES))
        has_next = (i + 1 < nq).astype(F32)

        def norm_rope_bwd(xv, g, d):
            du = _rope_bwd(d, cs_, s1_, s2_)
            r = lax.rsqrt(_dot(xv * xv, P, NN, HIGH) * (1.0 / HEAD) + EPS)
            xhat = xv * r
            dgain = jnp.sum(du * xhat, axis=0, keepdims=True)
            uu = du * g
            dx = r * (uu - xhat * (_dot(uu * xhat, P, NN, HIGH) * (1.0 / HEAD)))
            return dx, dgain + pltpu.roll(dgain, HEAD, 1)

        dqg = jnp.zeros((1, LANES), F32)
        for j in range(NP):
            sl = slice(j * LANES, (j + 1) * LANES)
            dx, dg = norm_rope_bwd(x_ref[:, sl], qg_ref[...], dq_ref[:, sl])
            o_ref[:, sl] = dx.astype(BF16)
            dqg = dqg + dg
        dqg_ref[...] += dqg

        def fold(c_ref, p_ref, g):
            t = jnp.zeros((BLK, LANES), F32)
            for j in range(g * grp, (g + 1) * grp):
                sl = slice(j * LANES, (j + 1) * LANES)
                t = t + c_ref[:, sl] + has_next * p_ref[:, sl]
            return t + pltpu.roll(t, HEAD, 1)

        dk = jnp.where(lo, fold(dkc_ref, dkp_ref, 0), fold(dkc_ref, dkp_ref, 1))
        dx, dg = norm_rope_bwd(x_ref[:, NQ:NQ + LANES], kg_ref[...], dk)
        o_ref[:, NQ:NQ + LANES] = dx.astype(BF16)
        dkg_ref[...] += dg
        dv = jnp.where(lo, fold(dvc_ref, dvp_ref, 0), fold(dvc_ref, dvp_ref, 1))
        o_ref[:, NQ + LANES:] = dv.astype(BF16)

    spec_t = pl.BlockSpec((BLK, LANES), lambda b, i: (b * nq + i, 0))
    spec_g = pl.BlockSpec((1, LANES), lambda b, i: (0, 0))
    spec_c = pl.BlockSpec((BLK, NQ), lambda b, i: (b * nq + i, 0))
    spec_n = pl.BlockSpec((BLK, NQ), lambda b, i: (b * nq + jnp.minimum(i + 1, nq - 1), 0))
    row = jax.ShapeDtypeStruct((1, LANES), F32)
    return pl.pallas_call(
        body, name=name, grid=(B, nq),
        in_specs=[pl.BlockSpec((BLK, W), lambda b, i: (b * nq + i, 0)), spec_t, spec_t, spec_t, spec_g, spec_g,
                  spec_c, spec_c, spec_n, spec_c, spec_n],
        out_specs=[pl.BlockSpec((BLK, W), lambda b, i: (b * nq + i, 0)), spec_g, spec_g],
        out_shape=[jax.ShapeDtypeStruct((T, W), BF16), row, row],
        compiler_params=_params("arbitrary", "arbitrary"),
    )(qkv, cs, s1, s2, qg, kg, dq, dkc, dkp, dvc, dvp)


def _split_bf16(x):
    hi = x.astype(BF16)
    return hi, (x - hi.astype(F32)).astype(BF16)


SB_TILE = 256


def _sb_terms(qs, kj, diagonal):
    z = _dot(qs, kj, NT)
    e = jnp.exp(-jnp.abs(z))
    lb = jnp.minimum(z, 0.0) - jnp.log(1.0 + e)
    L = lb - z
    if not diagonal:
        return lb, L, None, z, e
    tq = z.shape[0] // 2
    row = lax.broadcasted_iota(jnp.int32, z.shape, 0)
    row = jnp.where(row >= tq, row - tq, row)
    strict = lax.broadcasted_iota(jnp.int32, z.shape, 1) < row
    return lb, jnp.where(strict, L, 0.0), strict, z, e


def _tri(n, cmp):
    r = lax.broadcasted_iota(jnp.int32, (n, n), 0)
    c = lax.broadcasted_iota(jnp.int32, (n, n), 1)
    return cmp(r, c).astype(BF16)


def _sb_fwd(qkv, B, name):
    T, W = qkv.shape
    NQ = W // 3
    NP = NQ // LANES
    S = T // B
    tq = min(SB_TILE, S)
    nq = S // tq

    def body(q_ref, k_ref, v_ref, o_ref, t_ref):
        i = pl.program_id(2)
        qs = _stack_heads(q_ref[...]) * SCALE
        U = _tri(tq, lambda r, c: r > c)

        def tile(j, c, acc, diagonal):
            rows = pl.ds(pl.multiple_of(j * tq, tq), tq)
            lb, L, strict, _, _ = _sb_terms(qs, k_ref[rows, :], diagonal)
            Lh, Ll = _split_bf16(L)
            a = jnp.exp(lb + (_dot(Lh, U, NN) + _dot(Ll, U, NN) + c))
            if diagonal:
                a = jnp.where(strict, a, 0.0)
            a = a.astype(BF16)
            a2 = jnp.concatenate([a[:tq], a[tq:]], axis=1)
            acc = acc + _dot(a2, _stack_heads(v_ref[rows, :]), NN)
            return c + jnp.sum(L, axis=1, keepdims=True), acc

        carry = tile(i, jnp.zeros((2 * tq, 1), F32), jnp.zeros((tq, LANES), F32), True)
        c, acc = lax.fori_loop(0, i, lambda n, cr: tile(i - 1 - n, cr[0], cr[1], False), carry)
        o_ref[...] = acc.astype(BF16)
        t_ref[...] = _unstack_heads(jnp.broadcast_to(c, (2 * tq, LANES)))

    spec_q = pl.BlockSpec((tq, LANES), lambda b, p, i: (b * nq + i, p))
    return pl.pallas_call(
        body, name=name, grid=(B, NP, nq),
        in_specs=[spec_q, pl.BlockSpec((S, LANES), lambda b, p, i: (b, NP + p)),
                  pl.BlockSpec((S, LANES), lambda b, p, i: (b, 2 * NP + p))],
        out_specs=[spec_q, spec_q],
        out_shape=[jax.ShapeDtypeStruct((T, NQ), BF16), jax.ShapeDtypeStruct((T, NQ), F32)],
        compiler_params=_params("parallel", "parallel", "arbitrary"),
    )(qkv, qkv, qkv)


def _sb_bwd(qkv, do, tot, B, name):
    T, W = qkv.shape
    NQ = W // 3
    NP = NQ // LANES
    S = T // B
    tq = min(SB_TILE, S)
    nq = S // tq

    def body(q_ref, k_ref, v_ref, do_ref, t_ref, dq_ref, dk_ref, dv_ref):
        i = pl.program_id(2)

        @pl.when(i == 0)
        def _():
            dk_ref[...] = jnp.zeros_like(dk_ref)
            dv_ref[...] = jnp.zeros_like(dv_ref)
        qs = _stack_heads(q_ref[...]) * SCALE
        dos = _stack_heads(do_ref[...])
        tt = t_ref[...]
        tot_s = jnp.concatenate([tt[:, 0:1], tt[:, HEAD:HEAD + 1]], axis=0)
        Uinc = _tri(tq, lambda r, c: r <= c)
        Uexc = _tri(tq, lambda r, c: r < c)

        def tile(j, cc, cg, dq, diagonal):
            rows = pl.ds(pl.multiple_of(j * tq, tq), tq)
            kj, vj = k_ref[rows, :], v_ref[rows, :]
            lb, L, strict, z, e = _sb_terms(qs, kj, diagonal)
            Lh, Ll = _split_bf16(L)
            csum = _dot(Lh, Uinc, NN) + _dot(Ll, Uinc, NN) + cc
            a = jnp.exp(lb + (tot_s - csum))
            if diagonal:
                a = jnp.where(strict, a, 0.0)
            g = a * _dot(dos, vj, NT)
            gh, gl = _split_bf16(g)
            G = _dot(gh, Uexc, NN) + _dot(gl, Uexc, NN) + cg
            rcp = pl.reciprocal(1.0 + e, approx=True)
            pos = z >= 0.0
            beta = jnp.where(pos, rcp, e * rcp)
            omb = jnp.where(pos, e * rcp, rcp)
            dz = g * omb - G * beta
            if diagonal:
                dz = jnp.where(strict, dz, 0.0)
            dz = dz.astype(BF16)
            dk_ref[rows, :] += _dot(dz, qs, TN)
            dv_ref[rows, :] += _dot(a.astype(BF16), dos, TN)
            return (cc + jnp.sum(L, axis=1, keepdims=True), cg + jnp.sum(g, axis=1, keepdims=True),
                    dq + _dot(dz, kj, NN))

        zero = jnp.zeros((2 * tq, 1), F32)
        carry = lax.fori_loop(0, i, lambda j, cr: tile(j, cr[0], cr[1], cr[2], False),
                              (zero, zero, jnp.zeros((2 * tq, LANES), F32)))
        _, _, dq = tile(i, carry[0], carry[1], carry[2], True)
        dq_ref[...] = _unstack_heads(dq) * SCALE

    spec_q = pl.BlockSpec((tq, LANES), lambda b, p, i: (b * nq + i, p))
    spec_s = pl.BlockSpec((S, LANES), lambda b, p, i: (b, p))
    big = jax.ShapeDtypeStruct((T, NQ), F32)
    return pl.pallas_call(
        body, name=name, grid=(B, NP, nq),
        in_specs=[spec_q, pl.BlockSpec((S, LANES), lambda b, p, i: (b, NP + p)),
                  pl.BlockSpec((S, LANES), lambda b, p, i: (b, 2 * NP + p)), spec_q, spec_q],
        out_specs=[spec_q, spec_s, spec_s],
        out_shape=[big, big, big],
        compiler_params=_params("parallel", "parallel", "arbitrary"),
    )(qkv, qkv, qkv, do, tot)


def _adamw(w, g, m, v, name):
    shape = w.shape
    cols = shape[-1]
    rows = math.prod(shape[:-1])
    tr = _pick(rows, max(8, (1 << 19) // max(cols, LANES) // 8 * 8), 8)

    def body(w_ref, g_ref, m_ref, v_ref, d_ref, mo_ref, vo_ref):
        gv = g_ref[...]
        mn = ADAM_B1 * m_ref[...] + (1.0 - ADAM_B1) * gv
        vn = ADAM_B2 * v_ref[...] + (1.0 - ADAM_B2) * (gv * gv)
        m_hat = mn / (1.0 - ADAM_B1 ** ADAM_STEP)
        v_hat = vn / (1.0 - ADAM_B2 ** ADAM_STEP)
        d_ref[...] = -ADAM_LR * (m_hat / (jnp.sqrt(v_hat) + ADAM_EPS) + ADAM_WD * w_ref[...])
        mo_ref[...] = mn
        vo_ref[...] = vn

    spec = pl.BlockSpec((tr, cols), lambda i: (i, 0))
    out = jax.ShapeDtypeStruct((rows, cols), F32)
    d, mn, vn = pl.pallas_call(
        body, name=name, grid=(rows // tr,),
        in_specs=[spec] * 4, out_specs=[spec] * 3, out_shape=[out] * 3,
        compiler_params=_params("parallel"),
    )(w.reshape(rows, cols), g.reshape(rows, cols), m.reshape(rows, cols), v.reshape(rows, cols))
    return d.reshape(shape), mn.reshape(shape), vn.reshape(shape)


def _pad_rows(a, rows):
    return jnp.pad(a, ((0, rows - a.shape[0]), (0, 0)))


def kernel(x, c, positions, ada_w, ada_b, norm1_g, norm2_g, wqkv_a, q_norm_a, k_norm_a, sinks_a, wo_a, wqkv_b, wo_b, w_gate, w_up, w_down, loss_target, m_ada_w, m_ada_b, m_norm1_g, m_norm2_g, m_wqkv_a, m_q_norm_a, m_k_norm_a, m_sinks_a, m_wo_a, m_wqkv_b, m_wo_b, m_w_gate, m_w_up, m_w_down, v_ada_w, v_ada_b, v_norm1_g, v_norm2_g, v_wqkv_a, v_q_norm_a, v_k_norm_a, v_sinks_a, v_wo_a, v_wqkv_b, v_wo_b, v_w_gate, v_w_up, v_w_down):
    B, S, D = x.shape
    T = B * S
    L = ada_w.shape[0]
    NA, NB_ = wqkv_a.shape[0], wqkv_b.shape[0]
    me = 4 * lax.axis_index("x") + 2 * lax.axis_index("y") + lax.axis_index("c")
    xt = x.reshape(T, D)

    col_sharded = [(wqkv_a, NA), (wqkv_b, NB_), (w_gate, L), (w_up, L)]
    row_sharded = [(wo_a, NA), (wo_b, NB_), (w_down, L)]
    pieces, layout = [], []
    for w, n in col_sharded:
        for l in range(n):
            pieces.append(w[l].T.astype(BF16))
            layout.append(w.shape[2])
    for w, n in row_sharded:
        for l in range(n):
            pieces.append(w[l].astype(BF16))
            layout.append(w.shape[1])
    packed = jnp.concatenate(pieces, axis=0)
    R = packed.shape[0]
    gathered = _all_gather(packed, "ag_weights")
    full, off = [], 0
    for rows in layout:
        full.append(gathered[:, off:off + rows, :].reshape(NDEV * rows, D))
        off += rows
    it = iter(full)
    wqkv_a_t = [next(it) for _ in range(NA)]
    wqkv_b_t = [next(it) for _ in range(NB_)]
    wg_t = [next(it) for _ in range(L)]
    wu_t = [next(it) for _ in range(L)]
    wo_a_f = [next(it) for _ in range(NA)]
    wo_b_f = [next(it) for _ in range(NB_)]
    wd_f = [next(it) for _ in range(L)]

    WA = ada_w.shape[2]
    c_all = _all_gather(c, "ag_c").reshape(NDEV * B, D)
    bias = lax.dynamic_slice_in_dim(ada_b, me * WA, WA, axis=1).reshape(L, 1, WA)
    mod_part = _ada_fwd(c_all, ada_w, bias, "ada_fwd")
    mod_all = _all_gather(mod_part.reshape(L * NDEV * B, WA), "ag_mod")
    mod_all = mod_all.reshape(NDEV, L, NDEV * B, WA).transpose(1, 2, 0, 3).reshape(L, NDEV * B, NDEV * WA)
    mod = lax.dynamic_slice_in_dim(mod_all, me * B, B, axis=1)
    mod = mod.reshape(L, B, 6, 1, D)
    sh1, sc1, g1, sh2, sc2, g2 = [mod[:, :, k] for k in range(6)]

    half = ROT // 2
    inv_freq = jnp.power(jnp.float32(ROPE_THETA), -jnp.arange(half, dtype=F32) * 2.0 / ROT)
    ang = positions.reshape(T, 1).astype(F32) * inv_freq[None, :]
    cos, sin = jnp.cos(ang), jnp.sin(ang)
    ones = jnp.ones((T, HEAD - ROT), F32)
    zeros = jnp.zeros((T, HEAD - ROT), F32)
    z8 = jnp.zeros((T, half), F32)
    cs = jnp.tile(jnp.concatenate([cos, cos, ones], axis=1), (1, 2))
    s1 = jnp.tile(jnp.concatenate([-sin, z8, zeros], axis=1), (1, 2))
    s2 = jnp.tile(jnp.concatenate([z8, sin, zeros], axis=1), (1, 2))

    saved = []
    xc = xt
    for l in range(L):
        j = l // 2
        h1 = _norm_mod(xc, norm1_g[l:l + 1], sc1[l], sh1[l], S, f"norm1_{l}")
        sv = dict(x_in=xc, h1=h1)
        if l % 2 == 0:
            qkv = _mm_nt(h1, wqkv_a_t[j], F32, f"qkv_a_{l}")
            qg = jnp.tile(q_norm_a[j:j + 1], (1, 2))
            kg = jnp.tile(k_norm_a[j:j + 1], (1, 2))
            qn, kd, vd = _qk_prep(qkv, cs, s1, s2, qg, kg, f"qk_prep_{l}")
            sink2 = jnp.repeat(sinks_a[j].reshape(-1, 2), HEAD, axis=1).reshape(-1, 1, LANES)
            attn, lse = _swa_fwd(qn, kd, vd, sink2, B, f"swa_fwd_{l}")
            sv.update(qkv=qkv, qg=qg, kg=kg, qn=qn, kd=kd, vd=vd, sink2=sink2, lse=lse)
            wo = wo_a_f[j]
        else:
            qkv = _mm_nt(h1, wqkv_b_t[j], BF16, f"qkv_b_{l}")
            attn, tot = _sb_fwd(qkv, B, f"sb_fwd_{l}")
            sv.update(qkv=qkv, tot=tot)
            wo = wo_b_f[j]
        y1, xm = _mm_res(attn, wo, xc, g1[l], S, f"attn_out_{l}")
        h2 = _norm_mod(xm, norm2_g[l:l + 1], sc2[l], sh2[l], S, f"norm2_{l}")
        gate, up, act = _swiglu_fwd(h2, wg_t[l], wu_t[l], f"swiglu_fwd_{l}")
        y2, xc = _mm_res(act, wd_f[l], xm, g2[l], S, f"mlp_out_{l}")
        sv.update(attn=attn, y1=y1, x_mid=xm, h2=h2, gate=gate, up=up, act=act, y2=y2)
        saved.append(sv)

    dx, loss_tile = _loss_head(xc, loss_target.reshape(T, D), "loss_head")

    g_qkv_a, g_qkv_b, g_gate, g_up, g_wo_a, g_wo_b, g_down = ([None] * NA, [None] * NB_, [None] * L, [None] * L,
                                                             [None] * NA, [None] * NB_, [None] * L)
    dmod = [None] * L
    dn1, dn2 = [None] * L, [None] * L
    dqg, dkg, dsink = [None] * NA, [None] * NA, [None] * NA
    for l in reversed(range(L)):
        j = l // 2
        sv = saved[l]
        dy2, dg2 = _gate_bwd(dx, sv["y2"], g2[l], S, f"gate2_bwd_{l}")
        dgate, dup = _swiglu_bwd(dy2, wd_f[l], sv["gate"], sv["up"], f"swiglu_bwd_{l}")
        g_down[l] = _mm_tn(sv["act"], dy2, f"dw_down_{l}")
        dh2 = _mm_nn([(dgate, wg_t[l]), (dup, wu_t[l])], f"dh2_{l}")
        g_gate[l] = _mm_tn(dgate, sv["h2"], f"dw_gate_{l}")
        g_up[l] = _mm_tn(dup, sv["h2"], f"dw_up_{l}")
        dxm, dsh2, dsc2, dn2[l] = _norm_mod_bwd(sv["x_mid"], dh2, dx, norm2_g[l:l + 1], sc2[l], S, f"norm2_bwd_{l}")
        dy1, dg1 = _gate_bwd(dxm, sv["y1"], g1[l], S, f"gate1_bwd_{l}")
        wo = wo_a_f[j] if l % 2 == 0 else wo_b_f[j]
        dattn = _mm_nt(dy1, wo, BF16, f"dattn_{l}")
        gwo = _mm_tn(sv["attn"], dy1, f"dw_o_{l}")
        if l % 2 == 0:
            g_wo_a[j] = gwo
            dq, dkc, dkp, dvc, dvp, dsink[j] = _swa_bwd(sv["qn"], sv["kd"], sv["vd"], sv["sink2"], dattn, sv["lse"], B,
                                                        f"swa_bwd_{l}")
            dqkv, dqg[j], dkg[j] = _qk_prep_bwd(sv["qkv"], cs, s1, s2, sv["qg"], sv["kg"], dq, dkc, dkp, dvc, dvp, B,
                                                f"qk_prep_bwd_{l}")
            wt = wqkv_a_t[j]
        else:
            g_wo_b[j] = gwo
            dq, dk, dv = _sb_bwd(sv["qkv"], dattn, sv["tot"], B, f"sb_bwd_{l}")
            dqkv = jnp.concatenate([dq, dk, dv], axis=1).astype(BF16)
            wt = wqkv_b_t[j]
        dh1 = _mm_nn([(dqkv, wt)], f"dh1_{l}")
        gq = _mm_tn(dqkv, sv["h1"], f"dw_qkv_{l}")
        if l % 2 == 0:
            g_qkv_a[j] = gq
        else:
            g_qkv_b[j] = gq
        dx, dsh1, dsc1, dn1[l] = _norm_mod_bwd(sv["x_in"], dh1, dxm, norm1_g[l:l + 1], sc1[l], S, f"norm1_bwd_{l}")
        dmod[l] = jnp.concatenate([dsh1, dsc1, dg1, dsh2, dsc2, dg2], axis=1)
    grad_x = dx.reshape(B, S, D)

    ndm = L * 6
    dmod_rows = jnp.stack(dmod, axis=1).reshape(B * ndm, D)
    misc = jnp.concatenate(
        [jnp.concatenate(dn1, axis=0).reshape(B * L, D), jnp.concatenate(dn2, axis=0).reshape(B * L, D),
         _pad_rows(jnp.concatenate([jnp.pad(r, ((0, 0), (0, D - LANES))) for r in dqg + dkg]
                                   + [jnp.pad(r[:, 0, ::HEAD].reshape(1, -1), ((0, 0), (0, D - 2 * r.shape[0]))) for r in dsink]
                                   + [jnp.pad(loss_tile[0:1, 0:1], ((0, 0), (0, D - 1)))], axis=0), 8)], axis=0)
    nmisc = misc.shape[0]
    small = _all_gather(jnp.concatenate([dmod_rows, _pad_rows(misc, -(-nmisc // 8) * 8)], axis=0), "ag_small")
    dmod_all = small[:, :B * ndm].reshape(NDEV * B, ndm, D)
    g_ada_b = _sum_leading(dmod_all, "sum_dmod").reshape(L, 6 * D)
    misc_sum = _sum_leading(small[:, B * ndm:], "sum_misc")
    g_n1 = misc_sum[0:B * L].reshape(L, B, D)
    g_n2 = misc_sum[B * L:2 * B * L].reshape(L, B, D)
    g_norm1 = _sum_leading(g_n1.transpose(1, 0, 2), "sum_n1")
    g_norm2 = _sum_leading(g_n2.transpose(1, 0, 2), "sum_n2")
    o = 2 * B * L
    g_qn = misc_sum[o:o + NA, :HEAD]
    g_kn = misc_sum[o + NA:o + 2 * NA, :HEAD]
    nsink = sinks_a.shape[1]
    g_sink = misc_sum[o + 2 * NA:o + 3 * NA, :nsink]
    loss = misc_sum[o + 3 * NA, 0]

    dmod_loc = lax.dynamic_slice_in_dim(dmod_all.reshape(NDEV * B, L, 6 * D), me * WA, WA, axis=2)
    g_ada_w = _ada_bwd(c_all, dmod_loc.transpose(1, 0, 2), "ada_bwd")

    parts = []
    for g in g_qkv_a + g_qkv_b + g_gate + g_up + g_wo_a + g_wo_b + g_down:
        parts.append(g.reshape(NDEV, g.shape[0] // NDEV, D))
    partial = jnp.concatenate(parts, axis=1)
    received = _exchange(partial, "grad_exchange")
    gsum = _sum_leading(received, "grad_sum")
    shards, off = [], 0
    for rows in layout:
        shards.append(gsum[off:off + rows])
        off += rows
    it = iter(shards)
    gw_qkv_a = jnp.stack([next(it).T for _ in range(NA)])
    gw_qkv_b = jnp.stack([next(it).T for _ in range(NB_)])
    gw_gate = jnp.stack([next(it).T for _ in range(L)])
    gw_up = jnp.stack([next(it).T for _ in range(L)])
    gw_wo_a = jnp.stack([next(it) for _ in range(NA)])
    gw_wo_b = jnp.stack([next(it) for _ in range(NB_)])
    gw_down = jnp.stack([next(it) for _ in range(L)])

    grads = [g_ada_w, g_ada_b, g_norm1, g_norm2, gw_qkv_a, g_qn, g_kn, g_sink, gw_wo_a, gw_qkv_b, gw_wo_b,
             gw_gate, gw_up, gw_down]
    ws = [ada_w, ada_b, norm1_g, norm2_g, wqkv_a, q_norm_a, k_norm_a, sinks_a, wo_a, wqkv_b, wo_b, w_gate, w_up, w_down]
    ms = [m_ada_w, m_ada_b, m_norm1_g, m_norm2_g, m_wqkv_a, m_q_norm_a, m_k_norm_a, m_sinks_a, m_wo_a, m_wqkv_b,
          m_wo_b, m_w_gate, m_w_up, m_w_down]
    vs = [v_ada_w, v_ada_b, v_norm1_g, v_norm2_g, v_wqkv_a, v_q_norm_a, v_k_norm_a, v_sinks_a, v_wo_a, v_wqkv_b,
          v_wo_b, v_w_gate, v_w_up, v_w_down]
    deltas, new_m, new_v = [], [], []
    for k, (w, g, m, v) in enumerate(zip(ws, grads, ms, vs)):
        g = g.reshape(w.shape)
        d, mn, vn = _adamw(w, g, m, v, f"adamw_{k}")
        grads[k] = g
        deltas.append(d)
        new_m.append(mn)
        new_v.append(vn)
    return (loss, grad_x, *grads, *deltas, *new_m, *new_v)
```

```python
import functools
import math

import jax
import jax.numpy as jnp
from jax import lax
from jax.experimental import pallas as pl
from jax.experimental.pallas import tpu as pltpu

F32 = jnp.float32
BF16 = jnp.bfloat16
NDEV = 8
HEAD = 64
BLK = 128
LANES = 128
EPS = 1e-6
ROT = HEAD // 4
ROPE_THETA = 500000.0
SCALE = HEAD ** -0.5
NEG = -1e30
VMEM_LIMIT = 56 * 1024 * 1024
MESH = pl.DeviceIdType.MESH
HIGH = lax.Precision.HIGHEST

ADAM_LR = 0.001
ADAM_B1 = 0.9
ADAM_B2 = 0.999
ADAM_EPS = 1e-08
ADAM_WD = 0.01
ADAM_STEP = 10


def _params(*sem):
    return pltpu.CompilerParams(dimension_semantics=sem, vmem_limit_bytes=VMEM_LIMIT)


def _pick(n, cap, mult):
    if n <= cap:
        return n
    best = None
    for t in range(mult, cap + 1, mult):
        if n % t == 0:
            best = t
    assert best is not None, (n, cap, mult)
    return best


def _dot(a, b, dims, precision=None):
    return lax.dot_general(a, b, (dims, ((), ())), preferred_element_type=F32, precision=precision)


NN = ((1,), (0,))
NT = ((1,), (1,))
TN = ((0,), (0,))


def _all_gather(x, name):
    m, n = x.shape

    def body(x_ref, out_ref, send_sems, recv_sems, local_sem):
        ix, iy, ic = lax.axis_index("x"), lax.axis_index("y"), lax.axis_index("c")
        me, sibling = (ix, iy, ic), (ix, iy, 1 - ic)
        chips = [(1 - ix, iy), (ix, 1 - iy), (1 - ix, 1 - iy)]

        def slab(px, py, pc):
            return out_ref.at[4 * px + 2 * py + pc]

        def copy(k, block, to, src=None):
            return pltpu.make_async_remote_copy(
                src_ref=slab(*block) if src is None else src, dst_ref=slab(*block),
                send_sem=send_sems.at[k], recv_sem=recv_sems.at[k], device_id=to, device_id_type=MESH)

        mine = pltpu.make_async_copy(x_ref, slab(*me), local_sem)
        mine.start()
        first = [copy(0, me, sibling, src=x_ref)]
        first += [copy(1 + j, me, (*chip, ic), src=x_ref) for j, chip in enumerate(chips)]
        for cp in first:
            cp.start()
        passed = [copy(4 + j, (*chip, ic), sibling) for j, chip in enumerate(chips)]
        for j, chip in enumerate(chips):
            copy(1 + j, (*chip, ic), me).wait_recv()
            passed[j].start()
        copy(0, sibling, me).wait_recv()
        for j, chip in enumerate(chips):
            copy(4 + j, (*chip, 1 - ic), me).wait_recv()
        for cp in first + passed:
            cp.wait_send()
        mine.wait()

    return pl.pallas_call(
        body, name=name,
        out_shape=jax.ShapeDtypeStruct((NDEV, m, n), x.dtype),
        in_specs=[pl.BlockSpec(memory_space=pl.ANY)],
        out_specs=pl.BlockSpec(memory_space=pl.ANY),
        scratch_shapes=[pltpu.SemaphoreType.DMA((7,)), pltpu.SemaphoreType.DMA((7,)), pltpu.SemaphoreType.DMA(())],
    )(x)


def _exchange(p, name):
    _, m, n = p.shape

    def body(p_ref, r_ref, send_sems, recv_sems, local_sem):
        ix, iy, ic = lax.axis_index("x"), lax.axis_index("y"), lax.axis_index("c")
        me = 4 * ix + 2 * iy + ic
        own = pltpu.make_async_copy(p_ref.at[me], r_ref.at[me], local_sem)
        own.start()
        copies = []
        for k in range(1, NDEV):
            px = 1 - ix if k & 4 else ix
            py = 1 - iy if k & 2 else iy
            pc = 1 - ic if k & 1 else ic
            cp = pltpu.make_async_remote_copy(
                src_ref=p_ref.at[4 * px + 2 * py + pc], dst_ref=r_ref.at[me],
                send_sem=send_sems.at[k - 1], recv_sem=recv_sems.at[k - 1],
                device_id=(px, py, pc), device_id_type=MESH)
            cp.start()
            copies.append(cp)
        for cp in copies:
            cp.wait()
        own.wait()

    return pl.pallas_call(
        body, name=name,
        out_shape=jax.ShapeDtypeStruct(p.shape, p.dtype),
        in_specs=[pl.BlockSpec(memory_space=pl.ANY)],
        out_specs=pl.BlockSpec(memory_space=pl.ANY),
        scratch_shapes=[pltpu.SemaphoreType.DMA((7,)), pltpu.SemaphoreType.DMA((7,)), pltpu.SemaphoreType.DMA(())],
    )(p)


def _sum_leading(r, name):
    k, m, n = r.shape
    mult = 8 * (4 // r.dtype.itemsize)
    tm = _pick(m, max(mult, (4 * 1024 * 1024) // (k * n * r.dtype.itemsize) // mult * mult), mult)

    def body(r_ref, o_ref):
        acc = r_ref[0].astype(F32)
        for s in range(1, k):
            acc = acc + r_ref[s].astype(F32)
        o_ref[...] = acc

    return pl.pallas_call(
        body, name=name, grid=(m // tm,),
        in_specs=[pl.BlockSpec((k, tm, n), lambda i: (0, i, 0))],
        out_specs=pl.BlockSpec((tm, n), lambda i: (i, 0)),
        out_shape=jax.ShapeDtypeStruct((m, n), F32),
        compiler_params=_params("parallel"),
    )(r)


def _mm_nt(a, bt, out_dtype, name):
    M, K = a.shape
    N = bt.shape[0]
    tm, tn = _pick(M, 512, 8), _pick(N, 1536, LANES)

    def body(a_ref, b_ref, o_ref):
        o_ref[...] = _dot(a_ref[...], b_ref[...], NT).astype(out_dtype)

    return pl.pallas_call(
        body, name=name, grid=(M // tm, N // tn),
        in_specs=[pl.BlockSpec((tm, K), lambda i, j: (i, 0)), pl.BlockSpec((tn, K), lambda i, j: (j, 0))],
        out_specs=pl.BlockSpec((tm, tn), lambda i, j: (i, j)),
        out_shape=jax.ShapeDtypeStruct((M, N), out_dtype),
        compiler_params=_params("parallel", "parallel"),
    )(a, bt)


def _mm_nn(pairs, name):
    M = pairs[0][0].shape[0]
    N = pairs[0][1].shape[1]
    tm, tn = _pick(M, 512, 8), _pick(N, 1024, LANES)
    np_ = len(pairs)

    def body(*refs):
        o_ref = refs[-1]
        acc = _dot(refs[0][...], refs[1][...], NN)
        for p in range(1, np_):
            acc = acc + _dot(refs[2 * p][...], refs[2 * p + 1][...], NN)
        o_ref[...] = acc

    in_specs, args = [], []
    for a, b in pairs:
        K = a.shape[1]
        in_specs += [pl.BlockSpec((tm, K), lambda i, j: (i, 0)), pl.BlockSpec((K, tn), lambda i, j: (0, j))]
        args += [a, b]
    return pl.pallas_call(
        body, name=name, grid=(M // tm, N // tn),
        in_specs=in_specs,
        out_specs=pl.BlockSpec((tm, tn), lambda i, j: (i, j)),
        out_shape=jax.ShapeDtypeStruct((M, N), F32),
        compiler_params=_params("parallel", "parallel"),
    )(*args)


def _mm_tn(a, b, name):
    M, N1 = a.shape
    N2 = b.shape[1]
    t1, tk = _pick(N1, 1536, LANES), _pick(M, 512, 8)

    def body(a_ref, b_ref, o_ref):
        @pl.when(pl.program_id(1) == 0)
        def _():
            o_ref[...] = jnp.zeros_like(o_ref)
        o_ref[...] += _dot(a_ref[...], b_ref[...], TN)

    return pl.pallas_call(
        body, name=name, grid=(N1 // t1, M // tk),
        in_specs=[pl.BlockSpec((tk, t1), lambda i, k: (k, i)), pl.BlockSpec((tk, N2), lambda i, k: (k, 0))],
        out_specs=pl.BlockSpec((t1, N2), lambda i, k: (i, 0)),
        out_shape=jax.ShapeDtypeStruct((N1, N2), F32),
        compiler_params=_params("parallel", "arbitrary"),
    )(a, b)


def _mm_res(a, w, x, gate, S, name):
    T, K = a.shape
    D = w.shape[1]
    tm, tn = _pick(S, 512, 8), _pick(D, 512, LANES)
    nb = S // tm

    def body(a_ref, w_ref, x_ref, g_ref, y_ref, o_ref):
        y = _dot(a_ref[...], w_ref[...], NN)
        y_ref[...] = y
        o_ref[...] = x_ref[...] + g_ref[0] * y

    return pl.pallas_call(
        body, name=name, grid=(T // tm, D // tn),
        in_specs=[pl.BlockSpec((tm, K), lambda i, j: (i, 0)), pl.BlockSpec((K, tn), lambda i, j: (0, j)),
                  pl.BlockSpec((tm, tn), lambda i, j: (i, j)), pl.BlockSpec((1, 1, tn), lambda i, j: (i // nb, 0, j))],
        out_specs=[pl.BlockSpec((tm, tn), lambda i, j: (i, j)), pl.BlockSpec((tm, tn), lambda i, j: (i, j))],
        out_shape=[jax.ShapeDtypeStruct((T, D), F32), jax.ShapeDtypeStruct((T, D), F32)],
        compiler_params=_params("parallel", "parallel"),
    )(a, w, x, gate)


def _swiglu_fwd(h, wgt, wut, name):
    T, D = h.shape
    F = wgt.shape[0]
    tm, tn = _pick(T, 512, 8), _pick(F, 1536, LANES)

    def body(h_ref, g_ref, u_ref, go_ref, uo_ref, a_ref):
        hh = h_ref[...]
        g = _dot(hh, g_ref[...], NT)
        u = _dot(hh, u_ref[...], NT)
        go_ref[...] = g
        uo_ref[...] = u
        a_ref[...] = (g * jax.nn.sigmoid(g) * u).astype(BF16)

    spec_w = pl.BlockSpec((tn, D), lambda i, j: (j, 0))
    spec_o = pl.BlockSpec((tm, tn), lambda i, j: (i, j))
    return pl.pallas_call(
        body, name=name, grid=(T // tm, F // tn),
        in_specs=[pl.BlockSpec((tm, D), lambda i, j: (i, 0)), spec_w, spec_w],
        out_specs=[spec_o, spec_o, spec_o],
        out_shape=[jax.ShapeDtypeStruct((T, F), F32), jax.ShapeDtypeStruct((T, F), F32),
                   jax.ShapeDtypeStruct((T, F), BF16)],
        compiler_params=_params("parallel", "parallel"),
    )(h, wgt, wut)


def _swiglu_bwd(dy, wd, gate, up, name):
    T, D = dy.shape
    F = wd.shape[0]
    tm, tn = _pick(T, 512, 8), _pick(F, 1536, LANES)

    def body(dy_ref, w_ref, g_ref, u_ref, dg_ref, du_ref):
        da = _dot(dy_ref[...], w_ref[...], NT)
        g = g_ref[...]
        sg = jax.nn.sigmoid(g)
        silu = g * sg
        du_ref[...] = (da * silu).astype(BF16)
        dg_ref[...] = (da * u_ref[...] * (sg + silu * (1.0 - sg))).astype(BF16)

    spec_o = pl.BlockSpec((tm, tn), lambda i, j: (i, j))
    return pl.pallas_call(
        body, name=name, grid=(T // tm, F // tn),
        in_specs=[pl.BlockSpec((tm, D), lambda i, j: (i, 0)), pl.BlockSpec((tn, D), lambda i, j: (j, 0)), spec_o, spec_o],
        out_specs=[spec_o, spec_o],
        out_shape=[jax.ShapeDtypeStruct((T, F), BF16), jax.ShapeDtypeStruct((T, F), BF16)],
        compiler_params=_params("parallel", "parallel"),
    )(dy, wd, gate, up)


def _norm_mod(x, gain, sc, sh, S, name):
    T, D = x.shape
    tm = _pick(S, 512, 8)
    nb = S // tm

    def body(x_ref, g_ref, sc_ref, sh_ref, o_ref):
        xv = x_ref[...]
        r = lax.rsqrt(jnp.mean(xv * xv, axis=-1, keepdims=True) + EPS)
        o_ref[...] = ((xv * r) * g_ref[...] * (1.0 + sc_ref[0]) + sh_ref[0]).astype(BF16)

    spec_b = pl.BlockSpec((1, 1, D), lambda i: (i // nb, 0, 0))
    return pl.pallas_call(
        body, name=name, grid=(T // tm,),
        in_specs=[pl.BlockSpec((tm, D), lambda i: (i, 0)), pl.BlockSpec((1, D), lambda i: (0, 0)), spec_b, spec_b],
        out_specs=pl.BlockSpec((tm, D), lambda i: (i, 0)),
        out_shape=jax.ShapeDtypeStruct((T, D), BF16),
        compiler_params=_params("parallel"),
    )(x, gain, sc, sh)


def _norm_mod_bwd(x, dh, dres, gain, sc, S, name):
    T, D = x.shape
    B = T // S
    tm = _pick(S, 256, 8)
    nb = S // tm

    def body(x_ref, dh_ref, dr_ref, g_ref, sc_ref, o_ref, dsh_ref, dsc_ref, dg_ref):
        @pl.when(pl.program_id(1) == 0)
        def _():
            dsh_ref[...] = jnp.zeros_like(dsh_ref)
            dsc_ref[...] = jnp.zeros_like(dsc_ref)
            dg_ref[...] = jnp.zeros_like(dg_ref)
        xv, dhv, g = x_ref[...], dh_ref[...], g_ref[...]
        r = lax.rsqrt(jnp.mean(xv * xv, axis=-1, keepdims=True) + EPS)
        xhat = xv * r
        dsh_ref[0] += jnp.sum(dhv, axis=0, keepdims=True)
        dsc_ref[0] += jnp.sum(dhv * (xhat * g), axis=0, keepdims=True)
        dn = dhv * (1.0 + sc_ref[0])
        dg_ref[0] += jnp.sum(dn * xhat, axis=0, keepdims=True)
        dxh = dn * g
        o_ref[...] = dr_ref[...] + r * (dxh - xhat * jnp.mean(dxh * xhat, axis=-1, keepdims=True))

    spec_t = pl.BlockSpec((tm, D), lambda b, i: (b * nb + i, 0))
    spec_b = pl.BlockSpec((1, 1, D), lambda b, i: (b, 0, 0))
    red = jax.ShapeDtypeStruct((B, 1, D), F32)
    return pl.pallas_call(
        body, name=name, grid=(B, nb),
        in_specs=[spec_t, spec_t, spec_t, pl.BlockSpec((1, D), lambda b, i: (0, 0)), spec_b],
        out_specs=[spec_t, spec_b, spec_b, spec_b],
        out_shape=[jax.ShapeDtypeStruct((T, D), F32), red, red, red],
        compiler_params=_params("parallel", "arbitrary"),
    )(x, dh, dres, gain, sc)


def _gate_bwd(dx, y, gate, S, name):
    T, D = dx.shape
    B = T // S
    tm = _pick(S, 512, 8)
    nb = S // tm

    def body(dx_ref, y_ref, g_ref, dy_ref, dg_ref):
        @pl.when(pl.program_id(1) == 0)
        def _():
            dg_ref[...] = jnp.zeros_like(dg_ref)
        d = dx_ref[...]
        dy_ref[...] = (d * g_ref[0]).astype(BF16)
        dg_ref[0] += jnp.sum(d * y_ref[...], axis=0, keepdims=True)

    spec_t = pl.BlockSpec((tm, D), lambda b, i: (b * nb + i, 0))
    spec_b = pl.BlockSpec((1, 1, D), lambda b, i: (b, 0, 0))
    return pl.pallas_call(
        body, name=name, grid=(B, nb),
        in_specs=[spec_t, spec_t, spec_b],
        out_specs=[spec_t, spec_b],
        out_shape=[jax.ShapeDtypeStruct((T, D), BF16), jax.ShapeDtypeStruct((B, 1, D), F32)],
        compiler_params=_params("parallel", "arbitrary"),
    )(dx, y, gate)


def _loss_head(y, target, name):
    T, D = y.shape
    tm = _pick(T, 512, 8)

    def body(y_ref, t_ref, dy_ref, l_ref):
        @pl.when(pl.program_id(0) == 0)
        def _():
            l_ref[...] = jnp.zeros_like(l_ref)
        e = y_ref[...] - t_ref[...]
        dy_ref[...] = e * (1.0 / D)
        l_ref[...] += 0.5 * jnp.sum(jnp.mean(e * e, axis=-1, keepdims=True), axis=0, keepdims=True)

    spec = pl.BlockSpec((tm, D), lambda i: (i, 0))
    return pl.pallas_call(
        body, name=name, grid=(T // tm,),
        in_specs=[spec, spec],
        out_specs=[spec, pl.BlockSpec((8, LANES), lambda i: (0, 0))],
        out_shape=[jax.ShapeDtypeStruct((T, D), F32), jax.ShapeDtypeStruct((8, LANES), F32)],
        compiler_params=_params("arbitrary"),
    )(y, target)


def _ada_fwd(c_all, ada_w, bias, name):
    NB, D = c_all.shape
    L, _, W = ada_w.shape

    def body(c_ref, w_ref, b_ref, o_ref):
        cv = c_ref[...]
        cond = cv * jax.nn.sigmoid(cv)
        o_ref[0] = _dot(cond, w_ref[0], NN, HIGH) + b_ref[0]

    return pl.pallas_call(
        body, name=name, grid=(L,),
        in_specs=[pl.BlockSpec((NB, D), lambda l: (0, 0)), pl.BlockSpec((1, D, W), lambda l: (l, 0, 0)),
                  pl.BlockSpec((1, 1, W), lambda l: (l, 0, 0))],
        out_specs=pl.BlockSpec((1, NB, W), lambda l: (l, 0, 0)),
        out_shape=jax.ShapeDtypeStruct((L, NB, W), F32),
        compiler_params=_params("parallel"),
    )(c_all, ada_w, bias)


def _ada_bwd(c_all, dmod, name):
    NB, D = c_all.shape
    L, _, W = dmod.shape

    def body(c_ref, d_ref, o_ref):
        cv = c_ref[...]
        cond = cv * jax.nn.sigmoid(cv)
        o_ref[0] = _dot(cond, d_ref[0], TN, HIGH)

    return pl.pallas_call(
        body, name=name, grid=(L,),
        in_specs=[pl.BlockSpec((NB, D), lambda l: (0, 0)), pl.BlockSpec((1, NB, W), lambda l: (l, 0, 0))],
        out_specs=pl.BlockSpec((1, D, W), lambda l: (l, 0, 0)),
        out_shape=jax.ShapeDtypeStruct((L, D, W), F32),
        compiler_params=_params("parallel"),
    )(c_all, dmod)


def _lo_mask(shape):
    return lax.broadcasted_iota(jnp.int32, shape, len(shape) - 1) < HEAD


def _head_sum_matrix():
    r = lax.broadcasted_iota(jnp.int32, (LANES, LANES), 0) // HEAD
    c = lax.broadcasted_iota(jnp.int32, (LANES, LANES), 1) // HEAD
    return (r == c).astype(F32)


def _rope(y, cs, s1, s2):
    return y * cs + pltpu.roll(y, LANES - ROT // 2, 1) * s1 + pltpu.roll(y, ROT // 2, 1) * s2


def _rope_bwd(d, cs, s1, s2):
    return d * cs + pltpu.roll(d * s1, ROT // 2, 1) + pltpu.roll(d * s2, LANES - ROT // 2, 1)


def _qk_prep(qkv, cs, s1, s2, qg, kg, name):
    T, W = qkv.shape
    NQ = W - 2 * LANES
    tm = _pick(T, 512, 8)

    def body(x_ref, cs_ref, s1_ref, s2_ref, qg_ref, kg_ref, q_ref, k_ref, v_ref):
        P = _head_sum_matrix()
        cs_, s1_, s2_ = cs_ref[...], s1_ref[...], s2_ref[...]
        lo = _lo_mask((tm, LANES))

        def norm_rope(xv, g):
            ms = _dot(xv * xv, P, NN, HIGH) * (1.0 / HEAD)
            return _rope(xv * lax.rsqrt(ms + EPS) * g, cs_, s1_, s2_)

        for j in range(NQ // LANES):
            q_ref[:, j * LANES:(j + 1) * LANES] = norm_rope(x_ref[:, j * LANES:(j + 1) * LANES], qg_ref[...]).astype(BF16)
        kr = norm_rope(x_ref[:, NQ:NQ + LANES], kg_ref[...])
        ks = pltpu.roll(kr, HEAD, 1)
        k_ref[:, :LANES] = jnp.where(lo, kr, ks).astype(BF16)
        k_ref[:, LANES:] = jnp.where(lo, ks, kr).astype(BF16)
        vr = x_ref[:, NQ + LANES:]
        vs = pltpu.roll(vr, HEAD, 1)
        v_ref[:, :LANES] = jnp.where(lo, vr, vs).astype(BF16)
        v_ref[:, LANES:] = jnp.where(lo, vs, vr).astype(BF16)

    spec_t = pl.BlockSpec((tm, LANES), lambda i: (i, 0))
    spec_g = pl.BlockSpec((1, LANES), lambda i: (0, 0))
    return pl.pallas_call(
        body, name=name, grid=(T // tm,),
        in_specs=[pl.BlockSpec((tm, W), lambda i: (i, 0)), spec_t, spec_t, spec_t, spec_g, spec_g],
        out_specs=[pl.BlockSpec((tm, NQ), lambda i: (i, 0)), pl.BlockSpec((tm, 2 * LANES), lambda i: (i, 0)),
                   pl.BlockSpec((tm, 2 * LANES), lambda i: (i, 0))],
        out_shape=[jax.ShapeDtypeStruct((T, NQ), BF16), jax.ShapeDtypeStruct((T, 2 * LANES), BF16),
                   jax.ShapeDtypeStruct((T, 2 * LANES), BF16)],
        compiler_params=_params("parallel"),
    )(qkv, cs, s1, s2, qg, kg)


def _stack_heads(x2):
    lo = _lo_mask(x2.shape)
    z = jnp.zeros_like(x2)
    return jnp.concatenate([jnp.where(lo, x2, z), jnp.where(lo, z, x2)], axis=0)


def _unstack_heads(xs):
    r = xs.shape[0] // 2
    return jnp.where(_lo_mask((r, LANES)), xs[:r], xs[r:])


def _swa_valid(i):
    qo = lax.broadcasted_iota(jnp.int32, (2 * BLK, 2 * BLK), 0) % BLK
    kc_ = lax.broadcasted_iota(jnp.int32, (2 * BLK, 2 * BLK), 1)
    rel = qo + BLK - kc_
    return (rel >= 0) & (rel < BLK) & ((kc_ >= BLK) | (i > 0))


def _swa_scores(q2, kk, sink2, valid):
    qs = _stack_heads(q2) * SCALE
    s = _dot(qs, kk, NT)
    sk = jnp.concatenate([jnp.broadcast_to(sink2[:, 0:1], (BLK, 1)), jnp.broadcast_to(sink2[:, HEAD:HEAD + 1], (BLK, 1))], axis=0)
    return qs, jnp.where(valid, s, NEG), sk


def _swa_fwd(q, kd, vd, sink2, B, name):
    T, NQ = q.shape
    NP = NQ // LANES
    nq = T // B // BLK
    NG = kd.shape[1] // LANES
    grp = NP // NG

    def body(q_ref, kp_ref, kc_ref, vp_ref, vc_ref, s_ref, o_ref, l_ref):
        valid = _swa_valid(pl.program_id(2))
        kk = jnp.concatenate([kp_ref[...], kc_ref[...]], axis=0)
        vs = _stack_heads(jnp.concatenate([vp_ref[...], vc_ref[...]], axis=0))
        for jj in range(grp):
            sl = slice(jj * LANES, (jj + 1) * LANES)
            _, s, sk = _swa_scores(q_ref[:, sl], kk, s_ref[jj], valid)
            m = jnp.maximum(jnp.max(s, axis=1, keepdims=True), sk)
            p = jnp.where(valid, jnp.exp(s - m), 0.0)
            l = jnp.sum(p, axis=1, keepdims=True) + jnp.exp(sk - m)
            p = (p / l).astype(BF16)
            p2 = jnp.concatenate([p[:BLK], p[BLK:]], axis=1)
            o_ref[:, sl] = _dot(p2, vs, NN).astype(BF16)
            l_ref[:, sl] = _unstack_heads(jnp.broadcast_to(m + jnp.log(l), (2 * BLK, LANES)))

    spec_q = pl.BlockSpec((BLK, grp * LANES), lambda b, g, i: (b * nq + i, g))
    spec_p = pl.BlockSpec((BLK, LANES), lambda b, g, i: (b * nq + jnp.maximum(i - 1, 0), g))
    spec_c = pl.BlockSpec((BLK, LANES), lambda b, g, i: (b * nq + i, g))
    return pl.pallas_call(
        body, name=name, grid=(B, NG, nq),
        in_specs=[spec_q, spec_p, spec_c, spec_p, spec_c, pl.BlockSpec((grp, 1, LANES), lambda b, g, i: (g, 0, 0))],
        out_specs=[spec_q, spec_q],
        out_shape=[jax.ShapeDtypeStruct((T, NQ), BF16), jax.ShapeDtypeStruct((T, NQ), F32)],
        compiler_params=_params("parallel", "parallel", "parallel"),
    )(q, kd, kd, vd, vd, sink2)


def _swa_bwd(q, kd, vd, sink2, do, lse, B, name):
    T, NQ = q.shape
    NP = NQ // LANES
    nq = T // B // BLK
    NG = kd.shape[1] // LANES
    grp = NP // NG

    def body(q_ref, kp_ref, kc_ref, vp_ref, vc_ref, s_ref, do_ref, l_ref,
             dq_ref, dkc_ref, dkp_ref, dvc_ref, dvp_ref, ds_ref):
        b, i = pl.program_id(1), pl.program_id(2)

        @pl.when((b == 0) & (i == 0))
        def _():
            ds_ref[...] = jnp.zeros_like(ds_ref)
        valid = _swa_valid(i)
        kk = jnp.concatenate([kp_ref[...], kc_ref[...]], axis=0)
        vv = jnp.concatenate([vp_ref[...], vc_ref[...]], axis=0)
        dk = jnp.zeros((2 * BLK, LANES), F32)
        dv = jnp.zeros((2 * BLK, LANES), F32)
        for jj in range(grp):
            sl = slice(jj * LANES, (jj + 1) * LANES)
            qs, s, sk = _swa_scores(q_ref[:, sl], kk, s_ref[jj], valid)
            lse_ = l_ref[:, sl]
            lse_s = jnp.concatenate([lse_[:, 0:1], lse_[:, HEAD:HEAD + 1]], axis=0)
            p = jnp.where(valid, jnp.exp(s - lse_s), 0.0)
            dos = _stack_heads(do_ref[:, sl])
            dp = _dot(dos, vv, NT)
            delta = jnp.sum(p * dp, axis=1, keepdims=True)
            dsc = (p * (dp - delta)).astype(BF16)
            dsk = -jnp.exp(sk - lse_s) * delta
            dsk_lo = jnp.sum(dsk[:BLK], axis=0, keepdims=True)
            dsk_hi = jnp.sum(dsk[BLK:], axis=0, keepdims=True)
            ds_ref[jj] += jnp.where(_lo_mask((1, LANES)), dsk_lo, dsk_hi)
            dq_ref[:, sl] = _unstack_heads(_dot(dsc, kk, NN)) * SCALE
            dk = dk + _dot(dsc, qs, TN)
            dv = dv + _dot(p.astype(BF16), dos, TN)
        dkp_ref[...] = dk[:BLK]
        dkc_ref[...] = dk[BLK:]
        dvp_ref[...] = dv[:BLK]
        dvc_ref[...] = dv[BLK:]

    spec_q = pl.BlockSpec((BLK, grp * LANES), lambda g, b, i: (b * nq + i, g))
    spec_p = pl.BlockSpec((BLK, LANES), lambda g, b, i: (b * nq + jnp.maximum(i - 1, 0), g))
    spec_c = pl.BlockSpec((BLK, LANES), lambda g, b, i: (b * nq + i, g))
    spec_s = pl.BlockSpec((grp, 1, LANES), lambda g, b, i: (g, 0, 0))
    kv = jax.ShapeDtypeStruct((T, NG * LANES), F32)
    return pl.pallas_call(
        body, name=name, grid=(NG, B, nq),
        in_specs=[spec_q, spec_p, spec_c, spec_p, spec_c, spec_s, spec_q, spec_q],
        out_specs=[spec_q, spec_c, spec_c, spec_c, spec_c, spec_s],
        out_shape=[jax.ShapeDtypeStruct((T, NQ), F32), kv, kv, kv, kv, jax.ShapeDtypeStruct((NP, 1, LANES), F32)],
        compiler_params=_params("arbitrary", "arbitrary", "arbitrary"),
    )(q, kd, kd, vd, vd, sink2, do, lse)


def _qk_prep_bwd(qkv, cs, s1, s2, qg, kg, dq, dkc, dkp, dvc, dvp, B, name):
    T, W = qkv.shape
    NQ = W - 2 * LANES
    NP = NQ // LANES
    nq = T // B // BLK

    def body(x_ref, cs_ref, s1_ref, s2_ref, qg_ref, kg_ref, dq_ref, dkc_ref, dkp_ref, dvc_ref, dvp_ref,
             o_ref, dqg_ref, dkg_ref):
        b, i = pl.program_id(0), pl.program_id(1)

        @pl.when((b == 0) & (i == 0))
        def _():
            dqg_ref[...] = jnp.zeros_like(dqg_ref)
            dkg_ref[...] = jnp.zeros_like(dkg_ref)
        P = _head_sum_matrix()
        cs_, s1_, s2_ = cs_ref[...], s1_ref[...], s2_ref[...]
        lo = _lo_mask((BLK, LANES))
        has_next = (i + 1 < nq).astype(F32)

        def norm_rope_bwd(xv, g, d):
            du = _rope_bwd(d, cs_, s1_, s2_)
            r = lax.rsqrt(_dot(xv * xv, P, NN, HIGH) * (1.0 / HEAD) + EPS)
            xhat = xv * r
            dgain = jnp.sum(du * xhat, axis=0, keepdims=True)
            uu = du * g
            dx = r * (uu - xhat * (_dot(uu * xhat, P, NN, HIGH) * (1.0 / HEAD)))
            return dx, dgain + pltpu.roll(dgain, HEAD, 1)

        dqg = jnp.zeros((1, LANES), F32)
        for j in range(NP):
            sl = slice(j * LANES, (j + 1) * LANES)
            dx, dg = norm_rope_bwd(x_ref[:, sl], qg_ref[...], dq_ref[:, sl])
            o_ref[:, sl] = dx.astype(BF16)
            dqg = dqg + dg
        dqg_ref[...] += dqg

        def fold(c_ref, p_ref, g):
            sl = slice(g * LANES, (g + 1) * LANES)
            t = c_ref[:, sl] + has_next * p_ref[:, sl]
            return t + pltpu.roll(t, HEAD, 1)

        dk = jnp.where(lo, fold(dkc_ref, dkp_ref, 0), fold(dkc_ref, dkp_ref, 1))
        dx, dg = norm_rope_bwd(x_ref[:, NQ:NQ + LANES], kg_ref[...], dk)
        o_ref[:, NQ:NQ + LANES] = dx.astype(BF16)
        dkg_ref[...] += dg
        dv = jnp.where(lo, fold(dvc_ref, dvp_ref, 0), fold(dvc_ref, dvp_ref, 1))
        o_ref[:, NQ + LANES:] = dv.astype(BF16)

    spec_t = pl.BlockSpec((BLK, LANES), lambda b, i: (b * nq + i, 0))
    spec_g = pl.BlockSpec((1, LANES), lambda b, i: (0, 0))
    spec_c = pl.BlockSpec((BLK, 2 * LANES), lambda b, i: (b * nq + i, 0))
    spec_n = pl.BlockSpec((BLK, 2 * LANES), lambda b, i: (b * nq + jnp.minimum(i + 1, nq - 1), 0))
    row = jax.ShapeDtypeStruct((1, LANES), F32)
    return pl.pallas_call(
        body, name=name, grid=(B, nq),
        in_specs=[pl.BlockSpec((BLK, W), lambda b, i: (b * nq + i, 0)), spec_t, spec_t, spec_t, spec_g, spec_g,
                  pl.BlockSpec((BLK, NQ), lambda b, i: (b * nq + i, 0)), spec_c, spec_n, spec_c, spec_n],
        out_specs=[pl.BlockSpec((BLK, W), lambda b, i: (b * nq + i, 0)), spec_g, spec_g],
        out_shape=[jax.ShapeDtypeStruct((T, W), BF16), row, row],
        compiler_params=_params("arbitrary", "arbitrary"),
    )(qkv, cs, s1, s2, qg, kg, dq, dkc, dkp, dvc, dvp)


SB_TILE = 256


def _split_heads(x2, scale=None):
    lo = _lo_mask(x2.shape)
    z = jnp.zeros_like(x2)
    if scale is not None:
        x2 = x2 * scale
    return jnp.where(lo, x2, z), jnp.where(lo, z, x2)


def _sb_terms(qh, kj, diagonal):
    z = _dot(qh, kj, NT)
    e = jnp.exp(-jnp.abs(z))
    lb = jnp.minimum(z, 0.0) - jnp.log(1.0 + e)
    L = lb - z
    if not diagonal:
        return lb, L, None, z, e
    strict = lax.broadcasted_iota(jnp.int32, z.shape, 1) < lax.broadcasted_iota(jnp.int32, z.shape, 0)
    return lb, jnp.where(strict, L, 0.0), strict, z, e


def _tri(n, cmp):
    r = lax.broadcasted_iota(jnp.int32, (n, n), 0)
    c = lax.broadcasted_iota(jnp.int32, (n, n), 1)
    return cmp(r, c).astype(BF16)


def _sb_fwd(qkv, B, name):
    T, W = qkv.shape
    NQ = W // 3
    NP = NQ // LANES
    S = T // B
    tq = min(SB_TILE, S)
    nq = S // tq

    def body(q_ref, k_ref, v_ref, o_ref, t_ref):
        i = pl.program_id(2)
        qh = _split_heads(q_ref[...], SCALE)
        U = _tri(tq, lambda r, c: r > c)

        def tile(j, cs, acc, diagonal):
            rows = pl.ds(pl.multiple_of(j * tq, tq), tq)
            kj = k_ref[rows, :]
            vh = _split_heads(v_ref[rows, :])
            out = []
            for h in range(2):
                lb, L, strict, _, _ = _sb_terms(qh[h], kj, diagonal)
                a = jnp.exp(lb + (_dot(L.astype(BF16), U, NN) + cs[h]))
                if diagonal:
                    a = jnp.where(strict, a, 0.0)
                acc = acc + _dot(a.astype(BF16), vh[h], NN)
                out.append(cs[h] + jnp.sum(L, axis=1, keepdims=True))
            return (out[0], out[1]), acc

        zero = jnp.zeros((tq, 1), F32)
        def pair(n, cr):
            cr = tile(i - 1 - 2 * n, cr[0], cr[1], False)
            return tile(i - 2 - 2 * n, cr[0], cr[1], False)

        carry = tile(i, (zero, zero), jnp.zeros((tq, LANES), F32), True)
        carry = lax.fori_loop(0, i // 2, pair, carry)
        cs, acc = lax.cond(i % 2 == 1, lambda cr: tile(0, cr[0], cr[1], False), lambda cr: cr, carry)
        o_ref[...] = acc.astype(BF16)
        t_ref[...] = jnp.where(_lo_mask((tq, LANES)), cs[0], cs[1])

    spec_q = pl.BlockSpec((tq, LANES), lambda b, p, i: (b * nq + i, p))
    return pl.pallas_call(
        body, name=name, grid=(B, NP, nq),
        in_specs=[spec_q, pl.BlockSpec((S, LANES), lambda b, p, i: (b, NP + p)),
                  pl.BlockSpec((S, LANES), lambda b, p, i: (b, 2 * NP + p))],
        out_specs=[spec_q, spec_q],
        out_shape=[jax.ShapeDtypeStruct((T, NQ), BF16), jax.ShapeDtypeStruct((T, NQ), F32)],
        compiler_params=_params("parallel", "parallel", "arbitrary"),
    )(qkv, qkv, qkv)


def _sb_bwd(qkv, q_t, do, do_t, tot, B, name):
    T, W = qkv.shape
    NQ = W // 3
    NP = NQ // LANES
    S = T // B
    tq = min(SB_TILE, S)
    nq = S // tq

    def body(q_ref, k_ref, v_ref, do_ref, qt_ref, dot_ref, t_ref, dq_ref, dk_ref, dv_ref):
        i = pl.program_id(2)

        @pl.when(i == 0)
        def _():
            dk_ref[...] = jnp.zeros_like(dk_ref)
            dv_ref[...] = jnp.zeros_like(dv_ref)
        qh = _split_heads(q_ref[...], SCALE)
        doh = _split_heads(do_ref[...])
        top = lax.broadcasted_iota(jnp.int32, (LANES, tq), 0) < HEAD
        zt = jnp.zeros((LANES, tq), BF16)
        qt = qt_ref[...] * SCALE
        qth = (jnp.where(top, qt, zt), jnp.where(top, zt, qt))
        doth = (jnp.where(top, dot_ref[...], zt), jnp.where(top, zt, dot_ref[...]))
        tt = t_ref[...]
        tot = (tt[:, 0:1], tt[:, HEAD:HEAD + 1])
        Urev = _tri(tq, lambda r, c: r > c)
        Uexc = _tri(tq, lambda r, c: r < c)

        def tile(j, carry, diagonal):
            rows = pl.ds(pl.multiple_of(j * tq, tq), tq)
            kj, vj = k_ref[rows, :], v_ref[rows, :]
            dk = jnp.zeros((LANES, tq), F32)
            dv = jnp.zeros((LANES, tq), F32)
            new = []
            for h in range(2):
                cc, cg, dq = carry[h]
                lb, L, strict, z, e = _sb_terms(qh[h], kj, diagonal)
                cc = cc + jnp.sum(L, axis=1, keepdims=True)
                a = jnp.exp(lb + (_dot(L.astype(BF16), Urev, NN) + (tot[h] - cc)))
                if diagonal:
                    a = jnp.where(strict, a, 0.0)
                g = a * _dot(doh[h], vj, NT)
                G = _dot(g.astype(BF16), Uexc, NN) + cg
                dz = g - jnp.exp(lb) * (g + G)
                if diagonal:
                    dz = jnp.where(strict, dz, 0.0)
                dz = dz.astype(BF16)
                dk = dk + _dot(qth[h], dz, NN)
                dv = dv + _dot(doth[h], a.astype(BF16), NN)
                new.append((cc, cg + jnp.sum(g, axis=1, keepdims=True), dq + _dot(dz, kj, NN)))
            dk_ref[:, rows] += dk
            dv_ref[:, rows] += dv
            return tuple(new)

        zero = jnp.zeros((tq, 1), F32)
        zq = jnp.zeros((tq, LANES), F32)
        carry = lax.fori_loop(0, i // 2, lambda n, cr: tile(2 * n + 1, tile(2 * n, cr, False), False),
                              ((zero, zero, zq), (zero, zero, zq)))
        carry = lax.cond(i % 2 == 1, lambda cr: tile(i - 1, cr, False), lambda cr: cr, carry)
        carry = tile(i, carry, True)
        dq_ref[...] = jnp.where(_lo_mask((tq, LANES)), carry[0][2], carry[1][2]) * SCALE

    spec_q = pl.BlockSpec((tq, LANES), lambda b, p, i: (b * nq + i, p))
    spec_t = pl.BlockSpec((LANES, tq), lambda b, p, i: (p, b * nq + i))
    spec_s = pl.BlockSpec((LANES, S), lambda b, p, i: (b * NP + p, 0))
    key_side = jax.ShapeDtypeStruct((B * NQ, S), F32)
    return pl.pallas_call(
        body, name=name, grid=(B, NP, nq),
        in_specs=[spec_q, pl.BlockSpec((S, LANES), lambda b, p, i: (b, NP + p)),
                  pl.BlockSpec((S, LANES), lambda b, p, i: (b, 2 * NP + p)), spec_q, spec_t, spec_t, spec_q],
        out_specs=[spec_q, spec_s, spec_s],
        out_shape=[jax.ShapeDtypeStruct((T, NQ), F32), key_side, key_side],
        compiler_params=_params("parallel", "parallel", "arbitrary"),
    )(qkv, qkv, qkv, do, q_t, do_t, tot)


def _adamw(w, g, m, v, name):
    shape = w.shape
    cols = shape[-1]
    rows = math.prod(shape[:-1])
    tr = _pick(rows, max(8, (1 << 19) // max(cols, LANES) // 8 * 8), 8)

    def body(w_ref, g_ref, m_ref, v_ref, d_ref, mo_ref, vo_ref):
        gv = g_ref[...]
        mn = ADAM_B1 * m_ref[...] + (1.0 - ADAM_B1) * gv
        vn = ADAM_B2 * v_ref[...] + (1.0 - ADAM_B2) * (gv * gv)
        m_hat = mn / (1.0 - ADAM_B1 ** ADAM_STEP)
        v_hat = vn / (1.0 - ADAM_B2 ** ADAM_STEP)
        d_ref[...] = -ADAM_LR * (m_hat / (jnp.sqrt(v_hat) + ADAM_EPS) + ADAM_WD * w_ref[...])
        mo_ref[...] = mn
        vo_ref[...] = vn

    spec = pl.BlockSpec((tr, cols), lambda i: (i, 0))
    out = jax.ShapeDtypeStruct((rows, cols), F32)
    d, mn, vn = pl.pallas_call(
        body, name=name, grid=(rows // tr,),
        in_specs=[spec] * 4, out_specs=[spec] * 3, out_shape=[out] * 3,
        compiler_params=_params("parallel"),
    )(w.reshape(rows, cols), g.reshape(rows, cols), m.reshape(rows, cols), v.reshape(rows, cols))
    return d.reshape(shape), mn.reshape(shape), vn.reshape(shape)


def _pad_rows(a, rows):
    return jnp.pad(a, ((0, rows - a.shape[0]), (0, 0)))


def kernel(x, c, positions, ada_w, ada_b, norm1_g, norm2_g, wqkv_a, q_norm_a, k_norm_a, sinks_a, wo_a, wqkv_b, wo_b, w_gate, w_up, w_down, loss_target, m_ada_w, m_ada_b, m_norm1_g, m_norm2_g, m_wqkv_a, m_q_norm_a, m_k_norm_a, m_sinks_a, m_wo_a, m_wqkv_b, m_wo_b, m_w_gate, m_w_up, m_w_down, v_ada_w, v_ada_b, v_norm1_g, v_norm2_g, v_wqkv_a, v_q_norm_a, v_k_norm_a, v_sinks_a, v_wo_a, v_wqkv_b, v_wo_b, v_w_gate, v_w_up, v_w_down):
    B, S, D = x.shape
    T = B * S
    L = ada_w.shape[0]
    NA, NB_ = wqkv_a.shape[0], wqkv_b.shape[0]
    me = 4 * lax.axis_index("x") + 2 * lax.axis_index("y") + lax.axis_index("c")
    xt = x.reshape(T, D)

    col_sharded = [(wqkv_a, NA), (wqkv_b, NB_), (w_gate, L), (w_up, L)]
    row_sharded = [(wo_a, NA), (wo_b, NB_), (w_down, L)]
    pieces, layout = [], []
    for w, n in col_sharded:
        for l in range(n):
            pieces.append(w[l].T.astype(BF16))
            layout.append(w.shape[2])
    for w, n in row_sharded:
        for l in range(n):
            pieces.append(w[l].astype(BF16))
            layout.append(w.shape[1])
    packed = jnp.concatenate(pieces, axis=0)
    R = packed.shape[0]
    gathered = _all_gather(packed, "ag_weights")
    full, off = [], 0
    for rows in layout:
        full.append(gathered[:, off:off + rows, :].reshape(NDEV * rows, D))
        off += rows
    it = iter(full)
    wqkv_a_t = [next(it) for _ in range(NA)]
    wqkv_b_t = [next(it) for _ in range(NB_)]
    wg_t = [next(it) for _ in range(L)]
    wu_t = [next(it) for _ in range(L)]
    wo_a_f = [next(it) for _ in range(NA)]
    wo_b_f = [next(it) for _ in range(NB_)]
    wd_f = [next(it) for _ in range(L)]

    WA = ada_w.shape[2]
    c_all = _all_gather(c, "ag_c").reshape(NDEV * B, D)
    bias = lax.dynamic_slice_in_dim(ada_b, me * WA, WA, axis=1).reshape(L, 1, WA)
    mod_part = _ada_fwd(c_all, ada_w, bias, "ada_fwd")
    mod_all = _all_gather(mod_part.reshape(L * NDEV * B, WA), "ag_mod")
    mod_all = mod_all.reshape(NDEV, L, NDEV * B, WA).transpose(1, 2, 0, 3).reshape(L, NDEV * B, NDEV * WA)
    mod = lax.dynamic_slice_in_dim(mod_all, me * B, B, axis=1)
    mod = mod.reshape(L, B, 6, 1, D)
    sh1, sc1, g1, sh2, sc2, g2 = [mod[:, :, k] for k in range(6)]

    half = ROT // 2
    inv_freq = jnp.power(jnp.float32(ROPE_THETA), -jnp.arange(half, dtype=F32) * 2.0 / ROT)
    ang = positions.reshape(T, 1).astype(F32) * inv_freq[None, :]
    cos, sin = jnp.cos(ang), jnp.sin(ang)
    ones = jnp.ones((T, HEAD - ROT), F32)
    zeros = jnp.zeros((T, HEAD - ROT), F32)
    z8 = jnp.zeros((T, half), F32)
    cs = jnp.tile(jnp.concatenate([cos, cos, ones], axis=1), (1, 2))
    s1 = jnp.tile(jnp.concatenate([-sin, z8, zeros], axis=1), (1, 2))
    s2 = jnp.tile(jnp.concatenate([z8, sin, zeros], axis=1), (1, 2))

    saved = []
    xc = xt
    for l in range(L):
        j = l // 2
        h1 = _norm_mod(xc, norm1_g[l:l + 1], sc1[l], sh1[l], S, f"norm1_{l}")
        sv = dict(x_in=xc, h1=h1)
        if l % 2 == 0:
            qkv = _mm_nt(h1, wqkv_a_t[j], F32, f"qkv_a_{l}")
            qg = jnp.tile(q_norm_a[j:j + 1], (1, 2))
            kg = jnp.tile(k_norm_a[j:j + 1], (1, 2))
            qn, kd, vd = _qk_prep(qkv, cs, s1, s2, qg, kg, f"qk_prep_{l}")
            sink2 = jnp.repeat(sinks_a[j].reshape(-1, 2), HEAD, axis=1).reshape(-1, 1, LANES)
            attn, lse = _swa_fwd(qn, kd, vd, sink2, B, f"swa_fwd_{l}")
            sv.update(qkv=qkv, qg=qg, kg=kg, qn=qn, kd=kd, vd=vd, sink2=sink2, lse=lse)
            wo = wo_a_f[j]
        else:
            qkv = _mm_nt(h1, wqkv_b_t[j], BF16, f"qkv_b_{l}")
            attn, tot = _sb_fwd(qkv, B, f"sb_fwd_{l}")
            sv.update(qkv=qkv, tot=tot)
            wo = wo_b_f[j]
        y1, xm = _mm_res(attn, wo, xc, g1[l], S, f"attn_out_{l}")
        h2 = _norm_mod(xm, norm2_g[l:l + 1], sc2[l], sh2[l], S, f"norm2_{l}")
        gate, up, act = _swiglu_fwd(h2, wg_t[l], wu_t[l], f"swiglu_fwd_{l}")
        y2, xc = _mm_res(act, wd_f[l], xm, g2[l], S, f"mlp_out_{l}")
        sv.update(attn=attn, y1=y1, x_mid=xm, h2=h2, gate=gate, up=up, act=act, y2=y2)
        saved.append(sv)

    dx, loss_tile = _loss_head(xc, loss_target.reshape(T, D), "loss_head")

    g_qkv_a, g_qkv_b, g_gate, g_up, g_wo_a, g_wo_b, g_down = ([None] * NA, [None] * NB_, [None] * L, [None] * L,
                                                             [None] * NA, [None] * NB_, [None] * L)
    dmod = [None] * L
    dn1, dn2 = [None] * L, [None] * L
    dqg, dkg, dsink = [None] * NA, [None] * NA, [None] * NA
    for l in reversed(range(L)):
        j = l // 2
        sv = saved[l]
        dy2, dg2 = _gate_bwd(dx, sv["y2"], g2[l], S, f"gate2_bwd_{l}")
        dgate, dup = _swiglu_bwd(dy2, wd_f[l], sv["gate"], sv["up"], f"swiglu_bwd_{l}")
        g_down[l] = _mm_tn(sv["act"], dy2, f"dw_down_{l}")
        dh2 = _mm_nn([(dgate, wg_t[l]), (dup, wu_t[l])], f"dh2_{l}")
        g_gate[l] = _mm_tn(dgate, sv["h2"], f"dw_gate_{l}")
        g_up[l] = _mm_tn(dup, sv["h2"], f"dw_up_{l}")
        dxm, dsh2, dsc2, dn2[l] = _norm_mod_bwd(sv["x_mid"], dh2, dx, norm2_g[l:l + 1], sc2[l], S, f"norm2_bwd_{l}")
        dy1, dg1 = _gate_bwd(dxm, sv["y1"], g1[l], S, f"gate1_bwd_{l}")
        wo = wo_a_f[j] if l % 2 == 0 else wo_b_f[j]
        dattn = _mm_nt(dy1, wo, BF16, f"dattn_{l}")
        gwo = _mm_tn(sv["attn"], dy1, f"dw_o_{l}")
        if l % 2 == 0:
            g_wo_a[j] = gwo
            dq, dkc, dkp, dvc, dvp, dsink[j] = _swa_bwd(sv["qn"], sv["kd"], sv["vd"], sv["sink2"], dattn, sv["lse"], B,
                                                        f"swa_bwd_{l}")
            dqkv, dqg[j], dkg[j] = _qk_prep_bwd(sv["qkv"], cs, s1, s2, sv["qg"], sv["kg"], dq, dkc, dkp, dvc, dvp, B,
                                                f"qk_prep_bwd_{l}")
            wt = wqkv_a_t[j]
        else:
            g_wo_b[j] = gwo
            nqb = sv["qkv"].shape[1] // 3
            dq, dk_t, dv_t = _sb_bwd(sv["qkv"], sv["qkv"][:, :nqb].T, dattn, dattn.T, sv["tot"], B, f"sb_bwd_{l}")
            dk, dv = [t.reshape(B, nqb, S).transpose(0, 2, 1).reshape(T, nqb) for t in (dk_t, dv_t)]
            dqkv = jnp.concatenate([dq, dk, dv], axis=1).astype(BF16)
            wt = wqkv_b_t[j]
        dh1 = _mm_nn([(dqkv, wt)], f"dh1_{l}")
        gq = _mm_tn(dqkv, sv["h1"], f"dw_qkv_{l}")
        if l % 2 == 0:
            g_qkv_a[j] = gq
        else:
            g_qkv_b[j] = gq
        dx, dsh1, dsc1, dn1[l] = _norm_mod_bwd(sv["x_in"], dh1, dxm, norm1_g[l:l + 1], sc1[l], S, f"norm1_bwd_{l}")
        dmod[l] = jnp.concatenate([dsh1, dsc1, dg1, dsh2, dsc2, dg2], axis=1)
    grad_x = dx.reshape(B, S, D)

    ndm = L * 6
    dmod_rows = jnp.stack(dmod, axis=1).reshape(B * ndm, D)
    misc = jnp.concatenate(
        [jnp.concatenate(dn1, axis=0).reshape(B * L, D), jnp.concatenate(dn2, axis=0).reshape(B * L, D),
         _pad_rows(jnp.concatenate([jnp.pad(r, ((0, 0), (0, D - LANES))) for r in dqg + dkg]
                                   + [jnp.pad(r[:, 0, ::HEAD].reshape(1, -1), ((0, 0), (0, D - 2 * r.shape[0]))) for r in dsink]
                                   + [jnp.pad(loss_tile[0:1, 0:1], ((0, 0), (0, D - 1)))], axis=0), 8)], axis=0)
    nmisc = misc.shape[0]
    small = _all_gather(jnp.concatenate([dmod_rows, _pad_rows(misc, -(-nmisc // 8) * 8)], axis=0), "ag_small")
    dmod_all = small[:, :B * ndm].reshape(NDEV * B, ndm, D)
    g_ada_b = _sum_leading(dmod_all, "sum_dmod").reshape(L, 6 * D)
    misc_sum = _sum_leading(small[:, B * ndm:], "sum_misc")
    g_n1 = misc_sum[0:B * L].reshape(L, B, D)
    g_n2 = misc_sum[B * L:2 * B * L].reshape(L, B, D)
    g_norm1 = _sum_leading(g_n1.transpose(1, 0, 2), "sum_n1")
    g_norm2 = _sum_leading(g_n2.transpose(1, 0, 2), "sum_n2")
    o = 2 * B * L
    g_qn = misc_sum[o:o + NA, :HEAD]
    g_kn = misc_sum[o + NA:o + 2 * NA, :HEAD]
    nsink = sinks_a.shape[1]
    g_sink = misc_sum[o + 2 * NA:o + 3 * NA, :nsink]
    loss = misc_sum[o + 3 * NA, 0]

    dmod_loc = lax.dynamic_slice_in_dim(dmod_all.reshape(NDEV * B, L, 6 * D), me * WA, WA, axis=2)
    g_ada_w = _ada_bwd(c_all, dmod_loc.transpose(1, 0, 2), "ada_bwd")

    parts = []
    for g in g_qkv_a + g_qkv_b + g_gate + g_up + g_wo_a + g_wo_b + g_down:
        parts.append(g.reshape(NDEV, g.shape[0] // NDEV, D).astype(BF16))
    partial = jnp.concatenate(parts, axis=1)
    received = _exchange(partial, "grad_exchange")
    gsum = _sum_leading(received, "grad_sum")
    shards, off = [], 0
    for rows in layout:
        shards.append(gsum[off:off + rows])
        off += rows
    it = iter(shards)
    gw_qkv_a = jnp.stack([next(it).T for _ in range(NA)])
    gw_qkv_b = jnp.stack([next(it).T for _ in range(NB_)])
    gw_gate = jnp.stack([next(it).T for _ in range(L)])
    gw_up = jnp.stack([next(it).T for _ in range(L)])
    gw_wo_a = jnp.stack([next(it) for _ in range(NA)])
    gw_wo_b = jnp.stack([next(it) for _ in range(NB_)])
    gw_down = jnp.stack([next(it) for _ in range(L)])

    grads = [g_ada_w, g_ada_b, g_norm1, g_norm2, gw_qkv_a, g_qn, g_kn, g_sink, gw_wo_a, gw_qkv_b, gw_wo_b,
             gw_gate, gw_up, gw_down]
    ws = [ada_w, ada_b, norm1_g, norm2_g, wqkv_a, q_norm_a, k_norm_a, sinks_a, wo_a, wqkv_b, wo_b, w_gate, w_up, w_down]
    ms = [m_ada_w, m_ada_b, m_norm1_g, m_norm2_g, m_wqkv_a, m_q_norm_a, m_k_norm_a, m_sinks_a, m_wo_a, m_wqkv_b,
          m_wo_b, m_w_gate, m_w_up, m_w_down]
    vs = [v_ada_w, v_ada_b, v_norm1_g, v_norm2_g, v_wqkv_a, v_q_norm_a, v_k_norm_a, v_sinks_a, v_wo_a, v_wqkv_b,
          v_wo_b, v_w_gate, v_w_up, v_w_down]
    deltas, new_m, new_v = [], [], []
    for k, (w, g, m, v) in enumerate(zip(ws, grads, ms, vs)):
        g = g.reshape(w.shape)
        d, mn, vn = _adamw(w, g, m, v, f"adamw_{k}")
        grads[k] = g
        deltas.append(d)
        new_m.append(mn)
        new_v.append(vn)
    return (loss, grad_x, *grads, *deltas, *new_m, *new_v)
```

```python
import functools
import math

import jax
import jax.numpy as jnp
from jax import lax
from jax.experimental import pallas as pl
from jax.experimental.pallas import tpu as pltpu

F32 = jnp.float32
BF16 = jnp.bfloat16
NDEV = 8
HEAD = 64
BLK = 128
LANES = 128
EPS = 1e-6
ROT = HEAD // 4
ROPE_THETA = 500000.0
SCALE = HEAD ** -0.5
NEG = -1e30
VMEM_LIMIT = 56 * 1024 * 1024
MESH = pl.DeviceIdType.MESH
HIGH = lax.Precision.HIGHEST

ADAM_LR = 0.001
ADAM_B1 = 0.9
ADAM_B2 = 0.999
ADAM_EPS = 1e-08
ADAM_WD = 0.01
ADAM_STEP = 10


def _params(*sem):
    return pltpu.CompilerParams(dimension_semantics=sem, vmem_limit_bytes=VMEM_LIMIT)


def _pick(n, cap, mult):
    if n <= cap:
        return n
    best = None
    for t in range(mult, cap + 1, mult):
        if n % t == 0:
            best = t
    assert best is not None, (n, cap, mult)
    return best


def _dot(a, b, dims, precision=None):
    return lax.dot_general(a, b, (dims, ((), ())), preferred_element_type=F32, precision=precision)


NN = ((1,), (0,))
NT = ((1,), (1,))
TN = ((0,), (0,))


def _all_gather(x, name):
    m, n = x.shape

    def body(x_ref, out_ref, send_sems, recv_sems, local_sem):
        ix, iy, ic = lax.axis_index("x"), lax.axis_index("y"), lax.axis_index("c")
        me, sibling = (ix, iy, ic), (ix, iy, 1 - ic)
        chips = [(1 - ix, iy), (ix, 1 - iy), (1 - ix, 1 - iy)]

        def slab(px, py, pc):
            return out_ref.at[4 * px + 2 * py + pc]

        def copy(k, block, to, src=None):
            return pltpu.make_async_remote_copy(
                src_ref=slab(*block) if src is None else src, dst_ref=slab(*block),
                send_sem=send_sems.at[k], recv_sem=recv_sems.at[k], device_id=to, device_id_type=MESH)

        mine = pltpu.make_async_copy(x_ref, slab(*me), local_sem)
        mine.start()
        first = [copy(0, me, sibling, src=x_ref)]
        first += [copy(1 + j, me, (*chip, ic), src=x_ref) for j, chip in enumerate(chips)]
        for cp in first:
            cp.start()
        passed = [copy(4 + j, (*chip, ic), sibling) for j, chip in enumerate(chips)]
        for j, chip in enumerate(chips):
            copy(1 + j, (*chip, ic), me).wait_recv()
            passed[j].start()
        copy(0, sibling, me).wait_recv()
        for j, chip in enumerate(chips):
            copy(4 + j, (*chip, 1 - ic), me).wait_recv()
        for cp in first + passed:
            cp.wait_send()
        mine.wait()

    return pl.pallas_call(
        body, name=name,
        out_shape=jax.ShapeDtypeStruct((NDEV, m, n), x.dtype),
        in_specs=[pl.BlockSpec(memory_space=pl.ANY)],
        out_specs=pl.BlockSpec(memory_space=pl.ANY),
        scratch_shapes=[pltpu.SemaphoreType.DMA((7,)), pltpu.SemaphoreType.DMA((7,)), pltpu.SemaphoreType.DMA(())],
    )(x)


def _exchange(p, name):
    _, m, n = p.shape

    def body(p_ref, r_ref, send_sems, recv_sems, local_sem):
        ix, iy, ic = lax.axis_index("x"), lax.axis_index("y"), lax.axis_index("c")
        me = 4 * ix + 2 * iy + ic
        own = pltpu.make_async_copy(p_ref.at[me], r_ref.at[me], local_sem)
        own.start()
        copies = []
        for k in range(1, NDEV):
            px = 1 - ix if k & 4 else ix
            py = 1 - iy if k & 2 else iy
            pc = 1 - ic if k & 1 else ic
            cp = pltpu.make_async_remote_copy(
                src_ref=p_ref.at[4 * px + 2 * py + pc], dst_ref=r_ref.at[me],
                send_sem=send_sems.at[k - 1], recv_sem=recv_sems.at[k - 1],
                device_id=(px, py, pc), device_id_type=MESH)
            cp.start()
            copies.append(cp)
        for cp in copies:
            cp.wait()
        own.wait()

    return pl.pallas_call(
        body, name=name,
        out_shape=jax.ShapeDtypeStruct(p.shape, p.dtype),
        in_specs=[pl.BlockSpec(memory_space=pl.ANY)],
        out_specs=pl.BlockSpec(memory_space=pl.ANY),
        scratch_shapes=[pltpu.SemaphoreType.DMA((7,)), pltpu.SemaphoreType.DMA((7,)), pltpu.SemaphoreType.DMA(())],
    )(p)


def _sum_leading(r, name):
    k, m, n = r.shape
    mult = 8 * (4 // r.dtype.itemsize)
    tm = _pick(m, max(mult, (4 * 1024 * 1024) // (k * n * r.dtype.itemsize) // mult * mult), mult)

    def body(r_ref, o_ref):
        acc = r_ref[0].astype(F32)
        for s in range(1, k):
            acc = acc + r_ref[s].astype(F32)
        o_ref[...] = acc

    return pl.pallas_call(
        body, name=name, grid=(m // tm,),
        in_specs=[pl.BlockSpec((k, tm, n), lambda i: (0, i, 0))],
        out_specs=pl.BlockSpec((tm, n), lambda i: (i, 0)),
        out_shape=jax.ShapeDtypeStruct((m, n), F32),
        compiler_params=_params("parallel"),
    )(r)


def _mm_nt(a, bt, out_dtype, name):
    M, K = a.shape
    N = bt.shape[0]
    tm, tn = _pick(M, 512, 8), _pick(N, 1536, LANES)

    def body(a_ref, b_ref, o_ref):
        o_ref[...] = _dot(a_ref[...], b_ref[...], NT).astype(out_dtype)

    return pl.pallas_call(
        body, name=name, grid=(M // tm, N // tn),
        in_specs=[pl.BlockSpec((tm, K), lambda i, j: (i, 0)), pl.BlockSpec((tn, K), lambda i, j: (j, 0))],
        out_specs=pl.BlockSpec((tm, tn), lambda i, j: (i, j)),
        out_shape=jax.ShapeDtypeStruct((M, N), out_dtype),
        compiler_params=_params("parallel", "parallel"),
    )(a, bt)


def _mm_nn(pairs, name):
    M = pairs[0][0].shape[0]
    N = pairs[0][1].shape[1]
    tm, tn = _pick(M, 512, 8), _pick(N, 1024, LANES)
    np_ = len(pairs)

    def body(*refs):
        o_ref = refs[-1]
        acc = _dot(refs[0][...], refs[1][...], NN)
        for p in range(1, np_):
            acc = acc + _dot(refs[2 * p][...], refs[2 * p + 1][...], NN)
        o_ref[...] = acc

    in_specs, args = [], []
    for a, b in pairs:
        K = a.shape[1]
        in_specs += [pl.BlockSpec((tm, K), lambda i, j: (i, 0)), pl.BlockSpec((K, tn), lambda i, j: (0, j))]
        args += [a, b]
    return pl.pallas_call(
        body, name=name, grid=(M // tm, N // tn),
        in_specs=in_specs,
        out_specs=pl.BlockSpec((tm, tn), lambda i, j: (i, j)),
        out_shape=jax.ShapeDtypeStruct((M, N), F32),
        compiler_params=_params("parallel", "parallel"),
    )(*args)


def _mm_tn(a, b, name):
    M, N1 = a.shape
    N2 = b.shape[1]
    t1, tk = _pick(N1, 1536, LANES), _pick(M, 512, 8)

    def body(a_ref, b_ref, o_ref):
        @pl.when(pl.program_id(1) == 0)
        def _():
            o_ref[...] = jnp.zeros_like(o_ref)
        o_ref[...] += _dot(a_ref[...], b_ref[...], TN)

    return pl.pallas_call(
        body, name=name, grid=(N1 // t1, M // tk),
        in_specs=[pl.BlockSpec((tk, t1), lambda i, k: (k, i)), pl.BlockSpec((tk, N2), lambda i, k: (k, 0))],
        out_specs=pl.BlockSpec((t1, N2), lambda i, k: (i, 0)),
        out_shape=jax.ShapeDtypeStruct((N1, N2), F32),
        compiler_params=_params("parallel", "arbitrary"),
    )(a, b)


def _mm_res(a, w, x, gate, S, name):
    T, K = a.shape
    D = w.shape[1]
    tm, tn = _pick(S, 512, 8), _pick(D, 512, LANES)
    nb = S // tm

    def body(a_ref, w_ref, x_ref, g_ref, y_ref, o_ref):
        y = _dot(a_ref[...], w_ref[...], NN)
        y_ref[...] = y
        o_ref[...] = x_ref[...] + g_ref[0] * y

    return pl.pallas_call(
        body, name=name, grid=(T // tm, D // tn),
        in_specs=[pl.BlockSpec((tm, K), lambda i, j: (i, 0)), pl.BlockSpec((K, tn), lambda i, j: (0, j)),
                  pl.BlockSpec((tm, tn), lambda i, j: (i, j)), pl.BlockSpec((1, 1, tn), lambda i, j: (i // nb, 0, j))],
        out_specs=[pl.BlockSpec((tm, tn), lambda i, j: (i, j)), pl.BlockSpec((tm, tn), lambda i, j: (i, j))],
        out_shape=[jax.ShapeDtypeStruct((T, D), F32), jax.ShapeDtypeStruct((T, D), F32)],
        compiler_params=_params("parallel", "parallel"),
    )(a, w, x, gate)


def _swiglu_fwd(h, wgt, wut, name):
    T, D = h.shape
    F = wgt.shape[0]
    tm, tn = _pick(T, 512, 8), _pick(F, 1536, LANES)

    def body(h_ref, g_ref, u_ref, go_ref, uo_ref, a_ref):
        hh = h_ref[...]
        g = _dot(hh, g_ref[...], NT)
        u = _dot(hh, u_ref[...], NT)
        go_ref[...] = g
        uo_ref[...] = u
        a_ref[...] = (g * jax.nn.sigmoid(g) * u).astype(BF16)

    spec_w = pl.BlockSpec((tn, D), lambda i, j: (j, 0))
    spec_o = pl.BlockSpec((tm, tn), lambda i, j: (i, j))
    return pl.pallas_call(
        body, name=name, grid=(T // tm, F // tn),
        in_specs=[pl.BlockSpec((tm, D), lambda i, j: (i, 0)), spec_w, spec_w],
        out_specs=[spec_o, spec_o, spec_o],
        out_shape=[jax.ShapeDtypeStruct((T, F), F32), jax.ShapeDtypeStruct((T, F), F32),
                   jax.ShapeDtypeStruct((T, F), BF16)],
        compiler_params=_params("parallel", "parallel"),
    )(h, wgt, wut)


def _swiglu_bwd(dy, wd, gate, up, name):
    T, D = dy.shape
    F = wd.shape[0]
    tm, tn = _pick(T, 512, 8), _pick(F, 1536, LANES)

    def body(dy_ref, w_ref, g_ref, u_ref, dg_ref, du_ref):
        da = _dot(dy_ref[...], w_ref[...], NT)
        g = g_ref[...]
        sg = jax.nn.sigmoid(g)
        silu = g * sg
        du_ref[...] = (da * silu).astype(BF16)
        dg_ref[...] = (da * u_ref[...] * (sg + silu * (1.0 - sg))).astype(BF16)

    spec_o = pl.BlockSpec((tm, tn), lambda i, j: (i, j))
    return pl.pallas_call(
        body, name=name, grid=(T // tm, F // tn),
        in_specs=[pl.BlockSpec((tm, D), lambda i, j: (i, 0)), pl.BlockSpec((tn, D), lambda i, j: (j, 0)), spec_o, spec_o],
        out_specs=[spec_o, spec_o],
        out_shape=[jax.ShapeDtypeStruct((T, F), BF16), jax.ShapeDtypeStruct((T, F), BF16)],
        compiler_params=_params("parallel", "parallel"),
    )(dy, wd, gate, up)


def _norm_mod(x, gain, sc, sh, S, name):
    T, D = x.shape
    tm = _pick(S, 512, 8)
    nb = S // tm

    def body(x_ref, g_ref, sc_ref, sh_ref, o_ref):
        xv = x_ref[...]
        r = lax.rsqrt(jnp.mean(xv * xv, axis=-1, keepdims=True) + EPS)
        o_ref[...] = ((xv * r) * g_ref[...] * (1.0 + sc_ref[0]) + sh_ref[0]).astype(BF16)

    spec_b = pl.BlockSpec((1, 1, D), lambda i: (i // nb, 0, 0))
    return pl.pallas_call(
        body, name=name, grid=(T // tm,),
        in_specs=[pl.BlockSpec((tm, D), lambda i: (i, 0)), pl.BlockSpec((1, D), lambda i: (0, 0)), spec_b, spec_b],
        out_specs=pl.BlockSpec((tm, D), lambda i: (i, 0)),
        out_shape=jax.ShapeDtypeStruct((T, D), BF16),
        compiler_params=_params("parallel"),
    )(x, gain, sc, sh)


def _norm_mod_bwd(x, dh, dres, gain, sc, S, name):
    T, D = x.shape
    B = T // S
    tm = _pick(S, 256, 8)
    nb = S // tm

    def body(x_ref, dh_ref, dr_ref, g_ref, sc_ref, o_ref, dsh_ref, dsc_ref, dg_ref):
        @pl.when(pl.program_id(1) == 0)
        def _():
            dsh_ref[...] = jnp.zeros_like(dsh_ref)
            dsc_ref[...] = jnp.zeros_like(dsc_ref)
            dg_ref[...] = jnp.zeros_like(dg_ref)
        xv, dhv, g = x_ref[...], dh_ref[...], g_ref[...]
        r = lax.rsqrt(jnp.mean(xv * xv, axis=-1, keepdims=True) + EPS)
        xhat = xv * r
        dsh_ref[0] += jnp.sum(dhv, axis=0, keepdims=True)
        dsc_ref[0] += jnp.sum(dhv * (xhat * g), axis=0, keepdims=True)
        dn = dhv * (1.0 + sc_ref[0])
        dg_ref[0] += jnp.sum(dn * xhat, axis=0, keepdims=True)
        dxh = dn * g
        o_ref[...] = dr_ref[...] + r * (dxh - xhat * jnp.mean(dxh * xhat, axis=-1, keepdims=True))

    spec_t = pl.BlockSpec((tm, D), lambda b, i: (b * nb + i, 0))
    spec_b = pl.BlockSpec((1, 1, D), lambda b, i: (b, 0, 0))
    red = jax.ShapeDtypeStruct((B, 1, D), F32)
    return pl.pallas_call(
        body, name=name, grid=(B, nb),
        in_specs=[spec_t, spec_t, spec_t, pl.BlockSpec((1, D), lambda b, i: (0, 0)), spec_b],
        out_specs=[spec_t, spec_b, spec_b, spec_b],
        out_shape=[jax.ShapeDtypeStruct((T, D), F32), red, red, red],
        compiler_params=_params("parallel", "arbitrary"),
    )(x, dh, dres, gain, sc)


def _gate_bwd(dx, y, gate, S, name):
    T, D = dx.shape
    B = T // S
    tm = _pick(S, 512, 8)
    nb = S // tm

    def body(dx_ref, y_ref, g_ref, dy_ref, dg_ref):
        @pl.when(pl.program_id(1) == 0)
        def _():
            dg_ref[...] = jnp.zeros_like(dg_ref)
        d = dx_ref[...]
        dy_ref[...] = (d * g_ref[0]).astype(BF16)
        dg_ref[0] += jnp.sum(d * y_ref[...], axis=0, keepdims=True)

    spec_t = pl.BlockSpec((tm, D), lambda b, i: (b * nb + i, 0))
    spec_b = pl.BlockSpec((1, 1, D), lambda b, i: (b, 0, 0))
    return pl.pallas_call(
        body, name=name, grid=(B, nb),
        in_specs=[spec_t, spec_t, spec_b],
        out_specs=[spec_t, spec_b],
        out_shape=[jax.ShapeDtypeStruct((T, D), BF16), jax.ShapeDtypeStruct((B, 1, D), F32)],
        compiler_params=_params("parallel", "arbitrary"),
    )(dx, y, gate)


def _loss_head(y, target, name):
    T, D = y.shape
    tm = _pick(T, 512, 8)

    def body(y_ref, t_ref, dy_ref, l_ref):
        @pl.when(pl.program_id(0) == 0)
        def _():
            l_ref[...] = jnp.zeros_like(l_ref)
        e = y_ref[...] - t_ref[...]
        dy_ref[...] = e * (1.0 / D)
        l_ref[...] += 0.5 * jnp.sum(jnp.mean(e * e, axis=-1, keepdims=True), axis=0, keepdims=True)

    spec = pl.BlockSpec((tm, D), lambda i: (i, 0))
    return pl.pallas_call(
        body, name=name, grid=(T // tm,),
        in_specs=[spec, spec],
        out_specs=[spec, pl.BlockSpec((8, LANES), lambda i: (0, 0))],
        out_shape=[jax.ShapeDtypeStruct((T, D), F32), jax.ShapeDtypeStruct((8, LANES), F32)],
        compiler_params=_params("arbitrary"),
    )(y, target)


def _ada_fwd(c_all, ada_w, bias, name):
    NB, D = c_all.shape
    L, _, W = ada_w.shape

    def body(c_ref, w_ref, b_ref, o_ref):
        cv = c_ref[...]
        cond = cv * jax.nn.sigmoid(cv)
        o_ref[0] = _dot(cond, w_ref[0], NN, HIGH) + b_ref[0]

    return pl.pallas_call(
        body, name=name, grid=(L,),
        in_specs=[pl.BlockSpec((NB, D), lambda l: (0, 0)), pl.BlockSpec((1, D, W), lambda l: (l, 0, 0)),
                  pl.BlockSpec((1, 1, W), lambda l: (l, 0, 0))],
        out_specs=pl.BlockSpec((1, NB, W), lambda l: (l, 0, 0)),
        out_shape=jax.ShapeDtypeStruct((L, NB, W), F32),
        compiler_params=_params("parallel"),
    )(c_all, ada_w, bias)


def _ada_bwd(c_all, dmod, name):
    NB, D = c_all.shape
    L, _, W = dmod.shape

    def body(c_ref, d_ref, o_ref):
        cv = c_ref[...]
        cond = cv * jax.nn.sigmoid(cv)
        o_ref[0] = _dot(cond, d_ref[0], TN, HIGH)

    return pl.pallas_call(
        body, name=name, grid=(L,),
        in_specs=[pl.BlockSpec((NB, D), lambda l: (0, 0)), pl.BlockSpec((1, NB, W), lambda l: (l, 0, 0))],
        out_specs=pl.BlockSpec((1, D, W), lambda l: (l, 0, 0)),
        out_shape=jax.ShapeDtypeStruct((L, D, W), F32),
        compiler_params=_params("parallel"),
    )(c_all, dmod)


def _lo_mask(shape):
    return lax.broadcasted_iota(jnp.int32, shape, len(shape) - 1) < HEAD


def _head_sum_matrix():
    r = lax.broadcasted_iota(jnp.int32, (LANES, LANES), 0) // HEAD
    c = lax.broadcasted_iota(jnp.int32, (LANES, LANES), 1) // HEAD
    return (r == c).astype(F32)


def _rope(y, cs, s1, s2):
    return y * cs + pltpu.roll(y, LANES - ROT // 2, 1) * s1 + pltpu.roll(y, ROT // 2, 1) * s2


def _rope_bwd(d, cs, s1, s2):
    return d * cs + pltpu.roll(d * s1, ROT // 2, 1) + pltpu.roll(d * s2, LANES - ROT // 2, 1)


def _qk_prep(qkv, cs, s1, s2, qg, kg, name):
    T, W = qkv.shape
    NQ = W - 2 * LANES
    tm = _pick(T, 512, 8)

    def body(x_ref, cs_ref, s1_ref, s2_ref, qg_ref, kg_ref, q_ref, k_ref, v_ref):
        P = _head_sum_matrix()
        cs_, s1_, s2_ = cs_ref[...], s1_ref[...], s2_ref[...]
        lo = _lo_mask((tm, LANES))

        def norm_rope(xv, g):
            ms = _dot(xv * xv, P, NN, HIGH) * (1.0 / HEAD)
            return _rope(xv * lax.rsqrt(ms + EPS) * g, cs_, s1_, s2_)

        for j in range(NQ // LANES):
            q_ref[:, j * LANES:(j + 1) * LANES] = norm_rope(x_ref[:, j * LANES:(j + 1) * LANES], qg_ref[...]).astype(BF16)
        kr = norm_rope(x_ref[:, NQ:NQ + LANES], kg_ref[...])
        ks = pltpu.roll(kr, HEAD, 1)
        k_ref[:, :LANES] = jnp.where(lo, kr, ks).astype(BF16)
        k_ref[:, LANES:] = jnp.where(lo, ks, kr).astype(BF16)
        vr = x_ref[:, NQ + LANES:]
        vs = pltpu.roll(vr, HEAD, 1)
        v_ref[:, :LANES] = jnp.where(lo, vr, vs).astype(BF16)
        v_ref[:, LANES:] = jnp.where(lo, vs, vr).astype(BF16)

    spec_t = pl.BlockSpec((tm, LANES), lambda i: (i, 0))
    spec_g = pl.BlockSpec((1, LANES), lambda i: (0, 0))
    return pl.pallas_call(
        body, name=name, grid=(T // tm,),
        in_specs=[pl.BlockSpec((tm, W), lambda i: (i, 0)), spec_t, spec_t, spec_t, spec_g, spec_g],
        out_specs=[pl.BlockSpec((tm, NQ), lambda i: (i, 0)), pl.BlockSpec((tm, 2 * LANES), lambda i: (i, 0)),
                   pl.BlockSpec((tm, 2 * LANES), lambda i: (i, 0))],
        out_shape=[jax.ShapeDtypeStruct((T, NQ), BF16), jax.ShapeDtypeStruct((T, 2 * LANES), BF16),
                   jax.ShapeDtypeStruct((T, 2 * LANES), BF16)],
        compiler_params=_params("parallel"),
    )(qkv, cs, s1, s2, qg, kg)


def _stack_heads(x2):
    lo = _lo_mask(x2.shape)
    z = jnp.zeros_like(x2)
    return jnp.concatenate([jnp.where(lo, x2, z), jnp.where(lo, z, x2)], axis=0)


def _unstack_heads(xs):
    r = xs.shape[0] // 2
    return jnp.where(_lo_mask((r, LANES)), xs[:r], xs[r:])


def _swa_valid(i):
    qo = lax.broadcasted_iota(jnp.int32, (2 * BLK, 2 * BLK), 0) % BLK
    kc_ = lax.broadcasted_iota(jnp.int32, (2 * BLK, 2 * BLK), 1)
    rel = qo + BLK - kc_
    return (rel >= 0) & (rel < BLK) & ((kc_ >= BLK) | (i > 0))


def _swa_scores(q2, kk, sink2, valid):
    qs = _stack_heads(q2) * SCALE
    s = _dot(qs, kk, NT)
    sk = jnp.concatenate([jnp.broadcast_to(sink2[:, 0:1], (BLK, 1)), jnp.broadcast_to(sink2[:, HEAD:HEAD + 1], (BLK, 1))], axis=0)
    return qs, jnp.where(valid, s, NEG), sk


def _swa_fwd(q, kd, vd, sink2, B, name):
    T, NQ = q.shape
    NP = NQ // LANES
    nq = T // B // BLK
    NG = kd.shape[1] // LANES
    grp = NP // NG

    def body(q_ref, kp_ref, kc_ref, vp_ref, vc_ref, s_ref, o_ref, l_ref):
        valid = _swa_valid(pl.program_id(2))
        kk = jnp.concatenate([kp_ref[...], kc_ref[...]], axis=0)
        vs = _stack_heads(jnp.concatenate([vp_ref[...], vc_ref[...]], axis=0))
        for jj in range(grp):
            sl = slice(jj * LANES, (jj + 1) * LANES)
            _, s, sk = _swa_scores(q_ref[:, sl], kk, s_ref[jj], valid)
            m = jnp.maximum(jnp.max(s, axis=1, keepdims=True), sk)
            p = jnp.where(valid, jnp.exp(s - m), 0.0)
            l = jnp.sum(p, axis=1, keepdims=True) + jnp.exp(sk - m)
            p = (p * (1.0 / l)).astype(BF16)
            p2 = jnp.concatenate([p[:BLK], p[BLK:]], axis=1)
            o_ref[:, sl] = _dot(p2, vs, NN).astype(BF16)
            l_ref[:, sl] = _unstack_heads(jnp.broadcast_to(m + jnp.log(l), (2 * BLK, LANES)))

    spec_q = pl.BlockSpec((BLK, grp * LANES), lambda b, g, i: (b * nq + i, g))
    spec_p = pl.BlockSpec((BLK, LANES), lambda b, g, i: (b * nq + jnp.maximum(i - 1, 0), g))
    spec_c = pl.BlockSpec((BLK, LANES), lambda b, g, i: (b * nq + i, g))
    return pl.pallas_call(
        body, name=name, grid=(B, NG, nq),
        in_specs=[spec_q, spec_p, spec_c, spec_p, spec_c, pl.BlockSpec((grp, 1, LANES), lambda b, g, i: (g, 0, 0))],
        out_specs=[spec_q, spec_q],
        out_shape=[jax.ShapeDtypeStruct((T, NQ), BF16), jax.ShapeDtypeStruct((T, NQ), F32)],
        compiler_params=_params("parallel", "parallel", "parallel"),
    )(q, kd, kd, vd, vd, sink2)


def _swa_bwd(q, kd, vd, sink2, do, lse, B, name):
    T, NQ = q.shape
    NP = NQ // LANES
    nq = T // B // BLK
    NG = kd.shape[1] // LANES
    grp = NP // NG

    def body(q_ref, kp_ref, kc_ref, vp_ref, vc_ref, s_ref, do_ref, l_ref,
             dq_ref, dkc_ref, dkp_ref, dvc_ref, dvp_ref, ds_ref):
        b, i = pl.program_id(1), pl.program_id(2)

        @pl.when((b == 0) & (i == 0))
        def _():
            ds_ref[...] = jnp.zeros_like(ds_ref)
        valid = _swa_valid(i)
        kk = jnp.concatenate([kp_ref[...], kc_ref[...]], axis=0)
        vv = jnp.concatenate([vp_ref[...], vc_ref[...]], axis=0)
        dk = jnp.zeros((2 * BLK, LANES), F32)
        dv = jnp.zeros((2 * BLK, LANES), F32)
        for jj in range(grp):
            sl = slice(jj * LANES, (jj + 1) * LANES)
            qs, s, sk = _swa_scores(q_ref[:, sl], kk, s_ref[jj], valid)
            lse_ = l_ref[:, sl]
            lse_s = jnp.concatenate([lse_[:, 0:1], lse_[:, HEAD:HEAD + 1]], axis=0)
            p = jnp.where(valid, jnp.exp(s - lse_s), 0.0)
            dos = _stack_heads(do_ref[:, sl])
            dp = _dot(dos, vv, NT)
            delta = jnp.sum(p * dp, axis=1, keepdims=True)
            dsc = (p * (dp - delta)).astype(BF16)
            dsk = -jnp.exp(sk - lse_s) * delta
            dsk_lo = jnp.sum(dsk[:BLK], axis=0, keepdims=True)
            dsk_hi = jnp.sum(dsk[BLK:], axis=0, keepdims=True)
            ds_ref[jj] += jnp.where(_lo_mask((1, LANES)), dsk_lo, dsk_hi)
            dq_ref[:, sl] = _unstack_heads(_dot(dsc, kk, NN)) * SCALE
            dk = dk + _dot(dsc, qs, TN)
            dv = dv + _dot(p.astype(BF16), dos, TN)
        dkp_ref[...] = dk[:BLK]
        dkc_ref[...] = dk[BLK:]
        dvp_ref[...] = dv[:BLK]
        dvc_ref[...] = dv[BLK:]

    spec_q = pl.BlockSpec((BLK, grp * LANES), lambda g, b, i: (b * nq + i, g))
    spec_p = pl.BlockSpec((BLK, LANES), lambda g, b, i: (b * nq + jnp.maximum(i - 1, 0), g))
    spec_c = pl.BlockSpec((BLK, LANES), lambda g, b, i: (b * nq + i, g))
    spec_s = pl.BlockSpec((grp, 1, LANES), lambda g, b, i: (g, 0, 0))
    kv = jax.ShapeDtypeStruct((T, NG * LANES), F32)
    return pl.pallas_call(
        body, name=name, grid=(NG, B, nq),
        in_specs=[spec_q, spec_p, spec_c, spec_p, spec_c, spec_s, spec_q, spec_q],
        out_specs=[spec_q, spec_c, spec_c, spec_c, spec_c, spec_s],
        out_shape=[jax.ShapeDtypeStruct((T, NQ), F32), kv, kv, kv, kv, jax.ShapeDtypeStruct((NP, 1, LANES), F32)],
        compiler_params=_params("arbitrary", "arbitrary", "arbitrary"),
    )(q, kd, kd, vd, vd, sink2, do, lse)


def _qk_prep_bwd(qkv, cs, s1, s2, qg, kg, dq, dkc, dkp, dvc, dvp, B, name):
    T, W = qkv.shape
    NQ = W - 2 * LANES
    NP = NQ // LANES
    nq = T // B // BLK

    def body(x_ref, cs_ref, s1_ref, s2_ref, qg_ref, kg_ref, dq_ref, dkc_ref, dkp_ref, dvc_ref, dvp_ref,
             o_ref, dqg_ref, dkg_ref):
        b, i = pl.program_id(0), pl.program_id(1)

        @pl.when((b == 0) & (i == 0))
        def _():
            dqg_ref[...] = jnp.zeros_like(dqg_ref)
            dkg_ref[...] = jnp.zeros_like(dkg_ref)
        P = _head_sum_matrix()
        cs_, s1_, s2_ = cs_ref[...], s1_ref[...], s2_ref[...]
        lo = _lo_mask((BLK, LANES))
        has_next = (i + 1 < nq).astype(F32)

        def norm_rope_bwd(xv, g, d):
            du = _rope_bwd(d, cs_, s1_, s2_)
            r = lax.rsqrt(_dot(xv * xv, P, NN, HIGH) * (1.0 / HEAD) + EPS)
            xhat = xv * r
            dgain = jnp.sum(du * xhat, axis=0, keepdims=True)
            uu = du * g
            dx = r * (uu - xhat * (_dot(uu * xhat, P, NN, HIGH) * (1.0 / HEAD)))
            return dx, dgain + pltpu.roll(dgain, HEAD, 1)

        dqg = jnp.zeros((1, LANES), F32)
        for j in range(NP):
            sl = slice(j * LANES, (j + 1) * LANES)
            dx, dg = norm_rope_bwd(x_ref[:, sl], qg_ref[...], dq_ref[:, sl])
            o_ref[:, sl] = dx.astype(BF16)
            dqg = dqg + dg
        dqg_ref[...] += dqg

        def fold(c_ref, p_ref, g):
            sl = slice(g * LANES, (g + 1) * LANES)
            t = c_ref[:, sl] + has_next * p_ref[:, sl]
            return t + pltpu.roll(t, HEAD, 1)

        dk = jnp.where(lo, fold(dkc_ref, dkp_ref, 0), fold(dkc_ref, dkp_ref, 1))
        dx, dg = norm_rope_bwd(x_ref[:, NQ:NQ + LANES], kg_ref[...], dk)
        o_ref[:, NQ:NQ + LANES] = dx.astype(BF16)
        dkg_ref[...] += dg
        dv = jnp.where(lo, fold(dvc_ref, dvp_ref, 0), fold(dvc_ref, dvp_ref, 1))
        o_ref[:, NQ + LANES:] = dv.astype(BF16)

    spec_t = pl.BlockSpec((BLK, LANES), lambda b, i: (b * nq + i, 0))
    spec_g = pl.BlockSpec((1, LANES), lambda b, i: (0, 0))
    spec_c = pl.BlockSpec((BLK, 2 * LANES), lambda b, i: (b * nq + i, 0))
    spec_n = pl.BlockSpec((BLK, 2 * LANES), lambda b, i: (b * nq + jnp.minimum(i + 1, nq - 1), 0))
    row = jax.ShapeDtypeStruct((1, LANES), F32)
    return pl.pallas_call(
        body, name=name, grid=(B, nq),
        in_specs=[pl.BlockSpec((BLK, W), lambda b, i: (b * nq + i, 0)), spec_t, spec_t, spec_t, spec_g, spec_g,
                  pl.BlockSpec((BLK, NQ), lambda b, i: (b * nq + i, 0)), spec_c, spec_n, spec_c, spec_n],
        out_specs=[pl.BlockSpec((BLK, W), lambda b, i: (b * nq + i, 0)), spec_g, spec_g],
        out_shape=[jax.ShapeDtypeStruct((T, W), BF16), row, row],
        compiler_params=_params("arbitrary", "arbitrary"),
    )(qkv, cs, s1, s2, qg, kg, dq, dkc, dkp, dvc, dvp)


SB_TILE = 256
SB_UNROLL = 4
SB_UNROLL_BWD = 2


def _split_heads(x2, scale=None):
    lo = _lo_mask(x2.shape)
    z = jnp.zeros_like(x2)
    if scale is not None:
        x2 = x2 * scale
    return jnp.where(lo, x2, z), jnp.where(lo, z, x2)


def _sb_terms(qh, kj, diagonal):
    z = _dot(qh, kj, NT)
    e = jnp.exp(-jnp.abs(z))
    lb = jnp.minimum(z, 0.0) - jnp.log(1.0 + e)
    L = lb - z
    if not diagonal:
        return lb, L, None, z, e
    strict = lax.broadcasted_iota(jnp.int32, z.shape, 1) < lax.broadcasted_iota(jnp.int32, z.shape, 0)
    return lb, jnp.where(strict, L, 0.0), strict, z, e


def _tri(n, cmp):
    r = lax.broadcasted_iota(jnp.int32, (n, n), 0)
    c = lax.broadcasted_iota(jnp.int32, (n, n), 1)
    return cmp(r, c).astype(BF16)


def _sb_fwd(qkv, B, name):
    T, W = qkv.shape
    NQ = W // 3
    NP = NQ // LANES
    S = T // B
    tq = min(SB_TILE, S)
    nq = S // tq

    def body(q_ref, k_ref, v_ref, o_ref, t_ref):
        i = pl.program_id(2)
        qh = _split_heads(q_ref[...], SCALE)
        U = _tri(tq, lambda r, c: r > c)

        def sweep(tiles, cs, acc):
            chains = [(t, h) for t in range(len(tiles)) for h in range(2)]
            rows = [pl.ds(pl.multiple_of(j * tq, tq), tq) for j, _ in tiles]
            ks = [k_ref[r, :] for r in rows]
            vs = [_split_heads(v_ref[r, :]) for r in rows]
            terms = {(t, h): _sb_terms(qh[h], ks[t], tiles[t][1]) for t, h in chains}
            carry = {}
            for h in range(2):
                c = cs[h]
                for t in range(len(tiles)):
                    carry[t, h] = c
                    c = c + jnp.sum(terms[t, h][1], axis=1, keepdims=True)
                cs = cs[:h] + (c,) + cs[h + 1:]
            cum = {ch: _dot(terms[ch][1].astype(BF16), U, NN) for ch in chains}
            for ch in chains:
                a = jnp.exp(terms[ch][0] + (cum[ch] + carry[ch]))
                if tiles[ch[0]][1]:
                    a = jnp.where(terms[ch][2], a, 0.0)
                acc = acc + _dot(a.astype(BF16), vs[ch[0]][ch[1]], NN)
            return cs, acc

        zero = jnp.zeros((tq, 1), F32)
        carry = sweep([(i, True)], (zero, zero), jnp.zeros((tq, LANES), F32))

        def run(first, count):
            return lambda cr: sweep([(first - t, False) for t in range(count)], *cr)

        carry = lax.fori_loop(0, i // SB_UNROLL, lambda n, cr: run(i - 1 - SB_UNROLL * n, SB_UNROLL)(cr), carry)
        done = i // SB_UNROLL * SB_UNROLL
        u = SB_UNROLL // 2
        while u:
            carry = lax.cond((i & u) != 0, run(i - 1 - done, u), lambda cr: cr, carry)
            done = done + (i & u)
            u //= 2
        cs, acc = carry
        o_ref[...] = acc.astype(BF16)
        t_ref[...] = jnp.where(_lo_mask((tq, LANES)), cs[0], cs[1])

    spec_q = pl.BlockSpec((tq, LANES), lambda b, p, i: (b * nq + i, p))
    return pl.pallas_call(
        body, name=name, grid=(B, NP, nq),
        in_specs=[spec_q, pl.BlockSpec((S, LANES), lambda b, p, i: (b, NP + p)),
                  pl.BlockSpec((S, LANES), lambda b, p, i: (b, 2 * NP + p))],
        out_specs=[spec_q, spec_q],
        out_shape=[jax.ShapeDtypeStruct((T, NQ), BF16), jax.ShapeDtypeStruct((T, NQ), F32)],
        compiler_params=_params("parallel", "parallel", "arbitrary"),
    )(qkv, qkv, qkv)


def _sb_bwd(qkv, q_t, do, do_t, tot, B, name):
    T, W = qkv.shape
    NQ = W // 3
    NP = NQ // LANES
    S = T // B
    tq = min(SB_TILE, S)
    nq = S // tq

    def body(q_ref, k_ref, v_ref, do_ref, qt_ref, dot_ref, t_ref, dq_ref, dk_ref, dv_ref):
        i = pl.program_id(2)

        @pl.when(i == 0)
        def _():
            dk_ref[...] = jnp.zeros_like(dk_ref)
            dv_ref[...] = jnp.zeros_like(dv_ref)
        qh = _split_heads(q_ref[...], SCALE)
        doh = _split_heads(do_ref[...])
        top = lax.broadcasted_iota(jnp.int32, (LANES, tq), 0) < HEAD
        zt = jnp.zeros((LANES, tq), BF16)
        qt = qt_ref[...] * SCALE
        qth = (jnp.where(top, qt, zt), jnp.where(top, zt, qt))
        doth = (jnp.where(top, dot_ref[...], zt), jnp.where(top, zt, dot_ref[...]))
        tt = t_ref[...]
        tot = (tt[:, 0:1], tt[:, HEAD:HEAD + 1])
        Urev = _tri(tq, lambda r, c: r > c)
        Uexc = _tri(tq, lambda r, c: r < c)

        def sweep(tiles, carry):
            nt = len(tiles)
            chains = [(t, h) for t in range(nt) for h in range(2)]
            rows = [pl.ds(pl.multiple_of(j * tq, tq), tq) for j, _ in tiles]
            ks = [k_ref[r, :] for r in rows]
            vs = [v_ref[r, :] for r in rows]
            terms = {(t, h): _sb_terms(qh[h], ks[t], tiles[t][1]) for t, h in chains}
            da = {(t, h): _dot(doh[h], vs[t], NT) for t, h in chains}
            cc = [carry[h][0] for h in range(2)]
            later = {}
            for t, h in chains:
                cc[h] = cc[h] + jnp.sum(terms[t, h][1], axis=1, keepdims=True)
                later[t, h] = tot[h] - cc[h]
            cum = {ch: _dot(terms[ch][1].astype(BF16), Urev, NN) for ch in chains}
            a, g, before = {}, {}, {}
            cg = [carry[h][1] for h in range(2)]
            for ch in chains:
                a[ch] = jnp.exp(terms[ch][0] + (cum[ch] + later[ch]))
                if tiles[ch[0]][1]:
                    a[ch] = jnp.where(terms[ch][2], a[ch], 0.0)
                g[ch] = a[ch] * da[ch]
                before[ch] = cg[ch[1]]
                cg[ch[1]] = cg[ch[1]] + jnp.sum(g[ch], axis=1, keepdims=True)
            G = {ch: _dot(g[ch].astype(BF16), Uexc, NN) for ch in chains}
            dz = {}
            for ch in chains:
                d = g[ch] - jnp.exp(terms[ch][0]) * (g[ch] + (G[ch] + before[ch]))
                if tiles[ch[0]][1]:
                    d = jnp.where(terms[ch][2], d, 0.0)
                dz[ch] = d.astype(BF16)
            dq = [carry[h][2] for h in range(2)]
            for t, h in chains:
                dq[h] = dq[h] + _dot(dz[t, h], ks[t], NN)
            for t in range(nt):
                dk_ref[:, rows[t]] += _dot(qth[0], dz[t, 0], NN) + _dot(qth[1], dz[t, 1], NN)
                dv_ref[:, rows[t]] += _dot(doth[0], a[t, 0].astype(BF16), NN) + _dot(doth[1], a[t, 1].astype(BF16), NN)
            return tuple((cc[h], cg[h], dq[h]) for h in range(2))

        zero = jnp.zeros((tq, 1), F32)
        zq = jnp.zeros((tq, LANES), F32)
        def run(first, count):
            return lambda cr: sweep([(first + t, False) for t in range(count)], cr)

        carry = lax.fori_loop(0, i // SB_UNROLL_BWD, lambda n, cr: run(SB_UNROLL_BWD * n, SB_UNROLL_BWD)(cr),
                              ((zero, zero, zq), (zero, zero, zq)))
        done = i // SB_UNROLL_BWD * SB_UNROLL_BWD
        u = SB_UNROLL_BWD // 2
        while u:
            carry = lax.cond((i & u) != 0, run(done, u), lambda cr: cr, carry)
            done = done + (i & u)
            u //= 2
        carry = sweep([(i, True)], carry)
        dq_ref[...] = jnp.where(_lo_mask((tq, LANES)), carry[0][2], carry[1][2]) * SCALE

    spec_q = pl.BlockSpec((tq, LANES), lambda b, p, i: (b * nq + i, p))
    spec_t = pl.BlockSpec((LANES, tq), lambda b, p, i: (p, b * nq + i))
    spec_s = pl.BlockSpec((LANES, S), lambda b, p, i: (b * NP + p, 0))
    key_side = jax.ShapeDtypeStruct((B * NQ, S), F32)
    return pl.pallas_call(
        body, name=name, grid=(B, NP, nq),
        in_specs=[spec_q, pl.BlockSpec((S, LANES), lambda b, p, i: (b, NP + p)),
                  pl.BlockSpec((S, LANES), lambda b, p, i: (b, 2 * NP + p)), spec_q, spec_t, spec_t, spec_q],
        out_specs=[spec_q, spec_s, spec_s],
        out_shape=[jax.ShapeDtypeStruct((T, NQ), F32), key_side, key_side],
        compiler_params=_params("parallel", "parallel", "arbitrary"),
    )(qkv, qkv, qkv, do, q_t, do_t, tot)


def _adamw(w, g, m, v, name):
    shape = w.shape
    cols = shape[-1]
    rows = math.prod(shape[:-1])
    tr = _pick(rows, max(8, (1 << 19) // max(cols, LANES) // 8 * 8), 8)

    def body(w_ref, g_ref, m_ref, v_ref, d_ref, mo_ref, vo_ref):
        gv = g_ref[...]
        mn = ADAM_B1 * m_ref[...] + (1.0 - ADAM_B1) * gv
        vn = ADAM_B2 * v_ref[...] + (1.0 - ADAM_B2) * (gv * gv)
        m_hat = mn / (1.0 - ADAM_B1 ** ADAM_STEP)
        v_hat = vn / (1.0 - ADAM_B2 ** ADAM_STEP)
        d_ref[...] = -ADAM_LR * (m_hat / (jnp.sqrt(v_hat) + ADAM_EPS) + ADAM_WD * w_ref[...])
        mo_ref[...] = mn
        vo_ref[...] = vn

    spec = pl.BlockSpec((tr, cols), lambda i: (i, 0))
    out = jax.ShapeDtypeStruct((rows, cols), F32)
    d, mn, vn = pl.pallas_call(
        body, name=name, grid=(rows // tr,),
        in_specs=[spec] * 4, out_specs=[spec] * 3, out_shape=[out] * 3,
        compiler_params=_params("parallel"),
    )(w.reshape(rows, cols), g.reshape(rows, cols), m.reshape(rows, cols), v.reshape(rows, cols))
    return d.reshape(shape), mn.reshape(shape), vn.reshape(shape)


def _pad_rows(a, rows):
    return jnp.pad(a, ((0, rows - a.shape[0]), (0, 0)))


def kernel(x, c, positions, ada_w, ada_b, norm1_g, norm2_g, wqkv_a, q_norm_a, k_norm_a, sinks_a, wo_a, wqkv_b, wo_b, w_gate, w_up, w_down, loss_target, m_ada_w, m_ada_b, m_norm1_g, m_norm2_g, m_wqkv_a, m_q_norm_a, m_k_norm_a, m_sinks_a, m_wo_a, m_wqkv_b, m_wo_b, m_w_gate, m_w_up, m_w_down, v_ada_w, v_ada_b, v_norm1_g, v_norm2_g, v_wqkv_a, v_q_norm_a, v_k_norm_a, v_sinks_a, v_wo_a, v_wqkv_b, v_wo_b, v_w_gate, v_w_up, v_w_down):
    B, S, D = x.shape
    T = B * S
    L = ada_w.shape[0]
    NA, NB_ = wqkv_a.shape[0], wqkv_b.shape[0]
    me = 4 * lax.axis_index("x") + 2 * lax.axis_index("y") + lax.axis_index("c")
    xt = x.reshape(T, D)

    col_sharded = [(wqkv_a, NA), (wqkv_b, NB_), (w_gate, L), (w_up, L)]
    row_sharded = [(wo_a, NA), (wo_b, NB_), (w_down, L)]
    pieces, layout = [], []
    for w, n in col_sharded:
        for l in range(n):
            pieces.append(w[l].T.astype(BF16))
            layout.append(w.shape[2])
    for w, n in row_sharded:
        for l in range(n):
            pieces.append(w[l].astype(BF16))
            layout.append(w.shape[1])
    packed = jnp.concatenate(pieces, axis=0)
    R = packed.shape[0]
    gathered = _all_gather(packed, "ag_weights")
    full, off = [], 0
    for rows in layout:
        full.append(gathered[:, off:off + rows, :].reshape(NDEV * rows, D))
        off += rows
    it = iter(full)
    wqkv_a_t = [next(it) for _ in range(NA)]
    wqkv_b_t = [next(it) for _ in range(NB_)]
    wg_t = [next(it) for _ in range(L)]
    wu_t = [next(it) for _ in range(L)]
    wo_a_f = [next(it) for _ in range(NA)]
    wo_b_f = [next(it) for _ in range(NB_)]
    wd_f = [next(it) for _ in range(L)]

    WA = ada_w.shape[2]
    c_all = _all_gather(c, "ag_c").reshape(NDEV * B, D)
    bias = lax.dynamic_slice_in_dim(ada_b, me * WA, WA, axis=1).reshape(L, 1, WA)
    mod_part = _ada_fwd(c_all, ada_w, bias, "ada_fwd")
    mod_all = _all_gather(mod_part.reshape(L * NDEV * B, WA), "ag_mod")
    mod_all = mod_all.reshape(NDEV, L, NDEV * B, WA).transpose(1, 2, 0, 3).reshape(L, NDEV * B, NDEV * WA)
    mod = lax.dynamic_slice_in_dim(mod_all, me * B, B, axis=1)
    mod = mod.reshape(L, B, 6, 1, D)
    sh1, sc1, g1, sh2, sc2, g2 = [mod[:, :, k] for k in range(6)]

    half = ROT // 2
    inv_freq = jnp.power(jnp.float32(ROPE_THETA), -jnp.arange(half, dtype=F32) * 2.0 / ROT)
    ang = positions.reshape(T, 1).astype(F32) * inv_freq[None, :]
    cos, sin = jnp.cos(ang), jnp.sin(ang)
    ones = jnp.ones((T, HEAD - ROT), F32)
    zeros = jnp.zeros((T, HEAD - ROT), F32)
    z8 = jnp.zeros((T, half), F32)
    cs = jnp.tile(jnp.concatenate([cos, cos, ones], axis=1), (1, 2))
    s1 = jnp.tile(jnp.concatenate([-sin, z8, zeros], axis=1), (1, 2))
    s2 = jnp.tile(jnp.concatenate([z8, sin, zeros], axis=1), (1, 2))

    saved = []
    xc = xt
    for l in range(L):
        j = l // 2
        h1 = _norm_mod(xc, norm1_g[l:l + 1], sc1[l], sh1[l], S, f"norm1_{l}")
        sv = dict(x_in=xc, h1=h1)
        if l % 2 == 0:
            qkv = _mm_nt(h1, wqkv_a_t[j], F32, f"qkv_a_{l}")
            qg = jnp.tile(q_norm_a[j:j + 1], (1, 2))
            kg = jnp.tile(k_norm_a[j:j + 1], (1, 2))
            qn, kd, vd = _qk_prep(qkv, cs, s1, s2, qg, kg, f"qk_prep_{l}")
            sink2 = jnp.repeat(sinks_a[j].reshape(-1, 2), HEAD, axis=1).reshape(-1, 1, LANES)
            attn, lse = _swa_fwd(qn, kd, vd, sink2, B, f"swa_fwd_{l}")
            sv.update(qkv=qkv, qg=qg, kg=kg, qn=qn, kd=kd, vd=vd, sink2=sink2, lse=lse)
            wo = wo_a_f[j]
        else:
            qkv = _mm_nt(h1, wqkv_b_t[j], BF16, f"qkv_b_{l}")
            attn, tot = _sb_fwd(qkv, B, f"sb_fwd_{l}")
            sv.update(qkv=qkv, tot=tot)
            wo = wo_b_f[j]
        y1, xm = _mm_res(attn, wo, xc, g1[l], S, f"attn_out_{l}")
        h2 = _norm_mod(xm, norm2_g[l:l + 1], sc2[l], sh2[l], S, f"norm2_{l}")
        gate, up, act = _swiglu_fwd(h2, wg_t[l], wu_t[l], f"swiglu_fwd_{l}")
        y2, xc = _mm_res(act, wd_f[l], xm, g2[l], S, f"mlp_out_{l}")
        sv.update(attn=attn, y1=y1, x_mid=xm, h2=h2, gate=gate, up=up, act=act, y2=y2)
        saved.append(sv)

    dx, loss_tile = _loss_head(xc, loss_target.reshape(T, D), "loss_head")

    g_qkv_a, g_qkv_b, g_gate, g_up, g_wo_a, g_wo_b, g_down = ([None] * NA, [None] * NB_, [None] * L, [None] * L,
                                                             [None] * NA, [None] * NB_, [None] * L)
    dmod = [None] * L
    dn1, dn2 = [None] * L, [None] * L
    dqg, dkg, dsink = [None] * NA, [None] * NA, [None] * NA
    for l in reversed(range(L)):
        j = l // 2
        sv = saved[l]
        dy2, dg2 = _gate_bwd(dx, sv["y2"], g2[l], S, f"gate2_bwd_{l}")
        dgate, dup = _swiglu_bwd(dy2, wd_f[l], sv["gate"], sv["up"], f"swiglu_bwd_{l}")
        g_down[l] = _mm_tn(sv["act"], dy2, f"dw_down_{l}")
        dh2 = _mm_nn([(dgate, wg_t[l]), (dup, wu_t[l])], f"dh2_{l}")
        g_gate[l] = _mm_tn(dgate, sv["h2"], f"dw_gate_{l}")
        g_up[l] = _mm_tn(dup, sv["h2"], f"dw_up_{l}")
        dxm, dsh2, dsc2, dn2[l] = _norm_mod_bwd(sv["x_mid"], dh2, dx, norm2_g[l:l + 1], sc2[l], S, f"norm2_bwd_{l}")
        dy1, dg1 = _gate_bwd(dxm, sv["y1"], g1[l], S, f"gate1_bwd_{l}")
        wo = wo_a_f[j] if l % 2 == 0 else wo_b_f[j]
        dattn = _mm_nt(dy1, wo, BF16, f"dattn_{l}")
        gwo = _mm_tn(sv["attn"], dy1, f"dw_o_{l}")
        if l % 2 == 0:
            g_wo_a[j] = gwo
            dq, dkc, dkp, dvc, dvp, dsink[j] = _swa_bwd(sv["qn"], sv["kd"], sv["vd"], sv["sink2"], dattn, sv["lse"], B,
                                                        f"swa_bwd_{l}")
            dqkv, dqg[j], dkg[j] = _qk_prep_bwd(sv["qkv"], cs, s1, s2, sv["qg"], sv["kg"], dq, dkc, dkp, dvc, dvp, B,
                                                f"qk_prep_bwd_{l}")
            wt = wqkv_a_t[j]
        else:
            g_wo_b[j] = gwo
            nqb = sv["qkv"].shape[1] // 3
            dq, dk_t, dv_t = _sb_bwd(sv["qkv"], sv["qkv"][:, :nqb].T, dattn, dattn.T, sv["tot"], B, f"sb_bwd_{l}")
            dk, dv = [t.reshape(B, nqb, S).transpose(0, 2, 1).reshape(T, nqb) for t in (dk_t, dv_t)]
            dqkv = jnp.concatenate([dq, dk, dv], axis=1).astype(BF16)
            wt = wqkv_b_t[j]
        dh1 = _mm_nn([(dqkv, wt)], f"dh1_{l}")
        gq = _mm_tn(dqkv, sv["h1"], f"dw_qkv_{l}")
        if l % 2 == 0:
            g_qkv_a[j] = gq
        else:
            g_qkv_b[j] = gq
        dx, dsh1, dsc1, dn1[l] = _norm_mod_bwd(sv["x_in"], dh1, dxm, norm1_g[l:l + 1], sc1[l], S, f"norm1_bwd_{l}")
        dmod[l] = jnp.concatenate([dsh1, dsc1, dg1, dsh2, dsc2, dg2], axis=1)
    grad_x = dx.reshape(B, S, D)

    ndm = L * 6
    dmod_rows = jnp.stack(dmod, axis=1).reshape(B * ndm, D)
    misc = jnp.concatenate(
        [jnp.concatenate(dn1, axis=0).reshape(B * L, D), jnp.concatenate(dn2, axis=0).reshape(B * L, D),
         _pad_rows(jnp.concatenate([jnp.pad(r, ((0, 0), (0, D - LANES))) for r in dqg + dkg]
                                   + [jnp.pad(r[:, 0, ::HEAD].reshape(1, -1), ((0, 0), (0, D - 2 * r.shape[0]))) for r in dsink]
                                   + [jnp.pad(loss_tile[0:1, 0:1], ((0, 0), (0, D - 1)))], axis=0), 8)], axis=0)
    nmisc = misc.shape[0]
    small = _all_gather(jnp.concatenate([dmod_rows, _pad_rows(misc, -(-nmisc // 8) * 8)], axis=0), "ag_small")
    dmod_all = small[:, :B * ndm].reshape(NDEV * B, ndm, D)
    g_ada_b = _sum_leading(dmod_all, "sum_dmod").reshape(L, 6 * D)
    misc_sum = _sum_leading(small[:, B * ndm:], "sum_misc")
    g_n1 = misc_sum[0:B * L].reshape(L, B, D)
    g_n2 = misc_sum[B * L:2 * B * L].reshape(L, B, D)
    g_norm1 = _sum_leading(g_n1.transpose(1, 0, 2), "sum_n1")
    g_norm2 = _sum_leading(g_n2.transpose(1, 0, 2), "sum_n2")
    o = 2 * B * L
    g_qn = misc_sum[o:o + NA, :HEAD]
    g_kn = misc_sum[o + NA:o + 2 * NA, :HEAD]
    nsink = sinks_a.shape[1]
    g_sink = misc_sum[o + 2 * NA:o + 3 * NA, :nsink]
    loss = misc_sum[o + 3 * NA, 0]

    dmod_loc = lax.dynamic_slice_in_dim(dmod_all.reshape(NDEV * B, L, 6 * D), me * WA, WA, axis=2)
    g_ada_w = _ada_bwd(c_all, dmod_loc.transpose(1, 0, 2), "ada_bwd")

    parts = []
    for g in g_qkv_a + g_qkv_b + g_gate + g_up + g_wo_a + g_wo_b + g_down:
        parts.append(g.reshape(NDEV, g.shape[0] // NDEV, D).astype(BF16))
    partial = jnp.concatenate(parts, axis=1)
    received = _exchange(partial, "grad_exchange")
    gsum = _sum_leading(received, "grad_sum")
    shards, off = [], 0
    for rows in layout:
        shards.append(gsum[off:off + rows])
        off += rows
    it = iter(shards)
    gw_qkv_a = jnp.stack([next(it).T for _ in range(NA)])
    gw_qkv_b = jnp.stack([next(it).T for _ in range(NB_)])
    gw_gate = jnp.stack([next(it).T for _ in range(L)])
    gw_up = jnp.stack([next(it).T for _ in range(L)])
    gw_wo_a = jnp.stack([next(it) for _ in range(NA)])
    gw_wo_b = jnp.stack([next(it) for _ in range(NB_)])
    gw_down = jnp.stack([next(it) for _ in range(L)])

    grads = [g_ada_w, g_ada_b, g_norm1, g_norm2, gw_qkv_a, g_qn, g_kn, g_sink, gw_wo_a, gw_qkv_b, gw_wo_b,
             gw_gate, gw_up, gw_down]
    ws = [ada_w, ada_b, norm1_g, norm2_g, wqkv_a, q_norm_a, k_norm_a, sinks_a, wo_a, wqkv_b, wo_b, w_gate, w_up, w_down]
    ms = [m_ada_w, m_ada_b, m_norm1_g, m_norm2_g, m_wqkv_a, m_q_norm_a, m_k_norm_a, m_sinks_a, m_wo_a, m_wqkv_b,
          m_wo_b, m_w_gate, m_w_up, m_w_down]
    vs = [v_ada_w, v_ada_b, v_norm1_g, v_norm2_g, v_wqkv_a, v_q_norm_a, v_k_norm_a, v_sinks_a, v_wo_a, v_wqkv_b,
          v_wo_b, v_w_gate, v_w_up, v_w_down]
    deltas, new_m, new_v = [], [], []
    for k, (w, g, m, v) in enumerate(zip(ws, grads, ms, vs)):
        g = g.reshape(w.shape)
        d, mn, vn = _adamw(w, g, m, v, f"adamw_{k}")
        grads[k] = g
        deltas.append(d)
        new_m.append(mn)
        new_v.append(vn)
    return (loss, grad_x, *grads, *deltas, *new_m, *new_v)
```

```python
import functools
import math

import jax
import jax.numpy as jnp
from jax import lax
from jax.experimental import pallas as pl
from jax.experimental.pallas import tpu as pltpu

F32 = jnp.float32
BF16 = jnp.bfloat16
NDEV = 8
HEAD = 64
BLK = 128
LANES = 128
EPS = 1e-6
ROT = HEAD // 4
ROPE_THETA = 500000.0
SCALE = HEAD ** -0.5
NEG = -1e30
VMEM_LIMIT = 56 * 1024 * 1024
MESH = pl.DeviceIdType.MESH
HIGH = lax.Precision.HIGHEST

ADAM_LR = 0.001
ADAM_B1 = 0.9
ADAM_B2 = 0.999
ADAM_EPS = 1e-08
ADAM_WD = 0.01
ADAM_STEP = 10


def _params(*sem):
    return pltpu.CompilerParams(dimension_semantics=sem, vmem_limit_bytes=VMEM_LIMIT)


def _pick(n, cap, mult):
    if n <= cap:
        return n
    best = None
    for t in range(mult, cap + 1, mult):
        if n % t == 0:
            best = t
    assert best is not None, (n, cap, mult)
    return best


def _dot(a, b, dims, precision=None):
    return lax.dot_general(a, b, (dims, ((), ())), preferred_element_type=F32, precision=precision)


NN = ((1,), (0,))
NT = ((1,), (1,))
TN = ((0,), (0,))


def _all_gather(x, name):
    m, n = x.shape

    def body(x_ref, out_ref, send_sems, recv_sems, local_sem):
        ix, iy, ic = lax.axis_index("x"), lax.axis_index("y"), lax.axis_index("c")
        me, sibling = (ix, iy, ic), (ix, iy, 1 - ic)
        chips = [(1 - ix, iy), (ix, 1 - iy), (1 - ix, 1 - iy)]

        def slab(px, py, pc):
            return out_ref.at[4 * px + 2 * py + pc]

        def copy(k, block, to, src=None):
            return pltpu.make_async_remote_copy(
                src_ref=slab(*block) if src is None else src, dst_ref=slab(*block),
                send_sem=send_sems.at[k], recv_sem=recv_sems.at[k], device_id=to, device_id_type=MESH)

        mine = pltpu.make_async_copy(x_ref, slab(*me), local_sem)
        mine.start()
        first = [copy(0, me, sibling, src=x_ref)]
        first += [copy(1 + j, me, (*chip, ic), src=x_ref) for j, chip in enumerate(chips)]
        for cp in first:
            cp.start()
        passed = [copy(4 + j, (*chip, ic), sibling) for j, chip in enumerate(chips)]
        for j, chip in enumerate(chips):
            copy(1 + j, (*chip, ic), me).wait_recv()
            passed[j].start()
        copy(0, sibling, me).wait_recv()
        for j, chip in enumerate(chips):
            copy(4 + j, (*chip, 1 - ic), me).wait_recv()
        for cp in first + passed:
            cp.wait_send()
        mine.wait()

    return pl.pallas_call(
        body, name=name,
        out_shape=jax.ShapeDtypeStruct((NDEV, m, n), x.dtype),
        in_specs=[pl.BlockSpec(memory_space=pl.ANY)],
        out_specs=pl.BlockSpec(memory_space=pl.ANY),
        scratch_shapes=[pltpu.SemaphoreType.DMA((7,)), pltpu.SemaphoreType.DMA((7,)), pltpu.SemaphoreType.DMA(())],
    )(x)


def _exchange(p, name):
    _, m, n = p.shape

    def body(p_ref, r_ref, send_sems, recv_sems, local_sem):
        ix, iy, ic = lax.axis_index("x"), lax.axis_index("y"), lax.axis_index("c")
        me = 4 * ix + 2 * iy + ic
        own = pltpu.make_async_copy(p_ref.at[me], r_ref.at[me], local_sem)
        own.start()
        copies = []
        for k in range(1, NDEV):
            px = 1 - ix if k & 4 else ix
            py = 1 - iy if k & 2 else iy
            pc = 1 - ic if k & 1 else ic
            cp = pltpu.make_async_remote_copy(
                src_ref=p_ref.at[4 * px + 2 * py + pc], dst_ref=r_ref.at[me],
                send_sem=send_sems.at[k - 1], recv_sem=recv_sems.at[k - 1],
                device_id=(px, py, pc), device_id_type=MESH)
            cp.start()
            copies.append(cp)
        for cp in copies:
            cp.wait()
        own.wait()

    return pl.pallas_call(
        body, name=name,
        out_shape=jax.ShapeDtypeStruct(p.shape, p.dtype),
        in_specs=[pl.BlockSpec(memory_space=pl.ANY)],
        out_specs=pl.BlockSpec(memory_space=pl.ANY),
        scratch_shapes=[pltpu.SemaphoreType.DMA((7,)), pltpu.SemaphoreType.DMA((7,)), pltpu.SemaphoreType.DMA(())],
    )(p)


def _sum_leading(r, name):
    k, m, n = r.shape
    mult = 8 * (4 // r.dtype.itemsize)
    tm = _pick(m, max(mult, (4 * 1024 * 1024) // (k * n * r.dtype.itemsize) // mult * mult), mult)

    def body(r_ref, o_ref):
        acc = r_ref[0].astype(F32)
        for s in range(1, k):
            acc = acc + r_ref[s].astype(F32)
        o_ref[...] = acc

    return pl.pallas_call(
        body, name=name, grid=(m // tm,),
        in_specs=[pl.BlockSpec((k, tm, n), lambda i: (0, i, 0))],
        out_specs=pl.BlockSpec((tm, n), lambda i: (i, 0)),
        out_shape=jax.ShapeDtypeStruct((m, n), F32),
        compiler_params=_params("parallel"),
    )(r)


def _mm_nt(a, bt, out_dtype, name):
    M, K = a.shape
    N = bt.shape[0]
    tm, tn = _pick(M, 512, 8), _pick(N, 1536, LANES)

    def body(a_ref, b_ref, o_ref):
        o_ref[...] = _dot(a_ref[...], b_ref[...], NT).astype(out_dtype)

    return pl.pallas_call(
        body, name=name, grid=(N // tn, M // tm),
        in_specs=[pl.BlockSpec((tm, K), lambda j, i: (i, 0)), pl.BlockSpec((tn, K), lambda j, i: (j, 0))],
        out_specs=pl.BlockSpec((tm, tn), lambda j, i: (i, j)),
        out_shape=jax.ShapeDtypeStruct((M, N), out_dtype),
        compiler_params=_params("parallel", "parallel"),
    )(a, bt)


def _mm_nn(pairs, name):
    M = pairs[0][0].shape[0]
    N = pairs[0][1].shape[1]
    tm, tn = _pick(M, 512, 8), _pick(N, 1024, LANES)
    np_ = len(pairs)

    def body(*refs):
        o_ref = refs[-1]
        acc = _dot(refs[0][...], refs[1][...], NN)
        for p in range(1, np_):
            acc = acc + _dot(refs[2 * p][...], refs[2 * p + 1][...], NN)
        o_ref[...] = acc

    in_specs, args = [], []
    for a, b in pairs:
        K = a.shape[1]
        in_specs += [pl.BlockSpec((tm, K), lambda i, j: (i, 0)), pl.BlockSpec((K, tn), lambda i, j: (0, j))]
        args += [a, b]
    return pl.pallas_call(
        body, name=name, grid=(M // tm, N // tn),
        in_specs=in_specs,
        out_specs=pl.BlockSpec((tm, tn), lambda i, j: (i, j)),
        out_shape=jax.ShapeDtypeStruct((M, N), F32),
        compiler_params=_params("parallel", "parallel"),
    )(*args)


def _mm_tn(a, b, name):
    M, N1 = a.shape
    N2 = b.shape[1]
    t1, tk = _pick(N1, 1536, LANES), _pick(M, 512, 8)

    def body(a_ref, b_ref, o_ref):
        @pl.when(pl.program_id(1) == 0)
        def _():
            o_ref[...] = jnp.zeros_like(o_ref)
        o_ref[...] += _dot(a_ref[...], b_ref[...], TN)

    return pl.pallas_call(
        body, name=name, grid=(N1 // t1, M // tk),
        in_specs=[pl.BlockSpec((tk, t1), lambda i, k: (k, i)), pl.BlockSpec((tk, N2), lambda i, k: (k, 0))],
        out_specs=pl.BlockSpec((t1, N2), lambda i, k: (i, 0)),
        out_shape=jax.ShapeDtypeStruct((N1, N2), F32),
        compiler_params=_params("parallel", "arbitrary"),
    )(a, b)


def _mm_res(a, w, x, gate, S, name):
    T, K = a.shape
    D = w.shape[1]
    tm, tn = _pick(S, 512, 8), _pick(D, 1024, LANES)
    nb = S // tm

    def body(a_ref, w_ref, x_ref, g_ref, y_ref, o_ref):
        y = _dot(a_ref[...], w_ref[...], NN)
        y_ref[...] = y.astype(BF16)
        o_ref[...] = x_ref[...] + g_ref[0] * y

    return pl.pallas_call(
        body, name=name, grid=(T // tm, D // tn),
        in_specs=[pl.BlockSpec((tm, K), lambda i, j: (i, 0)), pl.BlockSpec((K, tn), lambda i, j: (0, j)),
                  pl.BlockSpec((tm, tn), lambda i, j: (i, j)), pl.BlockSpec((1, 1, tn), lambda i, j: (i // nb, 0, j))],
        out_specs=[pl.BlockSpec((tm, tn), lambda i, j: (i, j)), pl.BlockSpec((tm, tn), lambda i, j: (i, j))],
        out_shape=[jax.ShapeDtypeStruct((T, D), BF16), jax.ShapeDtypeStruct((T, D), F32)],
        compiler_params=_params("parallel", "parallel"),
    )(a, w, x, gate)


def _swiglu_fwd(h, wgt, wut, name):
    T, D = h.shape
    F = wgt.shape[0]
    tm, tn = _pick(T, 512, 8), _pick(F, 1536, LANES)

    def body(h_ref, g_ref, u_ref, go_ref, uo_ref, a_ref):
        hh = h_ref[...]
        g = _dot(hh, g_ref[...], NT)
        u = _dot(hh, u_ref[...], NT)
        go_ref[...] = g.astype(BF16)
        uo_ref[...] = u.astype(BF16)
        a_ref[...] = (g * jax.nn.sigmoid(g) * u).astype(BF16)

    spec_w = pl.BlockSpec((tn, D), lambda j, i: (j, 0))
    spec_o = pl.BlockSpec((tm, tn), lambda j, i: (i, j))
    out = jax.ShapeDtypeStruct((T, F), BF16)
    return pl.pallas_call(
        body, name=name, grid=(F // tn, T // tm),
        in_specs=[pl.BlockSpec((tm, D), lambda j, i: (i, 0)), spec_w, spec_w],
        out_specs=[spec_o, spec_o, spec_o],
        out_shape=[out, out, out],
        compiler_params=_params("parallel", "parallel"),
    )(h, wgt, wut)


def _swiglu_bwd(dy, wd, gate, up, name):
    T, D = dy.shape
    F = wd.shape[0]
    tm, tn = _pick(T, 512, 8), _pick(F, 1536, LANES)

    def body(dy_ref, w_ref, g_ref, u_ref, dg_ref, du_ref):
        da = _dot(dy_ref[...], w_ref[...], NT)
        g = g_ref[...].astype(F32)
        sg = jax.nn.sigmoid(g)
        silu = g * sg
        du_ref[...] = (da * silu).astype(BF16)
        dg_ref[...] = (da * u_ref[...].astype(F32) * (sg + silu * (1.0 - sg))).astype(BF16)

    spec_o = pl.BlockSpec((tm, tn), lambda j, i: (i, j))
    return pl.pallas_call(
        body, name=name, grid=(F // tn, T // tm),
        in_specs=[pl.BlockSpec((tm, D), lambda j, i: (i, 0)), pl.BlockSpec((tn, D), lambda j, i: (j, 0)), spec_o, spec_o],
        out_specs=[spec_o, spec_o],
        out_shape=[jax.ShapeDtypeStruct((T, F), BF16), jax.ShapeDtypeStruct((T, F), BF16)],
        compiler_params=_params("parallel", "parallel"),
    )(dy, wd, gate, up)


def _norm_mod(x, gain, sc, sh, S, name):
    T, D = x.shape
    tm = _pick(S, 512, 8)
    nb = S // tm

    def body(x_ref, g_ref, sc_ref, sh_ref, o_ref):
        xv = x_ref[...]
        r = lax.rsqrt(jnp.mean(xv * xv, axis=-1, keepdims=True) + EPS)
        o_ref[...] = ((xv * r) * g_ref[...] * (1.0 + sc_ref[0]) + sh_ref[0]).astype(BF16)

    spec_b = pl.BlockSpec((1, 1, D), lambda i: (i // nb, 0, 0))
    return pl.pallas_call(
        body, name=name, grid=(T // tm,),
        in_specs=[pl.BlockSpec((tm, D), lambda i: (i, 0)), pl.BlockSpec((1, D), lambda i: (0, 0)), spec_b, spec_b],
        out_specs=pl.BlockSpec((tm, D), lambda i: (i, 0)),
        out_shape=jax.ShapeDtypeStruct((T, D), BF16),
        compiler_params=_params("parallel"),
    )(x, gain, sc, sh)


def _norm_mod_bwd(x, dh, dres, gain, sc, S, name):
    T, D = x.shape
    B = T // S
    tm = _pick(S, 256, 8)
    nb = S // tm

    def body(x_ref, dh_ref, dr_ref, g_ref, sc_ref, o_ref, dsh_ref, dsc_ref, dg_ref):
        @pl.when(pl.program_id(1) == 0)
        def _():
            dsh_ref[...] = jnp.zeros_like(dsh_ref)
            dsc_ref[...] = jnp.zeros_like(dsc_ref)
            dg_ref[...] = jnp.zeros_like(dg_ref)
        xv, dhv, g = x_ref[...], dh_ref[...], g_ref[...]
        r = lax.rsqrt(jnp.mean(xv * xv, axis=-1, keepdims=True) + EPS)
        xhat = xv * r
        dsh_ref[0] += jnp.sum(dhv, axis=0, keepdims=True)
        dsc_ref[0] += jnp.sum(dhv * (xhat * g), axis=0, keepdims=True)
        dn = dhv * (1.0 + sc_ref[0])
        dg_ref[0] += jnp.sum(dn * xhat, axis=0, keepdims=True)
        dxh = dn * g
        o_ref[...] = dr_ref[...] + r * (dxh - xhat * jnp.mean(dxh * xhat, axis=-1, keepdims=True))

    spec_t = pl.BlockSpec((tm, D), lambda b, i: (b * nb + i, 0))
    spec_b = pl.BlockSpec((1, 1, D), lambda b, i: (b, 0, 0))
    red = jax.ShapeDtypeStruct((B, 1, D), F32)
    return pl.pallas_call(
        body, name=name, grid=(B, nb),
        in_specs=[spec_t, spec_t, spec_t, pl.BlockSpec((1, D), lambda b, i: (0, 0)), spec_b],
        out_specs=[spec_t, spec_b, spec_b, spec_b],
        out_shape=[jax.ShapeDtypeStruct((T, D), F32), red, red, red],
        compiler_params=_params("parallel", "arbitrary"),
    )(x, dh, dres, gain, sc)


def _gate_bwd(dx, y, gate, S, name):
    T, D = dx.shape
    B = T // S
    tm = _pick(S, 512, 8)
    nb = S // tm

    def body(dx_ref, y_ref, g_ref, dy_ref, dg_ref):
        @pl.when(pl.program_id(1) == 0)
        def _():
            dg_ref[...] = jnp.zeros_like(dg_ref)
        d = dx_ref[...]
        dy_ref[...] = (d * g_ref[0]).astype(BF16)
        dg_ref[0] += jnp.sum(d * y_ref[...].astype(F32), axis=0, keepdims=True)

    spec_t = pl.BlockSpec((tm, D), lambda b, i: (b * nb + i, 0))
    spec_b = pl.BlockSpec((1, 1, D), lambda b, i: (b, 0, 0))
    return pl.pallas_call(
        body, name=name, grid=(B, nb),
        in_specs=[spec_t, spec_t, spec_b],
        out_specs=[spec_t, spec_b],
        out_shape=[jax.ShapeDtypeStruct((T, D), BF16), jax.ShapeDtypeStruct((B, 1, D), F32)],
        compiler_params=_params("parallel", "arbitrary"),
    )(dx, y, gate)


def _loss_head(y, target, name):
    T, D = y.shape
    tm = _pick(T, 512, 8)

    def body(y_ref, t_ref, dy_ref, l_ref):
        @pl.when(pl.program_id(0) == 0)
        def _():
            l_ref[...] = jnp.zeros_like(l_ref)
        e = y_ref[...] - t_ref[...]
        dy_ref[...] = e * (1.0 / D)
        l_ref[...] += 0.5 * jnp.sum(jnp.mean(e * e, axis=-1, keepdims=True), axis=0, keepdims=True)

    spec = pl.BlockSpec((tm, D), lambda i: (i, 0))
    return pl.pallas_call(
        body, name=name, grid=(T // tm,),
        in_specs=[spec, spec],
        out_specs=[spec, pl.BlockSpec((8, LANES), lambda i: (0, 0))],
        out_shape=[jax.ShapeDtypeStruct((T, D), F32), jax.ShapeDtypeStruct((8, LANES), F32)],
        compiler_params=_params("arbitrary"),
    )(y, target)


def _ada_fwd(c_all, ada_w, bias, name):
    NB, D = c_all.shape
    L, _, W = ada_w.shape

    def body(c_ref, w_ref, b_ref, o_ref):
        cv = c_ref[...]
        cond = cv * jax.nn.sigmoid(cv)
        o_ref[0] = _dot(cond, w_ref[0], NN, HIGH) + b_ref[0]

    return pl.pallas_call(
        body, name=name, grid=(L,),
        in_specs=[pl.BlockSpec((NB, D), lambda l: (0, 0)), pl.BlockSpec((1, D, W), lambda l: (l, 0, 0)),
                  pl.BlockSpec((1, 1, W), lambda l: (l, 0, 0))],
        out_specs=pl.BlockSpec((1, NB, W), lambda l: (l, 0, 0)),
        out_shape=jax.ShapeDtypeStruct((L, NB, W), F32),
        compiler_params=_params("parallel"),
    )(c_all, ada_w, bias)


def _ada_bwd(c_all, dmod, name):
    NB, D = c_all.shape
    L, _, W = dmod.shape

    def body(c_ref, d_ref, o_ref):
        cv = c_ref[...]
        cond = cv * jax.nn.sigmoid(cv)
        o_ref[0] = _dot(cond, d_ref[0], TN, HIGH)

    return pl.pallas_call(
        body, name=name, grid=(L,),
        in_specs=[pl.BlockSpec((NB, D), lambda l: (0, 0)), pl.BlockSpec((1, NB, W), lambda l: (l, 0, 0))],
        out_specs=pl.BlockSpec((1, D, W), lambda l: (l, 0, 0)),
        out_shape=jax.ShapeDtypeStruct((L, D, W), F32),
        compiler_params=_params("parallel"),
    )(c_all, dmod)


def _lo_mask(shape):
    return lax.broadcasted_iota(jnp.int32, shape, len(shape) - 1) < HEAD


def _head_sum_matrix():
    r = lax.broadcasted_iota(jnp.int32, (LANES, LANES), 0) // HEAD
    c = lax.broadcasted_iota(jnp.int32, (LANES, LANES), 1) // HEAD
    return (r == c).astype(F32)


def _rope(y, cs, s1, s2):
    return y * cs + pltpu.roll(y, LANES - ROT // 2, 1) * s1 + pltpu.roll(y, ROT // 2, 1) * s2


def _rope_bwd(d, cs, s1, s2):
    return d * cs + pltpu.roll(d * s1, ROT // 2, 1) + pltpu.roll(d * s2, LANES - ROT // 2, 1)


def _qk_prep(qkv, cs, s1, s2, qg, kg, name):
    T, W = qkv.shape
    NQ = W - 2 * LANES
    tm = _pick(T, 512, 8)

    def body(x_ref, cs_ref, s1_ref, s2_ref, qg_ref, kg_ref, q_ref, k_ref, v_ref):
        P = _head_sum_matrix()
        cs_, s1_, s2_ = cs_ref[...], s1_ref[...], s2_ref[...]
        lo = _lo_mask((tm, LANES))

        def norm_rope(xv, g):
            ms = _dot(xv * xv, P, NN, HIGH) * (1.0 / HEAD)
            return _rope(xv * lax.rsqrt(ms + EPS) * g, cs_, s1_, s2_)

        for j in range(NQ // LANES):
            q_ref[:, j * LANES:(j + 1) * LANES] = norm_rope(x_ref[:, j * LANES:(j + 1) * LANES], qg_ref[...]).astype(BF16)
        kr = norm_rope(x_ref[:, NQ:NQ + LANES], kg_ref[...])
        ks = pltpu.roll(kr, HEAD, 1)
        k_ref[:, :LANES] = jnp.where(lo, kr, ks).astype(BF16)
        k_ref[:, LANES:] = jnp.where(lo, ks, kr).astype(BF16)
        vr = x_ref[:, NQ + LANES:]
        vs = pltpu.roll(vr, HEAD, 1)
        v_ref[:, :LANES] = jnp.where(lo, vr, vs).astype(BF16)
        v_ref[:, LANES:] = jnp.where(lo, vs, vr).astype(BF16)

    spec_t = pl.BlockSpec((tm, LANES), lambda i: (i, 0))
    spec_g = pl.BlockSpec((1, LANES), lambda i: (0, 0))
    return pl.pallas_call(
        body, name=name, grid=(T // tm,),
        in_specs=[pl.BlockSpec((tm, W), lambda i: (i, 0)), spec_t, spec_t, spec_t, spec_g, spec_g],
        out_specs=[pl.BlockSpec((tm, NQ), lambda i: (i, 0)), pl.BlockSpec((tm, 2 * LANES), lambda i: (i, 0)),
                   pl.BlockSpec((tm, 2 * LANES), lambda i: (i, 0))],
        out_shape=[jax.ShapeDtypeStruct((T, NQ), BF16), jax.ShapeDtypeStruct((T, 2 * LANES), BF16),
                   jax.ShapeDtypeStruct((T, 2 * LANES), BF16)],
        compiler_params=_params("parallel"),
    )(qkv, cs, s1, s2, qg, kg)


def _stack_heads(x2):
    lo = _lo_mask(x2.shape)
    z = jnp.zeros_like(x2)
    return jnp.concatenate([jnp.where(lo, x2, z), jnp.where(lo, z, x2)], axis=0)


def _unstack_heads(xs):
    r = xs.shape[0] // 2
    return jnp.where(_lo_mask((r, LANES)), xs[:r], xs[r:])


def _swa_valid(i):
    qo = lax.broadcasted_iota(jnp.int32, (2 * BLK, 2 * BLK), 0) % BLK
    kc_ = lax.broadcasted_iota(jnp.int32, (2 * BLK, 2 * BLK), 1)
    rel = qo + BLK - kc_
    return (rel >= 0) & (rel < BLK) & ((kc_ >= BLK) | (i > 0))


def _swa_scores(q2, kk, sink2, valid):
    qs = _stack_heads(q2) * SCALE
    s = _dot(qs, kk, NT)
    sk = jnp.concatenate([jnp.broadcast_to(sink2[:, 0:1], (BLK, 1)), jnp.broadcast_to(sink2[:, HEAD:HEAD + 1], (BLK, 1))], axis=0)
    return qs, jnp.where(valid, s, NEG), sk


def _swa_fwd(q, kd, vd, sink2, B, name):
    T, NQ = q.shape
    NP = NQ // LANES
    nq = T // B // BLK
    NG = kd.shape[1] // LANES
    grp = NP // NG

    def body(q_ref, kp_ref, kc_ref, vp_ref, vc_ref, s_ref, o_ref, l_ref):
        valid = _swa_valid(pl.program_id(2))
        kk = jnp.concatenate([kp_ref[...], kc_ref[...]], axis=0)
        vs = _stack_heads(jnp.concatenate([vp_ref[...], vc_ref[...]], axis=0))
        sls = [slice(jj * LANES, (jj + 1) * LANES) for jj in range(grp)]
        sc = [_swa_scores(q_ref[:, sl], kk, s_ref[jj], valid) for jj, sl in enumerate(sls)]
        ms = [jnp.maximum(jnp.max(s, axis=1, keepdims=True), sk) for _, s, sk in sc]
        ps = [jnp.where(valid, jnp.exp(s - m), 0.0) for (_, s, _), m in zip(sc, ms)]
        ls = [jnp.sum(p, axis=1, keepdims=True) + jnp.exp(sk - m) for p, (_, _, sk), m in zip(ps, sc, ms)]
        ps = [(p * (1.0 / l)).astype(BF16) for p, l in zip(ps, ls)]
        os_ = [_dot(jnp.concatenate([p[:BLK], p[BLK:]], axis=1), vs, NN) for p in ps]
        for sl, o, m, l in zip(sls, os_, ms, ls):
            o_ref[:, sl] = o.astype(BF16)
            l_ref[:, sl] = _unstack_heads(jnp.broadcast_to(m + jnp.log(l), (2 * BLK, LANES)))

    spec_q = pl.BlockSpec((BLK, grp * LANES), lambda b, g, i: (b * nq + i, g))
    spec_p = pl.BlockSpec((BLK, LANES), lambda b, g, i: (b * nq + jnp.maximum(i - 1, 0), g))
    spec_c = pl.BlockSpec((BLK, LANES), lambda b, g, i: (b * nq + i, g))
    return pl.pallas_call(
        body, name=name, grid=(B, NG, nq),
        in_specs=[spec_q, spec_p, spec_c, spec_p, spec_c, pl.BlockSpec((grp, 1, LANES), lambda b, g, i: (g, 0, 0))],
        out_specs=[spec_q, spec_q],
        out_shape=[jax.ShapeDtypeStruct((T, NQ), BF16), jax.ShapeDtypeStruct((T, NQ), F32)],
        compiler_params=_params("parallel", "parallel", "parallel"),
    )(q, kd, kd, vd, vd, sink2)


def _swa_bwd(q, kd, vd, sink2, do, lse, B, name):
    T, NQ = q.shape
    NP = NQ // LANES
    nq = T // B // BLK
    NG = kd.shape[1] // LANES
    grp = NP // NG

    def body(q_ref, kp_ref, kc_ref, vp_ref, vc_ref, s_ref, do_ref, l_ref,
             dq_ref, dkc_ref, dkp_ref, dvc_ref, dvp_ref, ds_ref):
        b, i = pl.program_id(1), pl.program_id(2)

        @pl.when((b == 0) & (i == 0))
        def _():
            ds_ref[...] = jnp.zeros_like(ds_ref)
        valid = _swa_valid(i)
        kk = jnp.concatenate([kp_ref[...], kc_ref[...]], axis=0)
        vv = jnp.concatenate([vp_ref[...], vc_ref[...]], axis=0)
        sls = [slice(jj * LANES, (jj + 1) * LANES) for jj in range(grp)]
        sc = [_swa_scores(q_ref[:, sl], kk, s_ref[jj], valid) for jj, sl in enumerate(sls)]
        dos = [_stack_heads(do_ref[:, sl]) for sl in sls]
        dps = [_dot(d, vv, NT) for d in dos]
        lses = [jnp.concatenate([l_ref[:, sl][:, 0:1], l_ref[:, sl][:, HEAD:HEAD + 1]], axis=0) for sl in sls]
        ps = [jnp.where(valid, jnp.exp(s - lse), 0.0) for (_, s, _), lse in zip(sc, lses)]
        deltas = [jnp.sum(p * dp, axis=1, keepdims=True) for p, dp in zip(ps, dps)]
        dscs = [(p * (dp - delta)).astype(BF16) for p, dp, delta in zip(ps, dps, deltas)]
        dqs = [_dot(dsc, kk, NN) for dsc in dscs]
        dk = jnp.zeros((2 * BLK, LANES), F32)
        dv = jnp.zeros((2 * BLK, LANES), F32)
        for jj, sl in enumerate(sls):
            dsk = -jnp.exp(sc[jj][2] - lses[jj]) * deltas[jj]
            dsk_lo = jnp.sum(dsk[:BLK], axis=0, keepdims=True)
            dsk_hi = jnp.sum(dsk[BLK:], axis=0, keepdims=True)
            ds_ref[jj] += jnp.where(_lo_mask((1, LANES)), dsk_lo, dsk_hi)
            dq_ref[:, sl] = _unstack_heads(dqs[jj]) * SCALE
            dk = dk + _dot(dscs[jj], sc[jj][0], TN)
            dv = dv + _dot(ps[jj].astype(BF16), dos[jj], TN)
        dkp_ref[...] = dk[:BLK]
        dkc_ref[...] = dk[BLK:]
        dvp_ref[...] = dv[:BLK]
        dvc_ref[...] = dv[BLK:]

    spec_q = pl.BlockSpec((BLK, grp * LANES), lambda g, b, i: (b * nq + i, g))
    spec_p = pl.BlockSpec((BLK, LANES), lambda g, b, i: (b * nq + jnp.maximum(i - 1, 0), g))
    spec_c = pl.BlockSpec((BLK, LANES), lambda g, b, i: (b * nq + i, g))
    spec_s = pl.BlockSpec((grp, 1, LANES), lambda g, b, i: (g, 0, 0))
    kv = jax.ShapeDtypeStruct((T, NG * LANES), F32)
    return pl.pallas_call(
        body, name=name, grid=(NG, B, nq),
        in_specs=[spec_q, spec_p, spec_c, spec_p, spec_c, spec_s, spec_q, spec_q],
        out_specs=[spec_q, spec_c, spec_c, spec_c, spec_c, spec_s],
        out_shape=[jax.ShapeDtypeStruct((T, NQ), F32), kv, kv, kv, kv, jax.ShapeDtypeStruct((NP, 1, LANES), F32)],
        compiler_params=_params("arbitrary", "arbitrary", "arbitrary"),
    )(q, kd, kd, vd, vd, sink2, do, lse)


def _qk_prep_bwd(qkv, cs, s1, s2, qg, kg, dq, dkc, dkp, dvc, dvp, B, name):
    T, W = qkv.shape
    NQ = W - 2 * LANES
    NP = NQ // LANES
    nq = T // B // BLK

    def body(x_ref, cs_ref, s1_ref, s2_ref, qg_ref, kg_ref, dq_ref, dkc_ref, dkp_ref, dvc_ref, dvp_ref,
             o_ref, dqg_ref, dkg_ref):
        b, i = pl.program_id(0), pl.program_id(1)

        @pl.when((b == 0) & (i == 0))
        def _():
            dqg_ref[...] = jnp.zeros_like(dqg_ref)
            dkg_ref[...] = jnp.zeros_like(dkg_ref)
        P = _head_sum_matrix()
        cs_, s1_, s2_ = cs_ref[...], s1_ref[...], s2_ref[...]
        lo = _lo_mask((BLK, LANES))
        has_next = (i + 1 < nq).astype(F32)

        def norm_rope_bwd(xv, g, d):
            du = _rope_bwd(d, cs_, s1_, s2_)
            r = lax.rsqrt(_dot(xv * xv, P, NN, HIGH) * (1.0 / HEAD) + EPS)
            xhat = xv * r
            dgain = jnp.sum(du * xhat, axis=0, keepdims=True)
            uu = du * g
            dx = r * (uu - xhat * (_dot(uu * xhat, P, NN, HIGH) * (1.0 / HEAD)))
            return dx, dgain + pltpu.roll(dgain, HEAD, 1)

        dqg = jnp.zeros((1, LANES), F32)
        for j in range(NP):
            sl = slice(j * LANES, (j + 1) * LANES)
            dx, dg = norm_rope_bwd(x_ref[:, sl], qg_ref[...], dq_ref[:, sl])
            o_ref[:, sl] = dx.astype(BF16)
            dqg = dqg + dg
        dqg_ref[...] += dqg

        def fold(c_ref, p_ref, g):
            sl = slice(g * LANES, (g + 1) * LANES)
            t = c_ref[:, sl] + has_next * p_ref[:, sl]
            return t + pltpu.roll(t, HEAD, 1)

        dk = jnp.where(lo, fold(dkc_ref, dkp_ref, 0), fold(dkc_ref, dkp_ref, 1))
        dx, dg = norm_rope_bwd(x_ref[:, NQ:NQ + LANES], kg_ref[...], dk)
        o_ref[:, NQ:NQ + LANES] = dx.astype(BF16)
        dkg_ref[...] += dg
        dv = jnp.where(lo, fold(dvc_ref, dvp_ref, 0), fold(dvc_ref, dvp_ref, 1))
        o_ref[:, NQ + LANES:] = dv.astype(BF16)

    spec_t = pl.BlockSpec((BLK, LANES), lambda b, i: (b * nq + i, 0))
    spec_g = pl.BlockSpec((1, LANES), lambda b, i: (0, 0))
    spec_c = pl.BlockSpec((BLK, 2 * LANES), lambda b, i: (b * nq + i, 0))
    spec_n = pl.BlockSpec((BLK, 2 * LANES), lambda b, i: (b * nq + jnp.minimum(i + 1, nq - 1), 0))
    row = jax.ShapeDtypeStruct((1, LANES), F32)
    return pl.pallas_call(
        body, name=name, grid=(B, nq),
        in_specs=[pl.BlockSpec((BLK, W), lambda b, i: (b * nq + i, 0)), spec_t, spec_t, spec_t, spec_g, spec_g,
                  pl.BlockSpec((BLK, NQ), lambda b, i: (b * nq + i, 0)), spec_c, spec_n, spec_c, spec_n],
        out_specs=[pl.BlockSpec((BLK, W), lambda b, i: (b * nq + i, 0)), spec_g, spec_g],
        out_shape=[jax.ShapeDtypeStruct((T, W), BF16), row, row],
        compiler_params=_params("arbitrary", "arbitrary"),
    )(qkv, cs, s1, s2, qg, kg, dq, dkc, dkp, dvc, dvp)


SB_TILE = 256
SB_UNROLL = 4
SB_UNROLL_BWD = 2


def _split_heads(x2, scale=None):
    lo = _lo_mask(x2.shape)
    z = jnp.zeros_like(x2)
    if scale is not None:
        x2 = x2 * scale
    return jnp.where(lo, x2, z), jnp.where(lo, z, x2)


def _sb_terms(qh, kj, diagonal):
    z = _dot(qh, kj, NT)
    e = jnp.exp(-jnp.abs(z))
    lb = jnp.minimum(z, 0.0) - jnp.log(1.0 + e)
    L = lb - z
    if not diagonal:
        return lb, L, None, z, e
    strict = lax.broadcasted_iota(jnp.int32, z.shape, 1) < lax.broadcasted_iota(jnp.int32, z.shape, 0)
    return lb, jnp.where(strict, L, 0.0), strict, z, e


def _tri(n, cmp):
    r = lax.broadcasted_iota(jnp.int32, (n, n), 0)
    c = lax.broadcasted_iota(jnp.int32, (n, n), 1)
    return cmp(r, c).astype(BF16)


def _sb_fwd(qkv, B, name):
    T, W = qkv.shape
    NQ = W // 3
    NP = NQ // LANES
    S = T // B
    tq = min(SB_TILE, S)
    nq = S // tq

    def body(q_ref, k_ref, v_ref, o_ref, t_ref):
        i = pl.program_id(2)
        qh = _split_heads(q_ref[...], SCALE)
        U = _tri(tq, lambda r, c: r > c)

        def sweep(tiles, cs, acc):
            chains = [(t, h) for t in range(len(tiles)) for h in range(2)]
            rows = [pl.ds(pl.multiple_of(j * tq, tq), tq) for j, _ in tiles]
            ks = [k_ref[r, :] for r in rows]
            vs = [_split_heads(v_ref[r, :]) for r in rows]
            terms = {(t, h): _sb_terms(qh[h], ks[t], tiles[t][1]) for t, h in chains}
            carry = {}
            for h in range(2):
                c = cs[h]
                for t in range(len(tiles)):
                    carry[t, h] = c
                    c = c + jnp.sum(terms[t, h][1], axis=1, keepdims=True)
                cs = cs[:h] + (c,) + cs[h + 1:]
            cum = {ch: _dot(terms[ch][1].astype(BF16), U, NN) for ch in chains}
            for ch in chains:
                a = jnp.exp(terms[ch][0] + (cum[ch] + carry[ch]))
                if tiles[ch[0]][1]:
                    a = jnp.where(terms[ch][2], a, 0.0)
                acc = acc + _dot(a.astype(BF16), vs[ch[0]][ch[1]], NN)
            return cs, acc

        zero = jnp.zeros((tq, 1), F32)
        carry = sweep([(i, True)], (zero, zero), jnp.zeros((tq, LANES), F32))

        def run(first, count):
            return lambda cr: sweep([(first - t, False) for t in range(count)], *cr)

        carry = lax.fori_loop(0, i // SB_UNROLL, lambda n, cr: run(i - 1 - SB_UNROLL * n, SB_UNROLL)(cr), carry)
        done = i // SB_UNROLL * SB_UNROLL
        u = SB_UNROLL // 2
        while u:
            carry = lax.cond((i & u) != 0, run(i - 1 - done, u), lambda cr: cr, carry)
            done = done + (i & u)
            u //= 2
        cs, acc = carry
        o_ref[...] = acc.astype(BF16)
        t_ref[...] = jnp.where(_lo_mask((tq, LANES)), cs[0], cs[1])

    spec_q = pl.BlockSpec((tq, LANES), lambda b, p, i: (b * nq + i, p))
    return pl.pallas_call(
        body, name=name, grid=(B, NP, nq),
        in_specs=[spec_q, pl.BlockSpec((S, LANES), lambda b, p, i: (b, NP + p)),
                  pl.BlockSpec((S, LANES), lambda b, p, i: (b, 2 * NP + p))],
        out_specs=[spec_q, spec_q],
        out_shape=[jax.ShapeDtypeStruct((T, NQ), BF16), jax.ShapeDtypeStruct((T, NQ), F32)],
        compiler_params=_params("parallel", "parallel", "arbitrary"),
    )(qkv, qkv, qkv)


def _sb_bwd(qkv, q_t, do, do_t, tot, B, name):
    T, W = qkv.shape
    NQ = W // 3
    NP = NQ // LANES
    S = T // B
    tq = min(SB_TILE, S)
    nq = S // tq

    def body(q_ref, k_ref, v_ref, do_ref, qt_ref, dot_ref, t_ref, dq_ref, dk_ref, dv_ref):
        i = pl.program_id(2)

        @pl.when(i == 0)
        def _():
            dk_ref[...] = jnp.zeros_like(dk_ref)
            dv_ref[...] = jnp.zeros_like(dv_ref)
        qh = _split_heads(q_ref[...], SCALE)
        doh = _split_heads(do_ref[...])
        top = lax.broadcasted_iota(jnp.int32, (LANES, tq), 0) < HEAD
        zt = jnp.zeros((LANES, tq), BF16)
        qt = qt_ref[...] * SCALE
        qth = (jnp.where(top, qt, zt), jnp.where(top, zt, qt))
        doth = (jnp.where(top, dot_ref[...], zt), jnp.where(top, zt, dot_ref[...]))
        tt = t_ref[...]
        tot = (tt[:, 0:1], tt[:, HEAD:HEAD + 1])
        Urev = _tri(tq, lambda r, c: r > c)
        Uexc = _tri(tq, lambda r, c: r < c)

        def sweep(tiles, carry):
            nt = len(tiles)
            chains = [(t, h) for t in range(nt) for h in range(2)]
            rows = [pl.ds(pl.multiple_of(j * tq, tq), tq) for j, _ in tiles]
            ks = [k_ref[r, :] for r in rows]
            vs = [v_ref[r, :] for r in rows]
            terms = {(t, h): _sb_terms(qh[h], ks[t], tiles[t][1]) for t, h in chains}
            da = {(t, h): _dot(doh[h], vs[t], NT) for t, h in chains}
            cc = [carry[h][0] for h in range(2)]
            later = {}
            for t, h in chains:
                cc[h] = cc[h] + jnp.sum(terms[t, h][1], axis=1, keepdims=True)
                later[t, h] = tot[h] - cc[h]
            cum = {ch: _dot(terms[ch][1].astype(BF16), Urev, NN) for ch in chains}
            a, g, before = {}, {}, {}
            cg = [carry[h][1] for h in range(2)]
            for ch in chains:
                a[ch] = jnp.exp(terms[ch][0] + (cum[ch] + later[ch]))
                if tiles[ch[0]][1]:
                    a[ch] = jnp.where(terms[ch][2], a[ch], 0.0)
                g[ch] = a[ch] * da[ch]
                before[ch] = cg[ch[1]]
                cg[ch[1]] = cg[ch[1]] + jnp.sum(g[ch], axis=1, keepdims=True)
            G = {ch: _dot(g[ch].astype(BF16), Uexc, NN) for ch in chains}
            dz = {}
            for ch in chains:
                d = g[ch] - jnp.exp(terms[ch][0]) * (g[ch] + (G[ch] + before[ch]))
                if tiles[ch[0]][1]:
                    d = jnp.where(terms[ch][2], d, 0.0)
                dz[ch] = d.astype(BF16)
            dq = [carry[h][2] for h in range(2)]
            for t, h in chains:
                dq[h] = dq[h] + _dot(dz[t, h], ks[t], NN)
            for t in range(nt):
                dk_ref[:, rows[t]] += _dot(qth[0], dz[t, 0], NN) + _dot(qth[1], dz[t, 1], NN)
                dv_ref[:, rows[t]] += _dot(doth[0], a[t, 0].astype(BF16), NN) + _dot(doth[1], a[t, 1].astype(BF16), NN)
            return tuple((cc[h], cg[h], dq[h]) for h in range(2))

        zero = jnp.zeros((tq, 1), F32)
        zq = jnp.zeros((tq, LANES), F32)
        def run(first, count):
            return lambda cr: sweep([(first + t, False) for t in range(count)], cr)

        carry = lax.fori_loop(0, i // SB_UNROLL_BWD, lambda n, cr: run(SB_UNROLL_BWD * n, SB_UNROLL_BWD)(cr),
                              ((zero, zero, zq), (zero, zero, zq)))
        done = i // SB_UNROLL_BWD * SB_UNROLL_BWD
        u = SB_UNROLL_BWD // 2
        while u:
            carry = lax.cond((i & u) != 0, run(done, u), lambda cr: cr, carry)
            done = done + (i & u)
            u //= 2
        carry = sweep([(i, True)], carry)
        dq_ref[...] = jnp.where(_lo_mask((tq, LANES)), carry[0][2], carry[1][2]) * SCALE

    spec_q = pl.BlockSpec((tq, LANES), lambda b, p, i: (b * nq + i, p))
    spec_t = pl.BlockSpec((LANES, tq), lambda b, p, i: (p, b * nq + i))
    spec_s = pl.BlockSpec((LANES, S), lambda b, p, i: (b * NP + p, 0))
    key_side = jax.ShapeDtypeStruct((B * NQ, S), F32)
    return pl.pallas_call(
        body, name=name, grid=(B, NP, nq),
        in_specs=[spec_q, pl.BlockSpec((S, LANES), lambda b, p, i: (b, NP + p)),
                  pl.BlockSpec((S, LANES), lambda b, p, i: (b, 2 * NP + p)), spec_q, spec_t, spec_t, spec_q],
        out_specs=[spec_q, spec_s, spec_s],
        out_shape=[jax.ShapeDtypeStruct((T, NQ), F32), key_side, key_side],
        compiler_params=_params("parallel", "parallel", "arbitrary"),
    )(qkv, qkv, qkv, do, q_t, do_t, tot)


def _adamw(w, g, m, v, name):
    shape = w.shape
    cols = shape[-1]
    rows = math.prod(shape[:-1])
    tr = _pick(rows, max(8, (1 << 19) // max(cols, LANES) // 8 * 8), 8)

    def body(w_ref, g_ref, m_ref, v_ref, d_ref, mo_ref, vo_ref):
        gv = g_ref[...]
        mn = ADAM_B1 * m_ref[...] + (1.0 - ADAM_B1) * gv
        vn = ADAM_B2 * v_ref[...] + (1.0 - ADAM_B2) * (gv * gv)
        m_hat = mn / (1.0 - ADAM_B1 ** ADAM_STEP)
        v_hat = vn / (1.0 - ADAM_B2 ** ADAM_STEP)
        d_ref[...] = -ADAM_LR * (m_hat / (jnp.sqrt(v_hat) + ADAM_EPS) + ADAM_WD * w_ref[...])
        mo_ref[...] = mn
        vo_ref[...] = vn

    spec = pl.BlockSpec((tr, cols), lambda i: (i, 0))
    out = jax.ShapeDtypeStruct((rows, cols), F32)
    d, mn, vn = pl.pallas_call(
        body, name=name, grid=(rows // tr,),
        in_specs=[spec] * 4, out_specs=[spec] * 3, out_shape=[out] * 3,
        compiler_params=_params("parallel"),
    )(w.reshape(rows, cols), g.reshape(rows, cols), m.reshape(rows, cols), v.reshape(rows, cols))
    return d.reshape(shape), mn.reshape(shape), vn.reshape(shape)


def _pad_rows(a, rows):
    return jnp.pad(a, ((0, rows - a.shape[0]), (0, 0)))


def kernel(x, c, positions, ada_w, ada_b, norm1_g, norm2_g, wqkv_a, q_norm_a, k_norm_a, sinks_a, wo_a, wqkv_b, wo_b, w_gate, w_up, w_down, loss_target, m_ada_w, m_ada_b, m_norm1_g, m_norm2_g, m_wqkv_a, m_q_norm_a, m_k_norm_a, m_sinks_a, m_wo_a, m_wqkv_b, m_wo_b, m_w_gate, m_w_up, m_w_down, v_ada_w, v_ada_b, v_norm1_g, v_norm2_g, v_wqkv_a, v_q_norm_a, v_k_norm_a, v_sinks_a, v_wo_a, v_wqkv_b, v_wo_b, v_w_gate, v_w_up, v_w_down):
    B, S, D = x.shape
    T = B * S
    L = ada_w.shape[0]
    NA, NB_ = wqkv_a.shape[0], wqkv_b.shape[0]
    me = 4 * lax.axis_index("x") + 2 * lax.axis_index("y") + lax.axis_index("c")
    xt = x.reshape(T, D)

    col_sharded = [(wqkv_a, NA), (wqkv_b, NB_), (w_gate, L), (w_up, L)]
    row_sharded = [(wo_a, NA), (wo_b, NB_), (w_down, L)]
    pieces, layout = [], []
    for w, n in col_sharded:
        for l in range(n):
            pieces.append(w[l].T.astype(BF16))
            layout.append(w.shape[2])
    for w, n in row_sharded:
        for l in range(n):
            pieces.append(w[l].astype(BF16))
            layout.append(w.shape[1])
    packed = jnp.concatenate(pieces, axis=0)
    R = packed.shape[0]
    gathered = _all_gather(packed, "ag_weights")
    full, off = [], 0
    for rows in layout:
        full.append(gathered[:, off:off + rows, :].reshape(NDEV * rows, D))
        off += rows
    it = iter(full)
    wqkv_a_t = [next(it) for _ in range(NA)]
    wqkv_b_t = [next(it) for _ in range(NB_)]
    wg_t = [next(it) for _ in range(L)]
    wu_t = [next(it) for _ in range(L)]
    wo_a_f = [next(it) for _ in range(NA)]
    wo_b_f = [next(it) for _ in range(NB_)]
    wd_f = [next(it) for _ in range(L)]

    WA = ada_w.shape[2]
    c_all = _all_gather(c, "ag_c").reshape(NDEV * B, D)
    bias = lax.dynamic_slice_in_dim(ada_b, me * WA, WA, axis=1).reshape(L, 1, WA)
    mod_part = _ada_fwd(c_all, ada_w, bias, "ada_fwd")
    mod_all = _all_gather(mod_part.reshape(L * NDEV * B, WA), "ag_mod")
    mod_all = mod_all.reshape(NDEV, L, NDEV * B, WA).transpose(1, 2, 0, 3).reshape(L, NDEV * B, NDEV * WA)
    mod = lax.dynamic_slice_in_dim(mod_all, me * B, B, axis=1)
    mod = mod.reshape(L, B, 6, 1, D)
    sh1, sc1, g1, sh2, sc2, g2 = [mod[:, :, k] for k in range(6)]

    half = ROT // 2
    inv_freq = jnp.power(jnp.float32(ROPE_THETA), -jnp.arange(half, dtype=F32) * 2.0 / ROT)
    ang = positions.reshape(T, 1).astype(F32) * inv_freq[None, :]
    cos, sin = jnp.cos(ang), jnp.sin(ang)
    ones = jnp.ones((T, HEAD - ROT), F32)
    zeros = jnp.zeros((T, HEAD - ROT), F32)
    z8 = jnp.zeros((T, half), F32)
    cs = jnp.tile(jnp.concatenate([cos, cos, ones], axis=1), (1, 2))
    s1 = jnp.tile(jnp.concatenate([-sin, z8, zeros], axis=1), (1, 2))
    s2 = jnp.tile(jnp.concatenate([z8, sin, zeros], axis=1), (1, 2))

    saved = []
    xc = xt
    for l in range(L):
        j = l // 2
        h1 = _norm_mod(xc, norm1_g[l:l + 1], sc1[l], sh1[l], S, f"norm1_{l}")
        sv = dict(x_in=xc, h1=h1)
        if l % 2 == 0:
            qkv = _mm_nt(h1, wqkv_a_t[j], F32, f"qkv_a_{l}")
            qg = jnp.tile(q_norm_a[j:j + 1], (1, 2))
            kg = jnp.tile(k_norm_a[j:j + 1], (1, 2))
            qn, kd, vd = _qk_prep(qkv, cs, s1, s2, qg, kg, f"qk_prep_{l}")
            sink2 = jnp.repeat(sinks_a[j].reshape(-1, 2), HEAD, axis=1).reshape(-1, 1, LANES)
            attn, lse = _swa_fwd(qn, kd, vd, sink2, B, f"swa_fwd_{l}")
            sv.update(qkv=qkv, qg=qg, kg=kg, qn=qn, kd=kd, vd=vd, sink2=sink2, lse=lse)
            wo = wo_a_f[j]
        else:
            qkv = _mm_nt(h1, wqkv_b_t[j], BF16, f"qkv_b_{l}")
            attn, tot = _sb_fwd(qkv, B, f"sb_fwd_{l}")
            sv.update(qkv=qkv, tot=tot)
            wo = wo_b_f[j]
        y1, xm = _mm_res(attn, wo, xc, g1[l], S, f"attn_out_{l}")
        h2 = _norm_mod(xm, norm2_g[l:l + 1], sc2[l], sh2[l], S, f"norm2_{l}")
        gate, up, act = _swiglu_fwd(h2, wg_t[l], wu_t[l], f"swiglu_fwd_{l}")
        y2, xc = _mm_res(act, wd_f[l], xm, g2[l], S, f"mlp_out_{l}")
        sv.update(attn=attn, y1=y1, x_mid=xm, h2=h2, gate=gate, up=up, act=act, y2=y2)
        saved.append(sv)

    dx, loss_tile = _loss_head(xc, loss_target.reshape(T, D), "loss_head")

    g_qkv_a, g_qkv_b, g_gate, g_up, g_wo_a, g_wo_b, g_down = ([None] * NA, [None] * NB_, [None] * L, [None] * L,
                                                             [None] * NA, [None] * NB_, [None] * L)
    dmod = [None] * L
    dn1, dn2 = [None] * L, [None] * L
    dqg, dkg, dsink = [None] * NA, [None] * NA, [None] * NA
    for l in reversed(range(L)):
        j = l // 2
        sv = saved[l]
        dy2, dg2 = _gate_bwd(dx, sv["y2"], g2[l], S, f"gate2_bwd_{l}")
        dgate, dup = _swiglu_bwd(dy2, wd_f[l], sv["gate"], sv["up"], f"swiglu_bwd_{l}")
        g_down[l] = _mm_tn(sv["act"], dy2, f"dw_down_{l}")
        dh2 = _mm_nn([(dgate, wg_t[l]), (dup, wu_t[l])], f"dh2_{l}")
        g_gate[l] = _mm_tn(dgate, sv["h2"], f"dw_gate_{l}")
        g_up[l] = _mm_tn(dup, sv["h2"], f"dw_up_{l}")
        dxm, dsh2, dsc2, dn2[l] = _norm_mod_bwd(sv["x_mid"], dh2, dx, norm2_g[l:l + 1], sc2[l], S, f"norm2_bwd_{l}")
        dy1, dg1 = _gate_bwd(dxm, sv["y1"], g1[l], S, f"gate1_bwd_{l}")
        wo = wo_a_f[j] if l % 2 == 0 else wo_b_f[j]
        dattn = _mm_nt(dy1, wo, BF16, f"dattn_{l}")
        gwo = _mm_tn(sv["attn"], dy1, f"dw_o_{l}")
        if l % 2 == 0:
            g_wo_a[j] = gwo
            dq, dkc, dkp, dvc, dvp, dsink[j] = _swa_bwd(sv["qn"], sv["kd"], sv["vd"], sv["sink2"], dattn, sv["lse"], B,
                                                        f"swa_bwd_{l}")
            dqkv, dqg[j], dkg[j] = _qk_prep_bwd(sv["qkv"], cs, s1, s2, sv["qg"], sv["kg"], dq, dkc, dkp, dvc, dvp, B,
                                                f"qk_prep_bwd_{l}")
            wt = wqkv_a_t[j]
        else:
            g_wo_b[j] = gwo
            nqb = sv["qkv"].shape[1] // 3
            dq, dk_t, dv_t = _sb_bwd(sv["qkv"], sv["qkv"][:, :nqb].T, dattn, dattn.T, sv["tot"], B, f"sb_bwd_{l}")
            dk, dv = [t.reshape(B, nqb, S).transpose(0, 2, 1).reshape(T, nqb) for t in (dk_t, dv_t)]
            dqkv = jnp.concatenate([dq, dk, dv], axis=1).astype(BF16)
            wt = wqkv_b_t[j]
        dh1 = _mm_nn([(dqkv, wt)], f"dh1_{l}")
        gq = _mm_tn(dqkv, sv["h1"], f"dw_qkv_{l}")
        if l % 2 == 0:
            g_qkv_a[j] = gq
        else:
            g_qkv_b[j] = gq
        dx, dsh1, dsc1, dn1[l] = _norm_mod_bwd(sv["x_in"], dh1, dxm, norm1_g[l:l + 1], sc1[l], S, f"norm1_bwd_{l}")
        dmod[l] = jnp.concatenate([dsh1, dsc1, dg1, dsh2, dsc2, dg2], axis=1)
    grad_x = dx.reshape(B, S, D)

    ndm = L * 6
    dmod_rows = jnp.stack(dmod, axis=1).reshape(B * ndm, D)
    misc = jnp.concatenate(
        [jnp.concatenate(dn1, axis=0).reshape(B * L, D), jnp.concatenate(dn2, axis=0).reshape(B * L, D),
         _pad_rows(jnp.concatenate([jnp.pad(r, ((0, 0), (0, D - LANES))) for r in dqg + dkg]
                                   + [jnp.pad(r[:, 0, ::HEAD].reshape(1, -1), ((0, 0), (0, D - 2 * r.shape[0]))) for r in dsink]
                                   + [jnp.pad(loss_tile[0:1, 0:1], ((0, 0), (0, D - 1)))], axis=0), 8)], axis=0)
    nmisc = misc.shape[0]
    small = _all_gather(jnp.concatenate([dmod_rows, _pad_rows(misc, -(-nmisc // 8) * 8)], axis=0), "ag_small")
    dmod_all = small[:, :B * ndm].reshape(NDEV * B, ndm, D)
    g_ada_b = _sum_leading(dmod_all, "sum_dmod").reshape(L, 6 * D)
    misc_sum = _sum_leading(small[:, B * ndm:], "sum_misc")
    g_n1 = misc_sum[0:B * L].reshape(L, B, D)
    g_n2 = misc_sum[B * L:2 * B * L].reshape(L, B, D)
    g_norm1 = _sum_leading(g_n1.transpose(1, 0, 2), "sum_n1")
    g_norm2 = _sum_leading(g_n2.transpose(1, 0, 2), "sum_n2")
    o = 2 * B * L
    g_qn = misc_sum[o:o + NA, :HEAD]
    g_kn = misc_sum[o + NA:o + 2 * NA, :HEAD]
    nsink = sinks_a.shape[1]
    g_sink = misc_sum[o + 2 * NA:o + 3 * NA, :nsink]
    loss = misc_sum[o + 3 * NA, 0]

    dmod_loc = lax.dynamic_slice_in_dim(dmod_all.reshape(NDEV * B, L, 6 * D), me * WA, WA, axis=2)
    g_ada_w = _ada_bwd(c_all, dmod_loc.transpose(1, 0, 2), "ada_bwd")

    parts = []
    for g in g_qkv_a + g_qkv_b + g_gate + g_up + g_wo_a + g_wo_b + g_down:
        parts.append(g.reshape(NDEV, g.shape[0] // NDEV, D).astype(BF16))
    partial = jnp.concatenate(parts, axis=1)
    received = _exchange(partial, "grad_exchange")
    gsum = _sum_leading(received, "grad_sum")
    shards, off = [], 0
    for rows in layout:
        shards.append(gsum[off:off + rows])
        off += rows
    it = iter(shards)
    gw_qkv_a = jnp.stack([next(it).T for _ in range(NA)])
    gw_qkv_b = jnp.stack([next(it).T for _ in range(NB_)])
    gw_gate = jnp.stack([next(it).T for _ in range(L)])
    gw_up = jnp.stack([next(it).T for _ in range(L)])
    gw_wo_a = jnp.stack([next(it) for _ in range(NA)])
    gw_wo_b = jnp.stack([next(it) for _ in range(NB_)])
    gw_down = jnp.stack([next(it) for _ in range(L)])

    grads = [g_ada_w, g_ada_b, g_norm1, g_norm2, gw_qkv_a, g_qn, g_kn, g_sink, gw_wo_a, gw_qkv_b, gw_wo_b,
             gw_gate, gw_up, gw_down]
    ws = [ada_w, ada_b, norm1_g, norm2_g, wqkv_a, q_norm_a, k_norm_a, sinks_a, wo_a, wqkv_b, wo_b, w_gate, w_up, w_down]
    ms = [m_ada_w, m_ada_b, m_norm1_g, m_norm2_g, m_wqkv_a, m_q_norm_a, m_k_norm_a, m_sinks_a, m_wo_a, m_wqkv_b,
          m_wo_b, m_w_gate, m_w_up, m_w_down]
    vs = [v_ada_w, v_ada_b, v_norm1_g, v_norm2_g, v_wqkv_a, v_q_norm_a, v_k_norm_a, v_sinks_a, v_wo_a, v_wqkv_b,
          v_wo_b, v_w_gate, v_w_up, v_w_down]
    deltas, new_m, new_v = [], [], []
    for k, (w, g, m, v) in enumerate(zip(ws, grads, ms, vs)):
        g = g.reshape(w.shape)
        d, mn, vn = _adamw(w, g, m, v, f"adamw_{k}")
        grads[k] = g
        deltas.append(d)
        new_m.append(mn)
        new_v.append(vn)
    return (loss, grad_x, *grads, *deltas, *new_m, *new_v)
```

```python
import functools
import math

import jax
import jax.numpy as jnp
from jax import lax
from jax.experimental import pallas as pl
from jax.experimental.pallas import tpu as pltpu

F32 = jnp.float32
BF16 = jnp.bfloat16
NDEV = 8
HEAD = 64
BLK = 128
LANES = 128
EPS = 1e-6
ROT = HEAD // 4
ROPE_THETA = 500000.0
SCALE = HEAD ** -0.5
NEG = -1e30
VMEM_LIMIT = 56 * 1024 * 1024
MESH = pl.DeviceIdType.MESH
HIGH = lax.Precision.HIGHEST

ADAM_LR = 0.001
ADAM_B1 = 0.9
ADAM_B2 = 0.999
ADAM_EPS = 1e-08
ADAM_WD = 0.01
ADAM_STEP = 10


def _params(*sem):
    return pltpu.CompilerParams(dimension_semantics=sem, vmem_limit_bytes=VMEM_LIMIT)


def _pick(n, cap, mult):
    if n <= cap:
        return n
    best = None
    for t in range(mult, cap + 1, mult):
        if n % t == 0:
            best = t
    assert best is not None, (n, cap, mult)
    return best


def _dot(a, b, dims, precision=None):
    return lax.dot_general(a, b, (dims, ((), ())), preferred_element_type=F32, precision=precision)


NN = ((1,), (0,))
NT = ((1,), (1,))
TN = ((0,), (0,))


def _all_gather(x, name):
    m, n = x.shape

    def body(x_ref, out_ref, send_sems, recv_sems, local_sem):
        ix, iy, ic = lax.axis_index("x"), lax.axis_index("y"), lax.axis_index("c")
        me, sibling = (ix, iy, ic), (ix, iy, 1 - ic)
        chips = [(1 - ix, iy), (ix, 1 - iy), (1 - ix, 1 - iy)]

        def slab(px, py, pc):
            return out_ref.at[4 * px + 2 * py + pc]

        def copy(k, block, to, src=None):
            return pltpu.make_async_remote_copy(
                src_ref=slab(*block) if src is None else src, dst_ref=slab(*block),
                send_sem=send_sems.at[k], recv_sem=recv_sems.at[k], device_id=to, device_id_type=MESH)

        mine = pltpu.make_async_copy(x_ref, slab(*me), local_sem)
        mine.start()
        first = [copy(0, me, sibling, src=x_ref)]
        first += [copy(1 + j, me, (*chip, ic), src=x_ref) for j, chip in enumerate(chips)]
        for cp in first:
            cp.start()
        passed = [copy(4 + j, (*chip, ic), sibling) for j, chip in enumerate(chips)]
        for j, chip in enumerate(chips):
            copy(1 + j, (*chip, ic), me).wait_recv()
            passed[j].start()
        copy(0, sibling, me).wait_recv()
        for j, chip in enumerate(chips):
            copy(4 + j, (*chip, 1 - ic), me).wait_recv()
        for cp in first + passed:
            cp.wait_send()
        mine.wait()

    return pl.pallas_call(
        body, name=name,
        out_shape=jax.ShapeDtypeStruct((NDEV, m, n), x.dtype),
        in_specs=[pl.BlockSpec(memory_space=pl.ANY)],
        out_specs=pl.BlockSpec(memory_space=pl.ANY),
        scratch_shapes=[pltpu.SemaphoreType.DMA((7,)), pltpu.SemaphoreType.DMA((7,)), pltpu.SemaphoreType.DMA(())],
    )(x)


COMM_SEMS = [pltpu.SemaphoreType.DMA((NDEV - 1,)), pltpu.SemaphoreType.DMA((NDEV - 1,)), pltpu.SemaphoreType.DMA(())]
HBM_SPEC = pl.BlockSpec(memory_space=pl.ANY)


def _direct_copies(src_ref, dst_ref, sems, scatter):
    send_sems, recv_sems, own_sem = sems
    ix, iy, ic = lax.axis_index("x"), lax.axis_index("y"), lax.axis_index("c")
    me = 4 * ix + 2 * iy + ic
    copies = [pltpu.make_async_copy(src_ref.at[me] if scatter else src_ref, dst_ref.at[me], own_sem)]
    for k in range(1, NDEV):
        px = 1 - ix if k & 4 else ix
        py = 1 - iy if k & 2 else iy
        pc = 1 - ic if k & 1 else ic
        copies.append(pltpu.make_async_remote_copy(
            src_ref=src_ref.at[4 * px + 2 * py + pc] if scatter else src_ref, dst_ref=dst_ref.at[me],
            send_sem=send_sems.at[k - 1], recv_sem=recv_sems.at[k - 1],
            device_id=(px, py, pc), device_id_type=MESH))
    return copies


def _exchange(p, name):
    def body(p_ref, r_ref, *sems):
        copies = _direct_copies(p_ref, r_ref, sems, True)
        for cp in copies:
            cp.start()
        for cp in copies:
            cp.wait()

    return pl.pallas_call(
        body, name=name,
        out_shape=jax.ShapeDtypeStruct(p.shape, p.dtype),
        in_specs=[HBM_SPEC], out_specs=HBM_SPEC, scratch_shapes=COMM_SEMS,
    )(p)


def _sum_leading(r, name):
    k, m, n = r.shape
    mult = 8 * (4 // r.dtype.itemsize)
    tm = _pick(m, max(mult, (4 * 1024 * 1024) // (k * n * r.dtype.itemsize) // mult * mult), mult)

    def body(r_ref, o_ref):
        acc = r_ref[0].astype(F32)
        for s in range(1, k):
            acc = acc + r_ref[s].astype(F32)
        o_ref[...] = acc

    return pl.pallas_call(
        body, name=name, grid=(m // tm,),
        in_specs=[pl.BlockSpec((k, tm, n), lambda i: (0, i, 0))],
        out_specs=pl.BlockSpec((tm, n), lambda i: (i, 0)),
        out_shape=jax.ShapeDtypeStruct((m, n), F32),
        compiler_params=_params("parallel"),
    )(r)


def _mm_nt(a, bt, out_dtype, name):
    M, K = a.shape
    N = bt.shape[0]
    tm, tn = _pick(M, 512, 8), _pick(N, 1536, LANES)

    def body(a_ref, b_ref, o_ref):
        o_ref[...] = _dot(a_ref[...], b_ref[...], NT).astype(out_dtype)

    return pl.pallas_call(
        body, name=name, grid=(N // tn, M // tm),
        in_specs=[pl.BlockSpec((tm, K), lambda j, i: (i, 0)), pl.BlockSpec((tn, K), lambda j, i: (j, 0))],
        out_specs=pl.BlockSpec((tm, tn), lambda j, i: (i, j)),
        out_shape=jax.ShapeDtypeStruct((M, N), out_dtype),
        compiler_params=_params("parallel", "parallel"),
    )(a, bt)


def _mm_nn(pairs, name):
    M = pairs[0][0].shape[0]
    N = pairs[0][1].shape[1]
    tm, tn = _pick(M, 512, 8), _pick(N, 1024, LANES)
    np_ = len(pairs)

    def body(*refs):
        o_ref = refs[-1]
        acc = _dot(refs[0][...], refs[1][...], NN)
        for p in range(1, np_):
            acc = acc + _dot(refs[2 * p][...], refs[2 * p + 1][...], NN)
        o_ref[...] = acc

    in_specs, args = [], []
    for a, b in pairs:
        K = a.shape[1]
        in_specs += [pl.BlockSpec((tm, K), lambda i, j: (i, 0)), pl.BlockSpec((K, tn), lambda i, j: (0, j))]
        args += [a, b]
    return pl.pallas_call(
        body, name=name, grid=(M // tm, N // tn),
        in_specs=in_specs,
        out_specs=pl.BlockSpec((tm, tn), lambda i, j: (i, j)),
        out_shape=jax.ShapeDtypeStruct((M, N), F32),
        compiler_params=_params("parallel", "parallel"),
    )(*args)


def _mm_tn(a, b, name):
    M, N1 = a.shape
    N2 = b.shape[1]
    t1, tk = _pick(N1, 1536, LANES), _pick(M, 512, 8)

    def body(a_ref, b_ref, o_ref):
        @pl.when(pl.program_id(1) == 0)
        def _():
            o_ref[...] = jnp.zeros_like(o_ref)
        o_ref[...] += _dot(a_ref[...], b_ref[...], TN)

    return pl.pallas_call(
        body, name=name, grid=(N1 // t1, M // tk),
        in_specs=[pl.BlockSpec((tk, t1), lambda i, k: (k, i)), pl.BlockSpec((tk, N2), lambda i, k: (k, 0))],
        out_specs=pl.BlockSpec((t1, N2), lambda i, k: (i, 0)),
        out_shape=jax.ShapeDtypeStruct((N1, N2), F32),
        compiler_params=_params("parallel", "arbitrary"),
    )(a, b)


def _mm_res(a, w, x, gate, S, name):
    T, K = a.shape
    D = w.shape[1]
    tm, tn = _pick(S, 512, 8), _pick(D, 1024, LANES)
    nb = S // tm

    def body(a_ref, w_ref, x_ref, g_ref, y_ref, o_ref):
        y = _dot(a_ref[...], w_ref[...], NN)
        y_ref[...] = y.astype(BF16)
        o_ref[...] = x_ref[...] + g_ref[0] * y

    return pl.pallas_call(
        body, name=name, grid=(T // tm, D // tn),
        in_specs=[pl.BlockSpec((tm, K), lambda i, j: (i, 0)), pl.BlockSpec((K, tn), lambda i, j: (0, j)),
                  pl.BlockSpec((tm, tn), lambda i, j: (i, j)), pl.BlockSpec((1, 1, tn), lambda i, j: (i // nb, 0, j))],
        out_specs=[pl.BlockSpec((tm, tn), lambda i, j: (i, j)), pl.BlockSpec((tm, tn), lambda i, j: (i, j))],
        out_shape=[jax.ShapeDtypeStruct((T, D), BF16), jax.ShapeDtypeStruct((T, D), F32)],
        compiler_params=_params("parallel", "parallel"),
    )(a, w, x, gate)


def _swiglu_fwd(h, wgt, wut, name):
    T, D = h.shape
    F = wgt.shape[0]
    tm, tn = _pick(T, 512, 8), _pick(F, 1536, LANES)

    def body(h_ref, g_ref, u_ref, go_ref, uo_ref, a_ref):
        hh = h_ref[...]
        g = _dot(hh, g_ref[...], NT)
        u = _dot(hh, u_ref[...], NT)
        go_ref[...] = g.astype(BF16)
        uo_ref[...] = u.astype(BF16)
        a_ref[...] = (g * jax.nn.sigmoid(g) * u).astype(BF16)

    spec_w = pl.BlockSpec((tn, D), lambda j, i: (j, 0))
    spec_o = pl.BlockSpec((tm, tn), lambda j, i: (i, j))
    out = jax.ShapeDtypeStruct((T, F), BF16)
    return pl.pallas_call(
        body, name=name, grid=(F // tn, T // tm),
        in_specs=[pl.BlockSpec((tm, D), lambda j, i: (i, 0)), spec_w, spec_w],
        out_specs=[spec_o, spec_o, spec_o],
        out_shape=[out, out, out],
        compiler_params=_params("parallel", "parallel"),
    )(h, wgt, wut)


def _swiglu_bwd(dy, wd, gate, up, name):
    T, D = dy.shape
    F = wd.shape[0]
    tm, tn = _pick(T, 512, 8), _pick(F, 1536, LANES)

    def body(dy_ref, w_ref, g_ref, u_ref, dg_ref, du_ref):
        da = _dot(dy_ref[...], w_ref[...], NT)
        g = g_ref[...].astype(F32)
        sg = jax.nn.sigmoid(g)
        silu = g * sg
        du_ref[...] = (da * silu).astype(BF16)
        dg_ref[...] = (da * u_ref[...].astype(F32) * (sg + silu * (1.0 - sg))).astype(BF16)

    spec_o = pl.BlockSpec((tm, tn), lambda j, i: (i, j))
    return pl.pallas_call(
        body, name=name, grid=(F // tn, T // tm),
        in_specs=[pl.BlockSpec((tm, D), lambda j, i: (i, 0)), pl.BlockSpec((tn, D), lambda j, i: (j, 0)), spec_o, spec_o],
        out_specs=[spec_o, spec_o],
        out_shape=[jax.ShapeDtypeStruct((T, F), BF16), jax.ShapeDtypeStruct((T, F), BF16)],
        compiler_params=_params("parallel", "parallel"),
    )(dy, wd, gate, up)


def _norm_mod(x, gain, sc, sh, S, name):
    T, D = x.shape
    tm = _pick(S, 512, 8)
    nb = S // tm

    def body(x_ref, g_ref, sc_ref, sh_ref, o_ref):
        xv = x_ref[...]
        r = lax.rsqrt(jnp.mean(xv * xv, axis=-1, keepdims=True) + EPS)
        o_ref[...] = ((xv * r) * g_ref[...] * (1.0 + sc_ref[0]) + sh_ref[0]).astype(BF16)

    spec_b = pl.BlockSpec((1, 1, D), lambda i: (i // nb, 0, 0))
    return pl.pallas_call(
        body, name=name, grid=(T // tm,),
        in_specs=[pl.BlockSpec((tm, D), lambda i: (i, 0)), pl.BlockSpec((1, D), lambda i: (0, 0)), spec_b, spec_b],
        out_specs=pl.BlockSpec((tm, D), lambda i: (i, 0)),
        out_shape=jax.ShapeDtypeStruct((T, D), BF16),
        compiler_params=_params("parallel"),
    )(x, gain, sc, sh)


def _norm_mod_bwd(x, dh, dres, gain, sc, S, name):
    T, D = x.shape
    B = T // S
    tm = _pick(S, 256, 8)
    nb = S // tm

    def body(x_ref, dh_ref, dr_ref, g_ref, sc_ref, o_ref, dsh_ref, dsc_ref, dg_ref):
        @pl.when(pl.program_id(1) == 0)
        def _():
            dsh_ref[...] = jnp.zeros_like(dsh_ref)
            dsc_ref[...] = jnp.zeros_like(dsc_ref)
            dg_ref[...] = jnp.zeros_like(dg_ref)
        xv, dhv, g = x_ref[...], dh_ref[...], g_ref[...]
        r = lax.rsqrt(jnp.mean(xv * xv, axis=-1, keepdims=True) + EPS)
        xhat = xv * r
        dsh_ref[0] += jnp.sum(dhv, axis=0, keepdims=True)
        dsc_ref[0] += jnp.sum(dhv * (xhat * g), axis=0, keepdims=True)
        dn = dhv * (1.0 + sc_ref[0])
        dg_ref[0] += jnp.sum(dn * xhat, axis=0, keepdims=True)
        dxh = dn * g
        o_ref[...] = dr_ref[...] + r * (dxh - xhat * jnp.mean(dxh * xhat, axis=-1, keepdims=True))

    spec_t = pl.BlockSpec((tm, D), lambda b, i: (b * nb + i, 0))
    spec_b = pl.BlockSpec((1, 1, D), lambda b, i: (b, 0, 0))
    red = jax.ShapeDtypeStruct((B, 1, D), F32)
    return pl.pallas_call(
        body, name=name, grid=(B, nb),
        in_specs=[spec_t, spec_t, spec_t, pl.BlockSpec((1, D), lambda b, i: (0, 0)), spec_b],
        out_specs=[spec_t, spec_b, spec_b, spec_b],
        out_shape=[jax.ShapeDtypeStruct((T, D), F32), red, red, red],
        compiler_params=_params("parallel", "arbitrary"),
    )(x, dh, dres, gain, sc)


def _gate_bwd(dx, y, gate, S, name):
    T, D = dx.shape
    B = T // S
    tm = _pick(S, 512, 8)
    nb = S // tm

    def body(dx_ref, y_ref, g_ref, dy_ref, dg_ref):
        @pl.when(pl.program_id(1) == 0)
        def _():
            dg_ref[...] = jnp.zeros_like(dg_ref)
        d = dx_ref[...]
        dy_ref[...] = (d * g_ref[0]).astype(BF16)
        dg_ref[0] += jnp.sum(d * y_ref[...].astype(F32), axis=0, keepdims=True)

    spec_t = pl.BlockSpec((tm, D), lambda b, i: (b * nb + i, 0))
    spec_b = pl.BlockSpec((1, 1, D), lambda b, i: (b, 0, 0))
    return pl.pallas_call(
        body, name=name, grid=(B, nb),
        in_specs=[spec_t, spec_t, spec_b],
        out_specs=[spec_t, spec_b],
        out_shape=[jax.ShapeDtypeStruct((T, D), BF16), jax.ShapeDtypeStruct((B, 1, D), F32)],
        compiler_params=_params("parallel", "arbitrary"),
    )(dx, y, gate)


def _loss_head(y, target, name):
    T, D = y.shape
    tm = _pick(T, 512, 8)

    def body(y_ref, t_ref, dy_ref, l_ref):
        @pl.when(pl.program_id(0) == 0)
        def _():
            l_ref[...] = jnp.zeros_like(l_ref)
        e = y_ref[...] - t_ref[...]
        dy_ref[...] = e * (1.0 / D)
        l_ref[...] += 0.5 * jnp.sum(jnp.mean(e * e, axis=-1, keepdims=True), axis=0, keepdims=True)

    spec = pl.BlockSpec((tm, D), lambda i: (i, 0))
    return pl.pallas_call(
        body, name=name, grid=(T // tm,),
        in_specs=[spec, spec],
        out_specs=[spec, pl.BlockSpec((8, LANES), lambda i: (0, 0))],
        out_shape=[jax.ShapeDtypeStruct((T, D), F32), jax.ShapeDtypeStruct((8, LANES), F32)],
        compiler_params=_params("arbitrary"),
    )(y, target)


def _ada_fwd(c_all, ada_w, bias, name):
    NB, D = c_all.shape
    L, _, W = ada_w.shape

    def body(c_ref, w_ref, b_ref, o_ref):
        cv = c_ref[...]
        cond = cv * jax.nn.sigmoid(cv)
        o_ref[0] = _dot(cond, w_ref[0], NN, HIGH) + b_ref[0]

    return pl.pallas_call(
        body, name=name, grid=(L,),
        in_specs=[pl.BlockSpec((NB, D), lambda l: (0, 0)), pl.BlockSpec((1, D, W), lambda l: (l, 0, 0)),
                  pl.BlockSpec((1, 1, W), lambda l: (l, 0, 0))],
        out_specs=pl.BlockSpec((1, NB, W), lambda l: (l, 0, 0)),
        out_shape=jax.ShapeDtypeStruct((L, NB, W), F32),
        compiler_params=_params("parallel"),
    )(c_all, ada_w, bias)


def _ada_bwd(c_all, dmod, name):
    NB, D = c_all.shape
    L, _, W = dmod.shape

    def body(c_ref, d_ref, o_ref):
        cv = c_ref[...]
        cond = cv * jax.nn.sigmoid(cv)
        o_ref[0] = _dot(cond, d_ref[0], TN, HIGH)

    return pl.pallas_call(
        body, name=name, grid=(L,),
        in_specs=[pl.BlockSpec((NB, D), lambda l: (0, 0)), pl.BlockSpec((1, NB, W), lambda l: (l, 0, 0))],
        out_specs=pl.BlockSpec((1, D, W), lambda l: (l, 0, 0)),
        out_shape=jax.ShapeDtypeStruct((L, D, W), F32),
        compiler_params=_params("parallel"),
    )(c_all, dmod)


def _lo_mask(shape):
    return lax.broadcasted_iota(jnp.int32, shape, len(shape) - 1) < HEAD


def _head_sum_matrix():
    r = lax.broadcasted_iota(jnp.int32, (LANES, LANES), 0) // HEAD
    c = lax.broadcasted_iota(jnp.int32, (LANES, LANES), 1) // HEAD
    return (r == c).astype(F32)


def _rope(y, cs, s1, s2):
    return y * cs + pltpu.roll(y, LANES - ROT // 2, 1) * s1 + pltpu.roll(y, ROT // 2, 1) * s2


def _rope_bwd(d, cs, s1, s2):
    return d * cs + pltpu.roll(d * s1, ROT // 2, 1) + pltpu.roll(d * s2, LANES - ROT // 2, 1)


def _qk_prep(qkv, cs, s1, s2, qg, kg, name):
    T, W = qkv.shape
    NQ = W - 2 * LANES
    tm = _pick(T, 512, 8)

    def body(x_ref, cs_ref, s1_ref, s2_ref, qg_ref, kg_ref, q_ref, k_ref, v_ref):
        P = _head_sum_matrix()
        cs_, s1_, s2_ = cs_ref[...], s1_ref[...], s2_ref[...]
        lo = _lo_mask((tm, LANES))

        def norm_rope(xv, g):
            ms = _dot(xv * xv, P, NN, HIGH) * (1.0 / HEAD)
            return _rope(xv * lax.rsqrt(ms + EPS) * g, cs_, s1_, s2_)

        for j in range(NQ // LANES):
            q_ref[:, j * LANES:(j + 1) * LANES] = norm_rope(x_ref[:, j * LANES:(j + 1) * LANES], qg_ref[...]).astype(BF16)
        kr = norm_rope(x_ref[:, NQ:NQ + LANES], kg_ref[...])
        ks = pltpu.roll(kr, HEAD, 1)
        k_ref[:, :LANES] = jnp.where(lo, kr, ks).astype(BF16)
        k_ref[:, LANES:] = jnp.where(lo, ks, kr).astype(BF16)
        vr = x_ref[:, NQ + LANES:]
        vs = pltpu.roll(vr, HEAD, 1)
        v_ref[:, :LANES] = jnp.where(lo, vr, vs).astype(BF16)
        v_ref[:, LANES:] = jnp.where(lo, vs, vr).astype(BF16)

    spec_t = pl.BlockSpec((tm, LANES), lambda i: (i, 0))
    spec_g = pl.BlockSpec((1, LANES), lambda i: (0, 0))
    return pl.pallas_call(
        body, name=name, grid=(T // tm,),
        in_specs=[pl.BlockSpec((tm, W), lambda i: (i, 0)), spec_t, spec_t, spec_t, spec_g, spec_g],
        out_specs=[pl.BlockSpec((tm, NQ), lambda i: (i, 0)), pl.BlockSpec((tm, 2 * LANES), lambda i: (i, 0)),
                   pl.BlockSpec((tm, 2 * LANES), lambda i: (i, 0))],
        out_shape=[jax.ShapeDtypeStruct((T, NQ), BF16), jax.ShapeDtypeStruct((T, 2 * LANES), BF16),
                   jax.ShapeDtypeStruct((T, 2 * LANES), BF16)],
        compiler_params=_params("parallel"),
    )(qkv, cs, s1, s2, qg, kg)


def _stack_heads(x2):
    lo = _lo_mask(x2.shape)
    z = jnp.zeros_like(x2)
    return jnp.concatenate([jnp.where(lo, x2, z), jnp.where(lo, z, x2)], axis=0)


def _unstack_heads(xs):
    r = xs.shape[0] // 2
    return jnp.where(_lo_mask((r, LANES)), xs[:r], xs[r:])


def _swa_valid(i):
    qo = lax.broadcasted_iota(jnp.int32, (2 * BLK, 2 * BLK), 0) % BLK
    kc_ = lax.broadcasted_iota(jnp.int32, (2 * BLK, 2 * BLK), 1)
    rel = qo + BLK - kc_
    return (rel >= 0) & (rel < BLK) & ((kc_ >= BLK) | (i > 0))


def _swa_scores(q2, kk, sink2, valid):
    qs = _stack_heads(q2) * SCALE
    s = _dot(qs, kk, NT)
    sk = jnp.concatenate([jnp.broadcast_to(sink2[:, 0:1], (BLK, 1)), jnp.broadcast_to(sink2[:, HEAD:HEAD + 1], (BLK, 1))], axis=0)
    return qs, jnp.where(valid, s, NEG), sk


def _swa_fwd(q, kd, vd, sink2, B, name):
    T, NQ = q.shape
    NP = NQ // LANES
    nq = T // B // BLK
    NG = kd.shape[1] // LANES
    grp = NP // NG

    def body(q_ref, kp_ref, kc_ref, vp_ref, vc_ref, s_ref, o_ref, l_ref):
        valid = _swa_valid(pl.program_id(2))
        kk = jnp.concatenate([kp_ref[...], kc_ref[...]], axis=0)
        vs = _stack_heads(jnp.concatenate([vp_ref[...], vc_ref[...]], axis=0))
        sls = [slice(jj * LANES, (jj + 1) * LANES) for jj in range(grp)]
        sc = [_swa_scores(q_ref[:, sl], kk, s_ref[jj], valid) for jj, sl in enumerate(sls)]
        ms = [jnp.maximum(jnp.max(s, axis=1, keepdims=True), sk) for _, s, sk in sc]
        ps = [jnp.where(valid, jnp.exp(s - m), 0.0) for (_, s, _), m in zip(sc, ms)]
        ls = [jnp.sum(p, axis=1, keepdims=True) + jnp.exp(sk - m) for p, (_, _, sk), m in zip(ps, sc, ms)]
        ps = [(p * (1.0 / l)).astype(BF16) for p, l in zip(ps, ls)]
        os_ = [_dot(jnp.concatenate([p[:BLK], p[BLK:]], axis=1), vs, NN) for p in ps]
        for sl, o, m, l in zip(sls, os_, ms, ls):
            o_ref[:, sl] = o.astype(BF16)
            l_ref[:, sl] = _unstack_heads(jnp.broadcast_to(m + jnp.log(l), (2 * BLK, LANES)))

    spec_q = pl.BlockSpec((BLK, grp * LANES), lambda b, g, i: (b * nq + i, g))
    spec_p = pl.BlockSpec((BLK, LANES), lambda b, g, i: (b * nq + jnp.maximum(i - 1, 0), g))
    spec_c = pl.BlockSpec((BLK, LANES), lambda b, g, i: (b * nq + i, g))
    return pl.pallas_call(
        body, name=name, grid=(B, NG, nq),
        in_specs=[spec_q, spec_p, spec_c, spec_p, spec_c, pl.BlockSpec((grp, 1, LANES), lambda b, g, i: (g, 0, 0))],
        out_specs=[spec_q, spec_q],
        out_shape=[jax.ShapeDtypeStruct((T, NQ), BF16), jax.ShapeDtypeStruct((T, NQ), F32)],
        compiler_params=_params("parallel", "parallel", "parallel"),
    )(q, kd, kd, vd, vd, sink2)


def _swa_bwd(q, kd, vd, sink2, do, lse, B, name):
    T, NQ = q.shape
    NP = NQ // LANES
    nq = T // B // BLK
    NG = kd.shape[1] // LANES
    grp = NP // NG

    def body(q_ref, kp_ref, kc_ref, vp_ref, vc_ref, s_ref, do_ref, l_ref,
             dq_ref, dkc_ref, dkp_ref, dvc_ref, dvp_ref, ds_ref):
        b, i = pl.program_id(1), pl.program_id(2)

        @pl.when((b == 0) & (i == 0))
        def _():
            ds_ref[...] = jnp.zeros_like(ds_ref)
        valid = _swa_valid(i)
        kk = jnp.concatenate([kp_ref[...], kc_ref[...]], axis=0)
        vv = jnp.concatenate([vp_ref[...], vc_ref[...]], axis=0)
        sls = [slice(jj * LANES, (jj + 1) * LANES) for jj in range(grp)]
        sc = [_swa_scores(q_ref[:, sl], kk, s_ref[jj], valid) for jj, sl in enumerate(sls)]
        dos = [_stack_heads(do_ref[:, sl]) for sl in sls]
        dps = [_dot(d, vv, NT) for d in dos]
        lses = [jnp.concatenate([l_ref[:, sl][:, 0:1], l_ref[:, sl][:, HEAD:HEAD + 1]], axis=0) for sl in sls]
        ps = [jnp.where(valid, jnp.exp(s - lse), 0.0) for (_, s, _), lse in zip(sc, lses)]
        deltas = [jnp.sum(p * dp, axis=1, keepdims=True) for p, dp in zip(ps, dps)]
        dscs = [(p * (dp - delta)).astype(BF16) for p, dp, delta in zip(ps, dps, deltas)]
        dqs = [_dot(dsc, kk, NN) for dsc in dscs]
        dk = jnp.zeros((2 * BLK, LANES), F32)
        dv = jnp.zeros((2 * BLK, LANES), F32)
        for jj, sl in enumerate(sls):
            dsk = -jnp.exp(sc[jj][2] - lses[jj]) * deltas[jj]
            dsk_lo = jnp.sum(dsk[:BLK], axis=0, keepdims=True)
            dsk_hi = jnp.sum(dsk[BLK:], axis=0, keepdims=True)
            ds_ref[jj] += jnp.where(_lo_mask((1, LANES)), dsk_lo, dsk_hi)
            dq_ref[:, sl] = _unstack_heads(dqs[jj]) * SCALE
            dk = dk + _dot(dscs[jj], sc[jj][0], TN)
            dv = dv + _dot(ps[jj].astype(BF16), dos[jj], TN)
        dkp_ref[...] = dk[:BLK]
        dkc_ref[...] = dk[BLK:]
        dvp_ref[...] = dv[:BLK]
        dvc_ref[...] = dv[BLK:]

    spec_q = pl.BlockSpec((BLK, grp * LANES), lambda g, b, i: (b * nq + i, g))
    spec_p = pl.BlockSpec((BLK, LANES), lambda g, b, i: (b * nq + jnp.maximum(i - 1, 0), g))
    spec_c = pl.BlockSpec((BLK, LANES), lambda g, b, i: (b * nq + i, g))
    spec_s = pl.BlockSpec((grp, 1, LANES), lambda g, b, i: (g, 0, 0))
    kv = jax.ShapeDtypeStruct((T, NG * LANES), F32)
    return pl.pallas_call(
        body, name=name, grid=(NG, B, nq),
        in_specs=[spec_q, spec_p, spec_c, spec_p, spec_c, spec_s, spec_q, spec_q],
        out_specs=[spec_q, spec_c, spec_c, spec_c, spec_c, spec_s],
        out_shape=[jax.ShapeDtypeStruct((T, NQ), F32), kv, kv, kv, kv, jax.ShapeDtypeStruct((NP, 1, LANES), F32)],
        compiler_params=_params("arbitrary", "arbitrary", "arbitrary"),
    )(q, kd, kd, vd, vd, sink2, do, lse)


def _qk_prep_bwd(qkv, cs, s1, s2, qg, kg, dq, dkc, dkp, dvc, dvp, B, name):
    T, W = qkv.shape
    NQ = W - 2 * LANES
    NP = NQ // LANES
    nq = T // B // BLK

    def body(x_ref, cs_ref, s1_ref, s2_ref, qg_ref, kg_ref, dq_ref, dkc_ref, dkp_ref, dvc_ref, dvp_ref,
             o_ref, dqg_ref, dkg_ref):
        b, i = pl.program_id(0), pl.program_id(1)

        @pl.when((b == 0) & (i == 0))
        def _():
            dqg_ref[...] = jnp.zeros_like(dqg_ref)
            dkg_ref[...] = jnp.zeros_like(dkg_ref)
        P = _head_sum_matrix()
        cs_, s1_, s2_ = cs_ref[...], s1_ref[...], s2_ref[...]
        lo = _lo_mask((BLK, LANES))
        has_next = (i + 1 < nq).astype(F32)

        def norm_rope_bwd(xv, g, d):
            du = _rope_bwd(d, cs_, s1_, s2_)
            r = lax.rsqrt(_dot(xv * xv, P, NN, HIGH) * (1.0 / HEAD) + EPS)
            xhat = xv * r
            dgain = jnp.sum(du * xhat, axis=0, keepdims=True)
            uu = du * g
            dx = r * (uu - xhat * (_dot(uu * xhat, P, NN, HIGH) * (1.0 / HEAD)))
            return dx, dgain + pltpu.roll(dgain, HEAD, 1)

        dqg = jnp.zeros((1, LANES), F32)
        for j in range(NP):
            sl = slice(j * LANES, (j + 1) * LANES)
            dx, dg = norm_rope_bwd(x_ref[:, sl], qg_ref[...], dq_ref[:, sl])
            o_ref[:, sl] = dx.astype(BF16)
            dqg = dqg + dg
        dqg_ref[...] += dqg

        def fold(c_ref, p_ref, g):
            sl = slice(g * LANES, (g + 1) * LANES)
            t = c_ref[:, sl] + has_next * p_ref[:, sl]
            return t + pltpu.roll(t, HEAD, 1)

        dk = jnp.where(lo, fold(dkc_ref, dkp_ref, 0), fold(dkc_ref, dkp_ref, 1))
        dx, dg = norm_rope_bwd(x_ref[:, NQ:NQ + LANES], kg_ref[...], dk)
        o_ref[:, NQ:NQ + LANES] = dx.astype(BF16)
        dkg_ref[...] += dg
        dv = jnp.where(lo, fold(dvc_ref, dvp_ref, 0), fold(dvc_ref, dvp_ref, 1))
        o_ref[:, NQ + LANES:] = dv.astype(BF16)

    spec_t = pl.BlockSpec((BLK, LANES), lambda b, i: (b * nq + i, 0))
    spec_g = pl.BlockSpec((1, LANES), lambda b, i: (0, 0))
    spec_c = pl.BlockSpec((BLK, 2 * LANES), lambda b, i: (b * nq + i, 0))
    spec_n = pl.BlockSpec((BLK, 2 * LANES), lambda b, i: (b * nq + jnp.minimum(i + 1, nq - 1), 0))
    row = jax.ShapeDtypeStruct((1, LANES), F32)
    return pl.pallas_call(
        body, name=name, grid=(B, nq),
        in_specs=[pl.BlockSpec((BLK, W), lambda b, i: (b * nq + i, 0)), spec_t, spec_t, spec_t, spec_g, spec_g,
                  pl.BlockSpec((BLK, NQ), lambda b, i: (b * nq + i, 0)), spec_c, spec_n, spec_c, spec_n],
        out_specs=[pl.BlockSpec((BLK, W), lambda b, i: (b * nq + i, 0)), spec_g, spec_g],
        out_shape=[jax.ShapeDtypeStruct((T, W), BF16), row, row],
        compiler_params=_params("arbitrary", "arbitrary"),
    )(qkv, cs, s1, s2, qg, kg, dq, dkc, dkp, dvc, dvp)


SB_TILE = 256
SB_UNROLL = 4
SB_UNROLL_BWD = 2


def _split_heads(x2, scale=None):
    lo = _lo_mask(x2.shape)
    z = jnp.zeros_like(x2)
    if scale is not None:
        x2 = x2 * scale
    return jnp.where(lo, x2, z), jnp.where(lo, z, x2)


def _sb_terms(qh, kj, diagonal):
    z = _dot(qh, kj, NT)
    e = jnp.exp(-jnp.abs(z))
    lb = jnp.minimum(z, 0.0) - jnp.log(1.0 + e)
    L = lb - z
    if not diagonal:
        return lb, L, None, z, e
    strict = lax.broadcasted_iota(jnp.int32, z.shape, 1) < lax.broadcasted_iota(jnp.int32, z.shape, 0)
    return lb, jnp.where(strict, L, 0.0), strict, z, e


def _tri(n, cmp):
    r = lax.broadcasted_iota(jnp.int32, (n, n), 0)
    c = lax.broadcasted_iota(jnp.int32, (n, n), 1)
    return cmp(r, c).astype(BF16)


def _whole_grid(first, dims):
    ids = [pl.program_id(a) for a in range(3)]
    return functools.reduce(lambda u, v: u & v, [i == (0 if first else d - 1) for i, d in zip(ids, dims)])


def _sb_fwd(qkv, B, name, gather=None):
    T, W = qkv.shape
    NQ = W // 3
    NP = NQ // LANES
    S = T // B
    tq = min(SB_TILE, S)
    nq = S // tq
    grid = (B, NP, nq)

    def body(q_ref, k_ref, v_ref, *rest):
        if gather is None:
            o_ref, t_ref = rest
        else:
            x_ref, o_ref, t_ref, g_ref, *sems = rest

            @pl.when(_whole_grid(True, grid))
            def _():
                for cp in _direct_copies(x_ref, g_ref, sems, False):
                    cp.start()
        i = pl.program_id(2)
        qh = _split_heads(q_ref[...], SCALE)
        U = _tri(tq, lambda r, c: r > c)

        def sweep(tiles, cs, acc):
            chains = [(t, h) for t in range(len(tiles)) for h in range(2)]
            rows = [pl.ds(pl.multiple_of(j * tq, tq), tq) for j, _ in tiles]
            ks = [k_ref[r, :] for r in rows]
            vs = [_split_heads(v_ref[r, :]) for r in rows]
            terms = {(t, h): _sb_terms(qh[h], ks[t], tiles[t][1]) for t, h in chains}
            carry = {}
            for h in range(2):
                c = cs[h]
                for t in range(len(tiles)):
                    carry[t, h] = c
                    c = c + jnp.sum(terms[t, h][1], axis=1, keepdims=True)
                cs = cs[:h] + (c,) + cs[h + 1:]
            cum = {ch: _dot(terms[ch][1].astype(BF16), U, NN) for ch in chains}
            for ch in chains:
                a = jnp.exp(terms[ch][0] + (cum[ch] + carry[ch]))
                if tiles[ch[0]][1]:
                    a = jnp.where(terms[ch][2], a, 0.0)
                acc = acc + _dot(a.astype(BF16), vs[ch[0]][ch[1]], NN)
            return cs, acc

        zero = jnp.zeros((tq, 1), F32)
        carry = sweep([(i, True)], (zero, zero), jnp.zeros((tq, LANES), F32))

        def run(first, count):
            return lambda cr: sweep([(first - t, False) for t in range(count)], *cr)

        carry = lax.fori_loop(0, i // SB_UNROLL, lambda n, cr: run(i - 1 - SB_UNROLL * n, SB_UNROLL)(cr), carry)
        done = i // SB_UNROLL * SB_UNROLL
        u = SB_UNROLL // 2
        while u:
            carry = lax.cond((i & u) != 0, run(i - 1 - done, u), lambda cr: cr, carry)
            done = done + (i & u)
            u //= 2
        cs, acc = carry
        o_ref[...] = acc.astype(BF16)
        t_ref[...] = jnp.where(_lo_mask((tq, LANES)), cs[0], cs[1])
        if gather is not None:
            @pl.when(_whole_grid(False, grid))
            def _():
                for cp in _direct_copies(x_ref, g_ref, sems, False):
                    cp.wait()

    spec_q = pl.BlockSpec((tq, LANES), lambda b, p, i: (b * nq + i, p))
    in_specs = [spec_q, pl.BlockSpec((S, LANES), lambda b, p, i: (b, NP + p)),
                pl.BlockSpec((S, LANES), lambda b, p, i: (b, 2 * NP + p))]
    out_specs = [spec_q, spec_q]
    out_shape = [jax.ShapeDtypeStruct((T, NQ), BF16), jax.ShapeDtypeStruct((T, NQ), F32)]
    args = [qkv, qkv, qkv]
    if gather is not None:
        in_specs.append(HBM_SPEC)
        out_specs.append(HBM_SPEC)
        out_shape.append(jax.ShapeDtypeStruct((NDEV,) + gather.shape, gather.dtype))
        args.append(gather)
    return pl.pallas_call(
        body, name=name, grid=grid, in_specs=in_specs, out_specs=out_specs, out_shape=out_shape,
        scratch_shapes=[] if gather is None else COMM_SEMS,
        compiler_params=_params("arbitrary", "arbitrary", "arbitrary"),
    )(*args)


def _sb_bwd(qkv, q_t, do, do_t, tot, B, name, exchange=None):
    T, W = qkv.shape
    NQ = W // 3
    NP = NQ // LANES
    S = T // B
    tq = min(SB_TILE, S)
    nq = S // tq
    grid = (B, NP, nq)

    def body(q_ref, k_ref, v_ref, do_ref, qt_ref, dot_ref, t_ref, *rest):
        if exchange is None:
            dq_ref, dk_ref, dv_ref = rest
        else:
            x_ref, dq_ref, dk_ref, dv_ref, r_ref, *sems = rest

            @pl.when(_whole_grid(True, grid))
            def _():
                for cp in _direct_copies(x_ref, r_ref, sems, True):
                    cp.start()
        i = pl.program_id(2)

        @pl.when(i == 0)
        def _():
            dk_ref[...] = jnp.zeros_like(dk_ref)
            dv_ref[...] = jnp.zeros_like(dv_ref)
        qh = _split_heads(q_ref[...], SCALE)
        doh = _split_heads(do_ref[...])
        top = lax.broadcasted_iota(jnp.int32, (LANES, tq), 0) < HEAD
        zt = jnp.zeros((LANES, tq), BF16)
        qt = qt_ref[...] * SCALE
        qth = (jnp.where(top, qt, zt), jnp.where(top, zt, qt))
        doth = (jnp.where(top, dot_ref[...], zt), jnp.where(top, zt, dot_ref[...]))
        tt = t_ref[...]
        tot = (tt[:, 0:1], tt[:, HEAD:HEAD + 1])
        Urev = _tri(tq, lambda r, c: r > c)
        Uexc = _tri(tq, lambda r, c: r < c)

        def sweep(tiles, carry):
            nt = len(tiles)
            chains = [(t, h) for t in range(nt) for h in range(2)]
            rows = [pl.ds(pl.multiple_of(j * tq, tq), tq) for j, _ in tiles]
            ks = [k_ref[r, :] for r in rows]
            vs = [v_ref[r, :] for r in rows]
            terms = {(t, h): _sb_terms(qh[h], ks[t], tiles[t][1]) for t, h in chains}
            da = {(t, h): _dot(doh[h], vs[t], NT) for t, h in chains}
            cc = [carry[h][0] for h in range(2)]
            later = {}
            for t, h in chains:
                cc[h] = cc[h] + jnp.sum(terms[t, h][1], axis=1, keepdims=True)
                later[t, h] = tot[h] - cc[h]
            cum = {ch: _dot(terms[ch][1].astype(BF16), Urev, NN) for ch in chains}
            a, g, before = {}, {}, {}
            cg = [carry[h][1] for h in range(2)]
            for ch in chains:
                a[ch] = jnp.exp(terms[ch][0] + (cum[ch] + later[ch]))
                if tiles[ch[0]][1]:
                    a[ch] = jnp.where(terms[ch][2], a[ch], 0.0)
                g[ch] = a[ch] * da[ch]
                before[ch] = cg[ch[1]]
                cg[ch[1]] = cg[ch[1]] + jnp.sum(g[ch], axis=1, keepdims=True)
            G = {ch: _dot(g[ch].astype(BF16), Uexc, NN) for ch in chains}
            dz = {}
            for ch in chains:
                d = g[ch] - jnp.exp(terms[ch][0]) * (g[ch] + (G[ch] + before[ch]))
                if tiles[ch[0]][1]:
                    d = jnp.where(terms[ch][2], d, 0.0)
                dz[ch] = d.astype(BF16)
            dq = [carry[h][2] for h in range(2)]
            for t, h in chains:
                dq[h] = dq[h] + _dot(dz[t, h], ks[t], NN)
            for t in range(nt):
                dk_ref[:, rows[t]] += _dot(qth[0], dz[t, 0], NN) + _dot(qth[1], dz[t, 1], NN)
                dv_ref[:, rows[t]] += _dot(doth[0], a[t, 0].astype(BF16), NN) + _dot(doth[1], a[t, 1].astype(BF16), NN)
            return tuple((cc[h], cg[h], dq[h]) for h in range(2))

        zero = jnp.zeros((tq, 1), F32)
        zq = jnp.zeros((tq, LANES), F32)
        def run(first, count):
            return lambda cr: sweep([(first + t, False) for t in range(count)], cr)

        carry = lax.fori_loop(0, i // SB_UNROLL_BWD, lambda n, cr: run(SB_UNROLL_BWD * n, SB_UNROLL_BWD)(cr),
                              ((zero, zero, zq), (zero, zero, zq)))
        done = i // SB_UNROLL_BWD * SB_UNROLL_BWD
        u = SB_UNROLL_BWD // 2
        while u:
            carry = lax.cond((i & u) != 0, run(done, u), lambda cr: cr, carry)
            done = done + (i & u)
            u //= 2
        carry = sweep([(i, True)], carry)
        dq_ref[...] = jnp.where(_lo_mask((tq, LANES)), carry[0][2], carry[1][2]) * SCALE
        if exchange is not None:
            @pl.when(_whole_grid(False, grid))
            def _():
                for cp in _direct_copies(x_ref, r_ref, sems, True):
                    cp.wait()

    spec_q = pl.BlockSpec((tq, LANES), lambda b, p, i: (b * nq + i, p))
    spec_t = pl.BlockSpec((LANES, tq), lambda b, p, i: (p, b * nq + i))
    spec_s = pl.BlockSpec((LANES, S), lambda b, p, i: (b * NP + p, 0))
    key_side = jax.ShapeDtypeStruct((B * NQ, S), F32)
    in_specs = [spec_q, pl.BlockSpec((S, LANES), lambda b, p, i: (b, NP + p)),
                pl.BlockSpec((S, LANES), lambda b, p, i: (b, 2 * NP + p)), spec_q, spec_t, spec_t, spec_q]
    out_specs = [spec_q, spec_s, spec_s]
    out_shape = [jax.ShapeDtypeStruct((T, NQ), F32), key_side, key_side]
    args = [qkv, qkv, qkv, do, q_t, do_t, tot]
    if exchange is not None:
        in_specs.append(HBM_SPEC)
        out_specs.append(HBM_SPEC)
        out_shape.append(jax.ShapeDtypeStruct(exchange.shape, exchange.dtype))
        args.append(exchange)
    return pl.pallas_call(
        body, name=name, grid=grid, in_specs=in_specs, out_specs=out_specs, out_shape=out_shape,
        scratch_shapes=[] if exchange is None else COMM_SEMS,
        compiler_params=_params("arbitrary", "arbitrary", "arbitrary"),
    )(*args)


def _adamw(w, g, m, v, name):
    shape = w.shape
    cols = shape[-1]
    rows = math.prod(shape[:-1])
    tr = _pick(rows, max(8, (1 << 19) // max(cols, LANES) // 8 * 8), 8)

    def body(w_ref, g_ref, m_ref, v_ref, d_ref, mo_ref, vo_ref):
        gv = g_ref[...]
        mn = ADAM_B1 * m_ref[...] + (1.0 - ADAM_B1) * gv
        vn = ADAM_B2 * v_ref[...] + (1.0 - ADAM_B2) * (gv * gv)
        m_hat = mn / (1.0 - ADAM_B1 ** ADAM_STEP)
        v_hat = vn / (1.0 - ADAM_B2 ** ADAM_STEP)
        d_ref[...] = -ADAM_LR * (m_hat / (jnp.sqrt(v_hat) + ADAM_EPS) + ADAM_WD * w_ref[...])
        mo_ref[...] = mn
        vo_ref[...] = vn

    spec = pl.BlockSpec((tr, cols), lambda i: (i, 0))
    out = jax.ShapeDtypeStruct((rows, cols), F32)
    d, mn, vn = pl.pallas_call(
        body, name=name, grid=(rows // tr,),
        in_specs=[spec] * 4, out_specs=[spec] * 3, out_shape=[out] * 3,
        compiler_params=_params("parallel"),
    )(w.reshape(rows, cols), g.reshape(rows, cols), m.reshape(rows, cols), v.reshape(rows, cols))
    return d.reshape(shape), mn.reshape(shape), vn.reshape(shape)


def _pad_rows(a, rows):
    return jnp.pad(a, ((0, rows - a.shape[0]), (0, 0)))


def kernel(x, c, positions, ada_w, ada_b, norm1_g, norm2_g, wqkv_a, q_norm_a, k_norm_a, sinks_a, wo_a, wqkv_b, wo_b, w_gate, w_up, w_down, loss_target, m_ada_w, m_ada_b, m_norm1_g, m_norm2_g, m_wqkv_a, m_q_norm_a, m_k_norm_a, m_sinks_a, m_wo_a, m_wqkv_b, m_wo_b, m_w_gate, m_w_up, m_w_down, v_ada_w, v_ada_b, v_norm1_g, v_norm2_g, v_wqkv_a, v_q_norm_a, v_k_norm_a, v_sinks_a, v_wo_a, v_wqkv_b, v_wo_b, v_w_gate, v_w_up, v_w_down):
    B, S, D = x.shape
    T = B * S
    L = ada_w.shape[0]
    NA, NB_ = wqkv_a.shape[0], wqkv_b.shape[0]
    me = 4 * lax.axis_index("x") + 2 * lax.axis_index("y") + lax.axis_index("c")
    xt = x.reshape(T, D)

    col_sharded = {"qkv_a": wqkv_a, "qkv_b": wqkv_b, "gate": w_gate, "up": w_up}
    row_sharded = {"wo_a": wo_a, "wo_b": wo_b, "down": w_down}

    def shard_rows(key):
        kind, idx = key
        return col_sharded[kind][idx].T if kind in col_sharded else row_sharded[kind][idx]

    def layer_keys(l):
        mix = "a" if l % 2 == 0 else "b"
        return [("qkv_" + mix, l // 2), ("wo_" + mix, l // 2), ("gate", l), ("up", l), ("down", l)]

    def unpack(buf, keys, reshape):
        out, off = {}, 0
        for key in keys:
            rows = shard_rows(key).shape[0]
            out[key] = reshape(buf[..., off:off + rows, :], rows)
            off += rows
        return out

    first_b = 1
    keys_early = layer_keys(0) + [("qkv_b", 0)]
    keys_late = [k for l in range(1, L) for k in layer_keys(l) if k != ("qkv_b", 0)]
    pack = lambda keys: jnp.concatenate([shard_rows(k).astype(BF16) for k in keys], axis=0)
    W = unpack(_all_gather(pack(keys_early), "ag_weights"), keys_early, lambda b, rows: b.reshape(NDEV * rows, D))

    WA = ada_w.shape[2]
    c_all = _all_gather(c, "ag_c").reshape(NDEV * B, D)
    bias = lax.dynamic_slice_in_dim(ada_b, me * WA, WA, axis=1).reshape(L, 1, WA)
    mod_part = _ada_fwd(c_all, ada_w, bias, "ada_fwd")
    mod_all = _all_gather(mod_part.reshape(L * NDEV * B, WA), "ag_mod")
    mod_all = mod_all.reshape(NDEV, L, NDEV * B, WA).transpose(1, 2, 0, 3).reshape(L, NDEV * B, NDEV * WA)
    mod = lax.dynamic_slice_in_dim(mod_all, me * B, B, axis=1)
    mod = mod.reshape(L, B, 6, 1, D)
    sh1, sc1, g1, sh2, sc2, g2 = [mod[:, :, k] for k in range(6)]

    half = ROT // 2
    inv_freq = jnp.power(jnp.float32(ROPE_THETA), -jnp.arange(half, dtype=F32) * 2.0 / ROT)
    ang = positions.reshape(T, 1).astype(F32) * inv_freq[None, :]
    cos, sin = jnp.cos(ang), jnp.sin(ang)
    ones = jnp.ones((T, HEAD - ROT), F32)
    zeros = jnp.zeros((T, HEAD - ROT), F32)
    z8 = jnp.zeros((T, half), F32)
    cs = jnp.tile(jnp.concatenate([cos, cos, ones], axis=1), (1, 2))
    s1 = jnp.tile(jnp.concatenate([-sin, z8, zeros], axis=1), (1, 2))
    s2 = jnp.tile(jnp.concatenate([z8, sin, zeros], axis=1), (1, 2))

    saved = []
    xc = xt
    for l in range(L):
        j = l // 2
        h1 = _norm_mod(xc, norm1_g[l:l + 1], sc1[l], sh1[l], S, f"norm1_{l}")
        sv = dict(x_in=xc, h1=h1)
        if l % 2 == 0:
            qkv = _mm_nt(h1, W["qkv_a", j], F32, f"qkv_a_{l}")
            qg = jnp.tile(q_norm_a[j:j + 1], (1, 2))
            kg = jnp.tile(k_norm_a[j:j + 1], (1, 2))
            qn, kd, vd = _qk_prep(qkv, cs, s1, s2, qg, kg, f"qk_prep_{l}")
            sink2 = jnp.repeat(sinks_a[j].reshape(-1, 2), HEAD, axis=1).reshape(-1, 1, LANES)
            attn, lse = _swa_fwd(qn, kd, vd, sink2, B, f"swa_fwd_{l}")
            sv.update(qkv=qkv, qg=qg, kg=kg, qn=qn, kd=kd, vd=vd, sink2=sink2, lse=lse)
            wo = W["wo_a", j]
        else:
            qkv = _mm_nt(h1, W["qkv_b", j], BF16, f"qkv_b_{l}")
            if l == first_b:
                attn, tot, late = _sb_fwd(qkv, B, f"sb_fwd_{l}", gather=pack(keys_late))
                W.update(unpack(late, keys_late, lambda b, rows: b.reshape(NDEV * rows, D)))
            else:
                attn, tot = _sb_fwd(qkv, B, f"sb_fwd_{l}")
            sv.update(qkv=qkv, tot=tot)
            wo = W["wo_b", j]
        y1, xm = _mm_res(attn, wo, xc, g1[l], S, f"attn_out_{l}")
        h2 = _norm_mod(xm, norm2_g[l:l + 1], sc2[l], sh2[l], S, f"norm2_{l}")
        gate, up, act = _swiglu_fwd(h2, W["gate", l], W["up", l], f"swiglu_fwd_{l}")
        y2, xc = _mm_res(act, W["down", l], xm, g2[l], S, f"mlp_out_{l}")
        sv.update(attn=attn, y1=y1, x_mid=xm, h2=h2, gate=gate, up=up, act=act, y2=y2)
        saved.append(sv)

    dx, loss_tile = _loss_head(xc, loss_target.reshape(T, D), "loss_head")

    G = {}
    pack_grads = lambda keys: jnp.concatenate([G[k].reshape(NDEV, G[k].shape[0] // NDEV, D).astype(BF16) for k in keys], axis=1)
    keys_hi = [k for l in range(first_b + 1, L) for k in layer_keys(l)]
    keys_lo = [k for l in range(first_b + 1) for k in layer_keys(l)]
    received_hi = None
    dmod = [None] * L
    dn1, dn2 = [None] * L, [None] * L
    dqg, dkg, dsink = [None] * NA, [None] * NA, [None] * NA
    for l in reversed(range(L)):
        j = l // 2
        mix = "a" if l % 2 == 0 else "b"
        sv = saved[l]
        dy2, dg2 = _gate_bwd(dx, sv["y2"], g2[l], S, f"gate2_bwd_{l}")
        dgate, dup = _swiglu_bwd(dy2, W["down", l], sv["gate"], sv["up"], f"swiglu_bwd_{l}")
        G["down", l] = _mm_tn(sv["act"], dy2, f"dw_down_{l}")
        dh2 = _mm_nn([(dgate, W["gate", l]), (dup, W["up", l])], f"dh2_{l}")
        G["gate", l] = _mm_tn(dgate, sv["h2"], f"dw_gate_{l}")
        G["up", l] = _mm_tn(dup, sv["h2"], f"dw_up_{l}")
        dxm, dsh2, dsc2, dn2[l] = _norm_mod_bwd(sv["x_mid"], dh2, dx, norm2_g[l:l + 1], sc2[l], S, f"norm2_bwd_{l}")
        dy1, dg1 = _gate_bwd(dxm, sv["y1"], g1[l], S, f"gate1_bwd_{l}")
        dattn = _mm_nt(dy1, W["wo_" + mix, j], BF16, f"dattn_{l}")
        G["wo_" + mix, j] = _mm_tn(sv["attn"], dy1, f"dw_o_{l}")
        if l % 2 == 0:
            dq, dkc, dkp, dvc, dvp, dsink[j] = _swa_bwd(sv["qn"], sv["kd"], sv["vd"], sv["sink2"], dattn, sv["lse"], B,
                                                        f"swa_bwd_{l}")
            dqkv, dqg[j], dkg[j] = _qk_prep_bwd(sv["qkv"], cs, s1, s2, sv["qg"], sv["kg"], dq, dkc, dkp, dvc, dvp, B,
                                                f"qk_prep_bwd_{l}")
        else:
            nqb = sv["qkv"].shape[1] // 3
            sb_args = (sv["qkv"], sv["qkv"][:, :nqb].T, dattn, dattn.T, sv["tot"], B, f"sb_bwd_{l}")
            if l == first_b and keys_hi:
                dq, dk_t, dv_t, received_hi = _sb_bwd(*sb_args, exchange=pack_grads(keys_hi))
            else:
                dq, dk_t, dv_t = _sb_bwd(*sb_args)
            dk, dv = [t.reshape(B, nqb, S).transpose(0, 2, 1).reshape(T, nqb) for t in (dk_t, dv_t)]
            dqkv = jnp.concatenate([dq, dk, dv], axis=1).astype(BF16)
        dh1 = _mm_nn([(dqkv, W["qkv_" + mix, j])], f"dh1_{l}")
        G["qkv_" + mix, j] = _mm_tn(dqkv, sv["h1"], f"dw_qkv_{l}")
        dx, dsh1, dsc1, dn1[l] = _norm_mod_bwd(sv["x_in"], dh1, dxm, norm1_g[l:l + 1], sc1[l], S, f"norm1_bwd_{l}")
        dmod[l] = jnp.concatenate([dsh1, dsc1, dg1, dsh2, dsc2, dg2], axis=1)
    grad_x = dx.reshape(B, S, D)

    ndm = L * 6
    dmod_rows = jnp.stack(dmod, axis=1).reshape(B * ndm, D)
    misc = jnp.concatenate(
        [jnp.concatenate(dn1, axis=0).reshape(B * L, D), jnp.concatenate(dn2, axis=0).reshape(B * L, D),
         _pad_rows(jnp.concatenate([jnp.pad(r, ((0, 0), (0, D - LANES))) for r in dqg + dkg]
                                   + [jnp.pad(r[:, 0, ::HEAD].reshape(1, -1), ((0, 0), (0, D - 2 * r.shape[0]))) for r in dsink]
                                   + [jnp.pad(loss_tile[0:1, 0:1], ((0, 0), (0, D - 1)))], axis=0), 8)], axis=0)
    nmisc = misc.shape[0]
    small = _all_gather(jnp.concatenate([dmod_rows, _pad_rows(misc, -(-nmisc // 8) * 8)], axis=0), "ag_small")
    dmod_all = small[:, :B * ndm].reshape(NDEV * B, ndm, D)
    g_ada_b = _sum_leading(dmod_all, "sum_dmod").reshape(L, 6 * D)
    misc_sum = _sum_leading(small[:, B * ndm:], "sum_misc")
    g_n1 = misc_sum[0:B * L].reshape(L, B, D)
    g_n2 = misc_sum[B * L:2 * B * L].reshape(L, B, D)
    g_norm1 = _sum_leading(g_n1.transpose(1, 0, 2), "sum_n1")
    g_norm2 = _sum_leading(g_n2.transpose(1, 0, 2), "sum_n2")
    o = 2 * B * L
    g_qn = misc_sum[o:o + NA, :HEAD]
    g_kn = misc_sum[o + NA:o + 2 * NA, :HEAD]
    nsink = sinks_a.shape[1]
    g_sink = misc_sum[o + 2 * NA:o + 3 * NA, :nsink]
    loss = misc_sum[o + 3 * NA, 0]

    dmod_loc = lax.dynamic_slice_in_dim(dmod_all.reshape(NDEV * B, L, 6 * D), me * WA, WA, axis=2)
    g_ada_w = _ada_bwd(c_all, dmod_loc.transpose(1, 0, 2), "ada_bwd")

    shard = unpack(_sum_leading(_exchange(pack_grads(keys_lo), "grad_exchange"), "grad_sum"), keys_lo, lambda b, rows: b)
    if received_hi is not None:
        shard.update(unpack(_sum_leading(received_hi, "grad_sum_hi"), keys_hi, lambda b, rows: b))

    def stacked(kind, n):
        return jnp.stack([shard[kind, i].T if kind in col_sharded else shard[kind, i] for i in range(n)])

    gw_qkv_a, gw_qkv_b, gw_gate, gw_up = stacked("qkv_a", NA), stacked("qkv_b", NB_), stacked("gate", L), stacked("up", L)
    gw_wo_a, gw_wo_b, gw_down = stacked("wo_a", NA), stacked("wo_b", NB_), stacked("down", L)

    grads = [g_ada_w, g_ada_b, g_norm1, g_norm2, gw_qkv_a, g_qn, g_kn, g_sink, gw_wo_a, gw_qkv_b, gw_wo_b,
             gw_gate, gw_up, gw_down]
    ws = [ada_w, ada_b, norm1_g, norm2_g, wqkv_a, q_norm_a, k_norm_a, sinks_a, wo_a, wqkv_b, wo_b, w_gate, w_up, w_down]
    ms = [m_ada_w, m_ada_b, m_norm1_g, m_norm2_g, m_wqkv_a, m_q_norm_a, m_k_norm_a, m_sinks_a, m_wo_a, m_wqkv_b,
          m_wo_b, m_w_gate, m_w_up, m_w_down]
    vs = [v_ada_w, v_ada_b, v_norm1_g, v_norm2_g, v_wqkv_a, v_q_norm_a, v_k_norm_a, v_sinks_a, v_wo_a, v_wqkv_b,
          v_wo_b, v_w_gate, v_w_up, v_w_down]
    deltas, new_m, new_v = [], [], []
    for k, (w, g, m, v) in enumerate(zip(ws, grads, ms, vs)):
        g = g.reshape(w.shape)
        d, mn, vn = _adamw(w, g, m, v, f"adamw_{k}")
        grads[k] = g
        deltas.append(d)
        new_m.append(mn)
        new_v.append(vn)
    return (loss, grad_x, *grads, *deltas, *new_m, *new_v)
```

```python
import functools
import math

import jax
import jax.numpy as jnp
from jax import lax
from jax.experimental import pallas as pl
from jax.experimental.pallas import tpu as pltpu

F32 = jnp.float32
BF16 = jnp.bfloat16
NDEV = 8
HEAD = 64
BLK = 128
LANES = 128
EPS = 1e-6
ROT = HEAD // 4
ROPE_THETA = 500000.0
SCALE = HEAD ** -0.5
NEG = -1e30
VMEM_LIMIT = 56 * 1024 * 1024
MESH = pl.DeviceIdType.MESH
HIGH = lax.Precision.HIGHEST

ADAM_LR = 0.001
ADAM_B1 = 0.9
ADAM_B2 = 0.999
ADAM_EPS = 1e-08
ADAM_WD = 0.01
ADAM_STEP = 10


def _params(*sem):
    return pltpu.CompilerParams(dimension_semantics=sem, vmem_limit_bytes=VMEM_LIMIT)


def _pick(n, cap, mult):
    if n <= cap:
        return n
    best = None
    for t in range(mult, cap + 1, mult):
        if n % t == 0:
            best = t
    assert best is not None, (n, cap, mult)
    return best


def _dot(a, b, dims, precision=None):
    return lax.dot_general(a, b, (dims, ((), ())), preferred_element_type=F32, precision=precision)


NN = ((1,), (0,))
NT = ((1,), (1,))
TN = ((0,), (0,))


def _all_gather(x, name):
    m, n = x.shape

    def body(x_ref, out_ref, send_sems, recv_sems, local_sem):
        ix, iy, ic = lax.axis_index("x"), lax.axis_index("y"), lax.axis_index("c")
        me, sibling = (ix, iy, ic), (ix, iy, 1 - ic)
        chips = [(1 - ix, iy), (ix, 1 - iy), (1 - ix, 1 - iy)]

        def slab(px, py, pc):
            return out_ref.at[4 * px + 2 * py + pc]

        def copy(k, block, to, src=None):
            return pltpu.make_async_remote_copy(
                src_ref=slab(*block) if src is None else src, dst_ref=slab(*block),
                send_sem=send_sems.at[k], recv_sem=recv_sems.at[k], device_id=to, device_id_type=MESH)

        mine = pltpu.make_async_copy(x_ref, slab(*me), local_sem)
        mine.start()
        first = [copy(0, me, sibling, src=x_ref)]
        first += [copy(1 + j, me, (*chip, ic), src=x_ref) for j, chip in enumerate(chips)]
        for cp in first:
            cp.start()
        passed = [copy(4 + j, (*chip, ic), sibling) for j, chip in enumerate(chips)]
        for j, chip in enumerate(chips):
            copy(1 + j, (*chip, ic), me).wait_recv()
            passed[j].start()
        copy(0, sibling, me).wait_recv()
        for j, chip in enumerate(chips):
            copy(4 + j, (*chip, 1 - ic), me).wait_recv()
        for cp in first + passed:
            cp.wait_send()
        mine.wait()

    return pl.pallas_call(
        body, name=name,
        out_shape=jax.ShapeDtypeStruct((NDEV, m, n), x.dtype),
        in_specs=[pl.BlockSpec(memory_space=pl.ANY)],
        out_specs=pl.BlockSpec(memory_space=pl.ANY),
        scratch_shapes=[pltpu.SemaphoreType.DMA((7,)), pltpu.SemaphoreType.DMA((7,)), pltpu.SemaphoreType.DMA(())],
    )(x)


COMM_SEMS = [pltpu.SemaphoreType.DMA((NDEV - 1,)), pltpu.SemaphoreType.DMA((NDEV - 1,)), pltpu.SemaphoreType.DMA(())]
HBM_SPEC = pl.BlockSpec(memory_space=pl.ANY)


def _direct_copies(src_ref, dst_ref, sems, scatter):
    send_sems, recv_sems, own_sem = sems
    ix, iy, ic = lax.axis_index("x"), lax.axis_index("y"), lax.axis_index("c")
    me = 4 * ix + 2 * iy + ic
    copies = [pltpu.make_async_copy(src_ref.at[me] if scatter else src_ref, dst_ref.at[me], own_sem)]
    for k in range(1, NDEV):
        px = 1 - ix if k & 4 else ix
        py = 1 - iy if k & 2 else iy
        pc = 1 - ic if k & 1 else ic
        copies.append(pltpu.make_async_remote_copy(
            src_ref=src_ref.at[4 * px + 2 * py + pc] if scatter else src_ref, dst_ref=dst_ref.at[me],
            send_sem=send_sems.at[k - 1], recv_sem=recv_sems.at[k - 1],
            device_id=(px, py, pc), device_id_type=MESH))
    return copies


def _exchange(p, name):
    def body(p_ref, r_ref, *sems):
        copies = _direct_copies(p_ref, r_ref, sems, True)
        for cp in copies:
            cp.start()
        for cp in copies:
            cp.wait()

    return pl.pallas_call(
        body, name=name,
        out_shape=jax.ShapeDtypeStruct(p.shape, p.dtype),
        in_specs=[HBM_SPEC], out_specs=HBM_SPEC, scratch_shapes=COMM_SEMS,
    )(p)


def _sum_leading(r, name):
    k, m, n = r.shape
    mult = 8 * (4 // r.dtype.itemsize)
    tm = _pick(m, max(mult, (4 * 1024 * 1024) // (k * n * r.dtype.itemsize) // mult * mult), mult)

    def body(r_ref, o_ref):
        acc = r_ref[0].astype(F32)
        for s in range(1, k):
            acc = acc + r_ref[s].astype(F32)
        o_ref[...] = acc

    return pl.pallas_call(
        body, name=name, grid=(m // tm,),
        in_specs=[pl.BlockSpec((k, tm, n), lambda i: (0, i, 0))],
        out_specs=pl.BlockSpec((tm, n), lambda i: (i, 0)),
        out_shape=jax.ShapeDtypeStruct((m, n), F32),
        compiler_params=_params("parallel"),
    )(r)


def _mm_nt(a, bt, out_dtype, name):
    M, K = a.shape
    N = bt.shape[0]
    tm, tn = _pick(M, 512, 8), _pick(N, 1536, LANES)

    def body(a_ref, b_ref, o_ref):
        o_ref[...] = _dot(a_ref[...], b_ref[...], NT).astype(out_dtype)

    return pl.pallas_call(
        body, name=name, grid=(N // tn, M // tm),
        in_specs=[pl.BlockSpec((tm, K), lambda j, i: (i, 0)), pl.BlockSpec((tn, K), lambda j, i: (j, 0))],
        out_specs=pl.BlockSpec((tm, tn), lambda j, i: (i, j)),
        out_shape=jax.ShapeDtypeStruct((M, N), out_dtype),
        compiler_params=_params("parallel", "parallel"),
    )(a, bt)


def _mm_nn(pairs, name):
    M = pairs[0][0].shape[0]
    N = pairs[0][1].shape[1]
    tm, tn = _pick(M, 512, 8), _pick(N, 1024, LANES)
    np_ = len(pairs)

    def body(*refs):
        o_ref = refs[-1]
        acc = _dot(refs[0][...], refs[1][...], NN)
        for p in range(1, np_):
            acc = acc + _dot(refs[2 * p][...], refs[2 * p + 1][...], NN)
        o_ref[...] = acc

    in_specs, args = [], []
    for a, b in pairs:
        K = a.shape[1]
        in_specs += [pl.BlockSpec((tm, K), lambda i, j: (i, 0)), pl.BlockSpec((K, tn), lambda i, j: (0, j))]
        args += [a, b]
    return pl.pallas_call(
        body, name=name, grid=(M // tm, N // tn),
        in_specs=in_specs,
        out_specs=pl.BlockSpec((tm, tn), lambda i, j: (i, j)),
        out_shape=jax.ShapeDtypeStruct((M, N), F32),
        compiler_params=_params("parallel", "parallel"),
    )(*args)


def _mm_tn(a, b, name):
    M, N1 = a.shape
    N2 = b.shape[1]
    t1, tk = _pick(N1, 1536, LANES), _pick(M, 512, 8)

    def body(a_ref, b_ref, o_ref):
        @pl.when(pl.program_id(1) == 0)
        def _():
            o_ref[...] = jnp.zeros_like(o_ref)
        o_ref[...] += _dot(a_ref[...], b_ref[...], TN)

    return pl.pallas_call(
        body, name=name, grid=(N1 // t1, M // tk),
        in_specs=[pl.BlockSpec((tk, t1), lambda i, k: (k, i)), pl.BlockSpec((tk, N2), lambda i, k: (k, 0))],
        out_specs=pl.BlockSpec((t1, N2), lambda i, k: (i, 0)),
        out_shape=jax.ShapeDtypeStruct((N1, N2), F32),
        compiler_params=_params("parallel", "arbitrary"),
    )(a, b)


def _mm_res(a, w, x, gate, S, name):
    T, K = a.shape
    D = w.shape[1]
    tm, tn = _pick(S, 512, 8), _pick(D, 1024, LANES)
    nb = S // tm

    def body(a_ref, w_ref, x_ref, g_ref, y_ref, o_ref):
        y = _dot(a_ref[...], w_ref[...], NN)
        y_ref[...] = y.astype(BF16)
        o_ref[...] = x_ref[...] + g_ref[0] * y

    return pl.pallas_call(
        body, name=name, grid=(T // tm, D // tn),
        in_specs=[pl.BlockSpec((tm, K), lambda i, j: (i, 0)), pl.BlockSpec((K, tn), lambda i, j: (0, j)),
                  pl.BlockSpec((tm, tn), lambda i, j: (i, j)), pl.BlockSpec((1, 1, tn), lambda i, j: (i // nb, 0, j))],
        out_specs=[pl.BlockSpec((tm, tn), lambda i, j: (i, j)), pl.BlockSpec((tm, tn), lambda i, j: (i, j))],
        out_shape=[jax.ShapeDtypeStruct((T, D), BF16), jax.ShapeDtypeStruct((T, D), F32)],
        compiler_params=_params("parallel", "parallel"),
    )(a, w, x, gate)


def _swiglu_fwd(h, wgt, wut, name):
    T, D = h.shape
    F = wgt.shape[0]
    tm, tn = _pick(T, 512, 8), _pick(F, 1536, LANES)

    def body(h_ref, g_ref, u_ref, go_ref, uo_ref, a_ref):
        hh = h_ref[...]
        g = _dot(hh, g_ref[...], NT)
        u = _dot(hh, u_ref[...], NT)
        go_ref[...] = g.astype(BF16)
        uo_ref[...] = u.astype(BF16)
        a_ref[...] = (g * jax.nn.sigmoid(g) * u).astype(BF16)

    spec_w = pl.BlockSpec((tn, D), lambda j, i: (j, 0))
    spec_o = pl.BlockSpec((tm, tn), lambda j, i: (i, j))
    out = jax.ShapeDtypeStruct((T, F), BF16)
    return pl.pallas_call(
        body, name=name, grid=(F // tn, T // tm),
        in_specs=[pl.BlockSpec((tm, D), lambda j, i: (i, 0)), spec_w, spec_w],
        out_specs=[spec_o, spec_o, spec_o],
        out_shape=[out, out, out],
        compiler_params=_params("parallel", "parallel"),
    )(h, wgt, wut)


def _swiglu_bwd(dy, wd, gate, up, name):
    T, D = dy.shape
    F = wd.shape[0]
    tm, tn = _pick(T, 512, 8), _pick(F, 1536, LANES)

    def body(dy_ref, w_ref, g_ref, u_ref, dg_ref, du_ref):
        da = _dot(dy_ref[...], w_ref[...], NT)
        g = g_ref[...].astype(F32)
        sg = jax.nn.sigmoid(g)
        silu = g * sg
        du_ref[...] = (da * silu).astype(BF16)
        dg_ref[...] = (da * u_ref[...].astype(F32) * (sg + silu * (1.0 - sg))).astype(BF16)

    spec_o = pl.BlockSpec((tm, tn), lambda j, i: (i, j))
    return pl.pallas_call(
        body, name=name, grid=(F // tn, T // tm),
        in_specs=[pl.BlockSpec((tm, D), lambda j, i: (i, 0)), pl.BlockSpec((tn, D), lambda j, i: (j, 0)), spec_o, spec_o],
        out_specs=[spec_o, spec_o],
        out_shape=[jax.ShapeDtypeStruct((T, F), BF16), jax.ShapeDtypeStruct((T, F), BF16)],
        compiler_params=_params("parallel", "parallel"),
    )(dy, wd, gate, up)


def _norm_mod(x, gain, sc, sh, S, name):
    T, D = x.shape
    tm = _pick(S, 512, 8)
    nb = S // tm

    def body(x_ref, g_ref, sc_ref, sh_ref, o_ref):
        xv = x_ref[...]
        r = lax.rsqrt(jnp.mean(xv * xv, axis=-1, keepdims=True) + EPS)
        o_ref[...] = ((xv * r) * g_ref[...] * (1.0 + sc_ref[0]) + sh_ref[0]).astype(BF16)

    spec_b = pl.BlockSpec((1, 1, D), lambda i: (i // nb, 0, 0))
    return pl.pallas_call(
        body, name=name, grid=(T // tm,),
        in_specs=[pl.BlockSpec((tm, D), lambda i: (i, 0)), pl.BlockSpec((1, D), lambda i: (0, 0)), spec_b, spec_b],
        out_specs=pl.BlockSpec((tm, D), lambda i: (i, 0)),
        out_shape=jax.ShapeDtypeStruct((T, D), BF16),
        compiler_params=_params("parallel"),
    )(x, gain, sc, sh)


def _norm_mod_bwd(x, dh, dres, gain, sc, S, name):
    T, D = x.shape
    B = T // S
    tm = _pick(S, 256, 8)
    nb = S // tm

    def body(x_ref, dh_ref, dr_ref, g_ref, sc_ref, o_ref, dsh_ref, dsc_ref, dg_ref):
        @pl.when(pl.program_id(1) == 0)
        def _():
            dsh_ref[...] = jnp.zeros_like(dsh_ref)
            dsc_ref[...] = jnp.zeros_like(dsc_ref)
            dg_ref[...] = jnp.zeros_like(dg_ref)
        xv, dhv, g = x_ref[...], dh_ref[...], g_ref[...]
        r = lax.rsqrt(jnp.mean(xv * xv, axis=-1, keepdims=True) + EPS)
        xhat = xv * r
        dsh_ref[0] += jnp.sum(dhv, axis=0, keepdims=True)
        dsc_ref[0] += jnp.sum(dhv * (xhat * g), axis=0, keepdims=True)
        dn = dhv * (1.0 + sc_ref[0])
        dg_ref[0] += jnp.sum(dn * xhat, axis=0, keepdims=True)
        dxh = dn * g
        o_ref[...] = dr_ref[...] + r * (dxh - xhat * jnp.mean(dxh * xhat, axis=-1, keepdims=True))

    spec_t = pl.BlockSpec((tm, D), lambda b, i: (b * nb + i, 0))
    spec_b = pl.BlockSpec((1, 1, D), lambda b, i: (b, 0, 0))
    red = jax.ShapeDtypeStruct((B, 1, D), F32)
    return pl.pallas_call(
        body, name=name, grid=(B, nb),
        in_specs=[spec_t, spec_t, spec_t, pl.BlockSpec((1, D), lambda b, i: (0, 0)), spec_b],
        out_specs=[spec_t, spec_b, spec_b, spec_b],
        out_shape=[jax.ShapeDtypeStruct((T, D), F32), red, red, red],
        compiler_params=_params("parallel", "arbitrary"),
    )(x, dh, dres, gain, sc)


def _gate_bwd(dx, y, gate, S, name):
    T, D = dx.shape
    B = T // S
    tm = _pick(S, 512, 8)
    nb = S // tm

    def body(dx_ref, y_ref, g_ref, dy_ref, dg_ref):
        @pl.when(pl.program_id(1) == 0)
        def _():
            dg_ref[...] = jnp.zeros_like(dg_ref)
        d = dx_ref[...]
        dy_ref[...] = (d * g_ref[0]).astype(BF16)
        dg_ref[0] += jnp.sum(d * y_ref[...].astype(F32), axis=0, keepdims=True)

    spec_t = pl.BlockSpec((tm, D), lambda b, i: (b * nb + i, 0))
    spec_b = pl.BlockSpec((1, 1, D), lambda b, i: (b, 0, 0))
    return pl.pallas_call(
        body, name=name, grid=(B, nb),
        in_specs=[spec_t, spec_t, spec_b],
        out_specs=[spec_t, spec_b],
        out_shape=[jax.ShapeDtypeStruct((T, D), BF16), jax.ShapeDtypeStruct((B, 1, D), F32)],
        compiler_params=_params("parallel", "arbitrary"),
    )(dx, y, gate)


def _loss_head(y, target, name):
    T, D = y.shape
    tm = _pick(T, 512, 8)

    def body(y_ref, t_ref, dy_ref, l_ref):
        @pl.when(pl.program_id(0) == 0)
        def _():
            l_ref[...] = jnp.zeros_like(l_ref)
        e = y_ref[...] - t_ref[...]
        dy_ref[...] = e * (1.0 / D)
        l_ref[...] += 0.5 * jnp.sum(jnp.mean(e * e, axis=-1, keepdims=True), axis=0, keepdims=True)

    spec = pl.BlockSpec((tm, D), lambda i: (i, 0))
    return pl.pallas_call(
        body, name=name, grid=(T // tm,),
        in_specs=[spec, spec],
        out_specs=[spec, pl.BlockSpec((8, LANES), lambda i: (0, 0))],
        out_shape=[jax.ShapeDtypeStruct((T, D), F32), jax.ShapeDtypeStruct((8, LANES), F32)],
        compiler_params=_params("arbitrary"),
    )(y, target)


def _ada_fwd(c_all, ada_w, bias, name):
    NB, D = c_all.shape
    L, _, W = ada_w.shape

    def body(c_ref, w_ref, b_ref, o_ref):
        cv = c_ref[...]
        cond = cv * jax.nn.sigmoid(cv)
        o_ref[0] = _dot(cond, w_ref[0], NN, HIGH) + b_ref[0]

    return pl.pallas_call(
        body, name=name, grid=(L,),
        in_specs=[pl.BlockSpec((NB, D), lambda l: (0, 0)), pl.BlockSpec((1, D, W), lambda l: (l, 0, 0)),
                  pl.BlockSpec((1, 1, W), lambda l: (l, 0, 0))],
        out_specs=pl.BlockSpec((1, NB, W), lambda l: (l, 0, 0)),
        out_shape=jax.ShapeDtypeStruct((L, NB, W), F32),
        compiler_params=_params("parallel"),
    )(c_all, ada_w, bias)


def _ada_bwd(c_all, dmod, name):
    NB, D = c_all.shape
    L, _, W = dmod.shape

    def body(c_ref, d_ref, o_ref):
        cv = c_ref[...]
        cond = cv * jax.nn.sigmoid(cv)
        o_ref[0] = _dot(cond, d_ref[0], TN, HIGH)

    return pl.pallas_call(
        body, name=name, grid=(L,),
        in_specs=[pl.BlockSpec((NB, D), lambda l: (0, 0)), pl.BlockSpec((1, NB, W), lambda l: (l, 0, 0))],
        out_specs=pl.BlockSpec((1, D, W), lambda l: (l, 0, 0)),
        out_shape=jax.ShapeDtypeStruct((L, D, W), F32),
        compiler_params=_params("parallel"),
    )(c_all, dmod)


def _lo_mask(shape):
    return lax.broadcasted_iota(jnp.int32, shape, len(shape) - 1) < HEAD


def _head_sum_matrix():
    r = lax.broadcasted_iota(jnp.int32, (LANES, LANES), 0) // HEAD
    c = lax.broadcasted_iota(jnp.int32, (LANES, LANES), 1) // HEAD
    return (r == c).astype(F32)


def _rope(y, cs, s1, s2):
    return y * cs + pltpu.roll(y, LANES - ROT // 2, 1) * s1 + pltpu.roll(y, ROT // 2, 1) * s2


def _rope_bwd(d, cs, s1, s2):
    return d * cs + pltpu.roll(d * s1, ROT // 2, 1) + pltpu.roll(d * s2, LANES - ROT // 2, 1)


def _qk_prep(qkv, cs, s1, s2, qg, kg, name):
    T, W = qkv.shape
    NQ = W - 2 * LANES
    tm = _pick(T, 512, 8)

    def body(x_ref, cs_ref, s1_ref, s2_ref, qg_ref, kg_ref, q_ref, k_ref, v_ref):
        P = _head_sum_matrix()
        cs_, s1_, s2_ = cs_ref[...], s1_ref[...], s2_ref[...]
        lo = _lo_mask((tm, LANES))

        def norm_rope(xv, g):
            ms = _dot(xv * xv, P, NN, HIGH) * (1.0 / HEAD)
            return _rope(xv * lax.rsqrt(ms + EPS) * g, cs_, s1_, s2_)

        for j in range(NQ // LANES):
            q_ref[:, j * LANES:(j + 1) * LANES] = norm_rope(x_ref[:, j * LANES:(j + 1) * LANES], qg_ref[...]).astype(BF16)
        kr = norm_rope(x_ref[:, NQ:NQ + LANES], kg_ref[...])
        ks = pltpu.roll(kr, HEAD, 1)
        k_ref[:, :LANES] = jnp.where(lo, kr, ks).astype(BF16)
        k_ref[:, LANES:] = jnp.where(lo, ks, kr).astype(BF16)
        vr = x_ref[:, NQ + LANES:]
        vs = pltpu.roll(vr, HEAD, 1)
        v_ref[:, :LANES] = jnp.where(lo, vr, vs).astype(BF16)
        v_ref[:, LANES:] = jnp.where(lo, vs, vr).astype(BF16)

    spec_t = pl.BlockSpec((tm, LANES), lambda i: (i, 0))
    spec_g = pl.BlockSpec((1, LANES), lambda i: (0, 0))
    return pl.pallas_call(
        body, name=name, grid=(T // tm,),
        in_specs=[pl.BlockSpec((tm, W), lambda i: (i, 0)), spec_t, spec_t, spec_t, spec_g, spec_g],
        out_specs=[pl.BlockSpec((tm, NQ), lambda i: (i, 0)), pl.BlockSpec((tm, 2 * LANES), lambda i: (i, 0)),
                   pl.BlockSpec((tm, 2 * LANES), lambda i: (i, 0))],
        out_shape=[jax.ShapeDtypeStruct((T, NQ), BF16), jax.ShapeDtypeStruct((T, 2 * LANES), BF16),
                   jax.ShapeDtypeStruct((T, 2 * LANES), BF16)],
        compiler_params=_params("parallel"),
    )(qkv, cs, s1, s2, qg, kg)


def _stack_heads(x2):
    lo = _lo_mask(x2.shape)
    z = jnp.zeros_like(x2)
    return jnp.concatenate([jnp.where(lo, x2, z), jnp.where(lo, z, x2)], axis=0)


def _unstack_heads(xs):
    r = xs.shape[0] // 2
    return jnp.where(_lo_mask((r, LANES)), xs[:r], xs[r:])


def _swa_valid(i):
    qo = lax.broadcasted_iota(jnp.int32, (2 * BLK, 2 * BLK), 0) % BLK
    kc_ = lax.broadcasted_iota(jnp.int32, (2 * BLK, 2 * BLK), 1)
    rel = qo + BLK - kc_
    return (rel >= 0) & (rel < BLK) & ((kc_ >= BLK) | (i > 0))


def _swa_scores(q2, kk, sink2, valid):
    qs = _stack_heads(q2) * SCALE
    s = _dot(qs, kk, NT)
    sk = jnp.concatenate([jnp.broadcast_to(sink2[:, 0:1], (BLK, 1)), jnp.broadcast_to(sink2[:, HEAD:HEAD + 1], (BLK, 1))], axis=0)
    return qs, jnp.where(valid, s, NEG), sk


def _swa_fwd(q, kd, vd, sink2, B, name):
    T, NQ = q.shape
    NP = NQ // LANES
    nq = T // B // BLK
    NG = kd.shape[1] // LANES
    grp = NP // NG

    def body(q_ref, kp_ref, kc_ref, vp_ref, vc_ref, s_ref, o_ref, l_ref):
        valid = _swa_valid(pl.program_id(2))
        kk = jnp.concatenate([kp_ref[...], kc_ref[...]], axis=0)
        vs = _stack_heads(jnp.concatenate([vp_ref[...], vc_ref[...]], axis=0))
        sls = [slice(jj * LANES, (jj + 1) * LANES) for jj in range(grp)]
        sc = [_swa_scores(q_ref[:, sl], kk, s_ref[jj], valid) for jj, sl in enumerate(sls)]
        ms = [jnp.maximum(jnp.max(s, axis=1, keepdims=True), sk) for _, s, sk in sc]
        ps = [jnp.where(valid, jnp.exp(s - m), 0.0) for (_, s, _), m in zip(sc, ms)]
        ls = [jnp.sum(p, axis=1, keepdims=True) + jnp.exp(sk - m) for p, (_, _, sk), m in zip(ps, sc, ms)]
        ps = [(p * (1.0 / l)).astype(BF16) for p, l in zip(ps, ls)]
        os_ = [_dot(jnp.concatenate([p[:BLK], p[BLK:]], axis=1), vs, NN) for p in ps]
        for sl, o, m, l in zip(sls, os_, ms, ls):
            o_ref[:, sl] = o.astype(BF16)
            l_ref[:, sl] = _unstack_heads(jnp.broadcast_to(m + jnp.log(l), (2 * BLK, LANES)))

    spec_q = pl.BlockSpec((BLK, grp * LANES), lambda b, g, i: (b * nq + i, g))
    spec_p = pl.BlockSpec((BLK, LANES), lambda b, g, i: (b * nq + jnp.maximum(i - 1, 0), g))
    spec_c = pl.BlockSpec((BLK, LANES), lambda b, g, i: (b * nq + i, g))
    return pl.pallas_call(
        body, name=name, grid=(B, NG, nq),
        in_specs=[spec_q, spec_p, spec_c, spec_p, spec_c, pl.BlockSpec((grp, 1, LANES), lambda b, g, i: (g, 0, 0))],
        out_specs=[spec_q, spec_q],
        out_shape=[jax.ShapeDtypeStruct((T, NQ), BF16), jax.ShapeDtypeStruct((T, NQ), F32)],
        compiler_params=_params("parallel", "parallel", "parallel"),
    )(q, kd, kd, vd, vd, sink2)


def _swa_bwd(q, kd, vd, sink2, do, lse, B, name):
    T, NQ = q.shape
    NP = NQ // LANES
    nq = T // B // BLK
    NG = kd.shape[1] // LANES
    grp = NP // NG

    def body(q_ref, kp_ref, kc_ref, vp_ref, vc_ref, s_ref, do_ref, l_ref,
             dq_ref, dkc_ref, dkp_ref, dvc_ref, dvp_ref, ds_ref):
        b, i = pl.program_id(1), pl.program_id(2)

        @pl.when((b == 0) & (i == 0))
        def _():
            ds_ref[...] = jnp.zeros_like(ds_ref)
        valid = _swa_valid(i)
        kk = jnp.concatenate([kp_ref[...], kc_ref[...]], axis=0)
        vv = jnp.concatenate([vp_ref[...], vc_ref[...]], axis=0)
        sls = [slice(jj * LANES, (jj + 1) * LANES) for jj in range(grp)]
        sc = [_swa_scores(q_ref[:, sl], kk, s_ref[jj], valid) for jj, sl in enumerate(sls)]
        dos = [_stack_heads(do_ref[:, sl]) for sl in sls]
        dps = [_dot(d, vv, NT) for d in dos]
        lses = [jnp.concatenate([l_ref[:, sl][:, 0:1], l_ref[:, sl][:, HEAD:HEAD + 1]], axis=0) for sl in sls]
        ps = [jnp.where(valid, jnp.exp(s - lse), 0.0) for (_, s, _), lse in zip(sc, lses)]
        deltas = [jnp.sum(p * dp, axis=1, keepdims=True) for p, dp in zip(ps, dps)]
        dscs = [(p * (dp - delta)).astype(BF16) for p, dp, delta in zip(ps, dps, deltas)]
        dqs = [_dot(dsc, kk, NN) for dsc in dscs]
        dk = jnp.zeros((2 * BLK, LANES), F32)
        dv = jnp.zeros((2 * BLK, LANES), F32)
        for jj, sl in enumerate(sls):
            dsk = -jnp.exp(sc[jj][2] - lses[jj]) * deltas[jj]
            dsk_lo = jnp.sum(dsk[:BLK], axis=0, keepdims=True)
            dsk_hi = jnp.sum(dsk[BLK:], axis=0, keepdims=True)
            ds_ref[jj] += jnp.where(_lo_mask((1, LANES)), dsk_lo, dsk_hi)
            dq_ref[:, sl] = _unstack_heads(dqs[jj]) * SCALE
            dk = dk + _dot(dscs[jj], sc[jj][0], TN)
            dv = dv + _dot(ps[jj].astype(BF16), dos[jj], TN)
        dkp_ref[...] = dk[:BLK]
        dkc_ref[...] = dk[BLK:]
        dvp_ref[...] = dv[:BLK]
        dvc_ref[...] = dv[BLK:]

    spec_q = pl.BlockSpec((BLK, grp * LANES), lambda g, b, i: (b * nq + i, g))
    spec_p = pl.BlockSpec((BLK, LANES), lambda g, b, i: (b * nq + jnp.maximum(i - 1, 0), g))
    spec_c = pl.BlockSpec((BLK, LANES), lambda g, b, i: (b * nq + i, g))
    spec_s = pl.BlockSpec((grp, 1, LANES), lambda g, b, i: (g, 0, 0))
    kv = jax.ShapeDtypeStruct((T, NG * LANES), F32)
    return pl.pallas_call(
        body, name=name, grid=(NG, B, nq),
        in_specs=[spec_q, spec_p, spec_c, spec_p, spec_c, spec_s, spec_q, spec_q],
        out_specs=[spec_q, spec_c, spec_c, spec_c, spec_c, spec_s],
        out_shape=[jax.ShapeDtypeStruct((T, NQ), F32), kv, kv, kv, kv, jax.ShapeDtypeStruct((NP, 1, LANES), F32)],
        compiler_params=_params("arbitrary", "arbitrary", "arbitrary"),
    )(q, kd, kd, vd, vd, sink2, do, lse)


def _qk_prep_bwd(qkv, cs, s1, s2, qg, kg, dq, dkc, dkp, dvc, dvp, B, name):
    T, W = qkv.shape
    NQ = W - 2 * LANES
    NP = NQ // LANES
    nq = T // B // BLK

    def body(x_ref, cs_ref, s1_ref, s2_ref, qg_ref, kg_ref, dq_ref, dkc_ref, dkp_ref, dvc_ref, dvp_ref,
             o_ref, dqg_ref, dkg_ref):
        b, i = pl.program_id(0), pl.program_id(1)

        @pl.when((b == 0) & (i == 0))
        def _():
            dqg_ref[...] = jnp.zeros_like(dqg_ref)
            dkg_ref[...] = jnp.zeros_like(dkg_ref)
        P = _head_sum_matrix()
        cs_, s1_, s2_ = cs_ref[...], s1_ref[...], s2_ref[...]
        lo = _lo_mask((BLK, LANES))
        has_next = (i + 1 < nq).astype(F32)

        def norm_rope_bwd(xv, g, d):
            du = _rope_bwd(d, cs_, s1_, s2_)
            r = lax.rsqrt(_dot(xv * xv, P, NN, HIGH) * (1.0 / HEAD) + EPS)
            xhat = xv * r
            dgain = jnp.sum(du * xhat, axis=0, keepdims=True)
            uu = du * g
            dx = r * (uu - xhat * (_dot(uu * xhat, P, NN, HIGH) * (1.0 / HEAD)))
            return dx, dgain + pltpu.roll(dgain, HEAD, 1)

        dqg = jnp.zeros((1, LANES), F32)
        for j in range(NP):
            sl = slice(j * LANES, (j + 1) * LANES)
            dx, dg = norm_rope_bwd(x_ref[:, sl], qg_ref[...], dq_ref[:, sl])
            o_ref[:, sl] = dx.astype(BF16)
            dqg = dqg + dg
        dqg_ref[...] += dqg

        def fold(c_ref, p_ref, g):
            sl = slice(g * LANES, (g + 1) * LANES)
            t = c_ref[:, sl] + has_next * p_ref[:, sl]
            return t + pltpu.roll(t, HEAD, 1)

        dk = jnp.where(lo, fold(dkc_ref, dkp_ref, 0), fold(dkc_ref, dkp_ref, 1))
        dx, dg = norm_rope_bwd(x_ref[:, NQ:NQ + LANES], kg_ref[...], dk)
        o_ref[:, NQ:NQ + LANES] = dx.astype(BF16)
        dkg_ref[...] += dg
        dv = jnp.where(lo, fold(dvc_ref, dvp_ref, 0), fold(dvc_ref, dvp_ref, 1))
        o_ref[:, NQ + LANES:] = dv.astype(BF16)

    spec_t = pl.BlockSpec((BLK, LANES), lambda b, i: (b * nq + i, 0))
    spec_g = pl.BlockSpec((1, LANES), lambda b, i: (0, 0))
    spec_c = pl.BlockSpec((BLK, 2 * LANES), lambda b, i: (b * nq + i, 0))
    spec_n = pl.BlockSpec((BLK, 2 * LANES), lambda b, i: (b * nq + jnp.minimum(i + 1, nq - 1), 0))
    row = jax.ShapeDtypeStruct((1, LANES), F32)
    return pl.pallas_call(
        body, name=name, grid=(B, nq),
        in_specs=[pl.BlockSpec((BLK, W), lambda b, i: (b * nq + i, 0)), spec_t, spec_t, spec_t, spec_g, spec_g,
                  pl.BlockSpec((BLK, NQ), lambda b, i: (b * nq + i, 0)), spec_c, spec_n, spec_c, spec_n],
        out_specs=[pl.BlockSpec((BLK, W), lambda b, i: (b * nq + i, 0)), spec_g, spec_g],
        out_shape=[jax.ShapeDtypeStruct((T, W), BF16), row, row],
        compiler_params=_params("arbitrary", "arbitrary"),
    )(qkv, cs, s1, s2, qg, kg, dq, dkc, dkp, dvc, dvp)


SB_TILE = 256
SB_UNROLL = 4
SB_UNROLL_BWD = 2


def _split_heads(x2, scale=None):
    lo = _lo_mask(x2.shape)
    z = jnp.zeros_like(x2)
    if scale is not None:
        x2 = x2 * scale
    return jnp.where(lo, x2, z), jnp.where(lo, z, x2)


def _sb_terms(qh, kj, diagonal):
    z = _dot(qh, kj, NT)
    e = jnp.exp(-jnp.abs(z))
    lb = jnp.minimum(z, 0.0) - jnp.log(1.0 + e)
    L = lb - z
    if not diagonal:
        return lb, L, None, z, e
    strict = lax.broadcasted_iota(jnp.int32, z.shape, 1) < lax.broadcasted_iota(jnp.int32, z.shape, 0)
    return lb, jnp.where(strict, L, 0.0), strict, z, e


def _tri(n, cmp):
    r = lax.broadcasted_iota(jnp.int32, (n, n), 0)
    c = lax.broadcasted_iota(jnp.int32, (n, n), 1)
    return cmp(r, c).astype(BF16)


def _by_value(r, fns, carry):
    if len(fns) == 1:
        return fns[0](carry)
    half = len(fns) // 2
    return lax.cond(r < half, lambda cr: _by_value(r, fns[:half], cr), lambda cr: _by_value(r - half, fns[half:], cr), carry)


def _whole_grid(first, dims):
    ids = [pl.program_id(a) for a in range(3)]
    return functools.reduce(lambda u, v: u & v, [i == (0 if first else d - 1) for i, d in zip(ids, dims)])


def _sb_fwd(qkv, B, name, gather=None):
    T, W = qkv.shape
    NQ = W // 3
    NP = NQ // LANES
    S = T // B
    tq = min(SB_TILE, S)
    nq = S // tq
    grid = (B, NP, nq)

    def body(q_ref, k_ref, v_ref, *rest):
        if gather is None:
            o_ref, t_ref = rest
        else:
            x_ref, o_ref, t_ref, g_ref, *sems = rest

            @pl.when(_whole_grid(True, grid))
            def _():
                for cp in _direct_copies(x_ref, g_ref, sems, False):
                    cp.start()
        i = pl.program_id(2)
        qh = _split_heads(q_ref[...], SCALE)
        U = _tri(tq, lambda r, c: r > c)

        def sweep(tiles, cs, acc):
            chains = [(t, h) for t in range(len(tiles)) for h in range(2)]
            rows = [pl.ds(pl.multiple_of(j * tq, tq), tq) for j, _ in tiles]
            ks = [k_ref[r, :] for r in rows]
            vs = [_split_heads(v_ref[r, :]) for r in rows]
            terms = {(t, h): _sb_terms(qh[h], ks[t], tiles[t][1]) for t, h in chains}
            carry = {}
            for h in range(2):
                c = cs[h]
                for t in range(len(tiles)):
                    carry[t, h] = c
                    c = c + jnp.sum(terms[t, h][1], axis=1, keepdims=True)
                cs = cs[:h] + (c,) + cs[h + 1:]
            cum = {ch: _dot(terms[ch][1].astype(BF16), U, NN) for ch in chains}
            for ch in chains:
                a = jnp.exp(terms[ch][0] + (cum[ch] + carry[ch]))
                if tiles[ch[0]][1]:
                    a = jnp.where(terms[ch][2], a, 0.0)
                acc = acc + _dot(a.astype(BF16), vs[ch[0]][ch[1]], NN)
            return cs, acc

        zero = jnp.zeros((tq, 1), F32)
        rem = i % SB_UNROLL
        heads = [lambda cr, k=k: sweep([(i, True)] + [(i - 1 - t, False) for t in range(k)], *cr) for k in range(SB_UNROLL)]
        carry = _by_value(rem, heads, ((zero, zero), jnp.zeros((tq, LANES), F32)))
        step = lambda n, cr: sweep([(i - 1 - rem - SB_UNROLL * n - t, False) for t in range(SB_UNROLL)], *cr)
        cs, acc = lax.fori_loop(0, i // SB_UNROLL, step, carry)
        o_ref[...] = acc.astype(BF16)
        t_ref[...] = jnp.where(_lo_mask((tq, LANES)), cs[0], cs[1])
        if gather is not None:
            @pl.when(_whole_grid(False, grid))
            def _():
                for cp in _direct_copies(x_ref, g_ref, sems, False):
                    cp.wait()

    spec_q = pl.BlockSpec((tq, LANES), lambda b, p, i: (b * nq + i, p))
    in_specs = [spec_q, pl.BlockSpec((S, LANES), lambda b, p, i: (b, NP + p)),
                pl.BlockSpec((S, LANES), lambda b, p, i: (b, 2 * NP + p))]
    out_specs = [spec_q, spec_q]
    out_shape = [jax.ShapeDtypeStruct((T, NQ), BF16), jax.ShapeDtypeStruct((T, NQ), F32)]
    args = [qkv, qkv, qkv]
    if gather is not None:
        in_specs.append(HBM_SPEC)
        out_specs.append(HBM_SPEC)
        out_shape.append(jax.ShapeDtypeStruct((NDEV,) + gather.shape, gather.dtype))
        args.append(gather)
    return pl.pallas_call(
        body, name=name, grid=grid, in_specs=in_specs, out_specs=out_specs, out_shape=out_shape,
        scratch_shapes=[] if gather is None else COMM_SEMS,
        compiler_params=_params("arbitrary", "arbitrary", "arbitrary"),
    )(*args)


def _sb_bwd(qkv, q_t, do, do_t, tot, B, name, exchange=None):
    T, W = qkv.shape
    NQ = W // 3
    NP = NQ // LANES
    S = T // B
    tq = min(SB_TILE, S)
    nq = S // tq
    grid = (B, NP, nq)

    def body(q_ref, k_ref, v_ref, do_ref, qt_ref, dot_ref, t_ref, *rest):
        if exchange is None:
            dq_ref, dk_ref, dv_ref = rest
        else:
            x_ref, dq_ref, dk_ref, dv_ref, r_ref, *sems = rest

            @pl.when(_whole_grid(True, grid))
            def _():
                for cp in _direct_copies(x_ref, r_ref, sems, True):
                    cp.start()
        i = pl.program_id(2)

        @pl.when(i == 0)
        def _():
            dk_ref[...] = jnp.zeros_like(dk_ref)
            dv_ref[...] = jnp.zeros_like(dv_ref)
        qh = _split_heads(q_ref[...], SCALE)
        doh = _split_heads(do_ref[...])
        top = lax.broadcasted_iota(jnp.int32, (LANES, tq), 0) < HEAD
        zt = jnp.zeros((LANES, tq), BF16)
        qt = qt_ref[...] * SCALE
        qth = (jnp.where(top, qt, zt), jnp.where(top, zt, qt))
        doth = (jnp.where(top, dot_ref[...], zt), jnp.where(top, zt, dot_ref[...]))
        tt = t_ref[...]
        tot = (tt[:, 0:1], tt[:, HEAD:HEAD + 1])
        Urev = _tri(tq, lambda r, c: r > c)
        Uexc = _tri(tq, lambda r, c: r < c)

        def sweep(tiles, carry):
            nt = len(tiles)
            chains = [(t, h) for t in range(nt) for h in range(2)]
            rows = [pl.ds(pl.multiple_of(j * tq, tq), tq) for j, _ in tiles]
            ks = [k_ref[r, :] for r in rows]
            vs = [v_ref[r, :] for r in rows]
            terms = {(t, h): _sb_terms(qh[h], ks[t], tiles[t][1]) for t, h in chains}
            da = {(t, h): _dot(doh[h], vs[t], NT) for t, h in chains}
            cc = [carry[h][0] for h in range(2)]
            later = {}
            for t, h in chains:
                cc[h] = cc[h] + jnp.sum(terms[t, h][1], axis=1, keepdims=True)
                later[t, h] = tot[h] - cc[h]
            cum = {ch: _dot(terms[ch][1].astype(BF16), Urev, NN) for ch in chains}
            a, g, before = {}, {}, {}
            cg = [carry[h][1] for h in range(2)]
            for ch in chains:
                a[ch] = jnp.exp(terms[ch][0] + (cum[ch] + later[ch]))
                if tiles[ch[0]][1]:
                    a[ch] = jnp.where(terms[ch][2], a[ch], 0.0)
                g[ch] = a[ch] * da[ch]
                before[ch] = cg[ch[1]]
                cg[ch[1]] = cg[ch[1]] + jnp.sum(g[ch], axis=1, keepdims=True)
            G = {ch: _dot(g[ch].astype(BF16), Uexc, NN) for ch in chains}
            dz = {}
            for ch in chains:
                d = g[ch] - jnp.exp(terms[ch][0]) * (g[ch] + (G[ch] + before[ch]))
                if tiles[ch[0]][1]:
                    d = jnp.where(terms[ch][2], d, 0.0)
                dz[ch] = d.astype(BF16)
            dq = [carry[h][2] for h in range(2)]
            for t, h in chains:
                dq[h] = dq[h] + _dot(dz[t, h], ks[t], NN)
            for t in range(nt):
                dk_ref[:, rows[t]] += _dot(qth[0], dz[t, 0], NN) + _dot(qth[1], dz[t, 1], NN)
                dv_ref[:, rows[t]] += _dot(doth[0], a[t, 0].astype(BF16), NN) + _dot(doth[1], a[t, 1].astype(BF16), NN)
            return tuple((cc[h], cg[h], dq[h]) for h in range(2))

        zero = jnp.zeros((tq, 1), F32)
        zq = jnp.zeros((tq, LANES), F32)
        step = lambda n, cr: sweep([(SB_UNROLL_BWD * n + t, False) for t in range(SB_UNROLL_BWD)], cr)
        carry = lax.fori_loop(0, i // SB_UNROLL_BWD, step, ((zero, zero, zq), (zero, zero, zq)))
        tails = [lambda cr, k=k: sweep([(i - k + t, False) for t in range(k)] + [(i, True)], cr) for k in range(SB_UNROLL_BWD)]
        carry = _by_value(i % SB_UNROLL_BWD, tails, carry)
        dq_ref[...] = jnp.where(_lo_mask((tq, LANES)), carry[0][2], carry[1][2]) * SCALE
        if exchange is not None:
            @pl.when(_whole_grid(False, grid))
            def _():
                for cp in _direct_copies(x_ref, r_ref, sems, True):
                    cp.wait()

    spec_q = pl.BlockSpec((tq, LANES), lambda b, p, i: (b * nq + i, p))
    spec_t = pl.BlockSpec((LANES, tq), lambda b, p, i: (p, b * nq + i))
    spec_s = pl.BlockSpec((LANES, S), lambda b, p, i: (b * NP + p, 0))
    key_side = jax.ShapeDtypeStruct((B * NQ, S), F32)
    in_specs = [spec_q, pl.BlockSpec((S, LANES), lambda b, p, i: (b, NP + p)),
                pl.BlockSpec((S, LANES), lambda b, p, i: (b, 2 * NP + p)), spec_q, spec_t, spec_t, spec_q]
    out_specs = [spec_q, spec_s, spec_s]
    out_shape = [jax.ShapeDtypeStruct((T, NQ), F32), key_side, key_side]
    args = [qkv, qkv, qkv, do, q_t, do_t, tot]
    if exchange is not None:
        in_specs.append(HBM_SPEC)
        out_specs.append(HBM_SPEC)
        out_shape.append(jax.ShapeDtypeStruct(exchange.shape, exchange.dtype))
        args.append(exchange)
    return pl.pallas_call(
        body, name=name, grid=grid, in_specs=in_specs, out_specs=out_specs, out_shape=out_shape,
        scratch_shapes=[] if exchange is None else COMM_SEMS,
        compiler_params=_params("arbitrary", "arbitrary", "arbitrary"),
    )(*args)


def _adamw(w, g, m, v, name):
    shape = w.shape
    cols = shape[-1]
    rows = math.prod(shape[:-1])
    tr = _pick(rows, max(8, (1 << 19) // max(cols, LANES) // 8 * 8), 8)

    def body(w_ref, g_ref, m_ref, v_ref, d_ref, mo_ref, vo_ref):
        gv = g_ref[...]
        mn = ADAM_B1 * m_ref[...] + (1.0 - ADAM_B1) * gv
        vn = ADAM_B2 * v_ref[...] + (1.0 - ADAM_B2) * (gv * gv)
        m_hat = mn / (1.0 - ADAM_B1 ** ADAM_STEP)
        v_hat = vn / (1.0 - ADAM_B2 ** ADAM_STEP)
        d_ref[...] = -ADAM_LR * (m_hat / (jnp.sqrt(v_hat) + ADAM_EPS) + ADAM_WD * w_ref[...])
        mo_ref[...] = mn
        vo_ref[...] = vn

    spec = pl.BlockSpec((tr, cols), lambda i: (i, 0))
    out = jax.ShapeDtypeStruct((rows, cols), F32)
    d, mn, vn = pl.pallas_call(
        body, name=name, grid=(rows // tr,),
        in_specs=[spec] * 4, out_specs=[spec] * 3, out_shape=[out] * 3,
        compiler_params=_params("parallel"),
    )(w.reshape(rows, cols), g.reshape(rows, cols), m.reshape(rows, cols), v.reshape(rows, cols))
    return d.reshape(shape), mn.reshape(shape), vn.reshape(shape)


def _pad_rows(a, rows):
    return jnp.pad(a, ((0, rows - a.shape[0]), (0, 0)))


def kernel(x, c, positions, ada_w, ada_b, norm1_g, norm2_g, wqkv_a, q_norm_a, k_norm_a, sinks_a, wo_a, wqkv_b, wo_b, w_gate, w_up, w_down, loss_target, m_ada_w, m_ada_b, m_norm1_g, m_norm2_g, m_wqkv_a, m_q_norm_a, m_k_norm_a, m_sinks_a, m_wo_a, m_wqkv_b, m_wo_b, m_w_gate, m_w_up, m_w_down, v_ada_w, v_ada_b, v_norm1_g, v_norm2_g, v_wqkv_a, v_q_norm_a, v_k_norm_a, v_sinks_a, v_wo_a, v_wqkv_b, v_wo_b, v_w_gate, v_w_up, v_w_down):
    B, S, D = x.shape
    T = B * S
    L = ada_w.shape[0]
    NA, NB_ = wqkv_a.shape[0], wqkv_b.shape[0]
    me = 4 * lax.axis_index("x") + 2 * lax.axis_index("y") + lax.axis_index("c")
    xt = x.reshape(T, D)

    col_sharded = {"qkv_a": wqkv_a, "qkv_b": wqkv_b, "gate": w_gate, "up": w_up}
    row_sharded = {"wo_a": wo_a, "wo_b": wo_b, "down": w_down}

    def shard_rows(key):
        kind, idx = key
        return col_sharded[kind][idx].T if kind in col_sharded else row_sharded[kind][idx]

    def layer_keys(l):
        mix = "a" if l % 2 == 0 else "b"
        return [("qkv_" + mix, l // 2), ("wo_" + mix, l // 2), ("gate", l), ("up", l), ("down", l)]

    def unpack(buf, keys, reshape):
        out, off = {}, 0
        for key in keys:
            rows = shard_rows(key).shape[0]
            out[key] = reshape(buf[..., off:off + rows, :], rows)
            off += rows
        return out

    first_b = 1
    keys_early = layer_keys(0) + [("qkv_b", 0)]
    keys_late = [k for l in range(1, L) for k in layer_keys(l) if k != ("qkv_b", 0)]
    pack = lambda keys: jnp.concatenate([shard_rows(k).astype(BF16) for k in keys], axis=0)
    W = unpack(_all_gather(pack(keys_early), "ag_weights"), keys_early, lambda b, rows: b.reshape(NDEV * rows, D))

    WA = ada_w.shape[2]
    c_all = _all_gather(c, "ag_c").reshape(NDEV * B, D)
    bias = lax.dynamic_slice_in_dim(ada_b, me * WA, WA, axis=1).reshape(L, 1, WA)
    mod_part = _ada_fwd(c_all, ada_w, bias, "ada_fwd")
    mod_all = _all_gather(mod_part.reshape(L * NDEV * B, WA), "ag_mod")
    mod_all = mod_all.reshape(NDEV, L, NDEV * B, WA).transpose(1, 2, 0, 3).reshape(L, NDEV * B, NDEV * WA)
    mod = lax.dynamic_slice_in_dim(mod_all, me * B, B, axis=1)
    mod = mod.reshape(L, B, 6, 1, D)
    sh1, sc1, g1, sh2, sc2, g2 = [mod[:, :, k] for k in range(6)]

    half = ROT // 2
    inv_freq = jnp.power(jnp.float32(ROPE_THETA), -jnp.arange(half, dtype=F32) * 2.0 / ROT)
    ang = positions.reshape(T, 1).astype(F32) * inv_freq[None, :]
    cos, sin = jnp.cos(ang), jnp.sin(ang)
    ones = jnp.ones((T, HEAD - ROT), F32)
    zeros = jnp.zeros((T, HEAD - ROT), F32)
    z8 = jnp.zeros((T, half), F32)
    cs = jnp.tile(jnp.concatenate([cos, cos, ones], axis=1), (1, 2))
    s1 = jnp.tile(jnp.concatenate([-sin, z8, zeros], axis=1), (1, 2))
    s2 = jnp.tile(jnp.concatenate([z8, sin, zeros], axis=1), (1, 2))

    saved = []
    xc = xt
    for l in range(L):
        j = l // 2
        h1 = _norm_mod(xc, norm1_g[l:l + 1], sc1[l], sh1[l], S, f"norm1_{l}")
        sv = dict(x_in=xc, h1=h1)
        if l % 2 == 0:
            qkv = _mm_nt(h1, W["qkv_a", j], F32, f"qkv_a_{l}")
            qg = jnp.tile(q_norm_a[j:j + 1], (1, 2))
            kg = jnp.tile(k_norm_a[j:j + 1], (1, 2))
            qn, kd, vd = _qk_prep(qkv, cs, s1, s2, qg, kg, f"qk_prep_{l}")
            sink2 = jnp.repeat(sinks_a[j].reshape(-1, 2), HEAD, axis=1).reshape(-1, 1, LANES)
            attn, lse = _swa_fwd(qn, kd, vd, sink2, B, f"swa_fwd_{l}")
            sv.update(qkv=qkv, qg=qg, kg=kg, qn=qn, kd=kd, vd=vd, sink2=sink2, lse=lse)
            wo = W["wo_a", j]
        else:
            qkv = _mm_nt(h1, W["qkv_b", j], BF16, f"qkv_b_{l}")
            if l == first_b:
                attn, tot, late = _sb_fwd(qkv, B, f"sb_fwd_{l}", gather=pack(keys_late))
                W.update(unpack(late, keys_late, lambda b, rows: b.reshape(NDEV * rows, D)))
            else:
                attn, tot = _sb_fwd(qkv, B, f"sb_fwd_{l}")
            sv.update(qkv=qkv, tot=tot)
            wo = W["wo_b", j]
        y1, xm = _mm_res(attn, wo, xc, g1[l], S, f"attn_out_{l}")
        h2 = _norm_mod(xm, norm2_g[l:l + 1], sc2[l], sh2[l], S, f"norm2_{l}")
        gate, up, act = _swiglu_fwd(h2, W["gate", l], W["up", l], f"swiglu_fwd_{l}")
        y2, xc = _mm_res(act, W["down", l], xm, g2[l], S, f"mlp_out_{l}")
        sv.update(attn=attn, y1=y1, x_mid=xm, h2=h2, gate=gate, up=up, act=act, y2=y2)
        saved.append(sv)

    dx, loss_tile = _loss_head(xc, loss_target.reshape(T, D), "loss_head")

    G = {}
    pack_grads = lambda keys: jnp.concatenate([G[k].reshape(NDEV, G[k].shape[0] // NDEV, D).astype(BF16) for k in keys], axis=1)
    keys_hi = [k for l in range(first_b + 1, L) for k in layer_keys(l)] + layer_keys(first_b)[1:]
    keys_lo = [k for l in range(first_b + 1) for k in layer_keys(l) if k not in keys_hi]
    received_hi = None
    dmod = [None] * L
    dn1, dn2 = [None] * L, [None] * L
    dqg, dkg, dsink = [None] * NA, [None] * NA, [None] * NA
    for l in reversed(range(L)):
        j = l // 2
        mix = "a" if l % 2 == 0 else "b"
        sv = saved[l]
        dy2, dg2 = _gate_bwd(dx, sv["y2"], g2[l], S, f"gate2_bwd_{l}")
        dgate, dup = _swiglu_bwd(dy2, W["down", l], sv["gate"], sv["up"], f"swiglu_bwd_{l}")
        G["down", l] = _mm_tn(sv["act"], dy2, f"dw_down_{l}")
        dh2 = _mm_nn([(dgate, W["gate", l]), (dup, W["up", l])], f"dh2_{l}")
        G["gate", l] = _mm_tn(dgate, sv["h2"], f"dw_gate_{l}")
        G["up", l] = _mm_tn(dup, sv["h2"], f"dw_up_{l}")
        dxm, dsh2, dsc2, dn2[l] = _norm_mod_bwd(sv["x_mid"], dh2, dx, norm2_g[l:l + 1], sc2[l], S, f"norm2_bwd_{l}")
        dy1, dg1 = _gate_bwd(dxm, sv["y1"], g1[l], S, f"gate1_bwd_{l}")
        dattn = _mm_nt(dy1, W["wo_" + mix, j], BF16, f"dattn_{l}")
        G["wo_" + mix, j] = _mm_tn(sv["attn"], dy1, f"dw_o_{l}")
        if l % 2 == 0:
            dq, dkc, dkp, dvc, dvp, dsink[j] = _swa_bwd(sv["qn"], sv["kd"], sv["vd"], sv["sink2"], dattn, sv["lse"], B,
                                                        f"swa_bwd_{l}")
            dqkv, dqg[j], dkg[j] = _qk_prep_bwd(sv["qkv"], cs, s1, s2, sv["qg"], sv["kg"], dq, dkc, dkp, dvc, dvp, B,
                                                f"qk_prep_bwd_{l}")
        else:
            nqb = sv["qkv"].shape[1] // 3
            sb_args = (sv["qkv"], sv["qkv"][:, :nqb].T, dattn, dattn.T, sv["tot"], B, f"sb_bwd_{l}")
            if l == first_b and keys_hi:
                dq, dk_t, dv_t, received_hi = _sb_bwd(*sb_args, exchange=pack_grads(keys_hi))
            else:
                dq, dk_t, dv_t = _sb_bwd(*sb_args)
            dk, dv = [t.reshape(B, nqb, S).transpose(0, 2, 1).reshape(T, nqb) for t in (dk_t, dv_t)]
            dqkv = jnp.concatenate([dq, dk, dv], axis=1).astype(BF16)
        dh1 = _mm_nn([(dqkv, W["qkv_" + mix, j])], f"dh1_{l}")
        G["qkv_" + mix, j] = _mm_tn(dqkv, sv["h1"], f"dw_qkv_{l}")
        dx, dsh1, dsc1, dn1[l] = _norm_mod_bwd(sv["x_in"], dh1, dxm, norm1_g[l:l + 1], sc1[l], S, f"norm1_bwd_{l}")
        dmod[l] = jnp.concatenate([dsh1, dsc1, dg1, dsh2, dsc2, dg2], axis=1)
    grad_x = dx.reshape(B, S, D)

    ndm = L * 6
    dmod_rows = jnp.stack(dmod, axis=1).reshape(B * ndm, D)
    misc = jnp.concatenate(
        [jnp.concatenate(dn1, axis=0).reshape(B * L, D), jnp.concatenate(dn2, axis=0).reshape(B * L, D),
         _pad_rows(jnp.concatenate([jnp.pad(r, ((0, 0), (0, D - LANES))) for r in dqg + dkg]
                                   + [jnp.pad(r[:, 0, ::HEAD].reshape(1, -1), ((0, 0), (0, D - 2 * r.shape[0]))) for r in dsink]
                                   + [jnp.pad(loss_tile[0:1, 0:1], ((0, 0), (0, D - 1)))], axis=0), 8)], axis=0)
    nmisc = misc.shape[0]
    small = _all_gather(jnp.concatenate([dmod_rows, _pad_rows(misc, -(-nmisc // 8) * 8)], axis=0), "ag_small")
    dmod_all = small[:, :B * ndm].reshape(NDEV * B, ndm, D)
    g_ada_b = _sum_leading(dmod_all, "sum_dmod").reshape(L, 6 * D)
    misc_sum = _sum_leading(small[:, B * ndm:], "sum_misc")
    g_n1 = misc_sum[0:B * L].reshape(L, B, D)
    g_n2 = misc_sum[B * L:2 * B * L].reshape(L, B, D)
    g_norm1 = _sum_leading(g_n1.transpose(1, 0, 2), "sum_n1")
    g_norm2 = _sum_leading(g_n2.transpose(1, 0, 2), "sum_n2")
    o = 2 * B * L
    g_qn = misc_sum[o:o + NA, :HEAD]
    g_kn = misc_sum[o + NA:o + 2 * NA, :HEAD]
    nsink = sinks_a.shape[1]
    g_sink = misc_sum[o + 2 * NA:o + 3 * NA, :nsink]
    loss = misc_sum[o + 3 * NA, 0]

    dmod_loc = lax.dynamic_slice_in_dim(dmod_all.reshape(NDEV * B, L, 6 * D), me * WA, WA, axis=2)
    g_ada_w = _ada_bwd(c_all, dmod_loc.transpose(1, 0, 2), "ada_bwd")

    shard = unpack(_sum_leading(_exchange(pack_grads(keys_lo), "grad_exchange"), "grad_sum"), keys_lo, lambda b, rows: b)
    if received_hi is not None:
        shard.update(unpack(_sum_leading(received_hi, "grad_sum_hi"), keys_hi, lambda b, rows: b))

    def stacked(kind, n):
        return jnp.stack([shard[kind, i].T if kind in col_sharded else shard[kind, i] for i in range(n)])

    gw_qkv_a, gw_qkv_b, gw_gate, gw_up = stacked("qkv_a", NA), stacked("qkv_b", NB_), stacked("gate", L), stacked("up", L)
    gw_wo_a, gw_wo_b, gw_down = stacked("wo_a", NA), stacked("wo_b", NB_), stacked("down", L)

    grads = [g_ada_w, g_ada_b, g_norm1, g_norm2, gw_qkv_a, g_qn, g_kn, g_sink, gw_wo_a, gw_qkv_b, gw_wo_b,
             gw_gate, gw_up, gw_down]
    ws = [ada_w, ada_b, norm1_g, norm2_g, wqkv_a, q_norm_a, k_norm_a, sinks_a, wo_a, wqkv_b, wo_b, w_gate, w_up, w_down]
    ms = [m_ada_w, m_ada_b, m_norm1_g, m_norm2_g, m_wqkv_a, m_q_norm_a, m_k_norm_a, m_sinks_a, m_wo_a, m_wqkv_b,
          m_wo_b, m_w_gate, m_w_up, m_w_down]
    vs = [v_ada_w, v_ada_b, v_norm1_g, v_norm2_g, v_wqkv_a, v_q_norm_a, v_k_norm_a, v_sinks_a, v_wo_a, v_wqkv_b,
          v_wo_b, v_w_gate, v_w_up, v_w_down]
    deltas, new_m, new_v = [], [], []
    for k, (w, g, m, v) in enumerate(zip(ws, grads, ms, vs)):
        g = g.reshape(w.shape)
        d, mn, vn = _adamw(w, g, m, v, f"adamw_{k}")
        grads[k] = g
        deltas.append(d)
        new_m.append(mn)
        new_v.append(vn)
    return (loss, grad_x, *grads, *deltas, *new_m, *new_v)
```

```python
import functools
import math

import jax
import jax.numpy as jnp
from jax import lax
from jax.experimental import pallas as pl
from jax.experimental.pallas import tpu as pltpu

F32 = jnp.float32
BF16 = jnp.bfloat16
NDEV = 8
HEAD = 64
BLK = 128
LANES = 128
EPS = 1e-6
ROT = HEAD // 4
ROPE_THETA = 500000.0
SCALE = HEAD ** -0.5
NEG = -1e30
VMEM_LIMIT = 56 * 1024 * 1024
MESH = pl.DeviceIdType.MESH
HIGH = lax.Precision.HIGHEST

ADAM_LR = 0.001
ADAM_B1 = 0.9
ADAM_B2 = 0.999
ADAM_EPS = 1e-08
ADAM_WD = 0.01
ADAM_STEP = 10


def _params(*sem):
    return pltpu.CompilerParams(dimension_semantics=sem, vmem_limit_bytes=VMEM_LIMIT)


def _pick(n, cap, mult):
    if n <= cap:
        return n
    best = None
    for t in range(mult, cap + 1, mult):
        if n % t == 0:
            best = t
    assert best is not None, (n, cap, mult)
    return best


def _dot(a, b, dims, precision=None):
    return lax.dot_general(a, b, (dims, ((), ())), preferred_element_type=F32, precision=precision)


NN = ((1,), (0,))
NT = ((1,), (1,))
TN = ((0,), (0,))


def _all_gather(x, name):
    m, n = x.shape

    def body(x_ref, out_ref, send_sems, recv_sems, local_sem):
        ix, iy, ic = lax.axis_index("x"), lax.axis_index("y"), lax.axis_index("c")
        me, sibling = (ix, iy, ic), (ix, iy, 1 - ic)
        chips = [(1 - ix, iy), (ix, 1 - iy), (1 - ix, 1 - iy)]

        def slab(px, py, pc):
            return out_ref.at[4 * px + 2 * py + pc]

        def copy(k, block, to, src=None):
            return pltpu.make_async_remote_copy(
                src_ref=slab(*block) if src is None else src, dst_ref=slab(*block),
                send_sem=send_sems.at[k], recv_sem=recv_sems.at[k], device_id=to, device_id_type=MESH)

        mine = pltpu.make_async_copy(x_ref, slab(*me), local_sem)
        mine.start()
        first = [copy(0, me, sibling, src=x_ref)]
        first += [copy(1 + j, me, (*chip, ic), src=x_ref) for j, chip in enumerate(chips)]
        for cp in first:
            cp.start()
        passed = [copy(4 + j, (*chip, ic), sibling) for j, chip in enumerate(chips)]
        for j, chip in enumerate(chips):
            copy(1 + j, (*chip, ic), me).wait_recv()
            passed[j].start()
        copy(0, sibling, me).wait_recv()
        for j, chip in enumerate(chips):
            copy(4 + j, (*chip, 1 - ic), me).wait_recv()
        for cp in first + passed:
            cp.wait_send()
        mine.wait()

    return pl.pallas_call(
        body, name=name,
        out_shape=jax.ShapeDtypeStruct((NDEV, m, n), x.dtype),
        in_specs=[pl.BlockSpec(memory_space=pl.ANY)],
        out_specs=pl.BlockSpec(memory_space=pl.ANY),
        scratch_shapes=[pltpu.SemaphoreType.DMA((7,)), pltpu.SemaphoreType.DMA((7,)), pltpu.SemaphoreType.DMA(())],
    )(x)


COMM_SEMS = [pltpu.SemaphoreType.DMA((NDEV - 1,)), pltpu.SemaphoreType.DMA((NDEV - 1,)), pltpu.SemaphoreType.DMA(())]
HBM_SPEC = pl.BlockSpec(memory_space=pl.ANY)


def _direct_copies(src_ref, dst_ref, sems, scatter):
    send_sems, recv_sems, own_sem = sems
    ix, iy, ic = lax.axis_index("x"), lax.axis_index("y"), lax.axis_index("c")
    me = 4 * ix + 2 * iy + ic
    copies = [pltpu.make_async_copy(src_ref.at[me] if scatter else src_ref, dst_ref.at[me], own_sem)]
    for k in range(1, NDEV):
        px = 1 - ix if k & 4 else ix
        py = 1 - iy if k & 2 else iy
        pc = 1 - ic if k & 1 else ic
        copies.append(pltpu.make_async_remote_copy(
            src_ref=src_ref.at[4 * px + 2 * py + pc] if scatter else src_ref, dst_ref=dst_ref.at[me],
            send_sem=send_sems.at[k - 1], recv_sem=recv_sems.at[k - 1],
            device_id=(px, py, pc), device_id_type=MESH))
    return copies


def _exchange(p, name):
    def body(p_ref, r_ref, *sems):
        copies = _direct_copies(p_ref, r_ref, sems, True)
        for cp in copies:
            cp.start()
        for cp in copies:
            cp.wait()

    return pl.pallas_call(
        body, name=name,
        out_shape=jax.ShapeDtypeStruct(p.shape, p.dtype),
        in_specs=[HBM_SPEC], out_specs=HBM_SPEC, scratch_shapes=COMM_SEMS,
    )(p)


def _sum_leading(r, name):
    k, m, n = r.shape
    mult = 8 * (4 // r.dtype.itemsize)
    tm = _pick(m, max(mult, (4 * 1024 * 1024) // (k * n * r.dtype.itemsize) // mult * mult), mult)

    def body(r_ref, o_ref):
        acc = r_ref[0].astype(F32)
        for s in range(1, k):
            acc = acc + r_ref[s].astype(F32)
        o_ref[...] = acc

    return pl.pallas_call(
        body, name=name, grid=(m // tm,),
        in_specs=[pl.BlockSpec((k, tm, n), lambda i: (0, i, 0))],
        out_specs=pl.BlockSpec((tm, n), lambda i: (i, 0)),
        out_shape=jax.ShapeDtypeStruct((m, n), F32),
        compiler_params=_params("parallel"),
    )(r)


def _mm_nt(a, bt, out_dtype, name):
    M, K = a.shape
    N = bt.shape[0]
    tm, tn = _pick(M, 512, 8), _pick(N, 1536, LANES)

    def body(a_ref, b_ref, o_ref):
        o_ref[...] = _dot(a_ref[...], b_ref[...], NT).astype(out_dtype)

    return pl.pallas_call(
        body, name=name, grid=(N // tn, M // tm),
        in_specs=[pl.BlockSpec((tm, K), lambda j, i: (i, 0)), pl.BlockSpec((tn, K), lambda j, i: (j, 0))],
        out_specs=pl.BlockSpec((tm, tn), lambda j, i: (i, j)),
        out_shape=jax.ShapeDtypeStruct((M, N), out_dtype),
        compiler_params=_params("parallel", "parallel"),
    )(a, bt)


def _mm_nn(pairs, name):
    M = pairs[0][0].shape[0]
    N = pairs[0][1].shape[1]
    tm, tn = _pick(M, 512, 8), _pick(N, 1024, LANES)
    np_ = len(pairs)

    def body(*refs):
        o_ref = refs[-1]
        acc = _dot(refs[0][...], refs[1][...], NN)
        for p in range(1, np_):
            acc = acc + _dot(refs[2 * p][...], refs[2 * p + 1][...], NN)
        o_ref[...] = acc

    in_specs, args = [], []
    for a, b in pairs:
        K = a.shape[1]
        in_specs += [pl.BlockSpec((tm, K), lambda i, j: (i, 0)), pl.BlockSpec((K, tn), lambda i, j: (0, j))]
        args += [a, b]
    return pl.pallas_call(
        body, name=name, grid=(M // tm, N // tn),
        in_specs=in_specs,
        out_specs=pl.BlockSpec((tm, tn), lambda i, j: (i, j)),
        out_shape=jax.ShapeDtypeStruct((M, N), F32),
        compiler_params=_params("parallel", "parallel"),
    )(*args)


def _mm_tn(a, b, name):
    M, N1 = a.shape
    N2 = b.shape[1]
    t1, tk = _pick(N1, 1536, LANES), _pick(M, 512, 8)
    nk = M // tk

    def body(a_ref, b_ref, o_ref, acc_ref):
        k = pl.program_id(1)

        @pl.when(k == 0)
        def _():
            acc_ref[...] = jnp.zeros_like(acc_ref)
        acc_ref[...] += _dot(a_ref[...], b_ref[...], TN)

        @pl.when(k == nk - 1)
        def _():
            o_ref[...] = acc_ref[...].astype(BF16)

    return pl.pallas_call(
        body, name=name, grid=(N1 // t1, nk),
        in_specs=[pl.BlockSpec((tk, t1), lambda i, k: (k, i)), pl.BlockSpec((tk, N2), lambda i, k: (k, 0))],
        out_specs=pl.BlockSpec((t1, N2), lambda i, k: (i, 0)),
        out_shape=jax.ShapeDtypeStruct((N1, N2), BF16),
        scratch_shapes=[pltpu.VMEM((t1, N2), F32)],
        compiler_params=_params("parallel", "arbitrary"),
    )(a, b)


def _mm_res(a, w, x, gate, S, name):
    T, K = a.shape
    D = w.shape[1]
    tm, tn = _pick(S, 512, 8), _pick(D, 1024, LANES)
    nb = S // tm

    def body(a_ref, w_ref, x_ref, g_ref, y_ref, o_ref):
        y = _dot(a_ref[...], w_ref[...], NN)
        y_ref[...] = y.astype(BF16)
        o_ref[...] = x_ref[...] + g_ref[0] * y

    return pl.pallas_call(
        body, name=name, grid=(T // tm, D // tn),
        in_specs=[pl.BlockSpec((tm, K), lambda i, j: (i, 0)), pl.BlockSpec((K, tn), lambda i, j: (0, j)),
                  pl.BlockSpec((tm, tn), lambda i, j: (i, j)), pl.BlockSpec((1, 1, tn), lambda i, j: (i // nb, 0, j))],
        out_specs=[pl.BlockSpec((tm, tn), lambda i, j: (i, j)), pl.BlockSpec((tm, tn), lambda i, j: (i, j))],
        out_shape=[jax.ShapeDtypeStruct((T, D), BF16), jax.ShapeDtypeStruct((T, D), F32)],
        compiler_params=_params("parallel", "parallel"),
    )(a, w, x, gate)


def _swiglu_fwd(h, wgt, wut, name):
    T, D = h.shape
    F = wgt.shape[0]
    tm, tn = _pick(T, 512, 8), _pick(F, 1536, LANES)

    def body(h_ref, g_ref, u_ref, go_ref, uo_ref, a_ref):
        hh = h_ref[...]
        g = _dot(hh, g_ref[...], NT)
        u = _dot(hh, u_ref[...], NT)
        go_ref[...] = g.astype(BF16)
        uo_ref[...] = u.astype(BF16)
        a_ref[...] = (g * jax.nn.sigmoid(g) * u).astype(BF16)

    spec_w = pl.BlockSpec((tn, D), lambda j, i: (j, 0))
    spec_o = pl.BlockSpec((tm, tn), lambda j, i: (i, j))
    out = jax.ShapeDtypeStruct((T, F), BF16)
    return pl.pallas_call(
        body, name=name, grid=(F // tn, T // tm),
        in_specs=[pl.BlockSpec((tm, D), lambda j, i: (i, 0)), spec_w, spec_w],
        out_specs=[spec_o, spec_o, spec_o],
        out_shape=[out, out, out],
        compiler_params=_params("parallel", "parallel"),
    )(h, wgt, wut)


def _swiglu_bwd(dy, wd, gate, up, name):
    T, D = dy.shape
    F = wd.shape[0]
    tm, tn = _pick(T, 512, 8), _pick(F, 1536, LANES)

    def body(dy_ref, w_ref, g_ref, u_ref, dg_ref, du_ref):
        da = _dot(dy_ref[...], w_ref[...], NT)
        g = g_ref[...].astype(F32)
        sg = jax.nn.sigmoid(g)
        silu = g * sg
        du_ref[...] = (da * silu).astype(BF16)
        dg_ref[...] = (da * u_ref[...].astype(F32) * (sg + silu * (1.0 - sg))).astype(BF16)

    spec_o = pl.BlockSpec((tm, tn), lambda j, i: (i, j))
    return pl.pallas_call(
        body, name=name, grid=(F // tn, T // tm),
        in_specs=[pl.BlockSpec((tm, D), lambda j, i: (i, 0)), pl.BlockSpec((tn, D), lambda j, i: (j, 0)), spec_o, spec_o],
        out_specs=[spec_o, spec_o],
        out_shape=[jax.ShapeDtypeStruct((T, F), BF16), jax.ShapeDtypeStruct((T, F), BF16)],
        compiler_params=_params("parallel", "parallel"),
    )(dy, wd, gate, up)


def _norm_mod(x, gain, sc, sh, S, name):
    T, D = x.shape
    tm = _pick(S, 512, 8)
    nb = S // tm

    def body(x_ref, g_ref, sc_ref, sh_ref, o_ref):
        xv = x_ref[...]
        r = lax.rsqrt(jnp.mean(xv * xv, axis=-1, keepdims=True) + EPS)
        o_ref[...] = ((xv * r) * g_ref[...] * (1.0 + sc_ref[0]) + sh_ref[0]).astype(BF16)

    spec_b = pl.BlockSpec((1, 1, D), lambda i: (i // nb, 0, 0))
    return pl.pallas_call(
        body, name=name, grid=(T // tm,),
        in_specs=[pl.BlockSpec((tm, D), lambda i: (i, 0)), pl.BlockSpec((1, D), lambda i: (0, 0)), spec_b, spec_b],
        out_specs=pl.BlockSpec((tm, D), lambda i: (i, 0)),
        out_shape=jax.ShapeDtypeStruct((T, D), BF16),
        compiler_params=_params("parallel"),
    )(x, gain, sc, sh)


def _norm_mod_bwd(x, dh, dres, gain, sc, S, name, below=None):
    T, D = x.shape
    B = T // S
    tm = _pick(S, 256, 8)
    nb = S // tm

    def body(x_ref, dh_ref, dr_ref, g_ref, sc_ref, *rest):
        if below is None:
            o_ref, dsh_ref, dsc_ref, dg_ref = rest
            sums = [dsh_ref, dsc_ref, dg_ref]
        else:
            y_ref, gt_ref, o_ref, dsh_ref, dsc_ref, dg_ref, dy_ref, dgt_ref = rest
            sums = [dsh_ref, dsc_ref, dg_ref, dgt_ref]

        @pl.when(pl.program_id(1) == 0)
        def _():
            for ref in sums:
                ref[...] = jnp.zeros_like(ref)
        xv, dhv, g = x_ref[...], dh_ref[...], g_ref[...]
        r = lax.rsqrt(jnp.mean(xv * xv, axis=-1, keepdims=True) + EPS)
        xhat = xv * r
        dsh_ref[0] += jnp.sum(dhv, axis=0, keepdims=True)
        dsc_ref[0] += jnp.sum(dhv * (xhat * g), axis=0, keepdims=True)
        dn = dhv * (1.0 + sc_ref[0])
        dg_ref[0] += jnp.sum(dn * xhat, axis=0, keepdims=True)
        dxh = dn * g
        out = dr_ref[...] + r * (dxh - xhat * jnp.mean(dxh * xhat, axis=-1, keepdims=True))
        o_ref[...] = out
        if below is not None:
            dy_ref[...] = (out * gt_ref[0]).astype(BF16)
            dgt_ref[0] += jnp.sum(out * y_ref[...].astype(F32), axis=0, keepdims=True)

    spec_t = pl.BlockSpec((tm, D), lambda b, i: (b * nb + i, 0))
    spec_b = pl.BlockSpec((1, 1, D), lambda b, i: (b, 0, 0))
    red = jax.ShapeDtypeStruct((B, 1, D), F32)
    in_specs = [spec_t, spec_t, spec_t, pl.BlockSpec((1, D), lambda b, i: (0, 0)), spec_b]
    out_specs = [spec_t, spec_b, spec_b, spec_b]
    out_shape = [jax.ShapeDtypeStruct((T, D), F32), red, red, red]
    args = [x, dh, dres, gain, sc]
    if below is not None:
        in_specs += [spec_t, spec_b]
        out_specs += [spec_t, spec_b]
        out_shape += [jax.ShapeDtypeStruct((T, D), BF16), red]
        args += list(below)
    return pl.pallas_call(
        body, name=name, grid=(B, nb), in_specs=in_specs, out_specs=out_specs, out_shape=out_shape,
        compiler_params=_params("parallel", "arbitrary"),
    )(*args)


def _gate_bwd(dx, y, gate, S, name):
    T, D = dx.shape
    B = T // S
    tm = _pick(S, 512, 8)
    nb = S // tm

    def body(dx_ref, y_ref, g_ref, dy_ref, dg_ref):
        @pl.when(pl.program_id(1) == 0)
        def _():
            dg_ref[...] = jnp.zeros_like(dg_ref)
        d = dx_ref[...]
        dy_ref[...] = (d * g_ref[0]).astype(BF16)
        dg_ref[0] += jnp.sum(d * y_ref[...].astype(F32), axis=0, keepdims=True)

    spec_t = pl.BlockSpec((tm, D), lambda b, i: (b * nb + i, 0))
    spec_b = pl.BlockSpec((1, 1, D), lambda b, i: (b, 0, 0))
    return pl.pallas_call(
        body, name=name, grid=(B, nb),
        in_specs=[spec_t, spec_t, spec_b],
        out_specs=[spec_t, spec_b],
        out_shape=[jax.ShapeDtypeStruct((T, D), BF16), jax.ShapeDtypeStruct((B, 1, D), F32)],
        compiler_params=_params("parallel", "arbitrary"),
    )(dx, y, gate)


def _loss_head(y, target, name):
    T, D = y.shape
    tm = _pick(T, 512, 8)

    def body(y_ref, t_ref, dy_ref, l_ref):
        @pl.when(pl.program_id(0) == 0)
        def _():
            l_ref[...] = jnp.zeros_like(l_ref)
        e = y_ref[...] - t_ref[...]
        dy_ref[...] = e * (1.0 / D)
        l_ref[...] += 0.5 * jnp.sum(jnp.mean(e * e, axis=-1, keepdims=True), axis=0, keepdims=True)

    spec = pl.BlockSpec((tm, D), lambda i: (i, 0))
    return pl.pallas_call(
        body, name=name, grid=(T // tm,),
        in_specs=[spec, spec],
        out_specs=[spec, pl.BlockSpec((8, LANES), lambda i: (0, 0))],
        out_shape=[jax.ShapeDtypeStruct((T, D), F32), jax.ShapeDtypeStruct((8, LANES), F32)],
        compiler_params=_params("arbitrary"),
    )(y, target)


def _ada_fwd(c_all, ada_w, bias, name):
    NB, D = c_all.shape
    L, _, W = ada_w.shape

    def body(c_ref, w_ref, b_ref, o_ref):
        cv = c_ref[...]
        cond = cv * jax.nn.sigmoid(cv)
        o_ref[0] = _dot(cond, w_ref[0], NN, HIGH) + b_ref[0]

    return pl.pallas_call(
        body, name=name, grid=(L,),
        in_specs=[pl.BlockSpec((NB, D), lambda l: (0, 0)), pl.BlockSpec((1, D, W), lambda l: (l, 0, 0)),
                  pl.BlockSpec((1, 1, W), lambda l: (l, 0, 0))],
        out_specs=pl.BlockSpec((1, NB, W), lambda l: (l, 0, 0)),
        out_shape=jax.ShapeDtypeStruct((L, NB, W), F32),
        compiler_params=_params("parallel"),
    )(c_all, ada_w, bias)


def _ada_bwd(c_all, dmod, name):
    NB, D = c_all.shape
    L, _, W = dmod.shape

    def body(c_ref, d_ref, o_ref):
        cv = c_ref[...]
        cond = cv * jax.nn.sigmoid(cv)
        o_ref[0] = _dot(cond, d_ref[0], TN, HIGH)

    return pl.pallas_call(
        body, name=name, grid=(L,),
        in_specs=[pl.BlockSpec((NB, D), lambda l: (0, 0)), pl.BlockSpec((1, NB, W), lambda l: (l, 0, 0))],
        out_specs=pl.BlockSpec((1, D, W), lambda l: (l, 0, 0)),
        out_shape=jax.ShapeDtypeStruct((L, D, W), F32),
        compiler_params=_params("parallel"),
    )(c_all, dmod)


def _lo_mask(shape):
    return lax.broadcasted_iota(jnp.int32, shape, len(shape) - 1) < HEAD


def _head_sum_matrix():
    r = lax.broadcasted_iota(jnp.int32, (LANES, LANES), 0) // HEAD
    c = lax.broadcasted_iota(jnp.int32, (LANES, LANES), 1) // HEAD
    return (r == c).astype(BF16)


def _head_sum(x, P):
    hi = x.astype(BF16)
    lo = (x - hi.astype(F32)).astype(BF16)
    return _dot(hi, P, NN) + _dot(lo, P, NN)


def _rope(y, cs, s1, s2):
    return y * cs + pltpu.roll(y, LANES - ROT // 2, 1) * s1 + pltpu.roll(y, ROT // 2, 1) * s2


def _rope_bwd(d, cs, s1, s2):
    return d * cs + pltpu.roll(d * s1, ROT // 2, 1) + pltpu.roll(d * s2, LANES - ROT // 2, 1)


def _qk_prep(qkv, cs, s1, s2, qg, kg, name):
    T, W = qkv.shape
    NQ = W - 2 * LANES
    tm = _pick(T, 512, 8)

    def body(x_ref, cs_ref, s1_ref, s2_ref, qg_ref, kg_ref, q_ref, k_ref, v_ref):
        P = _head_sum_matrix()
        cs_, s1_, s2_ = cs_ref[...], s1_ref[...], s2_ref[...]
        lo = _lo_mask((tm, LANES))

        def norm_rope(xv, g):
            ms = _head_sum(xv * xv, P) * (1.0 / HEAD)
            return _rope(xv * lax.rsqrt(ms + EPS) * g, cs_, s1_, s2_)

        for j in range(NQ // LANES):
            q_ref[:, j * LANES:(j + 1) * LANES] = norm_rope(x_ref[:, j * LANES:(j + 1) * LANES], qg_ref[...]).astype(BF16)
        kr = norm_rope(x_ref[:, NQ:NQ + LANES], kg_ref[...])
        ks = pltpu.roll(kr, HEAD, 1)
        k_ref[:, :LANES] = jnp.where(lo, kr, ks).astype(BF16)
        k_ref[:, LANES:] = jnp.where(lo, ks, kr).astype(BF16)
        vr = x_ref[:, NQ + LANES:]
        vs = pltpu.roll(vr, HEAD, 1)
        v_ref[:, :LANES] = jnp.where(lo, vr, vs).astype(BF16)
        v_ref[:, LANES:] = jnp.where(lo, vs, vr).astype(BF16)

    spec_t = pl.BlockSpec((tm, LANES), lambda i: (i, 0))
    spec_g = pl.BlockSpec((1, LANES), lambda i: (0, 0))
    return pl.pallas_call(
        body, name=name, grid=(T // tm,),
        in_specs=[pl.BlockSpec((tm, W), lambda i: (i, 0)), spec_t, spec_t, spec_t, spec_g, spec_g],
        out_specs=[pl.BlockSpec((tm, NQ), lambda i: (i, 0)), pl.BlockSpec((tm, 2 * LANES), lambda i: (i, 0)),
                   pl.BlockSpec((tm, 2 * LANES), lambda i: (i, 0))],
        out_shape=[jax.ShapeDtypeStruct((T, NQ), BF16), jax.ShapeDtypeStruct((T, 2 * LANES), BF16),
                   jax.ShapeDtypeStruct((T, 2 * LANES), BF16)],
        compiler_params=_params("parallel"),
    )(qkv, cs, s1, s2, qg, kg)


def _stack_heads(x2):
    lo = _lo_mask(x2.shape)
    z = jnp.zeros_like(x2)
    return jnp.concatenate([jnp.where(lo, x2, z), jnp.where(lo, z, x2)], axis=0)


def _unstack_heads(xs):
    r = xs.shape[0] // 2
    return jnp.where(_lo_mask((r, LANES)), xs[:r], xs[r:])


def _swa_valid(i):
    qo = lax.broadcasted_iota(jnp.int32, (2 * BLK, 2 * BLK), 0) % BLK
    kc_ = lax.broadcasted_iota(jnp.int32, (2 * BLK, 2 * BLK), 1)
    rel = qo + BLK - kc_
    return (rel >= 0) & (rel < BLK) & ((kc_ >= BLK) | (i > 0))


def _swa_scores(q2, kk, sink2, valid):
    qs = _stack_heads(q2) * SCALE
    s = _dot(qs, kk, NT)
    sk = jnp.concatenate([jnp.broadcast_to(sink2[:, 0:1], (BLK, 1)), jnp.broadcast_to(sink2[:, HEAD:HEAD + 1], (BLK, 1))], axis=0)
    return qs, jnp.where(valid, s, NEG), sk


def _swa_fwd(q, kd, vd, sink2, B, name):
    T, NQ = q.shape
    NP = NQ // LANES
    nq = T // B // BLK
    NG = kd.shape[1] // LANES
    grp = NP // NG

    def body(q_ref, kp_ref, kc_ref, vp_ref, vc_ref, s_ref, o_ref, l_ref):
        valid = _swa_valid(pl.program_id(2))
        kk = jnp.concatenate([kp_ref[...], kc_ref[...]], axis=0)
        vs = _stack_heads(jnp.concatenate([vp_ref[...], vc_ref[...]], axis=0))
        sls = [slice(jj * LANES, (jj + 1) * LANES) for jj in range(grp)]
        sc = [_swa_scores(q_ref[:, sl], kk, s_ref[jj], valid) for jj, sl in enumerate(sls)]
        ms = [jnp.maximum(jnp.max(s, axis=1, keepdims=True), sk) for _, s, sk in sc]
        ps = [jnp.where(valid, jnp.exp(s - m), 0.0) for (_, s, _), m in zip(sc, ms)]
        ls = [jnp.sum(p, axis=1, keepdims=True) + jnp.exp(sk - m) for p, (_, _, sk), m in zip(ps, sc, ms)]
        ps = [(p * (1.0 / l)).astype(BF16) for p, l in zip(ps, ls)]
        os_ = [_dot(jnp.concatenate([p[:BLK], p[BLK:]], axis=1), vs, NN) for p in ps]
        for sl, o, m, l in zip(sls, os_, ms, ls):
            o_ref[:, sl] = o.astype(BF16)
            l_ref[:, sl] = _unstack_heads(jnp.broadcast_to(m + jnp.log(l), (2 * BLK, LANES)))

    spec_q = pl.BlockSpec((BLK, grp * LANES), lambda b, g, i: (b * nq + i, g))
    spec_p = pl.BlockSpec((BLK, LANES), lambda b, g, i: (b * nq + jnp.maximum(i - 1, 0), g))
    spec_c = pl.BlockSpec((BLK, LANES), lambda b, g, i: (b * nq + i, g))
    return pl.pallas_call(
        body, name=name, grid=(B, NG, nq),
        in_specs=[spec_q, spec_p, spec_c, spec_p, spec_c, pl.BlockSpec((grp, 1, LANES), lambda b, g, i: (g, 0, 0))],
        out_specs=[spec_q, spec_q],
        out_shape=[jax.ShapeDtypeStruct((T, NQ), BF16), jax.ShapeDtypeStruct((T, NQ), F32)],
        compiler_params=_params("parallel", "parallel", "parallel"),
    )(q, kd, kd, vd, vd, sink2)


def _swa_bwd(q, kd, vd, sink2, do, lse, B, name):
    T, NQ = q.shape
    NP = NQ // LANES
    nq = T // B // BLK
    NG = kd.shape[1] // LANES
    grp = NP // NG

    def body(q_ref, kp_ref, kc_ref, vp_ref, vc_ref, s_ref, do_ref, l_ref,
             dq_ref, dkc_ref, dkp_ref, dvc_ref, dvp_ref, ds_ref):
        b, i = pl.program_id(1), pl.program_id(2)

        @pl.when((b == 0) & (i == 0))
        def _():
            ds_ref[...] = jnp.zeros_like(ds_ref)
        valid = _swa_valid(i)
        kk = jnp.concatenate([kp_ref[...], kc_ref[...]], axis=0)
        vv = jnp.concatenate([vp_ref[...], vc_ref[...]], axis=0)
        sls = [slice(jj * LANES, (jj + 1) * LANES) for jj in range(grp)]
        sc = [_swa_scores(q_ref[:, sl], kk, s_ref[jj], valid) for jj, sl in enumerate(sls)]
        dos = [_stack_heads(do_ref[:, sl]) for sl in sls]
        dps = [_dot(d, vv, NT) for d in dos]
        lses = [jnp.concatenate([l_ref[:, sl][:, 0:1], l_ref[:, sl][:, HEAD:HEAD + 1]], axis=0) for sl in sls]
        ps = [jnp.where(valid, jnp.exp(s - lse), 0.0) for (_, s, _), lse in zip(sc, lses)]
        deltas = [jnp.sum(p * dp, axis=1, keepdims=True) for p, dp in zip(ps, dps)]
        dscs = [(p * (dp - delta)).astype(BF16) for p, dp, delta in zip(ps, dps, deltas)]
        dqs = [_dot(dsc, kk, NN) for dsc in dscs]
        dk = jnp.zeros((2 * BLK, LANES), F32)
        dv = jnp.zeros((2 * BLK, LANES), F32)
        for jj, sl in enumerate(sls):
            dsk = -jnp.exp(sc[jj][2] - lses[jj]) * deltas[jj]
            dsk_lo = jnp.sum(dsk[:BLK], axis=0, keepdims=True)
            dsk_hi = jnp.sum(dsk[BLK:], axis=0, keepdims=True)
            ds_ref[jj] += jnp.where(_lo_mask((1, LANES)), dsk_lo, dsk_hi)
            dq_ref[:, sl] = _unstack_heads(dqs[jj]) * SCALE
            dk = dk + _dot(dscs[jj], sc[jj][0], TN)
            dv = dv + _dot(ps[jj].astype(BF16), dos[jj], TN)
        dkp_ref[...] = dk[:BLK]
        dkc_ref[...] = dk[BLK:]
        dvp_ref[...] = dv[:BLK]
        dvc_ref[...] = dv[BLK:]

    spec_q = pl.BlockSpec((BLK, grp * LANES), lambda g, b, i: (b * nq + i, g))
    spec_p = pl.BlockSpec((BLK, LANES), lambda g, b, i: (b * nq + jnp.maximum(i - 1, 0), g))
    spec_c = pl.BlockSpec((BLK, LANES), lambda g, b, i: (b * nq + i, g))
    spec_s = pl.BlockSpec((grp, 1, LANES), lambda g, b, i: (g, 0, 0))
    kv = jax.ShapeDtypeStruct((T, NG * LANES), F32)
    return pl.pallas_call(
        body, name=name, grid=(NG, B, nq),
        in_specs=[spec_q, spec_p, spec_c, spec_p, spec_c, spec_s, spec_q, spec_q],
        out_specs=[spec_q, spec_c, spec_c, spec_c, spec_c, spec_s],
        out_shape=[jax.ShapeDtypeStruct((T, NQ), F32), kv, kv, kv, kv, jax.ShapeDtypeStruct((NP, 1, LANES), F32)],
        compiler_params=_params("arbitrary", "arbitrary", "arbitrary"),
    )(q, kd, kd, vd, vd, sink2, do, lse)


def _qk_prep_bwd(qkv, cs, s1, s2, qg, kg, dq, dkc, dkp, dvc, dvp, B, name):
    T, W = qkv.shape
    NQ = W - 2 * LANES
    NP = NQ // LANES
    nq = T // B // BLK

    def body(x_ref, cs_ref, s1_ref, s2_ref, qg_ref, kg_ref, dq_ref, dkc_ref, dkp_ref, dvc_ref, dvp_ref,
             o_ref, dqg_ref, dkg_ref):
        b, i = pl.program_id(0), pl.program_id(1)

        @pl.when((b == 0) & (i == 0))
        def _():
            dqg_ref[...] = jnp.zeros_like(dqg_ref)
            dkg_ref[...] = jnp.zeros_like(dkg_ref)
        P = _head_sum_matrix()
        cs_, s1_, s2_ = cs_ref[...], s1_ref[...], s2_ref[...]
        lo = _lo_mask((BLK, LANES))
        has_next = (i + 1 < nq).astype(F32)

        def norm_rope_bwd(xv, g, d):
            du = _rope_bwd(d, cs_, s1_, s2_)
            r = lax.rsqrt(_head_sum(xv * xv, P) * (1.0 / HEAD) + EPS)
            xhat = xv * r
            dgain = jnp.sum(du * xhat, axis=0, keepdims=True)
            uu = du * g
            dx = r * (uu - xhat * (_head_sum(uu * xhat, P) * (1.0 / HEAD)))
            return dx, dgain + pltpu.roll(dgain, HEAD, 1)

        dqg = jnp.zeros((1, LANES), F32)
        for j in range(NP):
            sl = slice(j * LANES, (j + 1) * LANES)
            dx, dg = norm_rope_bwd(x_ref[:, sl], qg_ref[...], dq_ref[:, sl])
            o_ref[:, sl] = dx.astype(BF16)
            dqg = dqg + dg
        dqg_ref[...] += dqg

        def fold(c_ref, p_ref, g):
            sl = slice(g * LANES, (g + 1) * LANES)
            t = c_ref[:, sl] + has_next * p_ref[:, sl]
            return t + pltpu.roll(t, HEAD, 1)

        dk = jnp.where(lo, fold(dkc_ref, dkp_ref, 0), fold(dkc_ref, dkp_ref, 1))
        dx, dg = norm_rope_bwd(x_ref[:, NQ:NQ + LANES], kg_ref[...], dk)
        o_ref[:, NQ:NQ + LANES] = dx.astype(BF16)
        dkg_ref[...] += dg
        dv = jnp.where(lo, fold(dvc_ref, dvp_ref, 0), fold(dvc_ref, dvp_ref, 1))
        o_ref[:, NQ + LANES:] = dv.astype(BF16)

    spec_t = pl.BlockSpec((BLK, LANES), lambda b, i: (b * nq + i, 0))
    spec_g = pl.BlockSpec((1, LANES), lambda b, i: (0, 0))
    spec_c = pl.BlockSpec((BLK, 2 * LANES), lambda b, i: (b * nq + i, 0))
    spec_n = pl.BlockSpec((BLK, 2 * LANES), lambda b, i: (b * nq + jnp.minimum(i + 1, nq - 1), 0))
    row = jax.ShapeDtypeStruct((1, LANES), F32)
    return pl.pallas_call(
        body, name=name, grid=(B, nq),
        in_specs=[pl.BlockSpec((BLK, W), lambda b, i: (b * nq + i, 0)), spec_t, spec_t, spec_t, spec_g, spec_g,
                  pl.BlockSpec((BLK, NQ), lambda b, i: (b * nq + i, 0)), spec_c, spec_n, spec_c, spec_n],
        out_specs=[pl.BlockSpec((BLK, W), lambda b, i: (b * nq + i, 0)), spec_g, spec_g],
        out_shape=[jax.ShapeDtypeStruct((T, W), BF16), row, row],
        compiler_params=_params("arbitrary", "arbitrary"),
    )(qkv, cs, s1, s2, qg, kg, dq, dkc, dkp, dvc, dvp)


SB_TILE = 256
SB_UNROLL = 4
SB_UNROLL_BWD = 2


def _split_heads(x2, scale=None):
    lo = _lo_mask(x2.shape)
    z = jnp.zeros_like(x2)
    if scale is not None:
        x2 = x2 * scale
    return jnp.where(lo, x2, z), jnp.where(lo, z, x2)


def _sb_terms(qh, kj, diagonal):
    z = _dot(qh, kj, NT)
    e = jnp.exp(-jnp.abs(z))
    lb = jnp.minimum(z, 0.0) - jnp.log(1.0 + e)
    L = lb - z
    if not diagonal:
        return lb, L, None, z, e
    strict = lax.broadcasted_iota(jnp.int32, z.shape, 1) < lax.broadcasted_iota(jnp.int32, z.shape, 0)
    return lb, jnp.where(strict, L, 0.0), strict, z, e


def _tri(n, cmp):
    r = lax.broadcasted_iota(jnp.int32, (n, n), 0)
    c = lax.broadcasted_iota(jnp.int32, (n, n), 1)
    return cmp(r, c).astype(BF16)


def _by_value(r, fns, carry):
    if len(fns) == 1:
        return fns[0](carry)
    half = len(fns) // 2
    return lax.cond(r < half, lambda cr: _by_value(r, fns[:half], cr), lambda cr: _by_value(r - half, fns[half:], cr), carry)


def _whole_grid(first, dims):
    ids = [pl.program_id(a) for a in range(3)]
    return functools.reduce(lambda u, v: u & v, [i == (0 if first else d - 1) for i, d in zip(ids, dims)])


def _sb_fwd(qkv, B, name, gather=None):
    T, W = qkv.shape
    NQ = W // 3
    NP = NQ // LANES
    S = T // B
    tq = min(SB_TILE, S)
    nq = S // tq
    grid = (B, NP, nq)

    def body(q_ref, k_ref, v_ref, *rest):
        if gather is None:
            o_ref, t_ref = rest
        else:
            x_ref, o_ref, t_ref, g_ref, *sems = rest

            @pl.when(_whole_grid(True, grid))
            def _():
                for cp in _direct_copies(x_ref, g_ref, sems, False):
                    cp.start()
        i = pl.program_id(2)
        qh = _split_heads(q_ref[...], SCALE)
        U = _tri(tq, lambda r, c: r > c)

        def sweep(tiles, cs, acc):
            chains = [(t, h) for t in range(len(tiles)) for h in range(2)]
            rows = [pl.ds(pl.multiple_of(j * tq, tq), tq) for j, _ in tiles]
            ks = [k_ref[r, :] for r in rows]
            vs = [_split_heads(v_ref[r, :]) for r in rows]
            terms = {(t, h): _sb_terms(qh[h], ks[t], tiles[t][1]) for t, h in chains}
            carry = {}
            for h in range(2):
                c = cs[h]
                for t in range(len(tiles)):
                    carry[t, h] = c
                    c = c + jnp.sum(terms[t, h][1], axis=1, keepdims=True)
                cs = cs[:h] + (c,) + cs[h + 1:]
            cum = {ch: _dot(terms[ch][1].astype(BF16), U, NN) for ch in chains}
            for ch in chains:
                a = jnp.exp(terms[ch][0] + (cum[ch] + carry[ch]))
                if tiles[ch[0]][1]:
                    a = jnp.where(terms[ch][2], a, 0.0)
                acc = acc + _dot(a.astype(BF16), vs[ch[0]][ch[1]], NN)
            return cs, acc

        zero = jnp.zeros((tq, 1), F32)
        rem = i % SB_UNROLL
        heads = [lambda cr, k=k: sweep([(i, True)] + [(i - 1 - t, False) for t in range(k)], *cr) for k in range(SB_UNROLL)]
        carry = _by_value(rem, heads, ((zero, zero), jnp.zeros((tq, LANES), F32)))
        step = lambda n, cr: sweep([(i - 1 - rem - SB_UNROLL * n - t, False) for t in range(SB_UNROLL)], *cr)
        cs, acc = lax.fori_loop(0, i // SB_UNROLL, step, carry)
        o_ref[...] = acc.astype(BF16)
        t_ref[...] = jnp.where(_lo_mask((tq, LANES)), cs[0], cs[1])
        if gather is not None:
            @pl.when(_whole_grid(False, grid))
            def _():
                for cp in _direct_copies(x_ref, g_ref, sems, False):
                    cp.wait()

    spec_q = pl.BlockSpec((tq, LANES), lambda b, p, i: (b * nq + i, p))
    in_specs = [spec_q, pl.BlockSpec((S, LANES), lambda b, p, i: (b, NP + p)),
                pl.BlockSpec((S, LANES), lambda b, p, i: (b, 2 * NP + p))]
    out_specs = [spec_q, spec_q]
    out_shape = [jax.ShapeDtypeStruct((T, NQ), BF16), jax.ShapeDtypeStruct((T, NQ), F32)]
    args = [qkv, qkv, qkv]
    if gather is not None:
        in_specs.append(HBM_SPEC)
        out_specs.append(HBM_SPEC)
        out_shape.append(jax.ShapeDtypeStruct((NDEV,) + gather.shape, gather.dtype))
        args.append(gather)
    return pl.pallas_call(
        body, name=name, grid=grid, in_specs=in_specs, out_specs=out_specs, out_shape=out_shape,
        scratch_shapes=[] if gather is None else COMM_SEMS,
        compiler_params=_params("arbitrary", "arbitrary", "arbitrary"),
    )(*args)


def _sb_bwd(qkv, q_t, do, do_t, tot, B, name, exchange=None):
    T, W = qkv.shape
    NQ = W // 3
    NP = NQ // LANES
    S = T // B
    tq = min(SB_TILE, S)
    nq = S // tq
    grid = (B, NP, nq)

    def body(q_ref, k_ref, v_ref, do_ref, qt_ref, dot_ref, t_ref, *rest):
        if exchange is None:
            dq_ref, dk_ref, dv_ref = rest
        else:
            x_ref, dq_ref, dk_ref, dv_ref, r_ref, *sems = rest

            @pl.when(_whole_grid(True, grid))
            def _():
                for cp in _direct_copies(x_ref, r_ref, sems, True):
                    cp.start()
        i = pl.program_id(2)

        @pl.when(i == 0)
        def _():
            dk_ref[...] = jnp.zeros_like(dk_ref)
            dv_ref[...] = jnp.zeros_like(dv_ref)
        qh = _split_heads(q_ref[...], SCALE)
        doh = _split_heads(do_ref[...])
        top = lax.broadcasted_iota(jnp.int32, (LANES, tq), 0) < HEAD
        zt = jnp.zeros((LANES, tq), BF16)
        qt = qt_ref[...] * SCALE
        qth = (jnp.where(top, qt, zt), jnp.where(top, zt, qt))
        doth = (jnp.where(top, dot_ref[...], zt), jnp.where(top, zt, dot_ref[...]))
        tt = t_ref[...]
        tot = (tt[:, 0:1], tt[:, HEAD:HEAD + 1])
        Urev = _tri(tq, lambda r, c: r > c)
        Uexc = _tri(tq, lambda r, c: r < c)

        def sweep(tiles, carry):
            nt = len(tiles)
            chains = [(t, h) for t in range(nt) for h in range(2)]
            rows = [pl.ds(pl.multiple_of(j * tq, tq), tq) for j, _ in tiles]
            ks = [k_ref[r, :] for r in rows]
            vs = [v_ref[r, :] for r in rows]
            terms = {(t, h): _sb_terms(qh[h], ks[t], tiles[t][1]) for t, h in chains}
            da = {(t, h): _dot(doh[h], vs[t], NT) for t, h in chains}
            cc = [carry[h][0] for h in range(2)]
            later = {}
            for t, h in chains:
                cc[h] = cc[h] + jnp.sum(terms[t, h][1], axis=1, keepdims=True)
                later[t, h] = tot[h] - cc[h]
            cum = {ch: _dot(terms[ch][1].astype(BF16), Urev, NN) for ch in chains}
            a, g, before = {}, {}, {}
            cg = [carry[h][1] for h in range(2)]
            for ch in chains:
                a[ch] = jnp.exp(terms[ch][0] + (cum[ch] + later[ch]))
                if tiles[ch[0]][1]:
                    a[ch] = jnp.where(terms[ch][2], a[ch], 0.0)
                g[ch] = a[ch] * da[ch]
                before[ch] = cg[ch[1]]
                cg[ch[1]] = cg[ch[1]] + jnp.sum(g[ch], axis=1, keepdims=True)
            G = {ch: _dot(g[ch].astype(BF16), Uexc, NN) for ch in chains}
            dz = {}
            for ch in chains:
                d = g[ch] - jnp.exp(terms[ch][0]) * (g[ch] + (G[ch] + before[ch]))
                if tiles[ch[0]][1]:
                    d = jnp.where(terms[ch][2], d, 0.0)
                dz[ch] = d.astype(BF16)
            dq = [carry[h][2] for h in range(2)]
            for t, h in chains:
                dq[h] = dq[h] + _dot(dz[t, h], ks[t], NN)
            for t in range(nt):
                dk_ref[:, rows[t]] += _dot(qth[0], dz[t, 0], NN) + _dot(qth[1], dz[t, 1], NN)
                dv_ref[:, rows[t]] += _dot(doth[0], a[t, 0].astype(BF16), NN) + _dot(doth[1], a[t, 1].astype(BF16), NN)
            return tuple((cc[h], cg[h], dq[h]) for h in range(2))

        zero = jnp.zeros((tq, 1), F32)
        zq = jnp.zeros((tq, LANES), F32)
        step = lambda n, cr: sweep([(SB_UNROLL_BWD * n + t, False) for t in range(SB_UNROLL_BWD)], cr)
        carry = lax.fori_loop(0, i // SB_UNROLL_BWD, step, ((zero, zero, zq), (zero, zero, zq)))
        tails = [lambda cr, k=k: sweep([(i - k + t, False) for t in range(k)] + [(i, True)], cr) for k in range(SB_UNROLL_BWD)]
        carry = _by_value(i % SB_UNROLL_BWD, tails, carry)
        dq_ref[...] = jnp.where(_lo_mask((tq, LANES)), carry[0][2], carry[1][2]) * SCALE
        if exchange is not None:
            @pl.when(_whole_grid(False, grid))
            def _():
                for cp in _direct_copies(x_ref, r_ref, sems, True):
                    cp.wait()

    spec_q = pl.BlockSpec((tq, LANES), lambda b, p, i: (b * nq + i, p))
    spec_t = pl.BlockSpec((LANES, tq), lambda b, p, i: (p, b * nq + i))
    spec_s = pl.BlockSpec((LANES, S), lambda b, p, i: (b * NP + p, 0))
    key_side = jax.ShapeDtypeStruct((B * NQ, S), F32)
    in_specs = [spec_q, pl.BlockSpec((S, LANES), lambda b, p, i: (b, NP + p)),
                pl.BlockSpec((S, LANES), lambda b, p, i: (b, 2 * NP + p)), spec_q, spec_t, spec_t, spec_q]
    out_specs = [spec_q, spec_s, spec_s]
    out_shape = [jax.ShapeDtypeStruct((T, NQ), F32), key_side, key_side]
    args = [qkv, qkv, qkv, do, q_t, do_t, tot]
    if exchange is not None:
        in_specs.append(HBM_SPEC)
        out_specs.append(HBM_SPEC)
        out_shape.append(jax.ShapeDtypeStruct(exchange.shape, exchange.dtype))
        args.append(exchange)
    return pl.pallas_call(
        body, name=name, grid=grid, in_specs=in_specs, out_specs=out_specs, out_shape=out_shape,
        scratch_shapes=[] if exchange is None else COMM_SEMS,
        compiler_params=_params("arbitrary", "arbitrary", "arbitrary"),
    )(*args)


def _adamw(w, g, m, v, name):
    shape = w.shape
    cols = shape[-1]
    rows = math.prod(shape[:-1])
    tr = _pick(rows, max(8, (1 << 19) // max(cols, LANES) // 8 * 8), 8)

    def body(w_ref, g_ref, m_ref, v_ref, d_ref, mo_ref, vo_ref):
        gv = g_ref[...]
        mn = ADAM_B1 * m_ref[...] + (1.0 - ADAM_B1) * gv
        vn = ADAM_B2 * v_ref[...] + (1.0 - ADAM_B2) * (gv * gv)
        m_hat = mn / (1.0 - ADAM_B1 ** ADAM_STEP)
        v_hat = vn / (1.0 - ADAM_B2 ** ADAM_STEP)
        d_ref[...] = -ADAM_LR * (m_hat / (jnp.sqrt(v_hat) + ADAM_EPS) + ADAM_WD * w_ref[...])
        mo_ref[...] = mn
        vo_ref[...] = vn

    spec = pl.BlockSpec((tr, cols), lambda i: (i, 0))
    out = jax.ShapeDtypeStruct((rows, cols), F32)
    d, mn, vn = pl.pallas_call(
        body, name=name, grid=(rows // tr,),
        in_specs=[spec] * 4, out_specs=[spec] * 3, out_shape=[out] * 3,
        compiler_params=_params("parallel"),
    )(w.reshape(rows, cols), g.reshape(rows, cols), m.reshape(rows, cols), v.reshape(rows, cols))
    return d.reshape(shape), mn.reshape(shape), vn.reshape(shape)


def _pad_rows(a, rows):
    return jnp.pad(a, ((0, rows - a.shape[0]), (0, 0)))


def kernel(x, c, positions, ada_w, ada_b, norm1_g, norm2_g, wqkv_a, q_norm_a, k_norm_a, sinks_a, wo_a, wqkv_b, wo_b, w_gate, w_up, w_down, loss_target, m_ada_w, m_ada_b, m_norm1_g, m_norm2_g, m_wqkv_a, m_q_norm_a, m_k_norm_a, m_sinks_a, m_wo_a, m_wqkv_b, m_wo_b, m_w_gate, m_w_up, m_w_down, v_ada_w, v_ada_b, v_norm1_g, v_norm2_g, v_wqkv_a, v_q_norm_a, v_k_norm_a, v_sinks_a, v_wo_a, v_wqkv_b, v_wo_b, v_w_gate, v_w_up, v_w_down):
    B, S, D = x.shape
    T = B * S
    L = ada_w.shape[0]
    NA, NB_ = wqkv_a.shape[0], wqkv_b.shape[0]
    me = 4 * lax.axis_index("x") + 2 * lax.axis_index("y") + lax.axis_index("c")
    xt = x.reshape(T, D)

    col_sharded = {"qkv_a": wqkv_a, "qkv_b": wqkv_b, "gate": w_gate, "up": w_up}
    row_sharded = {"wo_a": wo_a, "wo_b": wo_b, "down": w_down}

    def shard_rows(key):
        kind, idx = key
        return col_sharded[kind][idx].T if kind in col_sharded else row_sharded[kind][idx]

    def layer_keys(l):
        mix = "a" if l % 2 == 0 else "b"
        return [("qkv_" + mix, l // 2), ("wo_" + mix, l // 2), ("gate", l), ("up", l), ("down", l)]

    def unpack(buf, keys, reshape):
        out, off = {}, 0
        for key in keys:
            rows = shard_rows(key).shape[0]
            out[key] = reshape(buf[..., off:off + rows, :], rows)
            off += rows
        return out

    first_b = 1
    keys_early = layer_keys(0) + [("qkv_b", 0)]
    keys_late = [k for l in range(1, L) for k in layer_keys(l) if k != ("qkv_b", 0)]
    pack = lambda keys: jnp.concatenate([shard_rows(k).astype(BF16) for k in keys], axis=0)
    W = unpack(_all_gather(pack(keys_early), "ag_weights"), keys_early, lambda b, rows: b.reshape(NDEV * rows, D))

    WA = ada_w.shape[2]
    c_all = _all_gather(c, "ag_c").reshape(NDEV * B, D)
    bias = lax.dynamic_slice_in_dim(ada_b, me * WA, WA, axis=1).reshape(L, 1, WA)
    mod_part = _ada_fwd(c_all, ada_w, bias, "ada_fwd")
    mod_all = _all_gather(mod_part.reshape(L * NDEV * B, WA), "ag_mod")
    mod_all = mod_all.reshape(NDEV, L, NDEV * B, WA).transpose(1, 2, 0, 3).reshape(L, NDEV * B, NDEV * WA)
    mod = lax.dynamic_slice_in_dim(mod_all, me * B, B, axis=1)
    mod = mod.reshape(L, B, 6, 1, D)
    sh1, sc1, g1, sh2, sc2, g2 = [mod[:, :, k] for k in range(6)]

    half = ROT // 2
    inv_freq = jnp.power(jnp.float32(ROPE_THETA), -jnp.arange(half, dtype=F32) * 2.0 / ROT)
    ang = positions.reshape(T, 1).astype(F32) * inv_freq[None, :]
    cos, sin = jnp.cos(ang), jnp.sin(ang)
    ones = jnp.ones((T, HEAD - ROT), F32)
    zeros = jnp.zeros((T, HEAD - ROT), F32)
    z8 = jnp.zeros((T, half), F32)
    cs = jnp.tile(jnp.concatenate([cos, cos, ones], axis=1), (1, 2))
    s1 = jnp.tile(jnp.concatenate([-sin, z8, zeros], axis=1), (1, 2))
    s2 = jnp.tile(jnp.concatenate([z8, sin, zeros], axis=1), (1, 2))

    saved = []
    xc = xt
    for l in range(L):
        j = l // 2
        h1 = _norm_mod(xc, norm1_g[l:l + 1], sc1[l], sh1[l], S, f"norm1_{l}")
        sv = dict(x_in=xc, h1=h1)
        if l % 2 == 0:
            qkv = _mm_nt(h1, W["qkv_a", j], F32, f"qkv_a_{l}")
            qg = jnp.tile(q_norm_a[j:j + 1], (1, 2))
            kg = jnp.tile(k_norm_a[j:j + 1], (1, 2))
            qn, kd, vd = _qk_prep(qkv, cs, s1, s2, qg, kg, f"qk_prep_{l}")
            sink2 = jnp.repeat(sinks_a[j].reshape(-1, 2), HEAD, axis=1).reshape(-1, 1, LANES)
            attn, lse = _swa_fwd(qn, kd, vd, sink2, B, f"swa_fwd_{l}")
            sv.update(qkv=qkv, qg=qg, kg=kg, qn=qn, kd=kd, vd=vd, sink2=sink2, lse=lse)
            wo = W["wo_a", j]
        else:
            qkv = _mm_nt(h1, W["qkv_b", j], BF16, f"qkv_b_{l}")
            if l == first_b:
                attn, tot, late = _sb_fwd(qkv, B, f"sb_fwd_{l}", gather=pack(keys_late))
                W.update(unpack(late, keys_late, lambda b, rows: b.reshape(NDEV * rows, D)))
            else:
                attn, tot = _sb_fwd(qkv, B, f"sb_fwd_{l}")
            sv.update(qkv=qkv, tot=tot)
            wo = W["wo_b", j]
        y1, xm = _mm_res(attn, wo, xc, g1[l], S, f"attn_out_{l}")
        h2 = _norm_mod(xm, norm2_g[l:l + 1], sc2[l], sh2[l], S, f"norm2_{l}")
        gate, up, act = _swiglu_fwd(h2, W["gate", l], W["up", l], f"swiglu_fwd_{l}")
        y2, xc = _mm_res(act, W["down", l], xm, g2[l], S, f"mlp_out_{l}")
        sv.update(attn=attn, y1=y1, x_mid=xm, h2=h2, gate=gate, up=up, act=act, y2=y2)
        saved.append(sv)

    dx, loss_tile = _loss_head(xc, loss_target.reshape(T, D), "loss_head")

    G = {}
    pack_grads = lambda keys: jnp.concatenate([G[k].reshape(NDEV, G[k].shape[0] // NDEV, D) for k in keys], axis=1)
    keys_hi = [k for l in range(first_b + 1, L) for k in layer_keys(l)] + layer_keys(first_b)[1:]
    keys_lo = [k for l in range(first_b + 1) for k in layer_keys(l) if k not in keys_hi]
    received_hi = None
    dmod = [None] * L
    dn1, dn2 = [None] * L, [None] * L
    dqg, dkg, dsink = [None] * NA, [None] * NA, [None] * NA
    dy2, dg2 = _gate_bwd(dx, saved[L - 1]["y2"], g2[L - 1], S, "gate2_bwd_top")
    for l in reversed(range(L)):
        j = l // 2
        mix = "a" if l % 2 == 0 else "b"
        sv = saved[l]
        dgate, dup = _swiglu_bwd(dy2, W["down", l], sv["gate"], sv["up"], f"swiglu_bwd_{l}")
        G["down", l] = _mm_tn(sv["act"], dy2, f"dw_down_{l}")
        dh2 = _mm_nn([(dgate, W["gate", l]), (dup, W["up", l])], f"dh2_{l}")
        G["gate", l] = _mm_tn(dgate, sv["h2"], f"dw_gate_{l}")
        G["up", l] = _mm_tn(dup, sv["h2"], f"dw_up_{l}")
        dxm, dsh2, dsc2, dn2[l], dy1, dg1 = _norm_mod_bwd(sv["x_mid"], dh2, dx, norm2_g[l:l + 1], sc2[l], S, f"norm2_bwd_{l}",
                                                         below=(sv["y1"], g1[l]))
        dattn = _mm_nt(dy1, W["wo_" + mix, j], BF16, f"dattn_{l}")
        G["wo_" + mix, j] = _mm_tn(sv["attn"], dy1, f"dw_o_{l}")
        if l % 2 == 0:
            dq, dkc, dkp, dvc, dvp, dsink[j] = _swa_bwd(sv["qn"], sv["kd"], sv["vd"], sv["sink2"], dattn, sv["lse"], B,
                                                        f"swa_bwd_{l}")
            dqkv, dqg[j], dkg[j] = _qk_prep_bwd(sv["qkv"], cs, s1, s2, sv["qg"], sv["kg"], dq, dkc, dkp, dvc, dvp, B,
                                                f"qk_prep_bwd_{l}")
        else:
            nqb = sv["qkv"].shape[1] // 3
            sb_args = (sv["qkv"], sv["qkv"][:, :nqb].T, dattn, dattn.T, sv["tot"], B, f"sb_bwd_{l}")
            if l == first_b and keys_hi:
                dq, dk_t, dv_t, received_hi = _sb_bwd(*sb_args, exchange=pack_grads(keys_hi))
            else:
                dq, dk_t, dv_t = _sb_bwd(*sb_args)
            dk, dv = [t.reshape(B, nqb, S).transpose(0, 2, 1).reshape(T, nqb) for t in (dk_t, dv_t)]
            dqkv = jnp.concatenate([dq, dk, dv], axis=1).astype(BF16)
        dh1 = _mm_nn([(dqkv, W["qkv_" + mix, j])], f"dh1_{l}")
        G["qkv_" + mix, j] = _mm_tn(dqkv, sv["h1"], f"dw_qkv_{l}")
        n1_args = (sv["x_in"], dh1, dxm, norm1_g[l:l + 1], sc1[l], S, f"norm1_bwd_{l}")
        dmod_l = [None, None, dg1, dsh2, dsc2, dg2]
        if l > 0:
            dx, dmod_l[0], dmod_l[1], dn1[l], dy2, dg2 = _norm_mod_bwd(*n1_args, below=(saved[l - 1]["y2"], g2[l - 1]))
        else:
            dx, dmod_l[0], dmod_l[1], dn1[l] = _norm_mod_bwd(*n1_args)
        dmod[l] = jnp.concatenate(dmod_l, axis=1)
    grad_x = dx.reshape(B, S, D)

    ndm = L * 6
    dmod_rows = jnp.stack(dmod, axis=1).reshape(B * ndm, D)
    misc = jnp.concatenate(
        [jnp.concatenate(dn1, axis=0).reshape(B * L, D), jnp.concatenate(dn2, axis=0).reshape(B * L, D),
         _pad_rows(jnp.concatenate([jnp.pad(r, ((0, 0), (0, D - LANES))) for r in dqg + dkg]
                                   + [jnp.pad(r[:, 0, ::HEAD].reshape(1, -1), ((0, 0), (0, D - 2 * r.shape[0]))) for r in dsink]
                                   + [jnp.pad(loss_tile[0:1, 0:1], ((0, 0), (0, D - 1)))], axis=0), 8)], axis=0)
    nmisc = misc.shape[0]
    small = _all_gather(jnp.concatenate([dmod_rows, _pad_rows(misc, -(-nmisc // 8) * 8)], axis=0), "ag_small")
    dmod_all = small[:, :B * ndm].reshape(NDEV * B, ndm, D)
    g_ada_b = _sum_leading(dmod_all, "sum_dmod").reshape(L, 6 * D)
    misc_sum = _sum_leading(small[:, B * ndm:], "sum_misc")
    g_n1 = misc_sum[0:B * L].reshape(L, B, D)
    g_n2 = misc_sum[B * L:2 * B * L].reshape(L, B, D)
    g_norm1 = _sum_leading(g_n1.transpose(1, 0, 2), "sum_n1")
    g_norm2 = _sum_leading(g_n2.transpose(1, 0, 2), "sum_n2")
    o = 2 * B * L
    g_qn = misc_sum[o:o + NA, :HEAD]
    g_kn = misc_sum[o + NA:o + 2 * NA, :HEAD]
    nsink = sinks_a.shape[1]
    g_sink = misc_sum[o + 2 * NA:o + 3 * NA, :nsink]
    loss = misc_sum[o + 3 * NA, 0]

    dmod_loc = lax.dynamic_slice_in_dim(dmod_all.reshape(NDEV * B, L, 6 * D), me * WA, WA, axis=2)
    g_ada_w = _ada_bwd(c_all, dmod_loc.transpose(1, 0, 2), "ada_bwd")

    shard = unpack(_sum_leading(_exchange(pack_grads(keys_lo), "grad_exchange"), "grad_sum"), keys_lo, lambda b, rows: b)
    if received_hi is not None:
        shard.update(unpack(_sum_leading(received_hi, "grad_sum_hi"), keys_hi, lambda b, rows: b))

    def stacked(kind, n):
        return jnp.stack([shard[kind, i].T if kind in col_sharded else shard[kind, i] for i in range(n)])

    gw_qkv_a, gw_qkv_b, gw_gate, gw_up = stacked("qkv_a", NA), stacked("qkv_b", NB_), stacked("gate", L), stacked("up", L)
    gw_wo_a, gw_wo_b, gw_down = stacked("wo_a", NA), stacked("wo_b", NB_), stacked("down", L)

    grads = [g_ada_w, g_ada_b, g_norm1, g_norm2, gw_qkv_a, g_qn, g_kn, g_sink, gw_wo_a, gw_qkv_b, gw_wo_b,
             gw_gate, gw_up, gw_down]
    ws = [ada_w, ada_b, norm1_g, norm2_g, wqkv_a, q_norm_a, k_norm_a, sinks_a, wo_a, wqkv_b, wo_b, w_gate, w_up, w_down]
    ms = [m_ada_w, m_ada_b, m_norm1_g, m_norm2_g, m_wqkv_a, m_q_norm_a, m_k_norm_a, m_sinks_a, m_wo_a, m_wqkv_b,
          m_wo_b, m_w_gate, m_w_up, m_w_down]
    vs = [v_ada_w, v_ada_b, v_norm1_g, v_norm2_g, v_wqkv_a, v_q_norm_a, v_k_norm_a, v_sinks_a, v_wo_a, v_wqkv_b,
          v_wo_b, v_w_gate, v_w_up, v_w_down]
    deltas, new_m, new_v = [], [], []
    for k, (w, g, m, v) in enumerate(zip(ws, grads, ms, vs)):
        g = g.reshape(w.shape)
        d, mn, vn = _adamw(w, g, m, v, f"adamw_{k}")
        grads[k] = g
        deltas.append(d)
        new_m.append(mn)
        new_v.append(vn)
    return (loss, grad_x, *grads, *deltas, *new_m, *new_v)
```

```python
import functools
import math

import jax
import jax.numpy as jnp
from jax import lax
from jax.experimental import pallas as pl
from jax.experimental.pallas import tpu as pltpu

F32 = jnp.float32
BF16 = jnp.bfloat16
NDEV = 8
HEAD = 64
BLK = 128
LANES = 128
EPS = 1e-6
ROT = HEAD // 4
ROPE_THETA = 500000.0
SCALE = HEAD ** -0.5
NEG = -1e30
VMEM_LIMIT = 56 * 1024 * 1024
MESH = pl.DeviceIdType.MESH
HIGH = lax.Precision.HIGHEST

ADAM_LR = 0.001
ADAM_B1 = 0.9
ADAM_B2 = 0.999
ADAM_EPS = 1e-08
ADAM_WD = 0.01
ADAM_STEP = 10


def _params(*sem):
    return pltpu.CompilerParams(dimension_semantics=sem, vmem_limit_bytes=VMEM_LIMIT)


def _pick(n, cap, mult):
    if n <= cap:
        return n
    best = None
    for t in range(mult, cap + 1, mult):
        if n % t == 0:
            best = t
    assert best is not None, (n, cap, mult)
    return best


def _dot(a, b, dims, precision=None):
    return lax.dot_general(a, b, (dims, ((), ())), preferred_element_type=F32, precision=precision)


NN = ((1,), (0,))
NT = ((1,), (1,))
TN = ((0,), (0,))


def _all_gather(x, name):
    m, n = x.shape

    def body(x_ref, out_ref, send_sems, recv_sems, local_sem):
        ix, iy, ic = lax.axis_index("x"), lax.axis_index("y"), lax.axis_index("c")
        me, sibling = (ix, iy, ic), (ix, iy, 1 - ic)
        chips = [(1 - ix, iy), (ix, 1 - iy), (1 - ix, 1 - iy)]

        def slab(px, py, pc):
            return out_ref.at[4 * px + 2 * py + pc]

        def copy(k, block, to, src=None):
            return pltpu.make_async_remote_copy(
                src_ref=slab(*block) if src is None else src, dst_ref=slab(*block),
                send_sem=send_sems.at[k], recv_sem=recv_sems.at[k], device_id=to, device_id_type=MESH)

        mine = pltpu.make_async_copy(x_ref, slab(*me), local_sem)
        mine.start()
        first = [copy(0, me, sibling, src=x_ref)]
        first += [copy(1 + j, me, (*chip, ic), src=x_ref) for j, chip in enumerate(chips)]
        for cp in first:
            cp.start()
        passed = [copy(4 + j, (*chip, ic), sibling) for j, chip in enumerate(chips)]
        for j, chip in enumerate(chips):
            copy(1 + j, (*chip, ic), me).wait_recv()
            passed[j].start()
        copy(0, sibling, me).wait_recv()
        for j, chip in enumerate(chips):
            copy(4 + j, (*chip, 1 - ic), me).wait_recv()
        for cp in first + passed:
            cp.wait_send()
        mine.wait()

    return pl.pallas_call(
        body, name=name,
        out_shape=jax.ShapeDtypeStruct((NDEV, m, n), x.dtype),
        in_specs=[pl.BlockSpec(memory_space=pl.ANY)],
        out_specs=pl.BlockSpec(memory_space=pl.ANY),
        scratch_shapes=[pltpu.SemaphoreType.DMA((7,)), pltpu.SemaphoreType.DMA((7,)), pltpu.SemaphoreType.DMA(())],
    )(x)


COMM_SEMS = [pltpu.SemaphoreType.DMA((NDEV - 1,)), pltpu.SemaphoreType.DMA((NDEV - 1,)), pltpu.SemaphoreType.DMA(())]
HBM_SPEC = pl.BlockSpec(memory_space=pl.ANY)


def _direct_copies(src_ref, dst_ref, sems, scatter):
    send_sems, recv_sems, own_sem = sems
    ix, iy, ic = lax.axis_index("x"), lax.axis_index("y"), lax.axis_index("c")
    me = 4 * ix + 2 * iy + ic
    copies = [pltpu.make_async_copy(src_ref.at[me] if scatter else src_ref, dst_ref.at[me], own_sem)]
    for k in range(1, NDEV):
        px = 1 - ix if k & 4 else ix
        py = 1 - iy if k & 2 else iy
        pc = 1 - ic if k & 1 else ic
        copies.append(pltpu.make_async_remote_copy(
            src_ref=src_ref.at[4 * px + 2 * py + pc] if scatter else src_ref, dst_ref=dst_ref.at[me],
            send_sem=send_sems.at[k - 1], recv_sem=recv_sems.at[k - 1],
            device_id=(px, py, pc), device_id_type=MESH))
    return copies


def _exchange(p, name):
    def body(p_ref, r_ref, *sems):
        copies = _direct_copies(p_ref, r_ref, sems, True)
        for cp in copies:
            cp.start()
        for cp in copies:
            cp.wait()

    return pl.pallas_call(
        body, name=name,
        out_shape=jax.ShapeDtypeStruct(p.shape, p.dtype),
        in_specs=[HBM_SPEC], out_specs=HBM_SPEC, scratch_shapes=COMM_SEMS,
    )(p)


def _call_behind(body, name, grid, in_specs, out_specs, out_shape, args, payload=None, scatter=False):
    params = _params("arbitrary", "arbitrary", "arbitrary")
    if payload is None:
        return pl.pallas_call(body, name=name, grid=grid, in_specs=in_specs, out_specs=out_specs, out_shape=out_shape,
                              compiler_params=params)(*args)
    n_in, n_out = len(in_specs), len(out_specs)

    def edge(first):
        ids = [pl.program_id(a) for a in range(3)]
        return functools.reduce(lambda u, v: u & v, [i == (0 if first else d - 1) for i, d in zip(ids, grid)])

    def wrapped(*refs):
        x_ref, r_ref, sems = refs[n_in], refs[n_in + 1 + n_out], refs[n_in + n_out + 2:]

        @pl.when(edge(True))
        def _():
            for cp in _direct_copies(x_ref, r_ref, sems, scatter):
                cp.start()
        body(*refs[:n_in], *refs[n_in + 1:n_in + 1 + n_out])

        @pl.when(edge(False))
        def _():
            for cp in _direct_copies(x_ref, r_ref, sems, scatter):
                cp.wait()

    arrived = jax.ShapeDtypeStruct(payload.shape if scatter else (NDEV,) + payload.shape, payload.dtype)
    return pl.pallas_call(
        wrapped, name=name, grid=grid, in_specs=list(in_specs) + [HBM_SPEC], out_specs=list(out_specs) + [HBM_SPEC],
        out_shape=list(out_shape) + [arrived], scratch_shapes=COMM_SEMS, compiler_params=params,
    )(*args, payload)


def _sum_leading(r, name):
    k, m, n = r.shape
    mult = 8 * (4 // r.dtype.itemsize)
    tm = _pick(m, max(mult, (4 * 1024 * 1024) // (k * n * r.dtype.itemsize) // mult * mult), mult)

    def body(r_ref, o_ref):
        acc = r_ref[0].astype(F32)
        for s in range(1, k):
            acc = acc + r_ref[s].astype(F32)
        o_ref[...] = acc

    return pl.pallas_call(
        body, name=name, grid=(m // tm,),
        in_specs=[pl.BlockSpec((k, tm, n), lambda i: (0, i, 0))],
        out_specs=pl.BlockSpec((tm, n), lambda i: (i, 0)),
        out_shape=jax.ShapeDtypeStruct((m, n), F32),
        compiler_params=_params("parallel"),
    )(r)


def _mm_nt(a, bt, out_dtype, name):
    M, K = a.shape
    N = bt.shape[0]
    tm, tn = _pick(M, 512, 8), _pick(N, 1536, LANES)

    def body(a_ref, b_ref, o_ref):
        o_ref[...] = _dot(a_ref[...], b_ref[...], NT).astype(out_dtype)

    return pl.pallas_call(
        body, name=name, grid=(N // tn, M // tm),
        in_specs=[pl.BlockSpec((tm, K), lambda j, i: (i, 0)), pl.BlockSpec((tn, K), lambda j, i: (j, 0))],
        out_specs=pl.BlockSpec((tm, tn), lambda j, i: (i, j)),
        out_shape=jax.ShapeDtypeStruct((M, N), out_dtype),
        compiler_params=_params("parallel", "parallel"),
    )(a, bt)


def _mm_nn(pairs, name):
    M = pairs[0][0].shape[0]
    N = pairs[0][1].shape[1]
    tm, tn = _pick(M, 512, 8), _pick(N, 1024, LANES)
    np_ = len(pairs)

    def body(*refs):
        o_ref = refs[-1]
        acc = _dot(refs[0][...], refs[1][...], NN)
        for p in range(1, np_):
            acc = acc + _dot(refs[2 * p][...], refs[2 * p + 1][...], NN)
        o_ref[...] = acc

    in_specs, args = [], []
    for a, b in pairs:
        K = a.shape[1]
        in_specs += [pl.BlockSpec((tm, K), lambda i, j: (i, 0)), pl.BlockSpec((K, tn), lambda i, j: (0, j))]
        args += [a, b]
    return pl.pallas_call(
        body, name=name, grid=(M // tm, N // tn),
        in_specs=in_specs,
        out_specs=pl.BlockSpec((tm, tn), lambda i, j: (i, j)),
        out_shape=jax.ShapeDtypeStruct((M, N), F32),
        compiler_params=_params("parallel", "parallel"),
    )(*args)


def _mm_tn(a, b, name):
    M, N1 = a.shape
    N2 = b.shape[1]
    t1, tk = _pick(N1, 1536, LANES), _pick(M, 512, 8)
    nk = M // tk

    def body(a_ref, b_ref, o_ref, acc_ref):
        k = pl.program_id(1)

        @pl.when(k == 0)
        def _():
            acc_ref[...] = jnp.zeros_like(acc_ref)
        acc_ref[...] += _dot(a_ref[...], b_ref[...], TN)

        @pl.when(k == nk - 1)
        def _():
            o_ref[...] = acc_ref[...].astype(BF16)

    return pl.pallas_call(
        body, name=name, grid=(N1 // t1, nk),
        in_specs=[pl.BlockSpec((tk, t1), lambda i, k: (k, i)), pl.BlockSpec((tk, N2), lambda i, k: (k, 0))],
        out_specs=pl.BlockSpec((t1, N2), lambda i, k: (i, 0)),
        out_shape=jax.ShapeDtypeStruct((N1, N2), BF16),
        scratch_shapes=[pltpu.VMEM((t1, N2), F32)],
        compiler_params=_params("parallel", "arbitrary"),
    )(a, b)


def _mm_res(a, w, x, gate, S, name):
    T, K = a.shape
    D = w.shape[1]
    tm, tn = _pick(S, 512, 8), _pick(D, 1024, LANES)
    nb = S // tm

    def body(a_ref, w_ref, x_ref, g_ref, y_ref, o_ref):
        y = _dot(a_ref[...], w_ref[...], NN)
        y_ref[...] = y.astype(BF16)
        o_ref[...] = x_ref[...] + g_ref[0] * y

    return pl.pallas_call(
        body, name=name, grid=(T // tm, D // tn),
        in_specs=[pl.BlockSpec((tm, K), lambda i, j: (i, 0)), pl.BlockSpec((K, tn), lambda i, j: (0, j)),
                  pl.BlockSpec((tm, tn), lambda i, j: (i, j)), pl.BlockSpec((1, 1, tn), lambda i, j: (i // nb, 0, j))],
        out_specs=[pl.BlockSpec((tm, tn), lambda i, j: (i, j)), pl.BlockSpec((tm, tn), lambda i, j: (i, j))],
        out_shape=[jax.ShapeDtypeStruct((T, D), BF16), jax.ShapeDtypeStruct((T, D), F32)],
        compiler_params=_params("parallel", "parallel"),
    )(a, w, x, gate)


def _swiglu_fwd(h, wgt, wut, name):
    T, D = h.shape
    F = wgt.shape[0]
    tm, tn = _pick(T, 512, 8), _pick(F, 1536, LANES)

    def body(h_ref, g_ref, u_ref, go_ref, uo_ref, a_ref):
        hh = h_ref[...]
        g = _dot(hh, g_ref[...], NT)
        u = _dot(hh, u_ref[...], NT)
        go_ref[...] = g.astype(BF16)
        uo_ref[...] = u.astype(BF16)
        a_ref[...] = (g * jax.nn.sigmoid(g) * u).astype(BF16)

    spec_w = pl.BlockSpec((tn, D), lambda j, i: (j, 0))
    spec_o = pl.BlockSpec((tm, tn), lambda j, i: (i, j))
    out = jax.ShapeDtypeStruct((T, F), BF16)
    return pl.pallas_call(
        body, name=name, grid=(F // tn, T // tm),
        in_specs=[pl.BlockSpec((tm, D), lambda j, i: (i, 0)), spec_w, spec_w],
        out_specs=[spec_o, spec_o, spec_o],
        out_shape=[out, out, out],
        compiler_params=_params("parallel", "parallel"),
    )(h, wgt, wut)


def _swiglu_bwd(dy, wd, gate, up, name):
    T, D = dy.shape
    F = wd.shape[0]
    tm, tn = _pick(T, 512, 8), _pick(F, 1536, LANES)

    def body(dy_ref, w_ref, g_ref, u_ref, dg_ref, du_ref):
        da = _dot(dy_ref[...], w_ref[...], NT)
        g = g_ref[...].astype(F32)
        sg = jax.nn.sigmoid(g)
        silu = g * sg
        du_ref[...] = (da * silu).astype(BF16)
        dg_ref[...] = (da * u_ref[...].astype(F32) * (sg + silu * (1.0 - sg))).astype(BF16)

    spec_o = pl.BlockSpec((tm, tn), lambda j, i: (i, j))
    return pl.pallas_call(
        body, name=name, grid=(F // tn, T // tm),
        in_specs=[pl.BlockSpec((tm, D), lambda j, i: (i, 0)), pl.BlockSpec((tn, D), lambda j, i: (j, 0)), spec_o, spec_o],
        out_specs=[spec_o, spec_o],
        out_shape=[jax.ShapeDtypeStruct((T, F), BF16), jax.ShapeDtypeStruct((T, F), BF16)],
        compiler_params=_params("parallel", "parallel"),
    )(dy, wd, gate, up)


def _norm_mod(x, gain, sc, sh, S, name):
    T, D = x.shape
    tm = _pick(S, 512, 8)
    nb = S // tm

    def body(x_ref, g_ref, sc_ref, sh_ref, o_ref):
        xv = x_ref[...]
        r = lax.rsqrt(jnp.mean(xv * xv, axis=-1, keepdims=True) + EPS)
        o_ref[...] = ((xv * r) * g_ref[...] * (1.0 + sc_ref[0]) + sh_ref[0]).astype(BF16)

    spec_b = pl.BlockSpec((1, 1, D), lambda i: (i // nb, 0, 0))
    return pl.pallas_call(
        body, name=name, grid=(T // tm,),
        in_specs=[pl.BlockSpec((tm, D), lambda i: (i, 0)), pl.BlockSpec((1, D), lambda i: (0, 0)), spec_b, spec_b],
        out_specs=pl.BlockSpec((tm, D), lambda i: (i, 0)),
        out_shape=jax.ShapeDtypeStruct((T, D), BF16),
        compiler_params=_params("parallel"),
    )(x, gain, sc, sh)


def _norm_mod_bwd(x, dh, dres, gain, sc, S, name, below=None):
    T, D = x.shape
    B = T // S
    tm = _pick(S, 256, 8)
    nb = S // tm

    def body(x_ref, dh_ref, dr_ref, g_ref, sc_ref, *rest):
        if below is None:
            o_ref, dsh_ref, dsc_ref, dg_ref = rest
            sums = [dsh_ref, dsc_ref, dg_ref]
        else:
            y_ref, gt_ref, o_ref, dsh_ref, dsc_ref, dg_ref, dy_ref, dgt_ref = rest
            sums = [dsh_ref, dsc_ref, dg_ref, dgt_ref]

        @pl.when(pl.program_id(1) == 0)
        def _():
            for ref in sums:
                ref[...] = jnp.zeros_like(ref)
        xv, dhv, g = x_ref[...], dh_ref[...], g_ref[...]
        r = lax.rsqrt(jnp.mean(xv * xv, axis=-1, keepdims=True) + EPS)
        xhat = xv * r
        dsh_ref[0] += jnp.sum(dhv, axis=0, keepdims=True)
        dsc_ref[0] += jnp.sum(dhv * (xhat * g), axis=0, keepdims=True)
        dn = dhv * (1.0 + sc_ref[0])
        dg_ref[0] += jnp.sum(dn * xhat, axis=0, keepdims=True)
        dxh = dn * g
        out = dr_ref[...] + r * (dxh - xhat * jnp.mean(dxh * xhat, axis=-1, keepdims=True))
        o_ref[...] = out
        if below is not None:
            dy_ref[...] = (out * gt_ref[0]).astype(BF16)
            dgt_ref[0] += jnp.sum(out * y_ref[...].astype(F32), axis=0, keepdims=True)

    spec_t = pl.BlockSpec((tm, D), lambda b, i: (b * nb + i, 0))
    spec_b = pl.BlockSpec((1, 1, D), lambda b, i: (b, 0, 0))
    red = jax.ShapeDtypeStruct((B, 1, D), F32)
    in_specs = [spec_t, spec_t, spec_t, pl.BlockSpec((1, D), lambda b, i: (0, 0)), spec_b]
    out_specs = [spec_t, spec_b, spec_b, spec_b]
    out_shape = [jax.ShapeDtypeStruct((T, D), F32), red, red, red]
    args = [x, dh, dres, gain, sc]
    if below is not None:
        in_specs += [spec_t, spec_b]
        out_specs += [spec_t, spec_b]
        out_shape += [jax.ShapeDtypeStruct((T, D), BF16), red]
        args += list(below)
    return pl.pallas_call(
        body, name=name, grid=(B, nb), in_specs=in_specs, out_specs=out_specs, out_shape=out_shape,
        compiler_params=_params("parallel", "arbitrary"),
    )(*args)


def _gate_bwd(dx, y, gate, S, name):
    T, D = dx.shape
    B = T // S
    tm = _pick(S, 512, 8)
    nb = S // tm

    def body(dx_ref, y_ref, g_ref, dy_ref, dg_ref):
        @pl.when(pl.program_id(1) == 0)
        def _():
            dg_ref[...] = jnp.zeros_like(dg_ref)
        d = dx_ref[...]
        dy_ref[...] = (d * g_ref[0]).astype(BF16)
        dg_ref[0] += jnp.sum(d * y_ref[...].astype(F32), axis=0, keepdims=True)

    spec_t = pl.BlockSpec((tm, D), lambda b, i: (b * nb + i, 0))
    spec_b = pl.BlockSpec((1, 1, D), lambda b, i: (b, 0, 0))
    return pl.pallas_call(
        body, name=name, grid=(B, nb),
        in_specs=[spec_t, spec_t, spec_b],
        out_specs=[spec_t, spec_b],
        out_shape=[jax.ShapeDtypeStruct((T, D), BF16), jax.ShapeDtypeStruct((B, 1, D), F32)],
        compiler_params=_params("parallel", "arbitrary"),
    )(dx, y, gate)


def _loss_head(y, target, name):
    T, D = y.shape
    tm = _pick(T, 512, 8)

    def body(y_ref, t_ref, dy_ref, l_ref):
        @pl.when(pl.program_id(0) == 0)
        def _():
            l_ref[...] = jnp.zeros_like(l_ref)
        e = y_ref[...] - t_ref[...]
        dy_ref[...] = e * (1.0 / D)
        l_ref[...] += 0.5 * jnp.sum(jnp.mean(e * e, axis=-1, keepdims=True), axis=0, keepdims=True)

    spec = pl.BlockSpec((tm, D), lambda i: (i, 0))
    return pl.pallas_call(
        body, name=name, grid=(T // tm,),
        in_specs=[spec, spec],
        out_specs=[spec, pl.BlockSpec((8, LANES), lambda i: (0, 0))],
        out_shape=[jax.ShapeDtypeStruct((T, D), F32), jax.ShapeDtypeStruct((8, LANES), F32)],
        compiler_params=_params("arbitrary"),
    )(y, target)


def _ada_fwd(c_all, ada_w, bias, name):
    NB, D = c_all.shape
    L, _, W = ada_w.shape

    def body(c_ref, w_ref, b_ref, o_ref):
        cv = c_ref[...]
        cond = cv * jax.nn.sigmoid(cv)
        o_ref[0] = _dot(cond, w_ref[0], NN, HIGH) + b_ref[0]

    return pl.pallas_call(
        body, name=name, grid=(L,),
        in_specs=[pl.BlockSpec((NB, D), lambda l: (0, 0)), pl.BlockSpec((1, D, W), lambda l: (l, 0, 0)),
                  pl.BlockSpec((1, 1, W), lambda l: (l, 0, 0))],
        out_specs=pl.BlockSpec((1, NB, W), lambda l: (l, 0, 0)),
        out_shape=jax.ShapeDtypeStruct((L, NB, W), F32),
        compiler_params=_params("parallel"),
    )(c_all, ada_w, bias)


def _ada_bwd(c_all, dmod, name):
    NB, D = c_all.shape
    L, _, W = dmod.shape

    def body(c_ref, d_ref, o_ref):
        cv = c_ref[...]
        cond = cv * jax.nn.sigmoid(cv)
        o_ref[0] = _dot(cond, d_ref[0], TN, HIGH)

    return pl.pallas_call(
        body, name=name, grid=(L,),
        in_specs=[pl.BlockSpec((NB, D), lambda l: (0, 0)), pl.BlockSpec((1, NB, W), lambda l: (l, 0, 0))],
        out_specs=pl.BlockSpec((1, D, W), lambda l: (l, 0, 0)),
        out_shape=jax.ShapeDtypeStruct((L, D, W), F32),
        compiler_params=_params("parallel"),
    )(c_all, dmod)


def _lo_mask(shape):
    return lax.broadcasted_iota(jnp.int32, shape, len(shape) - 1) < HEAD


def _head_sum_matrix():
    r = lax.broadcasted_iota(jnp.int32, (LANES, LANES), 0) // HEAD
    c = lax.broadcasted_iota(jnp.int32, (LANES, LANES), 1) // HEAD
    return (r == c).astype(BF16)


def _head_sum(x, P):
    hi = x.astype(BF16)
    lo = (x - hi.astype(F32)).astype(BF16)
    return _dot(hi, P, NN) + _dot(lo, P, NN)


def _rope(y, cs, s1, s2):
    return y * cs + pltpu.roll(y, LANES - ROT // 2, 1) * s1 + pltpu.roll(y, ROT // 2, 1) * s2


def _rope_bwd(d, cs, s1, s2):
    return d * cs + pltpu.roll(d * s1, ROT // 2, 1) + pltpu.roll(d * s2, LANES - ROT // 2, 1)


def _qk_prep(qkv, cs, s1, s2, qg, kg, name):
    T, W = qkv.shape
    NQ = W - 2 * LANES
    tm = _pick(T, 512, 8)

    def body(x_ref, cs_ref, s1_ref, s2_ref, qg_ref, kg_ref, q_ref, k_ref, v_ref):
        P = _head_sum_matrix()
        cs_, s1_, s2_ = cs_ref[...], s1_ref[...], s2_ref[...]
        lo = _lo_mask((tm, LANES))

        def norm_rope(xv, g):
            ms = _head_sum(xv * xv, P) * (1.0 / HEAD)
            return _rope(xv * lax.rsqrt(ms + EPS) * g, cs_, s1_, s2_)

        for j in range(NQ // LANES):
            q_ref[:, j * LANES:(j + 1) * LANES] = norm_rope(x_ref[:, j * LANES:(j + 1) * LANES], qg_ref[...]).astype(BF16)
        kr = norm_rope(x_ref[:, NQ:NQ + LANES], kg_ref[...])
        ks = pltpu.roll(kr, HEAD, 1)
        k_ref[:, :LANES] = jnp.where(lo, kr, ks).astype(BF16)
        k_ref[:, LANES:] = jnp.where(lo, ks, kr).astype(BF16)
        vr = x_ref[:, NQ + LANES:]
        vs = pltpu.roll(vr, HEAD, 1)
        v_ref[:, :LANES] = jnp.where(lo, vr, vs).astype(BF16)
        v_ref[:, LANES:] = jnp.where(lo, vs, vr).astype(BF16)

    spec_t = pl.BlockSpec((tm, LANES), lambda i: (i, 0))
    spec_g = pl.BlockSpec((1, LANES), lambda i: (0, 0))
    return pl.pallas_call(
        body, name=name, grid=(T // tm,),
        in_specs=[pl.BlockSpec((tm, W), lambda i: (i, 0)), spec_t, spec_t, spec_t, spec_g, spec_g],
        out_specs=[pl.BlockSpec((tm, NQ), lambda i: (i, 0)), pl.BlockSpec((tm, 2 * LANES), lambda i: (i, 0)),
                   pl.BlockSpec((tm, 2 * LANES), lambda i: (i, 0))],
        out_shape=[jax.ShapeDtypeStruct((T, NQ), BF16), jax.ShapeDtypeStruct((T, 2 * LANES), BF16),
                   jax.ShapeDtypeStruct((T, 2 * LANES), BF16)],
        compiler_params=_params("parallel"),
    )(qkv, cs, s1, s2, qg, kg)


def _stack_heads(x2):
    lo = _lo_mask(x2.shape)
    z = jnp.zeros_like(x2)
    return jnp.concatenate([jnp.where(lo, x2, z), jnp.where(lo, z, x2)], axis=0)


def _unstack_heads(xs):
    r = xs.shape[0] // 2
    return jnp.where(_lo_mask((r, LANES)), xs[:r], xs[r:])


def _swa_valid(i):
    qo = lax.broadcasted_iota(jnp.int32, (2 * BLK, 2 * BLK), 0) % BLK
    kc_ = lax.broadcasted_iota(jnp.int32, (2 * BLK, 2 * BLK), 1)
    rel = qo + BLK - kc_
    return (rel >= 0) & (rel < BLK) & ((kc_ >= BLK) | (i > 0))


def _swa_scores(q2, kk, sink2, valid):
    qs = _stack_heads(q2) * SCALE
    s = _dot(qs, kk, NT)
    sk = jnp.concatenate([jnp.broadcast_to(sink2[:, 0:1], (BLK, 1)), jnp.broadcast_to(sink2[:, HEAD:HEAD + 1], (BLK, 1))], axis=0)
    return qs, jnp.where(valid, s, NEG), sk


def _swa_fwd(q, kd, vd, sink2, B, name, gather=None):
    T, NQ = q.shape
    NP = NQ // LANES
    nq = T // B // BLK
    NG = kd.shape[1] // LANES
    grp = NP // NG

    def body(q_ref, kp_ref, kc_ref, vp_ref, vc_ref, s_ref, o_ref, l_ref):
        valid = _swa_valid(pl.program_id(2))
        kk = jnp.concatenate([kp_ref[...], kc_ref[...]], axis=0)
        vs = _stack_heads(jnp.concatenate([vp_ref[...], vc_ref[...]], axis=0))
        sls = [slice(jj * LANES, (jj + 1) * LANES) for jj in range(grp)]
        sc = [_swa_scores(q_ref[:, sl], kk, s_ref[jj], valid) for jj, sl in enumerate(sls)]
        ms = [jnp.maximum(jnp.max(s, axis=1, keepdims=True), sk) for _, s, sk in sc]
        ps = [jnp.where(valid, jnp.exp(s - m), 0.0) for (_, s, _), m in zip(sc, ms)]
        ls = [jnp.sum(p, axis=1, keepdims=True) + jnp.exp(sk - m) for p, (_, _, sk), m in zip(ps, sc, ms)]
        ps = [(p * (1.0 / l)).astype(BF16) for p, l in zip(ps, ls)]
        os_ = [_dot(jnp.concatenate([p[:BLK], p[BLK:]], axis=1), vs, NN) for p in ps]
        for sl, o, m, l in zip(sls, os_, ms, ls):
            o_ref[:, sl] = o.astype(BF16)
            l_ref[:, sl] = _unstack_heads(jnp.broadcast_to(m + jnp.log(l), (2 * BLK, LANES)))

    spec_q = pl.BlockSpec((BLK, grp * LANES), lambda b, g, i: (b * nq + i, g))
    spec_p = pl.BlockSpec((BLK, LANES), lambda b, g, i: (b * nq + jnp.maximum(i - 1, 0), g))
    spec_c = pl.BlockSpec((BLK, LANES), lambda b, g, i: (b * nq + i, g))
    in_specs = [spec_q, spec_p, spec_c, spec_p, spec_c, pl.BlockSpec((grp, 1, LANES), lambda b, g, i: (g, 0, 0))]
    out_shape = [jax.ShapeDtypeStruct((T, NQ), BF16), jax.ShapeDtypeStruct((T, NQ), F32)]
    return _call_behind(body, name, (B, NG, nq), in_specs, [spec_q, spec_q], out_shape, [q, kd, kd, vd, vd, sink2], gather, False)


def _swa_bwd(q, kd, vd, sink2, do, lse, B, name, exchange=None):
    T, NQ = q.shape
    NP = NQ // LANES
    nq = T // B // BLK
    NG = kd.shape[1] // LANES
    grp = NP // NG

    def body(q_ref, kp_ref, kc_ref, vp_ref, vc_ref, s_ref, do_ref, l_ref,
             dq_ref, dkc_ref, dkp_ref, dvc_ref, dvp_ref, ds_ref):
        b, i = pl.program_id(1), pl.program_id(2)

        @pl.when((b == 0) & (i == 0))
        def _():
            ds_ref[...] = jnp.zeros_like(ds_ref)
        valid = _swa_valid(i)
        kk = jnp.concatenate([kp_ref[...], kc_ref[...]], axis=0)
        vv = jnp.concatenate([vp_ref[...], vc_ref[...]], axis=0)
        sls = [slice(jj * LANES, (jj + 1) * LANES) for jj in range(grp)]
        sc = [_swa_scores(q_ref[:, sl], kk, s_ref[jj], valid) for jj, sl in enumerate(sls)]
        dos = [_stack_heads(do_ref[:, sl]) for sl in sls]
        dps = [_dot(d, vv, NT) for d in dos]
        lses = [jnp.concatenate([l_ref[:, sl][:, 0:1], l_ref[:, sl][:, HEAD:HEAD + 1]], axis=0) for sl in sls]
        ps = [jnp.where(valid, jnp.exp(s - lse), 0.0) for (_, s, _), lse in zip(sc, lses)]
        deltas = [jnp.sum(p * dp, axis=1, keepdims=True) for p, dp in zip(ps, dps)]
        dscs = [(p * (dp - delta)).astype(BF16) for p, dp, delta in zip(ps, dps, deltas)]
        dqs = [_dot(dsc, kk, NN) for dsc in dscs]
        dk = jnp.zeros((2 * BLK, LANES), F32)
        dv = jnp.zeros((2 * BLK, LANES), F32)
        for jj, sl in enumerate(sls):
            dsk = -jnp.exp(sc[jj][2] - lses[jj]) * deltas[jj]
            dsk_lo = jnp.sum(dsk[:BLK], axis=0, keepdims=True)
            dsk_hi = jnp.sum(dsk[BLK:], axis=0, keepdims=True)
            ds_ref[jj] += jnp.where(_lo_mask((1, LANES)), dsk_lo, dsk_hi)
            dq_ref[:, sl] = _unstack_heads(dqs[jj]) * SCALE
            dk = dk + _dot(dscs[jj], sc[jj][0], TN)
            dv = dv + _dot(ps[jj].astype(BF16), dos[jj], TN)
        dkp_ref[...] = dk[:BLK]
        dkc_ref[...] = dk[BLK:]
        dvp_ref[...] = dv[:BLK]
        dvc_ref[...] = dv[BLK:]

    spec_q = pl.BlockSpec((BLK, grp * LANES), lambda g, b, i: (b * nq + i, g))
    spec_p = pl.BlockSpec((BLK, LANES), lambda g, b, i: (b * nq + jnp.maximum(i - 1, 0), g))
    spec_c = pl.BlockSpec((BLK, LANES), lambda g, b, i: (b * nq + i, g))
    spec_s = pl.BlockSpec((grp, 1, LANES), lambda g, b, i: (g, 0, 0))
    kv = jax.ShapeDtypeStruct((T, NG * LANES), F32)
    in_specs = [spec_q, spec_p, spec_c, spec_p, spec_c, spec_s, spec_q, spec_q]
    out_specs = [spec_q, spec_c, spec_c, spec_c, spec_c, spec_s]
    out_shape = [jax.ShapeDtypeStruct((T, NQ), F32), kv, kv, kv, kv, jax.ShapeDtypeStruct((NP, 1, LANES), F32)]
    return _call_behind(body, name, (NG, B, nq), in_specs, out_specs, out_shape, [q, kd, kd, vd, vd, sink2, do, lse],
                        exchange, True)


def _qk_prep_bwd(qkv, cs, s1, s2, qg, kg, dq, dkc, dkp, dvc, dvp, B, name):
    T, W = qkv.shape
    NQ = W - 2 * LANES
    NP = NQ // LANES
    nq = T // B // BLK

    def body(x_ref, cs_ref, s1_ref, s2_ref, qg_ref, kg_ref, dq_ref, dkc_ref, dkp_ref, dvc_ref, dvp_ref,
             o_ref, dqg_ref, dkg_ref):
        b, i = pl.program_id(0), pl.program_id(1)

        @pl.when((b == 0) & (i == 0))
        def _():
            dqg_ref[...] = jnp.zeros_like(dqg_ref)
            dkg_ref[...] = jnp.zeros_like(dkg_ref)
        P = _head_sum_matrix()
        cs_, s1_, s2_ = cs_ref[...], s1_ref[...], s2_ref[...]
        lo = _lo_mask((BLK, LANES))
        has_next = (i + 1 < nq).astype(F32)

        def norm_rope_bwd(xv, g, d):
            du = _rope_bwd(d, cs_, s1_, s2_)
            r = lax.rsqrt(_head_sum(xv * xv, P) * (1.0 / HEAD) + EPS)
            xhat = xv * r
            dgain = jnp.sum(du * xhat, axis=0, keepdims=True)
            uu = du * g
            dx = r * (uu - xhat * (_head_sum(uu * xhat, P) * (1.0 / HEAD)))
            return dx, dgain + pltpu.roll(dgain, HEAD, 1)

        dqg = jnp.zeros((1, LANES), F32)
        for j in range(NP):
            sl = slice(j * LANES, (j + 1) * LANES)
            dx, dg = norm_rope_bwd(x_ref[:, sl], qg_ref[...], dq_ref[:, sl])
            o_ref[:, sl] = dx.astype(BF16)
            dqg = dqg + dg
        dqg_ref[...] += dqg

        def fold(c_ref, p_ref, g):
            sl = slice(g * LANES, (g + 1) * LANES)
            t = c_ref[:, sl] + has_next * p_ref[:, sl]
            return t + pltpu.roll(t, HEAD, 1)

        dk = jnp.where(lo, fold(dkc_ref, dkp_ref, 0), fold(dkc_ref, dkp_ref, 1))
        dx, dg = norm_rope_bwd(x_ref[:, NQ:NQ + LANES], kg_ref[...], dk)
        o_ref[:, NQ:NQ + LANES] = dx.astype(BF16)
        dkg_ref[...] += dg
        dv = jnp.where(lo, fold(dvc_ref, dvp_ref, 0), fold(dvc_ref, dvp_ref, 1))
        o_ref[:, NQ + LANES:] = dv.astype(BF16)

    spec_t = pl.BlockSpec((BLK, LANES), lambda b, i: (b * nq + i, 0))
    spec_g = pl.BlockSpec((1, LANES), lambda b, i: (0, 0))
    spec_c = pl.BlockSpec((BLK, 2 * LANES), lambda b, i: (b * nq + i, 0))
    spec_n = pl.BlockSpec((BLK, 2 * LANES), lambda b, i: (b * nq + jnp.minimum(i + 1, nq - 1), 0))
    row = jax.ShapeDtypeStruct((1, LANES), F32)
    return pl.pallas_call(
        body, name=name, grid=(B, nq),
        in_specs=[pl.BlockSpec((BLK, W), lambda b, i: (b * nq + i, 0)), spec_t, spec_t, spec_t, spec_g, spec_g,
                  pl.BlockSpec((BLK, NQ), lambda b, i: (b * nq + i, 0)), spec_c, spec_n, spec_c, spec_n],
        out_specs=[pl.BlockSpec((BLK, W), lambda b, i: (b * nq + i, 0)), spec_g, spec_g],
        out_shape=[jax.ShapeDtypeStruct((T, W), BF16), row, row],
        compiler_params=_params("arbitrary", "arbitrary"),
    )(qkv, cs, s1, s2, qg, kg, dq, dkc, dkp, dvc, dvp)


SB_TILE = 256
SB_UNROLL = 4
SB_UNROLL_BWD = 2


def _split_heads(x2, scale=None):
    lo = _lo_mask(x2.shape)
    z = jnp.zeros_like(x2)
    if scale is not None:
        x2 = x2 * scale
    return jnp.where(lo, x2, z), jnp.where(lo, z, x2)


def _sb_terms(qh, kj, diagonal):
    z = _dot(qh, kj, NT)
    e = jnp.exp(-jnp.abs(z))
    lb = jnp.minimum(z, 0.0) - jnp.log(1.0 + e)
    L = lb - z
    if not diagonal:
        return lb, L, None, z, e
    strict = lax.broadcasted_iota(jnp.int32, z.shape, 1) < lax.broadcasted_iota(jnp.int32, z.shape, 0)
    return lb, jnp.where(strict, L, 0.0), strict, z, e


def _tri(n, cmp):
    r = lax.broadcasted_iota(jnp.int32, (n, n), 0)
    c = lax.broadcasted_iota(jnp.int32, (n, n), 1)
    return cmp(r, c).astype(BF16)


def _by_value(r, fns, carry):
    if len(fns) == 1:
        return fns[0](carry)
    half = len(fns) // 2
    return lax.cond(r < half, lambda cr: _by_value(r, fns[:half], cr), lambda cr: _by_value(r - half, fns[half:], cr), carry)


def _sb_fwd(qkv, B, name, gather=None):
    T, W = qkv.shape
    NQ = W // 3
    NP = NQ // LANES
    S = T // B
    tq = min(SB_TILE, S)
    nq = S // tq
    grid = (B, NP, nq)

    def body(q_ref, k_ref, v_ref, o_ref, t_ref):
        i = pl.program_id(2)
        qh = _split_heads(q_ref[...], SCALE)
        U = _tri(tq, lambda r, c: r > c)

        def sweep(tiles, cs, acc):
            chains = [(t, h) for t in range(len(tiles)) for h in range(2)]
            rows = [pl.ds(pl.multiple_of(j * tq, tq), tq) for j, _ in tiles]
            ks = [k_ref[r, :] for r in rows]
            vs = [_split_heads(v_ref[r, :]) for r in rows]
            terms = {(t, h): _sb_terms(qh[h], ks[t], tiles[t][1]) for t, h in chains}
            carry = {}
            for h in range(2):
                c = cs[h]
                for t in range(len(tiles)):
                    carry[t, h] = c
                    c = c + jnp.sum(terms[t, h][1], axis=1, keepdims=True)
                cs = cs[:h] + (c,) + cs[h + 1:]
            cum = {ch: _dot(terms[ch][1].astype(BF16), U, NN) for ch in chains}
            for ch in chains:
                a = jnp.exp(terms[ch][0] + (cum[ch] + carry[ch]))
                if tiles[ch[0]][1]:
                    a = jnp.where(terms[ch][2], a, 0.0)
                acc = acc + _dot(a.astype(BF16), vs[ch[0]][ch[1]], NN)
            return cs, acc

        zero = jnp.zeros((tq, 1), F32)
        rem = i % SB_UNROLL
        heads = [lambda cr, k=k: sweep([(i, True)] + [(i - 1 - t, False) for t in range(k)], *cr) for k in range(SB_UNROLL)]
        carry = _by_value(rem, heads, ((zero, zero), jnp.zeros((tq, LANES), F32)))
        step = lambda n, cr: sweep([(i - 1 - rem - SB_UNROLL * n - t, False) for t in range(SB_UNROLL)], *cr)
        cs, acc = lax.fori_loop(0, i // SB_UNROLL, step, carry)
        o_ref[...] = acc.astype(BF16)
        t_ref[...] = jnp.where(_lo_mask((tq, LANES)), cs[0], cs[1])

    spec_q = pl.BlockSpec((tq, LANES), lambda b, p, i: (b * nq + i, p))
    in_specs = [spec_q, pl.BlockSpec((S, LANES), lambda b, p, i: (b, NP + p)),
                pl.BlockSpec((S, LANES), lambda b, p, i: (b, 2 * NP + p))]
    out_shape = [jax.ShapeDtypeStruct((T, NQ), BF16), jax.ShapeDtypeStruct((T, NQ), F32)]
    return _call_behind(body, name, grid, in_specs, [spec_q, spec_q], out_shape, [qkv, qkv, qkv], gather, False)


def _sb_bwd(qkv, q_t, do, do_t, tot, B, name, exchange=None):
    T, W = qkv.shape
    NQ = W // 3
    NP = NQ // LANES
    S = T // B
    tq = min(SB_TILE, S)
    nq = S // tq
    grid = (B, NP, nq)

    def body(q_ref, k_ref, v_ref, do_ref, qt_ref, dot_ref, t_ref, dq_ref, dk_ref, dv_ref):
        i = pl.program_id(2)

        @pl.when(i == 0)
        def _():
            dk_ref[...] = jnp.zeros_like(dk_ref)
            dv_ref[...] = jnp.zeros_like(dv_ref)
        qh = _split_heads(q_ref[...], SCALE)
        doh = _split_heads(do_ref[...])
        top = lax.broadcasted_iota(jnp.int32, (LANES, tq), 0) < HEAD
        zt = jnp.zeros((LANES, tq), BF16)
        qt = qt_ref[...] * SCALE
        qth = (jnp.where(top, qt, zt), jnp.where(top, zt, qt))
        doth = (jnp.where(top, dot_ref[...], zt), jnp.where(top, zt, dot_ref[...]))
        tt = t_ref[...]
        tot = (tt[:, 0:1], tt[:, HEAD:HEAD + 1])
        Urev = _tri(tq, lambda r, c: r > c)
        Uexc = _tri(tq, lambda r, c: r < c)

        def sweep(tiles, carry):
            nt = len(tiles)
            chains = [(t, h) for t in range(nt) for h in range(2)]
            rows = [pl.ds(pl.multiple_of(j * tq, tq), tq) for j, _ in tiles]
            ks = [k_ref[r, :] for r in rows]
            vs = [v_ref[r, :] for r in rows]
            terms = {(t, h): _sb_terms(qh[h], ks[t], tiles[t][1]) for t, h in chains}
            da = {(t, h): _dot(doh[h], vs[t], NT) for t, h in chains}
            cc = [carry[h][0] for h in range(2)]
            later = {}
            for t, h in chains:
                cc[h] = cc[h] + jnp.sum(terms[t, h][1], axis=1, keepdims=True)
                later[t, h] = tot[h] - cc[h]
            cum = {ch: _dot(terms[ch][1].astype(BF16), Urev, NN) for ch in chains}
            a, g, before = {}, {}, {}
            cg = [carry[h][1] for h in range(2)]
            for ch in chains:
                a[ch] = jnp.exp(terms[ch][0] + (cum[ch] + later[ch]))
                if tiles[ch[0]][1]:
                    a[ch] = jnp.where(terms[ch][2], a[ch], 0.0)
                g[ch] = a[ch] * da[ch]
                before[ch] = cg[ch[1]]
                cg[ch[1]] = cg[ch[1]] + jnp.sum(g[ch], axis=1, keepdims=True)
            G = {ch: _dot(g[ch].astype(BF16), Uexc, NN) for ch in chains}
            dz = {}
            for ch in chains:
                d = g[ch] - jnp.exp(terms[ch][0]) * (g[ch] + (G[ch] + before[ch]))
                if tiles[ch[0]][1]:
                    d = jnp.where(terms[ch][2], d, 0.0)
                dz[ch] = d.astype(BF16)
            dq = [carry[h][2] for h in range(2)]
            for t, h in chains:
                dq[h] = dq[h] + _dot(dz[t, h], ks[t], NN)
            for t in range(nt):
                dk_ref[:, rows[t]] += _dot(qth[0], dz[t, 0], NN) + _dot(qth[1], dz[t, 1], NN)
                dv_ref[:, rows[t]] += _dot(doth[0], a[t, 0].astype(BF16), NN) + _dot(doth[1], a[t, 1].astype(BF16), NN)
            return tuple((cc[h], cg[h], dq[h]) for h in range(2))

        zero = jnp.zeros((tq, 1), F32)
        zq = jnp.zeros((tq, LANES), F32)
        step = lambda n, cr: sweep([(SB_UNROLL_BWD * n + t, False) for t in range(SB_UNROLL_BWD)], cr)
        carry = lax.fori_loop(0, i // SB_UNROLL_BWD, step, ((zero, zero, zq), (zero, zero, zq)))
        tails = [lambda cr, k=k: sweep([(i - k + t, False) for t in range(k)] + [(i, True)], cr) for k in range(SB_UNROLL_BWD)]
        carry = _by_value(i % SB_UNROLL_BWD, tails, carry)
        dq_ref[...] = jnp.where(_lo_mask((tq, LANES)), carry[0][2], carry[1][2]) * SCALE

    spec_q = pl.BlockSpec((tq, LANES), lambda b, p, i: (b * nq + i, p))
    spec_t = pl.BlockSpec((LANES, tq), lambda b, p, i: (p, b * nq + i))
    spec_s = pl.BlockSpec((LANES, S), lambda b, p, i: (b * NP + p, 0))
    key_side = jax.ShapeDtypeStruct((B * NQ, S), F32)
    in_specs = [spec_q, pl.BlockSpec((S, LANES), lambda b, p, i: (b, NP + p)),
                pl.BlockSpec((S, LANES), lambda b, p, i: (b, 2 * NP + p)), spec_q, spec_t, spec_t, spec_q]
    out_specs = [spec_q, spec_s, spec_s]
    out_shape = [jax.ShapeDtypeStruct((T, NQ), F32), key_side, key_side]
    args = [qkv, qkv, qkv, do, q_t, do_t, tot]
    return _call_behind(body, name, grid, in_specs, out_specs, out_shape, args, exchange, True)


def _adamw(w, g, m, v, name):
    shape = w.shape
    cols = shape[-1]
    rows = math.prod(shape[:-1])
    tr = _pick(rows, max(8, (1 << 19) // max(cols, LANES) // 8 * 8), 8)

    def body(w_ref, g_ref, m_ref, v_ref, d_ref, mo_ref, vo_ref):
        gv = g_ref[...]
        mn = ADAM_B1 * m_ref[...] + (1.0 - ADAM_B1) * gv
        vn = ADAM_B2 * v_ref[...] + (1.0 - ADAM_B2) * (gv * gv)
        m_hat = mn / (1.0 - ADAM_B1 ** ADAM_STEP)
        v_hat = vn / (1.0 - ADAM_B2 ** ADAM_STEP)
        d_ref[...] = -ADAM_LR * (m_hat / (jnp.sqrt(v_hat) + ADAM_EPS) + ADAM_WD * w_ref[...])
        mo_ref[...] = mn
        vo_ref[...] = vn

    spec = pl.BlockSpec((tr, cols), lambda i: (i, 0))
    out = jax.ShapeDtypeStruct((rows, cols), F32)
    d, mn, vn = pl.pallas_call(
        body, name=name, grid=(rows // tr,),
        in_specs=[spec] * 4, out_specs=[spec] * 3, out_shape=[out] * 3,
        compiler_params=_params("parallel"),
    )(w.reshape(rows, cols), g.reshape(rows, cols), m.reshape(rows, cols), v.reshape(rows, cols))
    return d.reshape(shape), mn.reshape(shape), vn.reshape(shape)


def _pad_rows(a, rows):
    return jnp.pad(a, ((0, rows - a.shape[0]), (0, 0)))


def kernel(x, c, positions, ada_w, ada_b, norm1_g, norm2_g, wqkv_a, q_norm_a, k_norm_a, sinks_a, wo_a, wqkv_b, wo_b, w_gate, w_up, w_down, loss_target, m_ada_w, m_ada_b, m_norm1_g, m_norm2_g, m_wqkv_a, m_q_norm_a, m_k_norm_a, m_sinks_a, m_wo_a, m_wqkv_b, m_wo_b, m_w_gate, m_w_up, m_w_down, v_ada_w, v_ada_b, v_norm1_g, v_norm2_g, v_wqkv_a, v_q_norm_a, v_k_norm_a, v_sinks_a, v_wo_a, v_wqkv_b, v_wo_b, v_w_gate, v_w_up, v_w_down):
    B, S, D = x.shape
    T = B * S
    L = ada_w.shape[0]
    NA, NB_ = wqkv_a.shape[0], wqkv_b.shape[0]
    me = 4 * lax.axis_index("x") + 2 * lax.axis_index("y") + lax.axis_index("c")
    xt = x.reshape(T, D)

    col_sharded = {"qkv_a": wqkv_a, "qkv_b": wqkv_b, "gate": w_gate, "up": w_up}
    row_sharded = {"wo_a": wo_a, "wo_b": wo_b, "down": w_down}

    def shard_rows(key):
        kind, idx = key
        return col_sharded[kind][idx].T if kind in col_sharded else row_sharded[kind][idx]

    def layer_keys(l):
        mix = "a" if l % 2 == 0 else "b"
        return [("qkv_" + mix, l // 2), ("wo_" + mix, l // 2), ("gate", l), ("up", l), ("down", l)]

    def unpack(buf, keys, reshape):
        out, off = {}, 0
        for key in keys:
            rows = shard_rows(key).shape[0]
            out[key] = reshape(buf[..., off:off + rows, :], rows)
            off += rows
        return out

    first_b = 1
    keys_early = layer_keys(0)[:2]
    keys_mid = layer_keys(0)[2:] + [("qkv_b", 0)]
    keys_late = [k for l in range(1, L) for k in layer_keys(l) if k != ("qkv_b", 0)]
    pack = lambda keys: jnp.concatenate([shard_rows(k).astype(BF16) for k in keys], axis=0)
    full_rows = lambda b, rows: b.reshape(NDEV * rows, D)
    W = unpack(_all_gather(pack(keys_early), "ag_weights"), keys_early, full_rows)

    WA = ada_w.shape[2]
    c_all = _all_gather(c, "ag_c").reshape(NDEV * B, D)
    bias = lax.dynamic_slice_in_dim(ada_b, me * WA, WA, axis=1).reshape(L, 1, WA)
    mod_part = _ada_fwd(c_all, ada_w, bias, "ada_fwd")
    mod_all = _all_gather(mod_part.reshape(L * NDEV * B, WA), "ag_mod")
    mod_all = mod_all.reshape(NDEV, L, NDEV * B, WA).transpose(1, 2, 0, 3).reshape(L, NDEV * B, NDEV * WA)
    mod = lax.dynamic_slice_in_dim(mod_all, me * B, B, axis=1)
    mod = mod.reshape(L, B, 6, 1, D)
    sh1, sc1, g1, sh2, sc2, g2 = [mod[:, :, k] for k in range(6)]

    half = ROT // 2
    inv_freq = jnp.power(jnp.float32(ROPE_THETA), -jnp.arange(half, dtype=F32) * 2.0 / ROT)
    ang = positions.reshape(T, 1).astype(F32) * inv_freq[None, :]
    cos, sin = jnp.cos(ang), jnp.sin(ang)
    ones = jnp.ones((T, HEAD - ROT), F32)
    zeros = jnp.zeros((T, HEAD - ROT), F32)
    z8 = jnp.zeros((T, half), F32)
    cs = jnp.tile(jnp.concatenate([cos, cos, ones], axis=1), (1, 2))
    s1 = jnp.tile(jnp.concatenate([-sin, z8, zeros], axis=1), (1, 2))
    s2 = jnp.tile(jnp.concatenate([z8, sin, zeros], axis=1), (1, 2))

    saved = []
    xc = xt
    for l in range(L):
        j = l // 2
        h1 = _norm_mod(xc, norm1_g[l:l + 1], sc1[l], sh1[l], S, f"norm1_{l}")
        sv = dict(x_in=xc, h1=h1)
        if l % 2 == 0:
            qkv = _mm_nt(h1, W["qkv_a", j], F32, f"qkv_a_{l}")
            qg = jnp.tile(q_norm_a[j:j + 1], (1, 2))
            kg = jnp.tile(k_norm_a[j:j + 1], (1, 2))
            qn, kd, vd = _qk_prep(qkv, cs, s1, s2, qg, kg, f"qk_prep_{l}")
            sink2 = jnp.repeat(sinks_a[j].reshape(-1, 2), HEAD, axis=1).reshape(-1, 1, LANES)
            if l == 0:
                attn, lse, mid = _swa_fwd(qn, kd, vd, sink2, B, f"swa_fwd_{l}", gather=pack(keys_mid))
                W.update(unpack(mid, keys_mid, full_rows))
            else:
                attn, lse = _swa_fwd(qn, kd, vd, sink2, B, f"swa_fwd_{l}")
            sv.update(qkv=qkv, qg=qg, kg=kg, qn=qn, kd=kd, vd=vd, sink2=sink2, lse=lse)
            wo = W["wo_a", j]
        else:
            qkv = _mm_nt(h1, W["qkv_b", j], BF16, f"qkv_b_{l}")
            if l == first_b:
                attn, tot, late = _sb_fwd(qkv, B, f"sb_fwd_{l}", gather=pack(keys_late))
                W.update(unpack(late, keys_late, full_rows))
            else:
                attn, tot = _sb_fwd(qkv, B, f"sb_fwd_{l}")
            sv.update(qkv=qkv, tot=tot)
            wo = W["wo_b", j]
        y1, xm = _mm_res(attn, wo, xc, g1[l], S, f"attn_out_{l}")
        h2 = _norm_mod(xm, norm2_g[l:l + 1], sc2[l], sh2[l], S, f"norm2_{l}")
        gate, up, act = _swiglu_fwd(h2, W["gate", l], W["up", l], f"swiglu_fwd_{l}")
        y2, xc = _mm_res(act, W["down", l], xm, g2[l], S, f"mlp_out_{l}")
        sv.update(attn=attn, y1=y1, x_mid=xm, h2=h2, gate=gate, up=up, act=act, y2=y2)
        saved.append(sv)

    dx, loss_tile = _loss_head(xc, loss_target.reshape(T, D), "loss_head")

    G = {}
    pack_grads = lambda keys: jnp.concatenate([G[k].reshape(NDEV, G[k].shape[0] // NDEV, D) for k in keys], axis=1)
    keys_hi = [k for l in range(first_b + 1, L) for k in layer_keys(l)] + layer_keys(first_b)[1:]
    keys_mid_g = [("qkv_b", 0)] + layer_keys(0)[1:]
    keys_lo = layer_keys(0)[:1]
    received_hi = received_mid = None
    dmod = [None] * L
    dn1, dn2 = [None] * L, [None] * L
    dqg, dkg, dsink = [None] * NA, [None] * NA, [None] * NA
    dy2, dg2 = _gate_bwd(dx, saved[L - 1]["y2"], g2[L - 1], S, "gate2_bwd_top")
    for l in reversed(range(L)):
        j = l // 2
        mix = "a" if l % 2 == 0 else "b"
        sv = saved[l]
        dgate, dup = _swiglu_bwd(dy2, W["down", l], sv["gate"], sv["up"], f"swiglu_bwd_{l}")
        G["down", l] = _mm_tn(sv["act"], dy2, f"dw_down_{l}")
        dh2 = _mm_nn([(dgate, W["gate", l]), (dup, W["up", l])], f"dh2_{l}")
        G["gate", l] = _mm_tn(dgate, sv["h2"], f"dw_gate_{l}")
        G["up", l] = _mm_tn(dup, sv["h2"], f"dw_up_{l}")
        dxm, dsh2, dsc2, dn2[l], dy1, dg1 = _norm_mod_bwd(sv["x_mid"], dh2, dx, norm2_g[l:l + 1], sc2[l], S, f"norm2_bwd_{l}",
                                                         below=(sv["y1"], g1[l]))
        dattn = _mm_nt(dy1, W["wo_" + mix, j], BF16, f"dattn_{l}")
        G["wo_" + mix, j] = _mm_tn(sv["attn"], dy1, f"dw_o_{l}")
        if l % 2 == 0:
            swa_args = (sv["qn"], sv["kd"], sv["vd"], sv["sink2"], dattn, sv["lse"], B, f"swa_bwd_{l}")
            if l == 0:
                dq, dkc, dkp, dvc, dvp, dsink[j], received_mid = _swa_bwd(*swa_args, exchange=pack_grads(keys_mid_g))
            else:
                dq, dkc, dkp, dvc, dvp, dsink[j] = _swa_bwd(*swa_args)
            dqkv, dqg[j], dkg[j] = _qk_prep_bwd(sv["qkv"], cs, s1, s2, sv["qg"], sv["kg"], dq, dkc, dkp, dvc, dvp, B,
                                                f"qk_prep_bwd_{l}")
        else:
            nqb = sv["qkv"].shape[1] // 3
            sb_args = (sv["qkv"], sv["qkv"][:, :nqb].T, dattn, dattn.T, sv["tot"], B, f"sb_bwd_{l}")
            if l == first_b and keys_hi:
                dq, dk_t, dv_t, received_hi = _sb_bwd(*sb_args, exchange=pack_grads(keys_hi))
            else:
                dq, dk_t, dv_t = _sb_bwd(*sb_args)
            dk, dv = [t.reshape(B, nqb, S).transpose(0, 2, 1).reshape(T, nqb) for t in (dk_t, dv_t)]
            dqkv = jnp.concatenate([dq, dk, dv], axis=1).astype(BF16)
        dh1 = _mm_nn([(dqkv, W["qkv_" + mix, j])], f"dh1_{l}")
        G["qkv_" + mix, j] = _mm_tn(dqkv, sv["h1"], f"dw_qkv_{l}")
        n1_args = (sv["x_in"], dh1, dxm, norm1_g[l:l + 1], sc1[l], S, f"norm1_bwd_{l}")
        dmod_l = [None, None, dg1, dsh2, dsc2, dg2]
        if l > 0:
            dx, dmod_l[0], dmod_l[1], dn1[l], dy2, dg2 = _norm_mod_bwd(*n1_args, below=(saved[l - 1]["y2"], g2[l - 1]))
        else:
            dx, dmod_l[0], dmod_l[1], dn1[l] = _norm_mod_bwd(*n1_args)
        dmod[l] = jnp.concatenate(dmod_l, axis=1)
    grad_x = dx.reshape(B, S, D)

    ndm = L * 6
    dmod_rows = jnp.stack(dmod, axis=1).reshape(B * ndm, D)
    misc = jnp.concatenate(
        [jnp.concatenate(dn1, axis=0).reshape(B * L, D), jnp.concatenate(dn2, axis=0).reshape(B * L, D),
         _pad_rows(jnp.concatenate([jnp.pad(r, ((0, 0), (0, D - LANES))) for r in dqg + dkg]
                                   + [jnp.pad(r[:, 0, ::HEAD].reshape(1, -1), ((0, 0), (0, D - 2 * r.shape[0]))) for r in dsink]
                                   + [jnp.pad(loss_tile[0:1, 0:1], ((0, 0), (0, D - 1)))], axis=0), 8)], axis=0)
    nmisc = misc.shape[0]
    small = _all_gather(jnp.concatenate([dmod_rows, _pad_rows(misc, -(-nmisc // 8) * 8)], axis=0), "ag_small")
    dmod_all = small[:, :B * ndm].reshape(NDEV * B, ndm, D)
    g_ada_b = _sum_leading(dmod_all, "sum_dmod").reshape(L, 6 * D)
    misc_sum = _sum_leading(small[:, B * ndm:], "sum_misc")
    g_n1 = misc_sum[0:B * L].reshape(L, B, D)
    g_n2 = misc_sum[B * L:2 * B * L].reshape(L, B, D)
    g_norm1 = _sum_leading(g_n1.transpose(1, 0, 2), "sum_n1")
    g_norm2 = _sum_leading(g_n2.transpose(1, 0, 2), "sum_n2")
    o = 2 * B * L
    g_qn = misc_sum[o:o + NA, :HEAD]
    g_kn = misc_sum[o + NA:o + 2 * NA, :HEAD]
    nsink = sinks_a.shape[1]
    g_sink = misc_sum[o + 2 * NA:o + 3 * NA, :nsink]
    loss = misc_sum[o + 3 * NA, 0]

    dmod_loc = lax.dynamic_slice_in_dim(dmod_all.reshape(NDEV * B, L, 6 * D), me * WA, WA, axis=2)
    g_ada_w = _ada_bwd(c_all, dmod_loc.transpose(1, 0, 2), "ada_bwd")

    shard = unpack(_sum_leading(_exchange(pack_grads(keys_lo), "grad_exchange"), "grad_sum"), keys_lo, lambda b, rows: b)
    shard.update(unpack(_sum_leading(received_mid, "grad_sum_mid"), keys_mid_g, lambda b, rows: b))
    if received_hi is not None:
        shard.update(unpack(_sum_leading(received_hi, "grad_sum_hi"), keys_hi, lambda b, rows: b))

    def stacked(kind, n):
        return jnp.stack([shard[kind, i].T if kind in col_sharded else shard[kind, i] for i in range(n)])

    gw_qkv_a, gw_qkv_b, gw_gate, gw_up = stacked("qkv_a", NA), stacked("qkv_b", NB_), stacked("gate", L), stacked("up", L)
    gw_wo_a, gw_wo_b, gw_down = stacked("wo_a", NA), stacked("wo_b", NB_), stacked("down", L)

    grads = [g_ada_w, g_ada_b, g_norm1, g_norm2, gw_qkv_a, g_qn, g_kn, g_sink, gw_wo_a, gw_qkv_b, gw_wo_b,
             gw_gate, gw_up, gw_down]
    ws = [ada_w, ada_b, norm1_g, norm2_g, wqkv_a, q_norm_a, k_norm_a, sinks_a, wo_a, wqkv_b, wo_b, w_gate, w_up, w_down]
    ms = [m_ada_w, m_ada_b, m_norm1_g, m_norm2_g, m_wqkv_a, m_q_norm_a, m_k_norm_a, m_sinks_a, m_wo_a, m_wqkv_b,
          m_wo_b, m_w_gate, m_w_up, m_w_down]
    vs = [v_ada_w, v_ada_b, v_norm1_g, v_norm2_g, v_wqkv_a, v_q_norm_a, v_k_norm_a, v_sinks_a, v_wo_a, v_wqkv_b,
          v_wo_b, v_w_gate, v_w_up, v_w_down]
    deltas, new_m, new_v = [], [], []
    for k, (w, g, m, v) in enumerate(zip(ws, grads, ms, vs)):
        g = g.reshape(w.shape)
        d, mn, vn = _adamw(w, g, m, v, f"adamw_{k}")
        grads[k] = g
        deltas.append(d)
        new_m.append(mn)
        new_v.append(vn)
    return (loss, grad_x, *grads, *deltas, *new_m, *new_v)
```

```python
import functools
import math

import jax
import jax.numpy as jnp
from jax import lax
from jax.experimental import pallas as pl
from jax.experimental.pallas import tpu as pltpu

F32 = jnp.float32
BF16 = jnp.bfloat16
NDEV = 8
HEAD = 64
BLK = 128
LANES = 128
EPS = 1e-6
ROT = HEAD // 4
ROPE_THETA = 500000.0
SCALE = HEAD ** -0.5
NEG = -1e30
VMEM_LIMIT = 56 * 1024 * 1024
MESH = pl.DeviceIdType.MESH
HIGH = lax.Precision.HIGHEST

ADAM_LR = 0.001
ADAM_B1 = 0.9
ADAM_B2 = 0.999
ADAM_EPS = 1e-08
ADAM_WD = 0.01
ADAM_STEP = 10


def _params(*sem):
    return pltpu.CompilerParams(dimension_semantics=sem, vmem_limit_bytes=VMEM_LIMIT)


def _pick(n, cap, mult):
    if n <= cap:
        return n
    best = None
    for t in range(mult, cap + 1, mult):
        if n % t == 0:
            best = t
    assert best is not None, (n, cap, mult)
    return best


def _dot(a, b, dims, precision=None):
    return lax.dot_general(a, b, (dims, ((), ())), preferred_element_type=F32, precision=precision)


NN = ((1,), (0,))
NT = ((1,), (1,))
TN = ((0,), (0,))


def _all_gather(x, name):
    m, n = x.shape

    def body(x_ref, out_ref, send_sems, recv_sems, local_sem):
        ix, iy, ic = lax.axis_index("x"), lax.axis_index("y"), lax.axis_index("c")
        me, sibling = (ix, iy, ic), (ix, iy, 1 - ic)
        chips = [(1 - ix, iy), (ix, 1 - iy), (1 - ix, 1 - iy)]

        def slab(px, py, pc):
            return out_ref.at[4 * px + 2 * py + pc]

        def copy(k, block, to, src=None):
            return pltpu.make_async_remote_copy(
                src_ref=slab(*block) if src is None else src, dst_ref=slab(*block),
                send_sem=send_sems.at[k], recv_sem=recv_sems.at[k], device_id=to, device_id_type=MESH)

        mine = pltpu.make_async_copy(x_ref, slab(*me), local_sem)
        mine.start()
        first = [copy(0, me, sibling, src=x_ref)]
        first += [copy(1 + j, me, (*chip, ic), src=x_ref) for j, chip in enumerate(chips)]
        for cp in first:
            cp.start()
        passed = [copy(4 + j, (*chip, ic), sibling) for j, chip in enumerate(chips)]
        for j, chip in enumerate(chips):
            copy(1 + j, (*chip, ic), me).wait_recv()
            passed[j].start()
        copy(0, sibling, me).wait_recv()
        for j, chip in enumerate(chips):
            copy(4 + j, (*chip, 1 - ic), me).wait_recv()
        for cp in first + passed:
            cp.wait_send()
        mine.wait()

    return pl.pallas_call(
        body, name=name,
        out_shape=jax.ShapeDtypeStruct((NDEV, m, n), x.dtype),
        in_specs=[pl.BlockSpec(memory_space=pl.ANY)],
        out_specs=pl.BlockSpec(memory_space=pl.ANY),
        scratch_shapes=[pltpu.SemaphoreType.DMA((7,)), pltpu.SemaphoreType.DMA((7,)), pltpu.SemaphoreType.DMA(())],
    )(x)


COMM_SEMS = [pltpu.SemaphoreType.DMA((NDEV - 1,)), pltpu.SemaphoreType.DMA((NDEV - 1,)), pltpu.SemaphoreType.DMA(())]
HBM_SPEC = pl.BlockSpec(memory_space=pl.ANY)


def _direct_copies(src_ref, dst_ref, sems, scatter):
    send_sems, recv_sems, own_sem = sems
    ix, iy, ic = lax.axis_index("x"), lax.axis_index("y"), lax.axis_index("c")
    me = 4 * ix + 2 * iy + ic
    copies = [pltpu.make_async_copy(src_ref.at[me] if scatter else src_ref, dst_ref.at[me], own_sem)]
    for k in range(1, NDEV):
        px = 1 - ix if k & 4 else ix
        py = 1 - iy if k & 2 else iy
        pc = 1 - ic if k & 1 else ic
        copies.append(pltpu.make_async_remote_copy(
            src_ref=src_ref.at[4 * px + 2 * py + pc] if scatter else src_ref, dst_ref=dst_ref.at[me],
            send_sem=send_sems.at[k - 1], recv_sem=recv_sems.at[k - 1],
            device_id=(px, py, pc), device_id_type=MESH))
    return copies


def _exchange(p, name):
    def body(p_ref, r_ref, *sems):
        copies = _direct_copies(p_ref, r_ref, sems, True)
        for cp in copies:
            cp.start()
        for cp in copies:
            cp.wait()

    return pl.pallas_call(
        body, name=name,
        out_shape=jax.ShapeDtypeStruct(p.shape, p.dtype),
        in_specs=[HBM_SPEC], out_specs=HBM_SPEC, scratch_shapes=COMM_SEMS,
    )(p)


def _call_behind(body, name, grid, in_specs, out_specs, out_shape, args, payload=None, scatter=False):
    params = _params("arbitrary", "arbitrary", "arbitrary")
    if payload is None:
        return pl.pallas_call(body, name=name, grid=grid, in_specs=in_specs, out_specs=out_specs, out_shape=out_shape,
                              compiler_params=params)(*args)
    n_in, n_out = len(in_specs), len(out_specs)

    def edge(first):
        ids = [pl.program_id(a) for a in range(3)]
        return functools.reduce(lambda u, v: u & v, [i == (0 if first else d - 1) for i, d in zip(ids, grid)])

    def wrapped(*refs):
        x_ref, r_ref, sems = refs[n_in], refs[n_in + 1 + n_out], refs[n_in + n_out + 2:]

        @pl.when(edge(True))
        def _():
            for cp in _direct_copies(x_ref, r_ref, sems, scatter):
                cp.start()
        body(*refs[:n_in], *refs[n_in + 1:n_in + 1 + n_out])

        @pl.when(edge(False))
        def _():
            for cp in _direct_copies(x_ref, r_ref, sems, scatter):
                cp.wait()

    arrived = jax.ShapeDtypeStruct(payload.shape if scatter else (NDEV,) + payload.shape, payload.dtype)
    return pl.pallas_call(
        wrapped, name=name, grid=grid, in_specs=list(in_specs) + [HBM_SPEC], out_specs=list(out_specs) + [HBM_SPEC],
        out_shape=list(out_shape) + [arrived], scratch_shapes=COMM_SEMS, compiler_params=params,
    )(*args, payload)


def _sum_leading(r, name):
    k, m, n = r.shape
    mult = 8 * (4 // r.dtype.itemsize)
    tm = _pick(m, max(mult, (4 * 1024 * 1024) // (k * n * r.dtype.itemsize) // mult * mult), mult)

    def body(r_ref, o_ref):
        acc = r_ref[0].astype(F32)
        for s in range(1, k):
            acc = acc + r_ref[s].astype(F32)
        o_ref[...] = acc

    return pl.pallas_call(
        body, name=name, grid=(m // tm,),
        in_specs=[pl.BlockSpec((k, tm, n), lambda i: (0, i, 0))],
        out_specs=pl.BlockSpec((tm, n), lambda i: (i, 0)),
        out_shape=jax.ShapeDtypeStruct((m, n), F32),
        compiler_params=_params("parallel"),
    )(r)


def _mm_nt(a, bt, out_dtype, name):
    M, K = a.shape
    N = bt.shape[0]
    tm, tn = _pick(M, 512, 8), _pick(N, 1536, LANES)

    def body(a_ref, b_ref, o_ref):
        o_ref[...] = _dot(a_ref[...], b_ref[...], NT).astype(out_dtype)

    return pl.pallas_call(
        body, name=name, grid=(N // tn, M // tm),
        in_specs=[pl.BlockSpec((tm, K), lambda j, i: (i, 0)), pl.BlockSpec((tn, K), lambda j, i: (j, 0))],
        out_specs=pl.BlockSpec((tm, tn), lambda j, i: (i, j)),
        out_shape=jax.ShapeDtypeStruct((M, N), out_dtype),
        compiler_params=_params("parallel", "parallel"),
    )(a, bt)


def _mm_tn(a, b, name):
    M, N1 = a.shape
    N2 = b.shape[1]
    t1, tk = _pick(N1, 1536, LANES), _pick(M, 512, 8)
    nk = M // tk

    def body(a_ref, b_ref, o_ref, acc_ref):
        k = pl.program_id(1)

        @pl.when(k == 0)
        def _():
            acc_ref[...] = jnp.zeros_like(acc_ref)
        acc_ref[...] += _dot(a_ref[...], b_ref[...], TN)

        @pl.when(k == nk - 1)
        def _():
            o_ref[...] = acc_ref[...].astype(BF16)

    return pl.pallas_call(
        body, name=name, grid=(N1 // t1, nk),
        in_specs=[pl.BlockSpec((tk, t1), lambda i, k: (k, i)), pl.BlockSpec((tk, N2), lambda i, k: (k, 0))],
        out_specs=pl.BlockSpec((t1, N2), lambda i, k: (i, 0)),
        out_shape=jax.ShapeDtypeStruct((N1, N2), BF16),
        scratch_shapes=[pltpu.VMEM((t1, N2), F32)],
        compiler_params=_params("parallel", "arbitrary"),
    )(a, b)


def _mm_res(a, w, x, gate, S, name):
    T, K = a.shape
    D = w.shape[1]
    tm, tn = _pick(S, 512, 8), _pick(D, 1024, LANES)
    nb = S // tm

    def body(a_ref, w_ref, x_ref, g_ref, y_ref, o_ref):
        y = _dot(a_ref[...], w_ref[...], NN)
        y_ref[...] = y.astype(BF16)
        o_ref[...] = x_ref[...] + g_ref[0] * y

    return pl.pallas_call(
        body, name=name, grid=(T // tm, D // tn),
        in_specs=[pl.BlockSpec((tm, K), lambda i, j: (i, 0)), pl.BlockSpec((K, tn), lambda i, j: (0, j)),
                  pl.BlockSpec((tm, tn), lambda i, j: (i, j)), pl.BlockSpec((1, 1, tn), lambda i, j: (i // nb, 0, j))],
        out_specs=[pl.BlockSpec((tm, tn), lambda i, j: (i, j)), pl.BlockSpec((tm, tn), lambda i, j: (i, j))],
        out_shape=[jax.ShapeDtypeStruct((T, D), BF16), jax.ShapeDtypeStruct((T, D), F32)],
        compiler_params=_params("parallel", "parallel"),
    )(a, w, x, gate)


def _swiglu_fwd(h, wgt, wut, name):
    T, D = h.shape
    F = wgt.shape[0]
    tm, tn = _pick(T, 512, 8), _pick(F, 1536, LANES)

    def body(h_ref, g_ref, u_ref, go_ref, uo_ref, a_ref):
        hh = h_ref[...]
        g = _dot(hh, g_ref[...], NT)
        u = _dot(hh, u_ref[...], NT)
        go_ref[...] = g.astype(BF16)
        uo_ref[...] = u.astype(BF16)
        a_ref[...] = (g * jax.nn.sigmoid(g) * u).astype(BF16)

    spec_w = pl.BlockSpec((tn, D), lambda j, i: (j, 0))
    spec_o = pl.BlockSpec((tm, tn), lambda j, i: (i, j))
    out = jax.ShapeDtypeStruct((T, F), BF16)
    return pl.pallas_call(
        body, name=name, grid=(F // tn, T // tm),
        in_specs=[pl.BlockSpec((tm, D), lambda j, i: (i, 0)), spec_w, spec_w],
        out_specs=[spec_o, spec_o, spec_o],
        out_shape=[out, out, out],
        compiler_params=_params("parallel", "parallel"),
    )(h, wgt, wut)


def _swiglu_bwd(dy, wd, gate, up, name):
    T, D = dy.shape
    F = wd.shape[0]
    tm, tn = _pick(T, 512, 8), _pick(F, 1536, LANES)

    def body(dy_ref, w_ref, g_ref, u_ref, dg_ref, du_ref):
        da = _dot(dy_ref[...], w_ref[...], NT)
        g = g_ref[...].astype(F32)
        sg = jax.nn.sigmoid(g)
        silu = g * sg
        du_ref[...] = (da * silu).astype(BF16)
        dg_ref[...] = (da * u_ref[...].astype(F32) * (sg + silu * (1.0 - sg))).astype(BF16)

    spec_o = pl.BlockSpec((tm, tn), lambda j, i: (i, j))
    return pl.pallas_call(
        body, name=name, grid=(F // tn, T // tm),
        in_specs=[pl.BlockSpec((tm, D), lambda j, i: (i, 0)), pl.BlockSpec((tn, D), lambda j, i: (j, 0)), spec_o, spec_o],
        out_specs=[spec_o, spec_o],
        out_shape=[jax.ShapeDtypeStruct((T, F), BF16), jax.ShapeDtypeStruct((T, F), BF16)],
        compiler_params=_params("parallel", "parallel"),
    )(dy, wd, gate, up)


def _norm_mod(x, gain, sc, sh, S, name):
    T, D = x.shape
    tm = _pick(S, 512, 8)
    nb = S // tm

    def body(x_ref, g_ref, sc_ref, sh_ref, o_ref):
        xv = x_ref[...]
        r = lax.rsqrt(jnp.mean(xv * xv, axis=-1, keepdims=True) + EPS)
        o_ref[...] = ((xv * r) * g_ref[...] * (1.0 + sc_ref[0]) + sh_ref[0]).astype(BF16)

    spec_b = pl.BlockSpec((1, 1, D), lambda i: (i // nb, 0, 0))
    return pl.pallas_call(
        body, name=name, grid=(T // tm,),
        in_specs=[pl.BlockSpec((tm, D), lambda i: (i, 0)), pl.BlockSpec((1, D), lambda i: (0, 0)), spec_b, spec_b],
        out_specs=pl.BlockSpec((tm, D), lambda i: (i, 0)),
        out_shape=jax.ShapeDtypeStruct((T, D), BF16),
        compiler_params=_params("parallel"),
    )(x, gain, sc, sh)


def _norm_mod_bwd(x, pairs, dres, gain, sc, S, name, below=None):
    T, D = x.shape
    B = T // S
    tm = _pick(S, 512, 8)
    nb = S // tm
    n_mm = 2 * len(pairs)

    def body(*refs):
        mm, (x_ref, dr_ref, g_ref, sc_ref), rest = refs[:n_mm], refs[n_mm:n_mm + 4], refs[n_mm + 4:]
        if below is None:
            o_ref, dsh_ref, dsc_ref, dg_ref = rest
            sums = [dsh_ref, dsc_ref, dg_ref]
        else:
            y_ref, gt_ref, o_ref, dsh_ref, dsc_ref, dg_ref, dy_ref, dgt_ref = rest
            sums = [dsh_ref, dsc_ref, dg_ref, dgt_ref]

        @pl.when(pl.program_id(1) == 0)
        def _():
            for ref in sums:
                ref[...] = jnp.zeros_like(ref)
        dhv = _dot(mm[0][...], mm[1][...], NN)
        for p in range(2, n_mm, 2):
            dhv = dhv + _dot(mm[p][...], mm[p + 1][...], NN)
        xv, g = x_ref[...], g_ref[...]
        r = lax.rsqrt(jnp.mean(xv * xv, axis=-1, keepdims=True) + EPS)
        xhat = xv * r
        dsh_ref[0] += jnp.sum(dhv, axis=0, keepdims=True)
        dsc_ref[0] += jnp.sum(dhv * (xhat * g), axis=0, keepdims=True)
        dn = dhv * (1.0 + sc_ref[0])
        dg_ref[0] += jnp.sum(dn * xhat, axis=0, keepdims=True)
        dxh = dn * g
        out = dr_ref[...] + r * (dxh - xhat * jnp.mean(dxh * xhat, axis=-1, keepdims=True))
        o_ref[...] = out
        if below is not None:
            dy_ref[...] = (out * gt_ref[0]).astype(BF16)
            dgt_ref[0] += jnp.sum(out * y_ref[...].astype(F32), axis=0, keepdims=True)

    spec_t = pl.BlockSpec((tm, D), lambda b, i: (b * nb + i, 0))
    spec_b = pl.BlockSpec((1, 1, D), lambda b, i: (b, 0, 0))
    red = jax.ShapeDtypeStruct((B, 1, D), F32)
    in_specs, args = [], []
    for a, w in pairs:
        K = a.shape[1]
        in_specs += [pl.BlockSpec((tm, K), lambda b, i: (b * nb + i, 0)),
                     pl.BlockSpec((K, D), lambda b, i: (0, 0), pipeline_mode=pl.Buffered(1))]
        args += [a, w]
    in_specs += [spec_t, spec_t, pl.BlockSpec((1, D), lambda b, i: (0, 0)), spec_b]
    args += [x, dres, gain, sc]
    out_specs = [spec_t, spec_b, spec_b, spec_b]
    out_shape = [jax.ShapeDtypeStruct((T, D), F32), red, red, red]
    if below is not None:
        in_specs += [spec_t, spec_b]
        out_specs += [spec_t, spec_b]
        out_shape += [jax.ShapeDtypeStruct((T, D), BF16), red]
        args += list(below)
    return pl.pallas_call(
        body, name=name, grid=(B, nb), in_specs=in_specs, out_specs=out_specs, out_shape=out_shape,
        compiler_params=_params("parallel", "arbitrary"),
    )(*args)


def _gate_bwd(dx, y, gate, S, name):
    T, D = dx.shape
    B = T // S
    tm = _pick(S, 512, 8)
    nb = S // tm

    def body(dx_ref, y_ref, g_ref, dy_ref, dg_ref):
        @pl.when(pl.program_id(1) == 0)
        def _():
            dg_ref[...] = jnp.zeros_like(dg_ref)
        d = dx_ref[...]
        dy_ref[...] = (d * g_ref[0]).astype(BF16)
        dg_ref[0] += jnp.sum(d * y_ref[...].astype(F32), axis=0, keepdims=True)

    spec_t = pl.BlockSpec((tm, D), lambda b, i: (b * nb + i, 0))
    spec_b = pl.BlockSpec((1, 1, D), lambda b, i: (b, 0, 0))
    return pl.pallas_call(
        body, name=name, grid=(B, nb),
        in_specs=[spec_t, spec_t, spec_b],
        out_specs=[spec_t, spec_b],
        out_shape=[jax.ShapeDtypeStruct((T, D), BF16), jax.ShapeDtypeStruct((B, 1, D), F32)],
        compiler_params=_params("parallel", "arbitrary"),
    )(dx, y, gate)


def _loss_head(y, target, name):
    T, D = y.shape
    tm = _pick(T, 512, 8)

    def body(y_ref, t_ref, dy_ref, l_ref):
        @pl.when(pl.program_id(0) == 0)
        def _():
            l_ref[...] = jnp.zeros_like(l_ref)
        e = y_ref[...] - t_ref[...]
        dy_ref[...] = e * (1.0 / D)
        l_ref[...] += 0.5 * jnp.sum(jnp.mean(e * e, axis=-1, keepdims=True), axis=0, keepdims=True)

    spec = pl.BlockSpec((tm, D), lambda i: (i, 0))
    return pl.pallas_call(
        body, name=name, grid=(T // tm,),
        in_specs=[spec, spec],
        out_specs=[spec, pl.BlockSpec((8, LANES), lambda i: (0, 0))],
        out_shape=[jax.ShapeDtypeStruct((T, D), F32), jax.ShapeDtypeStruct((8, LANES), F32)],
        compiler_params=_params("arbitrary"),
    )(y, target)


def _ada_fwd(c_all, ada_w, bias, name):
    NB, D = c_all.shape
    L, _, W = ada_w.shape

    def body(c_ref, w_ref, b_ref, o_ref):
        cv = c_ref[...]
        cond = cv * jax.nn.sigmoid(cv)
        o_ref[0] = _dot(cond, w_ref[0], NN, HIGH) + b_ref[0]

    return pl.pallas_call(
        body, name=name, grid=(L,),
        in_specs=[pl.BlockSpec((NB, D), lambda l: (0, 0)), pl.BlockSpec((1, D, W), lambda l: (l, 0, 0)),
                  pl.BlockSpec((1, 1, W), lambda l: (l, 0, 0))],
        out_specs=pl.BlockSpec((1, NB, W), lambda l: (l, 0, 0)),
        out_shape=jax.ShapeDtypeStruct((L, NB, W), F32),
        compiler_params=_params("parallel"),
    )(c_all, ada_w, bias)


def _ada_bwd(c_all, dmod, name):
    NB, D = c_all.shape
    L, _, W = dmod.shape

    def body(c_ref, d_ref, o_ref):
        cv = c_ref[...]
        cond = cv * jax.nn.sigmoid(cv)
        o_ref[0] = _dot(cond, d_ref[0], TN, HIGH)

    return pl.pallas_call(
        body, name=name, grid=(L,),
        in_specs=[pl.BlockSpec((NB, D), lambda l: (0, 0)), pl.BlockSpec((1, NB, W), lambda l: (l, 0, 0))],
        out_specs=pl.BlockSpec((1, D, W), lambda l: (l, 0, 0)),
        out_shape=jax.ShapeDtypeStruct((L, D, W), F32),
        compiler_params=_params("parallel"),
    )(c_all, dmod)


def _lo_mask(shape):
    return lax.broadcasted_iota(jnp.int32, shape, len(shape) - 1) < HEAD


def _head_sum_matrix():
    r = lax.broadcasted_iota(jnp.int32, (LANES, LANES), 0) // HEAD
    c = lax.broadcasted_iota(jnp.int32, (LANES, LANES), 1) // HEAD
    return (r == c).astype(BF16)


def _head_sum(x, P):
    hi = x.astype(BF16)
    lo = (x - hi.astype(F32)).astype(BF16)
    return _dot(hi, P, NN) + _dot(lo, P, NN)


def _rope(y, cs, s1, s2):
    return y * cs + pltpu.roll(y, LANES - ROT // 2, 1) * s1 + pltpu.roll(y, ROT // 2, 1) * s2


def _rope_bwd(d, cs, s1, s2):
    return d * cs + pltpu.roll(d * s1, ROT // 2, 1) + pltpu.roll(d * s2, LANES - ROT // 2, 1)


def _qk_prep(qkv, cs, s1, s2, qg, kg, name):
    T, W = qkv.shape
    NQ = W - 2 * LANES
    tm = _pick(T, 512, 8)

    def body(x_ref, cs_ref, s1_ref, s2_ref, qg_ref, kg_ref, q_ref, k_ref, v_ref):
        P = _head_sum_matrix()
        cs_, s1_, s2_ = cs_ref[...], s1_ref[...], s2_ref[...]
        lo = _lo_mask((tm, LANES))

        def norm_rope(xv, g):
            ms = _head_sum(xv * xv, P) * (1.0 / HEAD)
            return _rope(xv * lax.rsqrt(ms + EPS) * g, cs_, s1_, s2_)

        for j in range(NQ // LANES):
            q_ref[:, j * LANES:(j + 1) * LANES] = norm_rope(x_ref[:, j * LANES:(j + 1) * LANES], qg_ref[...]).astype(BF16)
        kr = norm_rope(x_ref[:, NQ:NQ + LANES], kg_ref[...])
        ks = pltpu.roll(kr, HEAD, 1)
        k_ref[:, :LANES] = jnp.where(lo, kr, ks).astype(BF16)
        k_ref[:, LANES:] = jnp.where(lo, ks, kr).astype(BF16)
        vr = x_ref[:, NQ + LANES:]
        vs = pltpu.roll(vr, HEAD, 1)
        v_ref[:, :LANES] = jnp.where(lo, vr, vs).astype(BF16)
        v_ref[:, LANES:] = jnp.where(lo, vs, vr).astype(BF16)

    spec_t = pl.BlockSpec((tm, LANES), lambda i: (i, 0))
    spec_g = pl.BlockSpec((1, LANES), lambda i: (0, 0))
    return pl.pallas_call(
        body, name=name, grid=(T // tm,),
        in_specs=[pl.BlockSpec((tm, W), lambda i: (i, 0)), spec_t, spec_t, spec_t, spec_g, spec_g],
        out_specs=[pl.BlockSpec((tm, NQ), lambda i: (i, 0)), pl.BlockSpec((tm, 2 * LANES), lambda i: (i, 0)),
                   pl.BlockSpec((tm, 2 * LANES), lambda i: (i, 0))],
        out_shape=[jax.ShapeDtypeStruct((T, NQ), BF16), jax.ShapeDtypeStruct((T, 2 * LANES), BF16),
                   jax.ShapeDtypeStruct((T, 2 * LANES), BF16)],
        compiler_params=_params("parallel"),
    )(qkv, cs, s1, s2, qg, kg)


def _stack_heads(x2):
    lo = _lo_mask(x2.shape)
    z = jnp.zeros_like(x2)
    return jnp.concatenate([jnp.where(lo, x2, z), jnp.where(lo, z, x2)], axis=0)


def _unstack_heads(xs):
    r = xs.shape[0] // 2
    return jnp.where(_lo_mask((r, LANES)), xs[:r], xs[r:])


def _swa_valid(i):
    qo = lax.broadcasted_iota(jnp.int32, (2 * BLK, 2 * BLK), 0) % BLK
    kc_ = lax.broadcasted_iota(jnp.int32, (2 * BLK, 2 * BLK), 1)
    rel = qo + BLK - kc_
    return (rel >= 0) & (rel < BLK) & ((kc_ >= BLK) | (i > 0))


def _swa_scores(q2, kk, sink2, valid):
    qs = _stack_heads(q2) * SCALE
    s = _dot(qs, kk, NT)
    sk = jnp.concatenate([jnp.broadcast_to(sink2[:, 0:1], (BLK, 1)), jnp.broadcast_to(sink2[:, HEAD:HEAD + 1], (BLK, 1))], axis=0)
    return qs, jnp.where(valid, s, NEG), sk


def _swa_fwd(q, kd, vd, sink2, B, name, gather=None):
    T, NQ = q.shape
    NP = NQ // LANES
    nq = T // B // BLK
    NG = kd.shape[1] // LANES
    grp = NP // NG

    def body(q_ref, kp_ref, kc_ref, vp_ref, vc_ref, s_ref, o_ref, l_ref):
        valid = _swa_valid(pl.program_id(2))
        kk = jnp.concatenate([kp_ref[...], kc_ref[...]], axis=0)
        vs = _stack_heads(jnp.concatenate([vp_ref[...], vc_ref[...]], axis=0))
        sls = [slice(jj * LANES, (jj + 1) * LANES) for jj in range(grp)]
        sc = [_swa_scores(q_ref[:, sl], kk, s_ref[jj], valid) for jj, sl in enumerate(sls)]
        ms = [jnp.maximum(jnp.max(s, axis=1, keepdims=True), sk) for _, s, sk in sc]
        ps = [jnp.where(valid, jnp.exp(s - m), 0.0) for (_, s, _), m in zip(sc, ms)]
        ls = [jnp.sum(p, axis=1, keepdims=True) + jnp.exp(sk - m) for p, (_, _, sk), m in zip(ps, sc, ms)]
        ps = [(p * (1.0 / l)).astype(BF16) for p, l in zip(ps, ls)]
        os_ = [_dot(jnp.concatenate([p[:BLK], p[BLK:]], axis=1), vs, NN) for p in ps]
        for sl, o, m, l in zip(sls, os_, ms, ls):
            o_ref[:, sl] = o.astype(BF16)
            l_ref[:, sl] = _unstack_heads(jnp.broadcast_to(m + jnp.log(l), (2 * BLK, LANES)))

    spec_q = pl.BlockSpec((BLK, grp * LANES), lambda b, g, i: (b * nq + i, g))
    spec_p = pl.BlockSpec((BLK, LANES), lambda b, g, i: (b * nq + jnp.maximum(i - 1, 0), g))
    spec_c = pl.BlockSpec((BLK, LANES), lambda b, g, i: (b * nq + i, g))
    in_specs = [spec_q, spec_p, spec_c, spec_p, spec_c, pl.BlockSpec((grp, 1, LANES), lambda b, g, i: (g, 0, 0))]
    out_shape = [jax.ShapeDtypeStruct((T, NQ), BF16), jax.ShapeDtypeStruct((T, NQ), F32)]
    return _call_behind(body, name, (B, NG, nq), in_specs, [spec_q, spec_q], out_shape, [q, kd, kd, vd, vd, sink2], gather, False)


def _swa_bwd(q, kd, vd, sink2, do, lse, B, name, exchange=None):
    T, NQ = q.shape
    NP = NQ // LANES
    nq = T // B // BLK
    NG = kd.shape[1] // LANES
    grp = NP // NG

    def body(q_ref, kp_ref, kc_ref, vp_ref, vc_ref, s_ref, do_ref, l_ref,
             dq_ref, dkc_ref, dkp_ref, dvc_ref, dvp_ref, ds_ref):
        b, i = pl.program_id(1), pl.program_id(2)

        @pl.when((b == 0) & (i == 0))
        def _():
            ds_ref[...] = jnp.zeros_like(ds_ref)
        valid = _swa_valid(i)
        kk = jnp.concatenate([kp_ref[...], kc_ref[...]], axis=0)
        vv = jnp.concatenate([vp_ref[...], vc_ref[...]], axis=0)
        sls = [slice(jj * LANES, (jj + 1) * LANES) for jj in range(grp)]
        sc = [_swa_scores(q_ref[:, sl], kk, s_ref[jj], valid) for jj, sl in enumerate(sls)]
        dos = [_stack_heads(do_ref[:, sl]) for sl in sls]
        dps = [_dot(d, vv, NT) for d in dos]
        lses = [jnp.concatenate([l_ref[:, sl][:, 0:1], l_ref[:, sl][:, HEAD:HEAD + 1]], axis=0) for sl in sls]
        ps = [jnp.where(valid, jnp.exp(s - lse), 0.0) for (_, s, _), lse in zip(sc, lses)]
        deltas = [jnp.sum(p * dp, axis=1, keepdims=True) for p, dp in zip(ps, dps)]
        dscs = [(p * (dp - delta)).astype(BF16) for p, dp, delta in zip(ps, dps, deltas)]
        dqs = [_dot(dsc, kk, NN) for dsc in dscs]
        dk = jnp.zeros((2 * BLK, LANES), F32)
        dv = jnp.zeros((2 * BLK, LANES), F32)
        for jj, sl in enumerate(sls):
            dsk = -jnp.exp(sc[jj][2] - lses[jj]) * deltas[jj]
            dsk_lo = jnp.sum(dsk[:BLK], axis=0, keepdims=True)
            dsk_hi = jnp.sum(dsk[BLK:], axis=0, keepdims=True)
            ds_ref[jj] += jnp.where(_lo_mask((1, LANES)), dsk_lo, dsk_hi)
            dq_ref[:, sl] = _unstack_heads(dqs[jj]) * SCALE
            dk = dk + _dot(dscs[jj], sc[jj][0], TN)
            dv = dv + _dot(ps[jj].astype(BF16), dos[jj], TN)
        dkp_ref[...] = dk[:BLK]
        dkc_ref[...] = dk[BLK:]
        dvp_ref[...] = dv[:BLK]
        dvc_ref[...] = dv[BLK:]

    spec_q = pl.BlockSpec((BLK, grp * LANES), lambda g, b, i: (b * nq + i, g))
    spec_p = pl.BlockSpec((BLK, LANES), lambda g, b, i: (b * nq + jnp.maximum(i - 1, 0), g))
    spec_c = pl.BlockSpec((BLK, LANES), lambda g, b, i: (b * nq + i, g))
    spec_s = pl.BlockSpec((grp, 1, LANES), lambda g, b, i: (g, 0, 0))
    kv = jax.ShapeDtypeStruct((T, NG * LANES), F32)
    in_specs = [spec_q, spec_p, spec_c, spec_p, spec_c, spec_s, spec_q, spec_q]
    out_specs = [spec_q, spec_c, spec_c, spec_c, spec_c, spec_s]
    out_shape = [jax.ShapeDtypeStruct((T, NQ), F32), kv, kv, kv, kv, jax.ShapeDtypeStruct((NP, 1, LANES), F32)]
    return _call_behind(body, name, (NG, B, nq), in_specs, out_specs, out_shape, [q, kd, kd, vd, vd, sink2, do, lse],
                        exchange, True)


def _qk_prep_bwd(qkv, cs, s1, s2, qg, kg, dq, dkc, dkp, dvc, dvp, B, name):
    T, W = qkv.shape
    NQ = W - 2 * LANES
    NP = NQ // LANES
    nq = T // B // BLK

    def body(x_ref, cs_ref, s1_ref, s2_ref, qg_ref, kg_ref, dq_ref, dkc_ref, dkp_ref, dvc_ref, dvp_ref,
             o_ref, dqg_ref, dkg_ref):
        b, i = pl.program_id(0), pl.program_id(1)

        @pl.when((b == 0) & (i == 0))
        def _():
            dqg_ref[...] = jnp.zeros_like(dqg_ref)
            dkg_ref[...] = jnp.zeros_like(dkg_ref)
        P = _head_sum_matrix()
        cs_, s1_, s2_ = cs_ref[...], s1_ref[...], s2_ref[...]
        lo = _lo_mask((BLK, LANES))
        has_next = (i + 1 < nq).astype(F32)

        def norm_rope_bwd(xv, g, d):
            du = _rope_bwd(d, cs_, s1_, s2_)
            r = lax.rsqrt(_head_sum(xv * xv, P) * (1.0 / HEAD) + EPS)
            xhat = xv * r
            dgain = jnp.sum(du * xhat, axis=0, keepdims=True)
            uu = du * g
            dx = r * (uu - xhat * (_head_sum(uu * xhat, P) * (1.0 / HEAD)))
            return dx, dgain + pltpu.roll(dgain, HEAD, 1)

        dqg = jnp.zeros((1, LANES), F32)
        for j in range(NP):
            sl = slice(j * LANES, (j + 1) * LANES)
            dx, dg = norm_rope_bwd(x_ref[:, sl], qg_ref[...], dq_ref[:, sl])
            o_ref[:, sl] = dx.astype(BF16)
            dqg = dqg + dg
        dqg_ref[...] += dqg

        def fold(c_ref, p_ref, g):
            sl = slice(g * LANES, (g + 1) * LANES)
            t = c_ref[:, sl] + has_next * p_ref[:, sl]
            return t + pltpu.roll(t, HEAD, 1)

        dk = jnp.where(lo, fold(dkc_ref, dkp_ref, 0), fold(dkc_ref, dkp_ref, 1))
        dx, dg = norm_rope_bwd(x_ref[:, NQ:NQ + LANES], kg_ref[...], dk)
        o_ref[:, NQ:NQ + LANES] = dx.astype(BF16)
        dkg_ref[...] += dg
        dv = jnp.where(lo, fold(dvc_ref, dvp_ref, 0), fold(dvc_ref, dvp_ref, 1))
        o_ref[:, NQ + LANES:] = dv.astype(BF16)

    spec_t = pl.BlockSpec((BLK, LANES), lambda b, i: (b * nq + i, 0))
    spec_g = pl.BlockSpec((1, LANES), lambda b, i: (0, 0))
    spec_c = pl.BlockSpec((BLK, 2 * LANES), lambda b, i: (b * nq + i, 0))
    spec_n = pl.BlockSpec((BLK, 2 * LANES), lambda b, i: (b * nq + jnp.minimum(i + 1, nq - 1), 0))
    row = jax.ShapeDtypeStruct((1, LANES), F32)
    return pl.pallas_call(
        body, name=name, grid=(B, nq),
        in_specs=[pl.BlockSpec((BLK, W), lambda b, i: (b * nq + i, 0)), spec_t, spec_t, spec_t, spec_g, spec_g,
                  pl.BlockSpec((BLK, NQ), lambda b, i: (b * nq + i, 0)), spec_c, spec_n, spec_c, spec_n],
        out_specs=[pl.BlockSpec((BLK, W), lambda b, i: (b * nq + i, 0)), spec_g, spec_g],
        out_shape=[jax.ShapeDtypeStruct((T, W), BF16), row, row],
        compiler_params=_params("arbitrary", "arbitrary"),
    )(qkv, cs, s1, s2, qg, kg, dq, dkc, dkp, dvc, dvp)


SB_TILE = 256
SB_UNROLL = 4
SB_UNROLL_BWD = 2


def _split_heads(x2, scale=None):
    lo = _lo_mask(x2.shape)
    z = jnp.zeros_like(x2)
    if scale is not None:
        x2 = x2 * scale
    return jnp.where(lo, x2, z), jnp.where(lo, z, x2)


def _sb_terms(qh, kj, diagonal):
    z = _dot(qh, kj, NT)
    e = jnp.exp(-jnp.abs(z))
    lb = jnp.minimum(z, 0.0) - jnp.log(1.0 + e)
    L = lb - z
    if not diagonal:
        return lb, L, None, z, e
    strict = lax.broadcasted_iota(jnp.int32, z.shape, 1) < lax.broadcasted_iota(jnp.int32, z.shape, 0)
    return lb, jnp.where(strict, L, 0.0), strict, z, e


def _tri(n, cmp):
    r = lax.broadcasted_iota(jnp.int32, (n, n), 0)
    c = lax.broadcasted_iota(jnp.int32, (n, n), 1)
    return cmp(r, c).astype(BF16)


def _by_value(r, fns, carry):
    if len(fns) == 1:
        return fns[0](carry)
    half = len(fns) // 2
    return lax.cond(r < half, lambda cr: _by_value(r, fns[:half], cr), lambda cr: _by_value(r - half, fns[half:], cr), carry)


def _sb_fwd(qkv, B, name, gather=None):
    T, W = qkv.shape
    NQ = W // 3
    NP = NQ // LANES
    S = T // B
    tq = min(SB_TILE, S)
    nq = S // tq
    grid = (B, NP, nq)

    def body(q_ref, k_ref, v_ref, o_ref, t_ref):
        i = pl.program_id(2)
        qh = _split_heads(q_ref[...], SCALE)
        U = _tri(tq, lambda r, c: r > c)

        def sweep(tiles, cs, acc):
            chains = [(t, h) for t in range(len(tiles)) for h in range(2)]
            rows = [pl.ds(pl.multiple_of(j * tq, tq), tq) for j, _ in tiles]
            ks = [k_ref[r, :] for r in rows]
            vs = [_split_heads(v_ref[r, :]) for r in rows]
            terms = {(t, h): _sb_terms(qh[h], ks[t], tiles[t][1]) for t, h in chains}
            carry = {}
            for h in range(2):
                c = cs[h]
                for t in range(len(tiles)):
                    carry[t, h] = c
                    c = c + jnp.sum(terms[t, h][1], axis=1, keepdims=True)
                cs = cs[:h] + (c,) + cs[h + 1:]
            cum = {ch: _dot(terms[ch][1].astype(BF16), U, NN) for ch in chains}
            for ch in chains:
                a = jnp.exp(terms[ch][0] + (cum[ch] + carry[ch]))
                if tiles[ch[0]][1]:
                    a = jnp.where(terms[ch][2], a, 0.0)
                acc = acc + _dot(a.astype(BF16), vs[ch[0]][ch[1]], NN)
            return cs, acc

        zero = jnp.zeros((tq, 1), F32)
        rem = i % SB_UNROLL
        heads = [lambda cr, k=k: sweep([(i, True)] + [(i - 1 - t, False) for t in range(k)], *cr) for k in range(SB_UNROLL)]
        carry = _by_value(rem, heads, ((zero, zero), jnp.zeros((tq, LANES), F32)))
        step = lambda n, cr: sweep([(i - 1 - rem - SB_UNROLL * n - t, False) for t in range(SB_UNROLL)], *cr)
        cs, acc = lax.fori_loop(0, i // SB_UNROLL, step, carry)
        o_ref[...] = acc.astype(BF16)
        t_ref[...] = jnp.where(_lo_mask((tq, LANES)), cs[0], cs[1])

    spec_q = pl.BlockSpec((tq, LANES), lambda b, p, i: (b * nq + i, p))
    in_specs = [spec_q, pl.BlockSpec((S, LANES), lambda b, p, i: (b, NP + p)),
                pl.BlockSpec((S, LANES), lambda b, p, i: (b, 2 * NP + p))]
    out_shape = [jax.ShapeDtypeStruct((T, NQ), BF16), jax.ShapeDtypeStruct((T, NQ), F32)]
    return _call_behind(body, name, grid, in_specs, [spec_q, spec_q], out_shape, [qkv, qkv, qkv], gather, False)


def _sb_bwd(qkv, q_t, do, do_t, tot, B, name, exchange=None):
    T, W = qkv.shape
    NQ = W // 3
    NP = NQ // LANES
    S = T // B
    tq = min(SB_TILE, S)
    nq = S // tq
    grid = (B, NP, nq)

    def body(q_ref, k_ref, v_ref, do_ref, qt_ref, dot_ref, t_ref, dq_ref, dk_ref, dv_ref):
        i = pl.program_id(2)

        @pl.when(i == 0)
        def _():
            dk_ref[...] = jnp.zeros_like(dk_ref)
            dv_ref[...] = jnp.zeros_like(dv_ref)
        qh = _split_heads(q_ref[...], SCALE)
        doh = _split_heads(do_ref[...])
        top = lax.broadcasted_iota(jnp.int32, (LANES, tq), 0) < HEAD
        zt = jnp.zeros((LANES, tq), BF16)
        qt = qt_ref[...] * SCALE
        qth = (jnp.where(top, qt, zt), jnp.where(top, zt, qt))
        doth = (jnp.where(top, dot_ref[...], zt), jnp.where(top, zt, dot_ref[...]))
        tt = t_ref[...]
        tot = (tt[:, 0:1], tt[:, HEAD:HEAD + 1])
        Urev = _tri(tq, lambda r, c: r > c)
        Uexc = _tri(tq, lambda r, c: r < c)

        def sweep(tiles, carry):
            nt = len(tiles)
            chains = [(t, h) for t in range(nt) for h in range(2)]
            rows = [pl.ds(pl.multiple_of(j * tq, tq), tq) for j, _ in tiles]
            ks = [k_ref[r, :] for r in rows]
            vs = [v_ref[r, :] for r in rows]
            terms = {(t, h): _sb_terms(qh[h], ks[t], tiles[t][1]) for t, h in chains}
            da = {(t, h): _dot(doh[h], vs[t], NT) for t, h in chains}
            cc = [carry[h][0] for h in range(2)]
            later = {}
            for t, h in chains:
                cc[h] = cc[h] + jnp.sum(terms[t, h][1], axis=1, keepdims=True)
                later[t, h] = tot[h] - cc[h]
            cum = {ch: _dot(terms[ch][1].astype(BF16), Urev, NN) for ch in chains}
            a, g, before = {}, {}, {}
            cg = [carry[h][1] for h in range(2)]
            for ch in chains:
                a[ch] = jnp.exp(terms[ch][0] + (cum[ch] + later[ch]))
                if tiles[ch[0]][1]:
                    a[ch] = jnp.where(terms[ch][2], a[ch], 0.0)
                g[ch] = a[ch] * da[ch]
                before[ch] = cg[ch[1]]
                cg[ch[1]] = cg[ch[1]] + jnp.sum(g[ch], axis=1, keepdims=True)
            G = {ch: _dot(g[ch].astype(BF16), Uexc, NN) for ch in chains}
            dz = {}
            for ch in chains:
                d = g[ch] - jnp.exp(terms[ch][0]) * (g[ch] + (G[ch] + before[ch]))
                if tiles[ch[0]][1]:
                    d = jnp.where(terms[ch][2], d, 0.0)
                dz[ch] = d.astype(BF16)
            dq = [carry[h][2] for h in range(2)]
            for t, h in chains:
                dq[h] = dq[h] + _dot(dz[t, h], ks[t], NN)
            for t in range(nt):
                dk_ref[:, rows[t]] += _dot(qth[0], dz[t, 0], NN) + _dot(qth[1], dz[t, 1], NN)
                dv_ref[:, rows[t]] += _dot(doth[0], a[t, 0].astype(BF16), NN) + _dot(doth[1], a[t, 1].astype(BF16), NN)
            return tuple((cc[h], cg[h], dq[h]) for h in range(2))

        zero = jnp.zeros((tq, 1), F32)
        zq = jnp.zeros((tq, LANES), F32)
        step = lambda n, cr: sweep([(SB_UNROLL_BWD * n + t, False) for t in range(SB_UNROLL_BWD)], cr)
        carry = lax.fori_loop(0, i // SB_UNROLL_BWD, step, ((zero, zero, zq), (zero, zero, zq)))
        tails = [lambda cr, k=k: sweep([(i - k + t, False) for t in range(k)] + [(i, True)], cr) for k in range(SB_UNROLL_BWD)]
        carry = _by_value(i % SB_UNROLL_BWD, tails, carry)
        dq_ref[...] = jnp.where(_lo_mask((tq, LANES)), carry[0][2], carry[1][2]) * SCALE

    spec_q = pl.BlockSpec((tq, LANES), lambda b, p, i: (b * nq + i, p))
    spec_t = pl.BlockSpec((LANES, tq), lambda b, p, i: (p, b * nq + i))
    spec_s = pl.BlockSpec((LANES, S), lambda b, p, i: (b * NP + p, 0))
    key_side = jax.ShapeDtypeStruct((B * NQ, S), F32)
    in_specs = [spec_q, pl.BlockSpec((S, LANES), lambda b, p, i: (b, NP + p)),
                pl.BlockSpec((S, LANES), lambda b, p, i: (b, 2 * NP + p)), spec_q, spec_t, spec_t, spec_q]
    out_specs = [spec_q, spec_s, spec_s]
    out_shape = [jax.ShapeDtypeStruct((T, NQ), F32), key_side, key_side]
    args = [qkv, qkv, qkv, do, q_t, do_t, tot]
    return _call_behind(body, name, grid, in_specs, out_specs, out_shape, args, exchange, True)


def _adamw(w, g, m, v, name):
    shape = w.shape
    cols = shape[-1]
    rows = math.prod(shape[:-1])
    tr = _pick(rows, max(8, (1 << 19) // max(cols, LANES) // 8 * 8), 8)

    def body(w_ref, g_ref, m_ref, v_ref, d_ref, mo_ref, vo_ref):
        gv = g_ref[...]
        mn = ADAM_B1 * m_ref[...] + (1.0 - ADAM_B1) * gv
        vn = ADAM_B2 * v_ref[...] + (1.0 - ADAM_B2) * (gv * gv)
        m_hat = mn / (1.0 - ADAM_B1 ** ADAM_STEP)
        v_hat = vn / (1.0 - ADAM_B2 ** ADAM_STEP)
        d_ref[...] = -ADAM_LR * (m_hat / (jnp.sqrt(v_hat) + ADAM_EPS) + ADAM_WD * w_ref[...])
        mo_ref[...] = mn
        vo_ref[...] = vn

    spec = pl.BlockSpec((tr, cols), lambda i: (i, 0))
    out = jax.ShapeDtypeStruct((rows, cols), F32)
    d, mn, vn = pl.pallas_call(
        body, name=name, grid=(rows // tr,),
        in_specs=[spec] * 4, out_specs=[spec] * 3, out_shape=[out] * 3,
        compiler_params=_params("parallel"),
    )(w.reshape(rows, cols), g.reshape(rows, cols), m.reshape(rows, cols), v.reshape(rows, cols))
    return d.reshape(shape), mn.reshape(shape), vn.reshape(shape)


def _pad_rows(a, rows):
    return jnp.pad(a, ((0, rows - a.shape[0]), (0, 0)))


def kernel(x, c, positions, ada_w, ada_b, norm1_g, norm2_g, wqkv_a, q_norm_a, k_norm_a, sinks_a, wo_a, wqkv_b, wo_b, w_gate, w_up, w_down, loss_target, m_ada_w, m_ada_b, m_norm1_g, m_norm2_g, m_wqkv_a, m_q_norm_a, m_k_norm_a, m_sinks_a, m_wo_a, m_wqkv_b, m_wo_b, m_w_gate, m_w_up, m_w_down, v_ada_w, v_ada_b, v_norm1_g, v_norm2_g, v_wqkv_a, v_q_norm_a, v_k_norm_a, v_sinks_a, v_wo_a, v_wqkv_b, v_wo_b, v_w_gate, v_w_up, v_w_down):
    B, S, D = x.shape
    T = B * S
    L = ada_w.shape[0]
    NA, NB_ = wqkv_a.shape[0], wqkv_b.shape[0]
    me = 4 * lax.axis_index("x") + 2 * lax.axis_index("y") + lax.axis_index("c")
    xt = x.reshape(T, D)

    col_sharded = {"qkv_a": wqkv_a, "qkv_b": wqkv_b, "gate": w_gate, "up": w_up}
    row_sharded = {"wo_a": wo_a, "wo_b": wo_b, "down": w_down}

    def shard_rows(key):
        kind, idx = key
        return col_sharded[kind][idx].T if kind in col_sharded else row_sharded[kind][idx]

    def layer_keys(l):
        mix = "a" if l % 2 == 0 else "b"
        return [("qkv_" + mix, l // 2), ("wo_" + mix, l // 2), ("gate", l), ("up", l), ("down", l)]

    def unpack(buf, keys, reshape):
        out, off = {}, 0
        for key in keys:
            rows = shard_rows(key).shape[0]
            out[key] = reshape(buf[..., off:off + rows, :], rows)
            off += rows
        return out

    first_b = 1
    keys_early = layer_keys(0)[:2]
    keys_mid = layer_keys(0)[2:] + [("qkv_b", 0)]
    keys_late = [k for l in range(1, L) for k in layer_keys(l) if k != ("qkv_b", 0)]
    pack = lambda keys: jnp.concatenate([shard_rows(k).astype(BF16) for k in keys], axis=0)
    full_rows = lambda b, rows: b.reshape(NDEV * rows, D)
    W = unpack(_all_gather(pack(keys_early), "ag_weights"), keys_early, full_rows)

    WA = ada_w.shape[2]
    c_all = _all_gather(c, "ag_c").reshape(NDEV * B, D)
    bias = lax.dynamic_slice_in_dim(ada_b, me * WA, WA, axis=1).reshape(L, 1, WA)
    mod_part = _ada_fwd(c_all, ada_w, bias, "ada_fwd")
    mod_all = _all_gather(mod_part.reshape(L * NDEV * B, WA), "ag_mod")
    mod_all = mod_all.reshape(NDEV, L, NDEV * B, WA).transpose(1, 2, 0, 3).reshape(L, NDEV * B, NDEV * WA)
    mod = lax.dynamic_slice_in_dim(mod_all, me * B, B, axis=1)
    mod = mod.reshape(L, B, 6, 1, D)
    sh1, sc1, g1, sh2, sc2, g2 = [mod[:, :, k] for k in range(6)]

    half = ROT // 2
    inv_freq = jnp.power(jnp.float32(ROPE_THETA), -jnp.arange(half, dtype=F32) * 2.0 / ROT)
    ang = positions.reshape(T, 1).astype(F32) * inv_freq[None, :]
    cos, sin = jnp.cos(ang), jnp.sin(ang)
    ones = jnp.ones((T, HEAD - ROT), F32)
    zeros = jnp.zeros((T, HEAD - ROT), F32)
    z8 = jnp.zeros((T, half), F32)
    cs = jnp.tile(jnp.concatenate([cos, cos, ones], axis=1), (1, 2))
    s1 = jnp.tile(jnp.concatenate([-sin, z8, zeros], axis=1), (1, 2))
    s2 = jnp.tile(jnp.concatenate([z8, sin, zeros], axis=1), (1, 2))

    saved = []
    xc = xt
    for l in range(L):
        j = l // 2
        h1 = _norm_mod(xc, norm1_g[l:l + 1], sc1[l], sh1[l], S, f"norm1_{l}")
        sv = dict(x_in=xc, h1=h1)
        if l % 2 == 0:
            qkv = _mm_nt(h1, W["qkv_a", j], F32, f"qkv_a_{l}")
            qg = jnp.tile(q_norm_a[j:j + 1], (1, 2))
            kg = jnp.tile(k_norm_a[j:j + 1], (1, 2))
            qn, kd, vd = _qk_prep(qkv, cs, s1, s2, qg, kg, f"qk_prep_{l}")
            sink2 = jnp.repeat(sinks_a[j].reshape(-1, 2), HEAD, axis=1).reshape(-1, 1, LANES)
            if l == 0:
                attn, lse, mid = _swa_fwd(qn, kd, vd, sink2, B, f"swa_fwd_{l}", gather=pack(keys_mid))
                W.update(unpack(mid, keys_mid, full_rows))
            else:
                attn, lse = _swa_fwd(qn, kd, vd, sink2, B, f"swa_fwd_{l}")
            sv.update(qkv=qkv, qg=qg, kg=kg, qn=qn, kd=kd, vd=vd, sink2=sink2, lse=lse)
            wo = W["wo_a", j]
        else:
            qkv = _mm_nt(h1, W["qkv_b", j], BF16, f"qkv_b_{l}")
            if l == first_b:
                attn, tot, late = _sb_fwd(qkv, B, f"sb_fwd_{l}", gather=pack(keys_late))
                W.update(unpack(late, keys_late, full_rows))
            else:
                attn, tot = _sb_fwd(qkv, B, f"sb_fwd_{l}")
            sv.update(qkv=qkv, tot=tot)
            wo = W["wo_b", j]
        y1, xm = _mm_res(attn, wo, xc, g1[l], S, f"attn_out_{l}")
        h2 = _norm_mod(xm, norm2_g[l:l + 1], sc2[l], sh2[l], S, f"norm2_{l}")
        gate, up, act = _swiglu_fwd(h2, W["gate", l], W["up", l], f"swiglu_fwd_{l}")
        y2, xc = _mm_res(act, W["down", l], xm, g2[l], S, f"mlp_out_{l}")
        sv.update(attn=attn, y1=y1, x_mid=xm, h2=h2, gate=gate, up=up, act=act, y2=y2)
        saved.append(sv)

    dx, loss_tile = _loss_head(xc, loss_target.reshape(T, D), "loss_head")

    G = {}
    pack_grads = lambda keys: jnp.concatenate([G[k].reshape(NDEV, G[k].shape[0] // NDEV, D) for k in keys], axis=1)
    keys_hi = [k for l in range(first_b + 1, L) for k in layer_keys(l)] + layer_keys(first_b)[1:]
    keys_mid_g = [("qkv_b", 0)] + layer_keys(0)[1:]
    keys_lo = layer_keys(0)[:1]
    received_hi = received_mid = None
    dmod = [None] * L
    dn1, dn2 = [None] * L, [None] * L
    dqg, dkg, dsink = [None] * NA, [None] * NA, [None] * NA
    dy2, dg2 = _gate_bwd(dx, saved[L - 1]["y2"], g2[L - 1], S, "gate2_bwd_top")
    for l in reversed(range(L)):
        j = l // 2
        mix = "a" if l % 2 == 0 else "b"
        sv = saved[l]
        dgate, dup = _swiglu_bwd(dy2, W["down", l], sv["gate"], sv["up"], f"swiglu_bwd_{l}")
        G["down", l] = _mm_tn(sv["act"], dy2, f"dw_down_{l}")
        G["gate", l] = _mm_tn(dgate, sv["h2"], f"dw_gate_{l}")
        G["up", l] = _mm_tn(dup, sv["h2"], f"dw_up_{l}")
        dxm, dsh2, dsc2, dn2[l], dy1, dg1 = _norm_mod_bwd(sv["x_mid"], [(dgate, W["gate", l]), (dup, W["up", l])], dx,
                                                         norm2_g[l:l + 1], sc2[l], S, f"norm2_bwd_{l}", below=(sv["y1"], g1[l]))
        dattn = _mm_nt(dy1, W["wo_" + mix, j], BF16, f"dattn_{l}")
        G["wo_" + mix, j] = _mm_tn(sv["attn"], dy1, f"dw_o_{l}")
        if l % 2 == 0:
            swa_args = (sv["qn"], sv["kd"], sv["vd"], sv["sink2"], dattn, sv["lse"], B, f"swa_bwd_{l}")
            if l == 0:
                dq, dkc, dkp, dvc, dvp, dsink[j], received_mid = _swa_bwd(*swa_args, exchange=pack_grads(keys_mid_g))
            else:
                dq, dkc, dkp, dvc, dvp, dsink[j] = _swa_bwd(*swa_args)
            dqkv, dqg[j], dkg[j] = _qk_prep_bwd(sv["qkv"], cs, s1, s2, sv["qg"], sv["kg"], dq, dkc, dkp, dvc, dvp, B,
                                                f"qk_prep_bwd_{l}")
        else:
            nqb = sv["qkv"].shape[1] // 3
            sb_args = (sv["qkv"], sv["qkv"][:, :nqb].T, dattn, dattn.T, sv["tot"], B, f"sb_bwd_{l}")
            if l == first_b and keys_hi:
                dq, dk_t, dv_t, received_hi = _sb_bwd(*sb_args, exchange=pack_grads(keys_hi))
            else:
                dq, dk_t, dv_t = _sb_bwd(*sb_args)
            dk, dv = [t.reshape(B, nqb, S).transpose(0, 2, 1).reshape(T, nqb) for t in (dk_t, dv_t)]
            dqkv = jnp.concatenate([dq, dk, dv], axis=1).astype(BF16)
        G["qkv_" + mix, j] = _mm_tn(dqkv, sv["h1"], f"dw_qkv_{l}")
        n1_args = (sv["x_in"], [(dqkv, W["qkv_" + mix, j])], dxm, norm1_g[l:l + 1], sc1[l], S, f"norm1_bwd_{l}")
        dmod_l = [None, None, dg1, dsh2, dsc2, dg2]
        if l > 0:
            dx, dmod_l[0], dmod_l[1], dn1[l], dy2, dg2 = _norm_mod_bwd(*n1_args, below=(saved[l - 1]["y2"], g2[l - 1]))
        else:
            dx, dmod_l[0], dmod_l[1], dn1[l] = _norm_mod_bwd(*n1_args)
        dmod[l] = jnp.concatenate(dmod_l, axis=1)
    grad_x = dx.reshape(B, S, D)

    ndm = L * 6
    dmod_rows = jnp.stack(dmod, axis=1).reshape(B * ndm, D)
    misc = jnp.concatenate(
        [jnp.concatenate(dn1, axis=0).reshape(B * L, D), jnp.concatenate(dn2, axis=0).reshape(B * L, D),
         _pad_rows(jnp.concatenate([jnp.pad(r, ((0, 0), (0, D - LANES))) for r in dqg + dkg]
                                   + [jnp.pad(r[:, 0, ::HEAD].reshape(1, -1), ((0, 0), (0, D - 2 * r.shape[0]))) for r in dsink]
                                   + [jnp.pad(loss_tile[0:1, 0:1], ((0, 0), (0, D - 1)))], axis=0), 8)], axis=0)
    nmisc = misc.shape[0]
    small = _all_gather(jnp.concatenate([dmod_rows, _pad_rows(misc, -(-nmisc // 8) * 8)], axis=0), "ag_small")
    dmod_all = small[:, :B * ndm].reshape(NDEV * B, ndm, D)
    g_ada_b = _sum_leading(dmod_all, "sum_dmod").reshape(L, 6 * D)
    misc_sum = _sum_leading(small[:, B * ndm:], "sum_misc")
    g_n1 = misc_sum[0:B * L].reshape(L, B, D)
    g_n2 = misc_sum[B * L:2 * B * L].reshape(L, B, D)
    g_norm1 = _sum_leading(g_n1.transpose(1, 0, 2), "sum_n1")
    g_norm2 = _sum_leading(g_n2.transpose(1, 0, 2), "sum_n2")
    o = 2 * B * L
    g_qn = misc_sum[o:o + NA, :HEAD]
    g_kn = misc_sum[o + NA:o + 2 * NA, :HEAD]
    nsink = sinks_a.shape[1]
    g_sink = misc_sum[o + 2 * NA:o + 3 * NA, :nsink]
    loss = misc_sum[o + 3 * NA, 0]

    dmod_loc = lax.dynamic_slice_in_dim(dmod_all.reshape(NDEV * B, L, 6 * D), me * WA, WA, axis=2)
    g_ada_w = _ada_bwd(c_all, dmod_loc.transpose(1, 0, 2), "ada_bwd")

    shard = unpack(_sum_leading(_exchange(pack_grads(keys_lo), "grad_exchange"), "grad_sum"), keys_lo, lambda b, rows: b)
    shard.update(unpack(_sum_leading(received_mid, "grad_sum_mid"), keys_mid_g, lambda b, rows: b))
    if received_hi is not None:
        shard.update(unpack(_sum_leading(received_hi, "grad_sum_hi"), keys_hi, lambda b, rows: b))

    def stacked(kind, n):
        return jnp.stack([shard[kind, i].T if kind in col_sharded else shard[kind, i] for i in range(n)])

    gw_qkv_a, gw_qkv_b, gw_gate, gw_up = stacked("qkv_a", NA), stacked("qkv_b", NB_), stacked("gate", L), stacked("up", L)
    gw_wo_a, gw_wo_b, gw_down = stacked("wo_a", NA), stacked("wo_b", NB_), stacked("down", L)

    grads = [g_ada_w, g_ada_b, g_norm1, g_norm2, gw_qkv_a, g_qn, g_kn, g_sink, gw_wo_a, gw_qkv_b, gw_wo_b,
             gw_gate, gw_up, gw_down]
    ws = [ada_w, ada_b, norm1_g, norm2_g, wqkv_a, q_norm_a, k_norm_a, sinks_a, wo_a, wqkv_b, wo_b, w_gate, w_up, w_down]
    ms = [m_ada_w, m_ada_b, m_norm1_g, m_norm2_g, m_wqkv_a, m_q_norm_a, m_k_norm_a, m_sinks_a, m_wo_a, m_wqkv_b,
          m_wo_b, m_w_gate, m_w_up, m_w_down]
    vs = [v_ada_w, v_ada_b, v_norm1_g, v_norm2_g, v_wqkv_a, v_q_norm_a, v_k_norm_a, v_sinks_a, v_wo_a, v_wqkv_b,
          v_wo_b, v_w_gate, v_w_up, v_w_down]
    deltas, new_m, new_v = [], [], []
    for k, (w, g, m, v) in enumerate(zip(ws, grads, ms, vs)):
        g = g.reshape(w.shape)
        d, mn, vn = _adamw(w, g, m, v, f"adamw_{k}")
        grads[k] = g
        deltas.append(d)
        new_m.append(mn)
        new_v.append(vn)
    return (loss, grad_x, *grads, *deltas, *new_m, *new_v)
```

```python
import functools
import math

import jax
import jax.numpy as jnp
from jax import lax
from jax.experimental import pallas as pl
from jax.experimental.pallas import tpu as pltpu

F32 = jnp.float32
BF16 = jnp.bfloat16
NDEV = 8
HEAD = 64
BLK = 128
LANES = 128
EPS = 1e-6
ROT = HEAD // 4
ROPE_THETA = 500000.0
SCALE = HEAD ** -0.5
NEG = -1e30
VMEM_LIMIT = 56 * 1024 * 1024
MESH = pl.DeviceIdType.MESH
HIGH = lax.Precision.HIGHEST

ADAM_LR = 0.001
ADAM_B1 = 0.9
ADAM_B2 = 0.999
ADAM_EPS = 1e-08
ADAM_WD = 0.01
ADAM_STEP = 10


def _params(*sem):
    return pltpu.CompilerParams(dimension_semantics=sem, vmem_limit_bytes=VMEM_LIMIT)


def _pick(n, cap, mult):
    if n <= cap:
        return n
    best = None
    for t in range(mult, cap + 1, mult):
        if n % t == 0:
            best = t
    assert best is not None, (n, cap, mult)
    return best


def _dot(a, b, dims, precision=None):
    return lax.dot_general(a, b, (dims, ((), ())), preferred_element_type=F32, precision=precision)


NN = ((1,), (0,))
NT = ((1,), (1,))
TN = ((0,), (0,))


def _all_gather(x, name):
    m, n = x.shape

    def body(x_ref, out_ref, send_sems, recv_sems, local_sem):
        ix, iy, ic = lax.axis_index("x"), lax.axis_index("y"), lax.axis_index("c")
        me, sibling = (ix, iy, ic), (ix, iy, 1 - ic)
        chips = [(1 - ix, iy), (ix, 1 - iy), (1 - ix, 1 - iy)]

        def slab(px, py, pc):
            return out_ref.at[4 * px + 2 * py + pc]

        def copy(k, block, to, src=None):
            return pltpu.make_async_remote_copy(
                src_ref=slab(*block) if src is None else src, dst_ref=slab(*block),
                send_sem=send_sems.at[k], recv_sem=recv_sems.at[k], device_id=to, device_id_type=MESH)

        mine = pltpu.make_async_copy(x_ref, slab(*me), local_sem)
        mine.start()
        first = [copy(0, me, sibling, src=x_ref)]
        first += [copy(1 + j, me, (*chip, ic), src=x_ref) for j, chip in enumerate(chips)]
        for cp in first:
            cp.start()
        passed = [copy(4 + j, (*chip, ic), sibling) for j, chip in enumerate(chips)]
        for j, chip in enumerate(chips):
            copy(1 + j, (*chip, ic), me).wait_recv()
            passed[j].start()
        copy(0, sibling, me).wait_recv()
        for j, chip in enumerate(chips):
            copy(4 + j, (*chip, 1 - ic), me).wait_recv()
        for cp in first + passed:
            cp.wait_send()
        mine.wait()

    return pl.pallas_call(
        body, name=name,
        out_shape=jax.ShapeDtypeStruct((NDEV, m, n), x.dtype),
        in_specs=[pl.BlockSpec(memory_space=pl.ANY)],
        out_specs=pl.BlockSpec(memory_space=pl.ANY),
        scratch_shapes=[pltpu.SemaphoreType.DMA((7,)), pltpu.SemaphoreType.DMA((7,)), pltpu.SemaphoreType.DMA(())],
    )(x)


COMM_SEMS = [pltpu.SemaphoreType.DMA((NDEV - 1,)), pltpu.SemaphoreType.DMA((NDEV - 1,)), pltpu.SemaphoreType.DMA(())]
HBM_SPEC = pl.BlockSpec(memory_space=pl.ANY)


def _direct_copies(src_ref, dst_ref, sems, scatter):
    send_sems, recv_sems, own_sem = sems
    ix, iy, ic = lax.axis_index("x"), lax.axis_index("y"), lax.axis_index("c")
    me = 4 * ix + 2 * iy + ic
    copies = [pltpu.make_async_copy(src_ref.at[me] if scatter else src_ref, dst_ref.at[me], own_sem)]
    for k in range(1, NDEV):
        px = 1 - ix if k & 4 else ix
        py = 1 - iy if k & 2 else iy
        pc = 1 - ic if k & 1 else ic
        copies.append(pltpu.make_async_remote_copy(
            src_ref=src_ref.at[4 * px + 2 * py + pc] if scatter else src_ref, dst_ref=dst_ref.at[me],
            send_sem=send_sems.at[k - 1], recv_sem=recv_sems.at[k - 1],
            device_id=(px, py, pc), device_id_type=MESH))
    return copies


def _exchange(p, name):
    def body(p_ref, r_ref, *sems):
        copies = _direct_copies(p_ref, r_ref, sems, True)
        for cp in copies:
            cp.start()
        for cp in copies:
            cp.wait()

    return pl.pallas_call(
        body, name=name,
        out_shape=jax.ShapeDtypeStruct(p.shape, p.dtype),
        in_specs=[HBM_SPEC], out_specs=HBM_SPEC, scratch_shapes=COMM_SEMS,
    )(p)


def _call_behind(body, name, grid, in_specs, out_specs, out_shape, args, payload=None, scatter=False):
    params = _params("arbitrary", "arbitrary", "arbitrary")
    if payload is None:
        return pl.pallas_call(body, name=name, grid=grid, in_specs=in_specs, out_specs=out_specs, out_shape=out_shape,
                              compiler_params=params)(*args)
    n_in, n_out = len(in_specs), len(out_specs)

    def edge(first):
        ids = [pl.program_id(a) for a in range(3)]
        return functools.reduce(lambda u, v: u & v, [i == (0 if first else d - 1) for i, d in zip(ids, grid)])

    def wrapped(*refs):
        x_ref, r_ref, sems = refs[n_in], refs[n_in + 1 + n_out], refs[n_in + n_out + 2:]

        @pl.when(edge(True))
        def _():
            for cp in _direct_copies(x_ref, r_ref, sems, scatter):
                cp.start()
        body(*refs[:n_in], *refs[n_in + 1:n_in + 1 + n_out])

        @pl.when(edge(False))
        def _():
            for cp in _direct_copies(x_ref, r_ref, sems, scatter):
                cp.wait()

    arrived = jax.ShapeDtypeStruct(payload.shape if scatter else (NDEV,) + payload.shape, payload.dtype)
    return pl.pallas_call(
        wrapped, name=name, grid=grid, in_specs=list(in_specs) + [HBM_SPEC], out_specs=list(out_specs) + [HBM_SPEC],
        out_shape=list(out_shape) + [arrived], scratch_shapes=COMM_SEMS, compiler_params=params,
    )(*args, payload)


def _sum_leading(r, name):
    k, m, n = r.shape
    mult = 8 * (4 // r.dtype.itemsize)
    tm = _pick(m, max(mult, (4 * 1024 * 1024) // (k * n * r.dtype.itemsize) // mult * mult), mult)

    def body(r_ref, o_ref):
        acc = r_ref[0].astype(F32)
        for s in range(1, k):
            acc = acc + r_ref[s].astype(F32)
        o_ref[...] = acc

    return pl.pallas_call(
        body, name=name, grid=(m // tm,),
        in_specs=[pl.BlockSpec((k, tm, n), lambda i: (0, i, 0))],
        out_specs=pl.BlockSpec((tm, n), lambda i: (i, 0)),
        out_shape=jax.ShapeDtypeStruct((m, n), F32),
        compiler_params=_params("parallel"),
    )(r)


def _mm_nt(a, bt, out_dtype, name):
    M, K = a.shape
    N = bt.shape[0]
    tm, tn = _pick(M, 512, 8), _pick(N, 1536, LANES)

    def body(a_ref, b_ref, o_ref):
        o_ref[...] = _dot(a_ref[...], b_ref[...], NT).astype(out_dtype)

    return pl.pallas_call(
        body, name=name, grid=(N // tn, M // tm),
        in_specs=[pl.BlockSpec((tm, K), lambda j, i: (i, 0)), pl.BlockSpec((tn, K), lambda j, i: (j, 0))],
        out_specs=pl.BlockSpec((tm, tn), lambda j, i: (i, j)),
        out_shape=jax.ShapeDtypeStruct((M, N), out_dtype),
        compiler_params=_params("parallel", "parallel"),
    )(a, bt)


def _mm_tn(a, b, name):
    M, N1 = a.shape
    N2 = b.shape[1]
    t1, tk = _pick(N1, 1536, LANES), _pick(M, 512, 8)
    nk = M // tk

    def body(a_ref, b_ref, o_ref, acc_ref):
        k = pl.program_id(1)

        @pl.when(k == 0)
        def _():
            acc_ref[...] = jnp.zeros_like(acc_ref)
        acc_ref[...] += _dot(a_ref[...], b_ref[...], TN)

        @pl.when(k == nk - 1)
        def _():
            o_ref[...] = acc_ref[...].astype(BF16)

    return pl.pallas_call(
        body, name=name, grid=(N1 // t1, nk),
        in_specs=[pl.BlockSpec((tk, t1), lambda i, k: (k, i)), pl.BlockSpec((tk, N2), lambda i, k: (k, 0))],
        out_specs=pl.BlockSpec((t1, N2), lambda i, k: (i, 0)),
        out_shape=jax.ShapeDtypeStruct((N1, N2), BF16),
        scratch_shapes=[pltpu.VMEM((t1, N2), F32)],
        compiler_params=_params("parallel", "arbitrary"),
    )(a, b)


def _norm_mod_rows(xv, gain, sc, sh):
    r = lax.rsqrt(jnp.mean(xv * xv, axis=-1, keepdims=True) + EPS)
    return ((xv * r) * gain * (1.0 + sc) + sh).astype(BF16)


def _mm_res(a, w, x, gate, S, name, norm=None):
    T, K = a.shape
    D = w.shape[1]
    tm = _pick(S, 512, 8)
    nb = S // tm

    def body(a_ref, w_ref, x_ref, g_ref, *rest):
        y = _dot(a_ref[...], w_ref[...], NN)
        xn = x_ref[...] + g_ref[0] * y
        if norm is None:
            y_ref, o_ref = rest
        else:
            gain_ref, sc_ref, sh_ref, y_ref, o_ref, h_ref = rest
            h_ref[...] = _norm_mod_rows(xn, gain_ref[...], sc_ref[0], sh_ref[0])
        y_ref[...] = y.astype(BF16)
        o_ref[...] = xn

    spec_t = pl.BlockSpec((tm, D), lambda i: (i, 0))
    spec_b = pl.BlockSpec((1, 1, D), lambda i: (i // nb, 0, 0))
    in_specs = [pl.BlockSpec((tm, K), lambda i: (i, 0)), pl.BlockSpec((K, D), lambda i: (0, 0)), spec_t, spec_b]
    out_specs = [spec_t, spec_t]
    out_shape = [jax.ShapeDtypeStruct((T, D), BF16), jax.ShapeDtypeStruct((T, D), F32)]
    args = [a, w, x, gate]
    if norm is not None:
        in_specs += [pl.BlockSpec((1, D), lambda i: (0, 0)), spec_b, spec_b]
        out_specs.append(spec_t)
        out_shape.append(jax.ShapeDtypeStruct((T, D), BF16))
        args += list(norm)
    return pl.pallas_call(
        body, name=name, grid=(T // tm,), in_specs=in_specs, out_specs=out_specs, out_shape=out_shape,
        compiler_params=_params("parallel"),
    )(*args)


def _swiglu_fwd(h, wgt, wut, name):
    T, D = h.shape
    F = wgt.shape[0]
    tm, tn = _pick(T, 512, 8), _pick(F, 1536, LANES)

    def body(h_ref, g_ref, u_ref, go_ref, uo_ref, a_ref):
        hh = h_ref[...]
        g = _dot(hh, g_ref[...], NT)
        u = _dot(hh, u_ref[...], NT)
        go_ref[...] = g.astype(BF16)
        uo_ref[...] = u.astype(BF16)
        a_ref[...] = (g * jax.nn.sigmoid(g) * u).astype(BF16)

    spec_w = pl.BlockSpec((tn, D), lambda j, i: (j, 0))
    spec_o = pl.BlockSpec((tm, tn), lambda j, i: (i, j))
    out = jax.ShapeDtypeStruct((T, F), BF16)
    return pl.pallas_call(
        body, name=name, grid=(F // tn, T // tm),
        in_specs=[pl.BlockSpec((tm, D), lambda j, i: (i, 0)), spec_w, spec_w],
        out_specs=[spec_o, spec_o, spec_o],
        out_shape=[out, out, out],
        compiler_params=_params("parallel", "parallel"),
    )(h, wgt, wut)


def _swiglu_bwd(dy, wd, gate, up, name):
    T, D = dy.shape
    F = wd.shape[0]
    tm, tn = _pick(T, 512, 8), _pick(F, 1536, LANES)

    def body(dy_ref, w_ref, g_ref, u_ref, dg_ref, du_ref):
        da = _dot(dy_ref[...], w_ref[...], NT)
        g = g_ref[...].astype(F32)
        sg = jax.nn.sigmoid(g)
        silu = g * sg
        du_ref[...] = (da * silu).astype(BF16)
        dg_ref[...] = (da * u_ref[...].astype(F32) * (sg + silu * (1.0 - sg))).astype(BF16)

    spec_o = pl.BlockSpec((tm, tn), lambda j, i: (i, j))
    return pl.pallas_call(
        body, name=name, grid=(F // tn, T // tm),
        in_specs=[pl.BlockSpec((tm, D), lambda j, i: (i, 0)), pl.BlockSpec((tn, D), lambda j, i: (j, 0)), spec_o, spec_o],
        out_specs=[spec_o, spec_o],
        out_shape=[jax.ShapeDtypeStruct((T, F), BF16), jax.ShapeDtypeStruct((T, F), BF16)],
        compiler_params=_params("parallel", "parallel"),
    )(dy, wd, gate, up)


def _norm_mod(x, gain, sc, sh, S, name):
    T, D = x.shape
    tm = _pick(S, 512, 8)
    nb = S // tm

    def body(x_ref, g_ref, sc_ref, sh_ref, o_ref):
        o_ref[...] = _norm_mod_rows(x_ref[...], g_ref[...], sc_ref[0], sh_ref[0])

    spec_b = pl.BlockSpec((1, 1, D), lambda i: (i // nb, 0, 0))
    return pl.pallas_call(
        body, name=name, grid=(T // tm,),
        in_specs=[pl.BlockSpec((tm, D), lambda i: (i, 0)), pl.BlockSpec((1, D), lambda i: (0, 0)), spec_b, spec_b],
        out_specs=pl.BlockSpec((tm, D), lambda i: (i, 0)),
        out_shape=jax.ShapeDtypeStruct((T, D), BF16),
        compiler_params=_params("parallel"),
    )(x, gain, sc, sh)


def _norm_mod_bwd(x, pairs, dres, gain, sc, S, name, below=None):
    T, D = x.shape
    B = T // S
    tm = _pick(S, 512, 8)
    nb = S // tm
    n_mm = 2 * len(pairs)

    def body(*refs):
        mm, (x_ref, dr_ref, g_ref, sc_ref), rest = refs[:n_mm], refs[n_mm:n_mm + 4], refs[n_mm + 4:]
        if below is None:
            o_ref, dsh_ref, dsc_ref, dg_ref = rest
            sums = [dsh_ref, dsc_ref, dg_ref]
        else:
            y_ref, gt_ref, o_ref, dsh_ref, dsc_ref, dg_ref, dy_ref, dgt_ref = rest
            sums = [dsh_ref, dsc_ref, dg_ref, dgt_ref]

        @pl.when(pl.program_id(1) == 0)
        def _():
            for ref in sums:
                ref[...] = jnp.zeros_like(ref)
        dhv = _dot(mm[0][...], mm[1][...], NN)
        for p in range(2, n_mm, 2):
            dhv = dhv + _dot(mm[p][...], mm[p + 1][...], NN)
        xv, g = x_ref[...], g_ref[...]
        r = lax.rsqrt(jnp.mean(xv * xv, axis=-1, keepdims=True) + EPS)
        xhat = xv * r
        dsh_ref[0] += jnp.sum(dhv, axis=0, keepdims=True)
        dsc_ref[0] += jnp.sum(dhv * (xhat * g), axis=0, keepdims=True)
        dn = dhv * (1.0 + sc_ref[0])
        dg_ref[0] += jnp.sum(dn * xhat, axis=0, keepdims=True)
        dxh = dn * g
        out = dr_ref[...] + r * (dxh - xhat * jnp.mean(dxh * xhat, axis=-1, keepdims=True))
        o_ref[...] = out
        if below is not None:
            dy_ref[...] = (out * gt_ref[0]).astype(BF16)
            dgt_ref[0] += jnp.sum(out * y_ref[...].astype(F32), axis=0, keepdims=True)

    spec_t = pl.BlockSpec((tm, D), lambda b, i: (b * nb + i, 0))
    spec_b = pl.BlockSpec((1, 1, D), lambda b, i: (b, 0, 0))
    red = jax.ShapeDtypeStruct((B, 1, D), F32)
    in_specs, args = [], []
    for a, w in pairs:
        K = a.shape[1]
        in_specs += [pl.BlockSpec((tm, K), lambda b, i: (b * nb + i, 0)),
                     pl.BlockSpec((K, D), lambda b, i: (0, 0), pipeline_mode=pl.Buffered(1))]
        args += [a, w]
    in_specs += [spec_t, spec_t, pl.BlockSpec((1, D), lambda b, i: (0, 0)), spec_b]
    args += [x, dres, gain, sc]
    out_specs = [spec_t, spec_b, spec_b, spec_b]
    out_shape = [jax.ShapeDtypeStruct((T, D), F32), red, red, red]
    if below is not None:
        in_specs += [spec_t, spec_b]
        out_specs += [spec_t, spec_b]
        out_shape += [jax.ShapeDtypeStruct((T, D), BF16), red]
        args += list(below)
    return pl.pallas_call(
        body, name=name, grid=(B, nb), in_specs=in_specs, out_specs=out_specs, out_shape=out_shape,
        compiler_params=_params("parallel", "arbitrary"),
    )(*args)


def _gate_bwd(dx, y, gate, S, name):
    T, D = dx.shape
    B = T // S
    tm = _pick(S, 512, 8)
    nb = S // tm

    def body(dx_ref, y_ref, g_ref, dy_ref, dg_ref):
        @pl.when(pl.program_id(1) == 0)
        def _():
            dg_ref[...] = jnp.zeros_like(dg_ref)
        d = dx_ref[...]
        dy_ref[...] = (d * g_ref[0]).astype(BF16)
        dg_ref[0] += jnp.sum(d * y_ref[...].astype(F32), axis=0, keepdims=True)

    spec_t = pl.BlockSpec((tm, D), lambda b, i: (b * nb + i, 0))
    spec_b = pl.BlockSpec((1, 1, D), lambda b, i: (b, 0, 0))
    return pl.pallas_call(
        body, name=name, grid=(B, nb),
        in_specs=[spec_t, spec_t, spec_b],
        out_specs=[spec_t, spec_b],
        out_shape=[jax.ShapeDtypeStruct((T, D), BF16), jax.ShapeDtypeStruct((B, 1, D), F32)],
        compiler_params=_params("parallel", "arbitrary"),
    )(dx, y, gate)


def _loss_head(y, target, name):
    T, D = y.shape
    tm = _pick(T, 512, 8)

    def body(y_ref, t_ref, dy_ref, l_ref):
        @pl.when(pl.program_id(0) == 0)
        def _():
            l_ref[...] = jnp.zeros_like(l_ref)
        e = y_ref[...] - t_ref[...]
        dy_ref[...] = e * (1.0 / D)
        l_ref[...] += 0.5 * jnp.sum(jnp.mean(e * e, axis=-1, keepdims=True), axis=0, keepdims=True)

    spec = pl.BlockSpec((tm, D), lambda i: (i, 0))
    return pl.pallas_call(
        body, name=name, grid=(T // tm,),
        in_specs=[spec, spec],
        out_specs=[spec, pl.BlockSpec((8, LANES), lambda i: (0, 0))],
        out_shape=[jax.ShapeDtypeStruct((T, D), F32), jax.ShapeDtypeStruct((8, LANES), F32)],
        compiler_params=_params("arbitrary"),
    )(y, target)


def _ada_fwd(c_all, ada_w, bias, name):
    NB, D = c_all.shape
    L, _, W = ada_w.shape

    def body(c_ref, w_ref, b_ref, o_ref):
        cv = c_ref[...]
        cond = cv * jax.nn.sigmoid(cv)
        o_ref[0] = _dot(cond, w_ref[0], NN, HIGH) + b_ref[0]

    return pl.pallas_call(
        body, name=name, grid=(L,),
        in_specs=[pl.BlockSpec((NB, D), lambda l: (0, 0)), pl.BlockSpec((1, D, W), lambda l: (l, 0, 0)),
                  pl.BlockSpec((1, 1, W), lambda l: (l, 0, 0))],
        out_specs=pl.BlockSpec((1, NB, W), lambda l: (l, 0, 0)),
        out_shape=jax.ShapeDtypeStruct((L, NB, W), F32),
        compiler_params=_params("parallel"),
    )(c_all, ada_w, bias)


def _ada_bwd(c_all, dmod, name):
    NB, D = c_all.shape
    L, _, W = dmod.shape

    def body(c_ref, d_ref, o_ref):
        cv = c_ref[...]
        cond = cv * jax.nn.sigmoid(cv)
        o_ref[0] = _dot(cond, d_ref[0], TN, HIGH)

    return pl.pallas_call(
        body, name=name, grid=(L,),
        in_specs=[pl.BlockSpec((NB, D), lambda l: (0, 0)), pl.BlockSpec((1, NB, W), lambda l: (l, 0, 0))],
        out_specs=pl.BlockSpec((1, D, W), lambda l: (l, 0, 0)),
        out_shape=jax.ShapeDtypeStruct((L, D, W), F32),
        compiler_params=_params("parallel"),
    )(c_all, dmod)


def _lo_mask(shape):
    return lax.broadcasted_iota(jnp.int32, shape, len(shape) - 1) < HEAD


def _head_sum_matrix():
    r = lax.broadcasted_iota(jnp.int32, (LANES, LANES), 0) // HEAD
    c = lax.broadcasted_iota(jnp.int32, (LANES, LANES), 1) // HEAD
    return (r == c).astype(BF16)


def _head_sum(x, P):
    hi = x.astype(BF16)
    lo = (x - hi.astype(F32)).astype(BF16)
    return _dot(hi, P, NN) + _dot(lo, P, NN)


def _rope(y, cs, s1, s2):
    return y * cs + pltpu.roll(y, LANES - ROT // 2, 1) * s1 + pltpu.roll(y, ROT // 2, 1) * s2


def _rope_bwd(d, cs, s1, s2):
    return d * cs + pltpu.roll(d * s1, ROT // 2, 1) + pltpu.roll(d * s2, LANES - ROT // 2, 1)


def _qk_prep(qkv, cs, s1, s2, qg, kg, name):
    T, W = qkv.shape
    NQ = W - 2 * LANES
    tm = _pick(T, 512, 8)

    def body(x_ref, cs_ref, s1_ref, s2_ref, qg_ref, kg_ref, q_ref, k_ref, v_ref):
        P = _head_sum_matrix()
        cs_, s1_, s2_ = cs_ref[...], s1_ref[...], s2_ref[...]
        lo = _lo_mask((tm, LANES))

        def norm_rope(xv, g):
            ms = _head_sum(xv * xv, P) * (1.0 / HEAD)
            return _rope(xv * lax.rsqrt(ms + EPS) * g, cs_, s1_, s2_)

        for j in range(NQ // LANES):
            q_ref[:, j * LANES:(j + 1) * LANES] = norm_rope(x_ref[:, j * LANES:(j + 1) * LANES], qg_ref[...]).astype(BF16)
        kr = norm_rope(x_ref[:, NQ:NQ + LANES], kg_ref[...])
        ks = pltpu.roll(kr, HEAD, 1)
        k_ref[:, :LANES] = jnp.where(lo, kr, ks).astype(BF16)
        k_ref[:, LANES:] = jnp.where(lo, ks, kr).astype(BF16)
        vr = x_ref[:, NQ + LANES:]
        vs = pltpu.roll(vr, HEAD, 1)
        v_ref[:, :LANES] = jnp.where(lo, vr, vs).astype(BF16)
        v_ref[:, LANES:] = jnp.where(lo, vs, vr).astype(BF16)

    spec_t = pl.BlockSpec((tm, LANES), lambda i: (i, 0))
    spec_g = pl.BlockSpec((1, LANES), lambda i: (0, 0))
    return pl.pallas_call(
        body, name=name, grid=(T // tm,),
        in_specs=[pl.BlockSpec((tm, W), lambda i: (i, 0)), spec_t, spec_t, spec_t, spec_g, spec_g],
        out_specs=[pl.BlockSpec((tm, NQ), lambda i: (i, 0)), pl.BlockSpec((tm, 2 * LANES), lambda i: (i, 0)),
                   pl.BlockSpec((tm, 2 * LANES), lambda i: (i, 0))],
        out_shape=[jax.ShapeDtypeStruct((T, NQ), BF16), jax.ShapeDtypeStruct((T, 2 * LANES), BF16),
                   jax.ShapeDtypeStruct((T, 2 * LANES), BF16)],
        compiler_params=_params("parallel"),
    )(qkv, cs, s1, s2, qg, kg)


def _stack_heads(x2):
    lo = _lo_mask(x2.shape)
    z = jnp.zeros_like(x2)
    return jnp.concatenate([jnp.where(lo, x2, z), jnp.where(lo, z, x2)], axis=0)


def _unstack_heads(xs):
    r = xs.shape[0] // 2
    return jnp.where(_lo_mask((r, LANES)), xs[:r], xs[r:])


def _swa_valid(i):
    qo = lax.broadcasted_iota(jnp.int32, (2 * BLK, 2 * BLK), 0) % BLK
    kc_ = lax.broadcasted_iota(jnp.int32, (2 * BLK, 2 * BLK), 1)
    rel = qo + BLK - kc_
    return (rel >= 0) & (rel < BLK) & ((kc_ >= BLK) | (i > 0))


def _swa_scores(q2, kk, sink2, valid):
    qs = _stack_heads(q2) * SCALE
    s = _dot(qs, kk, NT)
    sk = jnp.concatenate([jnp.broadcast_to(sink2[:, 0:1], (BLK, 1)), jnp.broadcast_to(sink2[:, HEAD:HEAD + 1], (BLK, 1))], axis=0)
    return qs, jnp.where(valid, s, NEG), sk


def _swa_fwd(q, kd, vd, sink2, B, name, gather=None):
    T, NQ = q.shape
    NP = NQ // LANES
    nq = T // B // BLK
    NG = kd.shape[1] // LANES
    grp = NP // NG

    def body(q_ref, kp_ref, kc_ref, vp_ref, vc_ref, s_ref, o_ref, l_ref):
        valid = _swa_valid(pl.program_id(2))
        kk = jnp.concatenate([kp_ref[...], kc_ref[...]], axis=0)
        vs = _stack_heads(jnp.concatenate([vp_ref[...], vc_ref[...]], axis=0))
        sls = [slice(jj * LANES, (jj + 1) * LANES) for jj in range(grp)]
        sc = [_swa_scores(q_ref[:, sl], kk, s_ref[jj], valid) for jj, sl in enumerate(sls)]
        ms = [jnp.maximum(jnp.max(s, axis=1, keepdims=True), sk) for _, s, sk in sc]
        ps = [jnp.exp(s - m) for (_, s, _), m in zip(sc, ms)]
        ls = [jnp.sum(p, axis=1, keepdims=True) + jnp.exp(sk - m) for p, (_, _, sk), m in zip(ps, sc, ms)]
        ps = [(p * (1.0 / l)).astype(BF16) for p, l in zip(ps, ls)]
        os_ = [_dot(jnp.concatenate([p[:BLK], p[BLK:]], axis=1), vs, NN) for p in ps]
        for sl, o, m, l in zip(sls, os_, ms, ls):
            o_ref[:, sl] = o.astype(BF16)
            l_ref[:, sl] = _unstack_heads(jnp.broadcast_to(m + jnp.log(l), (2 * BLK, LANES)))

    spec_q = pl.BlockSpec((BLK, grp * LANES), lambda b, g, i: (b * nq + i, g))
    spec_p = pl.BlockSpec((BLK, LANES), lambda b, g, i: (b * nq + jnp.maximum(i - 1, 0), g))
    spec_c = pl.BlockSpec((BLK, LANES), lambda b, g, i: (b * nq + i, g))
    in_specs = [spec_q, spec_p, spec_c, spec_p, spec_c, pl.BlockSpec((grp, 1, LANES), lambda b, g, i: (g, 0, 0))]
    out_shape = [jax.ShapeDtypeStruct((T, NQ), BF16), jax.ShapeDtypeStruct((T, NQ), F32)]
    return _call_behind(body, name, (B, NG, nq), in_specs, [spec_q, spec_q], out_shape, [q, kd, kd, vd, vd, sink2], gather, False)


def _swa_bwd(q, kd, vd, sink2, do, lse, B, name, exchange=None):
    T, NQ = q.shape
    NP = NQ // LANES
    nq = T // B // BLK
    NG = kd.shape[1] // LANES
    grp = NP // NG

    def body(q_ref, kp_ref, kc_ref, vp_ref, vc_ref, s_ref, do_ref, l_ref,
             dq_ref, dkc_ref, dkp_ref, dvc_ref, dvp_ref, ds_ref):
        b, i = pl.program_id(1), pl.program_id(2)

        @pl.when((b == 0) & (i == 0))
        def _():
            ds_ref[...] = jnp.zeros_like(ds_ref)
        valid = _swa_valid(i)
        kk = jnp.concatenate([kp_ref[...], kc_ref[...]], axis=0)
        vv = jnp.concatenate([vp_ref[...], vc_ref[...]], axis=0)
        sls = [slice(jj * LANES, (jj + 1) * LANES) for jj in range(grp)]
        sc = [_swa_scores(q_ref[:, sl], kk, s_ref[jj], valid) for jj, sl in enumerate(sls)]
        dos = [_stack_heads(do_ref[:, sl]) for sl in sls]
        dps = [_dot(d, vv, NT) for d in dos]
        lses = [jnp.concatenate([l_ref[:, sl][:, 0:1], l_ref[:, sl][:, HEAD:HEAD + 1]], axis=0) for sl in sls]
        ps = [jnp.exp(s - lse) for (_, s, _), lse in zip(sc, lses)]
        deltas = [jnp.sum(p * dp, axis=1, keepdims=True) for p, dp in zip(ps, dps)]
        dscs = [(p * (dp - delta)).astype(BF16) for p, dp, delta in zip(ps, dps, deltas)]
        dqs = [_dot(dsc, kk, NN) for dsc in dscs]
        dk = jnp.zeros((2 * BLK, LANES), F32)
        dv = jnp.zeros((2 * BLK, LANES), F32)
        for jj, sl in enumerate(sls):
            dsk = -jnp.exp(sc[jj][2] - lses[jj]) * deltas[jj]
            dsk_lo = jnp.sum(dsk[:BLK], axis=0, keepdims=True)
            dsk_hi = jnp.sum(dsk[BLK:], axis=0, keepdims=True)
            ds_ref[jj] += jnp.where(_lo_mask((1, LANES)), dsk_lo, dsk_hi)
            dq_ref[:, sl] = _unstack_heads(dqs[jj]) * SCALE
            dk = dk + _dot(dscs[jj], sc[jj][0], TN)
            dv = dv + _dot(ps[jj].astype(BF16), dos[jj], TN)
        dkp_ref[...] = dk[:BLK]
        dkc_ref[...] = dk[BLK:]
        dvp_ref[...] = dv[:BLK]
        dvc_ref[...] = dv[BLK:]

    spec_q = pl.BlockSpec((BLK, grp * LANES), lambda g, b, i: (b * nq + i, g))
    spec_p = pl.BlockSpec((BLK, LANES), lambda g, b, i: (b * nq + jnp.maximum(i - 1, 0), g))
    spec_c = pl.BlockSpec((BLK, LANES), lambda g, b, i: (b * nq + i, g))
    spec_s = pl.BlockSpec((grp, 1, LANES), lambda g, b, i: (g, 0, 0))
    kv = jax.ShapeDtypeStruct((T, NG * LANES), F32)
    in_specs = [spec_q, spec_p, spec_c, spec_p, spec_c, spec_s, spec_q, spec_q]
    out_specs = [spec_q, spec_c, spec_c, spec_c, spec_c, spec_s]
    out_shape = [jax.ShapeDtypeStruct((T, NQ), F32), kv, kv, kv, kv, jax.ShapeDtypeStruct((NP, 1, LANES), F32)]
    return _call_behind(body, name, (NG, B, nq), in_specs, out_specs, out_shape, [q, kd, kd, vd, vd, sink2, do, lse],
                        exchange, True)


def _qk_prep_bwd(qkv, cs, s1, s2, qg, kg, dq, dkc, dkp, dvc, dvp, B, name):
    T, W = qkv.shape
    NQ = W - 2 * LANES
    NP = NQ // LANES
    nq = T // B // BLK

    def body(x_ref, cs_ref, s1_ref, s2_ref, qg_ref, kg_ref, dq_ref, dkc_ref, dkp_ref, dvc_ref, dvp_ref,
             o_ref, dqg_ref, dkg_ref):
        b, i = pl.program_id(0), pl.program_id(1)

        @pl.when((b == 0) & (i == 0))
        def _():
            dqg_ref[...] = jnp.zeros_like(dqg_ref)
            dkg_ref[...] = jnp.zeros_like(dkg_ref)
        P = _head_sum_matrix()
        cs_, s1_, s2_ = cs_ref[...], s1_ref[...], s2_ref[...]
        lo = _lo_mask((BLK, LANES))
        has_next = (i + 1 < nq).astype(F32)

        def norm_rope_bwd(xv, g, d):
            du = _rope_bwd(d, cs_, s1_, s2_)
            r = lax.rsqrt(_head_sum(xv * xv, P) * (1.0 / HEAD) + EPS)
            xhat = xv * r
            dgain = jnp.sum(du * xhat, axis=0, keepdims=True)
            uu = du * g
            dx = r * (uu - xhat * (_head_sum(uu * xhat, P) * (1.0 / HEAD)))
            return dx, dgain + pltpu.roll(dgain, HEAD, 1)

        dqg = jnp.zeros((1, LANES), F32)
        for j in range(NP):
            sl = slice(j * LANES, (j + 1) * LANES)
            dx, dg = norm_rope_bwd(x_ref[:, sl], qg_ref[...], dq_ref[:, sl])
            o_ref[:, sl] = dx.astype(BF16)
            dqg = dqg + dg
        dqg_ref[...] += dqg

        def fold(c_ref, p_ref, g):
            sl = slice(g * LANES, (g + 1) * LANES)
            t = c_ref[:, sl] + has_next * p_ref[:, sl]
            return t + pltpu.roll(t, HEAD, 1)

        dk = jnp.where(lo, fold(dkc_ref, dkp_ref, 0), fold(dkc_ref, dkp_ref, 1))
        dx, dg = norm_rope_bwd(x_ref[:, NQ:NQ + LANES], kg_ref[...], dk)
        o_ref[:, NQ:NQ + LANES] = dx.astype(BF16)
        dkg_ref[...] += dg
        dv = jnp.where(lo, fold(dvc_ref, dvp_ref, 0), fold(dvc_ref, dvp_ref, 1))
        o_ref[:, NQ + LANES:] = dv.astype(BF16)

    spec_t = pl.BlockSpec((BLK, LANES), lambda b, i: (b * nq + i, 0))
    spec_g = pl.BlockSpec((1, LANES), lambda b, i: (0, 0))
    spec_c = pl.BlockSpec((BLK, 2 * LANES), lambda b, i: (b * nq + i, 0))
    spec_n = pl.BlockSpec((BLK, 2 * LANES), lambda b, i: (b * nq + jnp.minimum(i + 1, nq - 1), 0))
    row = jax.ShapeDtypeStruct((1, LANES), F32)
    return pl.pallas_call(
        body, name=name, grid=(B, nq),
        in_specs=[pl.BlockSpec((BLK, W), lambda b, i: (b * nq + i, 0)), spec_t, spec_t, spec_t, spec_g, spec_g,
                  pl.BlockSpec((BLK, NQ), lambda b, i: (b * nq + i, 0)), spec_c, spec_n, spec_c, spec_n],
        out_specs=[pl.BlockSpec((BLK, W), lambda b, i: (b * nq + i, 0)), spec_g, spec_g],
        out_shape=[jax.ShapeDtypeStruct((T, W), BF16), row, row],
        compiler_params=_params("arbitrary", "arbitrary"),
    )(qkv, cs, s1, s2, qg, kg, dq, dkc, dkp, dvc, dvp)


SB_TILE = 256
SB_UNROLL = 4
SB_UNROLL_BWD = 2


def _split_heads(x2, scale=None):
    lo = _lo_mask(x2.shape)
    z = jnp.zeros_like(x2)
    if scale is not None:
        x2 = x2 * scale
    return jnp.where(lo, x2, z), jnp.where(lo, z, x2)


def _sb_terms(qh, kj, diagonal):
    z = _dot(qh, kj, NT)
    e = jnp.exp(-jnp.abs(z))
    lb = jnp.minimum(z, 0.0) - jnp.log(1.0 + e)
    L = lb - z
    if not diagonal:
        return lb, L, None, z, e
    strict = lax.broadcasted_iota(jnp.int32, z.shape, 1) < lax.broadcasted_iota(jnp.int32, z.shape, 0)
    return lb, jnp.where(strict, L, 0.0), strict, z, e


def _tri(n, cmp):
    r = lax.broadcasted_iota(jnp.int32, (n, n), 0)
    c = lax.broadcasted_iota(jnp.int32, (n, n), 1)
    return cmp(r, c).astype(BF16)


def _by_value(r, fns, carry):
    if len(fns) == 1:
        return fns[0](carry)
    half = len(fns) // 2
    return lax.cond(r < half, lambda cr: _by_value(r, fns[:half], cr), lambda cr: _by_value(r - half, fns[half:], cr), carry)


def _sb_fwd(qkv, B, name, gather=None):
    T, W = qkv.shape
    NQ = W // 3
    NP = NQ // LANES
    S = T // B
    tq = min(SB_TILE, S)
    nq = S // tq
    grid = (B, NP, nq)

    def body(q_ref, k_ref, v_ref, o_ref, t_ref):
        i = pl.program_id(2)
        qh = _split_heads(q_ref[...], SCALE)
        U = _tri(tq, lambda r, c: r > c)

        def sweep(tiles, cs, acc):
            chains = [(t, h) for t in range(len(tiles)) for h in range(2)]
            rows = [pl.ds(pl.multiple_of(j * tq, tq), tq) for j, _ in tiles]
            ks = [k_ref[r, :] for r in rows]
            vs = [_split_heads(v_ref[r, :]) for r in rows]
            terms = {(t, h): _sb_terms(qh[h], ks[t], tiles[t][1]) for t, h in chains}
            carry = {}
            for h in range(2):
                c = cs[h]
                for t in range(len(tiles)):
                    carry[t, h] = c
                    c = c + jnp.sum(terms[t, h][1], axis=1, keepdims=True)
                cs = cs[:h] + (c,) + cs[h + 1:]
            cum = {ch: _dot(terms[ch][1].astype(BF16), U, NN) for ch in chains}
            for ch in chains:
                a = jnp.exp(terms[ch][0] + (cum[ch] + carry[ch]))
                if tiles[ch[0]][1]:
                    a = jnp.where(terms[ch][2], a, 0.0)
                acc = acc + _dot(a.astype(BF16), vs[ch[0]][ch[1]], NN)
            return cs, acc

        zero = jnp.zeros((tq, 1), F32)
        rem = i % SB_UNROLL
        heads = [lambda cr, k=k: sweep([(i, True)] + [(i - 1 - t, False) for t in range(k)], *cr) for k in range(SB_UNROLL)]
        carry = _by_value(rem, heads, ((zero, zero), jnp.zeros((tq, LANES), F32)))
        step = lambda n, cr: sweep([(i - 1 - rem - SB_UNROLL * n - t, False) for t in range(SB_UNROLL)], *cr)
        cs, acc = lax.fori_loop(0, i // SB_UNROLL, step, carry)
        o_ref[...] = acc.astype(BF16)
        t_ref[...] = jnp.where(_lo_mask((tq, LANES)), cs[0], cs[1])

    spec_q = pl.BlockSpec((tq, LANES), lambda b, p, i: (b * nq + i, p))
    in_specs = [spec_q, pl.BlockSpec((S, LANES), lambda b, p, i: (b, NP + p)),
                pl.BlockSpec((S, LANES), lambda b, p, i: (b, 2 * NP + p))]
    out_shape = [jax.ShapeDtypeStruct((T, NQ), BF16), jax.ShapeDtypeStruct((T, NQ), F32)]
    return _call_behind(body, name, grid, in_specs, [spec_q, spec_q], out_shape, [qkv, qkv, qkv], gather, False)


def _sb_bwd(qkv, q_t, do, do_t, tot, B, name, exchange=None):
    T, W = qkv.shape
    NQ = W // 3
    NP = NQ // LANES
    S = T // B
    tq = min(SB_TILE, S)
    nq = S // tq
    grid = (B, NP, nq)

    def body(q_ref, k_ref, v_ref, do_ref, qt_ref, dot_ref, t_ref, dq_ref, dk_ref, dv_ref):
        i = pl.program_id(2)

        @pl.when(i == 0)
        def _():
            dk_ref[...] = jnp.zeros_like(dk_ref)
            dv_ref[...] = jnp.zeros_like(dv_ref)
        qh = _split_heads(q_ref[...], SCALE)
        doh = _split_heads(do_ref[...])
        top = lax.broadcasted_iota(jnp.int32, (LANES, tq), 0) < HEAD
        zt = jnp.zeros((LANES, tq), BF16)
        qt = qt_ref[...] * SCALE
        qth = (jnp.where(top, qt, zt), jnp.where(top, zt, qt))
        doth = (jnp.where(top, dot_ref[...], zt), jnp.where(top, zt, dot_ref[...]))
        tt = t_ref[...]
        tot = (tt[:, 0:1], tt[:, HEAD:HEAD + 1])
        Urev = _tri(tq, lambda r, c: r > c)
        Uexc = _tri(tq, lambda r, c: r < c)

        def sweep(tiles, carry):
            nt = len(tiles)
            chains = [(t, h) for t in range(nt) for h in range(2)]
            rows = [pl.ds(pl.multiple_of(j * tq, tq), tq) for j, _ in tiles]
            ks = [k_ref[r, :] for r in rows]
            vs = [v_ref[r, :] for r in rows]
            terms = {(t, h): _sb_terms(qh[h], ks[t], tiles[t][1]) for t, h in chains}
            cc = [carry[h][0] for h in range(2)]
            later = {}
            for t, h in chains:
                cc[h] = cc[h] + jnp.sum(terms[t, h][1], axis=1, keepdims=True)
                later[t, h] = tot[h] - cc[h]
            cum = {ch: _dot(terms[ch][1].astype(BF16), Urev, NN) for ch in chains}
            da = {(t, h): _dot(doh[h], vs[t], NT) for t, h in chains}
            a, g, before = {}, {}, {}
            cg = [carry[h][1] for h in range(2)]
            for ch in chains:
                a[ch] = jnp.exp(terms[ch][0] + (cum[ch] + later[ch]))
                if tiles[ch[0]][1]:
                    a[ch] = jnp.where(terms[ch][2], a[ch], 0.0)
                g[ch] = a[ch] * da[ch]
                before[ch] = cg[ch[1]]
                cg[ch[1]] = cg[ch[1]] + jnp.sum(g[ch], axis=1, keepdims=True)
            G = {ch: _dot(g[ch].astype(BF16), Uexc, NN) for ch in chains}
            dz = {}
            for ch in chains:
                d = g[ch] - jnp.exp(terms[ch][0]) * (g[ch] + (G[ch] + before[ch]))
                if tiles[ch[0]][1]:
                    d = jnp.where(terms[ch][2], d, 0.0)
                dz[ch] = d.astype(BF16)
            dq = [carry[h][2] for h in range(2)]
            for t, h in chains:
                dq[h] = dq[h] + _dot(dz[t, h], ks[t], NN)
            for t in range(nt):
                dk_ref[:, rows[t]] += _dot(qth[0], dz[t, 0], NN) + _dot(qth[1], dz[t, 1], NN)
                dv_ref[:, rows[t]] += _dot(doth[0], a[t, 0].astype(BF16), NN) + _dot(doth[1], a[t, 1].astype(BF16), NN)
            return tuple((cc[h], cg[h], dq[h]) for h in range(2))

        zero = jnp.zeros((tq, 1), F32)
        zq = jnp.zeros((tq, LANES), F32)
        step = lambda n, cr: sweep([(SB_UNROLL_BWD * n + t, False) for t in range(SB_UNROLL_BWD)], cr)
        carry = lax.fori_loop(0, i // SB_UNROLL_BWD, step, ((zero, zero, zq), (zero, zero, zq)))
        tails = [lambda cr, k=k: sweep([(i - k + t, False) for t in range(k)] + [(i, True)], cr) for k in range(SB_UNROLL_BWD)]
        carry = _by_value(i % SB_UNROLL_BWD, tails, carry)
        dq_ref[...] = jnp.where(_lo_mask((tq, LANES)), carry[0][2], carry[1][2]) * SCALE

    spec_q = pl.BlockSpec((tq, LANES), lambda b, p, i: (b * nq + i, p))
    spec_t = pl.BlockSpec((LANES, tq), lambda b, p, i: (p, b * nq + i))
    spec_s = pl.BlockSpec((LANES, S), lambda b, p, i: (b * NP + p, 0))
    key_side = jax.ShapeDtypeStruct((B * NQ, S), F32)
    in_specs = [spec_q, pl.BlockSpec((S, LANES), lambda b, p, i: (b, NP + p)),
                pl.BlockSpec((S, LANES), lambda b, p, i: (b, 2 * NP + p)), spec_q, spec_t, spec_t, spec_q]
    out_specs = [spec_q, spec_s, spec_s]
    out_shape = [jax.ShapeDtypeStruct((T, NQ), F32), key_side, key_side]
    args = [qkv, qkv, qkv, do, q_t, do_t, tot]
    return _call_behind(body, name, grid, in_specs, out_specs, out_shape, args, exchange, True)


def _adamw(w, g, m, v, name):
    shape = w.shape
    cols = shape[-1]
    rows = math.prod(shape[:-1])
    tr = _pick(rows, max(8, (1 << 19) // max(cols, LANES) // 8 * 8), 8)

    def body(w_ref, g_ref, m_ref, v_ref, d_ref, mo_ref, vo_ref):
        gv = g_ref[...]
        mn = ADAM_B1 * m_ref[...] + (1.0 - ADAM_B1) * gv
        vn = ADAM_B2 * v_ref[...] + (1.0 - ADAM_B2) * (gv * gv)
        m_hat = mn / (1.0 - ADAM_B1 ** ADAM_STEP)
        v_hat = vn / (1.0 - ADAM_B2 ** ADAM_STEP)
        d_ref[...] = -ADAM_LR * (m_hat / (jnp.sqrt(v_hat) + ADAM_EPS) + ADAM_WD * w_ref[...])
        mo_ref[...] = mn
        vo_ref[...] = vn

    spec = pl.BlockSpec((tr, cols), lambda i: (i, 0))
    out = jax.ShapeDtypeStruct((rows, cols), F32)
    d, mn, vn = pl.pallas_call(
        body, name=name, grid=(rows // tr,),
        in_specs=[spec] * 4, out_specs=[spec] * 3, out_shape=[out] * 3,
        compiler_params=_params("parallel"),
    )(w.reshape(rows, cols), g.reshape(rows, cols), m.reshape(rows, cols), v.reshape(rows, cols))
    return d.reshape(shape), mn.reshape(shape), vn.reshape(shape)


def _pad_rows(a, rows):
    return jnp.pad(a, ((0, rows - a.shape[0]), (0, 0)))


def kernel(x, c, positions, ada_w, ada_b, norm1_g, norm2_g, wqkv_a, q_norm_a, k_norm_a, sinks_a, wo_a, wqkv_b, wo_b, w_gate, w_up, w_down, loss_target, m_ada_w, m_ada_b, m_norm1_g, m_norm2_g, m_wqkv_a, m_q_norm_a, m_k_norm_a, m_sinks_a, m_wo_a, m_wqkv_b, m_wo_b, m_w_gate, m_w_up, m_w_down, v_ada_w, v_ada_b, v_norm1_g, v_norm2_g, v_wqkv_a, v_q_norm_a, v_k_norm_a, v_sinks_a, v_wo_a, v_wqkv_b, v_wo_b, v_w_gate, v_w_up, v_w_down):
    B, S, D = x.shape
    T = B * S
    L = ada_w.shape[0]
    NA, NB_ = wqkv_a.shape[0], wqkv_b.shape[0]
    me = 4 * lax.axis_index("x") + 2 * lax.axis_index("y") + lax.axis_index("c")
    xt = x.reshape(T, D)

    col_sharded = {"qkv_a": wqkv_a, "qkv_b": wqkv_b, "gate": w_gate, "up": w_up}
    row_sharded = {"wo_a": wo_a, "wo_b": wo_b, "down": w_down}

    def shard_rows(key):
        kind, idx = key
        return col_sharded[kind][idx].T if kind in col_sharded else row_sharded[kind][idx]

    def layer_keys(l):
        mix = "a" if l % 2 == 0 else "b"
        return [("qkv_" + mix, l // 2), ("wo_" + mix, l // 2), ("gate", l), ("up", l), ("down", l)]

    def unpack(buf, keys, reshape):
        out, off = {}, 0
        for key in keys:
            rows = shard_rows(key).shape[0]
            out[key] = reshape(buf[..., off:off + rows, :], rows)
            off += rows
        return out

    first_b = 1
    keys_early = layer_keys(0)[:2]
    keys_mid = layer_keys(0)[2:] + [("qkv_b", 0)]
    keys_late = [k for l in range(1, L) for k in layer_keys(l) if k != ("qkv_b", 0)]
    pack = lambda keys: jnp.concatenate([shard_rows(k).astype(BF16) for k in keys], axis=0)
    full_rows = lambda b, rows: b.reshape(NDEV * rows, D)
    W = unpack(_all_gather(pack(keys_early), "ag_weights"), keys_early, full_rows)

    WA = ada_w.shape[2]
    c_all = _all_gather(c, "ag_c").reshape(NDEV * B, D)
    bias = lax.dynamic_slice_in_dim(ada_b, me * WA, WA, axis=1).reshape(L, 1, WA)
    mod_part = _ada_fwd(c_all, ada_w, bias, "ada_fwd")
    mod_all = _all_gather(mod_part.reshape(L * NDEV * B, WA), "ag_mod")
    mod_all = mod_all.reshape(NDEV, L, NDEV * B, WA).transpose(1, 2, 0, 3).reshape(L, NDEV * B, NDEV * WA)
    mod = lax.dynamic_slice_in_dim(mod_all, me * B, B, axis=1)
    mod = mod.reshape(L, B, 6, 1, D)
    sh1, sc1, g1, sh2, sc2, g2 = [mod[:, :, k] for k in range(6)]

    half = ROT // 2
    inv_freq = jnp.power(jnp.float32(ROPE_THETA), -jnp.arange(half, dtype=F32) * 2.0 / ROT)
    ang = positions.reshape(T, 1).astype(F32) * inv_freq[None, :]
    cos, sin = jnp.cos(ang), jnp.sin(ang)
    ones = jnp.ones((T, HEAD - ROT), F32)
    zeros = jnp.zeros((T, HEAD - ROT), F32)
    z8 = jnp.zeros((T, half), F32)
    cs = jnp.tile(jnp.concatenate([cos, cos, ones], axis=1), (1, 2))
    s1 = jnp.tile(jnp.concatenate([-sin, z8, zeros], axis=1), (1, 2))
    s2 = jnp.tile(jnp.concatenate([z8, sin, zeros], axis=1), (1, 2))

    saved = []
    xc = xt
    h1 = _norm_mod(xc, norm1_g[0:1], sc1[0], sh1[0], S, "norm1_0")
    for l in range(L):
        j = l // 2
        sv = dict(x_in=xc, h1=h1)
        if l % 2 == 0:
            qkv = _mm_nt(h1, W["qkv_a", j], F32, f"qkv_a_{l}")
            qg = jnp.tile(q_norm_a[j:j + 1], (1, 2))
            kg = jnp.tile(k_norm_a[j:j + 1], (1, 2))
            qn, kd, vd = _qk_prep(qkv, cs, s1, s2, qg, kg, f"qk_prep_{l}")
            sink2 = jnp.repeat(sinks_a[j].reshape(-1, 2), HEAD, axis=1).reshape(-1, 1, LANES)
            if l == 0:
                attn, lse, mid = _swa_fwd(qn, kd, vd, sink2, B, f"swa_fwd_{l}", gather=pack(keys_mid))
                W.update(unpack(mid, keys_mid, full_rows))
            else:
                attn, lse = _swa_fwd(qn, kd, vd, sink2, B, f"swa_fwd_{l}")
            sv.update(qkv=qkv, qg=qg, kg=kg, qn=qn, kd=kd, vd=vd, sink2=sink2, lse=lse)
            wo = W["wo_a", j]
        else:
            qkv = _mm_nt(h1, W["qkv_b", j], BF16, f"qkv_b_{l}")
            if l == first_b:
                attn, tot, late = _sb_fwd(qkv, B, f"sb_fwd_{l}", gather=pack(keys_late))
                W.update(unpack(late, keys_late, full_rows))
            else:
                attn, tot = _sb_fwd(qkv, B, f"sb_fwd_{l}")
            sv.update(qkv=qkv, tot=tot)
            wo = W["wo_b", j]
        y1, xm, h2 = _mm_res(attn, wo, xc, g1[l], S, f"attn_out_{l}", norm=(norm2_g[l:l + 1], sc2[l], sh2[l]))
        gate, up, act = _swiglu_fwd(h2, W["gate", l], W["up", l], f"swiglu_fwd_{l}")
        if l + 1 < L:
            y2, xc, h1 = _mm_res(act, W["down", l], xm, g2[l], S, f"mlp_out_{l}",
                                 norm=(norm1_g[l + 1:l + 2], sc1[l + 1], sh1[l + 1]))
        else:
            y2, xc = _mm_res(act, W["down", l], xm, g2[l], S, f"mlp_out_{l}")
        sv.update(attn=attn, y1=y1, x_mid=xm, h2=h2, gate=gate, up=up, act=act, y2=y2)
        saved.append(sv)

    dx, loss_tile = _loss_head(xc, loss_target.reshape(T, D), "loss_head")

    G = {}
    pack_grads = lambda keys: jnp.concatenate([G[k].reshape(NDEV, G[k].shape[0] // NDEV, D) for k in keys], axis=1)
    keys_hi = [k for l in range(first_b + 1, L) for k in layer_keys(l)] + layer_keys(first_b)[1:]
    keys_mid_g = [("qkv_b", 0)] + layer_keys(0)[1:]
    keys_lo = layer_keys(0)[:1]
    received_hi = received_mid = None
    dmod = [None] * L
    dn1, dn2 = [None] * L, [None] * L
    dqg, dkg, dsink = [None] * NA, [None] * NA, [None] * NA
    dy2, dg2 = _gate_bwd(dx, saved[L - 1]["y2"], g2[L - 1], S, "gate2_bwd_top")
    for l in reversed(range(L)):
        j = l // 2
        mix = "a" if l % 2 == 0 else "b"
        sv = saved[l]
        dgate, dup = _swiglu_bwd(dy2, W["down", l], sv["gate"], sv["up"], f"swiglu_bwd_{l}")
        G["down", l] = _mm_tn(sv["act"], dy2, f"dw_down_{l}")
        G["gate", l] = _mm_tn(dgate, sv["h2"], f"dw_gate_{l}")
        G["up", l] = _mm_tn(dup, sv["h2"], f"dw_up_{l}")
        dxm, dsh2, dsc2, dn2[l], dy1, dg1 = _norm_mod_bwd(sv["x_mid"], [(dgate, W["gate", l]), (dup, W["up", l])], dx,
                                                         norm2_g[l:l + 1], sc2[l], S, f"norm2_bwd_{l}", below=(sv["y1"], g1[l]))
        dattn = _mm_nt(dy1, W["wo_" + mix, j], BF16, f"dattn_{l}")
        G["wo_" + mix, j] = _mm_tn(sv["attn"], dy1, f"dw_o_{l}")
        if l % 2 == 0:
            swa_args = (sv["qn"], sv["kd"], sv["vd"], sv["sink2"], dattn, sv["lse"], B, f"swa_bwd_{l}")
            if l == 0:
                dq, dkc, dkp, dvc, dvp, dsink[j], received_mid = _swa_bwd(*swa_args, exchange=pack_grads(keys_mid_g))
            else:
                dq, dkc, dkp, dvc, dvp, dsink[j] = _swa_bwd(*swa_args)
            dqkv, dqg[j], dkg[j] = _qk_prep_bwd(sv["qkv"], cs, s1, s2, sv["qg"], sv["kg"], dq, dkc, dkp, dvc, dvp, B,
                                                f"qk_prep_bwd_{l}")
        else:
            nqb = sv["qkv"].shape[1] // 3
            sb_args = (sv["qkv"], sv["qkv"][:, :nqb].T, dattn, dattn.T, sv["tot"], B, f"sb_bwd_{l}")
            if l == first_b and keys_hi:
                dq, dk_t, dv_t, received_hi = _sb_bwd(*sb_args, exchange=pack_grads(keys_hi))
            else:
                dq, dk_t, dv_t = _sb_bwd(*sb_args)
            dk, dv = [t.reshape(B, nqb, S).transpose(0, 2, 1).reshape(T, nqb) for t in (dk_t, dv_t)]
            dqkv = jnp.concatenate([dq, dk, dv], axis=1).astype(BF16)
        G["qkv_" + mix, j] = _mm_tn(dqkv, sv["h1"], f"dw_qkv_{l}")
        n1_args = (sv["x_in"], [(dqkv, W["qkv_" + mix, j])], dxm, norm1_g[l:l + 1], sc1[l], S, f"norm1_bwd_{l}")
        dmod_l = [None, None, dg1, dsh2, dsc2, dg2]
        if l > 0:
            dx, dmod_l[0], dmod_l[1], dn1[l], dy2, dg2 = _norm_mod_bwd(*n1_args, below=(saved[l - 1]["y2"], g2[l - 1]))
        else:
            dx, dmod_l[0], dmod_l[1], dn1[l] = _norm_mod_bwd(*n1_args)
        dmod[l] = jnp.concatenate(dmod_l, axis=1)
    grad_x = dx.reshape(B, S, D)

    ndm = L * 6
    dmod_rows = jnp.stack(dmod, axis=1).reshape(B * ndm, D)
    misc = jnp.concatenate(
        [jnp.concatenate(dn1, axis=0).reshape(B * L, D), jnp.concatenate(dn2, axis=0).reshape(B * L, D),
         _pad_rows(jnp.concatenate([jnp.pad(r, ((0, 0), (0, D - LANES))) for r in dqg + dkg]
                                   + [jnp.pad(r[:, 0, ::HEAD].reshape(1, -1), ((0, 0), (0, D - 2 * r.shape[0]))) for r in dsink]
                                   + [jnp.pad(loss_tile[0:1, 0:1], ((0, 0), (0, D - 1)))], axis=0), 8)], axis=0)
    nmisc = misc.shape[0]
    small = _all_gather(jnp.concatenate([dmod_rows, _pad_rows(misc, -(-nmisc // 8) * 8)], axis=0), "ag_small")
    dmod_all = small[:, :B * ndm].reshape(NDEV * B, ndm, D)
    g_ada_b = _sum_leading(dmod_all, "sum_dmod").reshape(L, 6 * D)
    misc_sum = _sum_leading(small[:, B * ndm:], "sum_misc")
    g_n1 = misc_sum[0:B * L].reshape(L, B, D)
    g_n2 = misc_sum[B * L:2 * B * L].reshape(L, B, D)
    g_norm1 = _sum_leading(g_n1.transpose(1, 0, 2), "sum_n1")
    g_norm2 = _sum_leading(g_n2.transpose(1, 0, 2), "sum_n2")
    o = 2 * B * L
    g_qn = misc_sum[o:o + NA, :HEAD]
    g_kn = misc_sum[o + NA:o + 2 * NA, :HEAD]
    nsink = sinks_a.shape[1]
    g_sink = misc_sum[o + 2 * NA:o + 3 * NA, :nsink]
    loss = misc_sum[o + 3 * NA, 0]

    dmod_loc = lax.dynamic_slice_in_dim(dmod_all.reshape(NDEV * B, L, 6 * D), me * WA, WA, axis=2)
    g_ada_w = _ada_bwd(c_all, dmod_loc.transpose(1, 0, 2), "ada_bwd")

    shard = unpack(_sum_leading(_exchange(pack_grads(keys_lo), "grad_exchange"), "grad_sum"), keys_lo, lambda b, rows: b)
    shard.update(unpack(_sum_leading(received_mid, "grad_sum_mid"), keys_mid_g, lambda b, rows: b))
    if received_hi is not None:
        shard.update(unpack(_sum_leading(received_hi, "grad_sum_hi"), keys_hi, lambda b, rows: b))

    def stacked(kind, n):
        return jnp.stack([shard[kind, i].T if kind in col_sharded else shard[kind, i] for i in range(n)])

    gw_qkv_a, gw_qkv_b, gw_gate, gw_up = stacked("qkv_a", NA), stacked("qkv_b", NB_), stacked("gate", L), stacked("up", L)
    gw_wo_a, gw_wo_b, gw_down = stacked("wo_a", NA), stacked("wo_b", NB_), stacked("down", L)

    grads = [g_ada_w, g_ada_b, g_norm1, g_norm2, gw_qkv_a, g_qn, g_kn, g_sink, gw_wo_a, gw_qkv_b, gw_wo_b,
             gw_gate, gw_up, gw_down]
    ws = [ada_w, ada_b, norm1_g, norm2_g, wqkv_a, q_norm_a, k_norm_a, sinks_a, wo_a, wqkv_b, wo_b, w_gate, w_up, w_down]
    ms = [m_ada_w, m_ada_b, m_norm1_g, m_norm2_g, m_wqkv_a, m_q_norm_a, m_k_norm_a, m_sinks_a, m_wo_a, m_wqkv_b,
          m_wo_b, m_w_gate, m_w_up, m_w_down]
    vs = [v_ada_w, v_ada_b, v_norm1_g, v_norm2_g, v_wqkv_a, v_q_norm_a, v_k_norm_a, v_sinks_a, v_wo_a, v_wqkv_b,
          v_wo_b, v_w_gate, v_w_up, v_w_down]
    deltas, new_m, new_v = [], [], []
    for k, (w, g, m, v) in enumerate(zip(ws, grads, ms, vs)):
        g = g.reshape(w.shape)
        d, mn, vn = _adamw(w, g, m, v, f"adamw_{k}")
        grads[k] = g
        deltas.append(d)
        new_m.append(mn)
        new_v.append(vn)
    return (loss, grad_x, *grads, *deltas, *new_m, *new_v)
```

```python
import functools
import math

import jax
import jax.numpy as jnp
from jax import lax
from jax.experimental import pallas as pl
from jax.experimental.pallas import tpu as pltpu

F32 = jnp.float32
BF16 = jnp.bfloat16
NDEV = 8
HEAD = 64
BLK = 128
LANES = 128
EPS = 1e-6
ROT = HEAD // 4
ROPE_THETA = 500000.0
SCALE = HEAD ** -0.5
NEG = -1e30
VMEM_LIMIT = 56 * 1024 * 1024
MESH = pl.DeviceIdType.MESH
HIGH = lax.Precision.HIGHEST

ADAM_LR = 0.001
ADAM_B1 = 0.9
ADAM_B2 = 0.999
ADAM_EPS = 1e-08
ADAM_WD = 0.01
ADAM_STEP = 10


def _params(*sem):
    return pltpu.CompilerParams(dimension_semantics=sem, vmem_limit_bytes=VMEM_LIMIT)


def _pick(n, cap, mult):
    if n <= cap:
        return n
    best = None
    for t in range(mult, cap + 1, mult):
        if n % t == 0:
            best = t
    assert best is not None, (n, cap, mult)
    return best


def _dot(a, b, dims, precision=None):
    return lax.dot_general(a, b, (dims, ((), ())), preferred_element_type=F32, precision=precision)


NN = ((1,), (0,))
NT = ((1,), (1,))
TN = ((0,), (0,))


def _all_gather(x, name):
    m, n = x.shape

    def body(x_ref, out_ref, send_sems, recv_sems, local_sem):
        ix, iy, ic = lax.axis_index("x"), lax.axis_index("y"), lax.axis_index("c")
        me, sibling = (ix, iy, ic), (ix, iy, 1 - ic)
        chips = [(1 - ix, iy), (ix, 1 - iy), (1 - ix, 1 - iy)]

        def slab(px, py, pc):
            return out_ref.at[4 * px + 2 * py + pc]

        def copy(k, block, to, src=None):
            return pltpu.make_async_remote_copy(
                src_ref=slab(*block) if src is None else src, dst_ref=slab(*block),
                send_sem=send_sems.at[k], recv_sem=recv_sems.at[k], device_id=to, device_id_type=MESH)

        mine = pltpu.make_async_copy(x_ref, slab(*me), local_sem)
        mine.start()
        first = [copy(0, me, sibling, src=x_ref)]
        first += [copy(1 + j, me, (*chip, ic), src=x_ref) for j, chip in enumerate(chips)]
        for cp in first:
            cp.start()
        passed = [copy(4 + j, (*chip, ic), sibling) for j, chip in enumerate(chips)]
        for j, chip in enumerate(chips):
            copy(1 + j, (*chip, ic), me).wait_recv()
            passed[j].start()
        copy(0, sibling, me).wait_recv()
        for j, chip in enumerate(chips):
            copy(4 + j, (*chip, 1 - ic), me).wait_recv()
        for cp in first + passed:
            cp.wait_send()
        mine.wait()

    return pl.pallas_call(
        body, name=name,
        out_shape=jax.ShapeDtypeStruct((NDEV, m, n), x.dtype),
        in_specs=[pl.BlockSpec(memory_space=pl.ANY)],
        out_specs=pl.BlockSpec(memory_space=pl.ANY),
        scratch_shapes=[pltpu.SemaphoreType.DMA((7,)), pltpu.SemaphoreType.DMA((7,)), pltpu.SemaphoreType.DMA(())],
    )(x)


COMM_SEMS = [pltpu.SemaphoreType.DMA((NDEV - 1,)), pltpu.SemaphoreType.DMA((NDEV - 1,)), pltpu.SemaphoreType.DMA(())]
HBM_SPEC = pl.BlockSpec(memory_space=pl.ANY)


def _direct_copies(src_ref, dst_ref, sems, scatter):
    send_sems, recv_sems, own_sem = sems
    ix, iy, ic = lax.axis_index("x"), lax.axis_index("y"), lax.axis_index("c")
    me = 4 * ix + 2 * iy + ic
    copies = [pltpu.make_async_copy(src_ref.at[me] if scatter else src_ref, dst_ref.at[me], own_sem)]
    for k in range(1, NDEV):
        px = 1 - ix if k & 4 else ix
        py = 1 - iy if k & 2 else iy
        pc = 1 - ic if k & 1 else ic
        copies.append(pltpu.make_async_remote_copy(
            src_ref=src_ref.at[4 * px + 2 * py + pc] if scatter else src_ref, dst_ref=dst_ref.at[me],
            send_sem=send_sems.at[k - 1], recv_sem=recv_sems.at[k - 1],
            device_id=(px, py, pc), device_id_type=MESH))
    return copies


def _exchange(p, name):
    def body(p_ref, r_ref, *sems):
        copies = _direct_copies(p_ref, r_ref, sems, True)
        for cp in copies:
            cp.start()
        for cp in copies:
            cp.wait()

    return pl.pallas_call(
        body, name=name,
        out_shape=jax.ShapeDtypeStruct(p.shape, p.dtype),
        in_specs=[HBM_SPEC], out_specs=HBM_SPEC, scratch_shapes=COMM_SEMS,
    )(p)


def _call_behind(body, name, grid, in_specs, out_specs, out_shape, args, payload=None, scatter=False):
    params = _params(*["arbitrary"] * len(grid))
    if payload is None:
        return pl.pallas_call(body, name=name, grid=grid, in_specs=in_specs, out_specs=out_specs, out_shape=out_shape,
                              compiler_params=params)(*args)
    n_in, n_out = len(in_specs), len(out_specs)

    def edge(first):
        ids = [pl.program_id(a) for a in range(len(grid))]
        return functools.reduce(lambda u, v: u & v, [i == (0 if first else d - 1) for i, d in zip(ids, grid)])

    def wrapped(*refs):
        x_ref, r_ref, sems = refs[n_in], refs[n_in + 1 + n_out], refs[n_in + n_out + 2:]

        @pl.when(edge(True))
        def _():
            for cp in _direct_copies(x_ref, r_ref, sems, scatter):
                cp.start()
        body(*refs[:n_in], *refs[n_in + 1:n_in + 1 + n_out])

        @pl.when(edge(False))
        def _():
            for cp in _direct_copies(x_ref, r_ref, sems, scatter):
                cp.wait()

    arrived = jax.ShapeDtypeStruct(payload.shape if scatter else (NDEV,) + payload.shape, payload.dtype)
    return pl.pallas_call(
        wrapped, name=name, grid=grid, in_specs=list(in_specs) + [HBM_SPEC], out_specs=list(out_specs) + [HBM_SPEC],
        out_shape=list(out_shape) + [arrived], scratch_shapes=COMM_SEMS, compiler_params=params,
    )(*args, payload)


def _sum_leading(r, name):
    k, m, n = r.shape
    mult = 8 * (4 // r.dtype.itemsize)
    tm = _pick(m, max(mult, (4 * 1024 * 1024) // (k * n * r.dtype.itemsize) // mult * mult), mult)

    def body(r_ref, o_ref):
        acc = r_ref[0].astype(F32)
        for s in range(1, k):
            acc = acc + r_ref[s].astype(F32)
        o_ref[...] = acc

    return pl.pallas_call(
        body, name=name, grid=(m // tm,),
        in_specs=[pl.BlockSpec((k, tm, n), lambda i: (0, i, 0))],
        out_specs=pl.BlockSpec((tm, n), lambda i: (i, 0)),
        out_shape=jax.ShapeDtypeStruct((m, n), F32),
        compiler_params=_params("parallel"),
    )(r)


def _mm_nt(a, bt, out_dtype, name):
    M, K = a.shape
    N = bt.shape[0]
    tm, tn = _pick(M, 512, 8), _pick(N, 1536, LANES)

    def body(a_ref, b_ref, o_ref):
        o_ref[...] = _dot(a_ref[...], b_ref[...], NT).astype(out_dtype)

    return pl.pallas_call(
        body, name=name, grid=(N // tn, M // tm),
        in_specs=[pl.BlockSpec((tm, K), lambda j, i: (i, 0)), pl.BlockSpec((tn, K), lambda j, i: (j, 0))],
        out_specs=pl.BlockSpec((tm, tn), lambda j, i: (i, j)),
        out_shape=jax.ShapeDtypeStruct((M, N), out_dtype),
        compiler_params=_params("parallel", "parallel"),
    )(a, bt)


def _mm_tn(a, b, name):
    M, N1 = a.shape
    N2 = b.shape[1]
    t1, tk = _pick(N1, 1536, LANES), _pick(M, 512, 8)
    nk = M // tk

    def body(a_ref, b_ref, o_ref, acc_ref):
        k = pl.program_id(1)

        @pl.when(k == 0)
        def _():
            acc_ref[...] = jnp.zeros_like(acc_ref)
        acc_ref[...] += _dot(a_ref[...], b_ref[...], TN)

        @pl.when(k == nk - 1)
        def _():
            o_ref[...] = acc_ref[...].astype(BF16)

    return pl.pallas_call(
        body, name=name, grid=(N1 // t1, nk),
        in_specs=[pl.BlockSpec((tk, t1), lambda i, k: (k, i)), pl.BlockSpec((tk, N2), lambda i, k: (k, 0))],
        out_specs=pl.BlockSpec((t1, N2), lambda i, k: (i, 0)),
        out_shape=jax.ShapeDtypeStruct((N1, N2), BF16),
        scratch_shapes=[pltpu.VMEM((t1, N2), F32)],
        compiler_params=_params("parallel", "arbitrary"),
    )(a, b)


def _norm_mod_rows(xv, gain, sc, sh):
    r = lax.rsqrt(jnp.mean(xv * xv, axis=-1, keepdims=True) + EPS)
    return ((xv * r) * gain * (1.0 + sc) + sh).astype(BF16)


def _mm_res(a, w, x, gate, S, name, norm=None):
    T, K = a.shape
    D = w.shape[1]
    tm = _pick(S, 512, 8)
    nb = S // tm

    def body(a_ref, w_ref, x_ref, g_ref, *rest):
        y = _dot(a_ref[...], w_ref[...], NN)
        xn = x_ref[...] + g_ref[0] * y
        if norm is None:
            y_ref, o_ref = rest
        else:
            gain_ref, sc_ref, sh_ref, y_ref, o_ref, h_ref = rest
            h_ref[...] = _norm_mod_rows(xn, gain_ref[...], sc_ref[0], sh_ref[0])
        y_ref[...] = y.astype(BF16)
        o_ref[...] = xn

    spec_t = pl.BlockSpec((tm, D), lambda i: (i, 0))
    spec_b = pl.BlockSpec((1, 1, D), lambda i: (i // nb, 0, 0))
    in_specs = [pl.BlockSpec((tm, K), lambda i: (i, 0)), pl.BlockSpec((K, D), lambda i: (0, 0)), spec_t, spec_b]
    out_specs = [spec_t, spec_t]
    out_shape = [jax.ShapeDtypeStruct((T, D), BF16), jax.ShapeDtypeStruct((T, D), F32)]
    args = [a, w, x, gate]
    if norm is not None:
        in_specs += [pl.BlockSpec((1, D), lambda i: (0, 0)), spec_b, spec_b]
        out_specs.append(spec_t)
        out_shape.append(jax.ShapeDtypeStruct((T, D), BF16))
        args += list(norm)
    return pl.pallas_call(
        body, name=name, grid=(T // tm,), in_specs=in_specs, out_specs=out_specs, out_shape=out_shape,
        compiler_params=_params("parallel"),
    )(*args)


def _swiglu_fwd(h, wgt, wut, name):
    T, D = h.shape
    F = wgt.shape[0]
    tm, tn = _pick(T, 512, 8), _pick(F, 1536, LANES)

    def body(h_ref, g_ref, u_ref, go_ref, uo_ref, a_ref):
        hh = h_ref[...]
        g = _dot(hh, g_ref[...], NT)
        u = _dot(hh, u_ref[...], NT)
        go_ref[...] = g.astype(BF16)
        uo_ref[...] = u.astype(BF16)
        a_ref[...] = (g * jax.nn.sigmoid(g) * u).astype(BF16)

    spec_w = pl.BlockSpec((tn, D), lambda j, i: (j, 0))
    spec_o = pl.BlockSpec((tm, tn), lambda j, i: (i, j))
    out = jax.ShapeDtypeStruct((T, F), BF16)
    return pl.pallas_call(
        body, name=name, grid=(F // tn, T // tm),
        in_specs=[pl.BlockSpec((tm, D), lambda j, i: (i, 0)), spec_w, spec_w],
        out_specs=[spec_o, spec_o, spec_o],
        out_shape=[out, out, out],
        compiler_params=_params("parallel", "parallel"),
    )(h, wgt, wut)


def _swiglu_bwd(dy, wd, gate, up, name):
    T, D = dy.shape
    F = wd.shape[0]
    tm, tn = _pick(T, 512, 8), _pick(F, 1536, LANES)

    halves = [slice(0, tn // 2), slice(tn // 2, tn)] if tn % (2 * LANES) == 0 else [slice(0, tn)]

    def body(dy_ref, w_ref, g_ref, u_ref, dg_ref, du_ref):
        das = [_dot(dy_ref[...], w_ref[sl, :], NT) for sl in halves]
        for sl, da in zip(halves, das):
            g = g_ref[:, sl].astype(F32)
            sg = jax.nn.sigmoid(g)
            t = da * sg
            du_ref[:, sl] = (t * g).astype(BF16)
            dg_ref[:, sl] = (t * u_ref[:, sl].astype(F32) * (1.0 + g * (1.0 - sg))).astype(BF16)

    spec_o = pl.BlockSpec((tm, tn), lambda j, i: (i, j))
    return pl.pallas_call(
        body, name=name, grid=(F // tn, T // tm),
        in_specs=[pl.BlockSpec((tm, D), lambda j, i: (i, 0)), pl.BlockSpec((tn, D), lambda j, i: (j, 0)), spec_o, spec_o],
        out_specs=[spec_o, spec_o],
        out_shape=[jax.ShapeDtypeStruct((T, F), BF16), jax.ShapeDtypeStruct((T, F), BF16)],
        compiler_params=_params("parallel", "parallel"),
    )(dy, wd, gate, up)


def _norm_mod(x, gain, sc, sh, S, name):
    T, D = x.shape
    tm = _pick(S, 512, 8)
    nb = S // tm

    def body(x_ref, g_ref, sc_ref, sh_ref, o_ref):
        o_ref[...] = _norm_mod_rows(x_ref[...], g_ref[...], sc_ref[0], sh_ref[0])

    spec_b = pl.BlockSpec((1, 1, D), lambda i: (i // nb, 0, 0))
    return pl.pallas_call(
        body, name=name, grid=(T // tm,),
        in_specs=[pl.BlockSpec((tm, D), lambda i: (i, 0)), pl.BlockSpec((1, D), lambda i: (0, 0)), spec_b, spec_b],
        out_specs=pl.BlockSpec((tm, D), lambda i: (i, 0)),
        out_shape=jax.ShapeDtypeStruct((T, D), BF16),
        compiler_params=_params("parallel"),
    )(x, gain, sc, sh)


def _norm_mod_bwd(x, pairs, dres, gain, sc, S, name, below=None, exchange=None):
    T, D = x.shape
    B = T // S
    tm = _pick(S, 512, 8)
    nb = S // tm
    n_mm = 2 * len(pairs)

    def body(*refs):
        mm, (x_ref, dr_ref, g_ref, sc_ref), rest = refs[:n_mm], refs[n_mm:n_mm + 4], refs[n_mm + 4:]
        if below is None:
            o_ref, dsh_ref, dsc_ref, dg_ref = rest
            sums = [dsh_ref, dsc_ref, dg_ref]
        else:
            y_ref, gt_ref, o_ref, dsh_ref, dsc_ref, dg_ref, dy_ref, dgt_ref = rest
            sums = [dsh_ref, dsc_ref, dg_ref, dgt_ref]

        @pl.when(pl.program_id(1) == 0)
        def _():
            for ref in sums:
                ref[...] = jnp.zeros_like(ref)
        dhv = _dot(mm[0][...], mm[1][...], NN)
        for p in range(2, n_mm, 2):
            dhv = dhv + _dot(mm[p][...], mm[p + 1][...], NN)
        xv, g = x_ref[...], g_ref[...]
        r = lax.rsqrt(jnp.mean(xv * xv, axis=-1, keepdims=True) + EPS)
        xhat = xv * r
        dsh_ref[0] += jnp.sum(dhv, axis=0, keepdims=True)
        dsc_ref[0] += jnp.sum(dhv * (xhat * g), axis=0, keepdims=True)
        dn = dhv * (1.0 + sc_ref[0])
        dg_ref[0] += jnp.sum(dn * xhat, axis=0, keepdims=True)
        dxh = dn * g
        out = dr_ref[...] + r * (dxh - xhat * jnp.mean(dxh * xhat, axis=-1, keepdims=True))
        o_ref[...] = out
        if below is not None:
            dy_ref[...] = (out * gt_ref[0]).astype(BF16)
            dgt_ref[0] += jnp.sum(out * y_ref[...].astype(F32), axis=0, keepdims=True)

    spec_t = pl.BlockSpec((tm, D), lambda b, i: (b * nb + i, 0))
    spec_b = pl.BlockSpec((1, 1, D), lambda b, i: (b, 0, 0))
    red = jax.ShapeDtypeStruct((B, 1, D), F32)
    in_specs, args = [], []
    for a, w in pairs:
        K = a.shape[1]
        in_specs += [pl.BlockSpec((tm, K), lambda b, i: (b * nb + i, 0)),
                     pl.BlockSpec((K, D), lambda b, i: (0, 0), pipeline_mode=pl.Buffered(1))]
        args += [a, w]
    in_specs += [spec_t, spec_t, pl.BlockSpec((1, D), lambda b, i: (0, 0)), spec_b]
    args += [x, dres, gain, sc]
    out_specs = [spec_t, spec_b, spec_b, spec_b]
    out_shape = [jax.ShapeDtypeStruct((T, D), F32), red, red, red]
    if below is not None:
        in_specs += [spec_t, spec_b]
        out_specs += [spec_t, spec_b]
        out_shape += [jax.ShapeDtypeStruct((T, D), BF16), red]
        args += list(below)
    return _call_behind(body, name, (B, nb), in_specs, out_specs, out_shape, args, exchange, True)


def _gate_bwd(dx, y, gate, S, name):
    T, D = dx.shape
    B = T // S
    tm = _pick(S, 512, 8)
    nb = S // tm

    def body(dx_ref, y_ref, g_ref, dy_ref, dg_ref):
        @pl.when(pl.program_id(1) == 0)
        def _():
            dg_ref[...] = jnp.zeros_like(dg_ref)
        d = dx_ref[...]
        dy_ref[...] = (d * g_ref[0]).astype(BF16)
        dg_ref[0] += jnp.sum(d * y_ref[...].astype(F32), axis=0, keepdims=True)

    spec_t = pl.BlockSpec((tm, D), lambda b, i: (b * nb + i, 0))
    spec_b = pl.BlockSpec((1, 1, D), lambda b, i: (b, 0, 0))
    return pl.pallas_call(
        body, name=name, grid=(B, nb),
        in_specs=[spec_t, spec_t, spec_b],
        out_specs=[spec_t, spec_b],
        out_shape=[jax.ShapeDtypeStruct((T, D), BF16), jax.ShapeDtypeStruct((B, 1, D), F32)],
        compiler_params=_params("parallel", "arbitrary"),
    )(dx, y, gate)


def _loss_head(y, target, name):
    T, D = y.shape
    tm = _pick(T, 512, 8)

    def body(y_ref, t_ref, dy_ref, l_ref):
        @pl.when(pl.program_id(0) == 0)
        def _():
            l_ref[...] = jnp.zeros_like(l_ref)
        e = y_ref[...] - t_ref[...]
        dy_ref[...] = e * (1.0 / D)
        l_ref[...] += 0.5 * jnp.sum(jnp.mean(e * e, axis=-1, keepdims=True), axis=0, keepdims=True)

    spec = pl.BlockSpec((tm, D), lambda i: (i, 0))
    return pl.pallas_call(
        body, name=name, grid=(T // tm,),
        in_specs=[spec, spec],
        out_specs=[spec, pl.BlockSpec((8, LANES), lambda i: (0, 0))],
        out_shape=[jax.ShapeDtypeStruct((T, D), F32), jax.ShapeDtypeStruct((8, LANES), F32)],
        compiler_params=_params("arbitrary"),
    )(y, target)


def _ada_fwd(c_all, ada_w, bias, name):
    NB, D = c_all.shape
    L, _, W = ada_w.shape

    def body(c_ref, w_ref, b_ref, o_ref):
        cv = c_ref[...]
        cond = cv * jax.nn.sigmoid(cv)
        o_ref[0] = _dot(cond, w_ref[0], NN, HIGH) + b_ref[0]

    return pl.pallas_call(
        body, name=name, grid=(L,),
        in_specs=[pl.BlockSpec((NB, D), lambda l: (0, 0)), pl.BlockSpec((1, D, W), lambda l: (l, 0, 0)),
                  pl.BlockSpec((1, 1, W), lambda l: (l, 0, 0))],
        out_specs=pl.BlockSpec((1, NB, W), lambda l: (l, 0, 0)),
        out_shape=jax.ShapeDtypeStruct((L, NB, W), F32),
        compiler_params=_params("parallel"),
    )(c_all, ada_w, bias)


def _ada_bwd(c_all, dmod, name):
    NB, D = c_all.shape
    L, _, W = dmod.shape

    def body(c_ref, d_ref, o_ref):
        cv = c_ref[...]
        cond = cv * jax.nn.sigmoid(cv)
        o_ref[0] = _dot(cond, d_ref[0], TN, HIGH)

    return pl.pallas_call(
        body, name=name, grid=(L,),
        in_specs=[pl.BlockSpec((NB, D), lambda l: (0, 0)), pl.BlockSpec((1, NB, W), lambda l: (l, 0, 0))],
        out_specs=pl.BlockSpec((1, D, W), lambda l: (l, 0, 0)),
        out_shape=jax.ShapeDtypeStruct((L, D, W), F32),
        compiler_params=_params("parallel"),
    )(c_all, dmod)


def _lo_mask(shape):
    return lax.broadcasted_iota(jnp.int32, shape, len(shape) - 1) < HEAD


def _head_sum_matrix():
    r = lax.broadcasted_iota(jnp.int32, (LANES, LANES), 0) // HEAD
    c = lax.broadcasted_iota(jnp.int32, (LANES, LANES), 1) // HEAD
    return (r == c).astype(BF16)


def _head_sum(x, P):
    hi = x.astype(BF16)
    lo = (x - hi.astype(F32)).astype(BF16)
    return _dot(hi, P, NN) + _dot(lo, P, NN)


def _rope(y, cs, s1, s2):
    return y * cs + pltpu.roll(y, LANES - ROT // 2, 1) * s1 + pltpu.roll(y, ROT // 2, 1) * s2


def _rope_bwd(d, cs, s1, s2):
    return d * cs + pltpu.roll(d * s1, ROT // 2, 1) + pltpu.roll(d * s2, LANES - ROT // 2, 1)


def _qk_prep(qkv, cs, s1, s2, qg, kg, name):
    T, W = qkv.shape
    NQ = W - 2 * LANES
    tm = _pick(T, 512, 8)

    def body(x_ref, cs_ref, s1_ref, s2_ref, qg_ref, kg_ref, q_ref, k_ref, v_ref):
        P = _head_sum_matrix()
        cs_, s1_, s2_ = cs_ref[...], s1_ref[...], s2_ref[...]
        lo = _lo_mask((tm, LANES))

        def norm_rope(xv, g):
            ms = _head_sum(xv * xv, P) * (1.0 / HEAD)
            return _rope(xv * lax.rsqrt(ms + EPS) * g, cs_, s1_, s2_)

        for j in range(NQ // LANES):
            q_ref[:, j * LANES:(j + 1) * LANES] = norm_rope(x_ref[:, j * LANES:(j + 1) * LANES], qg_ref[...]).astype(BF16)
        kr = norm_rope(x_ref[:, NQ:NQ + LANES], kg_ref[...])
        ks = pltpu.roll(kr, HEAD, 1)
        k_ref[:, :LANES] = jnp.where(lo, kr, ks).astype(BF16)
        k_ref[:, LANES:] = jnp.where(lo, ks, kr).astype(BF16)
        vr = x_ref[:, NQ + LANES:]
        vs = pltpu.roll(vr, HEAD, 1)
        v_ref[:, :LANES] = jnp.where(lo, vr, vs).astype(BF16)
        v_ref[:, LANES:] = jnp.where(lo, vs, vr).astype(BF16)

    spec_t = pl.BlockSpec((tm, LANES), lambda i: (i, 0))
    spec_g = pl.BlockSpec((1, LANES), lambda i: (0, 0))
    return pl.pallas_call(
        body, name=name, grid=(T // tm,),
        in_specs=[pl.BlockSpec((tm, W), lambda i: (i, 0)), spec_t, spec_t, spec_t, spec_g, spec_g],
        out_specs=[pl.BlockSpec((tm, NQ), lambda i: (i, 0)), pl.BlockSpec((tm, 2 * LANES), lambda i: (i, 0)),
                   pl.BlockSpec((tm, 2 * LANES), lambda i: (i, 0))],
        out_shape=[jax.ShapeDtypeStruct((T, NQ), BF16), jax.ShapeDtypeStruct((T, 2 * LANES), BF16),
                   jax.ShapeDtypeStruct((T, 2 * LANES), BF16)],
        compiler_params=_params("parallel"),
    )(qkv, cs, s1, s2, qg, kg)


def _stack_heads(x2):
    lo = _lo_mask(x2.shape)
    z = jnp.zeros_like(x2)
    return jnp.concatenate([jnp.where(lo, x2, z), jnp.where(lo, z, x2)], axis=0)


def _unstack_heads(xs):
    r = xs.shape[0] // 2
    return jnp.where(_lo_mask((r, LANES)), xs[:r], xs[r:])


def _swa_valid(i):
    qo = lax.broadcasted_iota(jnp.int32, (2 * BLK, 2 * BLK), 0) % BLK
    kc_ = lax.broadcasted_iota(jnp.int32, (2 * BLK, 2 * BLK), 1)
    rel = qo + BLK - kc_
    return (rel >= 0) & (rel < BLK) & ((kc_ >= BLK) | (i > 0))


def _swa_scores(q2, kk, sink2, valid):
    qs = _stack_heads(q2) * SCALE
    s = _dot(qs, kk, NT)
    sk = jnp.concatenate([jnp.broadcast_to(sink2[:, 0:1], (BLK, 1)), jnp.broadcast_to(sink2[:, HEAD:HEAD + 1], (BLK, 1))], axis=0)
    return qs, jnp.where(valid, s, NEG), sk


def _swa_fwd(q, kd, vd, sink2, B, name, gather=None):
    T, NQ = q.shape
    NP = NQ // LANES
    nq = T // B // BLK
    NG = kd.shape[1] // LANES
    grp = NP // NG

    def body(q_ref, kp_ref, kc_ref, vp_ref, vc_ref, s_ref, o_ref, l_ref):
        valid = _swa_valid(pl.program_id(2))
        kk = jnp.concatenate([kp_ref[...], kc_ref[...]], axis=0)
        vs = _stack_heads(jnp.concatenate([vp_ref[...], vc_ref[...]], axis=0))
        sls = [slice(jj * LANES, (jj + 1) * LANES) for jj in range(grp)]
        sc = [_swa_scores(q_ref[:, sl], kk, s_ref[jj], valid) for jj, sl in enumerate(sls)]
        ms = [jnp.maximum(jnp.max(s, axis=1, keepdims=True), sk) for _, s, sk in sc]
        ps = [jnp.exp(s - m) for (_, s, _), m in zip(sc, ms)]
        ls = [jnp.sum(p, axis=1, keepdims=True) + jnp.exp(sk - m) for p, (_, _, sk), m in zip(ps, sc, ms)]
        ps = [(p * (1.0 / l)).astype(BF16) for p, l in zip(ps, ls)]
        os_ = [_dot(jnp.concatenate([p[:BLK], p[BLK:]], axis=1), vs, NN) for p in ps]
        for sl, o, m, l in zip(sls, os_, ms, ls):
            o_ref[:, sl] = o.astype(BF16)
            l_ref[:, sl] = _unstack_heads(jnp.broadcast_to(m + jnp.log(l), (2 * BLK, LANES)))

    spec_q = pl.BlockSpec((BLK, grp * LANES), lambda b, g, i: (b * nq + i, g))
    spec_p = pl.BlockSpec((BLK, LANES), lambda b, g, i: (b * nq + jnp.maximum(i - 1, 0), g))
    spec_c = pl.BlockSpec((BLK, LANES), lambda b, g, i: (b * nq + i, g))
    in_specs = [spec_q, spec_p, spec_c, spec_p, spec_c, pl.BlockSpec((grp, 1, LANES), lambda b, g, i: (g, 0, 0))]
    out_shape = [jax.ShapeDtypeStruct((T, NQ), BF16), jax.ShapeDtypeStruct((T, NQ), F32)]
    return _call_behind(body, name, (B, NG, nq), in_specs, [spec_q, spec_q], out_shape, [q, kd, kd, vd, vd, sink2], gather, False)


def _swa_bwd(q, kd, vd, sink2, do, lse, B, name, exchange=None):
    T, NQ = q.shape
    NP = NQ // LANES
    nq = T // B // BLK
    NG = kd.shape[1] // LANES
    grp = NP // NG

    def body(q_ref, kp_ref, kc_ref, vp_ref, vc_ref, s_ref, do_ref, l_ref,
             dq_ref, dkc_ref, dkp_ref, dvc_ref, dvp_ref, ds_ref):
        b, i = pl.program_id(1), pl.program_id(2)

        @pl.when((b == 0) & (i == 0))
        def _():
            ds_ref[...] = jnp.zeros_like(ds_ref)
        valid = _swa_valid(i)
        kk = jnp.concatenate([kp_ref[...], kc_ref[...]], axis=0)
        vv = jnp.concatenate([vp_ref[...], vc_ref[...]], axis=0)
        sls = [slice(jj * LANES, (jj + 1) * LANES) for jj in range(grp)]
        sc = [_swa_scores(q_ref[:, sl], kk, s_ref[jj], valid) for jj, sl in enumerate(sls)]
        dos = [_stack_heads(do_ref[:, sl]) for sl in sls]
        dps = [_dot(d, vv, NT) for d in dos]
        lses = [jnp.concatenate([l_ref[:, sl][:, 0:1], l_ref[:, sl][:, HEAD:HEAD + 1]], axis=0) for sl in sls]
        ps = [jnp.exp(s - lse) for (_, s, _), lse in zip(sc, lses)]
        deltas = [jnp.sum(p * dp, axis=1, keepdims=True) for p, dp in zip(ps, dps)]
        dscs = [(p * (dp - delta)).astype(BF16) for p, dp, delta in zip(ps, dps, deltas)]
        dqs = [_dot(dsc, kk, NN) for dsc in dscs]
        dk = jnp.zeros((2 * BLK, LANES), F32)
        dv = jnp.zeros((2 * BLK, LANES), F32)
        for jj, sl in enumerate(sls):
            dsk = -jnp.exp(sc[jj][2] - lses[jj]) * deltas[jj]
            dsk_lo = jnp.sum(dsk[:BLK], axis=0, keepdims=True)
            dsk_hi = jnp.sum(dsk[BLK:], axis=0, keepdims=True)
            ds_ref[jj] += jnp.where(_lo_mask((1, LANES)), dsk_lo, dsk_hi)
            dq_ref[:, sl] = _unstack_heads(dqs[jj]) * SCALE
            dk = dk + _dot(dscs[jj], sc[jj][0], TN)
            dv = dv + _dot(ps[jj].astype(BF16), dos[jj], TN)
        dkp_ref[...] = dk[:BLK]
        dkc_ref[...] = dk[BLK:]
        dvp_ref[...] = dv[:BLK]
        dvc_ref[...] = dv[BLK:]

    spec_q = pl.BlockSpec((BLK, grp * LANES), lambda g, b, i: (b * nq + i, g))
    spec_p = pl.BlockSpec((BLK, LANES), lambda g, b, i: (b * nq + jnp.maximum(i - 1, 0), g))
    spec_c = pl.BlockSpec((BLK, LANES), lambda g, b, i: (b * nq + i, g))
    spec_s = pl.BlockSpec((grp, 1, LANES), lambda g, b, i: (g, 0, 0))
    kv = jax.ShapeDtypeStruct((T, NG * LANES), F32)
    in_specs = [spec_q, spec_p, spec_c, spec_p, spec_c, spec_s, spec_q, spec_q]
    out_specs = [spec_q, spec_c, spec_c, spec_c, spec_c, spec_s]
    out_shape = [jax.ShapeDtypeStruct((T, NQ), F32), kv, kv, kv, kv, jax.ShapeDtypeStruct((NP, 1, LANES), F32)]
    return _call_behind(body, name, (NG, B, nq), in_specs, out_specs, out_shape, [q, kd, kd, vd, vd, sink2, do, lse],
                        exchange, True)


def _qk_prep_bwd(qkv, cs, s1, s2, qg, kg, dq, dkc, dkp, dvc, dvp, B, name):
    T, W = qkv.shape
    NQ = W - 2 * LANES
    NP = NQ // LANES
    nq = T // B // BLK

    def body(x_ref, cs_ref, s1_ref, s2_ref, qg_ref, kg_ref, dq_ref, dkc_ref, dkp_ref, dvc_ref, dvp_ref,
             o_ref, dqg_ref, dkg_ref):
        b, i = pl.program_id(0), pl.program_id(1)

        @pl.when((b == 0) & (i == 0))
        def _():
            dqg_ref[...] = jnp.zeros_like(dqg_ref)
            dkg_ref[...] = jnp.zeros_like(dkg_ref)
        P = _head_sum_matrix()
        cs_, s1_, s2_ = cs_ref[...], s1_ref[...], s2_ref[...]
        lo = _lo_mask((BLK, LANES))
        has_next = (i + 1 < nq).astype(F32)

        def norm_rope_bwd(xv, g, d):
            du = _rope_bwd(d, cs_, s1_, s2_)
            r = lax.rsqrt(_head_sum(xv * xv, P) * (1.0 / HEAD) + EPS)
            xhat = xv * r
            dgain = jnp.sum(du * xhat, axis=0, keepdims=True)
            uu = du * g
            dx = r * (uu - xhat * (_head_sum(uu * xhat, P) * (1.0 / HEAD)))
            return dx, dgain + pltpu.roll(dgain, HEAD, 1)

        dqg = jnp.zeros((1, LANES), F32)
        for j in range(NP):
            sl = slice(j * LANES, (j + 1) * LANES)
            dx, dg = norm_rope_bwd(x_ref[:, sl], qg_ref[...], dq_ref[:, sl])
            o_ref[:, sl] = dx.astype(BF16)
            dqg = dqg + dg
        dqg_ref[...] += dqg

        def fold(c_ref, p_ref, g):
            sl = slice(g * LANES, (g + 1) * LANES)
            t = c_ref[:, sl] + has_next * p_ref[:, sl]
            return t + pltpu.roll(t, HEAD, 1)

        dk = jnp.where(lo, fold(dkc_ref, dkp_ref, 0), fold(dkc_ref, dkp_ref, 1))
        dx, dg = norm_rope_bwd(x_ref[:, NQ:NQ + LANES], kg_ref[...], dk)
        o_ref[:, NQ:NQ + LANES] = dx.astype(BF16)
        dkg_ref[...] += dg
        dv = jnp.where(lo, fold(dvc_ref, dvp_ref, 0), fold(dvc_ref, dvp_ref, 1))
        o_ref[:, NQ + LANES:] = dv.astype(BF16)

    spec_t = pl.BlockSpec((BLK, LANES), lambda b, i: (b * nq + i, 0))
    spec_g = pl.BlockSpec((1, LANES), lambda b, i: (0, 0))
    spec_c = pl.BlockSpec((BLK, 2 * LANES), lambda b, i: (b * nq + i, 0))
    spec_n = pl.BlockSpec((BLK, 2 * LANES), lambda b, i: (b * nq + jnp.minimum(i + 1, nq - 1), 0))
    row = jax.ShapeDtypeStruct((1, LANES), F32)
    return pl.pallas_call(
        body, name=name, grid=(B, nq),
        in_specs=[pl.BlockSpec((BLK, W), lambda b, i: (b * nq + i, 0)), spec_t, spec_t, spec_t, spec_g, spec_g,
                  pl.BlockSpec((BLK, NQ), lambda b, i: (b * nq + i, 0)), spec_c, spec_n, spec_c, spec_n],
        out_specs=[pl.BlockSpec((BLK, W), lambda b, i: (b * nq + i, 0)), spec_g, spec_g],
        out_shape=[jax.ShapeDtypeStruct((T, W), BF16), row, row],
        compiler_params=_params("arbitrary", "arbitrary"),
    )(qkv, cs, s1, s2, qg, kg, dq, dkc, dkp, dvc, dvp)


SB_TILE = 256
SB_UNROLL = 4
SB_UNROLL_BWD = 2


def _split_heads(x2, scale=None):
    lo = _lo_mask(x2.shape)
    z = jnp.zeros_like(x2)
    if scale is not None:
        x2 = x2 * scale
    return jnp.where(lo, x2, z), jnp.where(lo, z, x2)


def _sb_terms(qh, kj, diagonal):
    z = _dot(qh, kj, NT)
    e = jnp.exp(-jnp.abs(z))
    lb = jnp.minimum(z, 0.0) - jnp.log(1.0 + e)
    L = lb - z
    if not diagonal:
        return lb, L, None, z, e
    strict = lax.broadcasted_iota(jnp.int32, z.shape, 1) < lax.broadcasted_iota(jnp.int32, z.shape, 0)
    return lb, jnp.where(strict, L, 0.0), strict, z, e


def _tri(n, cmp):
    r = lax.broadcasted_iota(jnp.int32, (n, n), 0)
    c = lax.broadcasted_iota(jnp.int32, (n, n), 1)
    return cmp(r, c).astype(BF16)


def _by_value(r, fns, carry):
    if len(fns) == 1:
        return fns[0](carry)
    half = len(fns) // 2
    return lax.cond(r < half, lambda cr: _by_value(r, fns[:half], cr), lambda cr: _by_value(r - half, fns[half:], cr), carry)


def _sb_fwd(qkv, B, name, gather=None):
    T, W = qkv.shape
    NQ = W // 3
    NP = NQ // LANES
    S = T // B
    tq = min(SB_TILE, S)
    nq = S // tq
    grid = (B, NP, nq)

    def body(q_ref, k_ref, v_ref, o_ref, t_ref):
        i = pl.program_id(2)
        qh = _split_heads(q_ref[...], SCALE)
        U = _tri(tq, lambda r, c: r > c)

        def sweep(tiles, cs, acc):
            chains = [(t, h) for t in range(len(tiles)) for h in range(2)]
            rows = [pl.ds(pl.multiple_of(j * tq, tq), tq) for j, _ in tiles]
            ks = [k_ref[r, :] for r in rows]
            vs = [_split_heads(v_ref[r, :]) for r in rows]
            terms = {(t, h): _sb_terms(qh[h], ks[t], tiles[t][1]) for t, h in chains}
            carry = {}
            for h in range(2):
                c = cs[h]
                for t in range(len(tiles)):
                    carry[t, h] = c
                    c = c + jnp.sum(terms[t, h][1], axis=1, keepdims=True)
                cs = cs[:h] + (c,) + cs[h + 1:]
            cum = {ch: _dot(terms[ch][1].astype(BF16), U, NN) for ch in chains}
            for ch in chains:
                a = jnp.exp(terms[ch][0] + (cum[ch] + carry[ch]))
                if tiles[ch[0]][1]:
                    a = jnp.where(terms[ch][2], a, 0.0)
                acc = acc + _dot(a.astype(BF16), vs[ch[0]][ch[1]], NN)
            return cs, acc

        zero = jnp.zeros((tq, 1), F32)
        rem = i % SB_UNROLL
        heads = [lambda cr, k=k: sweep([(i, True)] + [(i - 1 - t, False) for t in range(k)], *cr) for k in range(SB_UNROLL)]
        carry = _by_value(rem, heads, ((zero, zero), jnp.zeros((tq, LANES), F32)))
        step = lambda n, cr: sweep([(i - 1 - rem - SB_UNROLL * n - t, False) for t in range(SB_UNROLL)], *cr)
        cs, acc = lax.fori_loop(0, i // SB_UNROLL, step, carry)
        o_ref[...] = acc.astype(BF16)
        t_ref[...] = jnp.where(_lo_mask((tq, LANES)), cs[0], cs[1])

    spec_q = pl.BlockSpec((tq, LANES), lambda b, p, i: (b * nq + i, p))
    in_specs = [spec_q, pl.BlockSpec((S, LANES), lambda b, p, i: (b, NP + p)),
                pl.BlockSpec((S, LANES), lambda b, p, i: (b, 2 * NP + p))]
    out_shape = [jax.ShapeDtypeStruct((T, NQ), BF16), jax.ShapeDtypeStruct((T, NQ), F32)]
    return _call_behind(body, name, grid, in_specs, [spec_q, spec_q], out_shape, [qkv, qkv, qkv], gather, False)


def _sb_bwd(qkv, q_t, do, do_t, tot, B, name, exchange=None):
    T, W = qkv.shape
    NQ = W // 3
    NP = NQ // LANES
    S = T // B
    tq = min(SB_TILE, S)
    nq = S // tq
    grid = (B, NP, nq)

    def body(q_ref, k_ref, v_ref, do_ref, qt_ref, dot_ref, t_ref, dq_ref, dk_ref, dv_ref):
        i = pl.program_id(2)

        @pl.when(i == 0)
        def _():
            dk_ref[...] = jnp.zeros_like(dk_ref)
            dv_ref[...] = jnp.zeros_like(dv_ref)
        qh = _split_heads(q_ref[...], SCALE)
        doh = _split_heads(do_ref[...])
        top = lax.broadcasted_iota(jnp.int32, (LANES, tq), 0) < HEAD
        zt = jnp.zeros((LANES, tq), BF16)
        qt = qt_ref[...] * SCALE
        qth = (jnp.where(top, qt, zt), jnp.where(top, zt, qt))
        doth = (jnp.where(top, dot_ref[...], zt), jnp.where(top, zt, dot_ref[...]))
        tt = t_ref[...]
        tot = (tt[:, 0:1], tt[:, HEAD:HEAD + 1])
        Urev = _tri(tq, lambda r, c: r > c)
        Uexc = _tri(tq, lambda r, c: r < c)

        def sweep(tiles, carry):
            nt = len(tiles)
            chains = [(t, h) for t in range(nt) for h in range(2)]
            rows = [pl.ds(pl.multiple_of(j * tq, tq), tq) for j, _ in tiles]
            ks = [k_ref[r, :] for r in rows]
            vs = [v_ref[r, :] for r in rows]
            terms = {(t, h): _sb_terms(qh[h], ks[t], tiles[t][1]) for t, h in chains}
            cc = [carry[h][0] for h in range(2)]
            later = {}
            for t, h in chains:
                cc[h] = cc[h] + jnp.sum(terms[t, h][1], axis=1, keepdims=True)
                later[t, h] = tot[h] - cc[h]
            cum = {ch: _dot(terms[ch][1].astype(BF16), Urev, NN) for ch in chains}
            da = {(t, h): _dot(doh[h], vs[t], NT) for t, h in chains}
            a, g, before = {}, {}, {}
            cg = [carry[h][1] for h in range(2)]
            for ch in chains:
                a[ch] = jnp.exp(terms[ch][0] + (cum[ch] + later[ch]))
                if tiles[ch[0]][1]:
                    a[ch] = jnp.where(terms[ch][2], a[ch], 0.0)
                g[ch] = a[ch] * da[ch]
                before[ch] = cg[ch[1]]
                cg[ch[1]] = cg[ch[1]] + jnp.sum(g[ch], axis=1, keepdims=True)
            G = {ch: _dot(g[ch].astype(BF16), Uexc, NN) for ch in chains}
            dz = {}
            for ch in chains:
                d = g[ch] - jnp.exp(terms[ch][0]) * (g[ch] + (G[ch] + before[ch]))
                if tiles[ch[0]][1]:
                    d = jnp.where(terms[ch][2], d, 0.0)
                dz[ch] = d.astype(BF16)
            dq = [carry[h][2] for h in range(2)]
            for t, h in chains:
                dq[h] = dq[h] + _dot(dz[t, h], ks[t], NN)
            for t in range(nt):
                dk_ref[:, rows[t]] += _dot(qth[0], dz[t, 0], NN) + _dot(qth[1], dz[t, 1], NN)
                dv_ref[:, rows[t]] += _dot(doth[0], a[t, 0].astype(BF16), NN) + _dot(doth[1], a[t, 1].astype(BF16), NN)
            return tuple((cc[h], cg[h], dq[h]) for h in range(2))

        zero = jnp.zeros((tq, 1), F32)
        zq = jnp.zeros((tq, LANES), F32)
        step = lambda n, cr: sweep([(SB_UNROLL_BWD * n + t, False) for t in range(SB_UNROLL_BWD)], cr)
        carry = lax.fori_loop(0, i // SB_UNROLL_BWD, step, ((zero, zero, zq), (zero, zero, zq)))
        tails = [lambda cr, k=k: sweep([(i - k + t, False) for t in range(k)] + [(i, True)], cr) for k in range(SB_UNROLL_BWD)]
        carry = _by_value(i % SB_UNROLL_BWD, tails, carry)
        dq_ref[...] = jnp.where(_lo_mask((tq, LANES)), carry[0][2], carry[1][2]) * SCALE

    spec_q = pl.BlockSpec((tq, LANES), lambda b, p, i: (b * nq + i, p))
    spec_t = pl.BlockSpec((LANES, tq), lambda b, p, i: (p, b * nq + i))
    spec_s = pl.BlockSpec((LANES, S), lambda b, p, i: (b * NP + p, 0))
    key_side = jax.ShapeDtypeStruct((B * NQ, S), F32)
    in_specs = [spec_q, pl.BlockSpec((S, LANES), lambda b, p, i: (b, NP + p)),
                pl.BlockSpec((S, LANES), lambda b, p, i: (b, 2 * NP + p)), spec_q, spec_t, spec_t, spec_q]
    out_specs = [spec_q, spec_s, spec_s]
    out_shape = [jax.ShapeDtypeStruct((T, NQ), F32), key_side, key_side]
    args = [qkv, qkv, qkv, do, q_t, do_t, tot]
    return _call_behind(body, name, grid, in_specs, out_specs, out_shape, args, exchange, True)


def _adamw(w, g, m, v, name):
    shape = w.shape
    cols = shape[-1]
    rows = math.prod(shape[:-1])
    tr = _pick(rows, max(8, (1 << 19) // max(cols, LANES) // 8 * 8), 8)

    def body(w_ref, g_ref, m_ref, v_ref, d_ref, mo_ref, vo_ref):
        gv = g_ref[...]
        mn = ADAM_B1 * m_ref[...] + (1.0 - ADAM_B1) * gv
        vn = ADAM_B2 * v_ref[...] + (1.0 - ADAM_B2) * (gv * gv)
        m_hat = mn / (1.0 - ADAM_B1 ** ADAM_STEP)
        v_hat = vn / (1.0 - ADAM_B2 ** ADAM_STEP)
        d_ref[...] = -ADAM_LR * (m_hat / (jnp.sqrt(v_hat) + ADAM_EPS) + ADAM_WD * w_ref[...])
        mo_ref[...] = mn
        vo_ref[...] = vn

    spec = pl.BlockSpec((tr, cols), lambda i: (i, 0))
    out = jax.ShapeDtypeStruct((rows, cols), F32)
    d, mn, vn = pl.pallas_call(
        body, name=name, grid=(rows // tr,),
        in_specs=[spec] * 4, out_specs=[spec] * 3, out_shape=[out] * 3,
        compiler_params=_params("parallel"),
    )(w.reshape(rows, cols), g.reshape(rows, cols), m.reshape(rows, cols), v.reshape(rows, cols))
    return d.reshape(shape), mn.reshape(shape), vn.reshape(shape)


def _pad_rows(a, rows):
    return jnp.pad(a, ((0, rows - a.shape[0]), (0, 0)))


def kernel(x, c, positions, ada_w, ada_b, norm1_g, norm2_g, wqkv_a, q_norm_a, k_norm_a, sinks_a, wo_a, wqkv_b, wo_b, w_gate, w_up, w_down, loss_target, m_ada_w, m_ada_b, m_norm1_g, m_norm2_g, m_wqkv_a, m_q_norm_a, m_k_norm_a, m_sinks_a, m_wo_a, m_wqkv_b, m_wo_b, m_w_gate, m_w_up, m_w_down, v_ada_w, v_ada_b, v_norm1_g, v_norm2_g, v_wqkv_a, v_q_norm_a, v_k_norm_a, v_sinks_a, v_wo_a, v_wqkv_b, v_wo_b, v_w_gate, v_w_up, v_w_down):
    B, S, D = x.shape
    T = B * S
    L = ada_w.shape[0]
    NA, NB_ = wqkv_a.shape[0], wqkv_b.shape[0]
    me = 4 * lax.axis_index("x") + 2 * lax.axis_index("y") + lax.axis_index("c")
    xt = x.reshape(T, D)

    col_sharded = {"qkv_a": wqkv_a, "qkv_b": wqkv_b, "gate": w_gate, "up": w_up}
    row_sharded = {"wo_a": wo_a, "wo_b": wo_b, "down": w_down}

    def shard_rows(key):
        kind, idx = key
        return col_sharded[kind][idx].T if kind in col_sharded else row_sharded[kind][idx]

    def layer_keys(l):
        mix = "a" if l % 2 == 0 else "b"
        return [("qkv_" + mix, l // 2), ("wo_" + mix, l // 2), ("gate", l), ("up", l), ("down", l)]

    def unpack(buf, keys, reshape):
        out, off = {}, 0
        for key in keys:
            rows = shard_rows(key).shape[0]
            out[key] = reshape(buf[..., off:off + rows, :], rows)
            off += rows
        return out

    first_b = 1
    keys_early = layer_keys(0)[:2]
    keys_mid = layer_keys(0)[2:] + [("qkv_b", 0)]
    keys_late = [k for l in range(1, L) for k in layer_keys(l) if k != ("qkv_b", 0)]
    pack = lambda keys: jnp.concatenate([shard_rows(k).astype(BF16) for k in keys], axis=0)
    full_rows = lambda b, rows: b.reshape(NDEV * rows, D)
    W = unpack(_all_gather(pack(keys_early), "ag_weights"), keys_early, full_rows)

    WA = ada_w.shape[2]
    c_all = _all_gather(c, "ag_c").reshape(NDEV * B, D)
    bias = lax.dynamic_slice_in_dim(ada_b, me * WA, WA, axis=1).reshape(L, 1, WA)
    mod_part = _ada_fwd(c_all, ada_w, bias, "ada_fwd")
    mod_all = _all_gather(mod_part.reshape(L * NDEV * B, WA), "ag_mod")
    mod_all = mod_all.reshape(NDEV, L, NDEV * B, WA).transpose(1, 2, 0, 3).reshape(L, NDEV * B, NDEV * WA)
    mod = lax.dynamic_slice_in_dim(mod_all, me * B, B, axis=1)
    mod = mod.reshape(L, B, 6, 1, D)
    sh1, sc1, g1, sh2, sc2, g2 = [mod[:, :, k] for k in range(6)]

    half = ROT // 2
    inv_freq = jnp.power(jnp.float32(ROPE_THETA), -jnp.arange(half, dtype=F32) * 2.0 / ROT)
    ang = positions.reshape(T, 1).astype(F32) * inv_freq[None, :]
    cos, sin = jnp.cos(ang), jnp.sin(ang)
    ones = jnp.ones((T, HEAD - ROT), F32)
    zeros = jnp.zeros((T, HEAD - ROT), F32)
    z8 = jnp.zeros((T, half), F32)
    cs = jnp.tile(jnp.concatenate([cos, cos, ones], axis=1), (1, 2))
    s1 = jnp.tile(jnp.concatenate([-sin, z8, zeros], axis=1), (1, 2))
    s2 = jnp.tile(jnp.concatenate([z8, sin, zeros], axis=1), (1, 2))

    saved = []
    xc = xt
    h1 = _norm_mod(xc, norm1_g[0:1], sc1[0], sh1[0], S, "norm1_0")
    for l in range(L):
        j = l // 2
        sv = dict(x_in=xc, h1=h1)
        if l % 2 == 0:
            qkv = _mm_nt(h1, W["qkv_a", j], F32, f"qkv_a_{l}")
            qg = jnp.tile(q_norm_a[j:j + 1], (1, 2))
            kg = jnp.tile(k_norm_a[j:j + 1], (1, 2))
            qn, kd, vd = _qk_prep(qkv, cs, s1, s2, qg, kg, f"qk_prep_{l}")
            sink2 = jnp.repeat(sinks_a[j].reshape(-1, 2), HEAD, axis=1).reshape(-1, 1, LANES)
            if l == 0:
                attn, lse, mid = _swa_fwd(qn, kd, vd, sink2, B, f"swa_fwd_{l}", gather=pack(keys_mid))
                W.update(unpack(mid, keys_mid, full_rows))
            else:
                attn, lse = _swa_fwd(qn, kd, vd, sink2, B, f"swa_fwd_{l}")
            sv.update(qkv=qkv, qg=qg, kg=kg, qn=qn, kd=kd, vd=vd, sink2=sink2, lse=lse)
            wo = W["wo_a", j]
        else:
            qkv = _mm_nt(h1, W["qkv_b", j], BF16, f"qkv_b_{l}")
            if l == first_b:
                attn, tot, late = _sb_fwd(qkv, B, f"sb_fwd_{l}", gather=pack(keys_late))
                W.update(unpack(late, keys_late, full_rows))
            else:
                attn, tot = _sb_fwd(qkv, B, f"sb_fwd_{l}")
            sv.update(qkv=qkv, tot=tot)
            wo = W["wo_b", j]
        y1, xm, h2 = _mm_res(attn, wo, xc, g1[l], S, f"attn_out_{l}", norm=(norm2_g[l:l + 1], sc2[l], sh2[l]))
        gate, up, act = _swiglu_fwd(h2, W["gate", l], W["up", l], f"swiglu_fwd_{l}")
        if l + 1 < L:
            y2, xc, h1 = _mm_res(act, W["down", l], xm, g2[l], S, f"mlp_out_{l}",
                                 norm=(norm1_g[l + 1:l + 2], sc1[l + 1], sh1[l + 1]))
        else:
            y2, xc = _mm_res(act, W["down", l], xm, g2[l], S, f"mlp_out_{l}")
        sv.update(attn=attn, y1=y1, x_mid=xm, h2=h2, gate=gate, up=up, act=act, y2=y2)
        saved.append(sv)

    dx, loss_tile = _loss_head(xc, loss_target.reshape(T, D), "loss_head")

    G = {}
    pack_grads = lambda keys: jnp.concatenate([G[k].reshape(NDEV, G[k].shape[0] // NDEV, D) for k in keys], axis=1)
    keys_hi = [k for l in range(first_b + 1, L) for k in layer_keys(l)] + layer_keys(first_b)[1:]
    keys_mlp0_g = [("qkv_b", 0), ("down", 0)]
    keys_mid_g = [("wo_a", 0), ("gate", 0), ("up", 0)]
    keys_lo = layer_keys(0)[:1]
    received_hi = received_mid = received_mlp0 = None
    dmod = [None] * L
    dn1, dn2 = [None] * L, [None] * L
    dqg, dkg, dsink = [None] * NA, [None] * NA, [None] * NA
    dy2, dg2 = _gate_bwd(dx, saved[L - 1]["y2"], g2[L - 1], S, "gate2_bwd_top")
    for l in reversed(range(L)):
        j = l // 2
        mix = "a" if l % 2 == 0 else "b"
        sv = saved[l]
        dgate, dup = _swiglu_bwd(dy2, W["down", l], sv["gate"], sv["up"], f"swiglu_bwd_{l}")
        G["down", l] = _mm_tn(sv["act"], dy2, f"dw_down_{l}")
        G["gate", l] = _mm_tn(dgate, sv["h2"], f"dw_gate_{l}")
        G["up", l] = _mm_tn(dup, sv["h2"], f"dw_up_{l}")
        n2 = _norm_mod_bwd(sv["x_mid"], [(dgate, W["gate", l]), (dup, W["up", l])], dx, norm2_g[l:l + 1], sc2[l], S,
                           f"norm2_bwd_{l}", below=(sv["y1"], g1[l]), exchange=pack_grads(keys_mlp0_g) if l == 0 else None)
        dxm, dsh2, dsc2, dn2[l], dy1, dg1 = n2[:6]
        if l == 0:
            received_mlp0 = n2[6]
        dattn = _mm_nt(dy1, W["wo_" + mix, j], BF16, f"dattn_{l}")
        G["wo_" + mix, j] = _mm_tn(sv["attn"], dy1, f"dw_o_{l}")
        if l % 2 == 0:
            swa_args = (sv["qn"], sv["kd"], sv["vd"], sv["sink2"], dattn, sv["lse"], B, f"swa_bwd_{l}")
            if l == 0:
                dq, dkc, dkp, dvc, dvp, dsink[j], received_mid = _swa_bwd(*swa_args, exchange=pack_grads(keys_mid_g))
            else:
                dq, dkc, dkp, dvc, dvp, dsink[j] = _swa_bwd(*swa_args)
            dqkv, dqg[j], dkg[j] = _qk_prep_bwd(sv["qkv"], cs, s1, s2, sv["qg"], sv["kg"], dq, dkc, dkp, dvc, dvp, B,
                                                f"qk_prep_bwd_{l}")
        else:
            nqb = sv["qkv"].shape[1] // 3
            sb_args = (sv["qkv"], sv["qkv"][:, :nqb].T, dattn, dattn.T, sv["tot"], B, f"sb_bwd_{l}")
            if l == first_b and keys_hi:
                dq, dk_t, dv_t, received_hi = _sb_bwd(*sb_args, exchange=pack_grads(keys_hi))
            else:
                dq, dk_t, dv_t = _sb_bwd(*sb_args)
            dk, dv = [t.reshape(B, nqb, S).transpose(0, 2, 1).reshape(T, nqb) for t in (dk_t, dv_t)]
            dqkv = jnp.concatenate([dq, dk, dv], axis=1).astype(BF16)
        G["qkv_" + mix, j] = _mm_tn(dqkv, sv["h1"], f"dw_qkv_{l}")
        n1_args = (sv["x_in"], [(dqkv, W["qkv_" + mix, j])], dxm, norm1_g[l:l + 1], sc1[l], S, f"norm1_bwd_{l}")
        dmod_l = [None, None, dg1, dsh2, dsc2, dg2]
        if l > 0:
            dx, dmod_l[0], dmod_l[1], dn1[l], dy2, dg2 = _norm_mod_bwd(*n1_args, below=(saved[l - 1]["y2"], g2[l - 1]))
        else:
            dx, dmod_l[0], dmod_l[1], dn1[l] = _norm_mod_bwd(*n1_args)
        dmod[l] = jnp.concatenate(dmod_l, axis=1)
    grad_x = dx.reshape(B, S, D)

    ndm = L * 6
    dmod_rows = jnp.stack(dmod, axis=1).reshape(B * ndm, D)
    misc = jnp.concatenate(
        [jnp.concatenate(dn1, axis=0).reshape(B * L, D), jnp.concatenate(dn2, axis=0).reshape(B * L, D),
         _pad_rows(jnp.concatenate([jnp.pad(r, ((0, 0), (0, D - LANES))) for r in dqg + dkg]
                                   + [jnp.pad(r[:, 0, ::HEAD].reshape(1, -1), ((0, 0), (0, D - 2 * r.shape[0]))) for r in dsink]
                                   + [jnp.pad(loss_tile[0:1, 0:1], ((0, 0), (0, D - 1)))], axis=0), 8)], axis=0)
    nmisc = misc.shape[0]
    small = _all_gather(jnp.concatenate([dmod_rows, _pad_rows(misc, -(-nmisc // 8) * 8)], axis=0), "ag_small")
    dmod_all = small[:, :B * ndm].reshape(NDEV * B, ndm, D)
    g_ada_b = _sum_leading(dmod_all, "sum_dmod").reshape(L, 6 * D)
    misc_sum = _sum_leading(small[:, B * ndm:], "sum_misc")
    g_n1 = misc_sum[0:B * L].reshape(L, B, D)
    g_n2 = misc_sum[B * L:2 * B * L].reshape(L, B, D)
    g_norm1 = _sum_leading(g_n1.transpose(1, 0, 2), "sum_n1")
    g_norm2 = _sum_leading(g_n2.transpose(1, 0, 2), "sum_n2")
    o = 2 * B * L
    g_qn = misc_sum[o:o + NA, :HEAD]
    g_kn = misc_sum[o + NA:o + 2 * NA, :HEAD]
    nsink = sinks_a.shape[1]
    g_sink = misc_sum[o + 2 * NA:o + 3 * NA, :nsink]
    loss = misc_sum[o + 3 * NA, 0]

    dmod_loc = lax.dynamic_slice_in_dim(dmod_all.reshape(NDEV * B, L, 6 * D), me * WA, WA, axis=2)
    g_ada_w = _ada_bwd(c_all, dmod_loc.transpose(1, 0, 2), "ada_bwd")

    shard = unpack(_sum_leading(_exchange(pack_grads(keys_lo), "grad_exchange"), "grad_sum"), keys_lo, lambda b, rows: b)
    shard.update(unpack(_sum_leading(received_mid, "grad_sum_mid"), keys_mid_g, lambda b, rows: b))
    shard.update(unpack(_sum_leading(received_mlp0, "grad_sum_mlp0"), keys_mlp0_g, lambda b, rows: b))
    if received_hi is not None:
        shard.update(unpack(_sum_leading(received_hi, "grad_sum_hi"), keys_hi, lambda b, rows: b))

    def stacked(kind, n):
        return jnp.stack([shard[kind, i].T if kind in col_sharded else shard[kind, i] for i in range(n)])

    gw_qkv_a, gw_qkv_b, gw_gate, gw_up = stacked("qkv_a", NA), stacked("qkv_b", NB_), stacked("gate", L), stacked("up", L)
    gw_wo_a, gw_wo_b, gw_down = stacked("wo_a", NA), stacked("wo_b", NB_), stacked("down", L)

    grads = [g_ada_w, g_ada_b, g_norm1, g_norm2, gw_qkv_a, g_qn, g_kn, g_sink, gw_wo_a, gw_qkv_b, gw_wo_b,
             gw_gate, gw_up, gw_down]
    ws = [ada_w, ada_b, norm1_g, norm2_g, wqkv_a, q_norm_a, k_norm_a, sinks_a, wo_a, wqkv_b, wo_b, w_gate, w_up, w_down]
    ms = [m_ada_w, m_ada_b, m_norm1_g, m_norm2_g, m_wqkv_a, m_q_norm_a, m_k_norm_a, m_sinks_a, m_wo_a, m_wqkv_b,
          m_wo_b, m_w_gate, m_w_up, m_w_down]
    vs = [v_ada_w, v_ada_b, v_norm1_g, v_norm2_g, v_wqkv_a, v_q_norm_a, v_k_norm_a, v_sinks_a, v_wo_a, v_wqkv_b,
          v_wo_b, v_w_gate, v_w_up, v_w_down]
    deltas, new_m, new_v = [], [], []
    for k, (w, g, m, v) in enumerate(zip(ws, grads, ms, vs)):
        g = g.reshape(w.shape)
        d, mn, vn = _adamw(w, g, m, v, f"adamw_{k}")
        grads[k] = g
        deltas.append(d)
        new_m.append(mn)
        new_v.append(vn)
    return (loss, grad_x, *grads, *deltas, *new_m, *new_v)
```

```python
import functools
import math

import jax
import jax.numpy as jnp
from jax import lax
from jax.experimental import pallas as pl
from jax.experimental.pallas import tpu as pltpu

F32 = jnp.float32
BF16 = jnp.bfloat16
NDEV = 8
HEAD = 64
BLK = 128
LANES = 128
EPS = 1e-6
ROT = HEAD // 4
ROPE_THETA = 500000.0
SCALE = HEAD ** -0.5
LOG2E = math.log2(math.e)
NEG = -1e30
VMEM_LIMIT = 56 * 1024 * 1024
MESH = pl.DeviceIdType.MESH
HIGH = lax.Precision.HIGHEST

ADAM_LR = 0.001
ADAM_B1 = 0.9
ADAM_B2 = 0.999
ADAM_EPS = 1e-08
ADAM_WD = 0.01
ADAM_STEP = 10


def _params(*sem):
    return pltpu.CompilerParams(dimension_semantics=sem, vmem_limit_bytes=VMEM_LIMIT)


def _pick(n, cap, mult):
    if n <= cap:
        return n
    best = None
    for t in range(mult, cap + 1, mult):
        if n % t == 0:
            best = t
    assert best is not None, (n, cap, mult)
    return best


def _dot(a, b, dims, precision=None):
    return lax.dot_general(a, b, (dims, ((), ())), preferred_element_type=F32, precision=precision)


NN = ((1,), (0,))
NT = ((1,), (1,))
TN = ((0,), (0,))


def _all_gather(x, name):
    m, n = x.shape

    def body(x_ref, out_ref, send_sems, recv_sems, local_sem):
        ix, iy, ic = lax.axis_index("x"), lax.axis_index("y"), lax.axis_index("c")
        me, sibling = (ix, iy, ic), (ix, iy, 1 - ic)
        chips = [(1 - ix, iy), (ix, 1 - iy), (1 - ix, 1 - iy)]

        def slab(px, py, pc):
            return out_ref.at[4 * px + 2 * py + pc]

        def copy(k, block, to, src=None):
            return pltpu.make_async_remote_copy(
                src_ref=slab(*block) if src is None else src, dst_ref=slab(*block),
                send_sem=send_sems.at[k], recv_sem=recv_sems.at[k], device_id=to, device_id_type=MESH)

        mine = pltpu.make_async_copy(x_ref, slab(*me), local_sem)
        mine.start()
        first = [copy(0, me, sibling, src=x_ref)]
        first += [copy(1 + j, me, (*chip, ic), src=x_ref) for j, chip in enumerate(chips)]
        for cp in first:
            cp.start()
        passed = [copy(4 + j, (*chip, ic), sibling) for j, chip in enumerate(chips)]
        for j, chip in enumerate(chips):
            copy(1 + j, (*chip, ic), me).wait_recv()
            passed[j].start()
        copy(0, sibling, me).wait_recv()
        for j, chip in enumerate(chips):
            copy(4 + j, (*chip, 1 - ic), me).wait_recv()
        for cp in first + passed:
            cp.wait_send()
        mine.wait()

    return pl.pallas_call(
        body, name=name,
        out_shape=jax.ShapeDtypeStruct((NDEV, m, n), x.dtype),
        in_specs=[pl.BlockSpec(memory_space=pl.ANY)],
        out_specs=pl.BlockSpec(memory_space=pl.ANY),
        scratch_shapes=[pltpu.SemaphoreType.DMA((7,)), pltpu.SemaphoreType.DMA((7,)), pltpu.SemaphoreType.DMA(())],
    )(x)


COMM_SEMS = [pltpu.SemaphoreType.DMA((NDEV - 1,)), pltpu.SemaphoreType.DMA((NDEV - 1,)), pltpu.SemaphoreType.DMA(())]
HBM_SPEC = pl.BlockSpec(memory_space=pl.ANY)


def _direct_copies(src_ref, dst_ref, sems, scatter):
    send_sems, recv_sems, own_sem = sems
    ix, iy, ic = lax.axis_index("x"), lax.axis_index("y"), lax.axis_index("c")
    me = 4 * ix + 2 * iy + ic
    copies = [pltpu.make_async_copy(src_ref.at[me] if scatter else src_ref, dst_ref.at[me], own_sem)]
    for k in range(1, NDEV):
        px = 1 - ix if k & 4 else ix
        py = 1 - iy if k & 2 else iy
        pc = 1 - ic if k & 1 else ic
        copies.append(pltpu.make_async_remote_copy(
            src_ref=src_ref.at[4 * px + 2 * py + pc] if scatter else src_ref, dst_ref=dst_ref.at[me],
            send_sem=send_sems.at[k - 1], recv_sem=recv_sems.at[k - 1],
            device_id=(px, py, pc), device_id_type=MESH))
    return copies


def _exchange(p, name):
    def body(p_ref, r_ref, *sems):
        copies = _direct_copies(p_ref, r_ref, sems, True)
        for cp in copies:
            cp.start()
        for cp in copies:
            cp.wait()

    return pl.pallas_call(
        body, name=name,
        out_shape=jax.ShapeDtypeStruct(p.shape, p.dtype),
        in_specs=[HBM_SPEC], out_specs=HBM_SPEC, scratch_shapes=COMM_SEMS,
    )(p)


def _call_behind(body, name, grid, in_specs, out_specs, out_shape, args, payload=None, scatter=False):
    params = _params(*["arbitrary"] * len(grid))
    if payload is None:
        return pl.pallas_call(body, name=name, grid=grid, in_specs=in_specs, out_specs=out_specs, out_shape=out_shape,
                              compiler_params=params)(*args)
    n_in, n_out = len(in_specs), len(out_specs)

    def edge(first):
        ids = [pl.program_id(a) for a in range(len(grid))]
        return functools.reduce(lambda u, v: u & v, [i == (0 if first else d - 1) for i, d in zip(ids, grid)])

    def wrapped(*refs):
        x_ref, r_ref, sems = refs[n_in], refs[n_in + 1 + n_out], refs[n_in + n_out + 2:]

        @pl.when(edge(True))
        def _():
            for cp in _direct_copies(x_ref, r_ref, sems, scatter):
                cp.start()
        body(*refs[:n_in], *refs[n_in + 1:n_in + 1 + n_out])

        @pl.when(edge(False))
        def _():
            for cp in _direct_copies(x_ref, r_ref, sems, scatter):
                cp.wait()

    arrived = jax.ShapeDtypeStruct(payload.shape if scatter else (NDEV,) + payload.shape, payload.dtype)
    return pl.pallas_call(
        wrapped, name=name, grid=grid, in_specs=list(in_specs) + [HBM_SPEC], out_specs=list(out_specs) + [HBM_SPEC],
        out_shape=list(out_shape) + [arrived], scratch_shapes=COMM_SEMS, compiler_params=params,
    )(*args, payload)


def _sum_leading(r, name):
    k, m, n = r.shape
    mult = 8 * (4 // r.dtype.itemsize)
    tm = _pick(m, max(mult, (4 * 1024 * 1024) // (k * n * r.dtype.itemsize) // mult * mult), mult)

    def body(r_ref, o_ref):
        acc = r_ref[0].astype(F32)
        for s in range(1, k):
            acc = acc + r_ref[s].astype(F32)
        o_ref[...] = acc

    return pl.pallas_call(
        body, name=name, grid=(m // tm,),
        in_specs=[pl.BlockSpec((k, tm, n), lambda i: (0, i, 0))],
        out_specs=pl.BlockSpec((tm, n), lambda i: (i, 0)),
        out_shape=jax.ShapeDtypeStruct((m, n), F32),
        compiler_params=_params("parallel"),
    )(r)


def _mm_nt(a, bt, out_dtype, name):
    M, K = a.shape
    N = bt.shape[0]
    tm, tn = _pick(M, 512, 8), _pick(N, 1536, LANES)

    def body(a_ref, b_ref, o_ref):
        o_ref[...] = _dot(a_ref[...], b_ref[...], NT).astype(out_dtype)

    return pl.pallas_call(
        body, name=name, grid=(N // tn, M // tm),
        in_specs=[pl.BlockSpec((tm, K), lambda j, i: (i, 0)), pl.BlockSpec((tn, K), lambda j, i: (j, 0))],
        out_specs=pl.BlockSpec((tm, tn), lambda j, i: (i, j)),
        out_shape=jax.ShapeDtypeStruct((M, N), out_dtype),
        compiler_params=_params("parallel", "parallel"),
    )(a, bt)


def _mm_tn(a, b, name):
    M, N1 = a.shape
    N2 = b.shape[1]
    t1, tk = _pick(N1, 1536, LANES), _pick(M, 512, 8)
    nk = M // tk

    def body(a_ref, b_ref, o_ref, acc_ref):
        k = pl.program_id(1)

        @pl.when(k == 0)
        def _():
            acc_ref[...] = jnp.zeros_like(acc_ref)
        acc_ref[...] += _dot(a_ref[...], b_ref[...], TN)

        @pl.when(k == nk - 1)
        def _():
            o_ref[...] = acc_ref[...].astype(BF16)

    return pl.pallas_call(
        body, name=name, grid=(N1 // t1, nk),
        in_specs=[pl.BlockSpec((tk, t1), lambda i, k: (k, i)), pl.BlockSpec((tk, N2), lambda i, k: (k, 0))],
        out_specs=pl.BlockSpec((t1, N2), lambda i, k: (i, 0)),
        out_shape=jax.ShapeDtypeStruct((N1, N2), BF16),
        scratch_shapes=[pltpu.VMEM((t1, N2), F32)],
        compiler_params=_params("parallel", "arbitrary"),
    )(a, b)


def _norm_mod_rows(xv, gain, sc, sh):
    r = lax.rsqrt(jnp.mean(xv * xv, axis=-1, keepdims=True) + EPS)
    return ((xv * r) * gain * (1.0 + sc) + sh).astype(BF16)


def _mm_res(a, w, x, gate, S, name, norm=None):
    T, K = a.shape
    D = w.shape[1]
    tm = _pick(S, 512, 8)
    nb = S // tm

    def body(a_ref, w_ref, x_ref, g_ref, *rest):
        y = _dot(a_ref[...], w_ref[...], NN)
        xn = x_ref[...] + g_ref[0] * y
        if norm is None:
            y_ref, o_ref = rest
        else:
            gain_ref, sc_ref, sh_ref, y_ref, o_ref, h_ref = rest
            h_ref[...] = _norm_mod_rows(xn, gain_ref[...], sc_ref[0], sh_ref[0])
        y_ref[...] = y.astype(BF16)
        o_ref[...] = xn

    spec_t = pl.BlockSpec((tm, D), lambda i: (i, 0))
    spec_b = pl.BlockSpec((1, 1, D), lambda i: (i // nb, 0, 0))
    in_specs = [pl.BlockSpec((tm, K), lambda i: (i, 0)), pl.BlockSpec((K, D), lambda i: (0, 0)), spec_t, spec_b]
    out_specs = [spec_t, spec_t]
    out_shape = [jax.ShapeDtypeStruct((T, D), BF16), jax.ShapeDtypeStruct((T, D), F32)]
    args = [a, w, x, gate]
    if norm is not None:
        in_specs += [pl.BlockSpec((1, D), lambda i: (0, 0)), spec_b, spec_b]
        out_specs.append(spec_t)
        out_shape.append(jax.ShapeDtypeStruct((T, D), BF16))
        args += list(norm)
    return pl.pallas_call(
        body, name=name, grid=(T // tm,), in_specs=in_specs, out_specs=out_specs, out_shape=out_shape,
        compiler_params=_params("parallel"),
    )(*args)


def _swiglu_fwd(h, wgt, wut, name):
    T, D = h.shape
    F = wgt.shape[0]
    tm, tn = _pick(T, 512, 8), _pick(F, 1536, LANES)

    def body(h_ref, g_ref, u_ref, go_ref, uo_ref, a_ref):
        hh = h_ref[...]
        g = _dot(hh, g_ref[...], NT)
        u = _dot(hh, u_ref[...], NT)
        go_ref[...] = g.astype(BF16)
        uo_ref[...] = u.astype(BF16)
        a_ref[...] = (g * jax.nn.sigmoid(g) * u).astype(BF16)

    spec_w = pl.BlockSpec((tn, D), lambda j, i: (j, 0))
    spec_o = pl.BlockSpec((tm, tn), lambda j, i: (i, j))
    out = jax.ShapeDtypeStruct((T, F), BF16)
    return pl.pallas_call(
        body, name=name, grid=(F // tn, T // tm),
        in_specs=[pl.BlockSpec((tm, D), lambda j, i: (i, 0)), spec_w, spec_w],
        out_specs=[spec_o, spec_o, spec_o],
        out_shape=[out, out, out],
        compiler_params=_params("parallel", "parallel"),
    )(h, wgt, wut)


def _swiglu_bwd(dy, wd, gate, up, name):
    T, D = dy.shape
    F = wd.shape[0]
    tm, tn = _pick(T, 512, 8), _pick(F, 1536, LANES)

    halves = [slice(0, tn // 2), slice(tn // 2, tn)] if tn % (2 * LANES) == 0 else [slice(0, tn)]

    def body(dy_ref, w_ref, g_ref, u_ref, dg_ref, du_ref):
        das = [_dot(dy_ref[...], w_ref[sl, :], NT) for sl in halves]
        for sl, da in zip(halves, das):
            g = g_ref[:, sl].astype(F32)
            sg = jax.nn.sigmoid(g)
            t = da * sg
            du_ref[:, sl] = (t * g).astype(BF16)
            dg_ref[:, sl] = (t * u_ref[:, sl].astype(F32) * (1.0 + g * (1.0 - sg))).astype(BF16)

    spec_o = pl.BlockSpec((tm, tn), lambda j, i: (i, j))
    return pl.pallas_call(
        body, name=name, grid=(F // tn, T // tm),
        in_specs=[pl.BlockSpec((tm, D), lambda j, i: (i, 0)), pl.BlockSpec((tn, D), lambda j, i: (j, 0)), spec_o, spec_o],
        out_specs=[spec_o, spec_o],
        out_shape=[jax.ShapeDtypeStruct((T, F), BF16), jax.ShapeDtypeStruct((T, F), BF16)],
        compiler_params=_params("parallel", "parallel"),
    )(dy, wd, gate, up)


def _norm_mod(x, gain, sc, sh, S, name):
    T, D = x.shape
    tm = _pick(S, 512, 8)
    nb = S // tm

    def body(x_ref, g_ref, sc_ref, sh_ref, o_ref):
        o_ref[...] = _norm_mod_rows(x_ref[...], g_ref[...], sc_ref[0], sh_ref[0])

    spec_b = pl.BlockSpec((1, 1, D), lambda i: (i // nb, 0, 0))
    return pl.pallas_call(
        body, name=name, grid=(T // tm,),
        in_specs=[pl.BlockSpec((tm, D), lambda i: (i, 0)), pl.BlockSpec((1, D), lambda i: (0, 0)), spec_b, spec_b],
        out_specs=pl.BlockSpec((tm, D), lambda i: (i, 0)),
        out_shape=jax.ShapeDtypeStruct((T, D), BF16),
        compiler_params=_params("parallel"),
    )(x, gain, sc, sh)


def _norm_mod_bwd(x, pairs, dres, gain, sc, S, name, below=None, exchange=None):
    T, D = x.shape
    B = T // S
    tm = _pick(S, 512, 8)
    nb = S // tm
    n_mm = 2 * len(pairs)

    def body(*refs):
        mm, (x_ref, dr_ref, g_ref, sc_ref), rest = refs[:n_mm], refs[n_mm:n_mm + 4], refs[n_mm + 4:]
        if below is None:
            o_ref, dsh_ref, dsc_ref, dg_ref = rest
            sums = [dsh_ref, dsc_ref, dg_ref]
        else:
            y_ref, gt_ref, o_ref, dsh_ref, dsc_ref, dg_ref, dy_ref, dgt_ref = rest
            sums = [dsh_ref, dsc_ref, dg_ref, dgt_ref]

        @pl.when(pl.program_id(1) == 0)
        def _():
            for ref in sums:
                ref[...] = jnp.zeros_like(ref)
        dhv = _dot(mm[0][...], mm[1][...], NN)
        for p in range(2, n_mm, 2):
            dhv = dhv + _dot(mm[p][...], mm[p + 1][...], NN)
        xv, g = x_ref[...], g_ref[...]
        r = lax.rsqrt(jnp.mean(xv * xv, axis=-1, keepdims=True) + EPS)
        xhat = xv * r
        dsh_ref[0] += jnp.sum(dhv, axis=0, keepdims=True)
        dsc_ref[0] += jnp.sum(dhv * (xhat * g), axis=0, keepdims=True)
        dn = dhv * (1.0 + sc_ref[0])
        dg_ref[0] += jnp.sum(dn * xhat, axis=0, keepdims=True)
        dxh = dn * g
        out = dr_ref[...] + r * (dxh - xhat * jnp.mean(dxh * xhat, axis=-1, keepdims=True))
        o_ref[...] = out
        if below is not None:
            dy_ref[...] = (out * gt_ref[0]).astype(BF16)
            dgt_ref[0] += jnp.sum(out * y_ref[...].astype(F32), axis=0, keepdims=True)

    spec_t = pl.BlockSpec((tm, D), lambda b, i: (b * nb + i, 0))
    spec_b = pl.BlockSpec((1, 1, D), lambda b, i: (b, 0, 0))
    red = jax.ShapeDtypeStruct((B, 1, D), F32)
    in_specs, args = [], []
    for a, w in pairs:
        K = a.shape[1]
        in_specs += [pl.BlockSpec((tm, K), lambda b, i: (b * nb + i, 0)),
                     pl.BlockSpec((K, D), lambda b, i: (0, 0), pipeline_mode=pl.Buffered(1))]
        args += [a, w]
    in_specs += [spec_t, spec_t, pl.BlockSpec((1, D), lambda b, i: (0, 0)), spec_b]
    args += [x, dres, gain, sc]
    out_specs = [spec_t, spec_b, spec_b, spec_b]
    out_shape = [jax.ShapeDtypeStruct((T, D), F32), red, red, red]
    if below is not None:
        in_specs += [spec_t, spec_b]
        out_specs += [spec_t, spec_b]
        out_shape += [jax.ShapeDtypeStruct((T, D), BF16), red]
        args += list(below)
    return _call_behind(body, name, (B, nb), in_specs, out_specs, out_shape, args, exchange, True)


def _gate_bwd(dx, y, gate, S, name):
    T, D = dx.shape
    B = T // S
    tm = _pick(S, 512, 8)
    nb = S // tm

    def body(dx_ref, y_ref, g_ref, dy_ref, dg_ref):
        @pl.when(pl.program_id(1) == 0)
        def _():
            dg_ref[...] = jnp.zeros_like(dg_ref)
        d = dx_ref[...]
        dy_ref[...] = (d * g_ref[0]).astype(BF16)
        dg_ref[0] += jnp.sum(d * y_ref[...].astype(F32), axis=0, keepdims=True)

    spec_t = pl.BlockSpec((tm, D), lambda b, i: (b * nb + i, 0))
    spec_b = pl.BlockSpec((1, 1, D), lambda b, i: (b, 0, 0))
    return pl.pallas_call(
        body, name=name, grid=(B, nb),
        in_specs=[spec_t, spec_t, spec_b],
        out_specs=[spec_t, spec_b],
        out_shape=[jax.ShapeDtypeStruct((T, D), BF16), jax.ShapeDtypeStruct((B, 1, D), F32)],
        compiler_params=_params("parallel", "arbitrary"),
    )(dx, y, gate)


def _loss_head(y, target, name):
    T, D = y.shape
    tm = _pick(T, 512, 8)

    def body(y_ref, t_ref, dy_ref, l_ref):
        @pl.when(pl.program_id(0) == 0)
        def _():
            l_ref[...] = jnp.zeros_like(l_ref)
        e = y_ref[...] - t_ref[...]
        dy_ref[...] = e * (1.0 / D)
        l_ref[...] += 0.5 * jnp.sum(jnp.mean(e * e, axis=-1, keepdims=True), axis=0, keepdims=True)

    spec = pl.BlockSpec((tm, D), lambda i: (i, 0))
    return pl.pallas_call(
        body, name=name, grid=(T // tm,),
        in_specs=[spec, spec],
        out_specs=[spec, pl.BlockSpec((8, LANES), lambda i: (0, 0))],
        out_shape=[jax.ShapeDtypeStruct((T, D), F32), jax.ShapeDtypeStruct((8, LANES), F32)],
        compiler_params=_params("arbitrary"),
    )(y, target)


def _ada_fwd(c_all, ada_w, bias, name):
    NB, D = c_all.shape
    L, _, W = ada_w.shape

    def body(c_ref, w_ref, b_ref, o_ref):
        cv = c_ref[...]
        cond = cv * jax.nn.sigmoid(cv)
        o_ref[0] = _dot(cond, w_ref[0], NN, HIGH) + b_ref[0]

    return pl.pallas_call(
        body, name=name, grid=(L,),
        in_specs=[pl.BlockSpec((NB, D), lambda l: (0, 0)), pl.BlockSpec((1, D, W), lambda l: (l, 0, 0)),
                  pl.BlockSpec((1, 1, W), lambda l: (l, 0, 0))],
        out_specs=pl.BlockSpec((1, NB, W), lambda l: (l, 0, 0)),
        out_shape=jax.ShapeDtypeStruct((L, NB, W), F32),
        compiler_params=_params("parallel"),
    )(c_all, ada_w, bias)


def _ada_bwd(c_all, dmod, name):
    NB, D = c_all.shape
    L, _, W = dmod.shape

    def body(c_ref, d_ref, o_ref):
        cv = c_ref[...]
        cond = cv * jax.nn.sigmoid(cv)
        o_ref[0] = _dot(cond, d_ref[0], TN, HIGH)

    return pl.pallas_call(
        body, name=name, grid=(L,),
        in_specs=[pl.BlockSpec((NB, D), lambda l: (0, 0)), pl.BlockSpec((1, NB, W), lambda l: (l, 0, 0))],
        out_specs=pl.BlockSpec((1, D, W), lambda l: (l, 0, 0)),
        out_shape=jax.ShapeDtypeStruct((L, D, W), F32),
        compiler_params=_params("parallel"),
    )(c_all, dmod)


def _lo_mask(shape):
    return lax.broadcasted_iota(jnp.int32, shape, len(shape) - 1) < HEAD


def _head_sum_matrix():
    r = lax.broadcasted_iota(jnp.int32, (LANES, LANES), 0) // HEAD
    c = lax.broadcasted_iota(jnp.int32, (LANES, LANES), 1) // HEAD
    return (r == c).astype(BF16)


def _head_sum(x, P):
    hi = x.astype(BF16)
    lo = (x - hi.astype(F32)).astype(BF16)
    return _dot(hi, P, NN) + _dot(lo, P, NN)


def _rope(y, cs, s1, s2):
    return y * cs + pltpu.roll(y, LANES - ROT // 2, 1) * s1 + pltpu.roll(y, ROT // 2, 1) * s2


def _rope_bwd(d, cs, s1, s2):
    return d * cs + pltpu.roll(d * s1, ROT // 2, 1) + pltpu.roll(d * s2, LANES - ROT // 2, 1)


def _qk_prep(qkv, cs, s1, s2, qg, kg, name):
    T, W = qkv.shape
    NQ = W - 2 * LANES
    tm = _pick(T, 512, 8)

    def body(x_ref, cs_ref, s1_ref, s2_ref, qg_ref, kg_ref, q_ref, k_ref, v_ref):
        P = _head_sum_matrix()
        cs_, s1_, s2_ = cs_ref[...], s1_ref[...], s2_ref[...]
        lo = _lo_mask((tm, LANES))

        def norm_rope(xv, g):
            ms = _head_sum(xv * xv, P) * (1.0 / HEAD)
            return _rope(xv * lax.rsqrt(ms + EPS) * g, cs_, s1_, s2_)

        for j in range(NQ // LANES):
            q_ref[:, j * LANES:(j + 1) * LANES] = norm_rope(x_ref[:, j * LANES:(j + 1) * LANES], qg_ref[...]).astype(BF16)
        kr = norm_rope(x_ref[:, NQ:NQ + LANES], kg_ref[...])
        ks = pltpu.roll(kr, HEAD, 1)
        k_ref[:, :LANES] = jnp.where(lo, kr, ks).astype(BF16)
        k_ref[:, LANES:] = jnp.where(lo, ks, kr).astype(BF16)
        vr = x_ref[:, NQ + LANES:]
        vs = pltpu.roll(vr, HEAD, 1)
        v_ref[:, :LANES] = jnp.where(lo, vr, vs).astype(BF16)
        v_ref[:, LANES:] = jnp.where(lo, vs, vr).astype(BF16)

    spec_t = pl.BlockSpec((tm, LANES), lambda i: (i, 0))
    spec_g = pl.BlockSpec((1, LANES), lambda i: (0, 0))
    return pl.pallas_call(
        body, name=name, grid=(T // tm,),
        in_specs=[pl.BlockSpec((tm, W), lambda i: (i, 0)), spec_t, spec_t, spec_t, spec_g, spec_g],
        out_specs=[pl.BlockSpec((tm, NQ), lambda i: (i, 0)), pl.BlockSpec((tm, 2 * LANES), lambda i: (i, 0)),
                   pl.BlockSpec((tm, 2 * LANES), lambda i: (i, 0))],
        out_shape=[jax.ShapeDtypeStruct((T, NQ), BF16), jax.ShapeDtypeStruct((T, 2 * LANES), BF16),
                   jax.ShapeDtypeStruct((T, 2 * LANES), BF16)],
        compiler_params=_params("parallel"),
    )(qkv, cs, s1, s2, qg, kg)


def _stack_heads(x2):
    lo = _lo_mask(x2.shape)
    z = jnp.zeros_like(x2)
    return jnp.concatenate([jnp.where(lo, x2, z), jnp.where(lo, z, x2)], axis=0)


def _unstack_heads(xs):
    r = xs.shape[0] // 2
    return jnp.where(_lo_mask((r, LANES)), xs[:r], xs[r:])


def _swa_valid(i):
    qo = lax.broadcasted_iota(jnp.int32, (2 * BLK, 2 * BLK), 0) % BLK
    kc_ = lax.broadcasted_iota(jnp.int32, (2 * BLK, 2 * BLK), 1)
    rel = qo + BLK - kc_
    return (rel >= 0) & (rel < BLK) & ((kc_ >= BLK) | (i > 0))


def _swa_scores(q2, kk, sink2, valid):
    qs = _stack_heads(q2) * SCALE
    s = _dot(qs, kk, NT)
    sk = jnp.concatenate([jnp.broadcast_to(sink2[:, 0:1], (BLK, 1)), jnp.broadcast_to(sink2[:, HEAD:HEAD + 1], (BLK, 1))], axis=0)
    return qs, jnp.where(valid, s, NEG), sk


def _swa_fwd(q, kd, vd, sink2, B, name, gather=None):
    T, NQ = q.shape
    NP = NQ // LANES
    nq = T // B // BLK
    NG = kd.shape[1] // LANES
    grp = NP // NG

    def body(q_ref, kp_ref, kc_ref, vp_ref, vc_ref, s_ref, o_ref, l_ref):
        valid = _swa_valid(pl.program_id(2))
        kk = jnp.concatenate([kp_ref[...], kc_ref[...]], axis=0)
        vs = _stack_heads(jnp.concatenate([vp_ref[...], vc_ref[...]], axis=0))
        sls = [slice(jj * LANES, (jj + 1) * LANES) for jj in range(grp)]
        sc = [_swa_scores(q_ref[:, sl], kk, s_ref[jj], valid) for jj, sl in enumerate(sls)]
        ms = [jnp.maximum(jnp.max(s, axis=1, keepdims=True), sk) for _, s, sk in sc]
        ps = [jnp.exp(s - m) for (_, s, _), m in zip(sc, ms)]
        ls = [jnp.sum(p, axis=1, keepdims=True) + jnp.exp(sk - m) for p, (_, _, sk), m in zip(ps, sc, ms)]
        ps = [(p * (1.0 / l)).astype(BF16) for p, l in zip(ps, ls)]
        os_ = [_dot(jnp.concatenate([p[:BLK], p[BLK:]], axis=1), vs, NN) for p in ps]
        for sl, o, m, l in zip(sls, os_, ms, ls):
            o_ref[:, sl] = o.astype(BF16)
            l_ref[:, sl] = _unstack_heads(jnp.broadcast_to(m + jnp.log(l), (2 * BLK, LANES)))

    spec_q = pl.BlockSpec((BLK, grp * LANES), lambda b, g, i: (b * nq + i, g))
    spec_p = pl.BlockSpec((BLK, LANES), lambda b, g, i: (b * nq + jnp.maximum(i - 1, 0), g))
    spec_c = pl.BlockSpec((BLK, LANES), lambda b, g, i: (b * nq + i, g))
    in_specs = [spec_q, spec_p, spec_c, spec_p, spec_c, pl.BlockSpec((grp, 1, LANES), lambda b, g, i: (g, 0, 0))]
    out_shape = [jax.ShapeDtypeStruct((T, NQ), BF16), jax.ShapeDtypeStruct((T, NQ), F32)]
    return _call_behind(body, name, (B, NG, nq), in_specs, [spec_q, spec_q], out_shape, [q, kd, kd, vd, vd, sink2], gather, False)


def _swa_bwd(q, kd, vd, sink2, do, lse, B, name, exchange=None):
    T, NQ = q.shape
    NP = NQ // LANES
    nq = T // B // BLK
    NG = kd.shape[1] // LANES
    grp = NP // NG

    def body(q_ref, kp_ref, kc_ref, vp_ref, vc_ref, s_ref, do_ref, l_ref,
             dq_ref, dkc_ref, dkp_ref, dvc_ref, dvp_ref, ds_ref):
        b, i = pl.program_id(1), pl.program_id(2)

        @pl.when((b == 0) & (i == 0))
        def _():
            ds_ref[...] = jnp.zeros_like(ds_ref)
        valid = _swa_valid(i)
        kk = jnp.concatenate([kp_ref[...], kc_ref[...]], axis=0)
        vv = jnp.concatenate([vp_ref[...], vc_ref[...]], axis=0)
        sls = [slice(jj * LANES, (jj + 1) * LANES) for jj in range(grp)]
        sc = [_swa_scores(q_ref[:, sl], kk, s_ref[jj], valid) for jj, sl in enumerate(sls)]
        dos = [_stack_heads(do_ref[:, sl]) for sl in sls]
        dps = [_dot(d, vv, NT) for d in dos]
        lses = [jnp.concatenate([l_ref[:, sl][:, 0:1], l_ref[:, sl][:, HEAD:HEAD + 1]], axis=0) for sl in sls]
        ps = [jnp.exp(s - lse) for (_, s, _), lse in zip(sc, lses)]
        deltas = [jnp.sum(p * dp, axis=1, keepdims=True) for p, dp in zip(ps, dps)]
        dscs = [(p * (dp - delta)).astype(BF16) for p, dp, delta in zip(ps, dps, deltas)]
        dqs = [_dot(dsc, kk, NN) for dsc in dscs]
        dk = jnp.zeros((2 * BLK, LANES), F32)
        dv = jnp.zeros((2 * BLK, LANES), F32)
        for jj, sl in enumerate(sls):
            dsk = -jnp.exp(sc[jj][2] - lses[jj]) * deltas[jj]
            dsk_lo = jnp.sum(dsk[:BLK], axis=0, keepdims=True)
            dsk_hi = jnp.sum(dsk[BLK:], axis=0, keepdims=True)
            ds_ref[jj] += jnp.where(_lo_mask((1, LANES)), dsk_lo, dsk_hi)
            dq_ref[:, sl] = _unstack_heads(dqs[jj]) * SCALE
            dk = dk + _dot(dscs[jj], sc[jj][0], TN)
            dv = dv + _dot(ps[jj].astype(BF16), dos[jj], TN)
        dkp_ref[...] = dk[:BLK]
        dkc_ref[...] = dk[BLK:]
        dvp_ref[...] = dv[:BLK]
        dvc_ref[...] = dv[BLK:]

    spec_q = pl.BlockSpec((BLK, grp * LANES), lambda g, b, i: (b * nq + i, g))
    spec_p = pl.BlockSpec((BLK, LANES), lambda g, b, i: (b * nq + jnp.maximum(i - 1, 0), g))
    spec_c = pl.BlockSpec((BLK, LANES), lambda g, b, i: (b * nq + i, g))
    spec_s = pl.BlockSpec((grp, 1, LANES), lambda g, b, i: (g, 0, 0))
    kv = jax.ShapeDtypeStruct((T, NG * LANES), F32)
    in_specs = [spec_q, spec_p, spec_c, spec_p, spec_c, spec_s, spec_q, spec_q]
    out_specs = [spec_q, spec_c, spec_c, spec_c, spec_c, spec_s]
    out_shape = [jax.ShapeDtypeStruct((T, NQ), F32), kv, kv, kv, kv, jax.ShapeDtypeStruct((NP, 1, LANES), F32)]
    return _call_behind(body, name, (NG, B, nq), in_specs, out_specs, out_shape, [q, kd, kd, vd, vd, sink2, do, lse],
                        exchange, True)


def _qk_prep_bwd(qkv, cs, s1, s2, qg, kg, dq, dkc, dkp, dvc, dvp, B, name):
    T, W = qkv.shape
    NQ = W - 2 * LANES
    NP = NQ // LANES
    nq = T // B // BLK

    def body(x_ref, cs_ref, s1_ref, s2_ref, qg_ref, kg_ref, dq_ref, dkc_ref, dkp_ref, dvc_ref, dvp_ref,
             o_ref, dqg_ref, dkg_ref):
        b, i = pl.program_id(0), pl.program_id(1)

        @pl.when((b == 0) & (i == 0))
        def _():
            dqg_ref[...] = jnp.zeros_like(dqg_ref)
            dkg_ref[...] = jnp.zeros_like(dkg_ref)
        P = _head_sum_matrix()
        cs_, s1_, s2_ = cs_ref[...], s1_ref[...], s2_ref[...]
        lo = _lo_mask((BLK, LANES))
        has_next = (i + 1 < nq).astype(F32)

        def norm_rope_bwd(xv, g, d):
            du = _rope_bwd(d, cs_, s1_, s2_)
            r = lax.rsqrt(_head_sum(xv * xv, P) * (1.0 / HEAD) + EPS)
            xhat = xv * r
            dgain = jnp.sum(du * xhat, axis=0, keepdims=True)
            uu = du * g
            dx = r * (uu - xhat * (_head_sum(uu * xhat, P) * (1.0 / HEAD)))
            return dx, dgain + pltpu.roll(dgain, HEAD, 1)

        dqg = jnp.zeros((1, LANES), F32)
        for j in range(NP):
            sl = slice(j * LANES, (j + 1) * LANES)
            dx, dg = norm_rope_bwd(x_ref[:, sl], qg_ref[...], dq_ref[:, sl])
            o_ref[:, sl] = dx.astype(BF16)
            dqg = dqg + dg
        dqg_ref[...] += dqg

        def fold(c_ref, p_ref, g):
            sl = slice(g * LANES, (g + 1) * LANES)
            t = c_ref[:, sl] + has_next * p_ref[:, sl]
            return t + pltpu.roll(t, HEAD, 1)

        dk = jnp.where(lo, fold(dkc_ref, dkp_ref, 0), fold(dkc_ref, dkp_ref, 1))
        dx, dg = norm_rope_bwd(x_ref[:, NQ:NQ + LANES], kg_ref[...], dk)
        o_ref[:, NQ:NQ + LANES] = dx.astype(BF16)
        dkg_ref[...] += dg
        dv = jnp.where(lo, fold(dvc_ref, dvp_ref, 0), fold(dvc_ref, dvp_ref, 1))
        o_ref[:, NQ + LANES:] = dv.astype(BF16)

    spec_t = pl.BlockSpec((BLK, LANES), lambda b, i: (b * nq + i, 0))
    spec_g = pl.BlockSpec((1, LANES), lambda b, i: (0, 0))
    spec_c = pl.BlockSpec((BLK, 2 * LANES), lambda b, i: (b * nq + i, 0))
    spec_n = pl.BlockSpec((BLK, 2 * LANES), lambda b, i: (b * nq + jnp.minimum(i + 1, nq - 1), 0))
    row = jax.ShapeDtypeStruct((1, LANES), F32)
    return pl.pallas_call(
        body, name=name, grid=(B, nq),
        in_specs=[pl.BlockSpec((BLK, W), lambda b, i: (b * nq + i, 0)), spec_t, spec_t, spec_t, spec_g, spec_g,
                  pl.BlockSpec((BLK, NQ), lambda b, i: (b * nq + i, 0)), spec_c, spec_n, spec_c, spec_n],
        out_specs=[pl.BlockSpec((BLK, W), lambda b, i: (b * nq + i, 0)), spec_g, spec_g],
        out_shape=[jax.ShapeDtypeStruct((T, W), BF16), row, row],
        compiler_params=_params("arbitrary", "arbitrary"),
    )(qkv, cs, s1, s2, qg, kg, dq, dkc, dkp, dvc, dvp)


SB_TILE = 256
SB_UNROLL = 4
SB_UNROLL_BWD = 2


def _split_heads(x2, scale=None):
    lo = _lo_mask(x2.shape)
    z = jnp.zeros_like(x2)
    if scale is not None:
        x2 = x2 * scale
    return jnp.where(lo, x2, z), jnp.where(lo, z, x2)


def _sb_terms(qh, kj, diagonal):
    z = _dot(qh, kj, NT)
    e = jnp.exp2(jnp.abs(z) * (-LOG2E))
    lb = jnp.minimum(z, 0.0) - jnp.log(1.0 + e)
    L = lb - z
    if not diagonal:
        return lb, L, None, z, e
    strict = lax.broadcasted_iota(jnp.int32, z.shape, 1) < lax.broadcasted_iota(jnp.int32, z.shape, 0)
    return lb, jnp.where(strict, L, 0.0), strict, z, e


def _tri(n, cmp):
    r = lax.broadcasted_iota(jnp.int32, (n, n), 0)
    c = lax.broadcasted_iota(jnp.int32, (n, n), 1)
    return cmp(r, c).astype(BF16)


def _by_value(r, fns, carry):
    if len(fns) == 1:
        return fns[0](carry)
    half = len(fns) // 2
    return lax.cond(r < half, lambda cr: _by_value(r, fns[:half], cr), lambda cr: _by_value(r - half, fns[half:], cr), carry)


def _sb_fwd(qkv, B, name, gather=None):
    T, W = qkv.shape
    NQ = W // 3
    NP = NQ // LANES
    S = T // B
    tq = min(SB_TILE, S)
    nq = S // tq
    grid = (B, NP, nq)

    def body(q_ref, k_ref, v_ref, o_ref, t_ref):
        i = pl.program_id(2)
        qh = _split_heads(q_ref[...], SCALE)
        U = _tri(tq, lambda r, c: r > c)

        def sweep(tiles, cs, acc):
            chains = [(t, h) for t in range(len(tiles)) for h in range(2)]
            rows = [pl.ds(pl.multiple_of(j * tq, tq), tq) for j, _ in tiles]
            ks = [k_ref[r, :] for r in rows]
            vs = [_split_heads(v_ref[r, :]) for r in rows]
            terms = {(t, h): _sb_terms(qh[h], ks[t], tiles[t][1]) for t, h in chains}
            carry = {}
            for h in range(2):
                c = cs[h]
                for t in range(len(tiles)):
                    carry[t, h] = c
                    c = c + jnp.sum(terms[t, h][1], axis=1, keepdims=True)
                cs = cs[:h] + (c,) + cs[h + 1:]
            cum = {ch: _dot(terms[ch][1].astype(BF16), U, NN) for ch in chains}
            for ch in chains:
                a = jnp.exp(terms[ch][0] + (cum[ch] + carry[ch]))
                if tiles[ch[0]][1]:
                    a = jnp.where(terms[ch][2], a, 0.0)
                acc = acc + _dot(a.astype(BF16), vs[ch[0]][ch[1]], NN)
            return cs, acc

        zero = jnp.zeros((tq, 1), F32)
        rem = i % SB_UNROLL
        heads = [lambda cr, k=k: sweep([(i, True)] + [(i - 1 - t, False) for t in range(k)], *cr) for k in range(SB_UNROLL)]
        carry = _by_value(rem, heads, ((zero, zero), jnp.zeros((tq, LANES), F32)))
        step = lambda n, cr: sweep([(i - 1 - rem - SB_UNROLL * n - t, False) for t in range(SB_UNROLL)], *cr)
        cs, acc = lax.fori_loop(0, i // SB_UNROLL, step, carry)
        o_ref[...] = acc.astype(BF16)
        t_ref[...] = jnp.where(_lo_mask((tq, LANES)), cs[0], cs[1])

    spec_q = pl.BlockSpec((tq, LANES), lambda b, p, i: (b * nq + i, p))
    in_specs = [spec_q, pl.BlockSpec((S, LANES), lambda b, p, i: (b, NP + p)),
                pl.BlockSpec((S, LANES), lambda b, p, i: (b, 2 * NP + p))]
    out_shape = [jax.ShapeDtypeStruct((T, NQ), BF16), jax.ShapeDtypeStruct((T, NQ), F32)]
    return _call_behind(body, name, grid, in_specs, [spec_q, spec_q], out_shape, [qkv, qkv, qkv], gather, False)


def _sb_bwd(qkv, q_t, do, do_t, tot, B, name, exchange=None):
    T, W = qkv.shape
    NQ = W // 3
    NP = NQ // LANES
    S = T // B
    tq = min(SB_TILE, S)
    nq = S // tq
    grid = (B, NP, nq)

    def body(q_ref, k_ref, v_ref, do_ref, qt_ref, dot_ref, t_ref, dq_ref, dk_ref, dv_ref):
        i = pl.program_id(2)

        @pl.when(i == 0)
        def _():
            dk_ref[...] = jnp.zeros_like(dk_ref)
            dv_ref[...] = jnp.zeros_like(dv_ref)
        qh = _split_heads(q_ref[...], SCALE)
        doh = _split_heads(do_ref[...])
        top = lax.broadcasted_iota(jnp.int32, (LANES, tq), 0) < HEAD
        zt = jnp.zeros((LANES, tq), BF16)
        qt = qt_ref[...] * SCALE
        qth = (jnp.where(top, qt, zt), jnp.where(top, zt, qt))
        doth = (jnp.where(top, dot_ref[...], zt), jnp.where(top, zt, dot_ref[...]))
        tt = t_ref[...]
        tot = (tt[:, 0:1], tt[:, HEAD:HEAD + 1])
        Urev = _tri(tq, lambda r, c: r > c)
        Uexc = _tri(tq, lambda r, c: r < c)

        def sweep(tiles, carry):
            nt = len(tiles)
            chains = [(t, h) for t in range(nt) for h in range(2)]
            rows = [pl.ds(pl.multiple_of(j * tq, tq), tq) for j, _ in tiles]
            ks = [k_ref[r, :] for r in rows]
            vs = [v_ref[r, :] for r in rows]
            terms = {(t, h): _sb_terms(qh[h], ks[t], tiles[t][1]) for t, h in chains}
            cc = [carry[h][0] for h in range(2)]
            later = {}
            for t, h in chains:
                cc[h] = cc[h] + jnp.sum(terms[t, h][1], axis=1, keepdims=True)
                later[t, h] = tot[h] - cc[h]
            cum = {ch: _dot(terms[ch][1].astype(BF16), Urev, NN) for ch in chains}
            da = {(t, h): _dot(doh[h], vs[t], NT) for t, h in chains}
            a, g, before = {}, {}, {}
            cg = [carry[h][1] for h in range(2)]
            for ch in chains:
                a[ch] = jnp.exp(terms[ch][0] + (cum[ch] + later[ch]))
                if tiles[ch[0]][1]:
                    a[ch] = jnp.where(terms[ch][2], a[ch], 0.0)
                g[ch] = a[ch] * da[ch]
                before[ch] = cg[ch[1]]
                cg[ch[1]] = cg[ch[1]] + jnp.sum(g[ch], axis=1, keepdims=True)
            G = {ch: _dot(g[ch].astype(BF16), Uexc, NN) for ch in chains}
            dz = {}
            for ch in chains:
                d = g[ch] - jnp.exp(terms[ch][0]) * (g[ch] + (G[ch] + before[ch]))
                if tiles[ch[0]][1]:
                    d = jnp.where(terms[ch][2], d, 0.0)
                dz[ch] = d.astype(BF16)
            dq = [carry[h][2] for h in range(2)]
            for t, h in chains:
                dq[h] = dq[h] + _dot(dz[t, h], ks[t], NN)
            for t in range(nt):
                dk_ref[:, rows[t]] += _dot(qth[0], dz[t, 0], NN) + _dot(qth[1], dz[t, 1], NN)
                dv_ref[:, rows[t]] += _dot(doth[0], a[t, 0].astype(BF16), NN) + _dot(doth[1], a[t, 1].astype(BF16), NN)
            return tuple((cc[h], cg[h], dq[h]) for h in range(2))

        zero = jnp.zeros((tq, 1), F32)
        zq = jnp.zeros((tq, LANES), F32)
        step = lambda n, cr: sweep([(SB_UNROLL_BWD * n + t, False) for t in range(SB_UNROLL_BWD)], cr)
        carry = lax.fori_loop(0, i // SB_UNROLL_BWD, step, ((zero, zero, zq), (zero, zero, zq)))
        tails = [lambda cr, k=k: sweep([(i - k + t, False) for t in range(k)] + [(i, True)], cr) for k in range(SB_UNROLL_BWD)]
        carry = _by_value(i % SB_UNROLL_BWD, tails, carry)
        dq_ref[...] = jnp.where(_lo_mask((tq, LANES)), carry[0][2], carry[1][2]) * SCALE

    spec_q = pl.BlockSpec((tq, LANES), lambda b, p, i: (b * nq + i, p))
    spec_t = pl.BlockSpec((LANES, tq), lambda b, p, i: (p, b * nq + i))
    spec_s = pl.BlockSpec((LANES, S), lambda b, p, i: (b * NP + p, 0))
    key_side = jax.ShapeDtypeStruct((B * NQ, S), F32)
    in_specs = [spec_q, pl.BlockSpec((S, LANES), lambda b, p, i: (b, NP + p)),
                pl.BlockSpec((S, LANES), lambda b, p, i: (b, 2 * NP + p)), spec_q, spec_t, spec_t, spec_q]
    out_specs = [spec_q, spec_s, spec_s]
    out_shape = [jax.ShapeDtypeStruct((T, NQ), F32), key_side, key_side]
    args = [qkv, qkv, qkv, do, q_t, do_t, tot]
    return _call_behind(body, name, grid, in_specs, out_specs, out_shape, args, exchange, True)


def _adamw(w, g, m, v, name):
    shape = w.shape
    cols = shape[-1]
    rows = math.prod(shape[:-1])
    tr = _pick(rows, max(8, (1 << 19) // max(cols, LANES) // 8 * 8), 8)

    def body(w_ref, g_ref, m_ref, v_ref, d_ref, mo_ref, vo_ref):
        gv = g_ref[...]
        mn = ADAM_B1 * m_ref[...] + (1.0 - ADAM_B1) * gv
        vn = ADAM_B2 * v_ref[...] + (1.0 - ADAM_B2) * (gv * gv)
        m_hat = mn / (1.0 - ADAM_B1 ** ADAM_STEP)
        v_hat = vn / (1.0 - ADAM_B2 ** ADAM_STEP)
        d_ref[...] = -ADAM_LR * (m_hat / (jnp.sqrt(v_hat) + ADAM_EPS) + ADAM_WD * w_ref[...])
        mo_ref[...] = mn
        vo_ref[...] = vn

    spec = pl.BlockSpec((tr, cols), lambda i: (i, 0))
    out = jax.ShapeDtypeStruct((rows, cols), F32)
    d, mn, vn = pl.pallas_call(
        body, name=name, grid=(rows // tr,),
        in_specs=[spec] * 4, out_specs=[spec] * 3, out_shape=[out] * 3,
        compiler_params=_params("parallel"),
    )(w.reshape(rows, cols), g.reshape(rows, cols), m.reshape(rows, cols), v.reshape(rows, cols))
    return d.reshape(shape), mn.reshape(shape), vn.reshape(shape)


def _pad_rows(a, rows):
    return jnp.pad(a, ((0, rows - a.shape[0]), (0, 0)))


def kernel(x, c, positions, ada_w, ada_b, norm1_g, norm2_g, wqkv_a, q_norm_a, k_norm_a, sinks_a, wo_a, wqkv_b, wo_b, w_gate, w_up, w_down, loss_target, m_ada_w, m_ada_b, m_norm1_g, m_norm2_g, m_wqkv_a, m_q_norm_a, m_k_norm_a, m_sinks_a, m_wo_a, m_wqkv_b, m_wo_b, m_w_gate, m_w_up, m_w_down, v_ada_w, v_ada_b, v_norm1_g, v_norm2_g, v_wqkv_a, v_q_norm_a, v_k_norm_a, v_sinks_a, v_wo_a, v_wqkv_b, v_wo_b, v_w_gate, v_w_up, v_w_down):
    B, S, D = x.shape
    T = B * S
    L = ada_w.shape[0]
    NA, NB_ = wqkv_a.shape[0], wqkv_b.shape[0]
    me = 4 * lax.axis_index("x") + 2 * lax.axis_index("y") + lax.axis_index("c")
    xt = x.reshape(T, D)

    col_sharded = {"qkv_a": wqkv_a, "qkv_b": wqkv_b, "gate": w_gate, "up": w_up}
    row_sharded = {"wo_a": wo_a, "wo_b": wo_b, "down": w_down}

    def shard_rows(key):
        kind, idx = key
        return col_sharded[kind][idx].T if kind in col_sharded else row_sharded[kind][idx]

    def layer_keys(l):
        mix = "a" if l % 2 == 0 else "b"
        return [("qkv_" + mix, l // 2), ("wo_" + mix, l // 2), ("gate", l), ("up", l), ("down", l)]

    def unpack(buf, keys, reshape):
        out, off = {}, 0
        for key in keys:
            rows = shard_rows(key).shape[0]
            out[key] = reshape(buf[..., off:off + rows, :], rows)
            off += rows
        return out

    first_b = 1
    keys_early = layer_keys(0)[:2]
    keys_mid = layer_keys(0)[2:] + [("qkv_b", 0)]
    keys_late = [k for l in range(1, L) for k in layer_keys(l) if k != ("qkv_b", 0)]
    pack = lambda keys: jnp.concatenate([shard_rows(k).astype(BF16) for k in keys], axis=0)
    full_rows = lambda b, rows: b.reshape(NDEV * rows, D)
    W = unpack(_all_gather(pack(keys_early), "ag_weights"), keys_early, full_rows)

    WA = ada_w.shape[2]
    c_all = _all_gather(c, "ag_c").reshape(NDEV * B, D)
    bias = lax.dynamic_slice_in_dim(ada_b, me * WA, WA, axis=1).reshape(L, 1, WA)
    mod_part = _ada_fwd(c_all, ada_w, bias, "ada_fwd")
    mod_all = _all_gather(mod_part.reshape(L * NDEV * B, WA), "ag_mod")
    mod_all = mod_all.reshape(NDEV, L, NDEV * B, WA).transpose(1, 2, 0, 3).reshape(L, NDEV * B, NDEV * WA)
    mod = lax.dynamic_slice_in_dim(mod_all, me * B, B, axis=1)
    mod = mod.reshape(L, B, 6, 1, D)
    sh1, sc1, g1, sh2, sc2, g2 = [mod[:, :, k] for k in range(6)]

    half = ROT // 2
    inv_freq = jnp.power(jnp.float32(ROPE_THETA), -jnp.arange(half, dtype=F32) * 2.0 / ROT)
    ang = positions.reshape(T, 1).astype(F32) * inv_freq[None, :]
    cos, sin = jnp.cos(ang), jnp.sin(ang)
    ones = jnp.ones((T, HEAD - ROT), F32)
    zeros = jnp.zeros((T, HEAD - ROT), F32)
    z8 = jnp.zeros((T, half), F32)
    cs = jnp.tile(jnp.concatenate([cos, cos, ones], axis=1), (1, 2))
    s1 = jnp.tile(jnp.concatenate([-sin, z8, zeros], axis=1), (1, 2))
    s2 = jnp.tile(jnp.concatenate([z8, sin, zeros], axis=1), (1, 2))

    saved = []
    xc = xt
    h1 = _norm_mod(xc, norm1_g[0:1], sc1[0], sh1[0], S, "norm1_0")
    for l in range(L):
        j = l // 2
        sv = dict(x_in=xc, h1=h1)
        if l % 2 == 0:
            qkv = _mm_nt(h1, W["qkv_a", j], F32, f"qkv_a_{l}")
            qg = jnp.tile(q_norm_a[j:j + 1], (1, 2))
            kg = jnp.tile(k_norm_a[j:j + 1], (1, 2))
            qn, kd, vd = _qk_prep(qkv, cs, s1, s2, qg, kg, f"qk_prep_{l}")
            sink2 = jnp.repeat(sinks_a[j].reshape(-1, 2), HEAD, axis=1).reshape(-1, 1, LANES)
            if l == 0:
                attn, lse, mid = _swa_fwd(qn, kd, vd, sink2, B, f"swa_fwd_{l}", gather=pack(keys_mid))
                W.update(unpack(mid, keys_mid, full_rows))
            else:
                attn, lse = _swa_fwd(qn, kd, vd, sink2, B, f"swa_fwd_{l}")
            sv.update(qkv=qkv, qg=qg, kg=kg, qn=qn, kd=kd, vd=vd, sink2=sink2, lse=lse)
            wo = W["wo_a", j]
        else:
            qkv = _mm_nt(h1, W["qkv_b", j], BF16, f"qkv_b_{l}")
            if l == first_b:
                attn, tot, late = _sb_fwd(qkv, B, f"sb_fwd_{l}", gather=pack(keys_late))
                W.update(unpack(late, keys_late, full_rows))
            else:
                attn, tot = _sb_fwd(qkv, B, f"sb_fwd_{l}")
            sv.update(qkv=qkv, tot=tot)
            wo = W["wo_b", j]
        y1, xm, h2 = _mm_res(attn, wo, xc, g1[l], S, f"attn_out_{l}", norm=(norm2_g[l:l + 1], sc2[l], sh2[l]))
        gate, up, act = _swiglu_fwd(h2, W["gate", l], W["up", l], f"swiglu_fwd_{l}")
        if l + 1 < L:
            y2, xc, h1 = _mm_res(act, W["down", l], xm, g2[l], S, f"mlp_out_{l}",
                                 norm=(norm1_g[l + 1:l + 2], sc1[l + 1], sh1[l + 1]))
        else:
            y2, xc = _mm_res(act, W["down", l], xm, g2[l], S, f"mlp_out_{l}")
        sv.update(attn=attn, y1=y1, x_mid=xm, h2=h2, gate=gate, up=up, act=act, y2=y2)
        saved.append(sv)

    dx, loss_tile = _loss_head(xc, loss_target.reshape(T, D), "loss_head")

    G = {}
    pack_grads = lambda keys: jnp.concatenate([G[k].reshape(NDEV, G[k].shape[0] // NDEV, D) for k in keys], axis=1)
    keys_hi = [k for l in range(first_b + 1, L) for k in layer_keys(l)] + layer_keys(first_b)[1:]
    keys_mlp0_g = [("qkv_b", 0), ("down", 0)]
    keys_mid_g = [("wo_a", 0), ("gate", 0), ("up", 0)]
    keys_lo = layer_keys(0)[:1]
    received_hi = received_mid = received_mlp0 = None
    dmod = [None] * L
    dn1, dn2 = [None] * L, [None] * L
    dqg, dkg, dsink = [None] * NA, [None] * NA, [None] * NA
    dy2, dg2 = _gate_bwd(dx, saved[L - 1]["y2"], g2[L - 1], S, "gate2_bwd_top")
    for l in reversed(range(L)):
        j = l // 2
        mix = "a" if l % 2 == 0 else "b"
        sv = saved[l]
        dgate, dup = _swiglu_bwd(dy2, W["down", l], sv["gate"], sv["up"], f"swiglu_bwd_{l}")
        G["down", l] = _mm_tn(sv["act"], dy2, f"dw_down_{l}")
        G["gate", l] = _mm_tn(dgate, sv["h2"], f"dw_gate_{l}")
        G["up", l] = _mm_tn(dup, sv["h2"], f"dw_up_{l}")
        n2 = _norm_mod_bwd(sv["x_mid"], [(dgate, W["gate", l]), (dup, W["up", l])], dx, norm2_g[l:l + 1], sc2[l], S,
                           f"norm2_bwd_{l}", below=(sv["y1"], g1[l]), exchange=pack_grads(keys_mlp0_g) if l == 0 else None)
        dxm, dsh2, dsc2, dn2[l], dy1, dg1 = n2[:6]
        if l == 0:
            received_mlp0 = n2[6]
        dattn = _mm_nt(dy1, W["wo_" + mix, j], BF16, f"dattn_{l}")
        G["wo_" + mix, j] = _mm_tn(sv["attn"], dy1, f"dw_o_{l}")
        if l % 2 == 0:
            swa_args = (sv["qn"], sv["kd"], sv["vd"], sv["sink2"], dattn, sv["lse"], B, f"swa_bwd_{l}")
            if l == 0:
                dq, dkc, dkp, dvc, dvp, dsink[j], received_mid = _swa_bwd(*swa_args, exchange=pack_grads(keys_mid_g))
            else:
                dq, dkc, dkp, dvc, dvp, dsink[j] = _swa_bwd(*swa_args)
            dqkv, dqg[j], dkg[j] = _qk_prep_bwd(sv["qkv"], cs, s1, s2, sv["qg"], sv["kg"], dq, dkc, dkp, dvc, dvp, B,
                                                f"qk_prep_bwd_{l}")
        else:
            nqb = sv["qkv"].shape[1] // 3
            sb_args = (sv["qkv"], sv["qkv"][:, :nqb].T, dattn, dattn.T, sv["tot"], B, f"sb_bwd_{l}")
            if l == first_b and keys_hi:
                dq, dk_t, dv_t, received_hi = _sb_bwd(*sb_args, exchange=pack_grads(keys_hi))
            else:
                dq, dk_t, dv_t = _sb_bwd(*sb_args)
            dk, dv = [t.reshape(B, nqb, S).transpose(0, 2, 1).reshape(T, nqb) for t in (dk_t, dv_t)]
            dqkv = jnp.concatenate([dq, dk, dv], axis=1).astype(BF16)
        G["qkv_" + mix, j] = _mm_tn(dqkv, sv["h1"], f"dw_qkv_{l}")
        n1_args = (sv["x_in"], [(dqkv, W["qkv_" + mix, j])], dxm, norm1_g[l:l + 1], sc1[l], S, f"norm1_bwd_{l}")
        dmod_l = [None, None, dg1, dsh2, dsc2, dg2]
        if l > 0:
            dx, dmod_l[0], dmod_l[1], dn1[l], dy2, dg2 = _norm_mod_bwd(*n1_args, below=(saved[l - 1]["y2"], g2[l - 1]))
        else:
            dx, dmod_l[0], dmod_l[1], dn1[l] = _norm_mod_bwd(*n1_args)
        dmod[l] = jnp.concatenate(dmod_l, axis=1)
    grad_x = dx.reshape(B, S, D)

    ndm = L * 6
    dmod_rows = jnp.stack(dmod, axis=1).reshape(B * ndm, D)
    misc = jnp.concatenate(
        [jnp.concatenate(dn1, axis=0).reshape(B * L, D), jnp.concatenate(dn2, axis=0).reshape(B * L, D),
         _pad_rows(jnp.concatenate([jnp.pad(r, ((0, 0), (0, D - LANES))) for r in dqg + dkg]
                                   + [jnp.pad(r[:, 0, ::HEAD].reshape(1, -1), ((0, 0), (0, D - 2 * r.shape[0]))) for r in dsink]
                                   + [jnp.pad(loss_tile[0:1, 0:1], ((0, 0), (0, D - 1)))], axis=0), 8)], axis=0)
    nmisc = misc.shape[0]
    small = _all_gather(jnp.concatenate([dmod_rows, _pad_rows(misc, -(-nmisc // 8) * 8)], axis=0), "ag_small")
    dmod_all = small[:, :B * ndm].reshape(NDEV * B, ndm, D)
    g_ada_b = _sum_leading(dmod_all, "sum_dmod").reshape(L, 6 * D)
    misc_sum = _sum_leading(small[:, B * ndm:], "sum_misc")
    g_n1 = misc_sum[0:B * L].reshape(L, B, D)
    g_n2 = misc_sum[B * L:2 * B * L].reshape(L, B, D)
    g_norm1 = _sum_leading(g_n1.transpose(1, 0, 2), "sum_n1")
    g_norm2 = _sum_leading(g_n2.transpose(1, 0, 2), "sum_n2")
    o = 2 * B * L
    g_qn = misc_sum[o:o + NA, :HEAD]
    g_kn = misc_sum[o + NA:o + 2 * NA, :HEAD]
    nsink = sinks_a.shape[1]
    g_sink = misc_sum[o + 2 * NA:o + 3 * NA, :nsink]
    loss = misc_sum[o + 3 * NA, 0]

    dmod_loc = lax.dynamic_slice_in_dim(dmod_all.reshape(NDEV * B, L, 6 * D), me * WA, WA, axis=2)
    g_ada_w = _ada_bwd(c_all, dmod_loc.transpose(1, 0, 2), "ada_bwd")

    shard = unpack(_sum_leading(_exchange(pack_grads(keys_lo), "grad_exchange"), "grad_sum"), keys_lo, lambda b, rows: b)
    shard.update(unpack(_sum_leading(received_mid, "grad_sum_mid"), keys_mid_g, lambda b, rows: b))
    shard.update(unpack(_sum_leading(received_mlp0, "grad_sum_mlp0"), keys_mlp0_g, lambda b, rows: b))
    if received_hi is not None:
        shard.update(unpack(_sum_leading(received_hi, "grad_sum_hi"), keys_hi, lambda b, rows: b))

    def stacked(kind, n):
        return jnp.stack([shard[kind, i].T if kind in col_sharded else shard[kind, i] for i in range(n)])

    gw_qkv_a, gw_qkv_b, gw_gate, gw_up = stacked("qkv_a", NA), stacked("qkv_b", NB_), stacked("gate", L), stacked("up", L)
    gw_wo_a, gw_wo_b, gw_down = stacked("wo_a", NA), stacked("wo_b", NB_), stacked("down", L)

    grads = [g_ada_w, g_ada_b, g_norm1, g_norm2, gw_qkv_a, g_qn, g_kn, g_sink, gw_wo_a, gw_qkv_b, gw_wo_b,
             gw_gate, gw_up, gw_down]
    ws = [ada_w, ada_b, norm1_g, norm2_g, wqkv_a, q_norm_a, k_norm_a, sinks_a, wo_a, wqkv_b, wo_b, w_gate, w_up, w_down]
    ms = [m_ada_w, m_ada_b, m_norm1_g, m_norm2_g, m_wqkv_a, m_q_norm_a, m_k_norm_a, m_sinks_a, m_wo_a, m_wqkv_b,
          m_wo_b, m_w_gate, m_w_up, m_w_down]
    vs = [v_ada_w, v_ada_b, v_norm1_g, v_norm2_g, v_wqkv_a, v_q_norm_a, v_k_norm_a, v_sinks_a, v_wo_a, v_wqkv_b,
          v_wo_b, v_w_gate, v_w_up, v_w_down]
    deltas, new_m, new_v = [], [], []
    for k, (w, g, m, v) in enumerate(zip(ws, grads, ms, vs)):
        g = g.reshape(w.shape)
        d, mn, vn = _adamw(w, g, m, v, f"adamw_{k}")
        grads[k] = g
        deltas.append(d)
        new_m.append(mn)
        new_v.append(vn)
    return (loss, grad_x, *grads, *deltas, *new_m, *new_v)
```

```python
import functools
import math

import jax
import jax.numpy as jnp
from jax import lax
from jax.experimental import pallas as pl
from jax.experimental.pallas import tpu as pltpu

F32 = jnp.float32
BF16 = jnp.bfloat16
NDEV = 8
HEAD = 64
BLK = 128
LANES = 128
EPS = 1e-6
ROT = HEAD // 4
ROPE_THETA = 500000.0
SCALE = HEAD ** -0.5
LOG2E = math.log2(math.e)
NEG = -1e30
VMEM_LIMIT = 56 * 1024 * 1024
MESH = pl.DeviceIdType.MESH
HIGH = lax.Precision.HIGHEST

ADAM_LR = 0.001
ADAM_B1 = 0.9
ADAM_B2 = 0.999
ADAM_EPS = 1e-08
ADAM_WD = 0.01
ADAM_STEP = 10


def _params(*sem):
    return pltpu.CompilerParams(dimension_semantics=sem, vmem_limit_bytes=VMEM_LIMIT)


def _pick(n, cap, mult):
    if n <= cap:
        return n
    best = None
    for t in range(mult, cap + 1, mult):
        if n % t == 0:
            best = t
    assert best is not None, (n, cap, mult)
    return best


def _dot(a, b, dims, precision=None):
    return lax.dot_general(a, b, (dims, ((), ())), preferred_element_type=F32, precision=precision)


NN = ((1,), (0,))
NT = ((1,), (1,))
TN = ((0,), (0,))


def _all_gather(x, name):
    m, n = x.shape

    def body(x_ref, out_ref, send_sems, recv_sems, local_sem):
        ix, iy, ic = lax.axis_index("x"), lax.axis_index("y"), lax.axis_index("c")
        me, sibling = (ix, iy, ic), (ix, iy, 1 - ic)
        chips = [(1 - ix, iy), (ix, 1 - iy), (1 - ix, 1 - iy)]

        def slab(px, py, pc):
            return out_ref.at[4 * px + 2 * py + pc]

        def copy(k, block, to, src=None):
            return pltpu.make_async_remote_copy(
                src_ref=slab(*block) if src is None else src, dst_ref=slab(*block),
                send_sem=send_sems.at[k], recv_sem=recv_sems.at[k], device_id=to, device_id_type=MESH)

        mine = pltpu.make_async_copy(x_ref, slab(*me), local_sem)
        mine.start()
        first = [copy(0, me, sibling, src=x_ref)]
        first += [copy(1 + j, me, (*chip, ic), src=x_ref) for j, chip in enumerate(chips)]
        for cp in first:
            cp.start()
        passed = [copy(4 + j, (*chip, ic), sibling) for j, chip in enumerate(chips)]
        for j, chip in enumerate(chips):
            copy(1 + j, (*chip, ic), me).wait_recv()
            passed[j].start()
        copy(0, sibling, me).wait_recv()
        for j, chip in enumerate(chips):
            copy(4 + j, (*chip, 1 - ic), me).wait_recv()
        for cp in first + passed:
            cp.wait_send()
        mine.wait()

    return pl.pallas_call(
        body, name=name,
        out_shape=jax.ShapeDtypeStruct((NDEV, m, n), x.dtype),
        in_specs=[pl.BlockSpec(memory_space=pl.ANY)],
        out_specs=pl.BlockSpec(memory_space=pl.ANY),
        scratch_shapes=[pltpu.SemaphoreType.DMA((7,)), pltpu.SemaphoreType.DMA((7,)), pltpu.SemaphoreType.DMA(())],
    )(x)


COMM_SEMS = [pltpu.SemaphoreType.DMA((NDEV - 1,)), pltpu.SemaphoreType.DMA((NDEV - 1,)), pltpu.SemaphoreType.DMA(())]
HBM_SPEC = pl.BlockSpec(memory_space=pl.ANY)


def _direct_copies(src_ref, dst_ref, sems, scatter):
    send_sems, recv_sems, own_sem = sems
    ix, iy, ic = lax.axis_index("x"), lax.axis_index("y"), lax.axis_index("c")
    me = 4 * ix + 2 * iy + ic
    copies = [pltpu.make_async_copy(src_ref.at[me] if scatter else src_ref, dst_ref.at[me], own_sem)]
    for k in range(1, NDEV):
        px = 1 - ix if k & 4 else ix
        py = 1 - iy if k & 2 else iy
        pc = 1 - ic if k & 1 else ic
        copies.append(pltpu.make_async_remote_copy(
            src_ref=src_ref.at[4 * px + 2 * py + pc] if scatter else src_ref, dst_ref=dst_ref.at[me],
            send_sem=send_sems.at[k - 1], recv_sem=recv_sems.at[k - 1],
            device_id=(px, py, pc), device_id_type=MESH))
    return copies


def _exchange(p, name):
    def body(p_ref, r_ref, *sems):
        copies = _direct_copies(p_ref, r_ref, sems, True)
        for cp in copies:
            cp.start()
        for cp in copies:
            cp.wait()

    return pl.pallas_call(
        body, name=name,
        out_shape=jax.ShapeDtypeStruct(p.shape, p.dtype),
        in_specs=[HBM_SPEC], out_specs=HBM_SPEC, scratch_shapes=COMM_SEMS,
    )(p)


def _call_behind(body, name, grid, in_specs, out_specs, out_shape, args, payload=None, scatter=False):
    params = _params(*["arbitrary"] * len(grid))
    if payload is None:
        return pl.pallas_call(body, name=name, grid=grid, in_specs=in_specs, out_specs=out_specs, out_shape=out_shape,
                              compiler_params=params)(*args)
    n_in, n_out = len(in_specs), len(out_specs)

    def edge(first):
        ids = [pl.program_id(a) for a in range(len(grid))]
        return functools.reduce(lambda u, v: u & v, [i == (0 if first else d - 1) for i, d in zip(ids, grid)])

    def wrapped(*refs):
        x_ref, r_ref, sems = refs[n_in], refs[n_in + 1 + n_out], refs[n_in + n_out + 2:]

        @pl.when(edge(True))
        def _():
            for cp in _direct_copies(x_ref, r_ref, sems, scatter):
                cp.start()
        body(*refs[:n_in], *refs[n_in + 1:n_in + 1 + n_out])

        @pl.when(edge(False))
        def _():
            for cp in _direct_copies(x_ref, r_ref, sems, scatter):
                cp.wait()

    arrived = jax.ShapeDtypeStruct(payload.shape if scatter else (NDEV,) + payload.shape, payload.dtype)
    return pl.pallas_call(
        wrapped, name=name, grid=grid, in_specs=list(in_specs) + [HBM_SPEC], out_specs=list(out_specs) + [HBM_SPEC],
        out_shape=list(out_shape) + [arrived], scratch_shapes=COMM_SEMS, compiler_params=params,
    )(*args, payload)


def _sum_leading(r, name):
    k, m, n = r.shape
    mult = 8 * (4 // r.dtype.itemsize)
    tm = _pick(m, max(mult, (4 * 1024 * 1024) // (k * n * r.dtype.itemsize) // mult * mult), mult)

    def body(r_ref, o_ref):
        acc = r_ref[0].astype(F32)
        for s in range(1, k):
            acc = acc + r_ref[s].astype(F32)
        o_ref[...] = acc

    return pl.pallas_call(
        body, name=name, grid=(m // tm,),
        in_specs=[pl.BlockSpec((k, tm, n), lambda i: (0, i, 0))],
        out_specs=pl.BlockSpec((tm, n), lambda i: (i, 0)),
        out_shape=jax.ShapeDtypeStruct((m, n), F32),
        compiler_params=_params("parallel"),
    )(r)


def _mm_nt(a, bt, out_dtype, name):
    M, K = a.shape
    N = bt.shape[0]
    tm, tn = _pick(M, 512, 8), _pick(N, 1536, LANES)

    def body(a_ref, b_ref, o_ref):
        o_ref[...] = _dot(a_ref[...], b_ref[...], NT).astype(out_dtype)

    return pl.pallas_call(
        body, name=name, grid=(N // tn, M // tm),
        in_specs=[pl.BlockSpec((tm, K), lambda j, i: (i, 0)), pl.BlockSpec((tn, K), lambda j, i: (j, 0))],
        out_specs=pl.BlockSpec((tm, tn), lambda j, i: (i, j)),
        out_shape=jax.ShapeDtypeStruct((M, N), out_dtype),
        compiler_params=_params("parallel", "parallel"),
    )(a, bt)


def _mm_tn(a, b, name):
    M, N1 = a.shape
    N2 = b.shape[1]
    t1, tk = _pick(N1, 1536, LANES), _pick(M, 512, 8)
    nk = M // tk

    def body(a_ref, b_ref, o_ref, acc_ref):
        k = pl.program_id(1)

        @pl.when(k == 0)
        def _():
            acc_ref[...] = jnp.zeros_like(acc_ref)
        acc_ref[...] += _dot(a_ref[...], b_ref[...], TN)

        @pl.when(k == nk - 1)
        def _():
            o_ref[...] = acc_ref[...].astype(BF16)

    return pl.pallas_call(
        body, name=name, grid=(N1 // t1, nk),
        in_specs=[pl.BlockSpec((tk, t1), lambda i, k: (k, i)), pl.BlockSpec((tk, N2), lambda i, k: (k, 0))],
        out_specs=pl.BlockSpec((t1, N2), lambda i, k: (i, 0)),
        out_shape=jax.ShapeDtypeStruct((N1, N2), BF16),
        scratch_shapes=[pltpu.VMEM((t1, N2), F32)],
        compiler_params=_params("parallel", "arbitrary"),
    )(a, b)


def _norm_mod_rows(xv, gain, sc, sh):
    r = lax.rsqrt(jnp.mean(xv * xv, axis=-1, keepdims=True) + EPS)
    return ((xv * r) * gain * (1.0 + sc) + sh).astype(BF16)


def _mm_res(a, w, x, gate, S, name, norm=None):
    T, K = a.shape
    D = w.shape[1]
    tm = _pick(S, 512, 8)
    nb = S // tm

    def body(a_ref, w_ref, x_ref, g_ref, *rest):
        y = _dot(a_ref[...], w_ref[...], NN)
        xn = x_ref[...] + g_ref[0] * y
        if norm is None:
            y_ref, o_ref = rest
        else:
            gain_ref, sc_ref, sh_ref, y_ref, o_ref, h_ref = rest
            h_ref[...] = _norm_mod_rows(xn, gain_ref[...], sc_ref[0], sh_ref[0])
        y_ref[...] = y.astype(BF16)
        o_ref[...] = xn

    spec_t = pl.BlockSpec((tm, D), lambda i: (i, 0))
    spec_b = pl.BlockSpec((1, 1, D), lambda i: (i // nb, 0, 0))
    in_specs = [pl.BlockSpec((tm, K), lambda i: (i, 0)), pl.BlockSpec((K, D), lambda i: (0, 0)), spec_t, spec_b]
    out_specs = [spec_t, spec_t]
    out_shape = [jax.ShapeDtypeStruct((T, D), BF16), jax.ShapeDtypeStruct((T, D), F32)]
    args = [a, w, x, gate]
    if norm is not None:
        in_specs += [pl.BlockSpec((1, D), lambda i: (0, 0)), spec_b, spec_b]
        out_specs.append(spec_t)
        out_shape.append(jax.ShapeDtypeStruct((T, D), BF16))
        args += list(norm)
    return pl.pallas_call(
        body, name=name, grid=(T // tm,), in_specs=in_specs, out_specs=out_specs, out_shape=out_shape,
        compiler_params=_params("parallel"),
    )(*args)


def _swiglu_fwd(h, wgt, wut, name):
    T, D = h.shape
    F = wgt.shape[0]
    tm, tn = _pick(T, 512, 8), _pick(F, 1536, LANES)

    def body(h_ref, g_ref, u_ref, go_ref, uo_ref, a_ref):
        hh = h_ref[...]
        g = _dot(hh, g_ref[...], NT)
        u = _dot(hh, u_ref[...], NT)
        go_ref[...] = g.astype(BF16)
        uo_ref[...] = u.astype(BF16)
        a_ref[...] = (g * jax.nn.sigmoid(g) * u).astype(BF16)

    spec_w = pl.BlockSpec((tn, D), lambda j, i: (j, 0))
    spec_o = pl.BlockSpec((tm, tn), lambda j, i: (i, j))
    out = jax.ShapeDtypeStruct((T, F), BF16)
    return pl.pallas_call(
        body, name=name, grid=(F // tn, T // tm),
        in_specs=[pl.BlockSpec((tm, D), lambda j, i: (i, 0)), spec_w, spec_w],
        out_specs=[spec_o, spec_o, spec_o],
        out_shape=[out, out, out],
        compiler_params=_params("parallel", "parallel"),
    )(h, wgt, wut)


def _swiglu_bwd(dy, wd, gate, up, name):
    T, D = dy.shape
    F = wd.shape[0]
    tm, tn = _pick(T, 512, 8), _pick(F, 1536, LANES)

    halves = [slice(0, tn // 2), slice(tn // 2, tn)] if tn % (2 * LANES) == 0 else [slice(0, tn)]

    def body(dy_ref, w_ref, g_ref, u_ref, dg_ref, du_ref):
        das = [_dot(dy_ref[...], w_ref[sl, :], NT) for sl in halves]
        for sl, da in zip(halves, das):
            g = g_ref[:, sl].astype(F32)
            sg = jax.nn.sigmoid(g)
            t = da * sg
            du_ref[:, sl] = (t * g).astype(BF16)
            dg_ref[:, sl] = (t * u_ref[:, sl].astype(F32) * (1.0 + g * (1.0 - sg))).astype(BF16)

    spec_o = pl.BlockSpec((tm, tn), lambda j, i: (i, j))
    return pl.pallas_call(
        body, name=name, grid=(F // tn, T // tm),
        in_specs=[pl.BlockSpec((tm, D), lambda j, i: (i, 0)), pl.BlockSpec((tn, D), lambda j, i: (j, 0)), spec_o, spec_o],
        out_specs=[spec_o, spec_o],
        out_shape=[jax.ShapeDtypeStruct((T, F), BF16), jax.ShapeDtypeStruct((T, F), BF16)],
        compiler_params=_params("parallel", "parallel"),
    )(dy, wd, gate, up)


def _norm_mod(x, gain, sc, sh, S, name):
    T, D = x.shape
    tm = _pick(S, 512, 8)
    nb = S // tm

    def body(x_ref, g_ref, sc_ref, sh_ref, o_ref):
        o_ref[...] = _norm_mod_rows(x_ref[...], g_ref[...], sc_ref[0], sh_ref[0])

    spec_b = pl.BlockSpec((1, 1, D), lambda i: (i // nb, 0, 0))
    return pl.pallas_call(
        body, name=name, grid=(T // tm,),
        in_specs=[pl.BlockSpec((tm, D), lambda i: (i, 0)), pl.BlockSpec((1, D), lambda i: (0, 0)), spec_b, spec_b],
        out_specs=pl.BlockSpec((tm, D), lambda i: (i, 0)),
        out_shape=jax.ShapeDtypeStruct((T, D), BF16),
        compiler_params=_params("parallel"),
    )(x, gain, sc, sh)


def _norm_mod_bwd(x, pairs, dres, gain, sc, S, name, below=None, exchange=None):
    T, D = x.shape
    B = T // S
    tm = _pick(S, 512, 8)
    nb = S // tm
    n_mm = 2 * len(pairs)

    def body(*refs):
        mm, (x_ref, dr_ref, g_ref, sc_ref), rest = refs[:n_mm], refs[n_mm:n_mm + 4], refs[n_mm + 4:]
        if below is None:
            o_ref, dsh_ref, dsc_ref, dg_ref = rest
            sums = [dsh_ref, dsc_ref, dg_ref]
        else:
            y_ref, gt_ref, o_ref, dsh_ref, dsc_ref, dg_ref, dy_ref, dgt_ref = rest
            sums = [dsh_ref, dsc_ref, dg_ref, dgt_ref]

        @pl.when(pl.program_id(1) == 0)
        def _():
            for ref in sums:
                ref[...] = jnp.zeros_like(ref)
        dhv = _dot(mm[0][...], mm[1][...], NN)
        for p in range(2, n_mm, 2):
            dhv = dhv + _dot(mm[p][...], mm[p + 1][...], NN)
        xv, g = x_ref[...], g_ref[...]
        r = lax.rsqrt(jnp.mean(xv * xv, axis=-1, keepdims=True) + EPS)
        xhat = xv * r
        dsh_ref[0] += jnp.sum(dhv, axis=0, keepdims=True)
        dsc_ref[0] += jnp.sum(dhv * (xhat * g), axis=0, keepdims=True)
        dn = dhv * (1.0 + sc_ref[0])
        dg_ref[0] += jnp.sum(dn * xhat, axis=0, keepdims=True)
        dxh = dn * g
        out = dr_ref[...] + r * (dxh - xhat * jnp.mean(dxh * xhat, axis=-1, keepdims=True))
        o_ref[...] = out
        if below is not None:
            dy_ref[...] = (out * gt_ref[0]).astype(BF16)
            dgt_ref[0] += jnp.sum(out * y_ref[...].astype(F32), axis=0, keepdims=True)

    spec_t = pl.BlockSpec((tm, D), lambda b, i: (b * nb + i, 0))
    spec_b = pl.BlockSpec((1, 1, D), lambda b, i: (b, 0, 0))
    red = jax.ShapeDtypeStruct((B, 1, D), F32)
    in_specs, args = [], []
    for a, w in pairs:
        K = a.shape[1]
        in_specs += [pl.BlockSpec((tm, K), lambda b, i: (b * nb + i, 0)),
                     pl.BlockSpec((K, D), lambda b, i: (0, 0), pipeline_mode=pl.Buffered(1))]
        args += [a, w]
    in_specs += [spec_t, spec_t, pl.BlockSpec((1, D), lambda b, i: (0, 0)), spec_b]
    args += [x, dres, gain, sc]
    out_specs = [spec_t, spec_b, spec_b, spec_b]
    out_shape = [jax.ShapeDtypeStruct((T, D), F32), red, red, red]
    if below is not None:
        in_specs += [spec_t, spec_b]
        out_specs += [spec_t, spec_b]
        out_shape += [jax.ShapeDtypeStruct((T, D), BF16), red]
        args += list(below)
    return _call_behind(body, name, (B, nb), in_specs, out_specs, out_shape, args, exchange, True)


def _gate_bwd(dx, y, gate, S, name):
    T, D = dx.shape
    B = T // S
    tm = _pick(S, 512, 8)
    nb = S // tm

    def body(dx_ref, y_ref, g_ref, dy_ref, dg_ref):
        @pl.when(pl.program_id(1) == 0)
        def _():
            dg_ref[...] = jnp.zeros_like(dg_ref)
        d = dx_ref[...]
        dy_ref[...] = (d * g_ref[0]).astype(BF16)
        dg_ref[0] += jnp.sum(d * y_ref[...].astype(F32), axis=0, keepdims=True)

    spec_t = pl.BlockSpec((tm, D), lambda b, i: (b * nb + i, 0))
    spec_b = pl.BlockSpec((1, 1, D), lambda b, i: (b, 0, 0))
    return pl.pallas_call(
        body, name=name, grid=(B, nb),
        in_specs=[spec_t, spec_t, spec_b],
        out_specs=[spec_t, spec_b],
        out_shape=[jax.ShapeDtypeStruct((T, D), BF16), jax.ShapeDtypeStruct((B, 1, D), F32)],
        compiler_params=_params("parallel", "arbitrary"),
    )(dx, y, gate)


def _loss_head(y, target, name):
    T, D = y.shape
    tm = _pick(T, 512, 8)

    def body(y_ref, t_ref, dy_ref, l_ref):
        @pl.when(pl.program_id(0) == 0)
        def _():
            l_ref[...] = jnp.zeros_like(l_ref)
        e = y_ref[...] - t_ref[...]
        dy_ref[...] = e * (1.0 / D)
        l_ref[...] += 0.5 * jnp.sum(jnp.mean(e * e, axis=-1, keepdims=True), axis=0, keepdims=True)

    spec = pl.BlockSpec((tm, D), lambda i: (i, 0))
    return pl.pallas_call(
        body, name=name, grid=(T // tm,),
        in_specs=[spec, spec],
        out_specs=[spec, pl.BlockSpec((8, LANES), lambda i: (0, 0))],
        out_shape=[jax.ShapeDtypeStruct((T, D), F32), jax.ShapeDtypeStruct((8, LANES), F32)],
        compiler_params=_params("arbitrary"),
    )(y, target)


def _ada_fwd(c_all, ada_w, bias, name):
    NB, D = c_all.shape
    L, _, W = ada_w.shape

    def body(c_ref, w_ref, b_ref, o_ref):
        cv = c_ref[...]
        cond = cv * jax.nn.sigmoid(cv)
        o_ref[0] = _dot(cond, w_ref[0], NN, HIGH) + b_ref[0]

    return pl.pallas_call(
        body, name=name, grid=(L,),
        in_specs=[pl.BlockSpec((NB, D), lambda l: (0, 0)), pl.BlockSpec((1, D, W), lambda l: (l, 0, 0)),
                  pl.BlockSpec((1, 1, W), lambda l: (l, 0, 0))],
        out_specs=pl.BlockSpec((1, NB, W), lambda l: (l, 0, 0)),
        out_shape=jax.ShapeDtypeStruct((L, NB, W), F32),
        compiler_params=_params("parallel"),
    )(c_all, ada_w, bias)


def _ada_bwd(c_all, dmod, name):
    NB, D = c_all.shape
    L, _, W = dmod.shape

    def body(c_ref, d_ref, o_ref):
        cv = c_ref[...]
        cond = cv * jax.nn.sigmoid(cv)
        o_ref[0] = _dot(cond, d_ref[0], TN, HIGH)

    return pl.pallas_call(
        body, name=name, grid=(L,),
        in_specs=[pl.BlockSpec((NB, D), lambda l: (0, 0)), pl.BlockSpec((1, NB, W), lambda l: (l, 0, 0))],
        out_specs=pl.BlockSpec((1, D, W), lambda l: (l, 0, 0)),
        out_shape=jax.ShapeDtypeStruct((L, D, W), F32),
        compiler_params=_params("parallel"),
    )(c_all, dmod)


def _lo_mask(shape):
    return lax.broadcasted_iota(jnp.int32, shape, len(shape) - 1) < HEAD


def _head_sum_matrix():
    r = lax.broadcasted_iota(jnp.int32, (LANES, LANES), 0) // HEAD
    c = lax.broadcasted_iota(jnp.int32, (LANES, LANES), 1) // HEAD
    return (r == c).astype(BF16)


def _head_sum(x, P):
    hi = x.astype(BF16)
    lo = (x - hi.astype(F32)).astype(BF16)
    return _dot(hi, P, NN) + _dot(lo, P, NN)


def _rope(y, cs, s1, s2):
    return y * cs + pltpu.roll(y, LANES - ROT // 2, 1) * s1 + pltpu.roll(y, ROT // 2, 1) * s2


def _rope_bwd(d, cs, s1, s2):
    return d * cs + pltpu.roll(d * s1, ROT // 2, 1) + pltpu.roll(d * s2, LANES - ROT // 2, 1)


def _qk_prep(qkv, cs, s1, s2, qg, kg, name):
    T, W = qkv.shape
    NQ = W - 2 * LANES
    tm = _pick(T, 512, 8)

    def body(x_ref, cs_ref, s1_ref, s2_ref, qg_ref, kg_ref, q_ref, k_ref, v_ref):
        P = _head_sum_matrix()
        cs_, s1_, s2_ = cs_ref[...], s1_ref[...], s2_ref[...]
        lo = _lo_mask((tm, LANES))

        def norm_rope(xv, g):
            ms = _head_sum(xv * xv, P) * (1.0 / HEAD)
            return _rope(xv * lax.rsqrt(ms + EPS) * g, cs_, s1_, s2_)

        for j in range(NQ // LANES):
            q_ref[:, j * LANES:(j + 1) * LANES] = norm_rope(x_ref[:, j * LANES:(j + 1) * LANES], qg_ref[...]).astype(BF16)
        kr = norm_rope(x_ref[:, NQ:NQ + LANES], kg_ref[...])
        ks = pltpu.roll(kr, HEAD, 1)
        k_ref[:, :LANES] = jnp.where(lo, kr, ks).astype(BF16)
        k_ref[:, LANES:] = jnp.where(lo, ks, kr).astype(BF16)
        vr = x_ref[:, NQ + LANES:]
        vs = pltpu.roll(vr, HEAD, 1)
        v_ref[:, :LANES] = jnp.where(lo, vr, vs).astype(BF16)
        v_ref[:, LANES:] = jnp.where(lo, vs, vr).astype(BF16)

    spec_t = pl.BlockSpec((tm, LANES), lambda i: (i, 0))
    spec_g = pl.BlockSpec((1, LANES), lambda i: (0, 0))
    return pl.pallas_call(
        body, name=name, grid=(T // tm,),
        in_specs=[pl.BlockSpec((tm, W), lambda i: (i, 0)), spec_t, spec_t, spec_t, spec_g, spec_g],
        out_specs=[pl.BlockSpec((tm, NQ), lambda i: (i, 0)), pl.BlockSpec((tm, 2 * LANES), lambda i: (i, 0)),
                   pl.BlockSpec((tm, 2 * LANES), lambda i: (i, 0))],
        out_shape=[jax.ShapeDtypeStruct((T, NQ), BF16), jax.ShapeDtypeStruct((T, 2 * LANES), BF16),
                   jax.ShapeDtypeStruct((T, 2 * LANES), BF16)],
        compiler_params=_params("parallel"),
    )(qkv, cs, s1, s2, qg, kg)


def _stack_heads(x2):
    lo = _lo_mask(x2.shape)
    z = jnp.zeros_like(x2)
    return jnp.concatenate([jnp.where(lo, x2, z), jnp.where(lo, z, x2)], axis=0)


def _unstack_heads(xs):
    r = xs.shape[0] // 2
    return jnp.where(_lo_mask((r, LANES)), xs[:r], xs[r:])


def _swa_valid(i):
    qo = lax.broadcasted_iota(jnp.int32, (2 * BLK, 2 * BLK), 0) % BLK
    kc_ = lax.broadcasted_iota(jnp.int32, (2 * BLK, 2 * BLK), 1)
    rel = qo + BLK - kc_
    return (rel >= 0) & (rel < BLK) & ((kc_ >= BLK) | (i > 0))


def _swa_scores(q2, kk, sink2, valid):
    qs = _stack_heads(q2) * SCALE
    s = _dot(qs, kk, NT)
    sk = jnp.concatenate([jnp.broadcast_to(sink2[:, 0:1], (BLK, 1)), jnp.broadcast_to(sink2[:, HEAD:HEAD + 1], (BLK, 1))], axis=0)
    return qs, jnp.where(valid, s, NEG), sk


def _swa_fwd(q, kd, vd, sink2, B, name, gather=None):
    T, NQ = q.shape
    NP = NQ // LANES
    nq = T // B // BLK
    NG = kd.shape[1] // LANES
    grp = NP // NG

    def body(q_ref, kp_ref, kc_ref, vp_ref, vc_ref, s_ref, o_ref, l_ref):
        valid = _swa_valid(pl.program_id(2))
        kk = jnp.concatenate([kp_ref[...], kc_ref[...]], axis=0)
        vs = _stack_heads(jnp.concatenate([vp_ref[...], vc_ref[...]], axis=0))
        sls = [slice(jj * LANES, (jj + 1) * LANES) for jj in range(grp)]
        sc = [_swa_scores(q_ref[:, sl], kk, s_ref[jj], valid) for jj, sl in enumerate(sls)]
        ms = [jnp.maximum(jnp.max(s, axis=1, keepdims=True), sk) for _, s, sk in sc]
        ps = [jnp.exp(s - m) for (_, s, _), m in zip(sc, ms)]
        ls = [jnp.sum(p, axis=1, keepdims=True) + jnp.exp(sk - m) for p, (_, _, sk), m in zip(ps, sc, ms)]
        ps = [(p * (1.0 / l)).astype(BF16) for p, l in zip(ps, ls)]
        os_ = [_dot(jnp.concatenate([p[:BLK], p[BLK:]], axis=1), vs, NN) for p in ps]
        for sl, o, m, l in zip(sls, os_, ms, ls):
            o_ref[:, sl] = o.astype(BF16)
            l_ref[:, sl] = _unstack_heads(jnp.broadcast_to(m + jnp.log(l), (2 * BLK, LANES)))

    spec_q = pl.BlockSpec((BLK, grp * LANES), lambda b, g, i: (b * nq + i, g))
    spec_p = pl.BlockSpec((BLK, LANES), lambda b, g, i: (b * nq + jnp.maximum(i - 1, 0), g))
    spec_c = pl.BlockSpec((BLK, LANES), lambda b, g, i: (b * nq + i, g))
    in_specs = [spec_q, spec_p, spec_c, spec_p, spec_c, pl.BlockSpec((grp, 1, LANES), lambda b, g, i: (g, 0, 0))]
    out_shape = [jax.ShapeDtypeStruct((T, NQ), BF16), jax.ShapeDtypeStruct((T, NQ), F32)]
    return _call_behind(body, name, (B, NG, nq), in_specs, [spec_q, spec_q], out_shape, [q, kd, kd, vd, vd, sink2], gather, False)


def _swa_bwd(q, kd, vd, sink2, do, lse, B, name, exchange=None):
    T, NQ = q.shape
    NP = NQ // LANES
    nq = T // B // BLK
    NG = kd.shape[1] // LANES
    grp = NP // NG

    def body(q_ref, kp_ref, kc_ref, vp_ref, vc_ref, s_ref, do_ref, l_ref,
             dq_ref, dkc_ref, dkp_ref, dvc_ref, dvp_ref, ds_ref):
        b, i = pl.program_id(1), pl.program_id(2)

        @pl.when((b == 0) & (i == 0))
        def _():
            ds_ref[...] = jnp.zeros_like(ds_ref)
        valid = _swa_valid(i)
        kk = jnp.concatenate([kp_ref[...], kc_ref[...]], axis=0)
        vv = jnp.concatenate([vp_ref[...], vc_ref[...]], axis=0)
        sls = [slice(jj * LANES, (jj + 1) * LANES) for jj in range(grp)]
        sc = [_swa_scores(q_ref[:, sl], kk, s_ref[jj], valid) for jj, sl in enumerate(sls)]
        dos = [_stack_heads(do_ref[:, sl]) for sl in sls]
        dps = [_dot(d, vv, NT) for d in dos]
        lses = [jnp.concatenate([l_ref[:, sl][:, 0:1], l_ref[:, sl][:, HEAD:HEAD + 1]], axis=0) for sl in sls]
        ps = [jnp.exp(s - lse) for (_, s, _), lse in zip(sc, lses)]
        deltas = [jnp.sum(p * dp, axis=1, keepdims=True) for p, dp in zip(ps, dps)]
        dscs = [(p * (dp - delta)).astype(BF16) for p, dp, delta in zip(ps, dps, deltas)]
        dqs = [_dot(dsc, kk, NN) for dsc in dscs]
        dk = jnp.zeros((2 * BLK, LANES), F32)
        dv = jnp.zeros((2 * BLK, LANES), F32)
        for jj, sl in enumerate(sls):
            dsk = -jnp.exp(sc[jj][2] - lses[jj]) * deltas[jj]
            dsk_lo = jnp.sum(dsk[:BLK], axis=0, keepdims=True)
            dsk_hi = jnp.sum(dsk[BLK:], axis=0, keepdims=True)
            ds_ref[jj] += jnp.where(_lo_mask((1, LANES)), dsk_lo, dsk_hi)
            dq_ref[:, sl] = _unstack_heads(dqs[jj]) * SCALE
            dk = dk + _dot(dscs[jj], sc[jj][0], TN)
            dv = dv + _dot(ps[jj].astype(BF16), dos[jj], TN)
        dkp_ref[...] = dk[:BLK]
        dkc_ref[...] = dk[BLK:]
        dvp_ref[...] = dv[:BLK]
        dvc_ref[...] = dv[BLK:]

    spec_q = pl.BlockSpec((BLK, grp * LANES), lambda g, b, i: (b * nq + i, g))
    spec_p = pl.BlockSpec((BLK, LANES), lambda g, b, i: (b * nq + jnp.maximum(i - 1, 0), g))
    spec_c = pl.BlockSpec((BLK, LANES), lambda g, b, i: (b * nq + i, g))
    spec_s = pl.BlockSpec((grp, 1, LANES), lambda g, b, i: (g, 0, 0))
    kv = jax.ShapeDtypeStruct((T, NG * LANES), F32)
    in_specs = [spec_q, spec_p, spec_c, spec_p, spec_c, spec_s, spec_q, spec_q]
    out_specs = [spec_q, spec_c, spec_c, spec_c, spec_c, spec_s]
    out_shape = [jax.ShapeDtypeStruct((T, NQ), F32), kv, kv, kv, kv, jax.ShapeDtypeStruct((NP, 1, LANES), F32)]
    return _call_behind(body, name, (NG, B, nq), in_specs, out_specs, out_shape, [q, kd, kd, vd, vd, sink2, do, lse],
                        exchange, True)


def _qk_prep_bwd(qkv, cs, s1, s2, qg, kg, dq, dkc, dkp, dvc, dvp, B, name):
    T, W = qkv.shape
    NQ = W - 2 * LANES
    NP = NQ // LANES
    nq = T // B // BLK

    def body(x_ref, cs_ref, s1_ref, s2_ref, qg_ref, kg_ref, dq_ref, dkc_ref, dkp_ref, dvc_ref, dvp_ref,
             o_ref, dqg_ref, dkg_ref):
        b, i = pl.program_id(0), pl.program_id(1)

        @pl.when((b == 0) & (i == 0))
        def _():
            dqg_ref[...] = jnp.zeros_like(dqg_ref)
            dkg_ref[...] = jnp.zeros_like(dkg_ref)
        P = _head_sum_matrix()
        cs_, s1_, s2_ = cs_ref[...], s1_ref[...], s2_ref[...]
        lo = _lo_mask((BLK, LANES))
        has_next = (i + 1 < nq).astype(F32)

        def norm_rope_bwd(xv, g, d):
            du = _rope_bwd(d, cs_, s1_, s2_)
            r = lax.rsqrt(_head_sum(xv * xv, P) * (1.0 / HEAD) + EPS)
            xhat = xv * r
            dgain = jnp.sum(du * xhat, axis=0, keepdims=True)
            uu = du * g
            dx = r * (uu - xhat * (_head_sum(uu * xhat, P) * (1.0 / HEAD)))
            return dx, dgain + pltpu.roll(dgain, HEAD, 1)

        dqg = jnp.zeros((1, LANES), F32)
        for j in range(NP):
            sl = slice(j * LANES, (j + 1) * LANES)
            dx, dg = norm_rope_bwd(x_ref[:, sl], qg_ref[...], dq_ref[:, sl])
            o_ref[:, sl] = dx.astype(BF16)
            dqg = dqg + dg
        dqg_ref[...] += dqg

        def fold(c_ref, p_ref, g):
            sl = slice(g * LANES, (g + 1) * LANES)
            t = c_ref[:, sl] + has_next * p_ref[:, sl]
            return t + pltpu.roll(t, HEAD, 1)

        dk = jnp.where(lo, fold(dkc_ref, dkp_ref, 0), fold(dkc_ref, dkp_ref, 1))
        dx, dg = norm_rope_bwd(x_ref[:, NQ:NQ + LANES], kg_ref[...], dk)
        o_ref[:, NQ:NQ + LANES] = dx.astype(BF16)
        dkg_ref[...] += dg
        dv = jnp.where(lo, fold(dvc_ref, dvp_ref, 0), fold(dvc_ref, dvp_ref, 1))
        o_ref[:, NQ + LANES:] = dv.astype(BF16)

    spec_t = pl.BlockSpec((BLK, LANES), lambda b, i: (b * nq + i, 0))
    spec_g = pl.BlockSpec((1, LANES), lambda b, i: (0, 0))
    spec_c = pl.BlockSpec((BLK, 2 * LANES), lambda b, i: (b * nq + i, 0))
    spec_n = pl.BlockSpec((BLK, 2 * LANES), lambda b, i: (b * nq + jnp.minimum(i + 1, nq - 1), 0))
    row = jax.ShapeDtypeStruct((1, LANES), F32)
    return pl.pallas_call(
        body, name=name, grid=(B, nq),
        in_specs=[pl.BlockSpec((BLK, W), lambda b, i: (b * nq + i, 0)), spec_t, spec_t, spec_t, spec_g, spec_g,
                  pl.BlockSpec((BLK, NQ), lambda b, i: (b * nq + i, 0)), spec_c, spec_n, spec_c, spec_n],
        out_specs=[pl.BlockSpec((BLK, W), lambda b, i: (b * nq + i, 0)), spec_g, spec_g],
        out_shape=[jax.ShapeDtypeStruct((T, W), BF16), row, row],
        compiler_params=_params("arbitrary", "arbitrary"),
    )(qkv, cs, s1, s2, qg, kg, dq, dkc, dkp, dvc, dvp)


SB_TILE = 256
SB_UNROLL = 4
SB_UNROLL_BWD = 2


def _split_heads(x2, scale=None):
    lo = _lo_mask(x2.shape)
    z = jnp.zeros_like(x2)
    if scale is not None:
        x2 = x2 * scale
    return jnp.where(lo, x2, z), jnp.where(lo, z, x2)


def _sb_terms(qh, kj, diagonal):
    z = _dot(qh, kj, NT)
    e = jnp.exp2(jnp.abs(z) * (-LOG2E))
    lb = jnp.minimum(z, 0.0) - jnp.log(1.0 + e)
    L = lb - z
    if not diagonal:
        return lb, L, None, z, e
    strict = lax.broadcasted_iota(jnp.int32, z.shape, 1) < lax.broadcasted_iota(jnp.int32, z.shape, 0)
    return lb, jnp.where(strict, L, 0.0), strict, z, e


def _tri(n, cmp):
    r = lax.broadcasted_iota(jnp.int32, (n, n), 0)
    c = lax.broadcasted_iota(jnp.int32, (n, n), 1)
    return cmp(r, c).astype(BF16)


def _by_value(r, fns, carry):
    if len(fns) == 1:
        return fns[0](carry)
    half = len(fns) // 2
    return lax.cond(r < half, lambda cr: _by_value(r, fns[:half], cr), lambda cr: _by_value(r - half, fns[half:], cr), carry)


def _sb_fwd(qkv, B, name, gather=None):
    T, W = qkv.shape
    NQ = W // 3
    NP = NQ // LANES
    S = T // B
    tq = min(SB_TILE, S)
    nq = S // tq
    grid = (B, NP, nq)

    def body(q_ref, k_ref, v_ref, o_ref, t_ref):
        i = pl.program_id(2)
        qh = _split_heads(q_ref[...], SCALE)
        U = _tri(tq, lambda r, c: r > c)

        def sweep(tiles, cs, acc):
            chains = [(t, h) for t in range(len(tiles)) for h in range(2)]
            rows = [pl.ds(pl.multiple_of(j * tq, tq), tq) for j, _ in tiles]
            ks = [k_ref[r, :] for r in rows]
            vs = [_split_heads(v_ref[r, :]) for r in rows]
            terms = {(t, h): _sb_terms(qh[h], ks[t], tiles[t][1]) for t, h in chains}
            carry = {}
            for h in range(2):
                c = cs[h]
                for t in range(len(tiles)):
                    carry[t, h] = c
                    c = c + jnp.sum(terms[t, h][1], axis=1, keepdims=True)
                cs = cs[:h] + (c,) + cs[h + 1:]
            cum = {ch: _dot(terms[ch][1].astype(BF16), U, NN) for ch in chains}
            for ch in chains:
                a = jnp.exp(terms[ch][0] + (cum[ch] + carry[ch]))
                if tiles[ch[0]][1]:
                    a = jnp.where(terms[ch][2], a, 0.0)
                acc = acc + _dot(a.astype(BF16), vs[ch[0]][ch[1]], NN)
            return cs, acc

        zero = jnp.zeros((tq, 1), F32)
        rem = i % SB_UNROLL
        heads = [lambda cr, k=k: sweep([(i, True)] + [(i - 1 - t, False) for t in range(k)], *cr) for k in range(SB_UNROLL)]
        carry = _by_value(rem, heads, ((zero, zero), jnp.zeros((tq, LANES), F32)))
        step = lambda n, cr: sweep([(i - 1 - rem - SB_UNROLL * n - t, False) for t in range(SB_UNROLL)], *cr)
        cs, acc = lax.fori_loop(0, i // SB_UNROLL, step, carry)
        o_ref[...] = acc.astype(BF16)
        t_ref[...] = jnp.where(_lo_mask((tq, LANES)), cs[0], cs[1])

    spec_q = pl.BlockSpec((tq, LANES), lambda b, p, i: (b * nq + i, p))
    in_specs = [spec_q, pl.BlockSpec((S, LANES), lambda b, p, i: (b, NP + p)),
                pl.BlockSpec((S, LANES), lambda b, p, i: (b, 2 * NP + p))]
    out_shape = [jax.ShapeDtypeStruct((T, NQ), BF16), jax.ShapeDtypeStruct((T, NQ), F32)]
    return _call_behind(body, name, grid, in_specs, [spec_q, spec_q], out_shape, [qkv, qkv, qkv], gather, False)


def _sb_bwd(qkv, q_t, do, do_t, tot, B, name, exchange=None):
    T, W = qkv.shape
    NQ = W // 3
    NP = NQ // LANES
    S = T // B
    tq = min(SB_TILE, S)
    nq = S // tq
    grid = (B, NP, nq)

    def body(q_ref, k_ref, v_ref, do_ref, qt_ref, dot_ref, t_ref, dq_ref, dk_ref, dv_ref, dkm_ref, dvm_ref):
        i = pl.program_id(2)

        @pl.when(i == 0)
        def _():
            dk_ref[...] = jnp.zeros_like(dk_ref)
            dv_ref[...] = jnp.zeros_like(dv_ref)
        qh = _split_heads(q_ref[...], SCALE)
        doh = _split_heads(do_ref[...])
        top = lax.broadcasted_iota(jnp.int32, (LANES, tq), 0) < HEAD
        zt = jnp.zeros((LANES, tq), BF16)
        qt = qt_ref[...] * SCALE
        qth = (jnp.where(top, qt, zt), jnp.where(top, zt, qt))
        doth = (jnp.where(top, dot_ref[...], zt), jnp.where(top, zt, dot_ref[...]))
        tt = t_ref[...]
        tot = (tt[:, 0:1], tt[:, HEAD:HEAD + 1])
        Urev = _tri(tq, lambda r, c: r > c)
        Uexc = _tri(tq, lambda r, c: r < c)

        def sweep(tiles, carry):
            nt = len(tiles)
            chains = [(t, h) for t in range(nt) for h in range(2)]
            rows = [pl.ds(pl.multiple_of(j * tq, tq), tq) for j, _ in tiles]
            ks = [k_ref[r, :] for r in rows]
            vs = [v_ref[r, :] for r in rows]
            terms = {(t, h): _sb_terms(qh[h], ks[t], tiles[t][1]) for t, h in chains}
            cc = [carry[h][0] for h in range(2)]
            later = {}
            for t, h in chains:
                cc[h] = cc[h] + jnp.sum(terms[t, h][1], axis=1, keepdims=True)
                later[t, h] = tot[h] - cc[h]
            cum = {ch: _dot(terms[ch][1].astype(BF16), Urev, NN) for ch in chains}
            da = {(t, h): _dot(doh[h], vs[t], NT) for t, h in chains}
            a, g, before = {}, {}, {}
            cg = [carry[h][1] for h in range(2)]
            for ch in chains:
                a[ch] = jnp.exp(terms[ch][0] + (cum[ch] + later[ch]))
                if tiles[ch[0]][1]:
                    a[ch] = jnp.where(terms[ch][2], a[ch], 0.0)
                g[ch] = a[ch] * da[ch]
                before[ch] = cg[ch[1]]
                cg[ch[1]] = cg[ch[1]] + jnp.sum(g[ch], axis=1, keepdims=True)
            G = {ch: _dot(g[ch].astype(BF16), Uexc, NN) for ch in chains}
            dz = {}
            for ch in chains:
                d = g[ch] - jnp.exp(terms[ch][0]) * (g[ch] + (G[ch] + before[ch]))
                if tiles[ch[0]][1]:
                    d = jnp.where(terms[ch][2], d, 0.0)
                dz[ch] = d.astype(BF16)
            dq = [carry[h][2] for h in range(2)]
            for t, h in chains:
                dq[h] = dq[h] + _dot(dz[t, h], ks[t], NN)
            for t in range(nt):
                dk_ref[:, rows[t]] += _dot(qth[0], dz[t, 0], NN) + _dot(qth[1], dz[t, 1], NN)
                dv_ref[:, rows[t]] += _dot(doth[0], a[t, 0].astype(BF16), NN) + _dot(doth[1], a[t, 1].astype(BF16), NN)
            return tuple((cc[h], cg[h], dq[h]) for h in range(2))

        zero = jnp.zeros((tq, 1), F32)
        zq = jnp.zeros((tq, LANES), F32)
        step = lambda n, cr: sweep([(SB_UNROLL_BWD * n + t, False) for t in range(SB_UNROLL_BWD)], cr)
        carry = lax.fori_loop(0, i // SB_UNROLL_BWD, step, ((zero, zero, zq), (zero, zero, zq)))
        tails = [lambda cr, k=k: sweep([(i - k + t, False) for t in range(k)] + [(i, True)], cr) for k in range(SB_UNROLL_BWD)]
        carry = _by_value(i % SB_UNROLL_BWD, tails, carry)
        dq_ref[...] = (jnp.where(_lo_mask((tq, LANES)), carry[0][2], carry[1][2]) * SCALE).astype(BF16)

        @pl.when(i == nq - 1)
        def _():
            for c in range(nq):
                cols = slice(c * tq, (c + 1) * tq)
                dkm_ref[cols, :] = dk_ref[:, cols].T.astype(BF16)
                dvm_ref[cols, :] = dv_ref[:, cols].T.astype(BF16)

    spec_q = pl.BlockSpec((tq, LANES), lambda b, p, i: (b * nq + i, p))
    spec_t = pl.BlockSpec((LANES, tq), lambda b, p, i: (p, b * nq + i))
    spec_s = pl.BlockSpec((LANES, S), lambda b, p, i: (b * NP + p, 0))
    spec_m = pl.BlockSpec((S, LANES), lambda b, p, i: (b, p))
    key_side = jax.ShapeDtypeStruct((B * NQ, S), F32)
    token_major = jax.ShapeDtypeStruct((T, NQ), BF16)
    in_specs = [spec_q, pl.BlockSpec((S, LANES), lambda b, p, i: (b, NP + p)),
                pl.BlockSpec((S, LANES), lambda b, p, i: (b, 2 * NP + p)), spec_q, spec_t, spec_t, spec_q]
    out_specs = [spec_q, spec_s, spec_s, spec_m, spec_m]
    out_shape = [token_major, key_side, key_side, token_major, token_major]
    args = [qkv, qkv, qkv, do, q_t, do_t, tot]
    return _call_behind(body, name, grid, in_specs, out_specs, out_shape, args, exchange, True)


def _adamw(w, g, m, v, name):
    shape = w.shape
    cols = shape[-1]
    rows = math.prod(shape[:-1])
    tr = _pick(rows, max(8, (1 << 19) // max(cols, LANES) // 8 * 8), 8)

    def body(w_ref, g_ref, m_ref, v_ref, d_ref, mo_ref, vo_ref):
        gv = g_ref[...]
        mn = ADAM_B1 * m_ref[...] + (1.0 - ADAM_B1) * gv
        vn = ADAM_B2 * v_ref[...] + (1.0 - ADAM_B2) * (gv * gv)
        m_hat = mn / (1.0 - ADAM_B1 ** ADAM_STEP)
        v_hat = vn / (1.0 - ADAM_B2 ** ADAM_STEP)
        d_ref[...] = -ADAM_LR * (m_hat / (jnp.sqrt(v_hat) + ADAM_EPS) + ADAM_WD * w_ref[...])
        mo_ref[...] = mn
        vo_ref[...] = vn

    spec = pl.BlockSpec((tr, cols), lambda i: (i, 0))
    out = jax.ShapeDtypeStruct((rows, cols), F32)
    d, mn, vn = pl.pallas_call(
        body, name=name, grid=(rows // tr,),
        in_specs=[spec] * 4, out_specs=[spec] * 3, out_shape=[out] * 3,
        compiler_params=_params("parallel"),
    )(w.reshape(rows, cols), g.reshape(rows, cols), m.reshape(rows, cols), v.reshape(rows, cols))
    return d.reshape(shape), mn.reshape(shape), vn.reshape(shape)


def _pad_rows(a, rows):
    return jnp.pad(a, ((0, rows - a.shape[0]), (0, 0)))


def kernel(x, c, positions, ada_w, ada_b, norm1_g, norm2_g, wqkv_a, q_norm_a, k_norm_a, sinks_a, wo_a, wqkv_b, wo_b, w_gate, w_up, w_down, loss_target, m_ada_w, m_ada_b, m_norm1_g, m_norm2_g, m_wqkv_a, m_q_norm_a, m_k_norm_a, m_sinks_a, m_wo_a, m_wqkv_b, m_wo_b, m_w_gate, m_w_up, m_w_down, v_ada_w, v_ada_b, v_norm1_g, v_norm2_g, v_wqkv_a, v_q_norm_a, v_k_norm_a, v_sinks_a, v_wo_a, v_wqkv_b, v_wo_b, v_w_gate, v_w_up, v_w_down):
    B, S, D = x.shape
    T = B * S
    L = ada_w.shape[0]
    NA, NB_ = wqkv_a.shape[0], wqkv_b.shape[0]
    me = 4 * lax.axis_index("x") + 2 * lax.axis_index("y") + lax.axis_index("c")
    xt = x.reshape(T, D)

    col_sharded = {"qkv_a": wqkv_a, "qkv_b": wqkv_b, "gate": w_gate, "up": w_up}
    row_sharded = {"wo_a": wo_a, "wo_b": wo_b, "down": w_down}

    def shard_rows(key):
        kind, idx = key
        return col_sharded[kind][idx].T if kind in col_sharded else row_sharded[kind][idx]

    def layer_keys(l):
        mix = "a" if l % 2 == 0 else "b"
        return [("qkv_" + mix, l // 2), ("wo_" + mix, l // 2), ("gate", l), ("up", l), ("down", l)]

    def unpack(buf, keys, reshape):
        out, off = {}, 0
        for key in keys:
            rows = shard_rows(key).shape[0]
            out[key] = reshape(buf[..., off:off + rows, :], rows)
            off += rows
        return out

    first_b = 1
    keys_early = layer_keys(0)[:2]
    keys_mid = layer_keys(0)[2:] + [("qkv_b", 0)]
    keys_late = [k for l in range(1, L) for k in layer_keys(l) if k != ("qkv_b", 0)]
    pack = lambda keys: jnp.concatenate([shard_rows(k).astype(BF16) for k in keys], axis=0)
    full_rows = lambda b, rows: b.reshape(NDEV * rows, D)
    W = unpack(_all_gather(pack(keys_early), "ag_weights"), keys_early, full_rows)

    WA = ada_w.shape[2]
    c_all = _all_gather(c, "ag_c").reshape(NDEV * B, D)
    bias = lax.dynamic_slice_in_dim(ada_b, me * WA, WA, axis=1).reshape(L, 1, WA)
    mod_part = _ada_fwd(c_all, ada_w, bias, "ada_fwd")
    mod_all = _all_gather(mod_part.reshape(L * NDEV * B, WA), "ag_mod")
    mod_all = mod_all.reshape(NDEV, L, NDEV * B, WA).transpose(1, 2, 0, 3).reshape(L, NDEV * B, NDEV * WA)
    mod = lax.dynamic_slice_in_dim(mod_all, me * B, B, axis=1)
    mod = mod.reshape(L, B, 6, 1, D)
    sh1, sc1, g1, sh2, sc2, g2 = [mod[:, :, k] for k in range(6)]

    half = ROT // 2
    inv_freq = jnp.power(jnp.float32(ROPE_THETA), -jnp.arange(half, dtype=F32) * 2.0 / ROT)
    ang = positions.reshape(T, 1).astype(F32) * inv_freq[None, :]
    cos, sin = jnp.cos(ang), jnp.sin(ang)
    ones = jnp.ones((T, HEAD - ROT), F32)
    zeros = jnp.zeros((T, HEAD - ROT), F32)
    z8 = jnp.zeros((T, half), F32)
    cs = jnp.tile(jnp.concatenate([cos, cos, ones], axis=1), (1, 2))
    s1 = jnp.tile(jnp.concatenate([-sin, z8, zeros], axis=1), (1, 2))
    s2 = jnp.tile(jnp.concatenate([z8, sin, zeros], axis=1), (1, 2))

    saved = []
    xc = xt
    h1 = _norm_mod(xc, norm1_g[0:1], sc1[0], sh1[0], S, "norm1_0")
    for l in range(L):
        j = l // 2
        sv = dict(x_in=xc, h1=h1)
        if l % 2 == 0:
            qkv = _mm_nt(h1, W["qkv_a", j], F32, f"qkv_a_{l}")
            qg = jnp.tile(q_norm_a[j:j + 1], (1, 2))
            kg = jnp.tile(k_norm_a[j:j + 1], (1, 2))
            qn, kd, vd = _qk_prep(qkv, cs, s1, s2, qg, kg, f"qk_prep_{l}")
            sink2 = jnp.repeat(sinks_a[j].reshape(-1, 2), HEAD, axis=1).reshape(-1, 1, LANES)
            if l == 0:
                attn, lse, mid = _swa_fwd(qn, kd, vd, sink2, B, f"swa_fwd_{l}", gather=pack(keys_mid))
                W.update(unpack(mid, keys_mid, full_rows))
            else:
                attn, lse = _swa_fwd(qn, kd, vd, sink2, B, f"swa_fwd_{l}")
            sv.update(qkv=qkv, qg=qg, kg=kg, qn=qn, kd=kd, vd=vd, sink2=sink2, lse=lse)
            wo = W["wo_a", j]
        else:
            qkv = _mm_nt(h1, W["qkv_b", j], BF16, f"qkv_b_{l}")
            if l == first_b:
                attn, tot, late = _sb_fwd(qkv, B, f"sb_fwd_{l}", gather=pack(keys_late))
                W.update(unpack(late, keys_late, full_rows))
            else:
                attn, tot = _sb_fwd(qkv, B, f"sb_fwd_{l}")
            sv.update(qkv=qkv, tot=tot)
            wo = W["wo_b", j]
        y1, xm, h2 = _mm_res(attn, wo, xc, g1[l], S, f"attn_out_{l}", norm=(norm2_g[l:l + 1], sc2[l], sh2[l]))
        gate, up, act = _swiglu_fwd(h2, W["gate", l], W["up", l], f"swiglu_fwd_{l}")
        if l + 1 < L:
            y2, xc, h1 = _mm_res(act, W["down", l], xm, g2[l], S, f"mlp_out_{l}",
                                 norm=(norm1_g[l + 1:l + 2], sc1[l + 1], sh1[l + 1]))
        else:
            y2, xc = _mm_res(act, W["down", l], xm, g2[l], S, f"mlp_out_{l}")
        sv.update(attn=attn, y1=y1, x_mid=xm, h2=h2, gate=gate, up=up, act=act, y2=y2)
        saved.append(sv)

    dx, loss_tile = _loss_head(xc, loss_target.reshape(T, D), "loss_head")

    G = {}
    pack_grads = lambda keys: jnp.concatenate([G[k].reshape(NDEV, G[k].shape[0] // NDEV, D) for k in keys], axis=1)
    keys_hi = [k for l in range(first_b + 1, L) for k in layer_keys(l)] + layer_keys(first_b)[1:]
    keys_mlp0_g = [("qkv_b", 0), ("down", 0)]
    keys_mid_g = [("wo_a", 0), ("gate", 0), ("up", 0)]
    keys_lo = layer_keys(0)[:1]
    received_hi = received_mid = received_mlp0 = None
    dmod = [None] * L
    dn1, dn2 = [None] * L, [None] * L
    dqg, dkg, dsink = [None] * NA, [None] * NA, [None] * NA
    dy2, dg2 = _gate_bwd(dx, saved[L - 1]["y2"], g2[L - 1], S, "gate2_bwd_top")
    for l in reversed(range(L)):
        j = l // 2
        mix = "a" if l % 2 == 0 else "b"
        sv = saved[l]
        dgate, dup = _swiglu_bwd(dy2, W["down", l], sv["gate"], sv["up"], f"swiglu_bwd_{l}")
        G["down", l] = _mm_tn(sv["act"], dy2, f"dw_down_{l}")
        G["gate", l] = _mm_tn(dgate, sv["h2"], f"dw_gate_{l}")
        G["up", l] = _mm_tn(dup, sv["h2"], f"dw_up_{l}")
        n2 = _norm_mod_bwd(sv["x_mid"], [(dgate, W["gate", l]), (dup, W["up", l])], dx, norm2_g[l:l + 1], sc2[l], S,
                           f"norm2_bwd_{l}", below=(sv["y1"], g1[l]), exchange=pack_grads(keys_mlp0_g) if l == 0 else None)
        dxm, dsh2, dsc2, dn2[l], dy1, dg1 = n2[:6]
        if l == 0:
            received_mlp0 = n2[6]
        dattn = _mm_nt(dy1, W["wo_" + mix, j], BF16, f"dattn_{l}")
        G["wo_" + mix, j] = _mm_tn(sv["attn"], dy1, f"dw_o_{l}")
        if l % 2 == 0:
            swa_args = (sv["qn"], sv["kd"], sv["vd"], sv["sink2"], dattn, sv["lse"], B, f"swa_bwd_{l}")
            if l == 0:
                dq, dkc, dkp, dvc, dvp, dsink[j], received_mid = _swa_bwd(*swa_args, exchange=pack_grads(keys_mid_g))
            else:
                dq, dkc, dkp, dvc, dvp, dsink[j] = _swa_bwd(*swa_args)
            dqkv, dqg[j], dkg[j] = _qk_prep_bwd(sv["qkv"], cs, s1, s2, sv["qg"], sv["kg"], dq, dkc, dkp, dvc, dvp, B,
                                                f"qk_prep_bwd_{l}")
        else:
            nqb = sv["qkv"].shape[1] // 3
            sb_args = (sv["qkv"], sv["qkv"][:, :nqb].T, dattn, dattn.T, sv["tot"], B, f"sb_bwd_{l}")
            if l == first_b and keys_hi:
                dq, _, _, dk, dv, received_hi = _sb_bwd(*sb_args, exchange=pack_grads(keys_hi))
            else:
                dq, _, _, dk, dv = _sb_bwd(*sb_args)
        wt = W["qkv_" + mix, j]
        if l % 2 == 0:
            dh1_pairs = [(dqkv, wt)]
            G["qkv_a", j] = _mm_tn(dqkv, sv["h1"], f"dw_qkv_{l}")
        else:
            parts = [dq, dk, dv]
            dh1_pairs = [(part, wt[k * nqb:(k + 1) * nqb]) for k, part in enumerate(parts)]
            G["qkv_b", j] = jnp.concatenate([_mm_tn(part, sv["h1"], f"dw_qkv_{l}_{k}") for k, part in enumerate(parts)], axis=0)
        n1_args = (sv["x_in"], dh1_pairs, dxm, norm1_g[l:l + 1], sc1[l], S, f"norm1_bwd_{l}")
        dmod_l = [None, None, dg1, dsh2, dsc2, dg2]
        if l > 0:
            dx, dmod_l[0], dmod_l[1], dn1[l], dy2, dg2 = _norm_mod_bwd(*n1_args, below=(saved[l - 1]["y2"], g2[l - 1]))
        else:
            dx, dmod_l[0], dmod_l[1], dn1[l] = _norm_mod_bwd(*n1_args)
        dmod[l] = jnp.concatenate(dmod_l, axis=1)
    grad_x = dx.reshape(B, S, D)

    ndm = L * 6
    dmod_rows = jnp.stack(dmod, axis=1).reshape(B * ndm, D)
    misc = jnp.concatenate(
        [jnp.concatenate(dn1, axis=0).reshape(B * L, D), jnp.concatenate(dn2, axis=0).reshape(B * L, D),
         _pad_rows(jnp.concatenate([jnp.pad(r, ((0, 0), (0, D - LANES))) for r in dqg + dkg]
                                   + [jnp.pad(r[:, 0, ::HEAD].reshape(1, -1), ((0, 0), (0, D - 2 * r.shape[0]))) for r in dsink]
                                   + [jnp.pad(loss_tile[0:1, 0:1], ((0, 0), (0, D - 1)))], axis=0), 8)], axis=0)
    nmisc = misc.shape[0]
    small = _all_gather(jnp.concatenate([dmod_rows, _pad_rows(misc, -(-nmisc // 8) * 8)], axis=0), "ag_small")
    dmod_all = small[:, :B * ndm].reshape(NDEV * B, ndm, D)
    g_ada_b = _sum_leading(dmod_all, "sum_dmod").reshape(L, 6 * D)
    misc_sum = _sum_leading(small[:, B * ndm:], "sum_misc")
    g_n1 = misc_sum[0:B * L].reshape(L, B, D)
    g_n2 = misc_sum[B * L:2 * B * L].reshape(L, B, D)
    g_norm1 = _sum_leading(g_n1.transpose(1, 0, 2), "sum_n1")
    g_norm2 = _sum_leading(g_n2.transpose(1, 0, 2), "sum_n2")
    o = 2 * B * L
    g_qn = misc_sum[o:o + NA, :HEAD]
    g_kn = misc_sum[o + NA:o + 2 * NA, :HEAD]
    nsink = sinks_a.shape[1]
    g_sink = misc_sum[o + 2 * NA:o + 3 * NA, :nsink]
    loss = misc_sum[o + 3 * NA, 0]

    dmod_loc = lax.dynamic_slice_in_dim(dmod_all.reshape(NDEV * B, L, 6 * D), me * WA, WA, axis=2)
    g_ada_w = _ada_bwd(c_all, dmod_loc.transpose(1, 0, 2), "ada_bwd")

    shard = unpack(_sum_leading(_exchange(pack_grads(keys_lo), "grad_exchange"), "grad_sum"), keys_lo, lambda b, rows: b)
    shard.update(unpack(_sum_leading(received_mid, "grad_sum_mid"), keys_mid_g, lambda b, rows: b))
    shard.update(unpack(_sum_leading(received_mlp0, "grad_sum_mlp0"), keys_mlp0_g, lambda b, rows: b))
    if received_hi is not None:
        shard.update(unpack(_sum_leading(received_hi, "grad_sum_hi"), keys_hi, lambda b, rows: b))

    def stacked(kind, n):
        return jnp.stack([shard[kind, i].T if kind in col_sharded else shard[kind, i] for i in range(n)])

    gw_qkv_a, gw_qkv_b, gw_gate, gw_up = stacked("qkv_a", NA), stacked("qkv_b", NB_), stacked("gate", L), stacked("up", L)
    gw_wo_a, gw_wo_b, gw_down = stacked("wo_a", NA), stacked("wo_b", NB_), stacked("down", L)

    grads = [g_ada_w, g_ada_b, g_norm1, g_norm2, gw_qkv_a, g_qn, g_kn, g_sink, gw_wo_a, gw_qkv_b, gw_wo_b,
             gw_gate, gw_up, gw_down]
    ws = [ada_w, ada_b, norm1_g, norm2_g, wqkv_a, q_norm_a, k_norm_a, sinks_a, wo_a, wqkv_b, wo_b, w_gate, w_up, w_down]
    ms = [m_ada_w, m_ada_b, m_norm1_g, m_norm2_g, m_wqkv_a, m_q_norm_a, m_k_norm_a, m_sinks_a, m_wo_a, m_wqkv_b,
          m_wo_b, m_w_gate, m_w_up, m_w_down]
    vs = [v_ada_w, v_ada_b, v_norm1_g, v_norm2_g, v_wqkv_a, v_q_norm_a, v_k_norm_a, v_sinks_a, v_wo_a, v_wqkv_b,
          v_wo_b, v_w_gate, v_w_up, v_w_down]
    deltas, new_m, new_v = [], [], []
    for k, (w, g, m, v) in enumerate(zip(ws, grads, ms, vs)):
        g = g.reshape(w.shape)
        d, mn, vn = _adamw(w, g, m, v, f"adamw_{k}")
        grads[k] = g
        deltas.append(d)
        new_m.append(mn)
        new_v.append(vn)
    return (loss, grad_x, *grads, *deltas, *new_m, *new_v)
```

```python
import functools
import math

import jax
import jax.numpy as jnp
from jax import lax
from jax.experimental import pallas as pl
from jax.experimental.pallas import tpu as pltpu

F32 = jnp.float32
BF16 = jnp.bfloat16
NDEV = 8
HEAD = 64
BLK = 128
LANES = 128
EPS = 1e-6
ROT = HEAD // 4
ROPE_THETA = 500000.0
SCALE = HEAD ** -0.5
LOG2E = math.log2(math.e)
NEG = -1e30
VMEM_LIMIT = 56 * 1024 * 1024
MESH = pl.DeviceIdType.MESH
HIGH = lax.Precision.HIGHEST

ADAM_LR = 0.001
ADAM_B1 = 0.9
ADAM_B2 = 0.999
ADAM_EPS = 1e-08
ADAM_WD = 0.01
ADAM_STEP = 10


def _params(*sem):
    return pltpu.CompilerParams(dimension_semantics=sem, vmem_limit_bytes=VMEM_LIMIT)


def _pick(n, cap, mult):
    if n <= cap:
        return n
    best = None
    for t in range(mult, cap + 1, mult):
        if n % t == 0:
            best = t
    assert best is not None, (n, cap, mult)
    return best


def _dot(a, b, dims, precision=None):
    return lax.dot_general(a, b, (dims, ((), ())), preferred_element_type=F32, precision=precision)


NN = ((1,), (0,))
NT = ((1,), (1,))
TN = ((0,), (0,))


def _all_gather(x, name):
    m, n = x.shape

    def body(x_ref, out_ref, send_sems, recv_sems, local_sem):
        ix, iy, ic = lax.axis_index("x"), lax.axis_index("y"), lax.axis_index("c")
        me, sibling = (ix, iy, ic), (ix, iy, 1 - ic)
        chips = [(1 - ix, iy), (ix, 1 - iy), (1 - ix, 1 - iy)]

        def slab(px, py, pc):
            return out_ref.at[4 * px + 2 * py + pc]

        def copy(k, block, to, src=None):
            return pltpu.make_async_remote_copy(
                src_ref=slab(*block) if src is None else src, dst_ref=slab(*block),
                send_sem=send_sems.at[k], recv_sem=recv_sems.at[k], device_id=to, device_id_type=MESH)

        mine = pltpu.make_async_copy(x_ref, slab(*me), local_sem)
        mine.start()
        first = [copy(0, me, sibling, src=x_ref)]
        first += [copy(1 + j, me, (*chip, ic), src=x_ref) for j, chip in enumerate(chips)]
        for cp in first:
            cp.start()
        passed = [copy(4 + j, (*chip, ic), sibling) for j, chip in enumerate(chips)]
        for j, chip in enumerate(chips):
            copy(1 + j, (*chip, ic), me).wait_recv()
            passed[j].start()
        copy(0, sibling, me).wait_recv()
        for j, chip in enumerate(chips):
            copy(4 + j, (*chip, 1 - ic), me).wait_recv()
        for cp in first + passed:
            cp.wait_send()
        mine.wait()

    return pl.pallas_call(
        body, name=name,
        out_shape=jax.ShapeDtypeStruct((NDEV, m, n), x.dtype),
        in_specs=[pl.BlockSpec(memory_space=pl.ANY)],
        out_specs=pl.BlockSpec(memory_space=pl.ANY),
        scratch_shapes=[pltpu.SemaphoreType.DMA((7,)), pltpu.SemaphoreType.DMA((7,)), pltpu.SemaphoreType.DMA(())],
    )(x)


COMM_SEMS = [pltpu.SemaphoreType.DMA((NDEV - 1,)), pltpu.SemaphoreType.DMA((NDEV - 1,)), pltpu.SemaphoreType.DMA(())]
HBM_SPEC = pl.BlockSpec(memory_space=pl.ANY)


def _direct_copies(src_ref, dst_ref, sems, scatter):
    send_sems, recv_sems, own_sem = sems
    ix, iy, ic = lax.axis_index("x"), lax.axis_index("y"), lax.axis_index("c")
    me = 4 * ix + 2 * iy + ic
    copies = [pltpu.make_async_copy(src_ref.at[me] if scatter else src_ref, dst_ref.at[me], own_sem)]
    for k in range(1, NDEV):
        px = 1 - ix if k & 4 else ix
        py = 1 - iy if k & 2 else iy
        pc = 1 - ic if k & 1 else ic
        copies.append(pltpu.make_async_remote_copy(
            src_ref=src_ref.at[4 * px + 2 * py + pc] if scatter else src_ref, dst_ref=dst_ref.at[me],
            send_sem=send_sems.at[k - 1], recv_sem=recv_sems.at[k - 1],
            device_id=(px, py, pc), device_id_type=MESH))
    return copies


def _exchange(p, name):
    def body(p_ref, r_ref, *sems):
        copies = _direct_copies(p_ref, r_ref, sems, True)
        for cp in copies:
            cp.start()
        for cp in copies:
            cp.wait()

    return pl.pallas_call(
        body, name=name,
        out_shape=jax.ShapeDtypeStruct(p.shape, p.dtype),
        in_specs=[HBM_SPEC], out_specs=HBM_SPEC, scratch_shapes=COMM_SEMS,
    )(p)


def _call_behind(body, name, grid, in_specs, out_specs, out_shape, args, payload=None, scatter=False):
    params = _params(*["arbitrary"] * len(grid))
    if payload is None:
        return pl.pallas_call(body, name=name, grid=grid, in_specs=in_specs, out_specs=out_specs, out_shape=out_shape,
                              compiler_params=params)(*args)
    n_in, n_out = len(in_specs), len(out_specs)

    def edge(first):
        ids = [pl.program_id(a) for a in range(len(grid))]
        return functools.reduce(lambda u, v: u & v, [i == (0 if first else d - 1) for i, d in zip(ids, grid)])

    def wrapped(*refs):
        x_ref, r_ref, sems = refs[n_in], refs[n_in + 1 + n_out], refs[n_in + n_out + 2:]

        @pl.when(edge(True))
        def _():
            for cp in _direct_copies(x_ref, r_ref, sems, scatter):
                cp.start()
        body(*refs[:n_in], *refs[n_in + 1:n_in + 1 + n_out])

        @pl.when(edge(False))
        def _():
            for cp in _direct_copies(x_ref, r_ref, sems, scatter):
                cp.wait()

    arrived = jax.ShapeDtypeStruct(payload.shape if scatter else (NDEV,) + payload.shape, payload.dtype)
    return pl.pallas_call(
        wrapped, name=name, grid=grid, in_specs=list(in_specs) + [HBM_SPEC], out_specs=list(out_specs) + [HBM_SPEC],
        out_shape=list(out_shape) + [arrived], scratch_shapes=COMM_SEMS, compiler_params=params,
    )(*args, payload)


def _sum_leading(r, name):
    k, m, n = r.shape
    mult = 8 * (4 // r.dtype.itemsize)
    tm = _pick(m, max(mult, (4 * 1024 * 1024) // (k * n * r.dtype.itemsize) // mult * mult), mult)

    def body(r_ref, o_ref):
        acc = r_ref[0].astype(F32)
        for s in range(1, k):
            acc = acc + r_ref[s].astype(F32)
        o_ref[...] = acc

    return pl.pallas_call(
        body, name=name, grid=(m // tm,),
        in_specs=[pl.BlockSpec((k, tm, n), lambda i: (0, i, 0))],
        out_specs=pl.BlockSpec((tm, n), lambda i: (i, 0)),
        out_shape=jax.ShapeDtypeStruct((m, n), F32),
        compiler_params=_params("parallel"),
    )(r)


def _mm_nt(a, bt, out_dtype, name):
    M, K = a.shape
    N = bt.shape[0]
    tm, tn = _pick(M, 512, 8), _pick(N, 1536, LANES)

    def body(a_ref, b_ref, o_ref):
        o_ref[...] = _dot(a_ref[...], b_ref[...], NT).astype(out_dtype)

    return pl.pallas_call(
        body, name=name, grid=(N // tn, M // tm),
        in_specs=[pl.BlockSpec((tm, K), lambda j, i: (i, 0)), pl.BlockSpec((tn, K), lambda j, i: (j, 0))],
        out_specs=pl.BlockSpec((tm, tn), lambda j, i: (i, j)),
        out_shape=jax.ShapeDtypeStruct((M, N), out_dtype),
        compiler_params=_params("parallel", "parallel"),
    )(a, bt)


def _mm_tn(a, b, name):
    M, N1 = a.shape
    N2 = b.shape[1]
    t1, tk = _pick(N1, 1536, LANES), _pick(M, 512, 8)
    nk = M // tk

    def body(a_ref, b_ref, o_ref, acc_ref):
        k = pl.program_id(1)

        @pl.when(k == 0)
        def _():
            acc_ref[...] = jnp.zeros_like(acc_ref)
        acc_ref[...] += _dot(a_ref[...], b_ref[...], TN)

        @pl.when(k == nk - 1)
        def _():
            o_ref[...] = acc_ref[...].astype(BF16)

    return pl.pallas_call(
        body, name=name, grid=(N1 // t1, nk),
        in_specs=[pl.BlockSpec((tk, t1), lambda i, k: (k, i)), pl.BlockSpec((tk, N2), lambda i, k: (k, 0))],
        out_specs=pl.BlockSpec((t1, N2), lambda i, k: (i, 0)),
        out_shape=jax.ShapeDtypeStruct((N1, N2), BF16),
        scratch_shapes=[pltpu.VMEM((t1, N2), F32)],
        compiler_params=_params("parallel", "arbitrary"),
    )(a, b)


def _norm_mod_rows(xv, gain, sc, sh):
    r = lax.rsqrt(jnp.mean(xv * xv, axis=-1, keepdims=True) + EPS)
    return ((xv * r) * gain * (1.0 + sc) + sh).astype(BF16)


def _mm_res(a, w, x, gate, S, name, norm=None):
    T, K = a.shape
    D = w.shape[1]
    tm = _pick(S, 512, 8)
    nb = S // tm

    def body(a_ref, w_ref, x_ref, g_ref, *rest):
        y = _dot(a_ref[...], w_ref[...], NN)
        xn = x_ref[...] + g_ref[0] * y
        if norm is None:
            y_ref, o_ref = rest
        else:
            gain_ref, sc_ref, sh_ref, y_ref, o_ref, h_ref = rest
            h_ref[...] = _norm_mod_rows(xn, gain_ref[...], sc_ref[0], sh_ref[0])
        y_ref[...] = y.astype(BF16)
        o_ref[...] = xn

    spec_t = pl.BlockSpec((tm, D), lambda i: (i, 0))
    spec_b = pl.BlockSpec((1, 1, D), lambda i: (i // nb, 0, 0))
    in_specs = [pl.BlockSpec((tm, K), lambda i: (i, 0)), pl.BlockSpec((K, D), lambda i: (0, 0)), spec_t, spec_b]
    out_specs = [spec_t, spec_t]
    out_shape = [jax.ShapeDtypeStruct((T, D), BF16), jax.ShapeDtypeStruct((T, D), F32)]
    args = [a, w, x, gate]
    if norm is not None:
        in_specs += [pl.BlockSpec((1, D), lambda i: (0, 0)), spec_b, spec_b]
        out_specs.append(spec_t)
        out_shape.append(jax.ShapeDtypeStruct((T, D), BF16))
        args += list(norm)
    return pl.pallas_call(
        body, name=name, grid=(T // tm,), in_specs=in_specs, out_specs=out_specs, out_shape=out_shape,
        compiler_params=_params("parallel"),
    )(*args)


def _swiglu_fwd(h, wgt, wut, name):
    T, D = h.shape
    F = wgt.shape[0]
    tm, tn = _pick(T, 512, 8), _pick(F, 1536, LANES)

    def body(h_ref, g_ref, u_ref, go_ref, uo_ref, a_ref):
        hh = h_ref[...]
        g = _dot(hh, g_ref[...], NT)
        u = _dot(hh, u_ref[...], NT)
        go_ref[...] = g.astype(BF16)
        uo_ref[...] = u.astype(BF16)
        a_ref[...] = (g * jax.nn.sigmoid(g) * u).astype(BF16)

    spec_w = pl.BlockSpec((tn, D), lambda j, i: (j, 0))
    spec_o = pl.BlockSpec((tm, tn), lambda j, i: (i, j))
    out = jax.ShapeDtypeStruct((T, F), BF16)
    return pl.pallas_call(
        body, name=name, grid=(F // tn, T // tm),
        in_specs=[pl.BlockSpec((tm, D), lambda j, i: (i, 0)), spec_w, spec_w],
        out_specs=[spec_o, spec_o, spec_o],
        out_shape=[out, out, out],
        compiler_params=_params("parallel", "parallel"),
    )(h, wgt, wut)


def _swiglu_bwd(dy, wd, gate, up, name):
    T, D = dy.shape
    F = wd.shape[0]
    tm, tn = _pick(T, 512, 8), _pick(F, 1536, LANES)

    halves = [slice(0, tn // 2), slice(tn // 2, tn)] if tn % (2 * LANES) == 0 else [slice(0, tn)]

    def body(dy_ref, w_ref, g_ref, u_ref, dg_ref, du_ref):
        das = [_dot(dy_ref[...], w_ref[sl, :], NT) for sl in halves]
        for sl, da in zip(halves, das):
            g = g_ref[:, sl].astype(F32)
            sg = jax.nn.sigmoid(g)
            t = da * sg
            du_ref[:, sl] = (t * g).astype(BF16)
            dg_ref[:, sl] = (t * u_ref[:, sl].astype(F32) * (1.0 + g * (1.0 - sg))).astype(BF16)

    spec_o = pl.BlockSpec((tm, tn), lambda j, i: (i, j))
    return pl.pallas_call(
        body, name=name, grid=(F // tn, T // tm),
        in_specs=[pl.BlockSpec((tm, D), lambda j, i: (i, 0)), pl.BlockSpec((tn, D), lambda j, i: (j, 0)), spec_o, spec_o],
        out_specs=[spec_o, spec_o],
        out_shape=[jax.ShapeDtypeStruct((T, F), BF16), jax.ShapeDtypeStruct((T, F), BF16)],
        compiler_params=_params("parallel", "parallel"),
    )(dy, wd, gate, up)


def _norm_mod(x, gain, sc, sh, S, name):
    T, D = x.shape
    tm = _pick(S, 512, 8)
    nb = S // tm

    def body(x_ref, g_ref, sc_ref, sh_ref, o_ref):
        o_ref[...] = _norm_mod_rows(x_ref[...], g_ref[...], sc_ref[0], sh_ref[0])

    spec_b = pl.BlockSpec((1, 1, D), lambda i: (i // nb, 0, 0))
    return pl.pallas_call(
        body, name=name, grid=(T // tm,),
        in_specs=[pl.BlockSpec((tm, D), lambda i: (i, 0)), pl.BlockSpec((1, D), lambda i: (0, 0)), spec_b, spec_b],
        out_specs=pl.BlockSpec((tm, D), lambda i: (i, 0)),
        out_shape=jax.ShapeDtypeStruct((T, D), BF16),
        compiler_params=_params("parallel"),
    )(x, gain, sc, sh)


def _norm_mod_bwd(x, pairs, dres, gain, sc, S, name, below=None, exchange=None):
    T, D = x.shape
    B = T // S
    tm = _pick(S, 512, 8)
    nb = S // tm
    n_mm = 2 * len(pairs)

    def body(*refs):
        mm, (x_ref, dr_ref, g_ref, sc_ref), rest = refs[:n_mm], refs[n_mm:n_mm + 4], refs[n_mm + 4:]
        if below is None:
            o_ref, dsh_ref, dsc_ref, dg_ref = rest
            sums = [dsh_ref, dsc_ref, dg_ref]
        else:
            y_ref, gt_ref, o_ref, dsh_ref, dsc_ref, dg_ref, dy_ref, dgt_ref = rest
            sums = [dsh_ref, dsc_ref, dg_ref, dgt_ref]

        @pl.when(pl.program_id(1) == 0)
        def _():
            for ref in sums:
                ref[...] = jnp.zeros_like(ref)
        dhv = _dot(mm[0][...], mm[1][...], NN)
        for p in range(2, n_mm, 2):
            dhv = dhv + _dot(mm[p][...], mm[p + 1][...], NN)
        xv, g = x_ref[...], g_ref[...]
        r = lax.rsqrt(jnp.mean(xv * xv, axis=-1, keepdims=True) + EPS)
        xhat = xv * r
        dsh_ref[0] += jnp.sum(dhv, axis=0, keepdims=True)
        dsc_ref[0] += jnp.sum(dhv * (xhat * g), axis=0, keepdims=True)
        dn = dhv * (1.0 + sc_ref[0])
        dg_ref[0] += jnp.sum(dn * xhat, axis=0, keepdims=True)
        dxh = dn * g
        out = dr_ref[...] + r * (dxh - xhat * jnp.mean(dxh * xhat, axis=-1, keepdims=True))
        o_ref[...] = out
        if below is not None:
            dy_ref[...] = (out * gt_ref[0]).astype(BF16)
            dgt_ref[0] += jnp.sum(out * y_ref[...].astype(F32), axis=0, keepdims=True)

    spec_t = pl.BlockSpec((tm, D), lambda b, i: (b * nb + i, 0))
    spec_b = pl.BlockSpec((1, 1, D), lambda b, i: (b, 0, 0))
    red = jax.ShapeDtypeStruct((B, 1, D), F32)
    in_specs, args = [], []
    for a, w in pairs:
        K = a.shape[1]
        in_specs += [pl.BlockSpec((tm, K), lambda b, i: (b * nb + i, 0)),
                     pl.BlockSpec((K, D), lambda b, i: (0, 0), pipeline_mode=pl.Buffered(1))]
        args += [a, w]
    in_specs += [spec_t, spec_t, pl.BlockSpec((1, D), lambda b, i: (0, 0)), spec_b]
    args += [x, dres, gain, sc]
    out_specs = [spec_t, spec_b, spec_b, spec_b]
    out_shape = [jax.ShapeDtypeStruct((T, D), F32), red, red, red]
    if below is not None:
        in_specs += [spec_t, spec_b]
        out_specs += [spec_t, spec_b]
        out_shape += [jax.ShapeDtypeStruct((T, D), BF16), red]
        args += list(below)
    return _call_behind(body, name, (B, nb), in_specs, out_specs, out_shape, args, exchange, True)


def _gate_bwd(dx, y, gate, S, name):
    T, D = dx.shape
    B = T // S
    tm = _pick(S, 512, 8)
    nb = S // tm

    def body(dx_ref, y_ref, g_ref, dy_ref, dg_ref):
        @pl.when(pl.program_id(1) == 0)
        def _():
            dg_ref[...] = jnp.zeros_like(dg_ref)
        d = dx_ref[...]
        dy_ref[...] = (d * g_ref[0]).astype(BF16)
        dg_ref[0] += jnp.sum(d * y_ref[...].astype(F32), axis=0, keepdims=True)

    spec_t = pl.BlockSpec((tm, D), lambda b, i: (b * nb + i, 0))
    spec_b = pl.BlockSpec((1, 1, D), lambda b, i: (b, 0, 0))
    return pl.pallas_call(
        body, name=name, grid=(B, nb),
        in_specs=[spec_t, spec_t, spec_b],
        out_specs=[spec_t, spec_b],
        out_shape=[jax.ShapeDtypeStruct((T, D), BF16), jax.ShapeDtypeStruct((B, 1, D), F32)],
        compiler_params=_params("parallel", "arbitrary"),
    )(dx, y, gate)


def _loss_head(y, target, name):
    T, D = y.shape
    tm = _pick(T, 512, 8)

    def body(y_ref, t_ref, dy_ref, l_ref):
        @pl.when(pl.program_id(0) == 0)
        def _():
            l_ref[...] = jnp.zeros_like(l_ref)
        e = y_ref[...] - t_ref[...]
        dy_ref[...] = e * (1.0 / D)
        l_ref[...] += 0.5 * jnp.sum(jnp.mean(e * e, axis=-1, keepdims=True), axis=0, keepdims=True)

    spec = pl.BlockSpec((tm, D), lambda i: (i, 0))
    return pl.pallas_call(
        body, name=name, grid=(T // tm,),
        in_specs=[spec, spec],
        out_specs=[spec, pl.BlockSpec((8, LANES), lambda i: (0, 0))],
        out_shape=[jax.ShapeDtypeStruct((T, D), F32), jax.ShapeDtypeStruct((8, LANES), F32)],
        compiler_params=_params("arbitrary"),
    )(y, target)


def _ada_fwd(c_all, ada_w, bias, name):
    NB, D = c_all.shape
    L, _, W = ada_w.shape

    def body(c_ref, w_ref, b_ref, o_ref):
        cv = c_ref[...]
        cond = cv * jax.nn.sigmoid(cv)
        o_ref[0] = _dot(cond, w_ref[0], NN, HIGH) + b_ref[0]

    return pl.pallas_call(
        body, name=name, grid=(L,),
        in_specs=[pl.BlockSpec((NB, D), lambda l: (0, 0)), pl.BlockSpec((1, D, W), lambda l: (l, 0, 0)),
                  pl.BlockSpec((1, 1, W), lambda l: (l, 0, 0))],
        out_specs=pl.BlockSpec((1, NB, W), lambda l: (l, 0, 0)),
        out_shape=jax.ShapeDtypeStruct((L, NB, W), F32),
        compiler_params=_params("parallel"),
    )(c_all, ada_w, bias)


def _ada_bwd(c_all, dmod, name):
    NB, D = c_all.shape
    L, _, W = dmod.shape

    def body(c_ref, d_ref, o_ref):
        cv = c_ref[...]
        cond = cv * jax.nn.sigmoid(cv)
        o_ref[0] = _dot(cond, d_ref[0], TN, HIGH)

    return pl.pallas_call(
        body, name=name, grid=(L,),
        in_specs=[pl.BlockSpec((NB, D), lambda l: (0, 0)), pl.BlockSpec((1, NB, W), lambda l: (l, 0, 0))],
        out_specs=pl.BlockSpec((1, D, W), lambda l: (l, 0, 0)),
        out_shape=jax.ShapeDtypeStruct((L, D, W), F32),
        compiler_params=_params("parallel"),
    )(c_all, dmod)


def _lo_mask(shape):
    return lax.broadcasted_iota(jnp.int32, shape, len(shape) - 1) < HEAD


def _head_sum_matrix():
    r = lax.broadcasted_iota(jnp.int32, (LANES, LANES), 0) // HEAD
    c = lax.broadcasted_iota(jnp.int32, (LANES, LANES), 1) // HEAD
    return (r == c).astype(BF16)


def _head_sum(x, P):
    hi = x.astype(BF16)
    lo = (x - hi.astype(F32)).astype(BF16)
    return _dot(hi, P, NN) + _dot(lo, P, NN)


def _rope(y, cs, s1, s2):
    return y * cs + pltpu.roll(y, LANES - ROT // 2, 1) * s1 + pltpu.roll(y, ROT // 2, 1) * s2


def _rope_bwd(d, cs, s1, s2):
    return d * cs + pltpu.roll(d * s1, ROT // 2, 1) + pltpu.roll(d * s2, LANES - ROT // 2, 1)


def _qk_prep(qkv, cs, s1, s2, qg, kg, name):
    T, W = qkv.shape
    NQ = W - 2 * LANES
    tm = _pick(T, 512, 8)

    def body(x_ref, cs_ref, s1_ref, s2_ref, qg_ref, kg_ref, q_ref, k_ref, v_ref):
        P = _head_sum_matrix()
        cs_, s1_, s2_ = cs_ref[...], s1_ref[...], s2_ref[...]
        lo = _lo_mask((tm, LANES))

        def norm_rope(xv, g):
            ms = _head_sum(xv * xv, P) * (1.0 / HEAD)
            return _rope(xv * lax.rsqrt(ms + EPS) * g, cs_, s1_, s2_)

        for j in range(NQ // LANES):
            q_ref[:, j * LANES:(j + 1) * LANES] = norm_rope(x_ref[:, j * LANES:(j + 1) * LANES], qg_ref[...]).astype(BF16)
        kr = norm_rope(x_ref[:, NQ:NQ + LANES], kg_ref[...])
        ks = pltpu.roll(kr, HEAD, 1)
        k_ref[:, :LANES] = jnp.where(lo, kr, ks).astype(BF16)
        k_ref[:, LANES:] = jnp.where(lo, ks, kr).astype(BF16)
        vr = x_ref[:, NQ + LANES:]
        vs = pltpu.roll(vr, HEAD, 1)
        v_ref[:, :LANES] = jnp.where(lo, vr, vs).astype(BF16)
        v_ref[:, LANES:] = jnp.where(lo, vs, vr).astype(BF16)

    spec_t = pl.BlockSpec((tm, LANES), lambda i: (i, 0))
    spec_g = pl.BlockSpec((1, LANES), lambda i: (0, 0))
    return pl.pallas_call(
        body, name=name, grid=(T // tm,),
        in_specs=[pl.BlockSpec((tm, W), lambda i: (i, 0)), spec_t, spec_t, spec_t, spec_g, spec_g],
        out_specs=[pl.BlockSpec((tm, NQ), lambda i: (i, 0)), pl.BlockSpec((tm, 2 * LANES), lambda i: (i, 0)),
                   pl.BlockSpec((tm, 2 * LANES), lambda i: (i, 0))],
        out_shape=[jax.ShapeDtypeStruct((T, NQ), BF16), jax.ShapeDtypeStruct((T, 2 * LANES), BF16),
                   jax.ShapeDtypeStruct((T, 2 * LANES), BF16)],
        compiler_params=_params("parallel"),
    )(qkv, cs, s1, s2, qg, kg)


def _stack_heads(x2):
    lo = _lo_mask(x2.shape)
    z = jnp.zeros_like(x2)
    return jnp.concatenate([jnp.where(lo, x2, z), jnp.where(lo, z, x2)], axis=0)


def _unstack_heads(xs):
    r = xs.shape[0] // 2
    return jnp.where(_lo_mask((r, LANES)), xs[:r], xs[r:])


def _swa_valid(i):
    qo = lax.broadcasted_iota(jnp.int32, (2 * BLK, 2 * BLK), 0) % BLK
    kc_ = lax.broadcasted_iota(jnp.int32, (2 * BLK, 2 * BLK), 1)
    rel = qo + BLK - kc_
    return (rel >= 0) & (rel < BLK) & ((kc_ >= BLK) | (i > 0))


def _swa_scores(q2, kk, sink2, valid):
    qs = _stack_heads(q2) * SCALE
    s = _dot(qs, kk, NT)
    sk = jnp.concatenate([jnp.broadcast_to(sink2[:, 0:1], (BLK, 1)), jnp.broadcast_to(sink2[:, HEAD:HEAD + 1], (BLK, 1))], axis=0)
    return qs, jnp.where(valid, s, NEG), sk


def _swa_fwd(q, kd, vd, sink2, B, name, gather=None):
    T, NQ = q.shape
    NP = NQ // LANES
    nq = T // B // BLK
    NG = kd.shape[1] // LANES
    grp = NP // NG

    def body(q_ref, kp_ref, kc_ref, vp_ref, vc_ref, s_ref, o_ref, l_ref):
        valid = _swa_valid(pl.program_id(2))
        kk = jnp.concatenate([kp_ref[...], kc_ref[...]], axis=0)
        vs = _stack_heads(jnp.concatenate([vp_ref[...], vc_ref[...]], axis=0))
        sls = [slice(jj * LANES, (jj + 1) * LANES) for jj in range(grp)]
        sc = [_swa_scores(q_ref[:, sl], kk, s_ref[jj], valid) for jj, sl in enumerate(sls)]
        ms = [jnp.maximum(jnp.max(s, axis=1, keepdims=True), sk) for _, s, sk in sc]
        ps = [jnp.exp(s - m) for (_, s, _), m in zip(sc, ms)]
        ls = [jnp.sum(p, axis=1, keepdims=True) + jnp.exp(sk - m) for p, (_, _, sk), m in zip(ps, sc, ms)]
        ps = [(p * (1.0 / l)).astype(BF16) for p, l in zip(ps, ls)]
        os_ = [_dot(jnp.concatenate([p[:BLK], p[BLK:]], axis=1), vs, NN) for p in ps]
        for sl, o, m, l in zip(sls, os_, ms, ls):
            o_ref[:, sl] = o.astype(BF16)
            l_ref[:, sl] = _unstack_heads(jnp.broadcast_to(m + jnp.log(l), (2 * BLK, LANES)))

    spec_q = pl.BlockSpec((BLK, grp * LANES), lambda b, g, i: (b * nq + i, g))
    spec_p = pl.BlockSpec((BLK, LANES), lambda b, g, i: (b * nq + jnp.maximum(i - 1, 0), g))
    spec_c = pl.BlockSpec((BLK, LANES), lambda b, g, i: (b * nq + i, g))
    in_specs = [spec_q, spec_p, spec_c, spec_p, spec_c, pl.BlockSpec((grp, 1, LANES), lambda b, g, i: (g, 0, 0))]
    out_shape = [jax.ShapeDtypeStruct((T, NQ), BF16), jax.ShapeDtypeStruct((T, NQ), F32)]
    return _call_behind(body, name, (B, NG, nq), in_specs, [spec_q, spec_q], out_shape, [q, kd, kd, vd, vd, sink2], gather, False)


def _swa_bwd(q, kd, vd, sink2, do, lse, B, name, exchange=None):
    T, NQ = q.shape
    NP = NQ // LANES
    nq = T // B // BLK
    NG = kd.shape[1] // LANES
    grp = NP // NG

    def body(q_ref, kp_ref, kc_ref, vp_ref, vc_ref, s_ref, do_ref, l_ref,
             dq_ref, dkc_ref, dkp_ref, dvc_ref, dvp_ref, ds_ref):
        b, i = pl.program_id(1), pl.program_id(2)

        @pl.when((b == 0) & (i == 0))
        def _():
            ds_ref[...] = jnp.zeros_like(ds_ref)
        valid = _swa_valid(i)
        kk = jnp.concatenate([kp_ref[...], kc_ref[...]], axis=0)
        vv = jnp.concatenate([vp_ref[...], vc_ref[...]], axis=0)
        sls = [slice(jj * LANES, (jj + 1) * LANES) for jj in range(grp)]
        sc = [_swa_scores(q_ref[:, sl], kk, s_ref[jj], valid) for jj, sl in enumerate(sls)]
        dos = [_stack_heads(do_ref[:, sl]) for sl in sls]
        dps = [_dot(d, vv, NT) for d in dos]
        lses = [jnp.concatenate([l_ref[:, sl][:, 0:1], l_ref[:, sl][:, HEAD:HEAD + 1]], axis=0) for sl in sls]
        ps = [jnp.exp(s - lse) for (_, s, _), lse in zip(sc, lses)]
        deltas = [jnp.sum(p * dp, axis=1, keepdims=True) for p, dp in zip(ps, dps)]
        dscs = [(p * (dp - delta)).astype(BF16) for p, dp, delta in zip(ps, dps, deltas)]
        dqs = [_dot(dsc, kk, NN) for dsc in dscs]
        dk = jnp.zeros((2 * BLK, LANES), F32)
        dv = jnp.zeros((2 * BLK, LANES), F32)
        for jj, sl in enumerate(sls):
            dsk = -jnp.exp(sc[jj][2] - lses[jj]) * deltas[jj]
            dsk_lo = jnp.sum(dsk[:BLK], axis=0, keepdims=True)
            dsk_hi = jnp.sum(dsk[BLK:], axis=0, keepdims=True)
            ds_ref[jj] += jnp.where(_lo_mask((1, LANES)), dsk_lo, dsk_hi)
            dq_ref[:, sl] = _unstack_heads(dqs[jj]) * SCALE
            dk = dk + _dot(dscs[jj], sc[jj][0], TN)
            dv = dv + _dot(ps[jj].astype(BF16), dos[jj], TN)
        dkp_ref[...] = dk[:BLK]
        dkc_ref[...] = dk[BLK:]
        dvp_ref[...] = dv[:BLK]
        dvc_ref[...] = dv[BLK:]

    spec_q = pl.BlockSpec((BLK, grp * LANES), lambda g, b, i: (b * nq + i, g))
    spec_p = pl.BlockSpec((BLK, LANES), lambda g, b, i: (b * nq + jnp.maximum(i - 1, 0), g))
    spec_c = pl.BlockSpec((BLK, LANES), lambda g, b, i: (b * nq + i, g))
    spec_s = pl.BlockSpec((grp, 1, LANES), lambda g, b, i: (g, 0, 0))
    kv = jax.ShapeDtypeStruct((T, NG * LANES), F32)
    in_specs = [spec_q, spec_p, spec_c, spec_p, spec_c, spec_s, spec_q, spec_q]
    out_specs = [spec_q, spec_c, spec_c, spec_c, spec_c, spec_s]
    out_shape = [jax.ShapeDtypeStruct((T, NQ), F32), kv, kv, kv, kv, jax.ShapeDtypeStruct((NP, 1, LANES), F32)]
    return _call_behind(body, name, (NG, B, nq), in_specs, out_specs, out_shape, [q, kd, kd, vd, vd, sink2, do, lse],
                        exchange, True)


def _qk_prep_bwd(qkv, cs, s1, s2, qg, kg, dq, dkc, dkp, dvc, dvp, B, name):
    T, W = qkv.shape
    NQ = W - 2 * LANES
    NP = NQ // LANES
    nq = T // B // BLK

    def body(x_ref, cs_ref, s1_ref, s2_ref, qg_ref, kg_ref, dq_ref, dkc_ref, dkp_ref, dvc_ref, dvp_ref,
             o_ref, dqg_ref, dkg_ref):
        b, i = pl.program_id(0), pl.program_id(1)

        @pl.when((b == 0) & (i == 0))
        def _():
            dqg_ref[...] = jnp.zeros_like(dqg_ref)
            dkg_ref[...] = jnp.zeros_like(dkg_ref)
        P = _head_sum_matrix()
        cs_, s1_, s2_ = cs_ref[...], s1_ref[...], s2_ref[...]
        lo = _lo_mask((BLK, LANES))
        has_next = (i + 1 < nq).astype(F32)

        def norm_rope_bwd(xv, g, d):
            du = _rope_bwd(d, cs_, s1_, s2_)
            r = lax.rsqrt(_head_sum(xv * xv, P) * (1.0 / HEAD) + EPS)
            xhat = xv * r
            dgain = jnp.sum(du * xhat, axis=0, keepdims=True)
            uu = du * g
            dx = r * (uu - xhat * (_head_sum(uu * xhat, P) * (1.0 / HEAD)))
            return dx, dgain + pltpu.roll(dgain, HEAD, 1)

        dqg = jnp.zeros((1, LANES), F32)
        for j in range(NP):
            sl = slice(j * LANES, (j + 1) * LANES)
            dx, dg = norm_rope_bwd(x_ref[:, sl], qg_ref[...], dq_ref[:, sl])
            o_ref[:, sl] = dx.astype(BF16)
            dqg = dqg + dg
        dqg_ref[...] += dqg

        def fold(c_ref, p_ref, g):
            sl = slice(g * LANES, (g + 1) * LANES)
            t = c_ref[:, sl] + has_next * p_ref[:, sl]
            return t + pltpu.roll(t, HEAD, 1)

        dk = jnp.where(lo, fold(dkc_ref, dkp_ref, 0), fold(dkc_ref, dkp_ref, 1))
        dx, dg = norm_rope_bwd(x_ref[:, NQ:NQ + LANES], kg_ref[...], dk)
        o_ref[:, NQ:NQ + LANES] = dx.astype(BF16)
        dkg_ref[...] += dg
        dv = jnp.where(lo, fold(dvc_ref, dvp_ref, 0), fold(dvc_ref, dvp_ref, 1))
        o_ref[:, NQ + LANES:] = dv.astype(BF16)

    spec_t = pl.BlockSpec((BLK, LANES), lambda b, i: (b * nq + i, 0))
    spec_g = pl.BlockSpec((1, LANES), lambda b, i: (0, 0))
    spec_c = pl.BlockSpec((BLK, 2 * LANES), lambda b, i: (b * nq + i, 0))
    spec_n = pl.BlockSpec((BLK, 2 * LANES), lambda b, i: (b * nq + jnp.minimum(i + 1, nq - 1), 0))
    row = jax.ShapeDtypeStruct((1, LANES), F32)
    return pl.pallas_call(
        body, name=name, grid=(B, nq),
        in_specs=[pl.BlockSpec((BLK, W), lambda b, i: (b * nq + i, 0)), spec_t, spec_t, spec_t, spec_g, spec_g,
                  pl.BlockSpec((BLK, NQ), lambda b, i: (b * nq + i, 0)), spec_c, spec_n, spec_c, spec_n],
        out_specs=[pl.BlockSpec((BLK, W), lambda b, i: (b * nq + i, 0)), spec_g, spec_g],
        out_shape=[jax.ShapeDtypeStruct((T, W), BF16), row, row],
        compiler_params=_params("arbitrary", "arbitrary"),
    )(qkv, cs, s1, s2, qg, kg, dq, dkc, dkp, dvc, dvp)


SB_TILE = 256
SB_UNROLL = 4
SB_UNROLL_BWD = 2


def _split_heads(x2, scale=None):
    lo = _lo_mask(x2.shape)
    z = jnp.zeros_like(x2)
    if scale is not None:
        x2 = x2 * scale
    return jnp.where(lo, x2, z), jnp.where(lo, z, x2)


def _sb_terms(qh, kj, diagonal):
    z = _dot(qh, kj, NT)
    e = jnp.exp2(jnp.abs(z) * (-LOG2E))
    lb = jnp.minimum(z, 0.0) - jnp.log(1.0 + e)
    L = lb - z
    if not diagonal:
        return lb, L, None, z, e
    strict = lax.broadcasted_iota(jnp.int32, z.shape, 1) < lax.broadcasted_iota(jnp.int32, z.shape, 0)
    return lb, jnp.where(strict, L, 0.0), strict, z, e


def _tri(n, cmp):
    r = lax.broadcasted_iota(jnp.int32, (n, n), 0)
    c = lax.broadcasted_iota(jnp.int32, (n, n), 1)
    return cmp(r, c).astype(BF16)


def _by_value(r, fns, carry):
    if len(fns) == 1:
        return fns[0](carry)
    half = len(fns) // 2
    return lax.cond(r < half, lambda cr: _by_value(r, fns[:half], cr), lambda cr: _by_value(r - half, fns[half:], cr), carry)


def _sb_fwd(qkv, B, name, gather=None):
    T, W = qkv.shape
    NQ = W // 3
    NP = NQ // LANES
    S = T // B
    tq = min(SB_TILE, S)
    nq = S // tq
    grid = (B, NP, nq)

    def body(q_ref, k_ref, v_ref, o_ref, t_ref):
        i = pl.program_id(2)
        qh = _split_heads(q_ref[...], SCALE)
        U = _tri(tq, lambda r, c: r > c)

        def sweep(tiles, cs, acc):
            chains = [(t, h) for t in range(len(tiles)) for h in range(2)]
            rows = [pl.ds(pl.multiple_of(j * tq, tq), tq) for j, _ in tiles]
            ks = [k_ref[r, :] for r in rows]
            vs = [_split_heads(v_ref[r, :]) for r in rows]
            terms = {(t, h): _sb_terms(qh[h], ks[t], tiles[t][1]) for t, h in chains}
            carry = {}
            for h in range(2):
                c = cs[h]
                for t in range(len(tiles)):
                    carry[t, h] = c
                    c = c + jnp.sum(terms[t, h][1], axis=1, keepdims=True)
                cs = cs[:h] + (c,) + cs[h + 1:]
            cum = {ch: _dot(terms[ch][1].astype(BF16), U, NN) for ch in chains}
            for ch in chains:
                a = jnp.exp(terms[ch][0] + (cum[ch] + carry[ch]))
                if tiles[ch[0]][1]:
                    a = jnp.where(terms[ch][2], a, 0.0)
                acc = acc + _dot(a.astype(BF16), vs[ch[0]][ch[1]], NN)
            return cs, acc

        zero = jnp.zeros((tq, 1), F32)
        rem = i % SB_UNROLL
        heads = [lambda cr, k=k: sweep([(i, True)] + [(i - 1 - t, False) for t in range(k)], *cr) for k in range(SB_UNROLL)]
        carry = _by_value(rem, heads, ((zero, zero), jnp.zeros((tq, LANES), F32)))
        step = lambda n, cr: sweep([(i - 1 - rem - SB_UNROLL * n - t, False) for t in range(SB_UNROLL)], *cr)
        cs, acc = lax.fori_loop(0, i // SB_UNROLL, step, carry)
        o_ref[...] = acc.astype(BF16)
        t_ref[...] = jnp.where(_lo_mask((tq, LANES)), cs[0], cs[1])

    spec_q = pl.BlockSpec((tq, LANES), lambda b, p, i: (b * nq + i, p))
    in_specs = [spec_q, pl.BlockSpec((S, LANES), lambda b, p, i: (b, NP + p)),
                pl.BlockSpec((S, LANES), lambda b, p, i: (b, 2 * NP + p))]
    out_shape = [jax.ShapeDtypeStruct((T, NQ), BF16), jax.ShapeDtypeStruct((T, NQ), F32)]
    return _call_behind(body, name, grid, in_specs, [spec_q, spec_q], out_shape, [qkv, qkv, qkv], gather, False)


def _sb_bwd(qkv, do, tot, B, name, exchange=None):
    T, W = qkv.shape
    NQ = W // 3
    NP = NQ // LANES
    S = T // B
    tq = min(SB_TILE, S)
    nq = S // tq
    grid = (B, NP, nq)

    def body(q_ref, k_ref, v_ref, do_ref, t_ref, dq_ref, dk_ref, dv_ref, dkm_ref, dvm_ref):
        i = pl.program_id(2)

        @pl.when(i == 0)
        def _():
            dk_ref[...] = jnp.zeros_like(dk_ref)
            dv_ref[...] = jnp.zeros_like(dv_ref)
        qh = _split_heads(q_ref[...], SCALE)
        doh = _split_heads(do_ref[...])
        top = lax.broadcasted_iota(jnp.int32, (LANES, tq), 0) < HEAD
        zt = jnp.zeros((LANES, tq), BF16)
        qt = (q_ref[...].astype(F32) * SCALE).T.astype(BF16)
        dot_ = do_ref[...].astype(F32).T.astype(BF16)
        qth = (jnp.where(top, qt, zt), jnp.where(top, zt, qt))
        doth = (jnp.where(top, dot_, zt), jnp.where(top, zt, dot_))
        tt = t_ref[...]
        tot = (tt[:, 0:1], tt[:, HEAD:HEAD + 1])
        Urev = _tri(tq, lambda r, c: r > c)
        Uexc = _tri(tq, lambda r, c: r < c)

        def sweep(tiles, carry):
            nt = len(tiles)
            chains = [(t, h) for t in range(nt) for h in range(2)]
            rows = [pl.ds(pl.multiple_of(j * tq, tq), tq) for j, _ in tiles]
            ks = [k_ref[r, :] for r in rows]
            vs = [v_ref[r, :] for r in rows]
            terms = {(t, h): _sb_terms(qh[h], ks[t], tiles[t][1]) for t, h in chains}
            cc = [carry[h][0] for h in range(2)]
            later = {}
            for t, h in chains:
                cc[h] = cc[h] + jnp.sum(terms[t, h][1], axis=1, keepdims=True)
                later[t, h] = tot[h] - cc[h]
            cum = {ch: _dot(terms[ch][1].astype(BF16), Urev, NN) for ch in chains}
            da = {(t, h): _dot(doh[h], vs[t], NT) for t, h in chains}
            a, g, before = {}, {}, {}
            cg = [carry[h][1] for h in range(2)]
            for ch in chains:
                a[ch] = jnp.exp(terms[ch][0] + (cum[ch] + later[ch]))
                if tiles[ch[0]][1]:
                    a[ch] = jnp.where(terms[ch][2], a[ch], 0.0)
                g[ch] = a[ch] * da[ch]
                before[ch] = cg[ch[1]]
                cg[ch[1]] = cg[ch[1]] + jnp.sum(g[ch], axis=1, keepdims=True)
            G = {ch: _dot(g[ch].astype(BF16), Uexc, NN) for ch in chains}
            dz = {}
            for ch in chains:
                d = g[ch] - jnp.exp(terms[ch][0]) * (g[ch] + (G[ch] + before[ch]))
                if tiles[ch[0]][1]:
                    d = jnp.where(terms[ch][2], d, 0.0)
                dz[ch] = d.astype(BF16)
            dq = [carry[h][2] for h in range(2)]
            for t, h in chains:
                dq[h] = dq[h] + _dot(dz[t, h], ks[t], NN)
            for t in range(nt):
                dk_ref[:, rows[t]] += _dot(qth[0], dz[t, 0], NN) + _dot(qth[1], dz[t, 1], NN)
                dv_ref[:, rows[t]] += _dot(doth[0], a[t, 0].astype(BF16), NN) + _dot(doth[1], a[t, 1].astype(BF16), NN)
            return tuple((cc[h], cg[h], dq[h]) for h in range(2))

        zero = jnp.zeros((tq, 1), F32)
        zq = jnp.zeros((tq, LANES), F32)
        step = lambda n, cr: sweep([(SB_UNROLL_BWD * n + t, False) for t in range(SB_UNROLL_BWD)], cr)
        carry = lax.fori_loop(0, i // SB_UNROLL_BWD, step, ((zero, zero, zq), (zero, zero, zq)))
        tails = [lambda cr, k=k: sweep([(i - k + t, False) for t in range(k)] + [(i, True)], cr) for k in range(SB_UNROLL_BWD)]
        carry = _by_value(i % SB_UNROLL_BWD, tails, carry)
        dq_ref[...] = (jnp.where(_lo_mask((tq, LANES)), carry[0][2], carry[1][2]) * SCALE).astype(BF16)

        @pl.when(i == nq - 1)
        def _():
            for c in range(nq):
                cols = slice(c * tq, (c + 1) * tq)
                dkm_ref[cols, :] = dk_ref[:, cols].T.astype(BF16)
                dvm_ref[cols, :] = dv_ref[:, cols].T.astype(BF16)

    spec_q = pl.BlockSpec((tq, LANES), lambda b, p, i: (b * nq + i, p))
    spec_s = pl.BlockSpec((LANES, S), lambda b, p, i: (b * NP + p, 0))
    spec_m = pl.BlockSpec((S, LANES), lambda b, p, i: (b, p))
    key_side = jax.ShapeDtypeStruct((B * NQ, S), F32)
    token_major = jax.ShapeDtypeStruct((T, NQ), BF16)
    in_specs = [spec_q, pl.BlockSpec((S, LANES), lambda b, p, i: (b, NP + p)),
                pl.BlockSpec((S, LANES), lambda b, p, i: (b, 2 * NP + p)), spec_q, spec_q]
    out_specs = [spec_q, spec_s, spec_s, spec_m, spec_m]
    out_shape = [token_major, key_side, key_side, token_major, token_major]
    args = [qkv, qkv, qkv, do, tot]
    return _call_behind(body, name, grid, in_specs, out_specs, out_shape, args, exchange, True)


def _adamw(w, g, m, v, name):
    shape = w.shape
    cols = shape[-1]
    rows = math.prod(shape[:-1])
    tr = _pick(rows, max(8, (1 << 19) // max(cols, LANES) // 8 * 8), 8)

    def body(w_ref, g_ref, m_ref, v_ref, d_ref, mo_ref, vo_ref):
        gv = g_ref[...]
        mn = ADAM_B1 * m_ref[...] + (1.0 - ADAM_B1) * gv
        vn = ADAM_B2 * v_ref[...] + (1.0 - ADAM_B2) * (gv * gv)
        m_hat = mn / (1.0 - ADAM_B1 ** ADAM_STEP)
        v_hat = vn / (1.0 - ADAM_B2 ** ADAM_STEP)
        d_ref[...] = -ADAM_LR * (m_hat / (jnp.sqrt(v_hat) + ADAM_EPS) + ADAM_WD * w_ref[...])
        mo_ref[...] = mn
        vo_ref[...] = vn

    spec = pl.BlockSpec((tr, cols), lambda i: (i, 0))
    out = jax.ShapeDtypeStruct((rows, cols), F32)
    d, mn, vn = pl.pallas_call(
        body, name=name, grid=(rows // tr,),
        in_specs=[spec] * 4, out_specs=[spec] * 3, out_shape=[out] * 3,
        compiler_params=_params("parallel"),
    )(w.reshape(rows, cols), g.reshape(rows, cols), m.reshape(rows, cols), v.reshape(rows, cols))
    return d.reshape(shape), mn.reshape(shape), vn.reshape(shape)


def _pad_rows(a, rows):
    return jnp.pad(a, ((0, rows - a.shape[0]), (0, 0)))


def kernel(x, c, positions, ada_w, ada_b, norm1_g, norm2_g, wqkv_a, q_norm_a, k_norm_a, sinks_a, wo_a, wqkv_b, wo_b, w_gate, w_up, w_down, loss_target, m_ada_w, m_ada_b, m_norm1_g, m_norm2_g, m_wqkv_a, m_q_norm_a, m_k_norm_a, m_sinks_a, m_wo_a, m_wqkv_b, m_wo_b, m_w_gate, m_w_up, m_w_down, v_ada_w, v_ada_b, v_norm1_g, v_norm2_g, v_wqkv_a, v_q_norm_a, v_k_norm_a, v_sinks_a, v_wo_a, v_wqkv_b, v_wo_b, v_w_gate, v_w_up, v_w_down):
    B, S, D = x.shape
    T = B * S
    L = ada_w.shape[0]
    NA, NB_ = wqkv_a.shape[0], wqkv_b.shape[0]
    me = 4 * lax.axis_index("x") + 2 * lax.axis_index("y") + lax.axis_index("c")
    xt = x.reshape(T, D)

    col_sharded = {"qkv_a": wqkv_a, "qkv_b": wqkv_b, "gate": w_gate, "up": w_up}
    row_sharded = {"wo_a": wo_a, "wo_b": wo_b, "down": w_down}

    def shard_rows(key):
        kind, idx = key
        return col_sharded[kind][idx].T if kind in col_sharded else row_sharded[kind][idx]

    def layer_keys(l):
        mix = "a" if l % 2 == 0 else "b"
        return [("qkv_" + mix, l // 2), ("wo_" + mix, l // 2), ("gate", l), ("up", l), ("down", l)]

    def unpack(buf, keys, reshape):
        out, off = {}, 0
        for key in keys:
            rows = shard_rows(key).shape[0]
            out[key] = reshape(buf[..., off:off + rows, :], rows)
            off += rows
        return out

    first_b = 1
    keys_early = layer_keys(0)[:2]
    keys_mid = layer_keys(0)[2:] + [("qkv_b", 0)]
    keys_late = [k for l in range(1, L) for k in layer_keys(l) if k != ("qkv_b", 0)]
    pack = lambda keys: jnp.concatenate([shard_rows(k).astype(BF16) for k in keys], axis=0)
    full_rows = lambda b, rows: b.reshape(NDEV * rows, D)
    W = unpack(_all_gather(pack(keys_early), "ag_weights"), keys_early, full_rows)

    WA = ada_w.shape[2]
    c_all = _all_gather(c, "ag_c").reshape(NDEV * B, D)
    bias = lax.dynamic_slice_in_dim(ada_b, me * WA, WA, axis=1).reshape(L, 1, WA)
    mod_part = _ada_fwd(c_all, ada_w, bias, "ada_fwd")
    mod_all = _all_gather(mod_part.reshape(L * NDEV * B, WA), "ag_mod")
    mod_all = mod_all.reshape(NDEV, L, NDEV * B, WA).transpose(1, 2, 0, 3).reshape(L, NDEV * B, NDEV * WA)
    mod = lax.dynamic_slice_in_dim(mod_all, me * B, B, axis=1)
    mod = mod.reshape(L, B, 6, 1, D)
    sh1, sc1, g1, sh2, sc2, g2 = [mod[:, :, k] for k in range(6)]

    half = ROT // 2
    inv_freq = jnp.power(jnp.float32(ROPE_THETA), -jnp.arange(half, dtype=F32) * 2.0 / ROT)
    ang = positions.reshape(T, 1).astype(F32) * inv_freq[None, :]
    cos, sin = jnp.cos(ang), jnp.sin(ang)
    ones = jnp.ones((T, HEAD - ROT), F32)
    zeros = jnp.zeros((T, HEAD - ROT), F32)
    z8 = jnp.zeros((T, half), F32)
    cs = jnp.tile(jnp.concatenate([cos, cos, ones], axis=1), (1, 2))
    s1 = jnp.tile(jnp.concatenate([-sin, z8, zeros], axis=1), (1, 2))
    s2 = jnp.tile(jnp.concatenate([z8, sin, zeros], axis=1), (1, 2))

    saved = []
    xc = xt
    h1 = _norm_mod(xc, norm1_g[0:1], sc1[0], sh1[0], S, "norm1_0")
    for l in range(L):
        j = l // 2
        sv = dict(x_in=xc, h1=h1)
        if l % 2 == 0:
            qkv = _mm_nt(h1, W["qkv_a", j], F32, f"qkv_a_{l}")
            qg = jnp.tile(q_norm_a[j:j + 1], (1, 2))
            kg = jnp.tile(k_norm_a[j:j + 1], (1, 2))
            qn, kd, vd = _qk_prep(qkv, cs, s1, s2, qg, kg, f"qk_prep_{l}")
            sink2 = jnp.repeat(sinks_a[j].reshape(-1, 2), HEAD, axis=1).reshape(-1, 1, LANES)
            if l == 0:
                attn, lse, mid = _swa_fwd(qn, kd, vd, sink2, B, f"swa_fwd_{l}", gather=pack(keys_mid))
                W.update(unpack(mid, keys_mid, full_rows))
            else:
                attn, lse = _swa_fwd(qn, kd, vd, sink2, B, f"swa_fwd_{l}")
            sv.update(qkv=qkv, qg=qg, kg=kg, qn=qn, kd=kd, vd=vd, sink2=sink2, lse=lse)
            wo = W["wo_a", j]
        else:
            qkv = _mm_nt(h1, W["qkv_b", j], BF16, f"qkv_b_{l}")
            if l == first_b:
                attn, tot, late = _sb_fwd(qkv, B, f"sb_fwd_{l}", gather=pack(keys_late))
                W.update(unpack(late, keys_late, full_rows))
            else:
                attn, tot = _sb_fwd(qkv, B, f"sb_fwd_{l}")
            sv.update(qkv=qkv, tot=tot)
            wo = W["wo_b", j]
        y1, xm, h2 = _mm_res(attn, wo, xc, g1[l], S, f"attn_out_{l}", norm=(norm2_g[l:l + 1], sc2[l], sh2[l]))
        gate, up, act = _swiglu_fwd(h2, W["gate", l], W["up", l], f"swiglu_fwd_{l}")
        if l + 1 < L:
            y2, xc, h1 = _mm_res(act, W["down", l], xm, g2[l], S, f"mlp_out_{l}",
                                 norm=(norm1_g[l + 1:l + 2], sc1[l + 1], sh1[l + 1]))
        else:
            y2, xc = _mm_res(act, W["down", l], xm, g2[l], S, f"mlp_out_{l}")
        sv.update(attn=attn, y1=y1, x_mid=xm, h2=h2, gate=gate, up=up, act=act, y2=y2)
        saved.append(sv)

    dx, loss_tile = _loss_head(xc, loss_target.reshape(T, D), "loss_head")

    G = {}
    pack_grads = lambda keys: jnp.concatenate([G[k].reshape(NDEV, G[k].shape[0] // NDEV, D) for k in keys], axis=1)
    keys_hi = [k for l in range(first_b + 1, L) for k in layer_keys(l)] + layer_keys(first_b)[1:]
    keys_mlp0_g = [("qkv_b", 0), ("down", 0)]
    keys_mid_g = [("wo_a", 0), ("gate", 0), ("up", 0)]
    keys_lo = layer_keys(0)[:1]
    received_hi = received_mid = received_mlp0 = None
    dmod = [None] * L
    dn1, dn2 = [None] * L, [None] * L
    dqg, dkg, dsink = [None] * NA, [None] * NA, [None] * NA
    dy2, dg2 = _gate_bwd(dx, saved[L - 1]["y2"], g2[L - 1], S, "gate2_bwd_top")
    for l in reversed(range(L)):
        j = l // 2
        mix = "a" if l % 2 == 0 else "b"
        sv = saved[l]
        dgate, dup = _swiglu_bwd(dy2, W["down", l], sv["gate"], sv["up"], f"swiglu_bwd_{l}")
        G["down", l] = _mm_tn(sv["act"], dy2, f"dw_down_{l}")
        G["gate", l] = _mm_tn(dgate, sv["h2"], f"dw_gate_{l}")
        G["up", l] = _mm_tn(dup, sv["h2"], f"dw_up_{l}")
        n2 = _norm_mod_bwd(sv["x_mid"], [(dgate, W["gate", l]), (dup, W["up", l])], dx, norm2_g[l:l + 1], sc2[l], S,
                           f"norm2_bwd_{l}", below=(sv["y1"], g1[l]), exchange=pack_grads(keys_mlp0_g) if l == 0 else None)
        dxm, dsh2, dsc2, dn2[l], dy1, dg1 = n2[:6]
        if l == 0:
            received_mlp0 = n2[6]
        dattn = _mm_nt(dy1, W["wo_" + mix, j], BF16, f"dattn_{l}")
        G["wo_" + mix, j] = _mm_tn(sv["attn"], dy1, f"dw_o_{l}")
        if l % 2 == 0:
            swa_args = (sv["qn"], sv["kd"], sv["vd"], sv["sink2"], dattn, sv["lse"], B, f"swa_bwd_{l}")
            if l == 0:
                dq, dkc, dkp, dvc, dvp, dsink[j], received_mid = _swa_bwd(*swa_args, exchange=pack_grads(keys_mid_g))
            else:
                dq, dkc, dkp, dvc, dvp, dsink[j] = _swa_bwd(*swa_args)
            dqkv, dqg[j], dkg[j] = _qk_prep_bwd(sv["qkv"], cs, s1, s2, sv["qg"], sv["kg"], dq, dkc, dkp, dvc, dvp, B,
                                                f"qk_prep_bwd_{l}")
        else:
            nqb = sv["qkv"].shape[1] // 3
            sb_args = (sv["qkv"], dattn, sv["tot"], B, f"sb_bwd_{l}")
            if l == first_b and keys_hi:
                dq, _, _, dk, dv, received_hi = _sb_bwd(*sb_args, exchange=pack_grads(keys_hi))
            else:
                dq, _, _, dk, dv = _sb_bwd(*sb_args)
        wt = W["qkv_" + mix, j]
        if l % 2 == 0:
            dh1_pairs = [(dqkv, wt)]
            G["qkv_a", j] = _mm_tn(dqkv, sv["h1"], f"dw_qkv_{l}")
        else:
            parts = [dq, dk, dv]
            dh1_pairs = [(part, wt[k * nqb:(k + 1) * nqb]) for k, part in enumerate(parts)]
            G["qkv_b", j] = jnp.concatenate([_mm_tn(part, sv["h1"], f"dw_qkv_{l}_{k}") for k, part in enumerate(parts)], axis=0)
        n1_args = (sv["x_in"], dh1_pairs, dxm, norm1_g[l:l + 1], sc1[l], S, f"norm1_bwd_{l}")
        dmod_l = [None, None, dg1, dsh2, dsc2, dg2]
        if l > 0:
            dx, dmod_l[0], dmod_l[1], dn1[l], dy2, dg2 = _norm_mod_bwd(*n1_args, below=(saved[l - 1]["y2"], g2[l - 1]))
        else:
            dx, dmod_l[0], dmod_l[1], dn1[l] = _norm_mod_bwd(*n1_args)
        dmod[l] = jnp.concatenate(dmod_l, axis=1)
    grad_x = dx.reshape(B, S, D)

    ndm = L * 6
    dmod_rows = jnp.stack(dmod, axis=1).reshape(B * ndm, D)
    misc = jnp.concatenate(
        [jnp.concatenate(dn1, axis=0).reshape(B * L, D), jnp.concatenate(dn2, axis=0).reshape(B * L, D),
         _pad_rows(jnp.concatenate([jnp.pad(r, ((0, 0), (0, D - LANES))) for r in dqg + dkg]
                                   + [jnp.pad(r[:, 0, ::HEAD].reshape(1, -1), ((0, 0), (0, D - 2 * r.shape[0]))) for r in dsink]
                                   + [jnp.pad(loss_tile[0:1, 0:1], ((0, 0), (0, D - 1)))], axis=0), 8)], axis=0)
    nmisc = misc.shape[0]
    small = _all_gather(jnp.concatenate([dmod_rows, _pad_rows(misc, -(-nmisc // 8) * 8)], axis=0), "ag_small")
    dmod_all = small[:, :B * ndm].reshape(NDEV * B, ndm, D)
    g_ada_b = _sum_leading(dmod_all, "sum_dmod").reshape(L, 6 * D)
    misc_sum = _sum_leading(small[:, B * ndm:], "sum_misc")
    g_n1 = misc_sum[0:B * L].reshape(L, B, D)
    g_n2 = misc_sum[B * L:2 * B * L].reshape(L, B, D)
    g_norm1 = _sum_leading(g_n1.transpose(1, 0, 2), "sum_n1")
    g_norm2 = _sum_leading(g_n2.transpose(1, 0, 2), "sum_n2")
    o = 2 * B * L
    g_qn = misc_sum[o:o + NA, :HEAD]
    g_kn = misc_sum[o + NA:o + 2 * NA, :HEAD]
    nsink = sinks_a.shape[1]
    g_sink = misc_sum[o + 2 * NA:o + 3 * NA, :nsink]
    loss = misc_sum[o + 3 * NA, 0]

    dmod_loc = lax.dynamic_slice_in_dim(dmod_all.reshape(NDEV * B, L, 6 * D), me * WA, WA, axis=2)
    g_ada_w = _ada_bwd(c_all, dmod_loc.transpose(1, 0, 2), "ada_bwd")

    shard = unpack(_sum_leading(_exchange(pack_grads(keys_lo), "grad_exchange"), "grad_sum"), keys_lo, lambda b, rows: b)
    shard.update(unpack(_sum_leading(received_mid, "grad_sum_mid"), keys_mid_g, lambda b, rows: b))
    shard.update(unpack(_sum_leading(received_mlp0, "grad_sum_mlp0"), keys_mlp0_g, lambda b, rows: b))
    if received_hi is not None:
        shard.update(unpack(_sum_leading(received_hi, "grad_sum_hi"), keys_hi, lambda b, rows: b))

    def stacked(kind, n):
        return jnp.stack([shard[kind, i].T if kind in col_sharded else shard[kind, i] for i in range(n)])

    gw_qkv_a, gw_qkv_b, gw_gate, gw_up = stacked("qkv_a", NA), stacked("qkv_b", NB_), stacked("gate", L), stacked("up", L)
    gw_wo_a, gw_wo_b, gw_down = stacked("wo_a", NA), stacked("wo_b", NB_), stacked("down", L)

    grads = [g_ada_w, g_ada_b, g_norm1, g_norm2, gw_qkv_a, g_qn, g_kn, g_sink, gw_wo_a, gw_qkv_b, gw_wo_b,
             gw_gate, gw_up, gw_down]
    ws = [ada_w, ada_b, norm1_g, norm2_g, wqkv_a, q_norm_a, k_norm_a, sinks_a, wo_a, wqkv_b, wo_b, w_gate, w_up, w_down]
    ms = [m_ada_w, m_ada_b, m_norm1_g, m_norm2_g, m_wqkv_a, m_q_norm_a, m_k_norm_a, m_sinks_a, m_wo_a, m_wqkv_b,
          m_wo_b, m_w_gate, m_w_up, m_w_down]
    vs = [v_ada_w, v_ada_b, v_norm1_g, v_norm2_g, v_wqkv_a, v_q_norm_a, v_k_norm_a, v_sinks_a, v_wo_a, v_wqkv_b,
          v_wo_b, v_w_gate, v_w_up, v_w_down]
    deltas, new_m, new_v = [], [], []
    for k, (w, g, m, v) in enumerate(zip(ws, grads, ms, vs)):
        g = g.reshape(w.shape)
        d, mn, vn = _adamw(w, g, m, v, f"adamw_{k}")
        grads[k] = g
        deltas.append(d)
        new_m.append(mn)
        new_v.append(vn)
    return (loss, grad_x, *grads, *deltas, *new_m, *new_v)
```

```python
import functools
import math

import jax
import jax.numpy as jnp
from jax import lax
from jax.experimental import pallas as pl
from jax.experimental.pallas import tpu as pltpu

F32 = jnp.float32
BF16 = jnp.bfloat16
NDEV = 8
HEAD = 64
BLK = 128
LANES = 128
EPS = 1e-6
ROT = HEAD // 4
ROPE_THETA = 500000.0
SCALE = HEAD ** -0.5
LOG2E = math.log2(math.e)
NEG = -1e30
VMEM_LIMIT = 56 * 1024 * 1024
MESH = pl.DeviceIdType.MESH
HIGH = lax.Precision.HIGHEST

ADAM_LR = 0.001
ADAM_B1 = 0.9
ADAM_B2 = 0.999
ADAM_EPS = 1e-08
ADAM_WD = 0.01
ADAM_STEP = 10


def _params(*sem):
    return pltpu.CompilerParams(dimension_semantics=sem, vmem_limit_bytes=VMEM_LIMIT)


def _pick(n, cap, mult):
    if n <= cap:
        return n
    best = None
    for t in range(mult, cap + 1, mult):
        if n % t == 0:
            best = t
    assert best is not None, (n, cap, mult)
    return best


def _dot(a, b, dims, precision=None):
    return lax.dot_general(a, b, (dims, ((), ())), preferred_element_type=F32, precision=precision)


NN = ((1,), (0,))
NT = ((1,), (1,))
TN = ((0,), (0,))


def _all_gather(x, name):
    m, n = x.shape

    def body(x_ref, out_ref, send_sems, recv_sems, local_sem):
        ix, iy, ic = lax.axis_index("x"), lax.axis_index("y"), lax.axis_index("c")
        me, sibling = (ix, iy, ic), (ix, iy, 1 - ic)
        chips = [(1 - ix, iy), (ix, 1 - iy), (1 - ix, 1 - iy)]

        def slab(px, py, pc):
            return out_ref.at[4 * px + 2 * py + pc]

        def copy(k, block, to, src=None):
            return pltpu.make_async_remote_copy(
                src_ref=slab(*block) if src is None else src, dst_ref=slab(*block),
                send_sem=send_sems.at[k], recv_sem=recv_sems.at[k], device_id=to, device_id_type=MESH)

        mine = pltpu.make_async_copy(x_ref, slab(*me), local_sem)
        mine.start()
        first = [copy(0, me, sibling, src=x_ref)]
        first += [copy(1 + j, me, (*chip, ic), src=x_ref) for j, chip in enumerate(chips)]
        for cp in first:
            cp.start()
        passed = [copy(4 + j, (*chip, ic), sibling) for j, chip in enumerate(chips)]
        for j, chip in enumerate(chips):
            copy(1 + j, (*chip, ic), me).wait_recv()
            passed[j].start()
        copy(0, sibling, me).wait_recv()
        for j, chip in enumerate(chips):
            copy(4 + j, (*chip, 1 - ic), me).wait_recv()
        for cp in first + passed:
            cp.wait_send()
        mine.wait()

    return pl.pallas_call(
        body, name=name,
        out_shape=jax.ShapeDtypeStruct((NDEV, m, n), x.dtype),
        in_specs=[pl.BlockSpec(memory_space=pl.ANY)],
        out_specs=pl.BlockSpec(memory_space=pl.ANY),
        scratch_shapes=[pltpu.SemaphoreType.DMA((7,)), pltpu.SemaphoreType.DMA((7,)), pltpu.SemaphoreType.DMA(())],
    )(x)


COMM_SEMS = [pltpu.SemaphoreType.DMA((NDEV - 1,)), pltpu.SemaphoreType.DMA((NDEV - 1,)), pltpu.SemaphoreType.DMA(())]
HBM_SPEC = pl.BlockSpec(memory_space=pl.ANY)


def _direct_copies(src_ref, dst_ref, sems, scatter):
    send_sems, recv_sems, own_sem = sems
    ix, iy, ic = lax.axis_index("x"), lax.axis_index("y"), lax.axis_index("c")
    me = 4 * ix + 2 * iy + ic
    copies = [pltpu.make_async_copy(src_ref.at[me] if scatter else src_ref, dst_ref.at[me], own_sem)]
    for k in range(1, NDEV):
        px = 1 - ix if k & 4 else ix
        py = 1 - iy if k & 2 else iy
        pc = 1 - ic if k & 1 else ic
        copies.append(pltpu.make_async_remote_copy(
            src_ref=src_ref.at[4 * px + 2 * py + pc] if scatter else src_ref, dst_ref=dst_ref.at[me],
            send_sem=send_sems.at[k - 1], recv_sem=recv_sems.at[k - 1],
            device_id=(px, py, pc), device_id_type=MESH))
    return copies


def _exchange(p, name):
    def body(p_ref, r_ref, *sems):
        copies = _direct_copies(p_ref, r_ref, sems, True)
        for cp in copies:
            cp.start()
        for cp in copies:
            cp.wait()

    return pl.pallas_call(
        body, name=name,
        out_shape=jax.ShapeDtypeStruct(p.shape, p.dtype),
        in_specs=[HBM_SPEC], out_specs=HBM_SPEC, scratch_shapes=COMM_SEMS,
    )(p)


def _call_behind(body, name, grid, in_specs, out_specs, out_shape, args, payload=None, scatter=False):
    params = _params(*["arbitrary"] * len(grid))
    if payload is None:
        return pl.pallas_call(body, name=name, grid=grid, in_specs=in_specs, out_specs=out_specs, out_shape=out_shape,
                              compiler_params=params)(*args)
    n_in, n_out = len(in_specs), len(out_specs)

    def edge(first):
        ids = [pl.program_id(a) for a in range(len(grid))]
        return functools.reduce(lambda u, v: u & v, [i == (0 if first else d - 1) for i, d in zip(ids, grid)])

    def wrapped(*refs):
        x_ref, r_ref, sems = refs[n_in], refs[n_in + 1 + n_out], refs[n_in + n_out + 2:]

        @pl.when(edge(True))
        def _():
            for cp in _direct_copies(x_ref, r_ref, sems, scatter):
                cp.start()
        body(*refs[:n_in], *refs[n_in + 1:n_in + 1 + n_out])

        @pl.when(edge(False))
        def _():
            for cp in _direct_copies(x_ref, r_ref, sems, scatter):
                cp.wait()

    arrived = jax.ShapeDtypeStruct(payload.shape if scatter else (NDEV,) + payload.shape, payload.dtype)
    return pl.pallas_call(
        wrapped, name=name, grid=grid, in_specs=list(in_specs) + [HBM_SPEC], out_specs=list(out_specs) + [HBM_SPEC],
        out_shape=list(out_shape) + [arrived], scratch_shapes=COMM_SEMS, compiler_params=params,
    )(*args, payload)


def _sum_leading(r, name):
    k, m, n = r.shape
    mult = 8 * (4 // r.dtype.itemsize)
    tm = _pick(m, max(mult, (4 * 1024 * 1024) // (k * n * r.dtype.itemsize) // mult * mult), mult)

    def body(r_ref, o_ref):
        acc = r_ref[0].astype(F32)
        for s in range(1, k):
            acc = acc + r_ref[s].astype(F32)
        o_ref[...] = acc

    return pl.pallas_call(
        body, name=name, grid=(m // tm,),
        in_specs=[pl.BlockSpec((k, tm, n), lambda i: (0, i, 0))],
        out_specs=pl.BlockSpec((tm, n), lambda i: (i, 0)),
        out_shape=jax.ShapeDtypeStruct((m, n), F32),
        compiler_params=_params("parallel"),
    )(r)


def _mm_nt(a, bt, out_dtype, name):
    M, K = a.shape
    N = bt.shape[0]
    tm, tn = _pick(M, 512, 8), _pick(N, 1536, LANES)

    def body(a_ref, b_ref, o_ref):
        o_ref[...] = _dot(a_ref[...], b_ref[...], NT).astype(out_dtype)

    return pl.pallas_call(
        body, name=name, grid=(N // tn, M // tm),
        in_specs=[pl.BlockSpec((tm, K), lambda j, i: (i, 0)), pl.BlockSpec((tn, K), lambda j, i: (j, 0))],
        out_specs=pl.BlockSpec((tm, tn), lambda j, i: (i, j)),
        out_shape=jax.ShapeDtypeStruct((M, N), out_dtype),
        compiler_params=_params("parallel", "parallel"),
    )(a, bt)


def _mm_tn(a, b, name):
    M, N1 = a.shape
    N2 = b.shape[1]
    t1, tk = _pick(N1, 1536, LANES), _pick(M, 1024, 8)
    nk = M // tk

    def body(a_ref, b_ref, o_ref, acc_ref):
        k = pl.program_id(1)

        @pl.when(k == 0)
        def _():
            acc_ref[...] = jnp.zeros_like(acc_ref)
        acc_ref[...] += _dot(a_ref[...], b_ref[...], TN)

        @pl.when(k == nk - 1)
        def _():
            o_ref[...] = acc_ref[...].astype(BF16)

    return pl.pallas_call(
        body, name=name, grid=(N1 // t1, nk),
        in_specs=[pl.BlockSpec((tk, t1), lambda i, k: (k, i)), pl.BlockSpec((tk, N2), lambda i, k: (k, 0))],
        out_specs=pl.BlockSpec((t1, N2), lambda i, k: (i, 0)),
        out_shape=jax.ShapeDtypeStruct((N1, N2), BF16),
        scratch_shapes=[pltpu.VMEM((t1, N2), F32)],
        compiler_params=_params("parallel", "arbitrary"),
    )(a, b)


def _norm_mod_rows(xv, gain, sc, sh):
    r = lax.rsqrt(jnp.mean(xv * xv, axis=-1, keepdims=True) + EPS)
    return ((xv * r) * gain * (1.0 + sc) + sh).astype(BF16)


def _mm_res(a, w, x, gate, S, name, norm=None):
    T, K = a.shape
    D = w.shape[1]
    tm = _pick(S, 512, 8)
    nb = S // tm

    def body(a_ref, w_ref, x_ref, g_ref, *rest):
        y = _dot(a_ref[...], w_ref[...], NN)
        xn = x_ref[...] + g_ref[0] * y
        if norm is None:
            y_ref, o_ref = rest
        else:
            gain_ref, sc_ref, sh_ref, y_ref, o_ref, h_ref = rest
            h_ref[...] = _norm_mod_rows(xn, gain_ref[...], sc_ref[0], sh_ref[0])
        y_ref[...] = y.astype(BF16)
        o_ref[...] = xn

    spec_t = pl.BlockSpec((tm, D), lambda i: (i, 0))
    spec_b = pl.BlockSpec((1, 1, D), lambda i: (i // nb, 0, 0))
    in_specs = [pl.BlockSpec((tm, K), lambda i: (i, 0)), pl.BlockSpec((K, D), lambda i: (0, 0)), spec_t, spec_b]
    out_specs = [spec_t, spec_t]
    out_shape = [jax.ShapeDtypeStruct((T, D), BF16), jax.ShapeDtypeStruct((T, D), F32)]
    args = [a, w, x, gate]
    if norm is not None:
        in_specs += [pl.BlockSpec((1, D), lambda i: (0, 0)), spec_b, spec_b]
        out_specs.append(spec_t)
        out_shape.append(jax.ShapeDtypeStruct((T, D), BF16))
        args += list(norm)
    return pl.pallas_call(
        body, name=name, grid=(T // tm,), in_specs=in_specs, out_specs=out_specs, out_shape=out_shape,
        compiler_params=_params("parallel"),
    )(*args)


def _swiglu_fwd(h, wgt, wut, name):
    T, D = h.shape
    F = wgt.shape[0]
    tm, tn = _pick(T, 512, 8), _pick(F, 1536, LANES)

    def body(h_ref, g_ref, u_ref, go_ref, uo_ref, a_ref):
        hh = h_ref[...]
        g = _dot(hh, g_ref[...], NT)
        u = _dot(hh, u_ref[...], NT)
        go_ref[...] = g.astype(BF16)
        uo_ref[...] = u.astype(BF16)
        a_ref[...] = (g * jax.nn.sigmoid(g) * u).astype(BF16)

    spec_w = pl.BlockSpec((tn, D), lambda j, i: (j, 0))
    spec_o = pl.BlockSpec((tm, tn), lambda j, i: (i, j))
    out = jax.ShapeDtypeStruct((T, F), BF16)
    return pl.pallas_call(
        body, name=name, grid=(F // tn, T // tm),
        in_specs=[pl.BlockSpec((tm, D), lambda j, i: (i, 0)), spec_w, spec_w],
        out_specs=[spec_o, spec_o, spec_o],
        out_shape=[out, out, out],
        compiler_params=_params("parallel", "parallel"),
    )(h, wgt, wut)


def _swiglu_bwd(dy, wd, gate, up, name):
    T, D = dy.shape
    F = wd.shape[0]
    tm, tn = _pick(T, 512, 8), _pick(F, 1536, LANES)

    halves = [slice(0, tn // 2), slice(tn // 2, tn)] if tn % (2 * LANES) == 0 else [slice(0, tn)]

    def body(dy_ref, w_ref, g_ref, u_ref, dg_ref, du_ref):
        das = [_dot(dy_ref[...], w_ref[sl, :], NT) for sl in halves]
        for sl, da in zip(halves, das):
            g = g_ref[:, sl].astype(F32)
            sg = jax.nn.sigmoid(g)
            t = da * sg
            du_ref[:, sl] = (t * g).astype(BF16)
            dg_ref[:, sl] = (t * u_ref[:, sl].astype(F32) * (1.0 + g * (1.0 - sg))).astype(BF16)

    spec_o = pl.BlockSpec((tm, tn), lambda j, i: (i, j))
    return pl.pallas_call(
        body, name=name, grid=(F // tn, T // tm),
        in_specs=[pl.BlockSpec((tm, D), lambda j, i: (i, 0)), pl.BlockSpec((tn, D), lambda j, i: (j, 0)), spec_o, spec_o],
        out_specs=[spec_o, spec_o],
        out_shape=[jax.ShapeDtypeStruct((T, F), BF16), jax.ShapeDtypeStruct((T, F), BF16)],
        compiler_params=_params("parallel", "parallel"),
    )(dy, wd, gate, up)


def _norm_mod(x, gain, sc, sh, S, name):
    T, D = x.shape
    tm = _pick(S, 512, 8)
    nb = S // tm

    def body(x_ref, g_ref, sc_ref, sh_ref, o_ref):
        o_ref[...] = _norm_mod_rows(x_ref[...], g_ref[...], sc_ref[0], sh_ref[0])

    spec_b = pl.BlockSpec((1, 1, D), lambda i: (i // nb, 0, 0))
    return pl.pallas_call(
        body, name=name, grid=(T // tm,),
        in_specs=[pl.BlockSpec((tm, D), lambda i: (i, 0)), pl.BlockSpec((1, D), lambda i: (0, 0)), spec_b, spec_b],
        out_specs=pl.BlockSpec((tm, D), lambda i: (i, 0)),
        out_shape=jax.ShapeDtypeStruct((T, D), BF16),
        compiler_params=_params("parallel"),
    )(x, gain, sc, sh)


def _norm_mod_bwd(x, pairs, dres, gain, sc, S, name, below=None, exchange=None):
    T, D = x.shape
    B = T // S
    tm = _pick(S, 512, 8)
    nb = S // tm
    n_mm = 2 * len(pairs)

    def body(*refs):
        mm, (x_ref, dr_ref, g_ref, sc_ref), rest = refs[:n_mm], refs[n_mm:n_mm + 4], refs[n_mm + 4:]
        if below is None:
            o_ref, dsh_ref, dsc_ref, dg_ref = rest
            sums = [dsh_ref, dsc_ref, dg_ref]
        else:
            y_ref, gt_ref, o_ref, dsh_ref, dsc_ref, dg_ref, dy_ref, dgt_ref = rest
            sums = [dsh_ref, dsc_ref, dg_ref, dgt_ref]

        @pl.when(pl.program_id(1) == 0)
        def _():
            for ref in sums:
                ref[...] = jnp.zeros_like(ref)
        dhv = _dot(mm[0][...], mm[1][...], NN)
        for p in range(2, n_mm, 2):
            dhv = dhv + _dot(mm[p][...], mm[p + 1][...], NN)
        xv, g = x_ref[...], g_ref[...]
        r = lax.rsqrt(jnp.mean(xv * xv, axis=-1, keepdims=True) + EPS)
        xhat = xv * r
        dsh_ref[0] += jnp.sum(dhv, axis=0, keepdims=True)
        dsc_ref[0] += jnp.sum(dhv * (xhat * g), axis=0, keepdims=True)
        dn = dhv * (1.0 + sc_ref[0])
        dg_ref[0] += jnp.sum(dn * xhat, axis=0, keepdims=True)
        dxh = dn * g
        out = dr_ref[...] + r * (dxh - xhat * jnp.mean(dxh * xhat, axis=-1, keepdims=True))
        o_ref[...] = out
        if below is not None:
            dy_ref[...] = (out * gt_ref[0]).astype(BF16)
            dgt_ref[0] += jnp.sum(out * y_ref[...].astype(F32), axis=0, keepdims=True)

    spec_t = pl.BlockSpec((tm, D), lambda b, i: (b * nb + i, 0))
    spec_b = pl.BlockSpec((1, 1, D), lambda b, i: (b, 0, 0))
    red = jax.ShapeDtypeStruct((B, 1, D), F32)
    in_specs, args = [], []
    for a, w in pairs:
        K = a.shape[1]
        in_specs += [pl.BlockSpec((tm, K), lambda b, i: (b * nb + i, 0)),
                     pl.BlockSpec((K, D), lambda b, i: (0, 0), pipeline_mode=pl.Buffered(1))]
        args += [a, w]
    in_specs += [spec_t, spec_t, pl.BlockSpec((1, D), lambda b, i: (0, 0)), spec_b]
    args += [x, dres, gain, sc]
    out_specs = [spec_t, spec_b, spec_b, spec_b]
    out_shape = [jax.ShapeDtypeStruct((T, D), F32), red, red, red]
    if below is not None:
        in_specs += [spec_t, spec_b]
        out_specs += [spec_t, spec_b]
        out_shape += [jax.ShapeDtypeStruct((T, D), BF16), red]
        args += list(below)
    return _call_behind(body, name, (B, nb), in_specs, out_specs, out_shape, args, exchange, True)


def _gate_bwd(dx, y, gate, S, name):
    T, D = dx.shape
    B = T // S
    tm = _pick(S, 512, 8)
    nb = S // tm

    def body(dx_ref, y_ref, g_ref, dy_ref, dg_ref):
        @pl.when(pl.program_id(1) == 0)
        def _():
            dg_ref[...] = jnp.zeros_like(dg_ref)
        d = dx_ref[...]
        dy_ref[...] = (d * g_ref[0]).astype(BF16)
        dg_ref[0] += jnp.sum(d * y_ref[...].astype(F32), axis=0, keepdims=True)

    spec_t = pl.BlockSpec((tm, D), lambda b, i: (b * nb + i, 0))
    spec_b = pl.BlockSpec((1, 1, D), lambda b, i: (b, 0, 0))
    return pl.pallas_call(
        body, name=name, grid=(B, nb),
        in_specs=[spec_t, spec_t, spec_b],
        out_specs=[spec_t, spec_b],
        out_shape=[jax.ShapeDtypeStruct((T, D), BF16), jax.ShapeDtypeStruct((B, 1, D), F32)],
        compiler_params=_params("parallel", "arbitrary"),
    )(dx, y, gate)


def _loss_head(y, target, name):
    T, D = y.shape
    tm = _pick(T, 512, 8)

    def body(y_ref, t_ref, dy_ref, l_ref):
        @pl.when(pl.program_id(0) == 0)
        def _():
            l_ref[...] = jnp.zeros_like(l_ref)
        e = y_ref[...] - t_ref[...]
        dy_ref[...] = e * (1.0 / D)
        l_ref[...] += 0.5 * jnp.sum(jnp.mean(e * e, axis=-1, keepdims=True), axis=0, keepdims=True)

    spec = pl.BlockSpec((tm, D), lambda i: (i, 0))
    return pl.pallas_call(
        body, name=name, grid=(T // tm,),
        in_specs=[spec, spec],
        out_specs=[spec, pl.BlockSpec((8, LANES), lambda i: (0, 0))],
        out_shape=[jax.ShapeDtypeStruct((T, D), F32), jax.ShapeDtypeStruct((8, LANES), F32)],
        compiler_params=_params("arbitrary"),
    )(y, target)


def _ada_fwd(c_all, ada_w, bias, name):
    NB, D = c_all.shape
    L, _, W = ada_w.shape

    def body(c_ref, w_ref, b_ref, o_ref):
        cv = c_ref[...]
        cond = cv * jax.nn.sigmoid(cv)
        o_ref[0] = _dot(cond, w_ref[0], NN, HIGH) + b_ref[0]

    return pl.pallas_call(
        body, name=name, grid=(L,),
        in_specs=[pl.BlockSpec((NB, D), lambda l: (0, 0)), pl.BlockSpec((1, D, W), lambda l: (l, 0, 0)),
                  pl.BlockSpec((1, 1, W), lambda l: (l, 0, 0))],
        out_specs=pl.BlockSpec((1, NB, W), lambda l: (l, 0, 0)),
        out_shape=jax.ShapeDtypeStruct((L, NB, W), F32),
        compiler_params=_params("parallel"),
    )(c_all, ada_w, bias)


def _ada_bwd(c_all, dmod, name):
    NB, D = c_all.shape
    L, _, W = dmod.shape

    def body(c_ref, d_ref, o_ref):
        cv = c_ref[...]
        cond = cv * jax.nn.sigmoid(cv)
        o_ref[0] = _dot(cond, d_ref[0], TN, HIGH)

    return pl.pallas_call(
        body, name=name, grid=(L,),
        in_specs=[pl.BlockSpec((NB, D), lambda l: (0, 0)), pl.BlockSpec((1, NB, W), lambda l: (l, 0, 0))],
        out_specs=pl.BlockSpec((1, D, W), lambda l: (l, 0, 0)),
        out_shape=jax.ShapeDtypeStruct((L, D, W), F32),
        compiler_params=_params("parallel"),
    )(c_all, dmod)


def _lo_mask(shape):
    return lax.broadcasted_iota(jnp.int32, shape, len(shape) - 1) < HEAD


def _head_sum_matrix():
    r = lax.broadcasted_iota(jnp.int32, (LANES, LANES), 0) // HEAD
    c = lax.broadcasted_iota(jnp.int32, (LANES, LANES), 1) // HEAD
    return (r == c).astype(BF16)


def _head_sum(x, P):
    hi = x.astype(BF16)
    lo = (x - hi.astype(F32)).astype(BF16)
    return _dot(hi, P, NN) + _dot(lo, P, NN)


def _rope(y, cs, s1, s2):
    return y * cs + pltpu.roll(y, LANES - ROT // 2, 1) * s1 + pltpu.roll(y, ROT // 2, 1) * s2


def _rope_bwd(d, cs, s1, s2):
    return d * cs + pltpu.roll(d * s1, ROT // 2, 1) + pltpu.roll(d * s2, LANES - ROT // 2, 1)


def _qk_prep(qkv, cs, s1, s2, qg, kg, name):
    T, W = qkv.shape
    NQ = W - 2 * LANES
    tm = _pick(T, 512, 8)

    def body(x_ref, cs_ref, s1_ref, s2_ref, qg_ref, kg_ref, q_ref, k_ref, v_ref):
        P = _head_sum_matrix()
        cs_, s1_, s2_ = cs_ref[...], s1_ref[...], s2_ref[...]
        lo = _lo_mask((tm, LANES))

        def norm_rope(xv, g):
            ms = _head_sum(xv * xv, P) * (1.0 / HEAD)
            return _rope(xv * lax.rsqrt(ms + EPS) * g, cs_, s1_, s2_)

        for j in range(NQ // LANES):
            q_ref[:, j * LANES:(j + 1) * LANES] = norm_rope(x_ref[:, j * LANES:(j + 1) * LANES], qg_ref[...]).astype(BF16)
        kr = norm_rope(x_ref[:, NQ:NQ + LANES], kg_ref[...])
        ks = pltpu.roll(kr, HEAD, 1)
        k_ref[:, :LANES] = jnp.where(lo, kr, ks).astype(BF16)
        k_ref[:, LANES:] = jnp.where(lo, ks, kr).astype(BF16)
        vr = x_ref[:, NQ + LANES:]
        vs = pltpu.roll(vr, HEAD, 1)
        v_ref[:, :LANES] = jnp.where(lo, vr, vs).astype(BF16)
        v_ref[:, LANES:] = jnp.where(lo, vs, vr).astype(BF16)

    spec_t = pl.BlockSpec((tm, LANES), lambda i: (i, 0))
    spec_g = pl.BlockSpec((1, LANES), lambda i: (0, 0))
    return pl.pallas_call(
        body, name=name, grid=(T // tm,),
        in_specs=[pl.BlockSpec((tm, W), lambda i: (i, 0)), spec_t, spec_t, spec_t, spec_g, spec_g],
        out_specs=[pl.BlockSpec((tm, NQ), lambda i: (i, 0)), pl.BlockSpec((tm, 2 * LANES), lambda i: (i, 0)),
                   pl.BlockSpec((tm, 2 * LANES), lambda i: (i, 0))],
        out_shape=[jax.ShapeDtypeStruct((T, NQ), BF16), jax.ShapeDtypeStruct((T, 2 * LANES), BF16),
                   jax.ShapeDtypeStruct((T, 2 * LANES), BF16)],
        compiler_params=_params("parallel"),
    )(qkv, cs, s1, s2, qg, kg)


def _stack_heads(x2):
    lo = _lo_mask(x2.shape)
    z = jnp.zeros_like(x2)
    return jnp.concatenate([jnp.where(lo, x2, z), jnp.where(lo, z, x2)], axis=0)


def _unstack_heads(xs):
    r = xs.shape[0] // 2
    return jnp.where(_lo_mask((r, LANES)), xs[:r], xs[r:])


def _swa_valid(i):
    qo = lax.broadcasted_iota(jnp.int32, (2 * BLK, 2 * BLK), 0) % BLK
    kc_ = lax.broadcasted_iota(jnp.int32, (2 * BLK, 2 * BLK), 1)
    rel = qo + BLK - kc_
    return (rel >= 0) & (rel < BLK) & ((kc_ >= BLK) | (i > 0))


def _swa_scores(q2, kk, sink2, valid):
    qs = _stack_heads(q2) * SCALE
    s = _dot(qs, kk, NT)
    sk = jnp.concatenate([jnp.broadcast_to(sink2[:, 0:1], (BLK, 1)), jnp.broadcast_to(sink2[:, HEAD:HEAD + 1], (BLK, 1))], axis=0)
    return qs, jnp.where(valid, s, NEG), sk


def _swa_fwd(q, kd, vd, sink2, B, name, gather=None):
    T, NQ = q.shape
    NP = NQ // LANES
    nq = T // B // BLK
    NG = kd.shape[1] // LANES
    grp = NP // NG

    def body(q_ref, kp_ref, kc_ref, vp_ref, vc_ref, s_ref, o_ref, l_ref):
        valid = _swa_valid(pl.program_id(2))
        kk = jnp.concatenate([kp_ref[...], kc_ref[...]], axis=0)
        vs = _stack_heads(jnp.concatenate([vp_ref[...], vc_ref[...]], axis=0))
        sls = [slice(jj * LANES, (jj + 1) * LANES) for jj in range(grp)]
        sc = [_swa_scores(q_ref[:, sl], kk, s_ref[jj], valid) for jj, sl in enumerate(sls)]
        ms = [jnp.maximum(jnp.max(s, axis=1, keepdims=True), sk) for _, s, sk in sc]
        ps = [jnp.exp(s - m) for (_, s, _), m in zip(sc, ms)]
        ls = [jnp.sum(p, axis=1, keepdims=True) + jnp.exp(sk - m) for p, (_, _, sk), m in zip(ps, sc, ms)]
        ps = [(p * (1.0 / l)).astype(BF16) for p, l in zip(ps, ls)]
        os_ = [_dot(jnp.concatenate([p[:BLK], p[BLK:]], axis=1), vs, NN) for p in ps]
        for sl, o, m, l in zip(sls, os_, ms, ls):
            o_ref[:, sl] = o.astype(BF16)
            l_ref[:, sl] = _unstack_heads(jnp.broadcast_to(m + jnp.log(l), (2 * BLK, LANES)))

    spec_q = pl.BlockSpec((BLK, grp * LANES), lambda b, g, i: (b * nq + i, g))
    spec_p = pl.BlockSpec((BLK, LANES), lambda b, g, i: (b * nq + jnp.maximum(i - 1, 0), g))
    spec_c = pl.BlockSpec((BLK, LANES), lambda b, g, i: (b * nq + i, g))
    in_specs = [spec_q, spec_p, spec_c, spec_p, spec_c, pl.BlockSpec((grp, 1, LANES), lambda b, g, i: (g, 0, 0))]
    out_shape = [jax.ShapeDtypeStruct((T, NQ), BF16), jax.ShapeDtypeStruct((T, NQ), F32)]
    return _call_behind(body, name, (B, NG, nq), in_specs, [spec_q, spec_q], out_shape, [q, kd, kd, vd, vd, sink2], gather, False)


def _swa_bwd(q, kd, vd, sink2, do, lse, B, name, exchange=None):
    T, NQ = q.shape
    NP = NQ // LANES
    nq = T // B // BLK
    NG = kd.shape[1] // LANES
    grp = NP // NG

    def body(q_ref, kp_ref, kc_ref, vp_ref, vc_ref, s_ref, do_ref, l_ref,
             dq_ref, dkc_ref, dkp_ref, dvc_ref, dvp_ref, ds_ref):
        b, i = pl.program_id(1), pl.program_id(2)

        @pl.when((b == 0) & (i == 0))
        def _():
            ds_ref[...] = jnp.zeros_like(ds_ref)
        valid = _swa_valid(i)
        kk = jnp.concatenate([kp_ref[...], kc_ref[...]], axis=0)
        vv = jnp.concatenate([vp_ref[...], vc_ref[...]], axis=0)
        sls = [slice(jj * LANES, (jj + 1) * LANES) for jj in range(grp)]
        sc = [_swa_scores(q_ref[:, sl], kk, s_ref[jj], valid) for jj, sl in enumerate(sls)]
        dos = [_stack_heads(do_ref[:, sl]) for sl in sls]
        dps = [_dot(d, vv, NT) for d in dos]
        lses = [jnp.concatenate([l_ref[:, sl][:, 0:1], l_ref[:, sl][:, HEAD:HEAD + 1]], axis=0) for sl in sls]
        ps = [jnp.exp(s - lse) for (_, s, _), lse in zip(sc, lses)]
        deltas = [jnp.sum(p * dp, axis=1, keepdims=True) for p, dp in zip(ps, dps)]
        dscs = [(p * (dp - delta)).astype(BF16) for p, dp, delta in zip(ps, dps, deltas)]
        dqs = [_dot(dsc, kk, NN) for dsc in dscs]
        dk = jnp.zeros((2 * BLK, LANES), F32)
        dv = jnp.zeros((2 * BLK, LANES), F32)
        for jj, sl in enumerate(sls):
            dsk = -jnp.exp(sc[jj][2] - lses[jj]) * deltas[jj]
            dsk_lo = jnp.sum(dsk[:BLK], axis=0, keepdims=True)
            dsk_hi = jnp.sum(dsk[BLK:], axis=0, keepdims=True)
            ds_ref[jj] += jnp.where(_lo_mask((1, LANES)), dsk_lo, dsk_hi)
            dq_ref[:, sl] = _unstack_heads(dqs[jj]) * SCALE
            dk = dk + _dot(dscs[jj], sc[jj][0], TN)
            dv = dv + _dot(ps[jj].astype(BF16), dos[jj], TN)
        dkp_ref[...] = dk[:BLK]
        dkc_ref[...] = dk[BLK:]
        dvp_ref[...] = dv[:BLK]
        dvc_ref[...] = dv[BLK:]

    spec_q = pl.BlockSpec((BLK, grp * LANES), lambda g, b, i: (b * nq + i, g))
    spec_p = pl.BlockSpec((BLK, LANES), lambda g, b, i: (b * nq + jnp.maximum(i - 1, 0), g))
    spec_c = pl.BlockSpec((BLK, LANES), lambda g, b, i: (b * nq + i, g))
    spec_s = pl.BlockSpec((grp, 1, LANES), lambda g, b, i: (g, 0, 0))
    kv = jax.ShapeDtypeStruct((T, NG * LANES), F32)
    in_specs = [spec_q, spec_p, spec_c, spec_p, spec_c, spec_s, spec_q, spec_q]
    out_specs = [spec_q, spec_c, spec_c, spec_c, spec_c, spec_s]
    out_shape = [jax.ShapeDtypeStruct((T, NQ), F32), kv, kv, kv, kv, jax.ShapeDtypeStruct((NP, 1, LANES), F32)]
    return _call_behind(body, name, (NG, B, nq), in_specs, out_specs, out_shape, [q, kd, kd, vd, vd, sink2, do, lse],
                        exchange, True)


def _qk_prep_bwd(qkv, cs, s1, s2, qg, kg, dq, dkc, dkp, dvc, dvp, B, name):
    T, W = qkv.shape
    NQ = W - 2 * LANES
    NP = NQ // LANES
    nq = T // B // BLK

    def body(x_ref, cs_ref, s1_ref, s2_ref, qg_ref, kg_ref, dq_ref, dkc_ref, dkp_ref, dvc_ref, dvp_ref,
             o_ref, dqg_ref, dkg_ref):
        b, i = pl.program_id(0), pl.program_id(1)

        @pl.when((b == 0) & (i == 0))
        def _():
            dqg_ref[...] = jnp.zeros_like(dqg_ref)
            dkg_ref[...] = jnp.zeros_like(dkg_ref)
        P = _head_sum_matrix()
        cs_, s1_, s2_ = cs_ref[...], s1_ref[...], s2_ref[...]
        lo = _lo_mask((BLK, LANES))
        has_next = (i + 1 < nq).astype(F32)

        def norm_rope_bwd(xv, g, d):
            du = _rope_bwd(d, cs_, s1_, s2_)
            r = lax.rsqrt(_head_sum(xv * xv, P) * (1.0 / HEAD) + EPS)
            xhat = xv * r
            dgain = jnp.sum(du * xhat, axis=0, keepdims=True)
            uu = du * g
            dx = r * (uu - xhat * (_head_sum(uu * xhat, P) * (1.0 / HEAD)))
            return dx, dgain + pltpu.roll(dgain, HEAD, 1)

        dqg = jnp.zeros((1, LANES), F32)
        for j in range(NP):
            sl = slice(j * LANES, (j + 1) * LANES)
            dx, dg = norm_rope_bwd(x_ref[:, sl], qg_ref[...], dq_ref[:, sl])
            o_ref[:, sl] = dx.astype(BF16)
            dqg = dqg + dg
        dqg_ref[...] += dqg

        def fold(c_ref, p_ref, g):
            sl = slice(g * LANES, (g + 1) * LANES)
            t = c_ref[:, sl] + has_next * p_ref[:, sl]
            return t + pltpu.roll(t, HEAD, 1)

        dk = jnp.where(lo, fold(dkc_ref, dkp_ref, 0), fold(dkc_ref, dkp_ref, 1))
        dx, dg = norm_rope_bwd(x_ref[:, NQ:NQ + LANES], kg_ref[...], dk)
        o_ref[:, NQ:NQ + LANES] = dx.astype(BF16)
        dkg_ref[...] += dg
        dv = jnp.where(lo, fold(dvc_ref, dvp_ref, 0), fold(dvc_ref, dvp_ref, 1))
        o_ref[:, NQ + LANES:] = dv.astype(BF16)

    spec_t = pl.BlockSpec((BLK, LANES), lambda b, i: (b * nq + i, 0))
    spec_g = pl.BlockSpec((1, LANES), lambda b, i: (0, 0))
    spec_c = pl.BlockSpec((BLK, 2 * LANES), lambda b, i: (b * nq + i, 0))
    spec_n = pl.BlockSpec((BLK, 2 * LANES), lambda b, i: (b * nq + jnp.minimum(i + 1, nq - 1), 0))
    row = jax.ShapeDtypeStruct((1, LANES), F32)
    return pl.pallas_call(
        body, name=name, grid=(B, nq),
        in_specs=[pl.BlockSpec((BLK, W), lambda b, i: (b * nq + i, 0)), spec_t, spec_t, spec_t, spec_g, spec_g,
                  pl.BlockSpec((BLK, NQ), lambda b, i: (b * nq + i, 0)), spec_c, spec_n, spec_c, spec_n],
        out_specs=[pl.BlockSpec((BLK, W), lambda b, i: (b * nq + i, 0)), spec_g, spec_g],
        out_shape=[jax.ShapeDtypeStruct((T, W), BF16), row, row],
        compiler_params=_params("arbitrary", "arbitrary"),
    )(qkv, cs, s1, s2, qg, kg, dq, dkc, dkp, dvc, dvp)


SB_TILE = 256
SB_UNROLL = 4
SB_UNROLL_BWD = 2


def _split_heads(x2, scale=None):
    lo = _lo_mask(x2.shape)
    z = jnp.zeros_like(x2)
    if scale is not None:
        x2 = x2 * scale
    return jnp.where(lo, x2, z), jnp.where(lo, z, x2)


def _sb_terms(qh, kj, diagonal):
    z = _dot(qh, kj, NT)
    e = jnp.exp2(jnp.abs(z) * (-LOG2E))
    lb = jnp.minimum(z, 0.0) - jnp.log(1.0 + e)
    L = lb - z
    if not diagonal:
        return lb, L, None, z, e
    strict = lax.broadcasted_iota(jnp.int32, z.shape, 1) < lax.broadcasted_iota(jnp.int32, z.shape, 0)
    return lb, jnp.where(strict, L, 0.0), strict, z, e


def _tri(n, cmp):
    r = lax.broadcasted_iota(jnp.int32, (n, n), 0)
    c = lax.broadcasted_iota(jnp.int32, (n, n), 1)
    return cmp(r, c).astype(BF16)


def _by_value(r, fns, carry):
    if len(fns) == 1:
        return fns[0](carry)
    half = len(fns) // 2
    return lax.cond(r < half, lambda cr: _by_value(r, fns[:half], cr), lambda cr: _by_value(r - half, fns[half:], cr), carry)


def _sb_fwd(qkv, B, name, gather=None):
    T, W = qkv.shape
    NQ = W // 3
    NP = NQ // LANES
    S = T // B
    tq = min(SB_TILE, S)
    nq = S // tq
    grid = (B, NP, nq)

    def body(q_ref, k_ref, v_ref, o_ref, t_ref):
        i = pl.program_id(2)
        qh = _split_heads(q_ref[...], SCALE)
        U = _tri(tq, lambda r, c: r > c)

        def sweep(tiles, cs, acc):
            chains = [(t, h) for t in range(len(tiles)) for h in range(2)]
            rows = [pl.ds(pl.multiple_of(j * tq, tq), tq) for j, _ in tiles]
            ks = [k_ref[r, :] for r in rows]
            vs = [_split_heads(v_ref[r, :]) for r in rows]
            terms = {(t, h): _sb_terms(qh[h], ks[t], tiles[t][1]) for t, h in chains}
            carry = {}
            for h in range(2):
                c = cs[h]
                for t in range(len(tiles)):
                    carry[t, h] = c
                    c = c + jnp.sum(terms[t, h][1], axis=1, keepdims=True)
                cs = cs[:h] + (c,) + cs[h + 1:]
            cum = {ch: _dot(terms[ch][1].astype(BF16), U, NN) for ch in chains}
            for ch in chains:
                a = jnp.exp(terms[ch][0] + (cum[ch] + carry[ch]))
                if tiles[ch[0]][1]:
                    a = jnp.where(terms[ch][2], a, 0.0)
                acc = acc + _dot(a.astype(BF16), vs[ch[0]][ch[1]], NN)
            return cs, acc

        zero = jnp.zeros((tq, 1), F32)
        rem = i % SB_UNROLL
        heads = [lambda cr, k=k: sweep([(i, True)] + [(i - 1 - t, False) for t in range(k)], *cr) for k in range(SB_UNROLL)]
        carry = _by_value(rem, heads, ((zero, zero), jnp.zeros((tq, LANES), F32)))
        step = lambda n, cr: sweep([(i - 1 - rem - SB_UNROLL * n - t, False) for t in range(SB_UNROLL)], *cr)
        cs, acc = lax.fori_loop(0, i // SB_UNROLL, step, carry)
        o_ref[...] = acc.astype(BF16)
        t_ref[...] = jnp.where(_lo_mask((tq, LANES)), cs[0], cs[1])

    spec_q = pl.BlockSpec((tq, LANES), lambda b, p, i: (b * nq + i, p))
    in_specs = [spec_q, pl.BlockSpec((S, LANES), lambda b, p, i: (b, NP + p)),
                pl.BlockSpec((S, LANES), lambda b, p, i: (b, 2 * NP + p))]
    out_shape = [jax.ShapeDtypeStruct((T, NQ), BF16), jax.ShapeDtypeStruct((T, NQ), F32)]
    return _call_behind(body, name, grid, in_specs, [spec_q, spec_q], out_shape, [qkv, qkv, qkv], gather, False)


def _sb_bwd(qkv, do, tot, B, name, exchange=None):
    T, W = qkv.shape
    NQ = W // 3
    NP = NQ // LANES
    S = T // B
    tq = min(SB_TILE, S)
    nq = S // tq
    grid = (B, NP, nq)

    def body(q_ref, k_ref, v_ref, do_ref, t_ref, dq_ref, dk_ref, dv_ref, dkm_ref, dvm_ref):
        i = pl.program_id(2)

        @pl.when(i == 0)
        def _():
            dk_ref[...] = jnp.zeros_like(dk_ref)
            dv_ref[...] = jnp.zeros_like(dv_ref)
        qh = _split_heads(q_ref[...], SCALE)
        doh = _split_heads(do_ref[...])
        top = lax.broadcasted_iota(jnp.int32, (LANES, tq), 0) < HEAD
        zt = jnp.zeros((LANES, tq), BF16)
        qt = (q_ref[...].astype(F32) * SCALE).T.astype(BF16)
        dot_ = do_ref[...].astype(F32).T.astype(BF16)
        qth = (jnp.where(top, qt, zt), jnp.where(top, zt, qt))
        doth = (jnp.where(top, dot_, zt), jnp.where(top, zt, dot_))
        tt = t_ref[...]
        tot = (tt[:, 0:1], tt[:, HEAD:HEAD + 1])
        Urev = _tri(tq, lambda r, c: r > c)
        Uexc = _tri(tq, lambda r, c: r < c)

        def sweep(tiles, carry):
            nt = len(tiles)
            chains = [(t, h) for t in range(nt) for h in range(2)]
            rows = [pl.ds(pl.multiple_of(j * tq, tq), tq) for j, _ in tiles]
            ks = [k_ref[r, :] for r in rows]
            vs = [v_ref[r, :] for r in rows]
            terms = {(t, h): _sb_terms(qh[h], ks[t], tiles[t][1]) for t, h in chains}
            cc = [carry[h][0] for h in range(2)]
            later = {}
            for t, h in chains:
                cc[h] = cc[h] + jnp.sum(terms[t, h][1], axis=1, keepdims=True)
                later[t, h] = tot[h] - cc[h]
            cum = {ch: _dot(terms[ch][1].astype(BF16), Urev, NN) for ch in chains}
            da = {(t, h): _dot(doh[h], vs[t], NT) for t, h in chains}
            a, g, before = {}, {}, {}
            cg = [carry[h][1] for h in range(2)]
            for ch in chains:
                a[ch] = jnp.exp(terms[ch][0] + (cum[ch] + later[ch]))
                if tiles[ch[0]][1]:
                    a[ch] = jnp.where(terms[ch][2], a[ch], 0.0)
                g[ch] = a[ch] * da[ch]
                before[ch] = cg[ch[1]]
                cg[ch[1]] = cg[ch[1]] + jnp.sum(g[ch], axis=1, keepdims=True)
            G = {ch: _dot(g[ch].astype(BF16), Uexc, NN) for ch in chains}
            dz = {}
            for ch in chains:
                d = g[ch] - jnp.exp(terms[ch][0]) * (g[ch] + (G[ch] + before[ch]))
                if tiles[ch[0]][1]:
                    d = jnp.where(terms[ch][2], d, 0.0)
                dz[ch] = d.astype(BF16)
            dq = [carry[h][2] for h in range(2)]
            for t, h in chains:
                dq[h] = dq[h] + _dot(dz[t, h], ks[t], NN)
            for t in range(nt):
                dk_ref[:, rows[t]] += _dot(qth[0], dz[t, 0], NN) + _dot(qth[1], dz[t, 1], NN)
                dv_ref[:, rows[t]] += _dot(doth[0], a[t, 0].astype(BF16), NN) + _dot(doth[1], a[t, 1].astype(BF16), NN)
            return tuple((cc[h], cg[h], dq[h]) for h in range(2))

        zero = jnp.zeros((tq, 1), F32)
        zq = jnp.zeros((tq, LANES), F32)
        step = lambda n, cr: sweep([(SB_UNROLL_BWD * n + t, False) for t in range(SB_UNROLL_BWD)], cr)
        carry = lax.fori_loop(0, i // SB_UNROLL_BWD, step, ((zero, zero, zq), (zero, zero, zq)))
        tails = [lambda cr, k=k: sweep([(i - k + t, False) for t in range(k)] + [(i, True)], cr) for k in range(SB_UNROLL_BWD)]
        carry = _by_value(i % SB_UNROLL_BWD, tails, carry)
        dq_ref[...] = (jnp.where(_lo_mask((tq, LANES)), carry[0][2], carry[1][2]) * SCALE).astype(BF16)

        @pl.when(i == nq - 1)
        def _():
            for c in range(nq):
                cols = slice(c * tq, (c + 1) * tq)
                dkm_ref[cols, :] = dk_ref[:, cols].T.astype(BF16)
                dvm_ref[cols, :] = dv_ref[:, cols].T.astype(BF16)

    spec_q = pl.BlockSpec((tq, LANES), lambda b, p, i: (b * nq + i, p))
    spec_s = pl.BlockSpec((LANES, S), lambda b, p, i: (b * NP + p, 0))
    spec_m = pl.BlockSpec((S, LANES), lambda b, p, i: (b, p))
    key_side = jax.ShapeDtypeStruct((B * NQ, S), F32)
    token_major = jax.ShapeDtypeStruct((T, NQ), BF16)
    in_specs = [spec_q, pl.BlockSpec((S, LANES), lambda b, p, i: (b, NP + p)),
                pl.BlockSpec((S, LANES), lambda b, p, i: (b, 2 * NP + p)), spec_q, spec_q]
    out_specs = [spec_q, spec_s, spec_s, spec_m, spec_m]
    out_shape = [token_major, key_side, key_side, token_major, token_major]
    args = [qkv, qkv, qkv, do, tot]
    return _call_behind(body, name, grid, in_specs, out_specs, out_shape, args, exchange, True)


def _adamw(w, g, m, v, name):
    shape = w.shape
    cols = shape[-1]
    rows = math.prod(shape[:-1])
    tr = _pick(rows, max(8, (1 << 19) // max(cols, LANES) // 8 * 8), 8)

    def body(w_ref, g_ref, m_ref, v_ref, d_ref, mo_ref, vo_ref):
        gv = g_ref[...]
        mn = ADAM_B1 * m_ref[...] + (1.0 - ADAM_B1) * gv
        vn = ADAM_B2 * v_ref[...] + (1.0 - ADAM_B2) * (gv * gv)
        m_hat = mn / (1.0 - ADAM_B1 ** ADAM_STEP)
        v_hat = vn / (1.0 - ADAM_B2 ** ADAM_STEP)
        d_ref[...] = -ADAM_LR * (m_hat / (jnp.sqrt(v_hat) + ADAM_EPS) + ADAM_WD * w_ref[...])
        mo_ref[...] = mn
        vo_ref[...] = vn

    spec = pl.BlockSpec((tr, cols), lambda i: (i, 0))
    out = jax.ShapeDtypeStruct((rows, cols), F32)
    d, mn, vn = pl.pallas_call(
        body, name=name, grid=(rows // tr,),
        in_specs=[spec] * 4, out_specs=[spec] * 3, out_shape=[out] * 3,
        compiler_params=_params("parallel"),
    )(w.reshape(rows, cols), g.reshape(rows, cols), m.reshape(rows, cols), v.reshape(rows, cols))
    return d.reshape(shape), mn.reshape(shape), vn.reshape(shape)


def _pad_rows(a, rows):
    return jnp.pad(a, ((0, rows - a.shape[0]), (0, 0)))


def kernel(x, c, positions, ada_w, ada_b, norm1_g, norm2_g, wqkv_a, q_norm_a, k_norm_a, sinks_a, wo_a, wqkv_b, wo_b, w_gate, w_up, w_down, loss_target, m_ada_w, m_ada_b, m_norm1_g, m_norm2_g, m_wqkv_a, m_q_norm_a, m_k_norm_a, m_sinks_a, m_wo_a, m_wqkv_b, m_wo_b, m_w_gate, m_w_up, m_w_down, v_ada_w, v_ada_b, v_norm1_g, v_norm2_g, v_wqkv_a, v_q_norm_a, v_k_norm_a, v_sinks_a, v_wo_a, v_wqkv_b, v_wo_b, v_w_gate, v_w_up, v_w_down):
    B, S, D = x.shape
    T = B * S
    L = ada_w.shape[0]
    NA, NB_ = wqkv_a.shape[0], wqkv_b.shape[0]
    me = 4 * lax.axis_index("x") + 2 * lax.axis_index("y") + lax.axis_index("c")
    xt = x.reshape(T, D)

    col_sharded = {"qkv_a": wqkv_a, "qkv_b": wqkv_b, "gate": w_gate, "up": w_up}
    row_sharded = {"wo_a": wo_a, "wo_b": wo_b, "down": w_down}

    def shard_rows(key):
        kind, idx = key
        return col_sharded[kind][idx].T if kind in col_sharded else row_sharded[kind][idx]

    def layer_keys(l):
        mix = "a" if l % 2 == 0 else "b"
        return [("qkv_" + mix, l // 2), ("wo_" + mix, l // 2), ("gate", l), ("up", l), ("down", l)]

    def unpack(buf, keys, reshape):
        out, off = {}, 0
        for key in keys:
            rows = shard_rows(key).shape[0]
            out[key] = reshape(buf[..., off:off + rows, :], rows)
            off += rows
        return out

    first_b = 1
    keys_early = layer_keys(0)[:2]
    keys_mid = layer_keys(0)[2:] + [("qkv_b", 0)]
    keys_late = [k for l in range(1, L) for k in layer_keys(l) if k != ("qkv_b", 0)]
    pack = lambda keys: jnp.concatenate([shard_rows(k).astype(BF16) for k in keys], axis=0)
    full_rows = lambda b, rows: b.reshape(NDEV * rows, D)
    W = unpack(_all_gather(pack(keys_early), "ag_weights"), keys_early, full_rows)

    WA = ada_w.shape[2]
    c_all = _all_gather(c, "ag_c").reshape(NDEV * B, D)
    bias = lax.dynamic_slice_in_dim(ada_b, me * WA, WA, axis=1).reshape(L, 1, WA)
    mod_part = _ada_fwd(c_all, ada_w, bias, "ada_fwd")
    mod_all = _all_gather(mod_part.reshape(L * NDEV * B, WA), "ag_mod")
    mod_all = mod_all.reshape(NDEV, L, NDEV * B, WA).transpose(1, 2, 0, 3).reshape(L, NDEV * B, NDEV * WA)
    mod = lax.dynamic_slice_in_dim(mod_all, me * B, B, axis=1)
    mod = mod.reshape(L, B, 6, 1, D)
    sh1, sc1, g1, sh2, sc2, g2 = [mod[:, :, k] for k in range(6)]

    half = ROT // 2
    inv_freq = jnp.power(jnp.float32(ROPE_THETA), -jnp.arange(half, dtype=F32) * 2.0 / ROT)
    ang = positions.reshape(T, 1).astype(F32) * inv_freq[None, :]
    cos, sin = jnp.cos(ang), jnp.sin(ang)
    ones = jnp.ones((T, HEAD - ROT), F32)
    zeros = jnp.zeros((T, HEAD - ROT), F32)
    z8 = jnp.zeros((T, half), F32)
    cs = jnp.tile(jnp.concatenate([cos, cos, ones], axis=1), (1, 2))
    s1 = jnp.tile(jnp.concatenate([-sin, z8, zeros], axis=1), (1, 2))
    s2 = jnp.tile(jnp.concatenate([z8, sin, zeros], axis=1), (1, 2))

    saved = []
    xc = xt
    h1 = _norm_mod(xc, norm1_g[0:1], sc1[0], sh1[0], S, "norm1_0")
    for l in range(L):
        j = l // 2
        sv = dict(x_in=xc, h1=h1)
        if l % 2 == 0:
            qkv = _mm_nt(h1, W["qkv_a", j], F32, f"qkv_a_{l}")
            qg = jnp.tile(q_norm_a[j:j + 1], (1, 2))
            kg = jnp.tile(k_norm_a[j:j + 1], (1, 2))
            qn, kd, vd = _qk_prep(qkv, cs, s1, s2, qg, kg, f"qk_prep_{l}")
            sink2 = jnp.repeat(sinks_a[j].reshape(-1, 2), HEAD, axis=1).reshape(-1, 1, LANES)
            if l == 0:
                attn, lse, mid = _swa_fwd(qn, kd, vd, sink2, B, f"swa_fwd_{l}", gather=pack(keys_mid))
                W.update(unpack(mid, keys_mid, full_rows))
            else:
                attn, lse = _swa_fwd(qn, kd, vd, sink2, B, f"swa_fwd_{l}")
            sv.update(qkv=qkv, qg=qg, kg=kg, qn=qn, kd=kd, vd=vd, sink2=sink2, lse=lse)
            wo = W["wo_a", j]
        else:
            qkv = _mm_nt(h1, W["qkv_b", j], BF16, f"qkv_b_{l}")
            if l == first_b:
                attn, tot, late = _sb_fwd(qkv, B, f"sb_fwd_{l}", gather=pack(keys_late))
                W.update(unpack(late, keys_late, full_rows))
            else:
                attn, tot = _sb_fwd(qkv, B, f"sb_fwd_{l}")
            sv.update(qkv=qkv, tot=tot)
            wo = W["wo_b", j]
        y1, xm, h2 = _mm_res(attn, wo, xc, g1[l], S, f"attn_out_{l}", norm=(norm2_g[l:l + 1], sc2[l], sh2[l]))
        gate, up, act = _swiglu_fwd(h2, W["gate", l], W["up", l], f"swiglu_fwd_{l}")
        if l + 1 < L:
            y2, xc, h1 = _mm_res(act, W["down", l], xm, g2[l], S, f"mlp_out_{l}",
                                 norm=(norm1_g[l + 1:l + 2], sc1[l + 1], sh1[l + 1]))
        else:
            y2, xc = _mm_res(act, W["down", l], xm, g2[l], S, f"mlp_out_{l}")
        sv.update(attn=attn, y1=y1, x_mid=xm, h2=h2, gate=gate, up=up, act=act, y2=y2)
        saved.append(sv)

    dx, loss_tile = _loss_head(xc, loss_target.reshape(T, D), "loss_head")

    G = {}
    pack_grads = lambda keys: jnp.concatenate([G[k].reshape(NDEV, G[k].shape[0] // NDEV, D) for k in keys], axis=1)
    keys_hi = [k for l in range(first_b + 1, L) for k in layer_keys(l)] + layer_keys(first_b)[1:]
    keys_mlp0_g = [("qkv_b", 0), ("down", 0)]
    keys_mid_g = [("wo_a", 0), ("gate", 0), ("up", 0)]
    keys_lo = layer_keys(0)[:1]
    received_hi = received_mid = received_mlp0 = None
    dmod = [None] * L
    dn1, dn2 = [None] * L, [None] * L
    dqg, dkg, dsink = [None] * NA, [None] * NA, [None] * NA
    dy2, dg2 = _gate_bwd(dx, saved[L - 1]["y2"], g2[L - 1], S, "gate2_bwd_top")
    for l in reversed(range(L)):
        j = l // 2
        mix = "a" if l % 2 == 0 else "b"
        sv = saved[l]
        dgate, dup = _swiglu_bwd(dy2, W["down", l], sv["gate"], sv["up"], f"swiglu_bwd_{l}")
        G["down", l] = _mm_tn(sv["act"], dy2, f"dw_down_{l}")
        G["gate", l] = _mm_tn(dgate, sv["h2"], f"dw_gate_{l}")
        G["up", l] = _mm_tn(dup, sv["h2"], f"dw_up_{l}")
        n2 = _norm_mod_bwd(sv["x_mid"], [(dgate, W["gate", l]), (dup, W["up", l])], dx, norm2_g[l:l + 1], sc2[l], S,
                           f"norm2_bwd_{l}", below=(sv["y1"], g1[l]), exchange=pack_grads(keys_mlp0_g) if l == 0 else None)
        dxm, dsh2, dsc2, dn2[l], dy1, dg1 = n2[:6]
        if l == 0:
            received_mlp0 = n2[6]
        dattn = _mm_nt(dy1, W["wo_" + mix, j], BF16, f"dattn_{l}")
        G["wo_" + mix, j] = _mm_tn(sv["attn"], dy1, f"dw_o_{l}")
        if l % 2 == 0:
            swa_args = (sv["qn"], sv["kd"], sv["vd"], sv["sink2"], dattn, sv["lse"], B, f"swa_bwd_{l}")
            if l == 0:
                dq, dkc, dkp, dvc, dvp, dsink[j], received_mid = _swa_bwd(*swa_args, exchange=pack_grads(keys_mid_g))
            else:
                dq, dkc, dkp, dvc, dvp, dsink[j] = _swa_bwd(*swa_args)
            dqkv, dqg[j], dkg[j] = _qk_prep_bwd(sv["qkv"], cs, s1, s2, sv["qg"], sv["kg"], dq, dkc, dkp, dvc, dvp, B,
                                                f"qk_prep_bwd_{l}")
        else:
            nqb = sv["qkv"].shape[1] // 3
            sb_args = (sv["qkv"], dattn, sv["tot"], B, f"sb_bwd_{l}")
            if l == first_b and keys_hi:
                dq, _, _, dk, dv, received_hi = _sb_bwd(*sb_args, exchange=pack_grads(keys_hi))
            else:
                dq, _, _, dk, dv = _sb_bwd(*sb_args)
        wt = W["qkv_" + mix, j]
        if l % 2 == 0:
            dh1_pairs = [(dqkv, wt)]
            G["qkv_a", j] = _mm_tn(dqkv, sv["h1"], f"dw_qkv_{l}")
        else:
            parts = [dq, dk, dv]
            dh1_pairs = [(part, wt[k * nqb:(k + 1) * nqb]) for k, part in enumerate(parts)]
            G["qkv_b", j] = jnp.concatenate([_mm_tn(part, sv["h1"], f"dw_qkv_{l}_{k}") for k, part in enumerate(parts)], axis=0)
        n1_args = (sv["x_in"], dh1_pairs, dxm, norm1_g[l:l + 1], sc1[l], S, f"norm1_bwd_{l}")
        dmod_l = [None, None, dg1, dsh2, dsc2, dg2]
        if l > 0:
            dx, dmod_l[0], dmod_l[1], dn1[l], dy2, dg2 = _norm_mod_bwd(*n1_args, below=(saved[l - 1]["y2"], g2[l - 1]))
        else:
            dx, dmod_l[0], dmod_l[1], dn1[l] = _norm_mod_bwd(*n1_args)
        dmod[l] = jnp.concatenate(dmod_l, axis=1)
    grad_x = dx.reshape(B, S, D)

    ndm = L * 6
    dmod_rows = jnp.stack(dmod, axis=1).reshape(B * ndm, D)
    misc = jnp.concatenate(
        [jnp.concatenate(dn1, axis=0).reshape(B * L, D), jnp.concatenate(dn2, axis=0).reshape(B * L, D),
         _pad_rows(jnp.concatenate([jnp.pad(r, ((0, 0), (0, D - LANES))) for r in dqg + dkg]
                                   + [jnp.pad(r[:, 0, ::HEAD].reshape(1, -1), ((0, 0), (0, D - 2 * r.shape[0]))) for r in dsink]
                                   + [jnp.pad(loss_tile[0:1, 0:1], ((0, 0), (0, D - 1)))], axis=0), 8)], axis=0)
    nmisc = misc.shape[0]
    small = _all_gather(jnp.concatenate([dmod_rows, _pad_rows(misc, -(-nmisc // 8) * 8)], axis=0), "ag_small")
    dmod_all = small[:, :B * ndm].reshape(NDEV * B, ndm, D)
    g_ada_b = _sum_leading(dmod_all, "sum_dmod").reshape(L, 6 * D)
    misc_sum = _sum_leading(small[:, B * ndm:], "sum_misc")
    g_n1 = misc_sum[0:B * L].reshape(L, B, D)
    g_n2 = misc_sum[B * L:2 * B * L].reshape(L, B, D)
    g_norm1 = _sum_leading(g_n1.transpose(1, 0, 2), "sum_n1")
    g_norm2 = _sum_leading(g_n2.transpose(1, 0, 2), "sum_n2")
    o = 2 * B * L
    g_qn = misc_sum[o:o + NA, :HEAD]
    g_kn = misc_sum[o + NA:o + 2 * NA, :HEAD]
    nsink = sinks_a.shape[1]
    g_sink = misc_sum[o + 2 * NA:o + 3 * NA, :nsink]
    loss = misc_sum[o + 3 * NA, 0]

    dmod_loc = lax.dynamic_slice_in_dim(dmod_all.reshape(NDEV * B, L, 6 * D), me * WA, WA, axis=2)
    g_ada_w = _ada_bwd(c_all, dmod_loc.transpose(1, 0, 2), "ada_bwd")

    shard = unpack(_sum_leading(_exchange(pack_grads(keys_lo), "grad_exchange"), "grad_sum"), keys_lo, lambda b, rows: b)
    shard.update(unpack(_sum_leading(received_mid, "grad_sum_mid"), keys_mid_g, lambda b, rows: b))
    shard.update(unpack(_sum_leading(received_mlp0, "grad_sum_mlp0"), keys_mlp0_g, lambda b, rows: b))
    if received_hi is not None:
        shard.update(unpack(_sum_leading(received_hi, "grad_sum_hi"), keys_hi, lambda b, rows: b))

    def stacked(kind, n):
        return jnp.stack([shard[kind, i].T if kind in col_sharded else shard[kind, i] for i in range(n)])

    gw_qkv_a, gw_qkv_b, gw_gate, gw_up = stacked("qkv_a", NA), stacked("qkv_b", NB_), stacked("gate", L), stacked("up", L)
    gw_wo_a, gw_wo_b, gw_down = stacked("wo_a", NA), stacked("wo_b", NB_), stacked("down", L)

    grads = [g_ada_w, g_ada_b, g_norm1, g_norm2, gw_qkv_a, g_qn, g_kn, g_sink, gw_wo_a, gw_qkv_b, gw_wo_b,
             gw_gate, gw_up, gw_down]
    ws = [ada_w, ada_b, norm1_g, norm2_g, wqkv_a, q_norm_a, k_norm_a, sinks_a, wo_a, wqkv_b, wo_b, w_gate, w_up, w_down]
    ms = [m_ada_w, m_ada_b, m_norm1_g, m_norm2_g, m_wqkv_a, m_q_norm_a, m_k_norm_a, m_sinks_a, m_wo_a, m_wqkv_b,
          m_wo_b, m_w_gate, m_w_up, m_w_down]
    vs = [v_ada_w, v_ada_b, v_norm1_g, v_norm2_g, v_wqkv_a, v_q_norm_a, v_k_norm_a, v_sinks_a, v_wo_a, v_wqkv_b,
          v_wo_b, v_w_gate, v_w_up, v_w_down]
    deltas, new_m, new_v = [], [], []
    for k, (w, g, m, v) in enumerate(zip(ws, grads, ms, vs)):
        g = g.reshape(w.shape)
        d, mn, vn = _adamw(w, g, m, v, f"adamw_{k}")
        grads[k] = g
        deltas.append(d)
        new_m.append(mn)
        new_v.append(vn)
    return (loss, grad_x, *grads, *deltas, *new_m, *new_v)
```

```python
import functools
import math

import jax
import jax.numpy as jnp
from jax import lax
from jax.experimental import pallas as pl
from jax.experimental.pallas import tpu as pltpu

F32 = jnp.float32
BF16 = jnp.bfloat16
NDEV = 8
HEAD = 64
BLK = 128
LANES = 128
EPS = 1e-6
ROT = HEAD // 4
ROPE_THETA = 500000.0
SCALE = HEAD ** -0.5
LOG2E = math.log2(math.e)
NEG = -1e30
VMEM_LIMIT = 56 * 1024 * 1024
MESH = pl.DeviceIdType.MESH
HIGH = lax.Precision.HIGHEST

ADAM_LR = 0.001
ADAM_B1 = 0.9
ADAM_B2 = 0.999
ADAM_EPS = 1e-08
ADAM_WD = 0.01
ADAM_STEP = 10


def _params(*sem):
    return pltpu.CompilerParams(dimension_semantics=sem, vmem_limit_bytes=VMEM_LIMIT)


def _pick(n, cap, mult):
    if n <= cap:
        return n
    best = None
    for t in range(mult, cap + 1, mult):
        if n % t == 0:
            best = t
    assert best is not None, (n, cap, mult)
    return best


def _dot(a, b, dims, precision=None):
    return lax.dot_general(a, b, (dims, ((), ())), preferred_element_type=F32, precision=precision)


NN = ((1,), (0,))
NT = ((1,), (1,))
TN = ((0,), (0,))


def _all_gather(x, name):
    m, n = x.shape

    def body(x_ref, out_ref, send_sems, recv_sems, local_sem):
        ix, iy, ic = lax.axis_index("x"), lax.axis_index("y"), lax.axis_index("c")
        me, sibling = (ix, iy, ic), (ix, iy, 1 - ic)
        chips = [(1 - ix, iy), (ix, 1 - iy), (1 - ix, 1 - iy)]

        def slab(px, py, pc):
            return out_ref.at[4 * px + 2 * py + pc]

        def copy(k, block, to, src=None):
            return pltpu.make_async_remote_copy(
                src_ref=slab(*block) if src is None else src, dst_ref=slab(*block),
                send_sem=send_sems.at[k], recv_sem=recv_sems.at[k], device_id=to, device_id_type=MESH)

        mine = pltpu.make_async_copy(x_ref, slab(*me), local_sem)
        mine.start()
        first = [copy(0, me, sibling, src=x_ref)]
        first += [copy(1 + j, me, (*chip, ic), src=x_ref) for j, chip in enumerate(chips)]
        for cp in first:
            cp.start()
        passed = [copy(4 + j, (*chip, ic), sibling) for j, chip in enumerate(chips)]
        for j, chip in enumerate(chips):
            copy(1 + j, (*chip, ic), me).wait_recv()
            passed[j].start()
        copy(0, sibling, me).wait_recv()
        for j, chip in enumerate(chips):
            copy(4 + j, (*chip, 1 - ic), me).wait_recv()
        for cp in first + passed:
            cp.wait_send()
        mine.wait()

    return pl.pallas_call(
        body, name=name,
        out_shape=jax.ShapeDtypeStruct((NDEV, m, n), x.dtype),
        in_specs=[pl.BlockSpec(memory_space=pl.ANY)],
        out_specs=pl.BlockSpec(memory_space=pl.ANY),
        scratch_shapes=[pltpu.SemaphoreType.DMA((7,)), pltpu.SemaphoreType.DMA((7,)), pltpu.SemaphoreType.DMA(())],
    )(x)


COMM_SEMS = [pltpu.SemaphoreType.DMA((NDEV - 1,)), pltpu.SemaphoreType.DMA((NDEV - 1,)), pltpu.SemaphoreType.DMA(())]
HBM_SPEC = pl.BlockSpec(memory_space=pl.ANY)


def _direct_copies(src_ref, dst_ref, sems, scatter):
    send_sems, recv_sems, own_sem = sems
    ix, iy, ic = lax.axis_index("x"), lax.axis_index("y"), lax.axis_index("c")
    me = 4 * ix + 2 * iy + ic
    copies = [pltpu.make_async_copy(src_ref.at[me] if scatter else src_ref, dst_ref.at[me], own_sem)]
    for k in range(1, NDEV):
        px = 1 - ix if k & 4 else ix
        py = 1 - iy if k & 2 else iy
        pc = 1 - ic if k & 1 else ic
        copies.append(pltpu.make_async_remote_copy(
            src_ref=src_ref.at[4 * px + 2 * py + pc] if scatter else src_ref, dst_ref=dst_ref.at[me],
            send_sem=send_sems.at[k - 1], recv_sem=recv_sems.at[k - 1],
            device_id=(px, py, pc), device_id_type=MESH))
    return copies


def _exchange(p, name):
    def body(p_ref, r_ref, *sems):
        copies = _direct_copies(p_ref, r_ref, sems, True)
        for cp in copies:
            cp.start()
        for cp in copies:
            cp.wait()

    return pl.pallas_call(
        body, name=name,
        out_shape=jax.ShapeDtypeStruct(p.shape, p.dtype),
        in_specs=[HBM_SPEC], out_specs=HBM_SPEC, scratch_shapes=COMM_SEMS,
    )(p)


def _call_behind(body, name, grid, in_specs, out_specs, out_shape, args, payload=None, scatter=False):
    params = _params(*["arbitrary"] * len(grid))
    if payload is None:
        return pl.pallas_call(body, name=name, grid=grid, in_specs=in_specs, out_specs=out_specs, out_shape=out_shape,
                              compiler_params=params)(*args)
    n_in, n_out = len(in_specs), len(out_specs)

    def edge(first):
        ids = [pl.program_id(a) for a in range(len(grid))]
        return functools.reduce(lambda u, v: u & v, [i == (0 if first else d - 1) for i, d in zip(ids, grid)])

    def wrapped(*refs):
        x_ref, r_ref, sems = refs[n_in], refs[n_in + 1 + n_out], refs[n_in + n_out + 2:]

        @pl.when(edge(True))
        def _():
            for cp in _direct_copies(x_ref, r_ref, sems, scatter):
                cp.start()
        body(*refs[:n_in], *refs[n_in + 1:n_in + 1 + n_out])

        @pl.when(edge(False))
        def _():
            for cp in _direct_copies(x_ref, r_ref, sems, scatter):
                cp.wait()

    arrived = jax.ShapeDtypeStruct(payload.shape if scatter else (NDEV,) + payload.shape, payload.dtype)
    return pl.pallas_call(
        wrapped, name=name, grid=grid, in_specs=list(in_specs) + [HBM_SPEC], out_specs=list(out_specs) + [HBM_SPEC],
        out_shape=list(out_shape) + [arrived], scratch_shapes=COMM_SEMS, compiler_params=params,
    )(*args, payload)


def _sum_leading(r, name):
    k, m, n = r.shape
    mult = 8 * (4 // r.dtype.itemsize)
    tm = _pick(m, max(mult, (4 * 1024 * 1024) // (k * n * r.dtype.itemsize) // mult * mult), mult)

    def body(r_ref, o_ref):
        acc = r_ref[0].astype(F32)
        for s in range(1, k):
            acc = acc + r_ref[s].astype(F32)
        o_ref[...] = acc

    return pl.pallas_call(
        body, name=name, grid=(m // tm,),
        in_specs=[pl.BlockSpec((k, tm, n), lambda i: (0, i, 0))],
        out_specs=pl.BlockSpec((tm, n), lambda i: (i, 0)),
        out_shape=jax.ShapeDtypeStruct((m, n), F32),
        compiler_params=_params("parallel"),
    )(r)


def _mm_nt(a, bt, out_dtype, name):
    M, K = a.shape
    N = bt.shape[0]
    tm, tn = _pick(M, 1024, 8), _pick(N, 1536, LANES)

    def body(a_ref, b_ref, o_ref):
        o_ref[...] = _dot(a_ref[...], b_ref[...], NT).astype(out_dtype)

    return pl.pallas_call(
        body, name=name, grid=(N // tn, M // tm),
        in_specs=[pl.BlockSpec((tm, K), lambda j, i: (i, 0)), pl.BlockSpec((tn, K), lambda j, i: (j, 0))],
        out_specs=pl.BlockSpec((tm, tn), lambda j, i: (i, j)),
        out_shape=jax.ShapeDtypeStruct((M, N), out_dtype),
        compiler_params=_params("parallel", "parallel"),
    )(a, bt)


def _mm_tn(a, b, name):
    M, N1 = a.shape
    N2 = b.shape[1]
    t1, tk = _pick(N1, 1536, LANES), _pick(M, 2048, 8)
    nk = M // tk

    def body(a_ref, b_ref, o_ref, acc_ref):
        k = pl.program_id(1)

        @pl.when(k == 0)
        def _():
            acc_ref[...] = jnp.zeros_like(acc_ref)
        acc_ref[...] += _dot(a_ref[...], b_ref[...], TN)

        @pl.when(k == nk - 1)
        def _():
            o_ref[...] = acc_ref[...].astype(BF16)

    return pl.pallas_call(
        body, name=name, grid=(N1 // t1, nk),
        in_specs=[pl.BlockSpec((tk, t1), lambda i, k: (k, i)), pl.BlockSpec((tk, N2), lambda i, k: (k, 0))],
        out_specs=pl.BlockSpec((t1, N2), lambda i, k: (i, 0)),
        out_shape=jax.ShapeDtypeStruct((N1, N2), BF16),
        scratch_shapes=[pltpu.VMEM((t1, N2), F32)],
        compiler_params=_params("parallel", "arbitrary"),
    )(a, b)


def _norm_mod_rows(xv, gain, sc, sh):
    r = lax.rsqrt(jnp.mean(xv * xv, axis=-1, keepdims=True) + EPS)
    return ((xv * r) * gain * (1.0 + sc) + sh).astype(BF16)


def _mm_res(a, w, x, gate, S, name, norm=None):
    T, K = a.shape
    D = w.shape[1]
    tm = _pick(S, 512, 8)
    nb = S // tm

    def body(a_ref, w_ref, x_ref, g_ref, *rest):
        y = _dot(a_ref[...], w_ref[...], NN)
        xn = x_ref[...] + g_ref[0] * y
        if norm is None:
            y_ref, o_ref = rest
        else:
            gain_ref, sc_ref, sh_ref, y_ref, o_ref, h_ref = rest
            h_ref[...] = _norm_mod_rows(xn, gain_ref[...], sc_ref[0], sh_ref[0])
        y_ref[...] = y.astype(BF16)
        o_ref[...] = xn

    spec_t = pl.BlockSpec((tm, D), lambda i: (i, 0))
    spec_b = pl.BlockSpec((1, 1, D), lambda i: (i // nb, 0, 0))
    in_specs = [pl.BlockSpec((tm, K), lambda i: (i, 0)), pl.BlockSpec((K, D), lambda i: (0, 0)), spec_t, spec_b]
    out_specs = [spec_t, spec_t]
    out_shape = [jax.ShapeDtypeStruct((T, D), BF16), jax.ShapeDtypeStruct((T, D), F32)]
    args = [a, w, x, gate]
    if norm is not None:
        in_specs += [pl.BlockSpec((1, D), lambda i: (0, 0)), spec_b, spec_b]
        out_specs.append(spec_t)
        out_shape.append(jax.ShapeDtypeStruct((T, D), BF16))
        args += list(norm)
    return pl.pallas_call(
        body, name=name, grid=(T // tm,), in_specs=in_specs, out_specs=out_specs, out_shape=out_shape,
        compiler_params=_params("parallel"),
    )(*args)


def _swiglu_fwd(h, wgt, wut, name):
    T, D = h.shape
    F = wgt.shape[0]
    tm, tn = _pick(T, 512, 8), _pick(F, 1536, LANES)

    def body(h_ref, g_ref, u_ref, go_ref, uo_ref, a_ref):
        hh = h_ref[...]
        g = _dot(hh, g_ref[...], NT)
        u = _dot(hh, u_ref[...], NT)
        go_ref[...] = g.astype(BF16)
        uo_ref[...] = u.astype(BF16)
        a_ref[...] = (g * jax.nn.sigmoid(g) * u).astype(BF16)

    spec_w = pl.BlockSpec((tn, D), lambda j, i: (j, 0))
    spec_o = pl.BlockSpec((tm, tn), lambda j, i: (i, j))
    out = jax.ShapeDtypeStruct((T, F), BF16)
    return pl.pallas_call(
        body, name=name, grid=(F // tn, T // tm),
        in_specs=[pl.BlockSpec((tm, D), lambda j, i: (i, 0)), spec_w, spec_w],
        out_specs=[spec_o, spec_o, spec_o],
        out_shape=[out, out, out],
        compiler_params=_params("parallel", "parallel"),
    )(h, wgt, wut)


def _swiglu_bwd(dy, wd, gate, up, name):
    T, D = dy.shape
    F = wd.shape[0]
    tm, tn = _pick(T, 512, 8), _pick(F, 1536, LANES)

    halves = [slice(0, tn // 2), slice(tn // 2, tn)] if tn % (2 * LANES) == 0 else [slice(0, tn)]

    def body(dy_ref, w_ref, g_ref, u_ref, dg_ref, du_ref):
        das = [_dot(dy_ref[...], w_ref[sl, :], NT) for sl in halves]
        for sl, da in zip(halves, das):
            g = g_ref[:, sl].astype(F32)
            sg = jax.nn.sigmoid(g)
            t = da * sg
            du_ref[:, sl] = (t * g).astype(BF16)
            dg_ref[:, sl] = (t * u_ref[:, sl].astype(F32) * (1.0 + g * (1.0 - sg))).astype(BF16)

    spec_o = pl.BlockSpec((tm, tn), lambda j, i: (i, j))
    return pl.pallas_call(
        body, name=name, grid=(F // tn, T // tm),
        in_specs=[pl.BlockSpec((tm, D), lambda j, i: (i, 0)), pl.BlockSpec((tn, D), lambda j, i: (j, 0)), spec_o, spec_o],
        out_specs=[spec_o, spec_o],
        out_shape=[jax.ShapeDtypeStruct((T, F), BF16), jax.ShapeDtypeStruct((T, F), BF16)],
        compiler_params=_params("parallel", "parallel"),
    )(dy, wd, gate, up)


def _norm_mod(x, gain, sc, sh, S, name):
    T, D = x.shape
    tm = _pick(S, 512, 8)
    nb = S // tm

    def body(x_ref, g_ref, sc_ref, sh_ref, o_ref):
        o_ref[...] = _norm_mod_rows(x_ref[...], g_ref[...], sc_ref[0], sh_ref[0])

    spec_b = pl.BlockSpec((1, 1, D), lambda i: (i // nb, 0, 0))
    return pl.pallas_call(
        body, name=name, grid=(T // tm,),
        in_specs=[pl.BlockSpec((tm, D), lambda i: (i, 0)), pl.BlockSpec((1, D), lambda i: (0, 0)), spec_b, spec_b],
        out_specs=pl.BlockSpec((tm, D), lambda i: (i, 0)),
        out_shape=jax.ShapeDtypeStruct((T, D), BF16),
        compiler_params=_params("parallel"),
    )(x, gain, sc, sh)


def _norm_mod_bwd(x, pairs, dres, gain, sc, S, name, below=None, exchange=None):
    T, D = x.shape
    B = T // S
    tm = _pick(S, 512, 8)
    nb = S // tm
    n_mm = 2 * len(pairs)

    def body(*refs):
        mm, (x_ref, dr_ref, g_ref, sc_ref), rest = refs[:n_mm], refs[n_mm:n_mm + 4], refs[n_mm + 4:]
        if below is None:
            o_ref, dsh_ref, dsc_ref, dg_ref = rest
            sums = [dsh_ref, dsc_ref, dg_ref]
        else:
            y_ref, gt_ref, o_ref, dsh_ref, dsc_ref, dg_ref, dy_ref, dgt_ref = rest
            sums = [dsh_ref, dsc_ref, dg_ref, dgt_ref]

        @pl.when(pl.program_id(1) == 0)
        def _():
            for ref in sums:
                ref[...] = jnp.zeros_like(ref)
        dhv = _dot(mm[0][...], mm[1][...], NN)
        for p in range(2, n_mm, 2):
            dhv = dhv + _dot(mm[p][...], mm[p + 1][...], NN)
        xv, g = x_ref[...], g_ref[...]
        r = lax.rsqrt(jnp.mean(xv * xv, axis=-1, keepdims=True) + EPS)
        xhat = xv * r
        dsh_ref[0] += jnp.sum(dhv, axis=0, keepdims=True)
        dsc_ref[0] += jnp.sum(dhv * (xhat * g), axis=0, keepdims=True)
        dn = dhv * (1.0 + sc_ref[0])
        dg_ref[0] += jnp.sum(dn * xhat, axis=0, keepdims=True)
        dxh = dn * g
        out = dr_ref[...] + r * (dxh - xhat * jnp.mean(dxh * xhat, axis=-1, keepdims=True))
        o_ref[...] = out
        if below is not None:
            dy_ref[...] = (out * gt_ref[0]).astype(BF16)
            dgt_ref[0] += jnp.sum(out * y_ref[...].astype(F32), axis=0, keepdims=True)

    spec_t = pl.BlockSpec((tm, D), lambda b, i: (b * nb + i, 0))
    spec_b = pl.BlockSpec((1, 1, D), lambda b, i: (b, 0, 0))
    red = jax.ShapeDtypeStruct((B, 1, D), F32)
    in_specs, args = [], []
    for a, w in pairs:
        K = a.shape[1]
        in_specs += [pl.BlockSpec((tm, K), lambda b, i: (b * nb + i, 0)),
                     pl.BlockSpec((K, D), lambda b, i: (0, 0), pipeline_mode=pl.Buffered(1))]
        args += [a, w]
    in_specs += [spec_t, spec_t, pl.BlockSpec((1, D), lambda b, i: (0, 0)), spec_b]
    args += [x, dres, gain, sc]
    out_specs = [spec_t, spec_b, spec_b, spec_b]
    out_shape = [jax.ShapeDtypeStruct((T, D), F32), red, red, red]
    if below is not None:
        in_specs += [spec_t, spec_b]
        out_specs += [spec_t, spec_b]
        out_shape += [jax.ShapeDtypeStruct((T, D), BF16), red]
        args += list(below)
    return _call_behind(body, name, (B, nb), in_specs, out_specs, out_shape, args, exchange, True)


def _gate_bwd(dx, y, gate, S, name):
    T, D = dx.shape
    B = T // S
    tm = _pick(S, 512, 8)
    nb = S // tm

    def body(dx_ref, y_ref, g_ref, dy_ref, dg_ref):
        @pl.when(pl.program_id(1) == 0)
        def _():
            dg_ref[...] = jnp.zeros_like(dg_ref)
        d = dx_ref[...]
        dy_ref[...] = (d * g_ref[0]).astype(BF16)
        dg_ref[0] += jnp.sum(d * y_ref[...].astype(F32), axis=0, keepdims=True)

    spec_t = pl.BlockSpec((tm, D), lambda b, i: (b * nb + i, 0))
    spec_b = pl.BlockSpec((1, 1, D), lambda b, i: (b, 0, 0))
    return pl.pallas_call(
        body, name=name, grid=(B, nb),
        in_specs=[spec_t, spec_t, spec_b],
        out_specs=[spec_t, spec_b],
        out_shape=[jax.ShapeDtypeStruct((T, D), BF16), jax.ShapeDtypeStruct((B, 1, D), F32)],
        compiler_params=_params("parallel", "arbitrary"),
    )(dx, y, gate)


def _loss_head(y, target, name):
    T, D = y.shape
    tm = _pick(T, 512, 8)

    def body(y_ref, t_ref, dy_ref, l_ref):
        @pl.when(pl.program_id(0) == 0)
        def _():
            l_ref[...] = jnp.zeros_like(l_ref)
        e = y_ref[...] - t_ref[...]
        dy_ref[...] = e * (1.0 / D)
        l_ref[...] += 0.5 * jnp.sum(jnp.mean(e * e, axis=-1, keepdims=True), axis=0, keepdims=True)

    spec = pl.BlockSpec((tm, D), lambda i: (i, 0))
    return pl.pallas_call(
        body, name=name, grid=(T // tm,),
        in_specs=[spec, spec],
        out_specs=[spec, pl.BlockSpec((8, LANES), lambda i: (0, 0))],
        out_shape=[jax.ShapeDtypeStruct((T, D), F32), jax.ShapeDtypeStruct((8, LANES), F32)],
        compiler_params=_params("arbitrary"),
    )(y, target)


def _ada_fwd(c_all, ada_w, bias, name):
    NB, D = c_all.shape
    L, _, W = ada_w.shape

    def body(c_ref, w_ref, b_ref, o_ref):
        cv = c_ref[...]
        cond = cv * jax.nn.sigmoid(cv)
        o_ref[0] = _dot(cond, w_ref[0], NN, HIGH) + b_ref[0]

    return pl.pallas_call(
        body, name=name, grid=(L,),
        in_specs=[pl.BlockSpec((NB, D), lambda l: (0, 0)), pl.BlockSpec((1, D, W), lambda l: (l, 0, 0)),
                  pl.BlockSpec((1, 1, W), lambda l: (l, 0, 0))],
        out_specs=pl.BlockSpec((1, NB, W), lambda l: (l, 0, 0)),
        out_shape=jax.ShapeDtypeStruct((L, NB, W), F32),
        compiler_params=_params("parallel"),
    )(c_all, ada_w, bias)


def _ada_bwd(c_all, dmod, name):
    NB, D = c_all.shape
    L, _, W = dmod.shape

    def body(c_ref, d_ref, o_ref):
        cv = c_ref[...]
        cond = cv * jax.nn.sigmoid(cv)
        o_ref[0] = _dot(cond, d_ref[0], TN, HIGH)

    return pl.pallas_call(
        body, name=name, grid=(L,),
        in_specs=[pl.BlockSpec((NB, D), lambda l: (0, 0)), pl.BlockSpec((1, NB, W), lambda l: (l, 0, 0))],
        out_specs=pl.BlockSpec((1, D, W), lambda l: (l, 0, 0)),
        out_shape=jax.ShapeDtypeStruct((L, D, W), F32),
        compiler_params=_params("parallel"),
    )(c_all, dmod)


def _lo_mask(shape):
    return lax.broadcasted_iota(jnp.int32, shape, len(shape) - 1) < HEAD


def _head_sum_matrix():
    r = lax.broadcasted_iota(jnp.int32, (LANES, LANES), 0) // HEAD
    c = lax.broadcasted_iota(jnp.int32, (LANES, LANES), 1) // HEAD
    return (r == c).astype(BF16)


def _head_sum(x, P):
    hi = x.astype(BF16)
    lo = (x - hi.astype(F32)).astype(BF16)
    return _dot(hi, P, NN) + _dot(lo, P, NN)


def _rope(y, cs, s1, s2):
    return y * cs + pltpu.roll(y, LANES - ROT // 2, 1) * s1 + pltpu.roll(y, ROT // 2, 1) * s2


def _rope_bwd(d, cs, s1, s2):
    return d * cs + pltpu.roll(d * s1, ROT // 2, 1) + pltpu.roll(d * s2, LANES - ROT // 2, 1)


def _qk_prep(qkv, cs, s1, s2, qg, kg, name):
    T, W = qkv.shape
    NQ = W - 2 * LANES
    tm = _pick(T, 512, 8)

    def body(x_ref, cs_ref, s1_ref, s2_ref, qg_ref, kg_ref, q_ref, k_ref, v_ref):
        P = _head_sum_matrix()
        cs_, s1_, s2_ = cs_ref[...], s1_ref[...], s2_ref[...]
        lo = _lo_mask((tm, LANES))

        def norm_rope(xv, g):
            ms = _head_sum(xv * xv, P) * (1.0 / HEAD)
            return _rope(xv * lax.rsqrt(ms + EPS) * g, cs_, s1_, s2_)

        for j in range(NQ // LANES):
            q_ref[:, j * LANES:(j + 1) * LANES] = norm_rope(x_ref[:, j * LANES:(j + 1) * LANES], qg_ref[...]).astype(BF16)
        kr = norm_rope(x_ref[:, NQ:NQ + LANES], kg_ref[...])
        ks = pltpu.roll(kr, HEAD, 1)
        k_ref[:, :LANES] = jnp.where(lo, kr, ks).astype(BF16)
        k_ref[:, LANES:] = jnp.where(lo, ks, kr).astype(BF16)
        vr = x_ref[:, NQ + LANES:]
        vs = pltpu.roll(vr, HEAD, 1)
        v_ref[:, :LANES] = jnp.where(lo, vr, vs).astype(BF16)
        v_ref[:, LANES:] = jnp.where(lo, vs, vr).astype(BF16)

    spec_t = pl.BlockSpec((tm, LANES), lambda i: (i, 0))
    spec_g = pl.BlockSpec((1, LANES), lambda i: (0, 0))
    return pl.pallas_call(
        body, name=name, grid=(T // tm,),
        in_specs=[pl.BlockSpec((tm, W), lambda i: (i, 0)), spec_t, spec_t, spec_t, spec_g, spec_g],
        out_specs=[pl.BlockSpec((tm, NQ), lambda i: (i, 0)), pl.BlockSpec((tm, 2 * LANES), lambda i: (i, 0)),
                   pl.BlockSpec((tm, 2 * LANES), lambda i: (i, 0))],
        out_shape=[jax.ShapeDtypeStruct((T, NQ), BF16), jax.ShapeDtypeStruct((T, 2 * LANES), BF16),
                   jax.ShapeDtypeStruct((T, 2 * LANES), BF16)],
        compiler_params=_params("parallel"),
    )(qkv, cs, s1, s2, qg, kg)


def _stack_heads(x2):
    lo = _lo_mask(x2.shape)
    z = jnp.zeros_like(x2)
    return jnp.concatenate([jnp.where(lo, x2, z), jnp.where(lo, z, x2)], axis=0)


def _unstack_heads(xs):
    r = xs.shape[0] // 2
    return jnp.where(_lo_mask((r, LANES)), xs[:r], xs[r:])


def _swa_valid(i):
    qo = lax.broadcasted_iota(jnp.int32, (2 * BLK, 2 * BLK), 0) % BLK
    kc_ = lax.broadcasted_iota(jnp.int32, (2 * BLK, 2 * BLK), 1)
    rel = qo + BLK - kc_
    return (rel >= 0) & (rel < BLK) & ((kc_ >= BLK) | (i > 0))


def _swa_scores(q2, kk, sink2, valid):
    qs = _stack_heads(q2) * SCALE
    s = _dot(qs, kk, NT)
    sk = jnp.concatenate([jnp.broadcast_to(sink2[:, 0:1], (BLK, 1)), jnp.broadcast_to(sink2[:, HEAD:HEAD + 1], (BLK, 1))], axis=0)
    return qs, jnp.where(valid, s, NEG), sk


def _swa_fwd(q, kd, vd, sink2, B, name, gather=None):
    T, NQ = q.shape
    NP = NQ // LANES
    nq = T // B // BLK
    NG = kd.shape[1] // LANES
    grp = NP // NG

    def body(q_ref, kp_ref, kc_ref, vp_ref, vc_ref, s_ref, o_ref, l_ref):
        valid = _swa_valid(pl.program_id(2))
        kk = jnp.concatenate([kp_ref[...], kc_ref[...]], axis=0)
        vs = _stack_heads(jnp.concatenate([vp_ref[...], vc_ref[...]], axis=0))
        sls = [slice(jj * LANES, (jj + 1) * LANES) for jj in range(grp)]
        sc = [_swa_scores(q_ref[:, sl], kk, s_ref[jj], valid) for jj, sl in enumerate(sls)]
        ms = [jnp.maximum(jnp.max(s, axis=1, keepdims=True), sk) for _, s, sk in sc]
        ps = [jnp.exp(s - m) for (_, s, _), m in zip(sc, ms)]
        ls = [jnp.sum(p, axis=1, keepdims=True) + jnp.exp(sk - m) for p, (_, _, sk), m in zip(ps, sc, ms)]
        ps = [(p * (1.0 / l)).astype(BF16) for p, l in zip(ps, ls)]
        os_ = [_dot(jnp.concatenate([p[:BLK], p[BLK:]], axis=1), vs, NN) for p in ps]
        for sl, o, m, l in zip(sls, os_, ms, ls):
            o_ref[:, sl] = o.astype(BF16)
            l_ref[:, sl] = _unstack_heads(jnp.broadcast_to(m + jnp.log(l), (2 * BLK, LANES)))

    spec_q = pl.BlockSpec((BLK, grp * LANES), lambda b, g, i: (b * nq + i, g))
    spec_p = pl.BlockSpec((BLK, LANES), lambda b, g, i: (b * nq + jnp.maximum(i - 1, 0), g))
    spec_c = pl.BlockSpec((BLK, LANES), lambda b, g, i: (b * nq + i, g))
    in_specs = [spec_q, spec_p, spec_c, spec_p, spec_c, pl.BlockSpec((grp, 1, LANES), lambda b, g, i: (g, 0, 0))]
    out_shape = [jax.ShapeDtypeStruct((T, NQ), BF16), jax.ShapeDtypeStruct((T, NQ), F32)]
    return _call_behind(body, name, (B, NG, nq), in_specs, [spec_q, spec_q], out_shape, [q, kd, kd, vd, vd, sink2], gather, False)


def _swa_bwd(q, kd, vd, sink2, do, lse, B, name, exchange=None):
    T, NQ = q.shape
    NP = NQ // LANES
    nq = T // B // BLK
    NG = kd.shape[1] // LANES
    grp = NP // NG

    def body(q_ref, kp_ref, kc_ref, vp_ref, vc_ref, s_ref, do_ref, l_ref,
             dq_ref, dkc_ref, dkp_ref, dvc_ref, dvp_ref, ds_ref):
        b, i = pl.program_id(1), pl.program_id(2)

        @pl.when((b == 0) & (i == 0))
        def _():
            ds_ref[...] = jnp.zeros_like(ds_ref)
        valid = _swa_valid(i)
        kk = jnp.concatenate([kp_ref[...], kc_ref[...]], axis=0)
        vv = jnp.concatenate([vp_ref[...], vc_ref[...]], axis=0)
        sls = [slice(jj * LANES, (jj + 1) * LANES) for jj in range(grp)]
        sc = [_swa_scores(q_ref[:, sl], kk, s_ref[jj], valid) for jj, sl in enumerate(sls)]
        dos = [_stack_heads(do_ref[:, sl]) for sl in sls]
        dps = [_dot(d, vv, NT) for d in dos]
        lses = [jnp.concatenate([l_ref[:, sl][:, 0:1], l_ref[:, sl][:, HEAD:HEAD + 1]], axis=0) for sl in sls]
        ps = [jnp.exp(s - lse) for (_, s, _), lse in zip(sc, lses)]
        deltas = [jnp.sum(p * dp, axis=1, keepdims=True) for p, dp in zip(ps, dps)]
        dscs = [(p * (dp - delta)).astype(BF16) for p, dp, delta in zip(ps, dps, deltas)]
        dqs = [_dot(dsc, kk, NN) for dsc in dscs]
        dk = jnp.zeros((2 * BLK, LANES), F32)
        dv = jnp.zeros((2 * BLK, LANES), F32)
        for jj, sl in enumerate(sls):
            dsk = -jnp.exp(sc[jj][2] - lses[jj]) * deltas[jj]
            dsk_lo = jnp.sum(dsk[:BLK], axis=0, keepdims=True)
            dsk_hi = jnp.sum(dsk[BLK:], axis=0, keepdims=True)
            ds_ref[jj] += jnp.where(_lo_mask((1, LANES)), dsk_lo, dsk_hi)
            dq_ref[:, sl] = _unstack_heads(dqs[jj]) * SCALE
            dk = dk + _dot(dscs[jj], sc[jj][0], TN)
            dv = dv + _dot(ps[jj].astype(BF16), dos[jj], TN)
        dkp_ref[...] = dk[:BLK]
        dkc_ref[...] = dk[BLK:]
        dvp_ref[...] = dv[:BLK]
        dvc_ref[...] = dv[BLK:]

    spec_q = pl.BlockSpec((BLK, grp * LANES), lambda g, b, i: (b * nq + i, g))
    spec_p = pl.BlockSpec((BLK, LANES), lambda g, b, i: (b * nq + jnp.maximum(i - 1, 0), g))
    spec_c = pl.BlockSpec((BLK, LANES), lambda g, b, i: (b * nq + i, g))
    spec_s = pl.BlockSpec((grp, 1, LANES), lambda g, b, i: (g, 0, 0))
    kv = jax.ShapeDtypeStruct((T, NG * LANES), F32)
    in_specs = [spec_q, spec_p, spec_c, spec_p, spec_c, spec_s, spec_q, spec_q]
    out_specs = [spec_q, spec_c, spec_c, spec_c, spec_c, spec_s]
    out_shape = [jax.ShapeDtypeStruct((T, NQ), F32), kv, kv, kv, kv, jax.ShapeDtypeStruct((NP, 1, LANES), F32)]
    return _call_behind(body, name, (NG, B, nq), in_specs, out_specs, out_shape, [q, kd, kd, vd, vd, sink2, do, lse],
                        exchange, True)


def _qk_prep_bwd(qkv, cs, s1, s2, qg, kg, dq, dkc, dkp, dvc, dvp, B, name):
    T, W = qkv.shape
    NQ = W - 2 * LANES
    NP = NQ // LANES
    nq = T // B // BLK

    def body(x_ref, cs_ref, s1_ref, s2_ref, qg_ref, kg_ref, dq_ref, dkc_ref, dkp_ref, dvc_ref, dvp_ref,
             o_ref, dqg_ref, dkg_ref):
        b, i = pl.program_id(0), pl.program_id(1)

        @pl.when((b == 0) & (i == 0))
        def _():
            dqg_ref[...] = jnp.zeros_like(dqg_ref)
            dkg_ref[...] = jnp.zeros_like(dkg_ref)
        P = _head_sum_matrix()
        cs_, s1_, s2_ = cs_ref[...], s1_ref[...], s2_ref[...]
        lo = _lo_mask((BLK, LANES))
        has_next = (i + 1 < nq).astype(F32)

        def norm_rope_bwd(xv, g, d):
            du = _rope_bwd(d, cs_, s1_, s2_)
            r = lax.rsqrt(_head_sum(xv * xv, P) * (1.0 / HEAD) + EPS)
            xhat = xv * r
            dgain = jnp.sum(du * xhat, axis=0, keepdims=True)
            uu = du * g
            dx = r * (uu - xhat * (_head_sum(uu * xhat, P) * (1.0 / HEAD)))
            return dx, dgain + pltpu.roll(dgain, HEAD, 1)

        dqg = jnp.zeros((1, LANES), F32)
        for j in range(NP):
            sl = slice(j * LANES, (j + 1) * LANES)
            dx, dg = norm_rope_bwd(x_ref[:, sl], qg_ref[...], dq_ref[:, sl])
            o_ref[:, sl] = dx.astype(BF16)
            dqg = dqg + dg
        dqg_ref[...] += dqg

        def fold(c_ref, p_ref, g):
            sl = slice(g * LANES, (g + 1) * LANES)
            t = c_ref[:, sl] + has_next * p_ref[:, sl]
            return t + pltpu.roll(t, HEAD, 1)

        dk = jnp.where(lo, fold(dkc_ref, dkp_ref, 0), fold(dkc_ref, dkp_ref, 1))
        dx, dg = norm_rope_bwd(x_ref[:, NQ:NQ + LANES], kg_ref[...], dk)
        o_ref[:, NQ:NQ + LANES] = dx.astype(BF16)
        dkg_ref[...] += dg
        dv = jnp.where(lo, fold(dvc_ref, dvp_ref, 0), fold(dvc_ref, dvp_ref, 1))
        o_ref[:, NQ + LANES:] = dv.astype(BF16)

    spec_t = pl.BlockSpec((BLK, LANES), lambda b, i: (b * nq + i, 0))
    spec_g = pl.BlockSpec((1, LANES), lambda b, i: (0, 0))
    spec_c = pl.BlockSpec((BLK, 2 * LANES), lambda b, i: (b * nq + i, 0))
    spec_n = pl.BlockSpec((BLK, 2 * LANES), lambda b, i: (b * nq + jnp.minimum(i + 1, nq - 1), 0))
    row = jax.ShapeDtypeStruct((1, LANES), F32)
    return pl.pallas_call(
        body, name=name, grid=(B, nq),
        in_specs=[pl.BlockSpec((BLK, W), lambda b, i: (b * nq + i, 0)), spec_t, spec_t, spec_t, spec_g, spec_g,
                  pl.BlockSpec((BLK, NQ), lambda b, i: (b * nq + i, 0)), spec_c, spec_n, spec_c, spec_n],
        out_specs=[pl.BlockSpec((BLK, W), lambda b, i: (b * nq + i, 0)), spec_g, spec_g],
        out_shape=[jax.ShapeDtypeStruct((T, W), BF16), row, row],
        compiler_params=_params("arbitrary", "arbitrary"),
    )(qkv, cs, s1, s2, qg, kg, dq, dkc, dkp, dvc, dvp)


SB_TILE = 256
SB_UNROLL = 4
SB_UNROLL_BWD = 2


def _split_heads(x2, scale=None):
    lo = _lo_mask(x2.shape)
    z = jnp.zeros_like(x2)
    if scale is not None:
        x2 = x2 * scale
    return jnp.where(lo, x2, z), jnp.where(lo, z, x2)


def _sb_terms(qh, kj, diagonal):
    z = _dot(qh, kj, NT)
    e = jnp.exp2(jnp.abs(z) * (-LOG2E))
    lb = jnp.minimum(z, 0.0) - jnp.log(1.0 + e)
    L = lb - z
    if not diagonal:
        return lb, L, None, z, e
    strict = lax.broadcasted_iota(jnp.int32, z.shape, 1) < lax.broadcasted_iota(jnp.int32, z.shape, 0)
    return lb, jnp.where(strict, L, 0.0), strict, z, e


def _tri(n, cmp):
    r = lax.broadcasted_iota(jnp.int32, (n, n), 0)
    c = lax.broadcasted_iota(jnp.int32, (n, n), 1)
    return cmp(r, c).astype(BF16)


def _by_value(r, fns, carry):
    if len(fns) == 1:
        return fns[0](carry)
    half = len(fns) // 2
    return lax.cond(r < half, lambda cr: _by_value(r, fns[:half], cr), lambda cr: _by_value(r - half, fns[half:], cr), carry)


def _sb_fwd(qkv, B, name, gather=None):
    T, W = qkv.shape
    NQ = W // 3
    NP = NQ // LANES
    S = T // B
    tq = min(SB_TILE, S)
    nq = S // tq
    grid = (B, NP, nq)

    def body(q_ref, k_ref, v_ref, o_ref, t_ref):
        i = pl.program_id(2)
        qh = _split_heads(q_ref[...], SCALE)
        U = _tri(tq, lambda r, c: r > c)

        def sweep(tiles, cs, acc):
            chains = [(t, h) for t in range(len(tiles)) for h in range(2)]
            rows = [pl.ds(pl.multiple_of(j * tq, tq), tq) for j, _ in tiles]
            ks = [k_ref[r, :] for r in rows]
            vs = [_split_heads(v_ref[r, :]) for r in rows]
            terms = {(t, h): _sb_terms(qh[h], ks[t], tiles[t][1]) for t, h in chains}
            carry = {}
            for h in range(2):
                c = cs[h]
                for t in range(len(tiles)):
                    carry[t, h] = c
                    c = c + jnp.sum(terms[t, h][1], axis=1, keepdims=True)
                cs = cs[:h] + (c,) + cs[h + 1:]
            cum = {ch: _dot(terms[ch][1].astype(BF16), U, NN) for ch in chains}
            for ch in chains:
                a = jnp.exp(terms[ch][0] + (cum[ch] + carry[ch]))
                if tiles[ch[0]][1]:
                    a = jnp.where(terms[ch][2], a, 0.0)
                acc = acc + _dot(a.astype(BF16), vs[ch[0]][ch[1]], NN)
            return cs, acc

        zero = jnp.zeros((tq, 1), F32)
        rem = i % SB_UNROLL
        heads = [lambda cr, k=k: sweep([(i, True)] + [(i - 1 - t, False) for t in range(k)], *cr) for k in range(SB_UNROLL)]
        carry = _by_value(rem, heads, ((zero, zero), jnp.zeros((tq, LANES), F32)))
        step = lambda n, cr: sweep([(i - 1 - rem - SB_UNROLL * n - t, False) for t in range(SB_UNROLL)], *cr)
        cs, acc = lax.fori_loop(0, i // SB_UNROLL, step, carry)
        o_ref[...] = acc.astype(BF16)
        t_ref[...] = jnp.where(_lo_mask((tq, LANES)), cs[0], cs[1])

    spec_q = pl.BlockSpec((tq, LANES), lambda b, p, i: (b * nq + i, p))
    in_specs = [spec_q, pl.BlockSpec((S, LANES), lambda b, p, i: (b, NP + p)),
                pl.BlockSpec((S, LANES), lambda b, p, i: (b, 2 * NP + p))]
    out_shape = [jax.ShapeDtypeStruct((T, NQ), BF16), jax.ShapeDtypeStruct((T, NQ), F32)]
    return _call_behind(body, name, grid, in_specs, [spec_q, spec_q], out_shape, [qkv, qkv, qkv], gather, False)


def _sb_bwd(qkv, do, tot, B, name, exchange=None):
    T, W = qkv.shape
    NQ = W // 3
    NP = NQ // LANES
    S = T // B
    tq = min(SB_TILE, S)
    nq = S // tq
    grid = (B, NP, nq)

    def body(q_ref, k_ref, v_ref, do_ref, t_ref, dq_ref, dk_ref, dv_ref, dkm_ref, dvm_ref):
        i = pl.program_id(2)

        @pl.when(i == 0)
        def _():
            dk_ref[...] = jnp.zeros_like(dk_ref)
            dv_ref[...] = jnp.zeros_like(dv_ref)
        qh = _split_heads(q_ref[...], SCALE)
        doh = _split_heads(do_ref[...])
        top = lax.broadcasted_iota(jnp.int32, (LANES, tq), 0) < HEAD
        zt = jnp.zeros((LANES, tq), BF16)
        qt = (q_ref[...].astype(F32) * SCALE).T.astype(BF16)
        dot_ = do_ref[...].astype(F32).T.astype(BF16)
        qth = (jnp.where(top, qt, zt), jnp.where(top, zt, qt))
        doth = (jnp.where(top, dot_, zt), jnp.where(top, zt, dot_))
        tt = t_ref[...]
        tot = (tt[:, 0:1], tt[:, HEAD:HEAD + 1])
        Urev = _tri(tq, lambda r, c: r > c)
        Uexc = _tri(tq, lambda r, c: r < c)

        def sweep(tiles, carry):
            nt = len(tiles)
            chains = [(t, h) for t in range(nt) for h in range(2)]
            rows = [pl.ds(pl.multiple_of(j * tq, tq), tq) for j, _ in tiles]
            ks = [k_ref[r, :] for r in rows]
            vs = [v_ref[r, :] for r in rows]
            terms = {(t, h): _sb_terms(qh[h], ks[t], tiles[t][1]) for t, h in chains}
            cc = [carry[h][0] for h in range(2)]
            later = {}
            for t, h in chains:
                cc[h] = cc[h] + jnp.sum(terms[t, h][1], axis=1, keepdims=True)
                later[t, h] = tot[h] - cc[h]
            cum = {ch: _dot(terms[ch][1].astype(BF16), Urev, NN) for ch in chains}
            da = {(t, h): _dot(doh[h], vs[t], NT) for t, h in chains}
            a, g, before = {}, {}, {}
            cg = [carry[h][1] for h in range(2)]
            for ch in chains:
                a[ch] = jnp.exp(terms[ch][0] + (cum[ch] + later[ch]))
                if tiles[ch[0]][1]:
                    a[ch] = jnp.where(terms[ch][2], a[ch], 0.0)
                g[ch] = a[ch] * da[ch]
                before[ch] = cg[ch[1]]
                cg[ch[1]] = cg[ch[1]] + jnp.sum(g[ch], axis=1, keepdims=True)
            G = {ch: _dot(g[ch].astype(BF16), Uexc, NN) for ch in chains}
            dz = {}
            for ch in chains:
                d = g[ch] - jnp.exp(terms[ch][0]) * (g[ch] + (G[ch] + before[ch]))
                if tiles[ch[0]][1]:
                    d = jnp.where(terms[ch][2], d, 0.0)
                dz[ch] = d.astype(BF16)
            dq = [carry[h][2] for h in range(2)]
            for t, h in chains:
                dq[h] = dq[h] + _dot(dz[t, h], ks[t], NN)
            for t in range(nt):
                dk_ref[:, rows[t]] += _dot(qth[0], dz[t, 0], NN) + _dot(qth[1], dz[t, 1], NN)
                dv_ref[:, rows[t]] += _dot(doth[0], a[t, 0].astype(BF16), NN) + _dot(doth[1], a[t, 1].astype(BF16), NN)
            return tuple((cc[h], cg[h], dq[h]) for h in range(2))

        zero = jnp.zeros((tq, 1), F32)
        zq = jnp.zeros((tq, LANES), F32)
        step = lambda n, cr: sweep([(SB_UNROLL_BWD * n + t, False) for t in range(SB_UNROLL_BWD)], cr)
        carry = lax.fori_loop(0, i // SB_UNROLL_BWD, step, ((zero, zero, zq), (zero, zero, zq)))
        tails = [lambda cr, k=k: sweep([(i - k + t, False) for t in range(k)] + [(i, True)], cr) for k in range(SB_UNROLL_BWD)]
        carry = _by_value(i % SB_UNROLL_BWD, tails, carry)
        dq_ref[...] = (jnp.where(_lo_mask((tq, LANES)), carry[0][2], carry[1][2]) * SCALE).astype(BF16)

        @pl.when(i == nq - 1)
        def _():
            for c in range(nq):
                cols = slice(c * tq, (c + 1) * tq)
                dkm_ref[cols, :] = dk_ref[:, cols].T.astype(BF16)
                dvm_ref[cols, :] = dv_ref[:, cols].T.astype(BF16)

    spec_q = pl.BlockSpec((tq, LANES), lambda b, p, i: (b * nq + i, p))
    spec_s = pl.BlockSpec((LANES, S), lambda b, p, i: (b * NP + p, 0))
    spec_m = pl.BlockSpec((S, LANES), lambda b, p, i: (b, p))
    key_side = jax.ShapeDtypeStruct((B * NQ, S), F32)
    token_major = jax.ShapeDtypeStruct((T, NQ), BF16)
    in_specs = [spec_q, pl.BlockSpec((S, LANES), lambda b, p, i: (b, NP + p)),
                pl.BlockSpec((S, LANES), lambda b, p, i: (b, 2 * NP + p)), spec_q, spec_q]
    out_specs = [spec_q, spec_s, spec_s, spec_m, spec_m]
    out_shape = [token_major, key_side, key_side, token_major, token_major]
    args = [qkv, qkv, qkv, do, tot]
    return _call_behind(body, name, grid, in_specs, out_specs, out_shape, args, exchange, True)


def _adamw(w, g, m, v, name):
    shape = w.shape
    cols = shape[-1]
    rows = math.prod(shape[:-1])
    tr = _pick(rows, max(8, (1 << 19) // max(cols, LANES) // 8 * 8), 8)

    def body(w_ref, g_ref, m_ref, v_ref, d_ref, mo_ref, vo_ref):
        gv = g_ref[...]
        mn = ADAM_B1 * m_ref[...] + (1.0 - ADAM_B1) * gv
        vn = ADAM_B2 * v_ref[...] + (1.0 - ADAM_B2) * (gv * gv)
        m_hat = mn / (1.0 - ADAM_B1 ** ADAM_STEP)
        v_hat = vn / (1.0 - ADAM_B2 ** ADAM_STEP)
        d_ref[...] = -ADAM_LR * (m_hat / (jnp.sqrt(v_hat) + ADAM_EPS) + ADAM_WD * w_ref[...])
        mo_ref[...] = mn
        vo_ref[...] = vn

    spec = pl.BlockSpec((tr, cols), lambda i: (i, 0))
    out = jax.ShapeDtypeStruct((rows, cols), F32)
    d, mn, vn = pl.pallas_call(
        body, name=name, grid=(rows // tr,),
        in_specs=[spec] * 4, out_specs=[spec] * 3, out_shape=[out] * 3,
        compiler_params=_params("parallel"),
    )(w.reshape(rows, cols), g.reshape(rows, cols), m.reshape(rows, cols), v.reshape(rows, cols))
    return d.reshape(shape), mn.reshape(shape), vn.reshape(shape)


def _pad_rows(a, rows):
    return jnp.pad(a, ((0, rows - a.shape[0]), (0, 0)))


def kernel(x, c, positions, ada_w, ada_b, norm1_g, norm2_g, wqkv_a, q_norm_a, k_norm_a, sinks_a, wo_a, wqkv_b, wo_b, w_gate, w_up, w_down, loss_target, m_ada_w, m_ada_b, m_norm1_g, m_norm2_g, m_wqkv_a, m_q_norm_a, m_k_norm_a, m_sinks_a, m_wo_a, m_wqkv_b, m_wo_b, m_w_gate, m_w_up, m_w_down, v_ada_w, v_ada_b, v_norm1_g, v_norm2_g, v_wqkv_a, v_q_norm_a, v_k_norm_a, v_sinks_a, v_wo_a, v_wqkv_b, v_wo_b, v_w_gate, v_w_up, v_w_down):
    B, S, D = x.shape
    T = B * S
    L = ada_w.shape[0]
    NA, NB_ = wqkv_a.shape[0], wqkv_b.shape[0]
    me = 4 * lax.axis_index("x") + 2 * lax.axis_index("y") + lax.axis_index("c")
    xt = x.reshape(T, D)

    col_sharded = {"qkv_a": wqkv_a, "qkv_b": wqkv_b, "gate": w_gate, "up": w_up}
    row_sharded = {"wo_a": wo_a, "wo_b": wo_b, "down": w_down}

    def shard_rows(key):
        kind, idx = key
        return col_sharded[kind][idx].T if kind in col_sharded else row_sharded[kind][idx]

    def layer_keys(l):
        mix = "a" if l % 2 == 0 else "b"
        return [("qkv_" + mix, l // 2), ("wo_" + mix, l // 2), ("gate", l), ("up", l), ("down", l)]

    def unpack(buf, keys, reshape):
        out, off = {}, 0
        for key in keys:
            rows = shard_rows(key).shape[0]
            out[key] = reshape(buf[..., off:off + rows, :], rows)
            off += rows
        return out

    first_b = 1
    keys_early = layer_keys(0)[:2]
    keys_mid = layer_keys(0)[2:] + [("qkv_b", 0)]
    keys_late = [k for l in range(1, L) for k in layer_keys(l) if k != ("qkv_b", 0)]
    pack = lambda keys: jnp.concatenate([shard_rows(k).astype(BF16) for k in keys], axis=0)
    full_rows = lambda b, rows: b.reshape(NDEV * rows, D)
    W = unpack(_all_gather(pack(keys_early), "ag_weights"), keys_early, full_rows)

    WA = ada_w.shape[2]
    c_all = _all_gather(c, "ag_c").reshape(NDEV * B, D)
    bias = lax.dynamic_slice_in_dim(ada_b, me * WA, WA, axis=1).reshape(L, 1, WA)
    mod_part = _ada_fwd(c_all, ada_w, bias, "ada_fwd")
    mod_all = _all_gather(mod_part.reshape(L * NDEV * B, WA), "ag_mod")
    mod_all = mod_all.reshape(NDEV, L, NDEV * B, WA).transpose(1, 2, 0, 3).reshape(L, NDEV * B, NDEV * WA)
    mod = lax.dynamic_slice_in_dim(mod_all, me * B, B, axis=1)
    mod = mod.reshape(L, B, 6, 1, D)
    sh1, sc1, g1, sh2, sc2, g2 = [mod[:, :, k] for k in range(6)]

    half = ROT // 2
    inv_freq = jnp.power(jnp.float32(ROPE_THETA), -jnp.arange(half, dtype=F32) * 2.0 / ROT)
    ang = positions.reshape(T, 1).astype(F32) * inv_freq[None, :]
    cos, sin = jnp.cos(ang), jnp.sin(ang)
    ones = jnp.ones((T, HEAD - ROT), F32)
    zeros = jnp.zeros((T, HEAD - ROT), F32)
    z8 = jnp.zeros((T, half), F32)
    cs = jnp.tile(jnp.concatenate([cos, cos, ones], axis=1), (1, 2))
    s1 = jnp.tile(jnp.concatenate([-sin, z8, zeros], axis=1), (1, 2))
    s2 = jnp.tile(jnp.concatenate([z8, sin, zeros], axis=1), (1, 2))

    saved = []
    xc = xt
    h1 = _norm_mod(xc, norm1_g[0:1], sc1[0], sh1[0], S, "norm1_0")
    for l in range(L):
        j = l // 2
        sv = dict(x_in=xc, h1=h1)
        if l % 2 == 0:
            qkv = _mm_nt(h1, W["qkv_a", j], F32, f"qkv_a_{l}")
            qg = jnp.tile(q_norm_a[j:j + 1], (1, 2))
            kg = jnp.tile(k_norm_a[j:j + 1], (1, 2))
            qn, kd, vd = _qk_prep(qkv, cs, s1, s2, qg, kg, f"qk_prep_{l}")
            sink2 = jnp.repeat(sinks_a[j].reshape(-1, 2), HEAD, axis=1).reshape(-1, 1, LANES)
            if l == 0:
                attn, lse, mid = _swa_fwd(qn, kd, vd, sink2, B, f"swa_fwd_{l}", gather=pack(keys_mid))
                W.update(unpack(mid, keys_mid, full_rows))
            else:
                attn, lse = _swa_fwd(qn, kd, vd, sink2, B, f"swa_fwd_{l}")
            sv.update(qkv=qkv, qg=qg, kg=kg, qn=qn, kd=kd, vd=vd, sink2=sink2, lse=lse)
            wo = W["wo_a", j]
        else:
            qkv = _mm_nt(h1, W["qkv_b", j], BF16, f"qkv_b_{l}")
            if l == first_b:
                attn, tot, late = _sb_fwd(qkv, B, f"sb_fwd_{l}", gather=pack(keys_late))
                W.update(unpack(late, keys_late, full_rows))
            else:
                attn, tot = _sb_fwd(qkv, B, f"sb_fwd_{l}")
            sv.update(qkv=qkv, tot=tot)
            wo = W["wo_b", j]
        y1, xm, h2 = _mm_res(attn, wo, xc, g1[l], S, f"attn_out_{l}", norm=(norm2_g[l:l + 1], sc2[l], sh2[l]))
        gate, up, act = _swiglu_fwd(h2, W["gate", l], W["up", l], f"swiglu_fwd_{l}")
        if l + 1 < L:
            y2, xc, h1 = _mm_res(act, W["down", l], xm, g2[l], S, f"mlp_out_{l}",
                                 norm=(norm1_g[l + 1:l + 2], sc1[l + 1], sh1[l + 1]))
        else:
            y2, xc = _mm_res(act, W["down", l], xm, g2[l], S, f"mlp_out_{l}")
        sv.update(attn=attn, y1=y1, x_mid=xm, h2=h2, gate=gate, up=up, act=act, y2=y2)
        saved.append(sv)

    dx, loss_tile = _loss_head(xc, loss_target.reshape(T, D), "loss_head")

    G = {}
    pack_grads = lambda keys: jnp.concatenate([G[k].reshape(NDEV, G[k].shape[0] // NDEV, D) for k in keys], axis=1)
    keys_hi = [k for l in range(first_b + 1, L) for k in layer_keys(l)] + layer_keys(first_b)[1:]
    keys_mlp0_g = [("qkv_b", 0), ("down", 0)]
    keys_mid_g = [("wo_a", 0), ("gate", 0), ("up", 0)]
    keys_lo = layer_keys(0)[:1]
    received_hi = received_mid = received_mlp0 = None
    dmod = [None] * L
    dn1, dn2 = [None] * L, [None] * L
    dqg, dkg, dsink = [None] * NA, [None] * NA, [None] * NA
    dy2, dg2 = _gate_bwd(dx, saved[L - 1]["y2"], g2[L - 1], S, "gate2_bwd_top")
    for l in reversed(range(L)):
        j = l // 2
        mix = "a" if l % 2 == 0 else "b"
        sv = saved[l]
        dgate, dup = _swiglu_bwd(dy2, W["down", l], sv["gate"], sv["up"], f"swiglu_bwd_{l}")
        G["down", l] = _mm_tn(sv["act"], dy2, f"dw_down_{l}")
        G["gate", l] = _mm_tn(dgate, sv["h2"], f"dw_gate_{l}")
        G["up", l] = _mm_tn(dup, sv["h2"], f"dw_up_{l}")
        n2 = _norm_mod_bwd(sv["x_mid"], [(dgate, W["gate", l]), (dup, W["up", l])], dx, norm2_g[l:l + 1], sc2[l], S,
                           f"norm2_bwd_{l}", below=(sv["y1"], g1[l]), exchange=pack_grads(keys_mlp0_g) if l == 0 else None)
        dxm, dsh2, dsc2, dn2[l], dy1, dg1 = n2[:6]
        if l == 0:
            received_mlp0 = n2[6]
        dattn = _mm_nt(dy1, W["wo_" + mix, j], BF16, f"dattn_{l}")
        G["wo_" + mix, j] = _mm_tn(sv["attn"], dy1, f"dw_o_{l}")
        if l % 2 == 0:
            swa_args = (sv["qn"], sv["kd"], sv["vd"], sv["sink2"], dattn, sv["lse"], B, f"swa_bwd_{l}")
            if l == 0:
                dq, dkc, dkp, dvc, dvp, dsink[j], received_mid = _swa_bwd(*swa_args, exchange=pack_grads(keys_mid_g))
            else:
                dq, dkc, dkp, dvc, dvp, dsink[j] = _swa_bwd(*swa_args)
            dqkv, dqg[j], dkg[j] = _qk_prep_bwd(sv["qkv"], cs, s1, s2, sv["qg"], sv["kg"], dq, dkc, dkp, dvc, dvp, B,
                                                f"qk_prep_bwd_{l}")
        else:
            nqb = sv["qkv"].shape[1] // 3
            sb_args = (sv["qkv"], dattn, sv["tot"], B, f"sb_bwd_{l}")
            if l == first_b and keys_hi:
                dq, _, _, dk, dv, received_hi = _sb_bwd(*sb_args, exchange=pack_grads(keys_hi))
            else:
                dq, _, _, dk, dv = _sb_bwd(*sb_args)
        wt = W["qkv_" + mix, j]
        if l % 2 == 0:
            dh1_pairs = [(dqkv, wt)]
            G["qkv_a", j] = _mm_tn(dqkv, sv["h1"], f"dw_qkv_{l}")
        else:
            parts = [dq, dk, dv]
            dh1_pairs = [(part, wt[k * nqb:(k + 1) * nqb]) for k, part in enumerate(parts)]
            G["qkv_b", j] = jnp.concatenate([_mm_tn(part, sv["h1"], f"dw_qkv_{l}_{k}") for k, part in enumerate(parts)], axis=0)
        n1_args = (sv["x_in"], dh1_pairs, dxm, norm1_g[l:l + 1], sc1[l], S, f"norm1_bwd_{l}")
        dmod_l = [None, None, dg1, dsh2, dsc2, dg2]
        if l > 0:
            dx, dmod_l[0], dmod_l[1], dn1[l], dy2, dg2 = _norm_mod_bwd(*n1_args, below=(saved[l - 1]["y2"], g2[l - 1]))
        else:
            dx, dmod_l[0], dmod_l[1], dn1[l] = _norm_mod_bwd(*n1_args)
        dmod[l] = jnp.concatenate(dmod_l, axis=1)
    grad_x = dx.reshape(B, S, D)

    ndm = L * 6
    dmod_rows = jnp.stack(dmod, axis=1).reshape(B * ndm, D)
    misc = jnp.concatenate(
        [jnp.concatenate(dn1, axis=0).reshape(B * L, D), jnp.concatenate(dn2, axis=0).reshape(B * L, D),
         _pad_rows(jnp.concatenate([jnp.pad(r, ((0, 0), (0, D - LANES))) for r in dqg + dkg]
                                   + [jnp.pad(r[:, 0, ::HEAD].reshape(1, -1), ((0, 0), (0, D - 2 * r.shape[0]))) for r in dsink]
                                   + [jnp.pad(loss_tile[0:1, 0:1], ((0, 0), (0, D - 1)))], axis=0), 8)], axis=0)
    nmisc = misc.shape[0]
    small = _all_gather(jnp.concatenate([dmod_rows, _pad_rows(misc, -(-nmisc // 8) * 8)], axis=0), "ag_small")
    dmod_all = small[:, :B * ndm].reshape(NDEV * B, ndm, D)
    g_ada_b = _sum_leading(dmod_all, "sum_dmod").reshape(L, 6 * D)
    misc_sum = _sum_leading(small[:, B * ndm:], "sum_misc")
    g_n1 = misc_sum[0:B * L].reshape(L, B, D)
    g_n2 = misc_sum[B * L:2 * B * L].reshape(L, B, D)
    g_norm1 = _sum_leading(g_n1.transpose(1, 0, 2), "sum_n1")
    g_norm2 = _sum_leading(g_n2.transpose(1, 0, 2), "sum_n2")
    o = 2 * B * L
    g_qn = misc_sum[o:o + NA, :HEAD]
    g_kn = misc_sum[o + NA:o + 2 * NA, :HEAD]
    nsink = sinks_a.shape[1]
    g_sink = misc_sum[o + 2 * NA:o + 3 * NA, :nsink]
    loss = misc_sum[o + 3 * NA, 0]

    dmod_loc = lax.dynamic_slice_in_dim(dmod_all.reshape(NDEV * B, L, 6 * D), me * WA, WA, axis=2)
    g_ada_w = _ada_bwd(c_all, dmod_loc.transpose(1, 0, 2), "ada_bwd")

    shard = unpack(_sum_leading(_exchange(pack_grads(keys_lo), "grad_exchange"), "grad_sum"), keys_lo, lambda b, rows: b)
    shard.update(unpack(_sum_leading(received_mid, "grad_sum_mid"), keys_mid_g, lambda b, rows: b))
    shard.update(unpack(_sum_leading(received_mlp0, "grad_sum_mlp0"), keys_mlp0_g, lambda b, rows: b))
    if received_hi is not None:
        shard.update(unpack(_sum_leading(received_hi, "grad_sum_hi"), keys_hi, lambda b, rows: b))

    def stacked(kind, n):
        return jnp.stack([shard[kind, i].T if kind in col_sharded else shard[kind, i] for i in range(n)])

    gw_qkv_a, gw_qkv_b, gw_gate, gw_up = stacked("qkv_a", NA), stacked("qkv_b", NB_), stacked("gate", L), stacked("up", L)
    gw_wo_a, gw_wo_b, gw_down = stacked("wo_a", NA), stacked("wo_b", NB_), stacked("down", L)

    grads = [g_ada_w, g_ada_b, g_norm1, g_norm2, gw_qkv_a, g_qn, g_kn, g_sink, gw_wo_a, gw_qkv_b, gw_wo_b,
             gw_gate, gw_up, gw_down]
    ws = [ada_w, ada_b, norm1_g, norm2_g, wqkv_a, q_norm_a, k_norm_a, sinks_a, wo_a, wqkv_b, wo_b, w_gate, w_up, w_down]
    ms = [m_ada_w, m_ada_b, m_norm1_g, m_norm2_g, m_wqkv_a, m_q_norm_a, m_k_norm_a, m_sinks_a, m_wo_a, m_wqkv_b,
          m_wo_b, m_w_gate, m_w_up, m_w_down]
    vs = [v_ada_w, v_ada_b, v_norm1_g, v_norm2_g, v_wqkv_a, v_q_norm_a, v_k_norm_a, v_sinks_a, v_wo_a, v_wqkv_b,
          v_wo_b, v_w_gate, v_w_up, v_w_down]
    deltas, new_m, new_v = [], [], []
    for k, (w, g, m, v) in enumerate(zip(ws, grads, ms, vs)):
        g = g.reshape(w.shape)
        d, mn, vn = _adamw(w, g, m, v, f"adamw_{k}")
        grads[k] = g
        deltas.append(d)
        new_m.append(mn)
        new_v.append(vn)
    return (loss, grad_x, *grads, *deltas, *new_m, *new_v)
```

```python
import functools
import math

import jax
import jax.numpy as jnp
from jax import lax
from jax.experimental import pallas as pl
from jax.experimental.pallas import tpu as pltpu

F32 = jnp.float32
BF16 = jnp.bfloat16
NDEV = 8
HEAD = 64
BLK = 128
LANES = 128
EPS = 1e-6
ROT = HEAD // 4
ROPE_THETA = 500000.0
SCALE = HEAD ** -0.5
LOG2E = math.log2(math.e)
NEG = -1e30
VMEM_LIMIT = 56 * 1024 * 1024
MESH = pl.DeviceIdType.MESH
HIGH = lax.Precision.HIGHEST

ADAM_LR = 0.001
ADAM_B1 = 0.9
ADAM_B2 = 0.999
ADAM_EPS = 1e-08
ADAM_WD = 0.01
ADAM_STEP = 10


def _params(*sem):
    return pltpu.CompilerParams(dimension_semantics=sem, vmem_limit_bytes=VMEM_LIMIT)


def _pick(n, cap, mult):
    if n <= cap:
        return n
    best = None
    for t in range(mult, cap + 1, mult):
        if n % t == 0:
            best = t
    assert best is not None, (n, cap, mult)
    return best


def _dot(a, b, dims, precision=None):
    return lax.dot_general(a, b, (dims, ((), ())), preferred_element_type=F32, precision=precision)


NN = ((1,), (0,))
NT = ((1,), (1,))
TN = ((0,), (0,))


def _all_gather(x, name):
    m, n = x.shape

    def body(x_ref, out_ref, send_sems, recv_sems, local_sem):
        ix, iy, ic = lax.axis_index("x"), lax.axis_index("y"), lax.axis_index("c")
        me, sibling = (ix, iy, ic), (ix, iy, 1 - ic)
        chips = [(1 - ix, iy), (ix, 1 - iy), (1 - ix, 1 - iy)]

        def slab(px, py, pc):
            return out_ref.at[4 * px + 2 * py + pc]

        def copy(k, block, to, src=None):
            return pltpu.make_async_remote_copy(
                src_ref=slab(*block) if src is None else src, dst_ref=slab(*block),
                send_sem=send_sems.at[k], recv_sem=recv_sems.at[k], device_id=to, device_id_type=MESH)

        mine = pltpu.make_async_copy(x_ref, slab(*me), local_sem)
        mine.start()
        first = [copy(0, me, sibling, src=x_ref)]
        first += [copy(1 + j, me, (*chip, ic), src=x_ref) for j, chip in enumerate(chips)]
        for cp in first:
            cp.start()
        passed = [copy(4 + j, (*chip, ic), sibling) for j, chip in enumerate(chips)]
        for j, chip in enumerate(chips):
            copy(1 + j, (*chip, ic), me).wait_recv()
            passed[j].start()
        copy(0, sibling, me).wait_recv()
        for j, chip in enumerate(chips):
            copy(4 + j, (*chip, 1 - ic), me).wait_recv()
        for cp in first + passed:
            cp.wait_send()
        mine.wait()

    return pl.pallas_call(
        body, name=name,
        out_shape=jax.ShapeDtypeStruct((NDEV, m, n), x.dtype),
        in_specs=[pl.BlockSpec(memory_space=pl.ANY)],
        out_specs=pl.BlockSpec(memory_space=pl.ANY),
        scratch_shapes=[pltpu.SemaphoreType.DMA((7,)), pltpu.SemaphoreType.DMA((7,)), pltpu.SemaphoreType.DMA(())],
    )(x)


COMM_SEMS = [pltpu.SemaphoreType.DMA((NDEV - 1,)), pltpu.SemaphoreType.DMA((NDEV - 1,)), pltpu.SemaphoreType.DMA(())]
HBM_SPEC = pl.BlockSpec(memory_space=pl.ANY)


def _direct_copies(src_ref, dst_ref, sems, scatter):
    send_sems, recv_sems, own_sem = sems
    ix, iy, ic = lax.axis_index("x"), lax.axis_index("y"), lax.axis_index("c")
    me = 4 * ix + 2 * iy + ic
    copies = [pltpu.make_async_copy(src_ref.at[me] if scatter else src_ref, dst_ref.at[me], own_sem)]
    for k in range(1, NDEV):
        px = 1 - ix if k & 4 else ix
        py = 1 - iy if k & 2 else iy
        pc = 1 - ic if k & 1 else ic
        copies.append(pltpu.make_async_remote_copy(
            src_ref=src_ref.at[4 * px + 2 * py + pc] if scatter else src_ref, dst_ref=dst_ref.at[me],
            send_sem=send_sems.at[k - 1], recv_sem=recv_sems.at[k - 1],
            device_id=(px, py, pc), device_id_type=MESH))
    return copies


def _exchange(p, name):
    def body(p_ref, r_ref, *sems):
        copies = _direct_copies(p_ref, r_ref, sems, True)
        for cp in copies:
            cp.start()
        for cp in copies:
            cp.wait()

    return pl.pallas_call(
        body, name=name,
        out_shape=jax.ShapeDtypeStruct(p.shape, p.dtype),
        in_specs=[HBM_SPEC], out_specs=HBM_SPEC, scratch_shapes=COMM_SEMS,
    )(p)


def _call_behind(body, name, grid, in_specs, out_specs, out_shape, args, payload=None, scatter=False):
    params = _params(*["arbitrary"] * len(grid))
    if payload is None:
        return pl.pallas_call(body, name=name, grid=grid, in_specs=in_specs, out_specs=out_specs, out_shape=out_shape,
                              compiler_params=params)(*args)
    n_in, n_out = len(in_specs), len(out_specs)

    def edge(first):
        ids = [pl.program_id(a) for a in range(len(grid))]
        return functools.reduce(lambda u, v: u & v, [i == (0 if first else d - 1) for i, d in zip(ids, grid)])

    def wrapped(*refs):
        x_ref, r_ref, sems = refs[n_in], refs[n_in + 1 + n_out], refs[n_in + n_out + 2:]

        @pl.when(edge(True))
        def _():
            for cp in _direct_copies(x_ref, r_ref, sems, scatter):
                cp.start()
        body(*refs[:n_in], *refs[n_in + 1:n_in + 1 + n_out])

        @pl.when(edge(False))
        def _():
            for cp in _direct_copies(x_ref, r_ref, sems, scatter):
                cp.wait()

    arrived = jax.ShapeDtypeStruct(payload.shape if scatter else (NDEV,) + payload.shape, payload.dtype)
    return pl.pallas_call(
        wrapped, name=name, grid=grid, in_specs=list(in_specs) + [HBM_SPEC], out_specs=list(out_specs) + [HBM_SPEC],
        out_shape=list(out_shape) + [arrived], scratch_shapes=COMM_SEMS, compiler_params=params,
    )(*args, payload)


def _sum_leading(r, name):
    k, m, n = r.shape
    mult = 8 * (4 // r.dtype.itemsize)
    tm = _pick(m, max(mult, (4 * 1024 * 1024) // (k * n * r.dtype.itemsize) // mult * mult), mult)

    def body(r_ref, o_ref):
        acc = r_ref[0].astype(F32)
        for s in range(1, k):
            acc = acc + r_ref[s].astype(F32)
        o_ref[...] = acc

    return pl.pallas_call(
        body, name=name, grid=(m // tm,),
        in_specs=[pl.BlockSpec((k, tm, n), lambda i: (0, i, 0))],
        out_specs=pl.BlockSpec((tm, n), lambda i: (i, 0)),
        out_shape=jax.ShapeDtypeStruct((m, n), F32),
        compiler_params=_params("parallel"),
    )(r)


def _mm_nt(a, bt, out_dtype, name):
    M, K = a.shape
    N = bt.shape[0]
    tm, tn = _pick(M, 1024, 8), _pick(N, 1536, LANES)

    def body(a_ref, b_ref, o_ref):
        o_ref[...] = _dot(a_ref[...], b_ref[...], NT).astype(out_dtype)

    return pl.pallas_call(
        body, name=name, grid=(N // tn, M // tm),
        in_specs=[pl.BlockSpec((tm, K), lambda j, i: (i, 0)), pl.BlockSpec((tn, K), lambda j, i: (j, 0))],
        out_specs=pl.BlockSpec((tm, tn), lambda j, i: (i, j)),
        out_shape=jax.ShapeDtypeStruct((M, N), out_dtype),
        compiler_params=_params("parallel", "parallel"),
    )(a, bt)


def _mm_tn(a, b, name):
    M, N1 = a.shape
    N2 = b.shape[1]
    t1, tk = _pick(N1, 1536, LANES), _pick(M, 2048, 8)
    nk = M // tk

    def body(a_ref, b_ref, o_ref, acc_ref):
        k = pl.program_id(1)

        @pl.when(k == 0)
        def _():
            acc_ref[...] = jnp.zeros_like(acc_ref)
        acc_ref[...] += _dot(a_ref[...], b_ref[...], TN)

        @pl.when(k == nk - 1)
        def _():
            o_ref[...] = acc_ref[...].astype(BF16)

    return pl.pallas_call(
        body, name=name, grid=(N1 // t1, nk),
        in_specs=[pl.BlockSpec((tk, t1), lambda i, k: (k, i)), pl.BlockSpec((tk, N2), lambda i, k: (k, 0))],
        out_specs=pl.BlockSpec((t1, N2), lambda i, k: (i, 0)),
        out_shape=jax.ShapeDtypeStruct((N1, N2), BF16),
        scratch_shapes=[pltpu.VMEM((t1, N2), F32)],
        compiler_params=_params("parallel", "arbitrary"),
    )(a, b)


def _norm_mod_rows(xv, gain, sc, sh):
    r = lax.rsqrt(jnp.mean(xv * xv, axis=-1, keepdims=True) + EPS)
    return ((xv * r) * gain * (1.0 + sc) + sh).astype(BF16)


def _mm_res(a, w, x, gate, S, name, norm=None):
    T, K = a.shape
    D = w.shape[1]
    tm = _pick(S, 512, 8)
    nb = S // tm

    def body(a_ref, w_ref, x_ref, g_ref, *rest):
        y = _dot(a_ref[...], w_ref[...], NN)
        xn = x_ref[...] + g_ref[0] * y
        if norm is None:
            y_ref, o_ref = rest
        else:
            gain_ref, sc_ref, sh_ref, y_ref, o_ref, h_ref = rest
            h_ref[...] = _norm_mod_rows(xn, gain_ref[...], sc_ref[0], sh_ref[0])
        y_ref[...] = y.astype(BF16)
        o_ref[...] = xn

    spec_t = pl.BlockSpec((tm, D), lambda i: (i, 0))
    spec_b = pl.BlockSpec((1, 1, D), lambda i: (i // nb, 0, 0))
    in_specs = [pl.BlockSpec((tm, K), lambda i: (i, 0)), pl.BlockSpec((K, D), lambda i: (0, 0)), spec_t, spec_b]
    out_specs = [spec_t, spec_t]
    out_shape = [jax.ShapeDtypeStruct((T, D), BF16), jax.ShapeDtypeStruct((T, D), F32)]
    args = [a, w, x, gate]
    if norm is not None:
        in_specs += [pl.BlockSpec((1, D), lambda i: (0, 0)), spec_b, spec_b]
        out_specs.append(spec_t)
        out_shape.append(jax.ShapeDtypeStruct((T, D), BF16))
        args += list(norm)
    return pl.pallas_call(
        body, name=name, grid=(T // tm,), in_specs=in_specs, out_specs=out_specs, out_shape=out_shape,
        compiler_params=_params("parallel"),
    )(*args)


def _swiglu_fwd(h, wgt, wut, name):
    T, D = h.shape
    F = wgt.shape[0]
    tm, tn = _pick(T, 1024, 8), _pick(F, 1536, LANES)

    def body(h_ref, g_ref, u_ref, go_ref, uo_ref, a_ref):
        hh = h_ref[...]
        g = _dot(hh, g_ref[...], NT)
        u = _dot(hh, u_ref[...], NT)
        go_ref[...] = g.astype(BF16)
        uo_ref[...] = u.astype(BF16)
        a_ref[...] = (g * jax.nn.sigmoid(g) * u).astype(BF16)

    spec_w = pl.BlockSpec((tn, D), lambda j, i: (j, 0))
    spec_o = pl.BlockSpec((tm, tn), lambda j, i: (i, j))
    out = jax.ShapeDtypeStruct((T, F), BF16)
    return pl.pallas_call(
        body, name=name, grid=(F // tn, T // tm),
        in_specs=[pl.BlockSpec((tm, D), lambda j, i: (i, 0)), spec_w, spec_w],
        out_specs=[spec_o, spec_o, spec_o],
        out_shape=[out, out, out],
        compiler_params=_params("parallel", "parallel"),
    )(h, wgt, wut)


def _swiglu_bwd(dy, wd, gate, up, name):
    T, D = dy.shape
    F = wd.shape[0]
    tm, tn = _pick(T, 1024, 8), _pick(F, 1536, LANES)

    halves = [slice(0, tn // 2), slice(tn // 2, tn)] if tn % (2 * LANES) == 0 else [slice(0, tn)]

    def body(dy_ref, w_ref, g_ref, u_ref, dg_ref, du_ref):
        das = [_dot(dy_ref[...], w_ref[sl, :], NT) for sl in halves]
        for sl, da in zip(halves, das):
            g = g_ref[:, sl].astype(F32)
            sg = jax.nn.sigmoid(g)
            t = da * sg
            du_ref[:, sl] = (t * g).astype(BF16)
            dg_ref[:, sl] = (t * u_ref[:, sl].astype(F32) * (1.0 + g * (1.0 - sg))).astype(BF16)

    spec_o = pl.BlockSpec((tm, tn), lambda j, i: (i, j))
    return pl.pallas_call(
        body, name=name, grid=(F // tn, T // tm),
        in_specs=[pl.BlockSpec((tm, D), lambda j, i: (i, 0)), pl.BlockSpec((tn, D), lambda j, i: (j, 0)), spec_o, spec_o],
        out_specs=[spec_o, spec_o],
        out_shape=[jax.ShapeDtypeStruct((T, F), BF16), jax.ShapeDtypeStruct((T, F), BF16)],
        compiler_params=_params("parallel", "parallel"),
    )(dy, wd, gate, up)


def _norm_mod(x, gain, sc, sh, S, name):
    T, D = x.shape
    tm = _pick(S, 512, 8)
    nb = S // tm

    def body(x_ref, g_ref, sc_ref, sh_ref, o_ref):
        o_ref[...] = _norm_mod_rows(x_ref[...], g_ref[...], sc_ref[0], sh_ref[0])

    spec_b = pl.BlockSpec((1, 1, D), lambda i: (i // nb, 0, 0))
    return pl.pallas_call(
        body, name=name, grid=(T // tm,),
        in_specs=[pl.BlockSpec((tm, D), lambda i: (i, 0)), pl.BlockSpec((1, D), lambda i: (0, 0)), spec_b, spec_b],
        out_specs=pl.BlockSpec((tm, D), lambda i: (i, 0)),
        out_shape=jax.ShapeDtypeStruct((T, D), BF16),
        compiler_params=_params("parallel"),
    )(x, gain, sc, sh)


def _norm_mod_bwd(x, pairs, dres, gain, sc, S, name, below=None, exchange=None):
    T, D = x.shape
    B = T // S
    tm = _pick(S, 512, 8)
    nb = S // tm
    n_mm = 2 * len(pairs)

    def body(*refs):
        mm, (x_ref, dr_ref, g_ref, sc_ref), rest = refs[:n_mm], refs[n_mm:n_mm + 4], refs[n_mm + 4:]
        if below is None:
            o_ref, dsh_ref, dsc_ref, dg_ref = rest
            sums = [dsh_ref, dsc_ref, dg_ref]
        else:
            y_ref, gt_ref, o_ref, dsh_ref, dsc_ref, dg_ref, dy_ref, dgt_ref = rest
            sums = [dsh_ref, dsc_ref, dg_ref, dgt_ref]

        @pl.when(pl.program_id(1) == 0)
        def _():
            for ref in sums:
                ref[...] = jnp.zeros_like(ref)
        dhv = _dot(mm[0][...], mm[1][...], NN)
        for p in range(2, n_mm, 2):
            dhv = dhv + _dot(mm[p][...], mm[p + 1][...], NN)
        xv, g = x_ref[...], g_ref[...]
        r = lax.rsqrt(jnp.mean(xv * xv, axis=-1, keepdims=True) + EPS)
        xhat = xv * r
        dsh_ref[0] += jnp.sum(dhv, axis=0, keepdims=True)
        dsc_ref[0] += jnp.sum(dhv * (xhat * g), axis=0, keepdims=True)
        dn = dhv * (1.0 + sc_ref[0])
        dg_ref[0] += jnp.sum(dn * xhat, axis=0, keepdims=True)
        dxh = dn * g
        out = dr_ref[...] + r * (dxh - xhat * jnp.mean(dxh * xhat, axis=-1, keepdims=True))
        o_ref[...] = out
        if below is not None:
            dy_ref[...] = (out * gt_ref[0]).astype(BF16)
            dgt_ref[0] += jnp.sum(out * y_ref[...].astype(F32), axis=0, keepdims=True)

    spec_t = pl.BlockSpec((tm, D), lambda b, i: (b * nb + i, 0))
    spec_b = pl.BlockSpec((1, 1, D), lambda b, i: (b, 0, 0))
    red = jax.ShapeDtypeStruct((B, 1, D), F32)
    in_specs, args = [], []
    for a, w in pairs:
        K = a.shape[1]
        in_specs += [pl.BlockSpec((tm, K), lambda b, i: (b * nb + i, 0)),
                     pl.BlockSpec((K, D), lambda b, i: (0, 0), pipeline_mode=pl.Buffered(1))]
        args += [a, w]
    in_specs += [spec_t, spec_t, pl.BlockSpec((1, D), lambda b, i: (0, 0)), spec_b]
    args += [x, dres, gain, sc]
    out_specs = [spec_t, spec_b, spec_b, spec_b]
    out_shape = [jax.ShapeDtypeStruct((T, D), F32), red, red, red]
    if below is not None:
        in_specs += [spec_t, spec_b]
        out_specs += [spec_t, spec_b]
        out_shape += [jax.ShapeDtypeStruct((T, D), BF16), red]
        args += list(below)
    return _call_behind(body, name, (B, nb), in_specs, out_specs, out_shape, args, exchange, True)


def _gate_bwd(dx, y, gate, S, name):
    T, D = dx.shape
    B = T // S
    tm = _pick(S, 512, 8)
    nb = S // tm

    def body(dx_ref, y_ref, g_ref, dy_ref, dg_ref):
        @pl.when(pl.program_id(1) == 0)
        def _():
            dg_ref[...] = jnp.zeros_like(dg_ref)
        d = dx_ref[...]
        dy_ref[...] = (d * g_ref[0]).astype(BF16)
        dg_ref[0] += jnp.sum(d * y_ref[...].astype(F32), axis=0, keepdims=True)

    spec_t = pl.BlockSpec((tm, D), lambda b, i: (b * nb + i, 0))
    spec_b = pl.BlockSpec((1, 1, D), lambda b, i: (b, 0, 0))
    return pl.pallas_call(
        body, name=name, grid=(B, nb),
        in_specs=[spec_t, spec_t, spec_b],
        out_specs=[spec_t, spec_b],
        out_shape=[jax.ShapeDtypeStruct((T, D), BF16), jax.ShapeDtypeStruct((B, 1, D), F32)],
        compiler_params=_params("parallel", "arbitrary"),
    )(dx, y, gate)


def _loss_head(y, target, name):
    T, D = y.shape
    tm = _pick(T, 512, 8)

    def body(y_ref, t_ref, dy_ref, l_ref):
        @pl.when(pl.program_id(0) == 0)
        def _():
            l_ref[...] = jnp.zeros_like(l_ref)
        e = y_ref[...] - t_ref[...]
        dy_ref[...] = e * (1.0 / D)
        l_ref[...] += 0.5 * jnp.sum(jnp.mean(e * e, axis=-1, keepdims=True), axis=0, keepdims=True)

    spec = pl.BlockSpec((tm, D), lambda i: (i, 0))
    return pl.pallas_call(
        body, name=name, grid=(T // tm,),
        in_specs=[spec, spec],
        out_specs=[spec, pl.BlockSpec((8, LANES), lambda i: (0, 0))],
        out_shape=[jax.ShapeDtypeStruct((T, D), F32), jax.ShapeDtypeStruct((8, LANES), F32)],
        compiler_params=_params("arbitrary"),
    )(y, target)


def _ada_fwd(c_all, ada_w, bias, name):
    NB, D = c_all.shape
    L, _, W = ada_w.shape

    def body(c_ref, w_ref, b_ref, o_ref):
        cv = c_ref[...]
        cond = cv * jax.nn.sigmoid(cv)
        o_ref[0] = _dot(cond, w_ref[0], NN, HIGH) + b_ref[0]

    return pl.pallas_call(
        body, name=name, grid=(L,),
        in_specs=[pl.BlockSpec((NB, D), lambda l: (0, 0)), pl.BlockSpec((1, D, W), lambda l: (l, 0, 0)),
                  pl.BlockSpec((1, 1, W), lambda l: (l, 0, 0))],
        out_specs=pl.BlockSpec((1, NB, W), lambda l: (l, 0, 0)),
        out_shape=jax.ShapeDtypeStruct((L, NB, W), F32),
        compiler_params=_params("parallel"),
    )(c_all, ada_w, bias)


def _ada_bwd(c_all, dmod, name):
    NB, D = c_all.shape
    L, _, W = dmod.shape

    def body(c_ref, d_ref, o_ref):
        cv = c_ref[...]
        cond = cv * jax.nn.sigmoid(cv)
        o_ref[0] = _dot(cond, d_ref[0], TN, HIGH)

    return pl.pallas_call(
        body, name=name, grid=(L,),
        in_specs=[pl.BlockSpec((NB, D), lambda l: (0, 0)), pl.BlockSpec((1, NB, W), lambda l: (l, 0, 0))],
        out_specs=pl.BlockSpec((1, D, W), lambda l: (l, 0, 0)),
        out_shape=jax.ShapeDtypeStruct((L, D, W), F32),
        compiler_params=_params("parallel"),
    )(c_all, dmod)


def _lo_mask(shape):
    return lax.broadcasted_iota(jnp.int32, shape, len(shape) - 1) < HEAD


def _head_sum_matrix():
    r = lax.broadcasted_iota(jnp.int32, (LANES, LANES), 0) // HEAD
    c = lax.broadcasted_iota(jnp.int32, (LANES, LANES), 1) // HEAD
    return (r == c).astype(BF16)


def _head_sum(x, P):
    hi = x.astype(BF16)
    lo = (x - hi.astype(F32)).astype(BF16)
    return _dot(hi, P, NN) + _dot(lo, P, NN)


def _rope(y, cs, s1, s2):
    return y * cs + pltpu.roll(y, LANES - ROT // 2, 1) * s1 + pltpu.roll(y, ROT // 2, 1) * s2


def _rope_bwd(d, cs, s1, s2):
    return d * cs + pltpu.roll(d * s1, ROT // 2, 1) + pltpu.roll(d * s2, LANES - ROT // 2, 1)


def _qk_prep(qkv, cs, s1, s2, qg, kg, name):
    T, W = qkv.shape
    NQ = W - 2 * LANES
    tm = _pick(T, 512, 8)

    def body(x_ref, cs_ref, s1_ref, s2_ref, qg_ref, kg_ref, q_ref, k_ref, v_ref):
        P = _head_sum_matrix()
        cs_, s1_, s2_ = cs_ref[...], s1_ref[...], s2_ref[...]
        lo = _lo_mask((tm, LANES))

        def norm_rope(xv, g):
            ms = _head_sum(xv * xv, P) * (1.0 / HEAD)
            return _rope(xv * lax.rsqrt(ms + EPS) * g, cs_, s1_, s2_)

        for j in range(NQ // LANES):
            q_ref[:, j * LANES:(j + 1) * LANES] = norm_rope(x_ref[:, j * LANES:(j + 1) * LANES], qg_ref[...]).astype(BF16)
        kr = norm_rope(x_ref[:, NQ:NQ + LANES], kg_ref[...])
        ks = pltpu.roll(kr, HEAD, 1)
        k_ref[:, :LANES] = jnp.where(lo, kr, ks).astype(BF16)
        k_ref[:, LANES:] = jnp.where(lo, ks, kr).astype(BF16)
        vr = x_ref[:, NQ + LANES:]
        vs = pltpu.roll(vr, HEAD, 1)
        v_ref[:, :LANES] = jnp.where(lo, vr, vs).astype(BF16)
        v_ref[:, LANES:] = jnp.where(lo, vs, vr).astype(BF16)

    spec_t = pl.BlockSpec((tm, LANES), lambda i: (i, 0))
    spec_g = pl.BlockSpec((1, LANES), lambda i: (0, 0))
    return pl.pallas_call(
        body, name=name, grid=(T // tm,),
        in_specs=[pl.BlockSpec((tm, W), lambda i: (i, 0)), spec_t, spec_t, spec_t, spec_g, spec_g],
        out_specs=[pl.BlockSpec((tm, NQ), lambda i: (i, 0)), pl.BlockSpec((tm, 2 * LANES), lambda i: (i, 0)),
                   pl.BlockSpec((tm, 2 * LANES), lambda i: (i, 0))],
        out_shape=[jax.ShapeDtypeStruct((T, NQ), BF16), jax.ShapeDtypeStruct((T, 2 * LANES), BF16),
                   jax.ShapeDtypeStruct((T, 2 * LANES), BF16)],
        compiler_params=_params("parallel"),
    )(qkv, cs, s1, s2, qg, kg)


def _stack_heads(x2):
    lo = _lo_mask(x2.shape)
    z = jnp.zeros_like(x2)
    return jnp.concatenate([jnp.where(lo, x2, z), jnp.where(lo, z, x2)], axis=0)


def _unstack_heads(xs):
    r = xs.shape[0] // 2
    return jnp.where(_lo_mask((r, LANES)), xs[:r], xs[r:])


def _swa_valid(i):
    qo = lax.broadcasted_iota(jnp.int32, (2 * BLK, 2 * BLK), 0) % BLK
    kc_ = lax.broadcasted_iota(jnp.int32, (2 * BLK, 2 * BLK), 1)
    rel = qo + BLK - kc_
    return (rel >= 0) & (rel < BLK) & ((kc_ >= BLK) | (i > 0))


def _swa_scores(q2, kk, sink2, valid):
    qs = _stack_heads(q2) * SCALE
    s = _dot(qs, kk, NT)
    sk = jnp.concatenate([jnp.broadcast_to(sink2[:, 0:1], (BLK, 1)), jnp.broadcast_to(sink2[:, HEAD:HEAD + 1], (BLK, 1))], axis=0)
    return qs, jnp.where(valid, s, NEG), sk


def _swa_fwd(q, kd, vd, sink2, B, name, gather=None):
    T, NQ = q.shape
    NP = NQ // LANES
    nq = T // B // BLK
    NG = kd.shape[1] // LANES
    grp = NP // NG

    def body(q_ref, kp_ref, kc_ref, vp_ref, vc_ref, s_ref, o_ref, l_ref):
        valid = _swa_valid(pl.program_id(2))
        kk = jnp.concatenate([kp_ref[...], kc_ref[...]], axis=0)
        vs = _stack_heads(jnp.concatenate([vp_ref[...], vc_ref[...]], axis=0))
        sls = [slice(jj * LANES, (jj + 1) * LANES) for jj in range(grp)]
        sc = [_swa_scores(q_ref[:, sl], kk, s_ref[jj], valid) for jj, sl in enumerate(sls)]
        ms = [jnp.maximum(jnp.max(s, axis=1, keepdims=True), sk) for _, s, sk in sc]
        ps = [jnp.exp(s - m) for (_, s, _), m in zip(sc, ms)]
        ls = [jnp.sum(p, axis=1, keepdims=True) + jnp.exp(sk - m) for p, (_, _, sk), m in zip(ps, sc, ms)]
        ps = [(p * (1.0 / l)).astype(BF16) for p, l in zip(ps, ls)]
        os_ = [_dot(jnp.concatenate([p[:BLK], p[BLK:]], axis=1), vs, NN) for p in ps]
        for sl, o, m, l in zip(sls, os_, ms, ls):
            o_ref[:, sl] = o.astype(BF16)
            l_ref[:, sl] = _unstack_heads(jnp.broadcast_to(m + jnp.log(l), (2 * BLK, LANES)))

    spec_q = pl.BlockSpec((BLK, grp * LANES), lambda b, g, i: (b * nq + i, g))
    spec_p = pl.BlockSpec((BLK, LANES), lambda b, g, i: (b * nq + jnp.maximum(i - 1, 0), g))
    spec_c = pl.BlockSpec((BLK, LANES), lambda b, g, i: (b * nq + i, g))
    in_specs = [spec_q, spec_p, spec_c, spec_p, spec_c, pl.BlockSpec((grp, 1, LANES), lambda b, g, i: (g, 0, 0))]
    out_shape = [jax.ShapeDtypeStruct((T, NQ), BF16), jax.ShapeDtypeStruct((T, NQ), F32)]
    return _call_behind(body, name, (B, NG, nq), in_specs, [spec_q, spec_q], out_shape, [q, kd, kd, vd, vd, sink2], gather, False)


def _swa_bwd(q, kd, vd, sink2, do, lse, B, name, exchange=None):
    T, NQ = q.shape
    NP = NQ // LANES
    nq = T // B // BLK
    NG = kd.shape[1] // LANES
    grp = NP // NG

    def body(q_ref, kp_ref, kc_ref, vp_ref, vc_ref, s_ref, do_ref, l_ref,
             dq_ref, dkc_ref, dkp_ref, dvc_ref, dvp_ref, ds_ref):
        b, i = pl.program_id(1), pl.program_id(2)

        @pl.when((b == 0) & (i == 0))
        def _():
            ds_ref[...] = jnp.zeros_like(ds_ref)
        valid = _swa_valid(i)
        kk = jnp.concatenate([kp_ref[...], kc_ref[...]], axis=0)
        vv = jnp.concatenate([vp_ref[...], vc_ref[...]], axis=0)
        sls = [slice(jj * LANES, (jj + 1) * LANES) for jj in range(grp)]
        sc = [_swa_scores(q_ref[:, sl], kk, s_ref[jj], valid) for jj, sl in enumerate(sls)]
        dos = [_stack_heads(do_ref[:, sl]) for sl in sls]
        dps = [_dot(d, vv, NT) for d in dos]
        lses = [jnp.concatenate([l_ref[:, sl][:, 0:1], l_ref[:, sl][:, HEAD:HEAD + 1]], axis=0) for sl in sls]
        ps = [jnp.exp(s - lse) for (_, s, _), lse in zip(sc, lses)]
        deltas = [jnp.sum(p * dp, axis=1, keepdims=True) for p, dp in zip(ps, dps)]
        dscs = [(p * (dp - delta)).astype(BF16) for p, dp, delta in zip(ps, dps, deltas)]
        dqs = [_dot(dsc, kk, NN) for dsc in dscs]
        dk = jnp.zeros((2 * BLK, LANES), F32)
        dv = jnp.zeros((2 * BLK, LANES), F32)
        for jj, sl in enumerate(sls):
            dsk = -jnp.exp(sc[jj][2] - lses[jj]) * deltas[jj]
            dsk_lo = jnp.sum(dsk[:BLK], axis=0, keepdims=True)
            dsk_hi = jnp.sum(dsk[BLK:], axis=0, keepdims=True)
            ds_ref[jj] += jnp.where(_lo_mask((1, LANES)), dsk_lo, dsk_hi)
            dq_ref[:, sl] = _unstack_heads(dqs[jj]) * SCALE
            dk = dk + _dot(dscs[jj], sc[jj][0], TN)
            dv = dv + _dot(ps[jj].astype(BF16), dos[jj], TN)
        dkp_ref[...] = dk[:BLK]
        dkc_ref[...] = dk[BLK:]
        dvp_ref[...] = dv[:BLK]
        dvc_ref[...] = dv[BLK:]

    spec_q = pl.BlockSpec((BLK, grp * LANES), lambda g, b, i: (b * nq + i, g))
    spec_p = pl.BlockSpec((BLK, LANES), lambda g, b, i: (b * nq + jnp.maximum(i - 1, 0), g))
    spec_c = pl.BlockSpec((BLK, LANES), lambda g, b, i: (b * nq + i, g))
    spec_s = pl.BlockSpec((grp, 1, LANES), lambda g, b, i: (g, 0, 0))
    kv = jax.ShapeDtypeStruct((T, NG * LANES), F32)
    in_specs = [spec_q, spec_p, spec_c, spec_p, spec_c, spec_s, spec_q, spec_q]
    out_specs = [spec_q, spec_c, spec_c, spec_c, spec_c, spec_s]
    out_shape = [jax.ShapeDtypeStruct((T, NQ), F32), kv, kv, kv, kv, jax.ShapeDtypeStruct((NP, 1, LANES), F32)]
    return _call_behind(body, name, (NG, B, nq), in_specs, out_specs, out_shape, [q, kd, kd, vd, vd, sink2, do, lse],
                        exchange, True)


def _qk_prep_bwd(qkv, cs, s1, s2, qg, kg, dq, dkc, dkp, dvc, dvp, B, name):
    T, W = qkv.shape
    NQ = W - 2 * LANES
    NP = NQ // LANES
    nq = T // B // BLK

    def body(x_ref, cs_ref, s1_ref, s2_ref, qg_ref, kg_ref, dq_ref, dkc_ref, dkp_ref, dvc_ref, dvp_ref,
             o_ref, dqg_ref, dkg_ref):
        b, i = pl.program_id(0), pl.program_id(1)

        @pl.when((b == 0) & (i == 0))
        def _():
            dqg_ref[...] = jnp.zeros_like(dqg_ref)
            dkg_ref[...] = jnp.zeros_like(dkg_ref)
        P = _head_sum_matrix()
        cs_, s1_, s2_ = cs_ref[...], s1_ref[...], s2_ref[...]
        lo = _lo_mask((BLK, LANES))
        has_next = (i + 1 < nq).astype(F32)

        def norm_rope_bwd(xv, g, d):
            du = _rope_bwd(d, cs_, s1_, s2_)
            r = lax.rsqrt(_head_sum(xv * xv, P) * (1.0 / HEAD) + EPS)
            xhat = xv * r
            dgain = jnp.sum(du * xhat, axis=0, keepdims=True)
            uu = du * g
            dx = r * (uu - xhat * (_head_sum(uu * xhat, P) * (1.0 / HEAD)))
            return dx, dgain + pltpu.roll(dgain, HEAD, 1)

        dqg = jnp.zeros((1, LANES), F32)
        for j in range(NP):
            sl = slice(j * LANES, (j + 1) * LANES)
            dx, dg = norm_rope_bwd(x_ref[:, sl], qg_ref[...], dq_ref[:, sl])
            o_ref[:, sl] = dx.astype(BF16)
            dqg = dqg + dg
        dqg_ref[...] += dqg

        def fold(c_ref, p_ref, g):
            sl = slice(g * LANES, (g + 1) * LANES)
            t = c_ref[:, sl] + has_next * p_ref[:, sl]
            return t + pltpu.roll(t, HEAD, 1)

        dk = jnp.where(lo, fold(dkc_ref, dkp_ref, 0), fold(dkc_ref, dkp_ref, 1))
        dx, dg = norm_rope_bwd(x_ref[:, NQ:NQ + LANES], kg_ref[...], dk)
        o_ref[:, NQ:NQ + LANES] = dx.astype(BF16)
        dkg_ref[...] += dg
        dv = jnp.where(lo, fold(dvc_ref, dvp_ref, 0), fold(dvc_ref, dvp_ref, 1))
        o_ref[:, NQ + LANES:] = dv.astype(BF16)

    spec_t = pl.BlockSpec((BLK, LANES), lambda b, i: (b * nq + i, 0))
    spec_g = pl.BlockSpec((1, LANES), lambda b, i: (0, 0))
    spec_c = pl.BlockSpec((BLK, 2 * LANES), lambda b, i: (b * nq + i, 0))
    spec_n = pl.BlockSpec((BLK, 2 * LANES), lambda b, i: (b * nq + jnp.minimum(i + 1, nq - 1), 0))
    row = jax.ShapeDtypeStruct((1, LANES), F32)
    return pl.pallas_call(
        body, name=name, grid=(B, nq),
        in_specs=[pl.BlockSpec((BLK, W), lambda b, i: (b * nq + i, 0)), spec_t, spec_t, spec_t, spec_g, spec_g,
                  pl.BlockSpec((BLK, NQ), lambda b, i: (b * nq + i, 0)), spec_c, spec_n, spec_c, spec_n],
        out_specs=[pl.BlockSpec((BLK, W), lambda b, i: (b * nq + i, 0)), spec_g, spec_g],
        out_shape=[jax.ShapeDtypeStruct((T, W), BF16), row, row],
        compiler_params=_params("arbitrary", "arbitrary"),
    )(qkv, cs, s1, s2, qg, kg, dq, dkc, dkp, dvc, dvp)


SB_TILE = 256
SB_UNROLL = 4
SB_UNROLL_BWD = 2


def _split_heads(x2, scale=None):
    lo = _lo_mask(x2.shape)
    z = jnp.zeros_like(x2)
    if scale is not None:
        x2 = x2 * scale
    return jnp.where(lo, x2, z), jnp.where(lo, z, x2)


def _sb_terms(qh, kj, diagonal):
    z = _dot(qh, kj, NT)
    e = jnp.exp2(jnp.abs(z) * (-LOG2E))
    lb = jnp.minimum(z, 0.0) - jnp.log(1.0 + e)
    L = lb - z
    if not diagonal:
        return lb, L, None, z, e
    strict = lax.broadcasted_iota(jnp.int32, z.shape, 1) < lax.broadcasted_iota(jnp.int32, z.shape, 0)
    return lb, jnp.where(strict, L, 0.0), strict, z, e


def _tri(n, cmp):
    r = lax.broadcasted_iota(jnp.int32, (n, n), 0)
    c = lax.broadcasted_iota(jnp.int32, (n, n), 1)
    return cmp(r, c).astype(BF16)


def _by_value(r, fns, carry):
    if len(fns) == 1:
        return fns[0](carry)
    half = len(fns) // 2
    return lax.cond(r < half, lambda cr: _by_value(r, fns[:half], cr), lambda cr: _by_value(r - half, fns[half:], cr), carry)


def _sb_fwd(qkv, B, name, gather=None):
    T, W = qkv.shape
    NQ = W // 3
    NP = NQ // LANES
    S = T // B
    tq = min(SB_TILE, S)
    nq = S // tq
    grid = (B, NP, nq)

    def body(q_ref, k_ref, v_ref, o_ref, t_ref):
        i = pl.program_id(2)
        qh = _split_heads(q_ref[...], SCALE)
        U = _tri(tq, lambda r, c: r > c)

        def sweep(tiles, cs, acc):
            chains = [(t, h) for t in range(len(tiles)) for h in range(2)]
            rows = [pl.ds(pl.multiple_of(j * tq, tq), tq) for j, _ in tiles]
            ks = [k_ref[r, :] for r in rows]
            vs = [_split_heads(v_ref[r, :]) for r in rows]
            terms = {(t, h): _sb_terms(qh[h], ks[t], tiles[t][1]) for t, h in chains}
            carry = {}
            for h in range(2):
                c = cs[h]
                for t in range(len(tiles)):
                    carry[t, h] = c
                    c = c + jnp.sum(terms[t, h][1], axis=1, keepdims=True)
                cs = cs[:h] + (c,) + cs[h + 1:]
            cum = {ch: _dot(terms[ch][1].astype(BF16), U, NN) for ch in chains}
            for ch in chains:
                a = jnp.exp(terms[ch][0] + (cum[ch] + carry[ch]))
                if tiles[ch[0]][1]:
                    a = jnp.where(terms[ch][2], a, 0.0)
                acc = acc + _dot(a.astype(BF16), vs[ch[0]][ch[1]], NN)
            return cs, acc

        zero = jnp.zeros((tq, 1), F32)
        rem = i % SB_UNROLL
        heads = [lambda cr, k=k: sweep([(i, True)] + [(i - 1 - t, False) for t in range(k)], *cr) for k in range(SB_UNROLL)]
        carry = _by_value(rem, heads, ((zero, zero), jnp.zeros((tq, LANES), F32)))
        step = lambda n, cr: sweep([(i - 1 - rem - SB_UNROLL * n - t, False) for t in range(SB_UNROLL)], *cr)
        cs, acc = lax.fori_loop(0, i // SB_UNROLL, step, carry)
        o_ref[...] = acc.astype(BF16)
        t_ref[...] = jnp.where(_lo_mask((tq, LANES)), cs[0], cs[1])

    spec_q = pl.BlockSpec((tq, LANES), lambda b, p, i: (b * nq + i, p))
    in_specs = [spec_q, pl.BlockSpec((S, LANES), lambda b, p, i: (b, NP + p)),
                pl.BlockSpec((S, LANES), lambda b, p, i: (b, 2 * NP + p))]
    out_shape = [jax.ShapeDtypeStruct((T, NQ), BF16), jax.ShapeDtypeStruct((T, NQ), F32)]
    return _call_behind(body, name, grid, in_specs, [spec_q, spec_q], out_shape, [qkv, qkv, qkv], gather, False)


def _sb_bwd(qkv, do, tot, B, name, exchange=None):
    T, W = qkv.shape
    NQ = W // 3
    NP = NQ // LANES
    S = T // B
    tq = min(SB_TILE, S)
    nq = S // tq
    grid = (B, NP, nq)

    def body(q_ref, k_ref, v_ref, do_ref, t_ref, dq_ref, dk_ref, dv_ref, dkm_ref, dvm_ref):
        i = pl.program_id(2)

        @pl.when(i == 0)
        def _():
            dk_ref[...] = jnp.zeros_like(dk_ref)
            dv_ref[...] = jnp.zeros_like(dv_ref)
        qh = _split_heads(q_ref[...], SCALE)
        doh = _split_heads(do_ref[...])
        top = lax.broadcasted_iota(jnp.int32, (LANES, tq), 0) < HEAD
        zt = jnp.zeros((LANES, tq), BF16)
        qt = (q_ref[...].astype(F32) * SCALE).T.astype(BF16)
        dot_ = do_ref[...].astype(F32).T.astype(BF16)
        qth = (jnp.where(top, qt, zt), jnp.where(top, zt, qt))
        doth = (jnp.where(top, dot_, zt), jnp.where(top, zt, dot_))
        tt = t_ref[...]
        tot = (tt[:, 0:1], tt[:, HEAD:HEAD + 1])
        Urev = _tri(tq, lambda r, c: r > c)
        Uexc = _tri(tq, lambda r, c: r < c)

        def sweep(tiles, carry):
            nt = len(tiles)
            chains = [(t, h) for t in range(nt) for h in range(2)]
            rows = [pl.ds(pl.multiple_of(j * tq, tq), tq) for j, _ in tiles]
            ks = [k_ref[r, :] for r in rows]
            vs = [v_ref[r, :] for r in rows]
            terms = {(t, h): _sb_terms(qh[h], ks[t], tiles[t][1]) for t, h in chains}
            cc = [carry[h][0] for h in range(2)]
            later = {}
            for t, h in chains:
                cc[h] = cc[h] + jnp.sum(terms[t, h][1], axis=1, keepdims=True)
                later[t, h] = tot[h] - cc[h]
            cum = {ch: _dot(terms[ch][1].astype(BF16), Urev, NN) for ch in chains}
            da = {(t, h): _dot(doh[h], vs[t], NT) for t, h in chains}
            a, g, before = {}, {}, {}
            cg = [carry[h][1] for h in range(2)]
            for ch in chains:
                a[ch] = jnp.exp(terms[ch][0] + (cum[ch] + later[ch]))
                if tiles[ch[0]][1]:
                    a[ch] = jnp.where(terms[ch][2], a[ch], 0.0)
                g[ch] = a[ch] * da[ch]
                before[ch] = cg[ch[1]]
                cg[ch[1]] = cg[ch[1]] + jnp.sum(g[ch], axis=1, keepdims=True)
            G = {ch: _dot(g[ch].astype(BF16), Uexc, NN) for ch in chains}
            dz = {}
            for ch in chains:
                d = g[ch] - jnp.exp(terms[ch][0]) * (g[ch] + (G[ch] + before[ch]))
                if tiles[ch[0]][1]:
                    d = jnp.where(terms[ch][2], d, 0.0)
                dz[ch] = d.astype(BF16)
            dq = [carry[h][2] for h in range(2)]
            for t, h in chains:
                dq[h] = dq[h] + _dot(dz[t, h], ks[t], NN)
            for t in range(nt):
                dk_ref[:, rows[t]] += _dot(qth[0], dz[t, 0], NN) + _dot(qth[1], dz[t, 1], NN)
                dv_ref[:, rows[t]] += _dot(doth[0], a[t, 0].astype(BF16), NN) + _dot(doth[1], a[t, 1].astype(BF16), NN)
            return tuple((cc[h], cg[h], dq[h]) for h in range(2))

        zero = jnp.zeros((tq, 1), F32)
        zq = jnp.zeros((tq, LANES), F32)
        step = lambda n, cr: sweep([(SB_UNROLL_BWD * n + t, False) for t in range(SB_UNROLL_BWD)], cr)
        carry = lax.fori_loop(0, i // SB_UNROLL_BWD, step, ((zero, zero, zq), (zero, zero, zq)))
        tails = [lambda cr, k=k: sweep([(i - k + t, False) for t in range(k)] + [(i, True)], cr) for k in range(SB_UNROLL_BWD)]
        carry = _by_value(i % SB_UNROLL_BWD, tails, carry)
        dq_ref[...] = (jnp.where(_lo_mask((tq, LANES)), carry[0][2], carry[1][2]) * SCALE).astype(BF16)

        @pl.when(i == nq - 1)
        def _():
            for c in range(nq):
                cols = slice(c * tq, (c + 1) * tq)
                dkm_ref[cols, :] = dk_ref[:, cols].T.astype(BF16)
                dvm_ref[cols, :] = dv_ref[:, cols].T.astype(BF16)

    spec_q = pl.BlockSpec((tq, LANES), lambda b, p, i: (b * nq + i, p))
    spec_s = pl.BlockSpec((LANES, S), lambda b, p, i: (b * NP + p, 0))
    spec_m = pl.BlockSpec((S, LANES), lambda b, p, i: (b, p))
    key_side = jax.ShapeDtypeStruct((B * NQ, S), F32)
    token_major = jax.ShapeDtypeStruct((T, NQ), BF16)
    in_specs = [spec_q, pl.BlockSpec((S, LANES), lambda b, p, i: (b, NP + p)),
                pl.BlockSpec((S, LANES), lambda b, p, i: (b, 2 * NP + p)), spec_q, spec_q]
    out_specs = [spec_q, spec_s, spec_s, spec_m, spec_m]
    out_shape = [token_major, key_side, key_side, token_major, token_major]
    args = [qkv, qkv, qkv, do, tot]
    return _call_behind(body, name, grid, in_specs, out_specs, out_shape, args, exchange, True)


def _adamw(w, g, m, v, name):
    shape = w.shape
    cols = shape[-1]
    rows = math.prod(shape[:-1])
    tr = _pick(rows, max(8, (1 << 19) // max(cols, LANES) // 8 * 8), 8)

    def body(w_ref, g_ref, m_ref, v_ref, d_ref, mo_ref, vo_ref):
        gv = g_ref[...]
        mn = ADAM_B1 * m_ref[...] + (1.0 - ADAM_B1) * gv
        vn = ADAM_B2 * v_ref[...] + (1.0 - ADAM_B2) * (gv * gv)
        m_hat = mn / (1.0 - ADAM_B1 ** ADAM_STEP)
        v_hat = vn / (1.0 - ADAM_B2 ** ADAM_STEP)
        d_ref[...] = -ADAM_LR * (m_hat / (jnp.sqrt(v_hat) + ADAM_EPS) + ADAM_WD * w_ref[...])
        mo_ref[...] = mn
        vo_ref[...] = vn

    spec = pl.BlockSpec((tr, cols), lambda i: (i, 0))
    out = jax.ShapeDtypeStruct((rows, cols), F32)
    d, mn, vn = pl.pallas_call(
        body, name=name, grid=(rows // tr,),
        in_specs=[spec] * 4, out_specs=[spec] * 3, out_shape=[out] * 3,
        compiler_params=_params("parallel"),
    )(w.reshape(rows, cols), g.reshape(rows, cols), m.reshape(rows, cols), v.reshape(rows, cols))
    return d.reshape(shape), mn.reshape(shape), vn.reshape(shape)


def _pad_rows(a, rows):
    return jnp.pad(a, ((0, rows - a.shape[0]), (0, 0)))


def kernel(x, c, positions, ada_w, ada_b, norm1_g, norm2_g, wqkv_a, q_norm_a, k_norm_a, sinks_a, wo_a, wqkv_b, wo_b, w_gate, w_up, w_down, loss_target, m_ada_w, m_ada_b, m_norm1_g, m_norm2_g, m_wqkv_a, m_q_norm_a, m_k_norm_a, m_sinks_a, m_wo_a, m_wqkv_b, m_wo_b, m_w_gate, m_w_up, m_w_down, v_ada_w, v_ada_b, v_norm1_g, v_norm2_g, v_wqkv_a, v_q_norm_a, v_k_norm_a, v_sinks_a, v_wo_a, v_wqkv_b, v_wo_b, v_w_gate, v_w_up, v_w_down):
    B, S, D = x.shape
    T = B * S
    L = ada_w.shape[0]
    NA, NB_ = wqkv_a.shape[0], wqkv_b.shape[0]
    me = 4 * lax.axis_index("x") + 2 * lax.axis_index("y") + lax.axis_index("c")
    xt = x.reshape(T, D)

    col_sharded = {"qkv_a": wqkv_a, "qkv_b": wqkv_b, "gate": w_gate, "up": w_up}
    row_sharded = {"wo_a": wo_a, "wo_b": wo_b, "down": w_down}

    def shard_rows(key):
        kind, idx = key
        return col_sharded[kind][idx].T if kind in col_sharded else row_sharded[kind][idx]

    def layer_keys(l):
        mix = "a" if l % 2 == 0 else "b"
        return [("qkv_" + mix, l // 2), ("wo_" + mix, l // 2), ("gate", l), ("up", l), ("down", l)]

    def unpack(buf, keys, reshape):
        out, off = {}, 0
        for key in keys:
            rows = shard_rows(key).shape[0]
            out[key] = reshape(buf[..., off:off + rows, :], rows)
            off += rows
        return out

    first_b = 1
    keys_early = layer_keys(0)[:2]
    keys_mid = layer_keys(0)[2:] + [("qkv_b", 0)]
    keys_late = [k for l in range(1, L) for k in layer_keys(l) if k != ("qkv_b", 0)]
    pack = lambda keys: jnp.concatenate([shard_rows(k).astype(BF16) for k in keys], axis=0)
    full_rows = lambda b, rows: b.reshape(NDEV * rows, D)
    W = unpack(_all_gather(pack(keys_early), "ag_weights"), keys_early, full_rows)

    WA = ada_w.shape[2]
    c_all = _all_gather(c, "ag_c").reshape(NDEV * B, D)
    bias = lax.dynamic_slice_in_dim(ada_b, me * WA, WA, axis=1).reshape(L, 1, WA)
    mod_part = _ada_fwd(c_all, ada_w, bias, "ada_fwd")
    mod_all = _all_gather(mod_part.reshape(L * NDEV * B, WA), "ag_mod")
    mod_all = mod_all.reshape(NDEV, L, NDEV * B, WA).transpose(1, 2, 0, 3).reshape(L, NDEV * B, NDEV * WA)
    mod = lax.dynamic_slice_in_dim(mod_all, me * B, B, axis=1)
    mod = mod.reshape(L, B, 6, 1, D)
    sh1, sc1, g1, sh2, sc2, g2 = [mod[:, :, k] for k in range(6)]

    half = ROT // 2
    inv_freq = jnp.power(jnp.float32(ROPE_THETA), -jnp.arange(half, dtype=F32) * 2.0 / ROT)
    ang = positions.reshape(T, 1).astype(F32) * inv_freq[None, :]
    cos, sin = jnp.cos(ang), jnp.sin(ang)
    ones = jnp.ones((T, HEAD - ROT), F32)
    zeros = jnp.zeros((T, HEAD - ROT), F32)
    z8 = jnp.zeros((T, half), F32)
    cs = jnp.tile(jnp.concatenate([cos, cos, ones], axis=1), (1, 2))
    s1 = jnp.tile(jnp.concatenate([-sin, z8, zeros], axis=1), (1, 2))
    s2 = jnp.tile(jnp.concatenate([z8, sin, zeros], axis=1), (1, 2))

    saved = []
    xc = xt
    h1 = _norm_mod(xc, norm1_g[0:1], sc1[0], sh1[0], S, "norm1_0")
    for l in range(L):
        j = l // 2
        sv = dict(x_in=xc, h1=h1)
        if l % 2 == 0:
            qkv = _mm_nt(h1, W["qkv_a", j], F32, f"qkv_a_{l}")
            qg = jnp.tile(q_norm_a[j:j + 1], (1, 2))
            kg = jnp.tile(k_norm_a[j:j + 1], (1, 2))
            qn, kd, vd = _qk_prep(qkv, cs, s1, s2, qg, kg, f"qk_prep_{l}")
            sink2 = jnp.repeat(sinks_a[j].reshape(-1, 2), HEAD, axis=1).reshape(-1, 1, LANES)
            if l == 0:
                attn, lse, mid = _swa_fwd(qn, kd, vd, sink2, B, f"swa_fwd_{l}", gather=pack(keys_mid))
                W.update(unpack(mid, keys_mid, full_rows))
            else:
                attn, lse = _swa_fwd(qn, kd, vd, sink2, B, f"swa_fwd_{l}")
            sv.update(qkv=qkv, qg=qg, kg=kg, qn=qn, kd=kd, vd=vd, sink2=sink2, lse=lse)
            wo = W["wo_a", j]
        else:
            qkv = _mm_nt(h1, W["qkv_b", j], BF16, f"qkv_b_{l}")
            if l == first_b:
                attn, tot, late = _sb_fwd(qkv, B, f"sb_fwd_{l}", gather=pack(keys_late))
                W.update(unpack(late, keys_late, full_rows))
            else:
                attn, tot = _sb_fwd(qkv, B, f"sb_fwd_{l}")
            sv.update(qkv=qkv, tot=tot)
            wo = W["wo_b", j]
        y1, xm, h2 = _mm_res(attn, wo, xc, g1[l], S, f"attn_out_{l}", norm=(norm2_g[l:l + 1], sc2[l], sh2[l]))
        gate, up, act = _swiglu_fwd(h2, W["gate", l], W["up", l], f"swiglu_fwd_{l}")
        if l + 1 < L:
            y2, xc, h1 = _mm_res(act, W["down", l], xm, g2[l], S, f"mlp_out_{l}",
                                 norm=(norm1_g[l + 1:l + 2], sc1[l + 1], sh1[l + 1]))
        else:
            y2, xc = _mm_res(act, W["down", l], xm, g2[l], S, f"mlp_out_{l}")
        sv.update(attn=attn, y1=y1, x_mid=xm, h2=h2, gate=gate, up=up, act=act, y2=y2)
        saved.append(sv)

    dx, loss_tile = _loss_head(xc, loss_target.reshape(T, D), "loss_head")

    G = {}
    pack_grads = lambda keys: jnp.concatenate([G[k].reshape(NDEV, G[k].shape[0] // NDEV, D) for k in keys], axis=1)
    keys_hi = [k for l in range(first_b + 1, L) for k in layer_keys(l)] + layer_keys(first_b)[1:]
    keys_mlp0_g = [("qkv_b", 0), ("down", 0)]
    keys_mid_g = [("wo_a", 0), ("gate", 0), ("up", 0)]
    keys_lo = layer_keys(0)[:1]
    received_hi = received_mid = received_mlp0 = None
    dmod = [None] * L
    dn1, dn2 = [None] * L, [None] * L
    dqg, dkg, dsink = [None] * NA, [None] * NA, [None] * NA
    dy2, dg2 = _gate_bwd(dx, saved[L - 1]["y2"], g2[L - 1], S, "gate2_bwd_top")
    for l in reversed(range(L)):
        j = l // 2
        mix = "a" if l % 2 == 0 else "b"
        sv = saved[l]
        dgate, dup = _swiglu_bwd(dy2, W["down", l], sv["gate"], sv["up"], f"swiglu_bwd_{l}")
        G["down", l] = _mm_tn(sv["act"], dy2, f"dw_down_{l}")
        G["gate", l] = _mm_tn(dgate, sv["h2"], f"dw_gate_{l}")
        G["up", l] = _mm_tn(dup, sv["h2"], f"dw_up_{l}")
        n2 = _norm_mod_bwd(sv["x_mid"], [(dgate, W["gate", l]), (dup, W["up", l])], dx, norm2_g[l:l + 1], sc2[l], S,
                           f"norm2_bwd_{l}", below=(sv["y1"], g1[l]), exchange=pack_grads(keys_mlp0_g) if l == 0 else None)
        dxm, dsh2, dsc2, dn2[l], dy1, dg1 = n2[:6]
        if l == 0:
            received_mlp0 = n2[6]
        dattn = _mm_nt(dy1, W["wo_" + mix, j], BF16, f"dattn_{l}")
        G["wo_" + mix, j] = _mm_tn(sv["attn"], dy1, f"dw_o_{l}")
        if l % 2 == 0:
            swa_args = (sv["qn"], sv["kd"], sv["vd"], sv["sink2"], dattn, sv["lse"], B, f"swa_bwd_{l}")
            if l == 0:
                dq, dkc, dkp, dvc, dvp, dsink[j], received_mid = _swa_bwd(*swa_args, exchange=pack_grads(keys_mid_g))
            else:
                dq, dkc, dkp, dvc, dvp, dsink[j] = _swa_bwd(*swa_args)
            dqkv, dqg[j], dkg[j] = _qk_prep_bwd(sv["qkv"], cs, s1, s2, sv["qg"], sv["kg"], dq, dkc, dkp, dvc, dvp, B,
                                                f"qk_prep_bwd_{l}")
        else:
            nqb = sv["qkv"].shape[1] // 3
            sb_args = (sv["qkv"], dattn, sv["tot"], B, f"sb_bwd_{l}")
            if l == first_b and keys_hi:
                dq, _, _, dk, dv, received_hi = _sb_bwd(*sb_args, exchange=pack_grads(keys_hi))
            else:
                dq, _, _, dk, dv = _sb_bwd(*sb_args)
        wt = W["qkv_" + mix, j]
        if l % 2 == 0:
            dh1_pairs = [(dqkv, wt)]
            G["qkv_a", j] = _mm_tn(dqkv, sv["h1"], f"dw_qkv_{l}")
        else:
            parts = [dq, dk, dv]
            dh1_pairs = [(part, wt[k * nqb:(k + 1) * nqb]) for k, part in enumerate(parts)]
            G["qkv_b", j] = jnp.concatenate([_mm_tn(part, sv["h1"], f"dw_qkv_{l}_{k}") for k, part in enumerate(parts)], axis=0)
        n1_args = (sv["x_in"], dh1_pairs, dxm, norm1_g[l:l + 1], sc1[l], S, f"norm1_bwd_{l}")
        dmod_l = [None, None, dg1, dsh2, dsc2, dg2]
        if l > 0:
            dx, dmod_l[0], dmod_l[1], dn1[l], dy2, dg2 = _norm_mod_bwd(*n1_args, below=(saved[l - 1]["y2"], g2[l - 1]))
        else:
            dx, dmod_l[0], dmod_l[1], dn1[l] = _norm_mod_bwd(*n1_args)
        dmod[l] = jnp.concatenate(dmod_l, axis=1)
    grad_x = dx.reshape(B, S, D)

    ndm = L * 6
    dmod_rows = jnp.stack(dmod, axis=1).reshape(B * ndm, D)
    misc = jnp.concatenate(
        [jnp.concatenate(dn1, axis=0).reshape(B * L, D), jnp.concatenate(dn2, axis=0).reshape(B * L, D),
         _pad_rows(jnp.concatenate([jnp.pad(r, ((0, 0), (0, D - LANES))) for r in dqg + dkg]
                                   + [jnp.pad(r[:, 0, ::HEAD].reshape(1, -1), ((0, 0), (0, D - 2 * r.shape[0]))) for r in dsink]
                                   + [jnp.pad(loss_tile[0:1, 0:1], ((0, 0), (0, D - 1)))], axis=0), 8)], axis=0)
    nmisc = misc.shape[0]
    small = _all_gather(jnp.concatenate([dmod_rows, _pad_rows(misc, -(-nmisc // 8) * 8)], axis=0), "ag_small")
    dmod_all = small[:, :B * ndm].reshape(NDEV * B, ndm, D)
    g_ada_b = _sum_leading(dmod_all, "sum_dmod").reshape(L, 6 * D)
    misc_sum = _sum_leading(small[:, B * ndm:], "sum_misc")
    g_n1 = misc_sum[0:B * L].reshape(L, B, D)
    g_n2 = misc_sum[B * L:2 * B * L].reshape(L, B, D)
    g_norm1 = _sum_leading(g_n1.transpose(1, 0, 2), "sum_n1")
    g_norm2 = _sum_leading(g_n2.transpose(1, 0, 2), "sum_n2")
    o = 2 * B * L
    g_qn = misc_sum[o:o + NA, :HEAD]
    g_kn = misc_sum[o + NA:o + 2 * NA, :HEAD]
    nsink = sinks_a.shape[1]
    g_sink = misc_sum[o + 2 * NA:o + 3 * NA, :nsink]
    loss = misc_sum[o + 3 * NA, 0]

    dmod_loc = lax.dynamic_slice_in_dim(dmod_all.reshape(NDEV * B, L, 6 * D), me * WA, WA, axis=2)
    g_ada_w = _ada_bwd(c_all, dmod_loc.transpose(1, 0, 2), "ada_bwd")

    shard = unpack(_sum_leading(_exchange(pack_grads(keys_lo), "grad_exchange"), "grad_sum"), keys_lo, lambda b, rows: b)
    shard.update(unpack(_sum_leading(received_mid, "grad_sum_mid"), keys_mid_g, lambda b, rows: b))
    shard.update(unpack(_sum_leading(received_mlp0, "grad_sum_mlp0"), keys_mlp0_g, lambda b, rows: b))
    if received_hi is not None:
        shard.update(unpack(_sum_leading(received_hi, "grad_sum_hi"), keys_hi, lambda b, rows: b))

    def stacked(kind, n):
        return jnp.stack([shard[kind, i].T if kind in col_sharded else shard[kind, i] for i in range(n)])

    gw_qkv_a, gw_qkv_b, gw_gate, gw_up = stacked("qkv_a", NA), stacked("qkv_b", NB_), stacked("gate", L), stacked("up", L)
    gw_wo_a, gw_wo_b, gw_down = stacked("wo_a", NA), stacked("wo_b", NB_), stacked("down", L)

    grads = [g_ada_w, g_ada_b, g_norm1, g_norm2, gw_qkv_a, g_qn, g_kn, g_sink, gw_wo_a, gw_qkv_b, gw_wo_b,
             gw_gate, gw_up, gw_down]
    ws = [ada_w, ada_b, norm1_g, norm2_g, wqkv_a, q_norm_a, k_norm_a, sinks_a, wo_a, wqkv_b, wo_b, w_gate, w_up, w_down]
    ms = [m_ada_w, m_ada_b, m_norm1_g, m_norm2_g, m_wqkv_a, m_q_norm_a, m_k_norm_a, m_sinks_a, m_wo_a, m_wqkv_b,
          m_wo_b, m_w_gate, m_w_up, m_w_down]
    vs = [v_ada_w, v_ada_b, v_norm1_g, v_norm2_g, v_wqkv_a, v_q_norm_a, v_k_norm_a, v_sinks_a, v_wo_a, v_wqkv_b,
          v_wo_b, v_w_gate, v_w_up, v_w_down]
    deltas, new_m, new_v = [], [], []
    for k, (w, g, m, v) in enumerate(zip(ws, grads, ms, vs)):
        g = g.reshape(w.shape)
        d, mn, vn = _adamw(w, g, m, v, f"adamw_{k}")
        grads[k] = g
        deltas.append(d)
        new_m.append(mn)
        new_v.append(vn)
    return (loss, grad_x, *grads, *deltas, *new_m, *new_v)
```
